```python
import math
import jax, jax.numpy as jnp
from jax import lax
import numpy as np

D_MODEL = 1024
BATCH = 8
SEQ = 4096
DEPTH = 4

N_A_LAYERS = DEPTH // 2
N_B_LAYERS = DEPTH - N_A_LAYERS
CHUNK = 128
SGU_WIDTH = 2 * D_MODEL
SGU_GROUPS = 8
HEAD_DIM = 64
KV_HEADS = D_MODEL // 128
DILATED_GROUPS = ((128, 1), (512, 4), (2048, 16))
N_GROUPS = len(DILATED_GROUPS)
Q_HEADS = N_GROUPS * KV_HEADS
BLK = 128
REL_BUCKETS = 32
REL_MAX_DIST = 2048
D_FF = 2816
CONV_WIDTH = 3
ALPHA = (2 * DEPTH) ** 0.25
BETA = (8 * DEPTH) ** -0.25
LN_EPS = 1e-5
NEG = -1e30

kernel_name = 'hybrid_sgu_dilated_yoco'


def layer_norm(x, g, b):
    xf = x.astype(jnp.float32)
    mu = jnp.mean(xf, axis=-1, keepdims=True)
    var = jnp.mean(jnp.square(xf - mu), axis=-1, keepdims=True)
    y = (xf - mu) * lax.rsqrt(var + LN_EPS) * g.astype(jnp.float32) + b.astype(jnp.float32)
    return y.astype(x.dtype)


def conv_ffn(x, w_up, conv_w, conv_b, w_down):
    T = x.shape[1]
    h = x @ w_up
    hp = jnp.pad(h, ((0, 0), (CONV_WIDTH - 1, 0), (0, 0)))
    h = sum(hp[:, k:k + T] * conv_w[k] for k in range(CONV_WIDTH)) + conv_b
    a, g = jnp.split(h, 2, axis=-1)
    return (jax.nn.gelu(a, approximate=False) * g) @ w_down


def mixer_a(x, w_in, ln_g, ln_b, w_s, b_s, w_out):
    B, T, _ = x.shape
    z = jax.nn.gelu(x @ w_in, approximate=False)
    u, v = jnp.split(z, 2, axis=-1)
    v = layer_norm(v, ln_g, ln_b)
    v = v.reshape(B, T // CHUNK, CHUNK, SGU_GROUPS, SGU_WIDTH // SGU_GROUPS)
    ws = w_s * jnp.tril(jnp.ones((CHUNK, CHUNK), w_s.dtype))
    sv = jnp.einsum('gts,bnsgc->bntgc', ws, v) + b_s.T[None, None, :, :, None]
    y = u * sv.reshape(B, T, SGU_WIDTH)
    return y @ w_out


def t5_bucket(dist):
    n = dist.astype(jnp.int32)
    max_exact = REL_BUCKETS // 2
    nf = jnp.maximum(n, 1).astype(jnp.float32)
    large = max_exact + (jnp.log(nf / max_exact) / math.log(REL_MAX_DIST / max_exact)
                         * (REL_BUCKETS - max_exact)).astype(jnp.int32)
    large = jnp.minimum(large, REL_BUCKETS - 1)
    return jnp.where(n < max_exact, n, large)


def band_delta():
    iq = jnp.arange(BLK)[:, None]
    ik = jnp.arange(2 * BLK)[None, :]
    return iq + BLK - ik


def band_bias(table_g, dil):
    dist = jnp.clip(band_delta(), 0, None) * dil
    return jnp.transpose(table_g[t5_bucket(dist)], (2, 0, 1)).astype(jnp.float32)


def dilated_band_attention(q, k, v, dil, n_back, bias):
    B, T, H, Dh = q.shape
    span = dil * BLK
    Tp = -(-T // span) * span
    S = Tp // dil
    NB = S // BLK

    def to_blocks(a):
        a = jnp.pad(a, ((0, 0), (0, Tp - T), (0, 0), (0, 0)))
        a = a.reshape(B, S, dil, H, Dh).transpose(0, 2, 3, 1, 4)
        return a.reshape(B, dil, H, NB, BLK, Dh)

    def band(a):
        prev = jnp.pad(a[:, :, :, :-1], ((0, 0), (0, 0), (0, 0), (1, 0), (0, 0), (0, 0)))
        return jnp.concatenate([prev, a], axis=4)

    qb, kb, vb = to_blocks(q), band(to_blocks(k)), band(to_blocks(v))
    s = jnp.einsum('brhnqd,brhnkd->brhnqk', qb, kb).astype(jnp.float32) * (Dh ** -0.5)
    s = s + bias[:, None]
    delta = band_delta()
    valid = (delta >= 0) & (delta <= n_back)
    first = (jnp.arange(NB)[:, None, None] == 0) & (jnp.arange(2 * BLK)[None, None, :] < BLK)
    valid = valid[None] & ~first
    s = jnp.where(valid, s, NEG)
    m = jnp.max(s, axis=-1, keepdims=True)
    p = jnp.exp(s - m)
    den = jnp.sum(p, axis=-1, keepdims=True)
    o = jnp.einsum('brhnqk,brhnkd->brhnqd', (p / den).astype(v.dtype), vb)
    lse = (m + jnp.log(den))[..., 0]
    o = o.reshape(B, dil, H, S, Dh).transpose(0, 3, 1, 2, 4).reshape(B, Tp, H, Dh)[:, :T]
    lse = lse.reshape(B, dil, H, S).transpose(0, 3, 1, 2).reshape(B, Tp, H)[:, :T]
    return o, lse


def mixer_b(x, k, v, w_q, w_o, rel_table):
    B, T, _ = x.shape
    q = (x @ w_q).reshape(B, T, N_GROUPS, KV_HEADS, HEAD_DIM)
    outs, lses = [], []
    for g, (win, dil) in enumerate(DILATED_GROUPS):
        bias = band_bias(rel_table[:, g * KV_HEADS:(g + 1) * KV_HEADS], dil)
        o, lse = dilated_band_attention(q[:, :, g], k, v, dil, win // dil, bias)
        outs.append(o)
        lses.append(lse)
    wts = jax.nn.softmax(jnp.stack(lses, axis=0), axis=0)
    o = jnp.einsum('gbth,gbthd->bthd', wts, jnp.stack(outs, axis=0).astype(jnp.float32))
    return o.reshape(B, T, KV_HEADS * HEAD_DIM).astype(x.dtype) @ w_o


def _fwd_setup_inputs(seed: int = 0) -> dict:
    key = jax.random.key(seed)
    ks = jax.random.split(key, 20)
    f32 = jnp.float32
    nrm = lambda k, shape, s: jax.random.normal(k, shape, f32) * s
    E2 = 2 * SGU_WIDTH
    return {
        'x': nrm(ks[0], (BATCH, SEQ, D_MODEL), 1.0),
        'a_w_in': nrm(ks[1], (N_A_LAYERS, D_MODEL, E2), D_MODEL ** -0.5),
        'a_ln_g': 1.0 + nrm(ks[2], (N_A_LAYERS, SGU_WIDTH), 0.05),
        'a_ln_b': nrm(ks[3], (N_A_LAYERS, SGU_WIDTH), 0.02),
        'a_w_s': nrm(ks[4], (N_A_LAYERS, SGU_GROUPS, CHUNK, CHUNK), CHUNK ** -0.5),
        'a_b_s': 1.0 + nrm(ks[5], (N_A_LAYERS, SGU_GROUPS, CHUNK), 0.1),
        'a_w_out': nrm(ks[6], (N_A_LAYERS, SGU_WIDTH, D_MODEL), BETA * SGU_WIDTH ** -0.5),
        'kv_w': nrm(ks[7], (D_MODEL, 2 * KV_HEADS * HEAD_DIM), D_MODEL ** -0.5),
        'b_w_q': nrm(ks[8], (N_B_LAYERS, D_MODEL, Q_HEADS * HEAD_DIM), D_MODEL ** -0.5),
        'b_w_o': nrm(ks[9], (N_B_LAYERS, KV_HEADS * HEAD_DIM, D_MODEL), BETA * (KV_HEADS * HEAD_DIM) ** -0.5),
        'rel_table': nrm(ks[10], (REL_BUCKETS, Q_HEADS), 0.5),
        'ffn_w_up': nrm(ks[11], (DEPTH, D_MODEL, 2 * D_FF), D_MODEL ** -0.5),
        'ffn_conv_w': nrm(ks[12], (DEPTH, CONV_WIDTH, 2 * D_FF), CONV_WIDTH ** -0.5),
        'ffn_conv_b': nrm(ks[13], (DEPTH, 2 * D_FF), 0.02),
        'ffn_w_down': nrm(ks[14], (DEPTH, D_FF, D_MODEL), BETA * D_FF ** -0.5),
        'ln_g': 1.0 + nrm(ks[15], (DEPTH, 2, D_MODEL), 0.05),
        'ln_b': nrm(ks[16], (DEPTH, 2, D_MODEL), 0.02),
    }


def _fwd_reference(x, a_w_in, a_ln_g, a_ln_b, a_w_s, a_b_s, a_w_out, kv_w, b_w_q, b_w_o,
              rel_table, ffn_w_up, ffn_conv_w, ffn_conv_b, ffn_w_down, ln_g, ln_b):
    B, T, _ = x.shape
    k = v = None
    for i in range(DEPTH):
        if i < N_A_LAYERS:
            h = mixer_a(x, a_w_in[i], a_ln_g[i], a_ln_b[i], a_w_s[i], a_b_s[i], a_w_out[i])
        else:
            if i == N_A_LAYERS:
                kv = (x @ kv_w).reshape(B, T, 2, KV_HEADS, HEAD_DIM)
                k, v = kv[:, :, 0], kv[:, :, 1]
            j = i - N_A_LAYERS
            h = mixer_b(x, k, v, b_w_q[j], b_w_o[j], rel_table)
        x = layer_norm(ALPHA * x + h, ln_g[i, 0], ln_b[i, 0])
        f = conv_ffn(x, ffn_w_up[i], ffn_conv_w[i], ffn_conv_b[i], ffn_w_down[i])
        x = layer_norm(ALPHA * x + f, ln_g[i, 1], ln_b[i, 1])
    return x


import jax as _jax
import jax.numpy as _jnp

TWIN_FORMAT = 'train_step'
FWD_PARAMS = ['x', 'a_w_in', 'a_ln_g', 'a_ln_b', 'a_w_s', 'a_b_s', 'a_w_out', 'kv_w', 'b_w_q', 'b_w_o', 'rel_table', 'ffn_w_up', 'ffn_conv_w', 'ffn_conv_b', 'ffn_w_down', 'ln_g', 'ln_b']
TWIN_WEIGHTS = ['a_w_in', 'a_ln_g', 'a_ln_b', 'a_w_s', 'a_b_s', 'a_w_out', 'kv_w', 'b_w_q', 'b_w_o', 'rel_table', 'ffn_w_up', 'ffn_conv_w', 'ffn_conv_b', 'ffn_w_down', 'ln_g', 'ln_b']
TWIN_DIFF_INPUT = 'x'
TWIN_INPUTS = ['x', 'a_w_in', 'a_ln_g', 'a_ln_b', 'a_w_s', 'a_b_s', 'a_w_out', 'kv_w', 'b_w_q', 'b_w_o', 'rel_table', 'ffn_w_up', 'ffn_conv_w', 'ffn_conv_b', 'ffn_w_down', 'ln_g', 'ln_b', 'loss_target', 'm_a_w_in', 'm_a_ln_g', 'm_a_ln_b', 'm_a_w_s', 'm_a_b_s', 'm_a_w_out', 'm_kv_w', 'm_b_w_q', 'm_b_w_o', 'm_rel_table', 'm_ffn_w_up', 'm_ffn_conv_w', 'm_ffn_conv_b', 'm_ffn_w_down', 'm_ln_g', 'm_ln_b', 'v_a_w_in', 'v_a_ln_g', 'v_a_ln_b', 'v_a_w_s', 'v_a_b_s', 'v_a_w_out', 'v_kv_w', 'v_b_w_q', 'v_b_w_o', 'v_rel_table', 'v_ffn_w_up', 'v_ffn_conv_w', 'v_ffn_conv_b', 'v_ffn_w_down', 'v_ln_g', 'v_ln_b']
TWIN_OUTPUTS = ['loss', 'grad_x', 'grad_a_w_in', 'grad_a_ln_g', 'grad_a_ln_b', 'grad_a_w_s', 'grad_a_b_s', 'grad_a_w_out', 'grad_kv_w', 'grad_b_w_q', 'grad_b_w_o', 'grad_rel_table', 'grad_ffn_w_up', 'grad_ffn_conv_w', 'grad_ffn_conv_b', 'grad_ffn_w_down', 'grad_ln_g', 'grad_ln_b', 'delta_a_w_in', 'delta_a_ln_g', 'delta_a_ln_b', 'delta_a_w_s', 'delta_a_b_s', 'delta_a_w_out', 'delta_kv_w', 'delta_b_w_q', 'delta_b_w_o', 'delta_rel_table', 'delta_ffn_w_up', 'delta_ffn_conv_w', 'delta_ffn_conv_b', 'delta_ffn_w_down', 'delta_ln_g', 'delta_ln_b', 'new_m_a_w_in', 'new_m_a_ln_g', 'new_m_a_ln_b', 'new_m_a_w_s', 'new_m_a_b_s', 'new_m_a_w_out', 'new_m_kv_w', 'new_m_b_w_q', 'new_m_b_w_o', 'new_m_rel_table', 'new_m_ffn_w_up', 'new_m_ffn_conv_w', 'new_m_ffn_conv_b', 'new_m_ffn_w_down', 'new_m_ln_g', 'new_m_ln_b', 'new_v_a_w_in', 'new_v_a_ln_g', 'new_v_a_ln_b', 'new_v_a_w_s', 'new_v_a_b_s', 'new_v_a_w_out', 'new_v_kv_w', 'new_v_b_w_q', 'new_v_b_w_o', 'new_v_rel_table', 'new_v_ffn_w_up', 'new_v_ffn_conv_w', 'new_v_ffn_conv_b', 'new_v_ffn_w_down', 'new_v_ln_g', 'new_v_ln_b']
TWIN_LEAF_KINDS = {'loss': 'loss', 'grad_x': 'grad_x', 'grad_a_w_in': 'grad_w', 'grad_a_ln_g': 'grad_w', 'grad_a_ln_b': 'grad_w', 'grad_a_w_s': 'grad_w', 'grad_a_b_s': 'grad_w', 'grad_a_w_out': 'grad_w', 'grad_kv_w': 'grad_w', 'grad_b_w_q': 'grad_w', 'grad_b_w_o': 'grad_w', 'grad_rel_table': 'grad_w', 'grad_ffn_w_up': 'grad_w', 'grad_ffn_conv_w': 'grad_w', 'grad_ffn_conv_b': 'grad_w', 'grad_ffn_w_down': 'grad_w', 'grad_ln_g': 'grad_w', 'grad_ln_b': 'grad_w', 'delta_a_w_in': 'delta_w', 'delta_a_ln_g': 'delta_w', 'delta_a_ln_b': 'delta_w', 'delta_a_w_s': 'delta_w', 'delta_a_b_s': 'delta_w', 'delta_a_w_out': 'delta_w', 'delta_kv_w': 'delta_w', 'delta_b_w_q': 'delta_w', 'delta_b_w_o': 'delta_w', 'delta_rel_table': 'delta_w', 'delta_ffn_w_up': 'delta_w', 'delta_ffn_conv_w': 'delta_w', 'delta_ffn_conv_b': 'delta_w', 'delta_ffn_w_down': 'delta_w', 'delta_ln_g': 'delta_w', 'delta_ln_b': 'delta_w', 'new_m_a_w_in': 'new_m', 'new_m_a_ln_g': 'new_m', 'new_m_a_ln_b': 'new_m', 'new_m_a_w_s': 'new_m', 'new_m_a_b_s': 'new_m', 'new_m_a_w_out': 'new_m', 'new_m_kv_w': 'new_m', 'new_m_b_w_q': 'new_m', 'new_m_b_w_o': 'new_m', 'new_m_rel_table': 'new_m', 'new_m_ffn_w_up': 'new_m', 'new_m_ffn_conv_w': 'new_m', 'new_m_ffn_conv_b': 'new_m', 'new_m_ffn_w_down': 'new_m', 'new_m_ln_g': 'new_m', 'new_m_ln_b': 'new_m', 'new_v_a_w_in': 'new_v', 'new_v_a_ln_g': 'new_v', 'new_v_a_ln_b': 'new_v', 'new_v_a_w_s': 'new_v', 'new_v_a_b_s': 'new_v', 'new_v_a_w_out': 'new_v', 'new_v_kv_w': 'new_v', 'new_v_b_w_q': 'new_v', 'new_v_b_w_o': 'new_v', 'new_v_rel_table': 'new_v', 'new_v_ffn_w_up': 'new_v', 'new_v_ffn_conv_w': 'new_v', 'new_v_ffn_conv_b': 'new_v', 'new_v_ffn_w_down': 'new_v', 'new_v_ln_g': 'new_v', 'new_v_ln_b': 'new_v'}


def _forward(args):
    return _fwd_reference(*[args[k] for k in FWD_PARAMS])


def _output_shape():
    def fwd():
        inp = _fwd_setup_inputs(0)
        return _fwd_reference(*[inp[k] for k in FWD_PARAMS])
    out = _jax.eval_shape(fwd)
    return out.shape, out.dtype

N_MICROBATCH = 1
ADAM_LR = 0.001
ADAM_B1 = 0.9
ADAM_B2 = 0.999
ADAM_EPS = 1e-08
ADAM_WD = 0.01
ADAM_STEP = 10
PER_EXAMPLE_BATCH_AXIS = {'x': 0, 'loss_target': 0}
SHARED_INPUTS = []
_WEIGHT_DTYPES = {'a_w_in': _jnp.float32, 'a_ln_g': _jnp.float32, 'a_ln_b': _jnp.float32, 'a_w_s': _jnp.float32, 'a_b_s': _jnp.float32, 'a_w_out': _jnp.float32, 'kv_w': _jnp.float32, 'b_w_q': _jnp.float32, 'b_w_o': _jnp.float32, 'rel_table': _jnp.float32, 'ffn_w_up': _jnp.float32, 'ffn_conv_w': _jnp.float32, 'ffn_conv_b': _jnp.float32, 'ffn_w_down': _jnp.float32, 'ln_g': _jnp.float32, 'ln_b': _jnp.float32}
MOMENT_SCALE = {'a_w_in': 2.310214e-02, 'a_ln_g': 1.506478e-02, 'a_ln_b': 1.512254e-02, 'a_w_s': 2.132844e-02, 'a_b_s': 3.113842e-02, 'a_w_out': 1.286369e-01, 'kv_w': 2.058686e-02, 'b_w_q': 6.310292e-03, 'b_w_o': 3.096311e-02, 'rel_table': 1.232443e-02, 'ffn_w_up': 1.843971e-02, 'ffn_conv_w': 1.837467e-02, 'ffn_conv_b': 2.591960e-02, 'ffn_w_down': 7.222341e-02, 'ln_g': 1.205318e+01, 'ln_b': 1.427386e+00}


def _to_microbatches(a, axis):
    t = _jnp.moveaxis(a, axis, 0)
    t = t.reshape((N_MICROBATCH, t.shape[0] // N_MICROBATCH) + t.shape[1:])
    return _jnp.moveaxis(t, 1, axis + 1)


def setup_inputs(seed: int = 0) -> dict:
    inp = _fwd_setup_inputs(seed)
    key = _jax.random.fold_in(_jax.random.key(seed), 7919)
    shape, _ = _output_shape()
    out = dict(inp)
    out["loss_target"] = _jax.random.normal(_jax.random.fold_in(key, 0), shape, _jnp.float32)
    for i, name in enumerate(TWIN_WEIGHTS):
        w = inp[name].astype(_jnp.float32)
        if MOMENT_SCALE is None:
            s = _jnp.sqrt(_jnp.mean(_jnp.square(w)) + 1e-30)
        else:
            s = MOMENT_SCALE[name]
        km, kv = _jax.random.split(_jax.random.fold_in(key, i + 1))
        out[name] = w
        out["m_" + name] = s * _jax.random.normal(km, w.shape, _jnp.float32)
        out["v_" + name] = (s * s) * _jax.random.uniform(kv, w.shape, _jnp.float32, 0.5, 1.5)
    if N_MICROBATCH > 1:
        for name, axis in PER_EXAMPLE_BATCH_AXIS.items():
            out[name] = _to_microbatches(out[name], axis)
    return {'x': out['x'], 'a_w_in': out['a_w_in'], 'a_ln_g': out['a_ln_g'], 'a_ln_b': out['a_ln_b'], 'a_w_s': out['a_w_s'], 'a_b_s': out['a_b_s'], 'a_w_out': out['a_w_out'], 'kv_w': out['kv_w'], 'b_w_q': out['b_w_q'], 'b_w_o': out['b_w_o'], 'rel_table': out['rel_table'], 'ffn_w_up': out['ffn_w_up'], 'ffn_conv_w': out['ffn_conv_w'], 'ffn_conv_b': out['ffn_conv_b'], 'ffn_w_down': out['ffn_w_down'], 'ln_g': out['ln_g'], 'ln_b': out['ln_b'], 'loss_target': out['loss_target'], 'm_a_w_in': out['m_a_w_in'], 'm_a_ln_g': out['m_a_ln_g'], 'm_a_ln_b': out['m_a_ln_b'], 'm_a_w_s': out['m_a_w_s'], 'm_a_b_s': out['m_a_b_s'], 'm_a_w_out': out['m_a_w_out'], 'm_kv_w': out['m_kv_w'], 'm_b_w_q': out['m_b_w_q'], 'm_b_w_o': out['m_b_w_o'], 'm_rel_table': out['m_rel_table'], 'm_ffn_w_up': out['m_ffn_w_up'], 'm_ffn_conv_w': out['m_ffn_conv_w'], 'm_ffn_conv_b': out['m_ffn_conv_b'], 'm_ffn_w_down': out['m_ffn_w_down'], 'm_ln_g': out['m_ln_g'], 'm_ln_b': out['m_ln_b'], 'v_a_w_in': out['v_a_w_in'], 'v_a_ln_g': out['v_a_ln_g'], 'v_a_ln_b': out['v_a_ln_b'], 'v_a_w_s': out['v_a_w_s'], 'v_a_b_s': out['v_a_b_s'], 'v_a_w_out': out['v_a_w_out'], 'v_kv_w': out['v_kv_w'], 'v_b_w_q': out['v_b_w_q'], 'v_b_w_o': out['v_b_w_o'], 'v_rel_table': out['v_rel_table'], 'v_ffn_w_up': out['v_ffn_w_up'], 'v_ffn_conv_w': out['v_ffn_conv_w'], 'v_ffn_conv_b': out['v_ffn_conv_b'], 'v_ffn_w_down': out['v_ffn_w_down'], 'v_ln_g': out['v_ln_g'], 'v_ln_b': out['v_ln_b']}


def _loss(weights, diff, rest, loss_target):
    with _jax.named_scope("forward"):
        args = {**rest, TWIN_DIFF_INPUT: diff, **{k: w.astype(_WEIGHT_DTYPES[k]) for k, w in weights.items()}}
        y = _forward(args)
    with _jax.named_scope("loss_head"):
        err = _jnp.square(y.astype(_jnp.float32) - loss_target)
        return 0.5 * _jnp.sum(_jnp.mean(err, axis=-1)) if err.ndim else 0.5 * err


def _adamw(w, g, m, v):
    m = ADAM_B1 * m + (1.0 - ADAM_B1) * g
    v = ADAM_B2 * v + (1.0 - ADAM_B2) * _jnp.square(g)
    m_hat = m / (1.0 - ADAM_B1 ** ADAM_STEP)
    v_hat = v / (1.0 - ADAM_B2 ** ADAM_STEP)
    delta = -ADAM_LR * (m_hat / (_jnp.sqrt(v_hat) + ADAM_EPS) + ADAM_WD * w)
    return delta, m, v


def reference(x, a_w_in, a_ln_g, a_ln_b, a_w_s, a_b_s, a_w_out, kv_w, b_w_q, b_w_o, rel_table, ffn_w_up, ffn_conv_w, ffn_conv_b, ffn_w_down, ln_g, ln_b, loss_target, m_a_w_in, m_a_ln_g, m_a_ln_b, m_a_w_s, m_a_b_s, m_a_w_out, m_kv_w, m_b_w_q, m_b_w_o, m_rel_table, m_ffn_w_up, m_ffn_conv_w, m_ffn_conv_b, m_ffn_w_down, m_ln_g, m_ln_b, v_a_w_in, v_a_ln_g, v_a_ln_b, v_a_w_s, v_a_b_s, v_a_w_out, v_kv_w, v_b_w_q, v_b_w_o, v_rel_table, v_ffn_w_up, v_ffn_conv_w, v_ffn_conv_b, v_ffn_w_down, v_ln_g, v_ln_b):
    given = dict(x=x, a_w_in=a_w_in, a_ln_g=a_ln_g, a_ln_b=a_ln_b, a_w_s=a_w_s, a_b_s=a_b_s, a_w_out=a_w_out, kv_w=kv_w, b_w_q=b_w_q, b_w_o=b_w_o, rel_table=rel_table, ffn_w_up=ffn_w_up, ffn_conv_w=ffn_conv_w, ffn_conv_b=ffn_conv_b, ffn_w_down=ffn_w_down, ln_g=ln_g, ln_b=ln_b, loss_target=loss_target, m_a_w_in=m_a_w_in, m_a_ln_g=m_a_ln_g, m_a_ln_b=m_a_ln_b, m_a_w_s=m_a_w_s, m_a_b_s=m_a_b_s, m_a_w_out=m_a_w_out, m_kv_w=m_kv_w, m_b_w_q=m_b_w_q, m_b_w_o=m_b_w_o, m_rel_table=m_rel_table, m_ffn_w_up=m_ffn_w_up, m_ffn_conv_w=m_ffn_conv_w, m_ffn_conv_b=m_ffn_conv_b, m_ffn_w_down=m_ffn_w_down, m_ln_g=m_ln_g, m_ln_b=m_ln_b, v_a_w_in=v_a_w_in, v_a_ln_g=v_a_ln_g, v_a_ln_b=v_a_ln_b, v_a_w_s=v_a_w_s, v_a_b_s=v_a_b_s, v_a_w_out=v_a_w_out, v_kv_w=v_kv_w, v_b_w_q=v_b_w_q, v_b_w_o=v_b_w_o, v_rel_table=v_rel_table, v_ffn_w_up=v_ffn_w_up, v_ffn_conv_w=v_ffn_conv_w, v_ffn_conv_b=v_ffn_conv_b, v_ffn_w_down=v_ffn_w_down, v_ln_g=v_ln_g, v_ln_b=v_ln_b)
    weights = {n: given[n] for n in TWIN_WEIGHTS}
    shared = {n: given[n] for n in SHARED_INPUTS}
    per_example = {n: given[n] for n in ['x']}
    grad_fn = _jax.value_and_grad(_loss, argnums=(0, 1))

    def one_microbatch(ex, loss_target):
        ex = dict(ex)
        diff = ex.pop(TWIN_DIFF_INPUT)
        return grad_fn(weights, diff, {**shared, **ex}, loss_target)

    if N_MICROBATCH == 1:
        loss, (grad_w, grad_x) = one_microbatch(per_example, given["loss_target"])
    else:
        def body(carry, xs):
            loss_sum, grad_sum = carry
            l_k, (gw_k, gx_k) = one_microbatch(xs[0], xs[1])
            with _jax.named_scope("update"):
                return (loss_sum + l_k, _jax.tree.map(_jnp.add, grad_sum, gw_k)), gx_k

        init = (_jnp.zeros((), _jnp.float32), _jax.tree.map(_jnp.zeros_like, weights))
        (loss, grad_w), grad_x = _jax.lax.scan(body, init, (per_example, given["loss_target"]))
    with _jax.named_scope("update"):
        delta_w, new_m, new_v = {}, {}, {}
        for n in TWIN_WEIGHTS:
            delta_w[n], new_m[n], new_v[n] = _adamw(weights[n], grad_w[n], given["m_" + n], given["v_" + n])
    return (loss, grad_x, *[grad_w[n] for n in TWIN_WEIGHTS], *[delta_w[n] for n in TWIN_WEIGHTS],
            *[new_m[n] for n in TWIN_WEIGHTS], *[new_v[n] for n in TWIN_WEIGHTS])
```

```python
import functools
import math

import numpy as np
import jax
import jax.numpy as jnp
from jax import lax
from jax.experimental import pallas as pl
from jax.experimental.pallas import tpu as pltpu

F32 = jnp.float32
BF16 = jnp.bfloat16
MESH = pl.DeviceIdType.MESH

CHUNK = 128
HEAD_DIM = 64
DILATED_GROUPS = ((128, 1), (512, 4), (2048, 16))
N_GROUPS = len(DILATED_GROUPS)
BLK = 128
REL_BUCKETS = 32
REL_MAX_DIST = 2048
LN_EPS = 1e-5
NEG = -1e30
ADAM_LR = 0.001
ADAM_B1 = 0.9
ADAM_B2 = 0.999
ADAM_EPS = 1e-08
ADAM_WD = 0.01
ADAM_STEP = 10

N_CHIPS = 4
N_DEV = 8
LANES = 128
SUBLANES = 8
VMEM_LIMIT = 48 * 1024 * 1024

_SQRT_HALF = 0.7071067811865476
_INV_SQRT_2PI = 0.3989422804014327


def _cp(sem=None, vmem=VMEM_LIMIT):
    return pltpu.CompilerParams(dimension_semantics=sem, vmem_limit_bytes=vmem)


def _tile(dim, target, align=LANES):
    if dim <= target:
        return dim
    t = (target // align) * align
    while t >= align:
        if dim % t == 0:
            return t
        t -= align
    return dim


def _gelu(x):
    return 0.5 * x * (1.0 + lax.erf(x * _SQRT_HALF))


def _gelu_grad(x):
    return 0.5 * (1.0 + lax.erf(x * _SQRT_HALF)) + x * (_INV_SQRT_2PI * jnp.exp(-0.5 * x * x))


def _dot(a, b, ca, cb):
    return lax.dot_general(a, b, (((ca,), (cb,)), ((), ())), preferred_element_type=F32)


def _mm(a, b, *, name, ta=False, tb=False, out_dtype=F32, tiles=(1024, 1024, 512),
        a_lead=(), b_lead=(), a_halves=False, b_halves=False, add=None, add_scale=1.0, out_into=None):
    a2, b2 = a.shape[-2:], b.shape[-2:]
    M, K = (a2[1], a2[0]) if ta else a2
    Kb, N = (b2[1], b2[0]) if tb else b2
    if a_halves:
        K = 2 * K
    if b_halves:
        N = 2 * N
    assert K == Kb, (name, a.shape, b.shape)
    tm = _tile(M, tiles[0])
    tn = _tile(N // 2 if b_halves else N, tiles[1])
    tk = _tile(K // 2 if a_halves else K, tiles[2])
    nm, nn, nk = M // tm, N // tn, K // tk
    nkh, nnh = nk // 2, nn // 2

    def lead(idx, rest):
        return tuple(idx) + tuple(rest)

    sq_a = (None,) * (a.ndim - 2)
    sq_b = (None,) * (b.ndim - 2)
    if a_halves:
        assert not ta and a.ndim == 3
        a_spec = pl.BlockSpec((None, tm, tk), lambda i, j, k: (k // nkh, i, k % nkh))
    elif ta:
        a_spec = pl.BlockSpec(sq_a + (tk, tm), lambda i, j, k: lead(a_lead, (k, i)))
    else:
        a_spec = pl.BlockSpec(sq_a + (tm, tk), lambda i, j, k: lead(a_lead, (i, k)))
    if b_halves:
        assert not tb and b.ndim == 3
        b_spec = pl.BlockSpec((None, tk, tn), lambda i, j, k: (j // nnh, k, j % nnh))
    elif tb:
        b_spec = pl.BlockSpec(sq_b + (tn, tk), lambda i, j, k: lead(b_lead, (j, k)))
    else:
        b_spec = pl.BlockSpec(sq_b + (tk, tn), lambda i, j, k: lead(b_lead, (k, j)))
    mn_spec = pl.BlockSpec((tm, tn), lambda i, j, k: (i, j))
    in_specs = [a_spec, b_spec]
    args = [a, b]
    if add is not None:
        in_specs.append(mn_spec)
        args.append(add)
    aliases = {}
    if out_into is None:
        o_spec, out_shape = mn_spec, jax.ShapeDtypeStruct((M, N), out_dtype)
    else:
        buf, n_layers, layer = out_into
        o_spec = pl.BlockSpec((None, tm, tn), lambda i, j, k: (layer, i, j))
        out_shape = jax.ShapeDtypeStruct((n_layers, M, N), out_dtype)
        if buf is not None:
            aliases = {len(args): 0}
            in_specs.append(pl.BlockSpec(memory_space=pl.ANY))
            args.append(buf)
    n_in = len(args)
    ca, cb = (0 if ta else 1), (1 if tb else 0)

    def body(*refs):
        a_ref, b_ref = refs[:2]
        add_ref = refs[2] if add is not None else None
        o_ref = refs[n_in]

        def finish(acc):
            if add_ref is not None:
                acc = acc + add_scale * add_ref[...]
            o_ref[...] = acc.astype(o_ref.dtype)

        if nk == 1:
            finish(_dot(a_ref[...], b_ref[...], ca, cb))
        else:
            acc_ref = refs[-1]
            k = pl.program_id(2)

            @pl.when(k == 0)
            def _():
                acc_ref[...] = jnp.zeros_like(acc_ref)

            acc_ref[...] += _dot(a_ref[...], b_ref[...], ca, cb)

            @pl.when(k == nk - 1)
            def _():
                finish(acc_ref[...])

    return pl.pallas_call(
        body, name=name, grid=(nm, nn, nk),
        in_specs=in_specs, out_specs=o_spec, out_shape=out_shape,
        input_output_aliases=aliases,
        scratch_shapes=[] if nk == 1 else [pltpu.VMEM((tm, tn), F32)],
        compiler_params=_cp(("parallel", "parallel", "arbitrary")),
    )(*args)


def _ln_stats(r):
    mu = jnp.mean(r, axis=-1, keepdims=True)
    xc = r - mu
    var = jnp.mean(xc * xc, axis=-1, keepdims=True)
    rstd = lax.rsqrt(var + LN_EPS)
    return xc * rstd, rstd


def _add_ln(x, h, g, b, alpha, *, name):
    T, D = x.shape
    tr = _tile(T, 512, SUBLANES)
    row = pl.BlockSpec((tr, D), lambda i: (i, 0))
    vec = pl.BlockSpec((1, D), lambda i: (0, 0))

    def body(x_ref, h_ref, g_ref, b_ref, o_ref, ob_ref):
        xhat, _ = _ln_stats(alpha * x_ref[...] + h_ref[...])
        y = xhat * g_ref[...] + b_ref[...]
        o_ref[...] = y
        ob_ref[...] = y.astype(BF16)

    return pl.pallas_call(
        body, name=name, grid=(T // tr,),
        in_specs=[row, row, vec, vec], out_specs=[row, row],
        out_shape=[jax.ShapeDtypeStruct((T, D), F32), jax.ShapeDtypeStruct((T, D), BF16)],
        compiler_params=_cp(("parallel",)),
    )(x, h, g.reshape(1, D), b.reshape(1, D))


def _ln_bwd(dy, x, h, g, alpha, *, name):
    T, D = x.shape
    tr = _tile(T, 512, SUBLANES)
    row = pl.BlockSpec((tr, D), lambda i: (i, 0))
    vec = pl.BlockSpec((1, D), lambda i: (0, 0))

    def body(dy_ref, x_ref, h_ref, g_ref, dr_ref, drb_ref, dg_ref, db_ref):
        @pl.when(pl.program_id(0) == 0)
        def _():
            dg_ref[...] = jnp.zeros_like(dg_ref)
            db_ref[...] = jnp.zeros_like(db_ref)

        xhat, rstd = _ln_stats(alpha * x_ref[...] + h_ref[...])
        dyv = dy_ref[...]
        dxhat = dyv * g_ref[...]
        c1 = jnp.mean(dxhat, axis=-1, keepdims=True)
        c2 = jnp.mean(dxhat * xhat, axis=-1, keepdims=True)
        dr = rstd * (dxhat - c1 - xhat * c2)
        dr_ref[...] = dr
        drb_ref[...] = dr.astype(BF16)
        dg_ref[...] += jnp.sum(dyv * xhat, axis=0, keepdims=True)
        db_ref[...] += jnp.sum(dyv, axis=0, keepdims=True)

    return pl.pallas_call(
        body, name=name, grid=(T // tr,),
        in_specs=[row, row, row, vec], out_specs=[row, row, vec, vec],
        out_shape=[jax.ShapeDtypeStruct((T, D), F32), jax.ShapeDtypeStruct((T, D), BF16),
                   jax.ShapeDtypeStruct((1, D), F32), jax.ShapeDtypeStruct((1, D), F32)],
        compiler_params=_cp(("arbitrary",)),
    )(dy, x, h, g.reshape(1, D))


def _sgu_pieces(zp, lg, lb, E):
    z = _gelu(zp)
    u, v = z[:, :E], z[:, E:]
    vhat, rstd = _ln_stats(v)
    vn = vhat * lg + lb
    return u, vhat, rstd, vn


def _tril_mask():
    t = lax.broadcasted_iota(jnp.int32, (CHUNK, CHUNK), 0)
    s = lax.broadcasted_iota(jnp.int32, (CHUNK, CHUNK), 1)
    return s <= t


def _sgu_fwd(zp, ws, bs, lg, lb, *, name):
    T, E2 = zp.shape
    E = E2 // 2
    G = ws.shape[0]
    cw = E // G

    def body(zp_ref, ws_ref, bs_ref, lg_ref, lb_ref, y_ref):
        u, _, _, vn = _sgu_pieces(zp_ref[...], lg_ref[...], lb_ref[...], E)
        vnb = vn.astype(BF16)
        tril = _tril_mask()
        for g in range(G):
            cols = slice(g * cw, (g + 1) * cw)
            w = jnp.where(tril, ws_ref[g], 0.0).astype(BF16)
            sv = _dot(w, vnb[:, cols], 1, 0) + bs_ref[g]
            y_ref[:, cols] = (u[:, cols] * sv).astype(BF16)

    return pl.pallas_call(
        body, name=name, grid=(T // CHUNK,),
        in_specs=[pl.BlockSpec((CHUNK, E2), lambda n: (n, 0)),
                  pl.BlockSpec((G, CHUNK, CHUNK), lambda n: (0, 0, 0)),
                  pl.BlockSpec((G, CHUNK, 1), lambda n: (0, 0, 0)),
                  pl.BlockSpec((1, E), lambda n: (0, 0)),
                  pl.BlockSpec((1, E), lambda n: (0, 0))],
        out_specs=pl.BlockSpec((CHUNK, E), lambda n: (n, 0)),
        out_shape=jax.ShapeDtypeStruct((T, E), BF16),
        compiler_params=_cp(("parallel",)),
    )(zp, ws, bs.reshape(G, CHUNK, 1), lg.reshape(1, E), lb.reshape(1, E))


def _sgu_bwd(zp, dy, ws, bs, lg, lb, *, name):
    T, E2 = zp.shape
    E = E2 // 2
    G = ws.shape[0]
    cw = E // G
    assert G <= LANES

    def body(zp_ref, dy_ref, ws_ref, bs_ref, lg_ref, lb_ref,
             dzp_ref, dws_ref, dbs_ref, dlg_ref, dlb_ref, dvn_ref):
        @pl.when(pl.program_id(0) == 0)
        def _():
            dws_ref[...] = jnp.zeros_like(dws_ref)
            dbs_ref[...] = jnp.zeros_like(dbs_ref)
            dlg_ref[...] = jnp.zeros_like(dlg_ref)
            dlb_ref[...] = jnp.zeros_like(dlb_ref)

        zpv = zp_ref[...]
        lgv = lg_ref[...]
        u, vhat, rstd, vn = _sgu_pieces(zpv, lgv, lb_ref[...], E)
        gp = _gelu_grad(zpv)
        vnb = vn.astype(BF16)
        tril = _tril_mask()
        lane = lax.broadcasted_iota(jnp.int32, (CHUNK, LANES), 1)
        dbs = dbs_ref[...]
        for g in range(G):
            cols = slice(g * cw, (g + 1) * cw)
            w = jnp.where(tril, ws_ref[g], 0.0).astype(BF16)
            sv = _dot(w, vnb[:, cols], 1, 0) + bs_ref[g]
            dyg = dy_ref[:, cols]
            dzp_ref[:, cols] = (dyg * sv * gp[:, cols]).astype(BF16)
            dsv = dyg * u[:, cols]
            dsvb = dsv.astype(BF16)
            dws_ref[g] += jnp.where(tril, _dot(dsvb, vnb[:, cols], 1, 1), 0.0)
            dvn_ref[:, cols] = _dot(w, dsvb, 0, 0)
            dbs = dbs + jnp.where(lane == g, jnp.sum(dsv, axis=1, keepdims=True), 0.0)
        dbs_ref[...] = dbs
        dvn = dvn_ref[...]
        dlg_ref[...] += jnp.sum(dvn * vhat, axis=0, keepdims=True)
        dlb_ref[...] += jnp.sum(dvn, axis=0, keepdims=True)
        dvhat = dvn * lgv
        c1 = jnp.mean(dvhat, axis=-1, keepdims=True)
        c2 = jnp.mean(dvhat * vhat, axis=-1, keepdims=True)
        dv = rstd * (dvhat - c1 - vhat * c2)
        dzp_ref[:, E:] = (dv * gp[:, E:]).astype(BF16)

    vecE = pl.BlockSpec((1, E), lambda n: (0, 0))
    return pl.pallas_call(
        body, name=name, grid=(T // CHUNK,),
        in_specs=[pl.BlockSpec((CHUNK, E2), lambda n: (n, 0)),
                  pl.BlockSpec((CHUNK, E), lambda n: (n, 0)),
                  pl.BlockSpec((G, CHUNK, CHUNK), lambda n: (0, 0, 0)),
                  pl.BlockSpec((G, CHUNK, 1), lambda n: (0, 0, 0)),
                  vecE, vecE],
        out_specs=[pl.BlockSpec((CHUNK, E2), lambda n: (n, 0)),
                   pl.BlockSpec((G, CHUNK, CHUNK), lambda n: (0, 0, 0)),
                   pl.BlockSpec((CHUNK, LANES), lambda n: (0, 0)),
                   vecE, vecE],
        out_shape=[jax.ShapeDtypeStruct((T, E2), BF16),
                   jax.ShapeDtypeStruct((G, CHUNK, CHUNK), F32),
                   jax.ShapeDtypeStruct((CHUNK, LANES), F32),
                   jax.ShapeDtypeStruct((1, E), F32), jax.ShapeDtypeStruct((1, E), F32)],
        scratch_shapes=[pltpu.VMEM((CHUNK, E), F32)],
        compiler_params=_cp(("arbitrary",)),
    )(zp, dy, ws, bs.reshape(G, CHUNK, 1), lg.reshape(1, E), lb.reshape(1, E))


def _delay(h, k):
    t = lax.broadcasted_iota(jnp.int32, h.shape, 0)
    return jnp.where(t >= k, pltpu.roll(h, k, 0), 0.0)


def _advance(d, k):
    T = d.shape[0]
    t = lax.broadcasted_iota(jnp.int32, d.shape, 0)
    return jnp.where(t < T - k, pltpu.roll(d, T - k, 0), 0.0)


def _conv3(h, w_ref, b_ref):
    return w_ref[2:3, :] * h + w_ref[1:2, :] * _delay(h, 1) + w_ref[0:1, :] * _delay(h, 2) + b_ref[...]


def _convglu_fwd(hup, cw, cb, *, name):
    T, F2 = hup.shape
    F = F2 // 2
    tc = LANES
    nt = F // tc

    def body(ha_ref, hg_ref, wa_ref, wg_ref, ba_ref, bg_ref, o_ref):
        a = _conv3(ha_ref[...], wa_ref, ba_ref)
        g = _conv3(hg_ref[...], wg_ref, bg_ref)
        o_ref[...] = (_gelu(a) * g).astype(BF16)

    col = lambda off: pl.BlockSpec((T, tc), lambda j: (0, j + off))
    w3 = lambda off: pl.BlockSpec((3, tc), lambda j: (0, j + off))
    b1 = lambda off: pl.BlockSpec((1, tc), lambda j: (0, j + off))
    return pl.pallas_call(
        body, name=name, grid=(nt,),
        in_specs=[col(0), col(nt), w3(0), w3(nt), b1(0), b1(nt)],
        out_specs=pl.BlockSpec((T, tc), lambda j: (0, j)),
        out_shape=jax.ShapeDtypeStruct((T, F), BF16),
        compiler_params=_cp(("parallel",)),
    )(hup, hup, cw, cw, cb.reshape(1, F2), cb.reshape(1, F2))


def _convglu_bwd(hup, dact, cw, cb, *, name):
    T, F2 = hup.shape
    F = F2 // 2
    tc = LANES
    nt = F // tc

    def half(h, w_ref, d, dh_ref, dw_ref, db_ref, i):
        dh_ref[i] = (w_ref[2:3, :] * d + w_ref[1:2, :] * _advance(d, 1) + w_ref[0:1, :] * _advance(d, 2)).astype(BF16)
        dw_ref[i, 0:1, :] = jnp.sum(d * _delay(h, 2), axis=0, keepdims=True)
        dw_ref[i, 1:2, :] = jnp.sum(d * _delay(h, 1), axis=0, keepdims=True)
        dw_ref[i, 2:3, :] = jnp.sum(d * h, axis=0, keepdims=True)
        db_ref[i] = jnp.sum(d, axis=0, keepdims=True)

    def body(ha_ref, hg_ref, d_ref, wa_ref, wg_ref, ba_ref, bg_ref, dh_ref, dw_ref, db_ref):
        ha, hg = ha_ref[...], hg_ref[...]
        a = _conv3(ha, wa_ref, ba_ref)
        g = _conv3(hg, wg_ref, bg_ref)
        d = d_ref[...]
        half(ha, wa_ref, d * g * _gelu_grad(a), dh_ref, dw_ref, db_ref, 0)
        half(hg, wg_ref, d * _gelu(a), dh_ref, dw_ref, db_ref, 1)

    col = lambda off: pl.BlockSpec((T, tc), lambda j: (0, j + off))
    w3 = lambda off: pl.BlockSpec((3, tc), lambda j: (0, j + off))
    b1 = lambda off: pl.BlockSpec((1, tc), lambda j: (0, j + off))
    return pl.pallas_call(
        body, name=name, grid=(nt,),
        in_specs=[col(0), col(nt), col(0), w3(0), w3(nt), b1(0), b1(nt)],
        out_specs=[pl.BlockSpec((2, T, tc), lambda j: (0, 0, j)),
                   pl.BlockSpec((2, 3, tc), lambda j: (0, 0, j)),
                   pl.BlockSpec((2, 1, tc), lambda j: (0, 0, j))],
        out_shape=[jax.ShapeDtypeStruct((2, T, F), BF16),
                   jax.ShapeDtypeStruct((2, 3, F), F32),
                   jax.ShapeDtypeStruct((2, 1, F), F32)],
        compiler_params=_cp(("parallel",)),
    )(hup, hup, dact, cw, cw, cb.reshape(1, F2), cb.reshape(1, F2))


def _bucket_table():
    iq = np.arange(BLK)[:, None]
    ik = np.arange(2 * BLK)[None, :]
    delta = iq + BLK - ik
    out = []
    for win, dil in DILATED_GROUPS:
        valid = (delta >= 0) & (delta <= win // dil)
        n = (np.clip(delta, 0, None) * dil).astype(np.int32)
        max_exact = REL_BUCKETS // 2
        nf = np.maximum(n, 1).astype(np.float32)
        large = max_exact + (np.log(nf / np.float32(max_exact)) / np.float32(math.log(REL_MAX_DIST / max_exact))
                             * np.float32(REL_BUCKETS - max_exact)).astype(np.int32)
        large = np.minimum(large, REL_BUCKETS - 1)
        out.append(np.where(valid, np.where(n < max_exact, n, large), -1))
    return np.stack(out).astype(np.int32)


def _band_bias(rel_table, H):
    bkt = _bucket_table()
    out = []
    for g in range(N_GROUPS):
        tab = rel_table[:, g * H:(g + 1) * H]
        b = jnp.transpose(tab[np.clip(bkt[g], 0, None)], (2, 0, 1))
        b = jnp.where(bkt[g][None] >= 0, b, NEG)
        first = jnp.where((np.arange(2 * BLK) < BLK)[None, None, :], NEG, b)
        out.append(jnp.stack([b, first], axis=1))
    return jnp.stack(out).astype(F32)


def _attn_specs(H, T):
    nb = T // BLK

    def first(g, b):
        nbg = nb // jnp.left_shift(1, 2 * g)
        return jnp.where(b % nbg == 0, 1, 0)

    cur = pl.BlockSpec((None, None, BLK, HEAD_DIM), lambda g, h, b: (g, h, b, 0))
    prev = pl.BlockSpec((None, None, BLK, HEAD_DIM), lambda g, h, b: (g, h, jnp.maximum(b - 1, 0), 0))
    bias = pl.BlockSpec((None, None, None, BLK, 2 * BLK), lambda g, h, b: (g, h, first(g, b), 0, 0))
    return cur, prev, bias


def _scores(q, kp, kc, bias):
    scale = HEAD_DIM ** -0.5
    sp = _dot(q, kp, 1, 1) * scale + bias[:, :BLK]
    sc = _dot(q, kc, 1, 1) * scale + bias[:, BLK:]
    return sp, sc


def _attn_fwd(q, k, v, bias, *, name):
    _, H, T, _ = q.shape
    assert all(d == 4 ** g for g, (_, d) in enumerate(DILATED_GROUPS))
    cur, prev, bspec = _attn_specs(H, T)

    def body(q_ref, kp_ref, kc_ref, vp_ref, vc_ref, b_ref, o_ref, l_ref):
        sp, sc = _scores(q_ref[...], kp_ref[...], kc_ref[...], b_ref[...])
        m = jnp.maximum(jnp.max(sp, axis=-1, keepdims=True), jnp.max(sc, axis=-1, keepdims=True))
        pp, pc = jnp.exp(sp - m), jnp.exp(sc - m)
        den = jnp.sum(pp, axis=-1, keepdims=True) + jnp.sum(pc, axis=-1, keepdims=True)
        o_ref[...] = (_dot((pp / den).astype(BF16), vp_ref[...], 1, 0)
                      + _dot((pc / den).astype(BF16), vc_ref[...], 1, 0))
        l_ref[...] = jnp.broadcast_to(m + jnp.log(den), (BLK, HEAD_DIM))

    shp = jax.ShapeDtypeStruct(q.shape, F32)
    return pl.pallas_call(
        body, name=name, grid=(N_GROUPS, H, T // BLK),
        in_specs=[cur, prev, cur, prev, cur, bspec], out_specs=[cur, cur],
        out_shape=[shp, shp],
        compiler_params=_cp(("parallel", "parallel", "parallel")),
    )(q, k, k, v, v, bias)


def _attn_combine(o, lse, *, name):
    _, H, T, _ = o.shape
    tr = _tile(T, 1024, SUBLANES)
    grp = pl.BlockSpec((N_GROUPS, None, tr, HEAD_DIM), lambda h, i: (0, h, i, 0))
    one = pl.BlockSpec((None, tr, HEAD_DIM), lambda h, i: (h, i, 0))

    def body(o_ref, l_ref, oc_ref, lt_ref):
        l = l_ref[...]
        m = jnp.max(l, axis=0)
        w = jnp.exp(l - m[None])
        den = jnp.sum(w, axis=0)
        oc_ref[...] = jnp.sum((w / den[None]) * o_ref[...], axis=0)
        lt_ref[...] = m + jnp.log(den)

    shp = jax.ShapeDtypeStruct((H, T, HEAD_DIM), F32)
    return pl.pallas_call(
        body, name=name, grid=(H, T // tr),
        in_specs=[grp, grp], out_specs=[one, one], out_shape=[shp, shp],
        compiler_params=_cp(("parallel", "parallel")),
    )(o, lse)


def _attn_bwd_prep(do, oc, *, name):
    H, T, _ = do.shape
    tr = _tile(T, 1024, SUBLANES)
    one = pl.BlockSpec((None, tr, HEAD_DIM), lambda h, i: (h, i, 0))

    def body(do_ref, oc_ref, dl_ref, dob_ref):
        d = do_ref[...]
        dl_ref[...] = jnp.broadcast_to(jnp.sum(d * oc_ref[...], axis=-1, keepdims=True), d.shape)
        dob_ref[...] = d.astype(BF16)

    return pl.pallas_call(
        body, name=name, grid=(H, T // tr),
        in_specs=[one, one], out_specs=[one, one],
        out_shape=[jax.ShapeDtypeStruct(do.shape, F32), jax.ShapeDtypeStruct(do.shape, BF16)],
        compiler_params=_cp(("parallel", "parallel")),
    )(do, oc)


def _attn_bwd(q, k, v, bias, do, lse, delta, *, name):
    _, H, T, _ = q.shape
    cur, prev, bspec = _attn_specs(H, T)
    scale = HEAD_DIM ** -0.5

    def body(q_ref, kp_ref, kc_ref, vp_ref, vc_ref, b_ref, do_ref, l_ref, dl_ref,
             dq_ref, dkc_ref, dkp_ref, dvc_ref, dvp_ref, db_ref):
        @pl.when(pl.program_id(2) == 0)
        def _():
            db_ref[...] = jnp.zeros_like(db_ref)

        qv, kp, kc, vp, vc, dov = q_ref[...], kp_ref[...], kc_ref[...], vp_ref[...], vc_ref[...], do_ref[...]
        sp, sc = _scores(qv, kp, kc, b_ref[...])
        l = l_ref[:, 0:1]
        dl = dl_ref[:, 0:1]
        pp, pc = jnp.exp(sp - l), jnp.exp(sc - l)
        dsp = pp * (_dot(dov, vp, 1, 1) - dl)
        dsc = pc * (_dot(dov, vc, 1, 1) - dl)
        db_ref[:, :BLK] += dsp
        db_ref[:, BLK:] += dsc
        dspb, dscb = dsp.astype(BF16), dsc.astype(BF16)
        dq_ref[...] = (scale * (_dot(dspb, kp, 1, 0) + _dot(dscb, kc, 1, 0))).astype(BF16)
        dkp_ref[...] = scale * _dot(dspb, qv, 0, 0)
        dkc_ref[...] = scale * _dot(dscb, qv, 0, 0)
        dvp_ref[...] = _dot(pp.astype(BF16), dov, 0, 0)
        dvc_ref[...] = _dot(pc.astype(BF16), dov, 0, 0)

    f32 = jax.ShapeDtypeStruct(q.shape, F32)
    return pl.pallas_call(
        body, name=name, grid=(N_GROUPS, H, T // BLK),
        in_specs=[cur, prev, cur, prev, cur, bspec, cur, cur, cur],
        out_specs=[cur, cur, cur, cur, cur,
                   pl.BlockSpec((None, None, BLK, 2 * BLK), lambda g, h, b: (g, h, 0, 0))],
        out_shape=[jax.ShapeDtypeStruct(q.shape, BF16), f32, f32, f32, f32,
                   jax.ShapeDtypeStruct((N_GROUPS, H, BLK, 2 * BLK), F32)],
        compiler_params=_cp(("parallel", "parallel", "arbitrary")),
    )(q, k, k, v, v, bias, do, lse, delta)


def _attn_kv_merge(dkc, dkp, dvc, dvp, *, name):
    _, H, T, _ = dkc.shape
    nb = T // BLK
    cur = pl.BlockSpec((None, None, BLK, HEAD_DIM), lambda g, h, b: (g, h, b, 0))
    nxt = pl.BlockSpec((None, None, BLK, HEAD_DIM), lambda g, h, b: (g, h, jnp.minimum(b + 1, nb - 1), 0))

    def body(kc_ref, kp_ref, vc_ref, vp_ref, dk_ref, dv_ref):
        g, b = pl.program_id(0), pl.program_id(2)
        nbg = nb // jnp.left_shift(1, 2 * g)
        has_next = (b + 1) % nbg != 0
        dk_ref[...] = kc_ref[...] + jnp.where(has_next, kp_ref[...], 0.0)
        dv_ref[...] = vc_ref[...] + jnp.where(has_next, vp_ref[...], 0.0)

    shp = jax.ShapeDtypeStruct(dkc.shape, F32)
    return pl.pallas_call(
        body, name=name, grid=(N_GROUPS, H, nb),
        in_specs=[cur, nxt, cur, nxt], out_specs=[cur, cur], out_shape=[shp, shp],
        compiler_params=_cp(("parallel", "parallel", "parallel")),
    )(dkc, dkp, dvc, dvp)


def _rel_grad(dbs, *, name):
    H = dbs[0].shape[1]
    n = len(dbs)
    assert H <= LANES
    bkt = jnp.asarray(_bucket_table())

    def body(*refs):
        bk = refs[n][...]
        o_ref = refs[n + 1]
        db = refs[0][...]
        for r in refs[1:n]:
            db = db + r[...]
        row = lax.broadcasted_iota(jnp.int32, (REL_BUCKETS, LANES), 0)
        col = lax.broadcasted_iota(jnp.int32, (REL_BUCKETS, LANES), 1)

        def bucket(c, acc):
            sel = bk == c
            for h in range(H):
                val = jnp.sum(jnp.where(sel, db[h], 0.0))
                acc = jnp.where((row == c) & (col == h), val, acc)
            return acc

        o_ref[...] = lax.fori_loop(0, REL_BUCKETS, bucket, jnp.zeros((REL_BUCKETS, LANES), F32))

    dspec = pl.BlockSpec((None, H, BLK, 2 * BLK), lambda g: (g, 0, 0, 0))
    return pl.pallas_call(
        body, name=name, grid=(N_GROUPS,),
        in_specs=[dspec] * n + [pl.BlockSpec((None, BLK, 2 * BLK), lambda g: (g, 0, 0))],
        out_specs=pl.BlockSpec((None, REL_BUCKETS, LANES), lambda g: (g, 0, 0)),
        out_shape=jax.ShapeDtypeStruct((N_GROUPS, REL_BUCKETS, LANES), F32),
        compiler_params=_cp(("parallel",)),
    )(*dbs, bkt)


def _tok_to_group(a, d):
    T, H, Dh = a.shape
    return a.reshape(T // d, d, H, Dh).transpose(2, 1, 0, 3).reshape(H, T, Dh)


def _group_to_tok(a, d):
    H, T, Dh = a.shape
    return a.reshape(H, d, T // d, Dh).transpose(2, 1, 0, 3).reshape(T, H, Dh)


def _hm_to_group(a, d):
    H, T, Dh = a.shape
    return a.reshape(H, T // d, d, Dh).transpose(0, 2, 1, 3).reshape(H, T, Dh)


def _group_to_hm(a, d):
    H, T, Dh = a.shape
    return a.reshape(H, d, T // d, Dh).transpose(0, 2, 1, 3).reshape(H, T, Dh)


def _rows_view(a):
    a2 = a.reshape(-1, a.shape[-1])
    R, C = a2.shape
    tr = _tile(R, max(SUBLANES, (1 << 18) // C), SUBLANES)
    return a2, R, C, tr


def _addn(xs, out_dtype, *, name):
    shape = xs[0].shape
    x2s = [_rows_view(x)[0] for x in xs]
    _, R, C, tr = _rows_view(xs[0])
    spec = pl.BlockSpec((tr, C), lambda i: (i, 0))

    def body(*refs):
        acc = refs[0][...].astype(F32)
        for r in refs[1:-1]:
            acc = acc + r[...].astype(F32)
        refs[-1][...] = acc.astype(out_dtype)

    out = pl.pallas_call(
        body, name=name, grid=(R // tr,),
        in_specs=[spec] * len(xs), out_specs=spec,
        out_shape=jax.ShapeDtypeStruct((R, C), out_dtype),
        compiler_params=_cp(("parallel",)),
    )(*x2s)
    return out.reshape(shape)


def _sum_slots(buf, *, name):
    n = buf.shape[0]
    shape = buf.shape[1:]
    b3 = buf.reshape(n, -1, buf.shape[-1])
    _, R, C = b3.shape
    tr = _tile(R, max(SUBLANES, (1 << 17) // C), SUBLANES)

    def body(b_ref, o_ref):
        acc = b_ref[0]
        for s in range(1, n):
            acc = acc + b_ref[s]
        o_ref[...] = acc

    out = pl.pallas_call(
        body, name=name, grid=(R // tr,),
        in_specs=[pl.BlockSpec((n, tr, C), lambda i: (0, i, 0))],
        out_specs=pl.BlockSpec((tr, C), lambda i: (i, 0)),
        out_shape=jax.ShapeDtypeStruct((R, C), F32),
        compiler_params=_cp(("parallel",)),
    )(b3)
    return out.reshape(shape)


def _loss_grad(y, tgt, *, name):
    T, D = y.shape
    tr = _tile(T, 512, SUBLANES)
    row = pl.BlockSpec((tr, D), lambda i: (i, 0))

    def body(y_ref, t_ref, dy_ref, l_ref, acc_ref):
        i = pl.program_id(0)

        @pl.when(i == 0)
        def _():
            acc_ref[...] = jnp.zeros_like(acc_ref)

        err = y_ref[...] - t_ref[...]
        dy_ref[...] = err * (1.0 / D)
        acc_ref[...] += jnp.sum(err * err, axis=0, keepdims=True)

        @pl.when(i == T // tr - 1)
        def _():
            tot = jnp.sum(acc_ref[...], axis=1, keepdims=True) * (0.5 / D)
            l_ref[...] = jnp.broadcast_to(tot, (1, LANES))

    return pl.pallas_call(
        body, name=name, grid=(T // tr,),
        in_specs=[row, row],
        out_specs=[row, pl.BlockSpec((1, LANES), lambda i: (0, 0))],
        out_shape=[jax.ShapeDtypeStruct((T, D), F32), jax.ShapeDtypeStruct((1, LANES), F32)],
        scratch_shapes=[pltpu.VMEM((1, D), F32)],
        compiler_params=_cp(("arbitrary",)),
    )(y, tgt)


def _adamw(w, g, m, v, *, name):
    shape = w.shape
    w2, R, C, tr = _rows_view(w)
    spec = pl.BlockSpec((tr, C), lambda i: (i, 0))

    def body(w_ref, g_ref, m_ref, v_ref, d_ref, nm_ref, nv_ref):
        gv = g_ref[...]
        nm = ADAM_B1 * m_ref[...] + (1.0 - ADAM_B1) * gv
        nv = ADAM_B2 * v_ref[...] + (1.0 - ADAM_B2) * (gv * gv)
        m_hat = nm / (1.0 - ADAM_B1 ** ADAM_STEP)
        v_hat = nv / (1.0 - ADAM_B2 ** ADAM_STEP)
        d_ref[...] = -ADAM_LR * (m_hat / (jnp.sqrt(v_hat) + ADAM_EPS) + ADAM_WD * w_ref[...])
        nm_ref[...] = nm
        nv_ref[...] = nv

    shp = jax.ShapeDtypeStruct((R, C), F32)
    outs = pl.pallas_call(
        body, name=name, grid=(R // tr,),
        in_specs=[spec] * 4, out_specs=[spec] * 3, out_shape=[shp] * 3,
        compiler_params=_cp(("parallel",)),
    )(w2, g.reshape(R, C), m.reshape(R, C), v.reshape(R, C))
    return tuple(o.reshape(shape) for o in outs)


def _pack(arrs):
    flat = jnp.concatenate([a.reshape(-1).astype(F32) for a in arrs])
    unit = SUBLANES * LANES
    pad = (-flat.shape[0]) % unit
    return jnp.pad(flat, (0, pad)).reshape(-1, LANES)


def _unpack(buf, shapes):
    flat = buf.reshape(-1)
    out, off = [], 0
    for s in shapes:
        n = int(np.prod(s))
        out.append(flat[off:off + n].reshape(s))
        off += n
    return out


def _me():
    return lax.axis_index("x"), lax.axis_index("y"), lax.axis_index("c")


def _flip(pos, k):
    x, y, c = pos
    return (1 - x if k & 4 else x, 1 - y if k & 2 else y, 1 - c if k & 1 else c)


HBM = pl.BlockSpec(memory_space=pltpu.HBM)

COL, ROW = -1, -2


def _shard_window(ref, axis, j, n):
    start = pl.multiple_of(j * n, n)
    if axis == COL:
        return ref.at[:, :, pl.ds(start, n)]
    return ref.at[:, pl.ds(start, n), :]


def _gather_weights(shards, axes, small, *, name):
    n = len(shards)
    full_shapes = []
    for s, ax in zip(shards, axes):
        shp = list(s.shape)
        shp[ax] *= N_CHIPS
        full_shapes.append(jax.ShapeDtypeStruct(tuple(shp), s.dtype))
    full_shapes.append(jax.ShapeDtypeStruct((N_CHIPS,) + small.shape, small.dtype))
    n_t = n + 1
    peers = (2, 4, 6)

    def body(*refs):
        srcs, dsts = refs[:n_t], refs[n_t:2 * n_t]
        send_sems, recv_sems, local_sems = refs[2 * n_t:]
        me = _me()
        j = 2 * me[0] + me[1]

        def window(i, ref, chip):
            if i == n:
                return ref.at[chip]
            return _shard_window(ref, axes[i], chip, srcs[i].shape[axes[i]])

        local = [pltpu.make_async_copy(srcs[i], window(i, dsts[i], j), local_sems.at[i]) for i in range(n_t)]
        for cp in local:
            cp.start()
        sends = []
        for pi, k in enumerate(peers):
            for i in range(n_t):
                sends.append(pltpu.make_async_remote_copy(
                    src_ref=srcs[i], dst_ref=window(i, dsts[i], j),
                    send_sem=send_sems.at[pi * n_t + i], recv_sem=recv_sems.at[pi * n_t + i],
                    device_id=_flip(me, k), device_id_type=MESH))
        for cp in sends:
            cp.start()
        for pi, k in enumerate(peers):
            px, py, _ = _flip(me, k)
            for i in range(n_t):
                pltpu.make_async_remote_copy(
                    src_ref=srcs[i], dst_ref=window(i, dsts[i], 2 * px + py),
                    send_sem=send_sems.at[pi * n_t + i], recv_sem=recv_sems.at[pi * n_t + i],
                    device_id=_flip(me, k), device_id_type=MESH).wait_recv()
        for cp in sends:
            cp.wait_send()
        for cp in local:
            cp.wait()

    return pl.pallas_call(
        body, name=name,
        in_specs=[HBM] * n_t, out_specs=[HBM] * n_t, out_shape=full_shapes,
        scratch_shapes=[pltpu.SemaphoreType.DMA((len(peers) * n_t,)),
                        pltpu.SemaphoreType.DMA((len(peers) * n_t,)),
                        pltpu.SemaphoreType.DMA((n_t,))],
    )(*shards, small)


def _scatter_grads(grads, axes, small, *, name):
    n = len(grads)
    part_shapes = []
    for gr, ax in zip(grads, axes):
        L, R, C = gr.shape
        if ax == COL:
            C //= N_CHIPS
        else:
            R //= N_CHIPS
        if L == 1:
            R //= 2
        part_shapes.append((max(L // 2, 1), R, C))
    out_shapes = [jax.ShapeDtypeStruct((N_DEV,) + s, F32) for s in part_shapes]
    out_shapes.append(jax.ShapeDtypeStruct((N_DEV,) + small.shape, F32))
    n_t = n + 1

    def body(*refs):
        srcs, dsts = refs[:n_t], refs[n_t:2 * n_t]
        send_sems, recv_sems, local_sems = refs[2 * n_t:]
        me = _me()
        my_id = 4 * me[0] + 2 * me[1] + me[2]

        def part(i, pos):
            if i == n:
                return srcs[i]
            px, py, pc = pos
            L = srcs[i].shape[0]
            Lp, Rp, Cp = part_shapes[i]
            chip = 2 * px + py
            if L == 1:
                assert axes[i] == ROW
                return srcs[i].at[:, pl.ds(pl.multiple_of((2 * chip + pc) * Rp, Rp), Rp), :]
            lay = pl.ds(pc * Lp, Lp)
            if axes[i] == COL:
                return srcs[i].at[lay, :, pl.ds(pl.multiple_of(chip * Cp, Cp), Cp)]
            return srcs[i].at[lay, pl.ds(pl.multiple_of(chip * Rp, Rp), Rp), :]

        local = [pltpu.make_async_copy(part(i, me), dsts[i].at[my_id], local_sems.at[i]) for i in range(n_t)]
        for cp in local:
            cp.start()
        sends = []
        for k in range(1, N_DEV):
            peer = _flip(me, k)
            for i in range(n_t):
                sends.append(pltpu.make_async_remote_copy(
                    src_ref=part(i, peer), dst_ref=dsts[i].at[my_id],
                    send_sem=send_sems.at[(k - 1) * n_t + i], recv_sem=recv_sems.at[(k - 1) * n_t + i],
                    device_id=peer, device_id_type=MESH))
        for cp in sends:
            cp.start()
        for k in range(1, N_DEV):
            px, py, pc = _flip(me, k)
            for i in range(n_t):
                pltpu.make_async_remote_copy(
                    src_ref=part(i, me), dst_ref=dsts[i].at[4 * px + 2 * py + pc],
                    send_sem=send_sems.at[(k - 1) * n_t + i], recv_sem=recv_sems.at[(k - 1) * n_t + i],
                    device_id=(px, py, pc), device_id_type=MESH).wait_recv()
        for cp in sends:
            cp.wait_send()
        for cp in local:
            cp.wait()

    n_sem = (N_DEV - 1) * n_t
    return pl.pallas_call(
        body, name=name,
        in_specs=[HBM] * n_t, out_specs=[HBM] * n_t, out_shape=out_shapes,
        scratch_shapes=[pltpu.SemaphoreType.DMA((n_sem,)), pltpu.SemaphoreType.DMA((n_sem,)),
                        pltpu.SemaphoreType.DMA((n_t,))],
    )(*grads, small)


def _pair_gather(parts, *, name):
    n = len(parts)
    out_shapes = [jax.ShapeDtypeStruct((2,) + p.shape, p.dtype) for p in parts]

    def body(*refs):
        srcs, dsts = refs[:n], refs[n:2 * n]
        send_sems, recv_sems, local_sems = refs[2 * n:]
        me = _me()
        c = me[2]
        sib = _flip(me, 1)
        local = [pltpu.make_async_copy(srcs[i], dsts[i].at[c], local_sems.at[i]) for i in range(n)]
        for cp in local:
            cp.start()
        sends = [pltpu.make_async_remote_copy(
            src_ref=srcs[i], dst_ref=dsts[i].at[c], send_sem=send_sems.at[i], recv_sem=recv_sems.at[i],
            device_id=sib, device_id_type=MESH) for i in range(n)]
        for cp in sends:
            cp.start()
        for i in range(n):
            pltpu.make_async_remote_copy(
                src_ref=srcs[i], dst_ref=dsts[i].at[1 - c], send_sem=send_sems.at[i], recv_sem=recv_sems.at[i],
                device_id=sib, device_id_type=MESH).wait_recv()
        for cp in sends:
            cp.wait_send()
        for cp in local:
            cp.wait()

    return pl.pallas_call(
        body, name=name,
        in_specs=[HBM] * n, out_specs=[HBM] * n, out_shape=out_shapes,
        scratch_shapes=[pltpu.SemaphoreType.DMA((n,)), pltpu.SemaphoreType.DMA((n,)),
                        pltpu.SemaphoreType.DMA((n,))],
    )(*parts)


def _local_step(x, tgt, W, P):
    T, D = x.shape
    depth = P["ln_g"].shape[0]
    n_a = W["a_w_in"].shape[0]
    alpha = (2 * depth) ** 0.25
    H = W["kv_w"].shape[1] // (2 * HEAD_DIM)
    HD = H * HEAD_DIM
    dils = [d for _, d in DILATED_GROUPS]
    assert T % (dils[-1] * BLK) == 0

    bias = _band_bias(P["rel_table"], H)
    saved = []
    xf, xb = x, x.astype(BF16)
    kg = vg = None
    for i in range(depth):
        s = {"x": xf, "xb": xb}
        if i < n_a:
            s["zp"] = _mm(xb, W["a_w_in"], b_lead=(i,), name=f"a{i}_in")
            s["y"] = _sgu_fwd(s["zp"], P["a_w_s"][i], P["a_b_s"][i], P["a_ln_g"][i], P["a_ln_b"][i], name=f"a{i}_sgu")
            s["h"] = _mm(s["y"], W["a_w_out"], b_lead=(i,), name=f"a{i}_out")
        else:
            j = i - n_a
            if j == 0:
                kv = _mm(xb, W["kv_w"], out_dtype=BF16, name="kv_proj")
                k_tok = kv[:, :HD].reshape(T, H, HEAD_DIM)
                v_tok = kv[:, HD:].reshape(T, H, HEAD_DIM)
                kg = jnp.stack([_tok_to_group(k_tok, d) for d in dils])
                vg = jnp.stack([_tok_to_group(v_tok, d) for d in dils])
            q = _mm(xb, W["b_w_q"], b_lead=(j,), out_dtype=BF16, name=f"b{j}_q").reshape(T, N_GROUPS, H, HEAD_DIM)
            s["qg"] = jnp.stack([_tok_to_group(q[:, g], d) for g, d in enumerate(dils)])
            og, lg = _attn_fwd(s["qg"], kg, vg, bias, name=f"b{j}_attn")
            o_hm = jnp.stack([_group_to_hm(og[g], d) for g, d in enumerate(dils)])
            l_hm = jnp.stack([_group_to_hm(lg[g], d) for g, d in enumerate(dils)])
            s["oc"], s["lse"] = _attn_combine(o_hm, l_hm, name=f"b{j}_comb")
            s["ob"] = s["oc"].transpose(1, 0, 2).reshape(T, HD).astype(BF16)
            s["h"] = _mm(s["ob"], W["b_w_o"], b_lead=(j,), name=f"b{j}_o")
        s["x1"], s["x1b"] = _add_ln(xf, s["h"], P["ln_g"][i, 0], P["ln_b"][i, 0], alpha, name=f"l{i}_ln1")
        s["hup"] = _mm(s["x1b"], W["ffn_w_up"], b_lead=(i,), name=f"l{i}_up")
        s["act"] = _convglu_fwd(s["hup"], P["ffn_conv_w"][i], P["ffn_conv_b"][i], name=f"l{i}_glu")
        s["f"] = _mm(s["act"], W["ffn_w_down"], b_lead=(i,), name=f"l{i}_down")
        xf, xb = _add_ln(s["x1"], s["f"], P["ln_g"][i, 1], P["ln_b"][i, 1], alpha, name=f"l{i}_ln2")
        saved.append(s)

    G, loss_row = _loss_grad(xf, tgt, name="loss")

    gw = {}

    def dw(key, layer, a, b, name, **kw):
        n_layers = W[key].shape[0] if W[key].ndim == 3 else 1
        gw[key] = _mm(a, b, ta=True, out_into=(gw.get(key), n_layers, layer), name=name, **kw)

    gp = {k: [None] * n_a for k in ("a_ln_g", "a_ln_b", "a_w_s", "a_b_s")}
    gp.update({k: [None] * depth for k in ("ffn_conv_w", "ffn_conv_b", "ln_g", "ln_b")})
    dk_parts, dv_parts, dbias_parts = [], [], []
    for i in reversed(range(depth)):
        s = saved[i]
        dr2, dr2b, dg2, db2 = _ln_bwd(G, s["x1"], s["f"], P["ln_g"][i, 1], alpha, name=f"l{i}_ln2_bwd")
        dw("ffn_w_down", i, s["act"], dr2b, f"l{i}_down_dw")
        dact = _mm(dr2b, W["ffn_w_down"], tb=True, b_lead=(i,), tiles=(1024, 1408, 512), name=f"l{i}_down_dx")
        dhup, dcw, dcb = _convglu_bwd(s["hup"], dact, P["ffn_conv_w"][i], P["ffn_conv_b"][i], name=f"l{i}_glu_bwd")
        gp["ffn_conv_w"][i] = dcw.transpose(1, 0, 2).reshape(dcw.shape[1], -1)
        gp["ffn_conv_b"][i] = dcb.reshape(-1)
        dw("ffn_w_up", i, s["x1b"], dhup, f"l{i}_up_dw", b_halves=True)
        G1 = _mm(dhup, W["ffn_w_up"], tb=True, a_halves=True, b_lead=(i,), add=dr2, add_scale=alpha, name=f"l{i}_up_dx")
        dr1, dr1b, dg1, db1 = _ln_bwd(G1, s["x"], s["h"], P["ln_g"][i, 0], alpha, name=f"l{i}_ln1_bwd")
        gp["ln_g"][i] = jnp.concatenate([dg1, dg2], axis=0)
        gp["ln_b"][i] = jnp.concatenate([db1, db2], axis=0)
        if i < n_a:
            dw("a_w_out", i, s["y"], dr1b, f"a{i}_out_dw")
            dy = _mm(dr1b, W["a_w_out"], tb=True, b_lead=(i,), name=f"a{i}_out_dx")
            dzp, dws, dbs, dlg, dlb = _sgu_bwd(s["zp"], dy, P["a_w_s"][i], P["a_b_s"][i], P["a_ln_g"][i],
                                               P["a_ln_b"][i], name=f"a{i}_sgu_bwd")
            gp["a_w_s"][i], gp["a_b_s"][i] = dws, dbs[:, :dws.shape[0]].T
            gp["a_ln_g"][i], gp["a_ln_b"][i] = dlg[0], dlb[0]
            dw("a_w_in", i, s["xb"], dzp, f"a{i}_in_dw")
            G = _mm(dzp, W["a_w_in"], tb=True, b_lead=(i,), add=dr1, add_scale=alpha, name=f"a{i}_in_dx")
        else:
            j = i - n_a
            dw("b_w_o", j, s["ob"], dr1b, f"b{j}_o_dw")
            do_tok = _mm(dr1b, W["b_w_o"], tb=True, b_lead=(j,), name=f"b{j}_o_dx")
            do_hm = do_tok.reshape(T, H, HEAD_DIM).transpose(1, 0, 2)
            delta, dob = _attn_bwd_prep(do_hm, s["oc"], name=f"b{j}_prep")
            to_groups = lambda a: jnp.stack([_hm_to_group(a, d) for d in dils])
            dq, dkc, dkp, dvc, dvp, dbias = _attn_bwd(s["qg"], kg, vg, bias, to_groups(dob), to_groups(s["lse"]),
                                                      to_groups(delta), name=f"b{j}_attn_bwd")
            dbias_parts.append(dbias)
            dk, dv = _attn_kv_merge(dkc, dkp, dvc, dvp, name=f"b{j}_kv_merge")
            dk_parts += [_group_to_tok(dk[g], d).reshape(T, HD) for g, d in enumerate(dils)]
            dv_parts += [_group_to_tok(dv[g], d).reshape(T, HD) for g, d in enumerate(dils)]
            dq_tok = jnp.stack([_group_to_tok(dq[g], d) for g, d in enumerate(dils)], axis=1).reshape(T, N_GROUPS * HD)
            dw("b_w_q", j, s["xb"], dq_tok, f"b{j}_q_dw")
            G = _mm(dq_tok, W["b_w_q"], tb=True, b_lead=(j,), add=dr1, add_scale=alpha, name=f"b{j}_q_dx")
            if j == 0:
                dkv = jnp.concatenate([_addn(dk_parts, BF16, name="dk_sum"), _addn(dv_parts, BF16, name="dv_sum")], axis=1)
                dw("kv_w", 0, s["xb"], dkv, "kv_dw")
                G = _mm(dkv, W["kv_w"], tb=True, add=G, add_scale=1.0, name="kv_dx")
    rel = _rel_grad(dbias_parts, name="rel_grad")
    grel = rel[:, :, :H].transpose(1, 0, 2).reshape(REL_BUCKETS, N_GROUPS * H)
    gsmall = {k: jnp.stack(v) for k, v in gp.items()}
    gsmall["rel_table"] = grel
    return loss_row, G, gw, gsmall


BIG = (("a_w_in", COL), ("a_w_out", ROW), ("kv_w", ROW), ("b_w_q", COL), ("b_w_o", COL),
       ("ffn_w_up", COL), ("ffn_w_down", ROW))
SMALL_SHARDED = ("a_ln_g", "a_ln_b", "ffn_conv_w", "ln_g", "ln_b")
SMALL_REPLICATED = ("a_w_s", "a_b_s", "rel_table", "ffn_conv_b")
WEIGHTS = ("a_w_in", "a_ln_g", "a_ln_b", "a_w_s", "a_b_s", "a_w_out", "kv_w", "b_w_q", "b_w_o", "rel_table",
           "ffn_w_up", "ffn_conv_w", "ffn_conv_b", "ffn_w_down", "ln_g", "ln_b")


def kernel(x, a_w_in, a_ln_g, a_ln_b, a_w_s, a_b_s, a_w_out, kv_w, b_w_q, b_w_o, rel_table, ffn_w_up, ffn_conv_w, ffn_conv_b, ffn_w_down, ln_g, ln_b, loss_target, m_a_w_in, m_a_ln_g, m_a_ln_b, m_a_w_s, m_a_b_s, m_a_w_out, m_kv_w, m_b_w_q, m_b_w_o, m_rel_table, m_ffn_w_up, m_ffn_conv_w, m_ffn_conv_b, m_ffn_w_down, m_ln_g, m_ln_b, v_a_w_in, v_a_ln_g, v_a_ln_b, v_a_w_s, v_a_b_s, v_a_w_out, v_kv_w, v_b_w_q, v_b_w_o, v_rel_table, v_ffn_w_up, v_ffn_conv_w, v_ffn_conv_b, v_ffn_w_down, v_ln_g, v_ln_b):
    w = dict(a_w_in=a_w_in, a_ln_g=a_ln_g, a_ln_b=a_ln_b, a_w_s=a_w_s, a_b_s=a_b_s, a_w_out=a_w_out, kv_w=kv_w,
             b_w_q=b_w_q, b_w_o=b_w_o, rel_table=rel_table, ffn_w_up=ffn_w_up, ffn_conv_w=ffn_conv_w,
             ffn_conv_b=ffn_conv_b, ffn_w_down=ffn_w_down, ln_g=ln_g, ln_b=ln_b)
    m = dict(a_w_in=m_a_w_in, a_ln_g=m_a_ln_g, a_ln_b=m_a_ln_b, a_w_s=m_a_w_s, a_b_s=m_a_b_s, a_w_out=m_a_w_out,
             kv_w=m_kv_w, b_w_q=m_b_w_q, b_w_o=m_b_w_o, rel_table=m_rel_table, ffn_w_up=m_ffn_w_up,
             ffn_conv_w=m_ffn_conv_w, ffn_conv_b=m_ffn_conv_b, ffn_w_down=m_ffn_w_down, ln_g=m_ln_g, ln_b=m_ln_b)
    v = dict(a_w_in=v_a_w_in, a_ln_g=v_a_ln_g, a_ln_b=v_a_ln_b, a_w_s=v_a_w_s, a_b_s=v_a_b_s, a_w_out=v_a_w_out,
             kv_w=v_kv_w, b_w_q=v_b_w_q, b_w_o=v_b_w_o, rel_table=v_rel_table, ffn_w_up=v_ffn_w_up,
             ffn_conv_w=v_ffn_conv_w, ffn_conv_b=v_ffn_conv_b, ffn_w_down=v_ffn_w_down, ln_g=v_ln_g, ln_b=v_ln_b)
    chip = 2 * lax.axis_index("x") + lax.axis_index("y")

    big_names = [n for n, _ in BIG]
    big_axes = [a for _, a in BIG]
    shards = [w[n].astype(BF16) for n in big_names]
    shards = [s.reshape((1,) + s.shape) if s.ndim == 2 else s for s in shards]
    small_shapes = [w[n].shape for n in SMALL_SHARDED]
    gathered = _gather_weights(shards, big_axes, _pack([w[n] for n in SMALL_SHARDED]), name="gather_weights")
    W = dict(zip(big_names, gathered[:-1]))
    W["kv_w"] = W["kv_w"].reshape(W["kv_w"].shape[1:])
    P = {n: w[n] for n in SMALL_REPLICATED}
    per_chip = [_unpack(gathered[-1][j], small_shapes) for j in range(N_CHIPS)]
    for i, n in enumerate(SMALL_SHARDED):
        P[n] = jnp.concatenate([per_chip[j][i] for j in range(N_CHIPS)], axis=-1)

    loss_row, grad_x, gw, gsmall = _local_step(x[0], loss_target[0], W, P)
    loss = lax.psum(loss_row[0, 0], ("x", "y", "c"))

    small_names = list(SMALL_SHARDED) + list(SMALL_REPLICATED)
    small_pack = _pack([gsmall[n] for n in small_names])
    slots = _scatter_grads([gw[n] for n in big_names], big_axes, small_pack, name="scatter_grads")
    sums = [_sum_slots(b, name=f"sum_{n}") for n, b in zip(big_names, slots[:-1])]
    small_sum = _sum_slots(slots[-1], name="sum_small")
    pairs = _pair_gather(sums, name="pair_gather")
    grad = {n: p.reshape(w[n].shape) for n, p in zip(big_names, pairs)}
    full_small = _unpack(small_sum, [gsmall[n].shape for n in small_names])
    for n, gfull in zip(small_names, full_small):
        if n in SMALL_SHARDED:
            width = w[n].shape[-1]
            grad[n] = lax.dynamic_slice_in_dim(gfull, chip * width, width, axis=-1)
        else:
            grad[n] = gfull

    delta, new_m, new_v = {}, {}, {}
    for n, _ in BIG:
        delta[n], new_m[n], new_v[n] = _adamw(w[n], grad[n], m[n], v[n], name=f"adamw_{n}")
    shapes = [w[n].shape for n in small_names]
    packed = _adamw(_pack([w[n] for n in small_names]), _pack([grad[n] for n in small_names]),
                    _pack([m[n] for n in small_names]), _pack([v[n] for n in small_names]), name="adamw_small")
    for out, res in zip((delta, new_m, new_v), packed):
        for n, a in zip(small_names, _unpack(res, shapes)):
            out[n] = a

    return (loss, grad_x[None], *[grad[n] for n in WEIGHTS], *[delta[n] for n in WEIGHTS],
            *[new_m[n] for n in WEIGHTS], *[new_v[n] for n in WEIGHTS])
```

```python
import functools
import math

import numpy as np
import jax
import jax.numpy as jnp
from jax import lax
from jax.experimental import pallas as pl
from jax.experimental.pallas import tpu as pltpu

F32 = jnp.float32
BF16 = jnp.bfloat16
MESH = pl.DeviceIdType.MESH

CHUNK = 128
HEAD_DIM = 64
DILATED_GROUPS = ((128, 1), (512, 4), (2048, 16))
N_GROUPS = len(DILATED_GROUPS)
BLK = 128
REL_BUCKETS = 32
REL_MAX_DIST = 2048
LN_EPS = 1e-5
NEG = -1e30
ADAM_LR = 0.001
ADAM_B1 = 0.9
ADAM_B2 = 0.999
ADAM_EPS = 1e-08
ADAM_WD = 0.01
ADAM_STEP = 10

N_CHIPS = 4
N_DEV = 8
LANES = 128
SUBLANES = 8
VMEM_LIMIT = 48 * 1024 * 1024

_SQRT_HALF = 0.7071067811865476
_INV_SQRT_2PI = 0.3989422804014327


def _cp(sem=None, vmem=VMEM_LIMIT):
    return pltpu.CompilerParams(dimension_semantics=sem, vmem_limit_bytes=vmem)


def _tile(dim, target, align=LANES):
    if dim <= target:
        return dim
    t = (target // align) * align
    while t >= align:
        if dim % t == 0:
            return t
        t -= align
    return dim


def _gelu(x):
    return 0.5 * x * (1.0 + lax.erf(x * _SQRT_HALF))


def _gelu_grad(x):
    return 0.5 * (1.0 + lax.erf(x * _SQRT_HALF)) + x * (_INV_SQRT_2PI * jnp.exp(-0.5 * x * x))


def _dot(a, b, ca, cb):
    return lax.dot_general(a, b, (((ca,), (cb,)), ((), ())), preferred_element_type=F32)


def _mn_tile(dim):
    return max(_tile(dim, 1024), _tile(dim, 1408))


def _mm(a, b, *, name, ta=False, tb=False, out_dtype=F32, tiles=None,
        a_lead=(), b_lead=(), a_halves=False, b_halves=False, add=None, add_scale=1.0, out_into=None):
    a2, b2 = a.shape[-2:], b.shape[-2:]
    M, K = (a2[1], a2[0]) if ta else a2
    Kb, N = (b2[1], b2[0]) if tb else b2
    if a_halves:
        K = 2 * K
    if b_halves:
        N = 2 * N
    assert K == Kb, (name, a.shape, b.shape)
    if tiles is None:
        k_target = 1024 if ta else (1408 if a_halves else (K if K <= 2816 else 2048))
        tm, tn = _mn_tile(M), _mn_tile(N // 2 if b_halves else N)
        tk = _tile(K // 2 if a_halves else K, k_target)
    else:
        tm = _tile(M, tiles[0])
        tn = _tile(N // 2 if b_halves else N, tiles[1])
        tk = _tile(K // 2 if a_halves else K, tiles[2])
    nm, nn, nk = M // tm, N // tn, K // tk
    nkh, nnh = nk // 2, nn // 2

    def lead(idx, rest):
        return tuple(idx) + tuple(rest)

    sq_a = (None,) * (a.ndim - 2)
    sq_b = (None,) * (b.ndim - 2)
    if a_halves:
        assert not ta and a.ndim == 3
        a_spec = pl.BlockSpec((None, tm, tk), lambda i, j, k: (k // nkh, i, k % nkh))
    elif ta:
        a_spec = pl.BlockSpec(sq_a + (tk, tm), lambda i, j, k: lead(a_lead, (k, i)))
    else:
        a_spec = pl.BlockSpec(sq_a + (tm, tk), lambda i, j, k: lead(a_lead, (i, k)))
    if b_halves:
        assert not tb and b.ndim == 3
        b_spec = pl.BlockSpec((None, tk, tn), lambda i, j, k: (j // nnh, k, j % nnh))
    elif tb:
        b_spec = pl.BlockSpec(sq_b + (tn, tk), lambda i, j, k: lead(b_lead, (j, k)))
    else:
        b_spec = pl.BlockSpec(sq_b + (tk, tn), lambda i, j, k: lead(b_lead, (k, j)))
    mn_spec = pl.BlockSpec((tm, tn), lambda i, j, k: (i, j))
    in_specs = [a_spec, b_spec]
    args = [a, b]
    if add is not None:
        in_specs.append(mn_spec)
        args.append(add)
    aliases = {}
    if out_into is None:
        o_spec, out_shape = mn_spec, jax.ShapeDtypeStruct((M, N), out_dtype)
    else:
        buf, n_layers, layer = out_into
        o_spec = pl.BlockSpec((None, tm, tn), lambda i, j, k: (layer, i, j))
        out_shape = jax.ShapeDtypeStruct((n_layers, M, N), out_dtype)
        if buf is not None:
            aliases = {len(args): 0}
            in_specs.append(pl.BlockSpec(memory_space=pl.ANY))
            args.append(buf)
    n_in = len(args)
    ca, cb = (0 if ta else 1), (1 if tb else 0)

    def body(*refs):
        a_ref, b_ref = refs[:2]
        add_ref = refs[2] if add is not None else None
        o_ref = refs[n_in]

        def finish(acc):
            if add_ref is not None:
                acc = acc + add_scale * add_ref[...]
            o_ref[...] = acc.astype(o_ref.dtype)

        if nk == 1:
            finish(_dot(a_ref[...], b_ref[...], ca, cb))
        else:
            acc_ref = refs[-1]
            k = pl.program_id(2)

            @pl.when(k == 0)
            def _():
                acc_ref[...] = jnp.zeros_like(acc_ref)

            acc_ref[...] += _dot(a_ref[...], b_ref[...], ca, cb)

            @pl.when(k == nk - 1)
            def _():
                finish(acc_ref[...])

    return pl.pallas_call(
        body, name=name, grid=(nm, nn, nk),
        in_specs=in_specs, out_specs=o_spec, out_shape=out_shape,
        input_output_aliases=aliases,
        scratch_shapes=[] if nk == 1 else [pltpu.VMEM((tm, tn), F32)],
        compiler_params=_cp(("parallel", "parallel", "arbitrary")),
    )(*args)


def _ln_stats(r):
    mu = jnp.mean(r, axis=-1, keepdims=True)
    xc = r - mu
    var = jnp.mean(xc * xc, axis=-1, keepdims=True)
    rstd = lax.rsqrt(var + LN_EPS)
    return xc * rstd, rstd


def _add_ln(x, h, g, b, alpha, *, name):
    T, D = x.shape
    tr = _tile(T, 512, SUBLANES)
    row = pl.BlockSpec((tr, D), lambda i: (i, 0))
    vec = pl.BlockSpec((1, D), lambda i: (0, 0))

    def body(x_ref, h_ref, g_ref, b_ref, o_ref, ob_ref):
        xhat, _ = _ln_stats(alpha * x_ref[...] + h_ref[...])
        y = xhat * g_ref[...] + b_ref[...]
        o_ref[...] = y
        ob_ref[...] = y.astype(BF16)

    return pl.pallas_call(
        body, name=name, grid=(T // tr,),
        in_specs=[row, row, vec, vec], out_specs=[row, row],
        out_shape=[jax.ShapeDtypeStruct((T, D), F32), jax.ShapeDtypeStruct((T, D), BF16)],
        compiler_params=_cp(("parallel",)),
    )(x, h, g.reshape(1, D), b.reshape(1, D))


def _ln_bwd(dy, x, h, g, alpha, *, name):
    T, D = x.shape
    tr = _tile(T, 512, SUBLANES)
    row = pl.BlockSpec((tr, D), lambda i: (i, 0))
    vec = pl.BlockSpec((1, D), lambda i: (0, 0))

    def body(dy_ref, x_ref, h_ref, g_ref, dr_ref, drb_ref, dg_ref, db_ref):
        @pl.when(pl.program_id(0) == 0)
        def _():
            dg_ref[...] = jnp.zeros_like(dg_ref)
            db_ref[...] = jnp.zeros_like(db_ref)

        xhat, rstd = _ln_stats(alpha * x_ref[...] + h_ref[...])
        dyv = dy_ref[...]
        dxhat = dyv * g_ref[...]
        c1 = jnp.mean(dxhat, axis=-1, keepdims=True)
        c2 = jnp.mean(dxhat * xhat, axis=-1, keepdims=True)
        dr = rstd * (dxhat - c1 - xhat * c2)
        dr_ref[...] = dr
        drb_ref[...] = dr.astype(BF16)
        dg_ref[...] += jnp.sum(dyv * xhat, axis=0, keepdims=True)
        db_ref[...] += jnp.sum(dyv, axis=0, keepdims=True)

    return pl.pallas_call(
        body, name=name, grid=(T // tr,),
        in_specs=[row, row, row, vec], out_specs=[row, row, vec, vec],
        out_shape=[jax.ShapeDtypeStruct((T, D), F32), jax.ShapeDtypeStruct((T, D), BF16),
                   jax.ShapeDtypeStruct((1, D), F32), jax.ShapeDtypeStruct((1, D), F32)],
        compiler_params=_cp(("arbitrary",)),
    )(dy, x, h, g.reshape(1, D))


def _sgu_pieces(zp, lg, lb, E):
    z = _gelu(zp)
    u, v = z[:, :E], z[:, E:]
    vhat, rstd = _ln_stats(v)
    vn = vhat * lg + lb
    return u, vhat, rstd, vn


def _tril_mask():
    t = lax.broadcasted_iota(jnp.int32, (CHUNK, CHUNK), 0)
    s = lax.broadcasted_iota(jnp.int32, (CHUNK, CHUNK), 1)
    return s <= t


def _sgu_fwd(zp, ws, bs, lg, lb, *, name):
    T, E2 = zp.shape
    E = E2 // 2
    G = ws.shape[0]
    cw = E // G

    def body(zp_ref, ws_ref, bs_ref, lg_ref, lb_ref, y_ref):
        u, _, _, vn = _sgu_pieces(zp_ref[...], lg_ref[...], lb_ref[...], E)
        vnb = vn.astype(BF16)
        tril = _tril_mask()
        for g in range(G):
            cols = slice(g * cw, (g + 1) * cw)
            w = jnp.where(tril, ws_ref[g], 0.0).astype(BF16)
            sv = _dot(w, vnb[:, cols], 1, 0) + bs_ref[g]
            y_ref[:, cols] = (u[:, cols] * sv).astype(BF16)

    return pl.pallas_call(
        body, name=name, grid=(T // CHUNK,),
        in_specs=[pl.BlockSpec((CHUNK, E2), lambda n: (n, 0)),
                  pl.BlockSpec((G, CHUNK, CHUNK), lambda n: (0, 0, 0)),
                  pl.BlockSpec((G, CHUNK, 1), lambda n: (0, 0, 0)),
                  pl.BlockSpec((1, E), lambda n: (0, 0)),
                  pl.BlockSpec((1, E), lambda n: (0, 0))],
        out_specs=pl.BlockSpec((CHUNK, E), lambda n: (n, 0)),
        out_shape=jax.ShapeDtypeStruct((T, E), BF16),
        compiler_params=_cp(("parallel",)),
    )(zp, ws, bs.reshape(G, CHUNK, 1), lg.reshape(1, E), lb.reshape(1, E))


def _sgu_bwd(zp, dy, ws, bs, lg, lb, *, name):
    T, E2 = zp.shape
    E = E2 // 2
    G = ws.shape[0]
    cw = E // G
    assert G <= LANES

    def body(zp_ref, dy_ref, ws_ref, bs_ref, lg_ref, lb_ref,
             dzp_ref, dws_ref, dbs_ref, dlg_ref, dlb_ref, dvn_ref):
        @pl.when(pl.program_id(0) == 0)
        def _():
            dws_ref[...] = jnp.zeros_like(dws_ref)
            dbs_ref[...] = jnp.zeros_like(dbs_ref)
            dlg_ref[...] = jnp.zeros_like(dlg_ref)
            dlb_ref[...] = jnp.zeros_like(dlb_ref)

        zpv = zp_ref[...]
        lgv = lg_ref[...]
        u, vhat, rstd, vn = _sgu_pieces(zpv, lgv, lb_ref[...], E)
        gp = _gelu_grad(zpv)
        vnb = vn.astype(BF16)
        tril = _tril_mask()
        lane = lax.broadcasted_iota(jnp.int32, (CHUNK, LANES), 1)
        dbs = dbs_ref[...]
        for g in range(G):
            cols = slice(g * cw, (g + 1) * cw)
            w = jnp.where(tril, ws_ref[g], 0.0).astype(BF16)
            sv = _dot(w, vnb[:, cols], 1, 0) + bs_ref[g]
            dyg = dy_ref[:, cols]
            dzp_ref[:, cols] = (dyg * sv * gp[:, cols]).astype(BF16)
            dsv = dyg * u[:, cols]
            dsvb = dsv.astype(BF16)
            dws_ref[g] += jnp.where(tril, _dot(dsvb, vnb[:, cols], 1, 1), 0.0)
            dvn_ref[:, cols] = _dot(w, dsvb, 0, 0)
            dbs = dbs + jnp.where(lane == g, jnp.sum(dsv, axis=1, keepdims=True), 0.0)
        dbs_ref[...] = dbs
        dvn = dvn_ref[...]
        dlg_ref[...] += jnp.sum(dvn * vhat, axis=0, keepdims=True)
        dlb_ref[...] += jnp.sum(dvn, axis=0, keepdims=True)
        dvhat = dvn * lgv
        c1 = jnp.mean(dvhat, axis=-1, keepdims=True)
        c2 = jnp.mean(dvhat * vhat, axis=-1, keepdims=True)
        dv = rstd * (dvhat - c1 - vhat * c2)
        dzp_ref[:, E:] = (dv * gp[:, E:]).astype(BF16)

    vecE = pl.BlockSpec((1, E), lambda n: (0, 0))
    return pl.pallas_call(
        body, name=name, grid=(T // CHUNK,),
        in_specs=[pl.BlockSpec((CHUNK, E2), lambda n: (n, 0)),
                  pl.BlockSpec((CHUNK, E), lambda n: (n, 0)),
                  pl.BlockSpec((G, CHUNK, CHUNK), lambda n: (0, 0, 0)),
                  pl.BlockSpec((G, CHUNK, 1), lambda n: (0, 0, 0)),
                  vecE, vecE],
        out_specs=[pl.BlockSpec((CHUNK, E2), lambda n: (n, 0)),
                   pl.BlockSpec((G, CHUNK, CHUNK), lambda n: (0, 0, 0)),
                   pl.BlockSpec((CHUNK, LANES), lambda n: (0, 0)),
                   vecE, vecE],
        out_shape=[jax.ShapeDtypeStruct((T, E2), BF16),
                   jax.ShapeDtypeStruct((G, CHUNK, CHUNK), F32),
                   jax.ShapeDtypeStruct((CHUNK, LANES), F32),
                   jax.ShapeDtypeStruct((1, E), F32), jax.ShapeDtypeStruct((1, E), F32)],
        scratch_shapes=[pltpu.VMEM((CHUNK, E), F32)],
        compiler_params=_cp(("arbitrary",)),
    )(zp, dy, ws, bs.reshape(G, CHUNK, 1), lg.reshape(1, E), lb.reshape(1, E))


def _delay(h, k):
    t = lax.broadcasted_iota(jnp.int32, h.shape, 0)
    return jnp.where(t >= k, pltpu.roll(h, k, 0), 0.0)


def _advance(d, k):
    T = d.shape[0]
    t = lax.broadcasted_iota(jnp.int32, d.shape, 0)
    return jnp.where(t < T - k, pltpu.roll(d, T - k, 0), 0.0)


def _conv3(h, w_ref, b_ref):
    return w_ref[2:3, :] * h + w_ref[1:2, :] * _delay(h, 1) + w_ref[0:1, :] * _delay(h, 2) + b_ref[...]


def _convglu_fwd(hup, cw, cb, *, name):
    T, F2 = hup.shape
    F = F2 // 2
    tc = LANES
    nt = F // tc

    def body(ha_ref, hg_ref, wa_ref, wg_ref, ba_ref, bg_ref, o_ref):
        a = _conv3(ha_ref[...], wa_ref, ba_ref)
        g = _conv3(hg_ref[...], wg_ref, bg_ref)
        o_ref[...] = (_gelu(a) * g).astype(BF16)

    col = lambda off: pl.BlockSpec((T, tc), lambda j: (0, j + off))
    w3 = lambda off: pl.BlockSpec((3, tc), lambda j: (0, j + off))
    b1 = lambda off: pl.BlockSpec((1, tc), lambda j: (0, j + off))
    return pl.pallas_call(
        body, name=name, grid=(nt,),
        in_specs=[col(0), col(nt), w3(0), w3(nt), b1(0), b1(nt)],
        out_specs=pl.BlockSpec((T, tc), lambda j: (0, j)),
        out_shape=jax.ShapeDtypeStruct((T, F), BF16),
        compiler_params=_cp(("parallel",)),
    )(hup, hup, cw, cw, cb.reshape(1, F2), cb.reshape(1, F2))


def _convglu_bwd(hup, dact, cw, cb, *, name):
    T, F2 = hup.shape
    F = F2 // 2
    tc = LANES
    nt = F // tc

    def half(h, w_ref, d, dh_ref, dw_ref, db_ref, i):
        dh_ref[i] = (w_ref[2:3, :] * d + w_ref[1:2, :] * _advance(d, 1) + w_ref[0:1, :] * _advance(d, 2)).astype(BF16)
        dw_ref[i, 0:1, :] = jnp.sum(d * _delay(h, 2), axis=0, keepdims=True)
        dw_ref[i, 1:2, :] = jnp.sum(d * _delay(h, 1), axis=0, keepdims=True)
        dw_ref[i, 2:3, :] = jnp.sum(d * h, axis=0, keepdims=True)
        db_ref[i] = jnp.sum(d, axis=0, keepdims=True)

    def body(ha_ref, hg_ref, d_ref, wa_ref, wg_ref, ba_ref, bg_ref, dh_ref, dw_ref, db_ref):
        ha, hg = ha_ref[...], hg_ref[...]
        a = _conv3(ha, wa_ref, ba_ref)
        g = _conv3(hg, wg_ref, bg_ref)
        d = d_ref[...]
        half(ha, wa_ref, d * g * _gelu_grad(a), dh_ref, dw_ref, db_ref, 0)
        half(hg, wg_ref, d * _gelu(a), dh_ref, dw_ref, db_ref, 1)

    col = lambda off: pl.BlockSpec((T, tc), lambda j: (0, j + off))
    w3 = lambda off: pl.BlockSpec((3, tc), lambda j: (0, j + off))
    b1 = lambda off: pl.BlockSpec((1, tc), lambda j: (0, j + off))
    return pl.pallas_call(
        body, name=name, grid=(nt,),
        in_specs=[col(0), col(nt), col(0), w3(0), w3(nt), b1(0), b1(nt)],
        out_specs=[pl.BlockSpec((2, T, tc), lambda j: (0, 0, j)),
                   pl.BlockSpec((2, 3, tc), lambda j: (0, 0, j)),
                   pl.BlockSpec((2, 1, tc), lambda j: (0, 0, j))],
        out_shape=[jax.ShapeDtypeStruct((2, T, F), BF16),
                   jax.ShapeDtypeStruct((2, 3, F), F32),
                   jax.ShapeDtypeStruct((2, 1, F), F32)],
        compiler_params=_cp(("parallel",)),
    )(hup, hup, dact, cw, cw, cb.reshape(1, F2), cb.reshape(1, F2))


def _bucket_table():
    iq = np.arange(BLK)[:, None]
    ik = np.arange(2 * BLK)[None, :]
    delta = iq + BLK - ik
    out = []
    for win, dil in DILATED_GROUPS:
        valid = (delta >= 0) & (delta <= win // dil)
        n = (np.clip(delta, 0, None) * dil).astype(np.int32)
        max_exact = REL_BUCKETS // 2
        nf = np.maximum(n, 1).astype(np.float32)
        large = max_exact + (np.log(nf / np.float32(max_exact)) / np.float32(math.log(REL_MAX_DIST / max_exact))
                             * np.float32(REL_BUCKETS - max_exact)).astype(np.int32)
        large = np.minimum(large, REL_BUCKETS - 1)
        out.append(np.where(valid, np.where(n < max_exact, n, large), -1))
    return np.stack(out).astype(np.int32)


def _band_bias(rel_table, H):
    bkt = _bucket_table()
    out = []
    for g in range(N_GROUPS):
        tab = rel_table[:, g * H:(g + 1) * H]
        b = jnp.transpose(tab[np.clip(bkt[g], 0, None)], (2, 0, 1))
        b = jnp.where(bkt[g][None] >= 0, b, NEG)
        first = jnp.where((np.arange(2 * BLK) < BLK)[None, None, :], NEG, b)
        out.append(jnp.stack([b, first], axis=1))
    return jnp.stack(out).astype(F32)


def _attn_specs(H, T):
    nb = T // BLK

    def first(g, b):
        nbg = nb // jnp.left_shift(1, 2 * g)
        return jnp.where(b % nbg == 0, 1, 0)

    cur = pl.BlockSpec((None, None, BLK, HEAD_DIM), lambda g, h, b: (g, h, b, 0))
    prev = pl.BlockSpec((None, None, BLK, HEAD_DIM), lambda g, h, b: (g, h, jnp.maximum(b - 1, 0), 0))
    bias = pl.BlockSpec((None, None, None, BLK, 2 * BLK), lambda g, h, b: (g, h, first(g, b), 0, 0))
    return cur, prev, bias


def _scores(q, kp, kc, bias):
    scale = HEAD_DIM ** -0.5
    sp = _dot(q, kp, 1, 1) * scale + bias[:, :BLK]
    sc = _dot(q, kc, 1, 1) * scale + bias[:, BLK:]
    return sp, sc


def _attn_fwd(q, k, v, bias, *, name):
    _, H, T, _ = q.shape
    assert all(d == 4 ** g for g, (_, d) in enumerate(DILATED_GROUPS))
    cur, prev, bspec = _attn_specs(H, T)

    def body(q_ref, kp_ref, kc_ref, vp_ref, vc_ref, b_ref, o_ref, l_ref):
        sp, sc = _scores(q_ref[...], kp_ref[...], kc_ref[...], b_ref[...])
        m = jnp.maximum(jnp.max(sp, axis=-1, keepdims=True), jnp.max(sc, axis=-1, keepdims=True))
        pp, pc = jnp.exp(sp - m), jnp.exp(sc - m)
        den = jnp.sum(pp, axis=-1, keepdims=True) + jnp.sum(pc, axis=-1, keepdims=True)
        o_ref[...] = (_dot((pp / den).astype(BF16), vp_ref[...], 1, 0)
                      + _dot((pc / den).astype(BF16), vc_ref[...], 1, 0))
        l_ref[...] = jnp.broadcast_to(m + jnp.log(den), (BLK, HEAD_DIM))

    shp = jax.ShapeDtypeStruct(q.shape, F32)
    return pl.pallas_call(
        body, name=name, grid=(N_GROUPS, H, T // BLK),
        in_specs=[cur, prev, cur, prev, cur, bspec], out_specs=[cur, cur],
        out_shape=[shp, shp],
        compiler_params=_cp(("parallel", "parallel", "parallel")),
    )(q, k, k, v, v, bias)


def _attn_combine(o, lse, *, name):
    _, H, T, _ = o.shape
    tr = _tile(T, 1024, SUBLANES)
    grp = pl.BlockSpec((N_GROUPS, None, tr, HEAD_DIM), lambda h, i: (0, h, i, 0))
    one = pl.BlockSpec((None, tr, HEAD_DIM), lambda h, i: (h, i, 0))

    def body(o_ref, l_ref, oc_ref, lt_ref):
        l = l_ref[...]
        m = jnp.max(l, axis=0)
        w = jnp.exp(l - m[None])
        den = jnp.sum(w, axis=0)
        oc_ref[...] = jnp.sum((w / den[None]) * o_ref[...], axis=0)
        lt_ref[...] = m + jnp.log(den)

    shp = jax.ShapeDtypeStruct((H, T, HEAD_DIM), F32)
    return pl.pallas_call(
        body, name=name, grid=(H, T // tr),
        in_specs=[grp, grp], out_specs=[one, one], out_shape=[shp, shp],
        compiler_params=_cp(("parallel", "parallel")),
    )(o, lse)


def _attn_bwd_prep(do, oc, *, name):
    H, T, _ = do.shape
    tr = _tile(T, 1024, SUBLANES)
    one = pl.BlockSpec((None, tr, HEAD_DIM), lambda h, i: (h, i, 0))

    def body(do_ref, oc_ref, dl_ref, dob_ref):
        d = do_ref[...]
        dl_ref[...] = jnp.broadcast_to(jnp.sum(d * oc_ref[...], axis=-1, keepdims=True), d.shape)
        dob_ref[...] = d.astype(BF16)

    return pl.pallas_call(
        body, name=name, grid=(H, T // tr),
        in_specs=[one, one], out_specs=[one, one],
        out_shape=[jax.ShapeDtypeStruct(do.shape, F32), jax.ShapeDtypeStruct(do.shape, BF16)],
        compiler_params=_cp(("parallel", "parallel")),
    )(do, oc)


def _attn_bwd(q, k, v, bias, do, lse, delta, *, name):
    _, H, T, _ = q.shape
    cur, prev, bspec = _attn_specs(H, T)
    scale = HEAD_DIM ** -0.5

    def body(q_ref, kp_ref, kc_ref, vp_ref, vc_ref, b_ref, do_ref, l_ref, dl_ref,
             dq_ref, dkc_ref, dkp_ref, dvc_ref, dvp_ref, db_ref):
        @pl.when(pl.program_id(2) == 0)
        def _():
            db_ref[...] = jnp.zeros_like(db_ref)

        qv, kp, kc, vp, vc, dov = q_ref[...], kp_ref[...], kc_ref[...], vp_ref[...], vc_ref[...], do_ref[...]
        sp, sc = _scores(qv, kp, kc, b_ref[...])
        l = l_ref[:, 0:1]
        dl = dl_ref[:, 0:1]
        pp, pc = jnp.exp(sp - l), jnp.exp(sc - l)
        dsp = pp * (_dot(dov, vp, 1, 1) - dl)
        dsc = pc * (_dot(dov, vc, 1, 1) - dl)
        db_ref[:, :BLK] += dsp
        db_ref[:, BLK:] += dsc
        dspb, dscb = dsp.astype(BF16), dsc.astype(BF16)
        dq_ref[...] = (scale * (_dot(dspb, kp, 1, 0) + _dot(dscb, kc, 1, 0))).astype(BF16)
        dkp_ref[...] = scale * _dot(dspb, qv, 0, 0)
        dkc_ref[...] = scale * _dot(dscb, qv, 0, 0)
        dvp_ref[...] = _dot(pp.astype(BF16), dov, 0, 0)
        dvc_ref[...] = _dot(pc.astype(BF16), dov, 0, 0)

    f32 = jax.ShapeDtypeStruct(q.shape, F32)
    return pl.pallas_call(
        body, name=name, grid=(N_GROUPS, H, T // BLK),
        in_specs=[cur, prev, cur, prev, cur, bspec, cur, cur, cur],
        out_specs=[cur, cur, cur, cur, cur,
                   pl.BlockSpec((None, None, BLK, 2 * BLK), lambda g, h, b: (g, h, 0, 0))],
        out_shape=[jax.ShapeDtypeStruct(q.shape, BF16), f32, f32, f32, f32,
                   jax.ShapeDtypeStruct((N_GROUPS, H, BLK, 2 * BLK), F32)],
        compiler_params=_cp(("parallel", "parallel", "arbitrary")),
    )(q, k, k, v, v, bias, do, lse, delta)


def _attn_kv_merge(dkc, dkp, dvc, dvp, *, name):
    _, H, T, _ = dkc.shape
    nb = T // BLK
    cur = pl.BlockSpec((None, None, BLK, HEAD_DIM), lambda g, h, b: (g, h, b, 0))
    nxt = pl.BlockSpec((None, None, BLK, HEAD_DIM), lambda g, h, b: (g, h, jnp.minimum(b + 1, nb - 1), 0))

    def body(kc_ref, kp_ref, vc_ref, vp_ref, dk_ref, dv_ref):
        g, b = pl.program_id(0), pl.program_id(2)
        nbg = nb // jnp.left_shift(1, 2 * g)
        has_next = (b + 1) % nbg != 0
        dk_ref[...] = kc_ref[...] + jnp.where(has_next, kp_ref[...], 0.0)
        dv_ref[...] = vc_ref[...] + jnp.where(has_next, vp_ref[...], 0.0)

    shp = jax.ShapeDtypeStruct(dkc.shape, F32)
    return pl.pallas_call(
        body, name=name, grid=(N_GROUPS, H, nb),
        in_specs=[cur, nxt, cur, nxt], out_specs=[cur, cur], out_shape=[shp, shp],
        compiler_params=_cp(("parallel", "parallel", "parallel")),
    )(dkc, dkp, dvc, dvp)


def _rel_grad(dbs, *, name):
    H = dbs[0].shape[1]
    n = len(dbs)
    assert H <= LANES
    bkt = jnp.asarray(_bucket_table())

    def body(*refs):
        bk = refs[n][...]
        o_ref = refs[n + 1]
        db = refs[0][...]
        for r in refs[1:n]:
            db = db + r[...]
        row = lax.broadcasted_iota(jnp.int32, (REL_BUCKETS, LANES), 0)
        col = lax.broadcasted_iota(jnp.int32, (REL_BUCKETS, LANES), 1)

        def bucket(c, acc):
            sel = bk == c
            for h in range(H):
                val = jnp.sum(jnp.where(sel, db[h], 0.0))
                acc = jnp.where((row == c) & (col == h), val, acc)
            return acc

        o_ref[...] = lax.fori_loop(0, REL_BUCKETS, bucket, jnp.zeros((REL_BUCKETS, LANES), F32))

    dspec = pl.BlockSpec((None, H, BLK, 2 * BLK), lambda g: (g, 0, 0, 0))
    return pl.pallas_call(
        body, name=name, grid=(N_GROUPS,),
        in_specs=[dspec] * n + [pl.BlockSpec((None, BLK, 2 * BLK), lambda g: (g, 0, 0))],
        out_specs=pl.BlockSpec((None, REL_BUCKETS, LANES), lambda g: (g, 0, 0)),
        out_shape=jax.ShapeDtypeStruct((N_GROUPS, REL_BUCKETS, LANES), F32),
        compiler_params=_cp(("parallel",)),
    )(*dbs, bkt)


def _tok_to_group(a, d):
    T, H, Dh = a.shape
    return a.reshape(T // d, d, H, Dh).transpose(2, 1, 0, 3).reshape(H, T, Dh)


def _group_to_tok(a, d):
    H, T, Dh = a.shape
    return a.reshape(H, d, T // d, Dh).transpose(2, 1, 0, 3).reshape(T, H, Dh)


def _hm_to_group(a, d):
    H, T, Dh = a.shape
    return a.reshape(H, T // d, d, Dh).transpose(0, 2, 1, 3).reshape(H, T, Dh)


def _group_to_hm(a, d):
    H, T, Dh = a.shape
    return a.reshape(H, d, T // d, Dh).transpose(0, 2, 1, 3).reshape(H, T, Dh)


def _rows_view(a):
    a2 = a.reshape(-1, a.shape[-1])
    R, C = a2.shape
    tr = _tile(R, max(SUBLANES, (1 << 18) // C), SUBLANES)
    return a2, R, C, tr


def _addn(xs, out_dtype, *, name):
    shape = xs[0].shape
    x2s = [_rows_view(x)[0] for x in xs]
    _, R, C, tr = _rows_view(xs[0])
    spec = pl.BlockSpec((tr, C), lambda i: (i, 0))

    def body(*refs):
        acc = refs[0][...].astype(F32)
        for r in refs[1:-1]:
            acc = acc + r[...].astype(F32)
        refs[-1][...] = acc.astype(out_dtype)

    out = pl.pallas_call(
        body, name=name, grid=(R // tr,),
        in_specs=[spec] * len(xs), out_specs=spec,
        out_shape=jax.ShapeDtypeStruct((R, C), out_dtype),
        compiler_params=_cp(("parallel",)),
    )(*x2s)
    return out.reshape(shape)


def _sum_slots(buf, *, name):
    n = buf.shape[0]
    shape = buf.shape[1:]
    b3 = buf.reshape(n, -1, buf.shape[-1])
    _, R, C = b3.shape
    tr = _tile(R, max(2 * SUBLANES, (1 << 17) // C), 2 * SUBLANES)

    def body(b_ref, o_ref):
        acc = b_ref[0].astype(F32)
        for s in range(1, n):
            acc = acc + b_ref[s].astype(F32)
        o_ref[...] = acc

    out = pl.pallas_call(
        body, name=name, grid=(R // tr,),
        in_specs=[pl.BlockSpec((n, tr, C), lambda i: (0, i, 0))],
        out_specs=pl.BlockSpec((tr, C), lambda i: (i, 0)),
        out_shape=jax.ShapeDtypeStruct((R, C), F32),
        compiler_params=_cp(("parallel",)),
    )(b3)
    return out.reshape(shape)


def _loss_grad(y, tgt, *, name):
    T, D = y.shape
    tr = _tile(T, 512, SUBLANES)
    row = pl.BlockSpec((tr, D), lambda i: (i, 0))

    def body(y_ref, t_ref, dy_ref, l_ref, acc_ref):
        i = pl.program_id(0)

        @pl.when(i == 0)
        def _():
            acc_ref[...] = jnp.zeros_like(acc_ref)

        err = y_ref[...] - t_ref[...]
        dy_ref[...] = err * (1.0 / D)
        acc_ref[...] += jnp.sum(err * err, axis=0, keepdims=True)

        @pl.when(i == T // tr - 1)
        def _():
            tot = jnp.sum(acc_ref[...], axis=1, keepdims=True) * (0.5 / D)
            l_ref[...] = jnp.broadcast_to(tot, (1, LANES))

    return pl.pallas_call(
        body, name=name, grid=(T // tr,),
        in_specs=[row, row],
        out_specs=[row, pl.BlockSpec((1, LANES), lambda i: (0, 0))],
        out_shape=[jax.ShapeDtypeStruct((T, D), F32), jax.ShapeDtypeStruct((1, LANES), F32)],
        scratch_shapes=[pltpu.VMEM((1, D), F32)],
        compiler_params=_cp(("arbitrary",)),
    )(y, tgt)


def _adamw(w, g, m, v, *, name):
    shape = w.shape
    w2, R, C, tr = _rows_view(w)
    spec = pl.BlockSpec((tr, C), lambda i: (i, 0))

    def body(w_ref, g_ref, m_ref, v_ref, d_ref, nm_ref, nv_ref):
        gv = g_ref[...]
        nm = ADAM_B1 * m_ref[...] + (1.0 - ADAM_B1) * gv
        nv = ADAM_B2 * v_ref[...] + (1.0 - ADAM_B2) * (gv * gv)
        m_hat = nm / (1.0 - ADAM_B1 ** ADAM_STEP)
        v_hat = nv / (1.0 - ADAM_B2 ** ADAM_STEP)
        d_ref[...] = -ADAM_LR * (m_hat / (jnp.sqrt(v_hat) + ADAM_EPS) + ADAM_WD * w_ref[...])
        nm_ref[...] = nm
        nv_ref[...] = nv

    shp = jax.ShapeDtypeStruct((R, C), F32)
    outs = pl.pallas_call(
        body, name=name, grid=(R // tr,),
        in_specs=[spec] * 4, out_specs=[spec] * 3, out_shape=[shp] * 3,
        compiler_params=_cp(("parallel",)),
    )(w2, g.reshape(R, C), m.reshape(R, C), v.reshape(R, C))
    return tuple(o.reshape(shape) for o in outs)


def _pack(arrs):
    flat = jnp.concatenate([a.reshape(-1).astype(F32) for a in arrs])
    unit = SUBLANES * LANES
    pad = (-flat.shape[0]) % unit
    return jnp.pad(flat, (0, pad)).reshape(-1, LANES)


def _unpack(buf, shapes):
    flat = buf.reshape(-1)
    out, off = [], 0
    for s in shapes:
        n = int(np.prod(s))
        out.append(flat[off:off + n].reshape(s))
        off += n
    return out


def _me():
    return lax.axis_index("x"), lax.axis_index("y"), lax.axis_index("c")


def _flip(pos, k):
    x, y, c = pos
    return (1 - x if k & 4 else x, 1 - y if k & 2 else y, 1 - c if k & 1 else c)


HBM = pl.BlockSpec(memory_space=pltpu.HBM)

COL, ROW = -1, -2


def _shard_window(ref, axis, j, n):
    start = pl.multiple_of(j * n, n)
    if axis == COL:
        return ref.at[:, :, pl.ds(start, n)]
    return ref.at[:, pl.ds(start, n), :]


def _gather_weights(shards, axes, small, *, name):
    n = len(shards)
    full_shapes = []
    for s, ax in zip(shards, axes):
        shp = list(s.shape)
        shp[ax] *= N_CHIPS
        full_shapes.append(jax.ShapeDtypeStruct(tuple(shp), s.dtype))
    full_shapes.append(jax.ShapeDtypeStruct((N_CHIPS,) + small.shape, small.dtype))
    n_t = n + 1
    peers = (2, 4, 6)

    def body(*refs):
        srcs, dsts = refs[:n_t], refs[n_t:2 * n_t]
        send_sems, recv_sems, local_sems = refs[2 * n_t:]
        me = _me()
        j = 2 * me[0] + me[1]

        def window(i, ref, chip):
            if i == n:
                return ref.at[chip]
            return _shard_window(ref, axes[i], chip, srcs[i].shape[axes[i]])

        local = [pltpu.make_async_copy(srcs[i], window(i, dsts[i], j), local_sems.at[i]) for i in range(n_t)]
        for cp in local:
            cp.start()
        sends = []
        for pi, k in enumerate(peers):
            for i in range(n_t):
                sends.append(pltpu.make_async_remote_copy(
                    src_ref=srcs[i], dst_ref=window(i, dsts[i], j),
                    send_sem=send_sems.at[pi * n_t + i], recv_sem=recv_sems.at[pi * n_t + i],
                    device_id=_flip(me, k), device_id_type=MESH))
        for cp in sends:
            cp.start()
        for pi, k in enumerate(peers):
            px, py, _ = _flip(me, k)
            for i in range(n_t):
                pltpu.make_async_remote_copy(
                    src_ref=srcs[i], dst_ref=window(i, dsts[i], 2 * px + py),
                    send_sem=send_sems.at[pi * n_t + i], recv_sem=recv_sems.at[pi * n_t + i],
                    device_id=_flip(me, k), device_id_type=MESH).wait_recv()
        for cp in sends:
            cp.wait_send()
        for cp in local:
            cp.wait()

    return pl.pallas_call(
        body, name=name,
        in_specs=[HBM] * n_t, out_specs=[HBM] * n_t, out_shape=full_shapes,
        scratch_shapes=[pltpu.SemaphoreType.DMA((len(peers) * n_t,)),
                        pltpu.SemaphoreType.DMA((len(peers) * n_t,)),
                        pltpu.SemaphoreType.DMA((n_t,))],
    )(*shards, small)


def _scatter_grads(grads, axes, small, *, name):
    n = len(grads)
    part_shapes = []
    for gr, ax in zip(grads, axes):
        L, R, C = gr.shape
        if ax == COL:
            C //= N_CHIPS
        else:
            R //= N_CHIPS
        if L == 1:
            R //= 2
        part_shapes.append((max(L // 2, 1), R, C))
    out_shapes = [jax.ShapeDtypeStruct((N_DEV,) + s, gr.dtype) for s, gr in zip(part_shapes, grads)]
    out_shapes.append(jax.ShapeDtypeStruct((N_DEV,) + small.shape, small.dtype))
    n_t = n + 1

    def body(*refs):
        srcs, dsts = refs[:n_t], refs[n_t:2 * n_t]
        send_sems, recv_sems, local_sems = refs[2 * n_t:]
        me = _me()
        my_id = 4 * me[0] + 2 * me[1] + me[2]

        def part(i, pos):
            if i == n:
                return srcs[i]
            px, py, pc = pos
            L = srcs[i].shape[0]
            Lp, Rp, Cp = part_shapes[i]
            chip = 2 * px + py
            if L == 1:
                assert axes[i] == ROW
                return srcs[i].at[:, pl.ds(pl.multiple_of((2 * chip + pc) * Rp, Rp), Rp), :]
            lay = pl.ds(pc * Lp, Lp)
            if axes[i] == COL:
                return srcs[i].at[lay, :, pl.ds(pl.multiple_of(chip * Cp, Cp), Cp)]
            return srcs[i].at[lay, pl.ds(pl.multiple_of(chip * Rp, Rp), Rp), :]

        local = [pltpu.make_async_copy(part(i, me), dsts[i].at[my_id], local_sems.at[i]) for i in range(n_t)]
        for cp in local:
            cp.start()
        sends = []
        for k in range(1, N_DEV):
            peer = _flip(me, k)
            for i in range(n_t):
                sends.append(pltpu.make_async_remote_copy(
                    src_ref=part(i, peer), dst_ref=dsts[i].at[my_id],
                    send_sem=send_sems.at[(k - 1) * n_t + i], recv_sem=recv_sems.at[(k - 1) * n_t + i],
                    device_id=peer, device_id_type=MESH))
        for cp in sends:
            cp.start()
        for k in range(1, N_DEV):
            px, py, pc = _flip(me, k)
            for i in range(n_t):
                pltpu.make_async_remote_copy(
                    src_ref=part(i, me), dst_ref=dsts[i].at[4 * px + 2 * py + pc],
                    send_sem=send_sems.at[(k - 1) * n_t + i], recv_sem=recv_sems.at[(k - 1) * n_t + i],
                    device_id=(px, py, pc), device_id_type=MESH).wait_recv()
        for cp in sends:
            cp.wait_send()
        for cp in local:
            cp.wait()

    n_sem = (N_DEV - 1) * n_t
    return pl.pallas_call(
        body, name=name,
        in_specs=[HBM] * n_t, out_specs=[HBM] * n_t, out_shape=out_shapes,
        scratch_shapes=[pltpu.SemaphoreType.DMA((n_sem,)), pltpu.SemaphoreType.DMA((n_sem,)),
                        pltpu.SemaphoreType.DMA((n_t,))],
    )(*grads, small)


def _pair_gather(parts, *, name):
    n = len(parts)
    out_shapes = [jax.ShapeDtypeStruct((2,) + p.shape, p.dtype) for p in parts]

    def body(*refs):
        srcs, dsts = refs[:n], refs[n:2 * n]
        send_sems, recv_sems, local_sems = refs[2 * n:]
        me = _me()
        c = me[2]
        sib = _flip(me, 1)
        local = [pltpu.make_async_copy(srcs[i], dsts[i].at[c], local_sems.at[i]) for i in range(n)]
        for cp in local:
            cp.start()
        sends = [pltpu.make_async_remote_copy(
            src_ref=srcs[i], dst_ref=dsts[i].at[c], send_sem=send_sems.at[i], recv_sem=recv_sems.at[i],
            device_id=sib, device_id_type=MESH) for i in range(n)]
        for cp in sends:
            cp.start()
        for i in range(n):
            pltpu.make_async_remote_copy(
                src_ref=srcs[i], dst_ref=dsts[i].at[1 - c], send_sem=send_sems.at[i], recv_sem=recv_sems.at[i],
                device_id=sib, device_id_type=MESH).wait_recv()
        for cp in sends:
            cp.wait_send()
        for cp in local:
            cp.wait()

    return pl.pallas_call(
        body, name=name,
        in_specs=[HBM] * n, out_specs=[HBM] * n, out_shape=out_shapes,
        scratch_shapes=[pltpu.SemaphoreType.DMA((n,)), pltpu.SemaphoreType.DMA((n,)),
                        pltpu.SemaphoreType.DMA((n,))],
    )(*parts)


def _local_step(x, tgt, W, P):
    T, D = x.shape
    depth = P["ln_g"].shape[0]
    n_a = W["a_w_in"].shape[0]
    alpha = (2 * depth) ** 0.25
    H = W["kv_w"].shape[1] // (2 * HEAD_DIM)
    HD = H * HEAD_DIM
    dils = [d for _, d in DILATED_GROUPS]
    assert T % (dils[-1] * BLK) == 0

    bias = _band_bias(P["rel_table"], H)
    saved = []
    xf, xb = x, x.astype(BF16)
    kg = vg = None
    for i in range(depth):
        s = {"x": xf, "xb": xb}
        if i < n_a:
            s["zp"] = _mm(xb, W["a_w_in"], b_lead=(i,), name=f"a{i}_in")
            s["y"] = _sgu_fwd(s["zp"], P["a_w_s"][i], P["a_b_s"][i], P["a_ln_g"][i], P["a_ln_b"][i], name=f"a{i}_sgu")
            s["h"] = _mm(s["y"], W["a_w_out"], b_lead=(i,), name=f"a{i}_out")
        else:
            j = i - n_a
            if j == 0:
                kv = _mm(xb, W["kv_w"], out_dtype=BF16, name="kv_proj")
                k_tok = kv[:, :HD].reshape(T, H, HEAD_DIM)
                v_tok = kv[:, HD:].reshape(T, H, HEAD_DIM)
                kg = jnp.stack([_tok_to_group(k_tok, d) for d in dils])
                vg = jnp.stack([_tok_to_group(v_tok, d) for d in dils])
            q = _mm(xb, W["b_w_q"], b_lead=(j,), out_dtype=BF16, name=f"b{j}_q").reshape(T, N_GROUPS, H, HEAD_DIM)
            s["qg"] = jnp.stack([_tok_to_group(q[:, g], d) for g, d in enumerate(dils)])
            og, lg = _attn_fwd(s["qg"], kg, vg, bias, name=f"b{j}_attn")
            o_hm = jnp.stack([_group_to_hm(og[g], d) for g, d in enumerate(dils)])
            l_hm = jnp.stack([_group_to_hm(lg[g], d) for g, d in enumerate(dils)])
            s["oc"], s["lse"] = _attn_combine(o_hm, l_hm, name=f"b{j}_comb")
            s["ob"] = s["oc"].transpose(1, 0, 2).reshape(T, HD).astype(BF16)
            s["h"] = _mm(s["ob"], W["b_w_o"], b_lead=(j,), name=f"b{j}_o")
        s["x1"], s["x1b"] = _add_ln(xf, s["h"], P["ln_g"][i, 0], P["ln_b"][i, 0], alpha, name=f"l{i}_ln1")
        s["hup"] = _mm(s["x1b"], W["ffn_w_up"], b_lead=(i,), name=f"l{i}_up")
        s["act"] = _convglu_fwd(s["hup"], P["ffn_conv_w"][i], P["ffn_conv_b"][i], name=f"l{i}_glu")
        s["f"] = _mm(s["act"], W["ffn_w_down"], b_lead=(i,), name=f"l{i}_down")
        xf, xb = _add_ln(s["x1"], s["f"], P["ln_g"][i, 1], P["ln_b"][i, 1], alpha, name=f"l{i}_ln2")
        saved.append(s)

    G, loss_row = _loss_grad(xf, tgt, name="loss")

    gw = {}

    def dw(key, layer, a, b, name, **kw):
        n_layers = W[key].shape[0] if W[key].ndim == 3 else 1
        gw[key] = _mm(a, b, ta=True, out_dtype=BF16, out_into=(gw.get(key), n_layers, layer), name=name, **kw)

    gp = {k: [None] * n_a for k in ("a_ln_g", "a_ln_b", "a_w_s", "a_b_s")}
    gp.update({k: [None] * depth for k in ("ffn_conv_w", "ffn_conv_b", "ln_g", "ln_b")})
    dk_parts, dv_parts, dbias_parts = [], [], []
    for i in reversed(range(depth)):
        s = saved[i]
        dr2, dr2b, dg2, db2 = _ln_bwd(G, s["x1"], s["f"], P["ln_g"][i, 1], alpha, name=f"l{i}_ln2_bwd")
        dw("ffn_w_down", i, s["act"], dr2b, f"l{i}_down_dw")
        dact = _mm(dr2b, W["ffn_w_down"], tb=True, b_lead=(i,), name=f"l{i}_down_dx")
        dhup, dcw, dcb = _convglu_bwd(s["hup"], dact, P["ffn_conv_w"][i], P["ffn_conv_b"][i], name=f"l{i}_glu_bwd")
        gp["ffn_conv_w"][i] = dcw.transpose(1, 0, 2).reshape(dcw.shape[1], -1)
        gp["ffn_conv_b"][i] = dcb.reshape(-1)
        dw("ffn_w_up", i, s["x1b"], dhup, f"l{i}_up_dw", b_halves=True)
        G1 = _mm(dhup, W["ffn_w_up"], tb=True, a_halves=True, b_lead=(i,), add=dr2, add_scale=alpha, name=f"l{i}_up_dx")
        dr1, dr1b, dg1, db1 = _ln_bwd(G1, s["x"], s["h"], P["ln_g"][i, 0], alpha, name=f"l{i}_ln1_bwd")
        gp["ln_g"][i] = jnp.concatenate([dg1, dg2], axis=0)
        gp["ln_b"][i] = jnp.concatenate([db1, db2], axis=0)
        if i < n_a:
            dw("a_w_out", i, s["y"], dr1b, f"a{i}_out_dw")
            dy = _mm(dr1b, W["a_w_out"], tb=True, b_lead=(i,), name=f"a{i}_out_dx")
            dzp, dws, dbs, dlg, dlb = _sgu_bwd(s["zp"], dy, P["a_w_s"][i], P["a_b_s"][i], P["a_ln_g"][i],
                                               P["a_ln_b"][i], name=f"a{i}_sgu_bwd")
            gp["a_w_s"][i], gp["a_b_s"][i] = dws, dbs[:, :dws.shape[0]].T
            gp["a_ln_g"][i], gp["a_ln_b"][i] = dlg[0], dlb[0]
            dw("a_w_in", i, s["xb"], dzp, f"a{i}_in_dw")
            G = _mm(dzp, W["a_w_in"], tb=True, b_lead=(i,), add=dr1, add_scale=alpha, name=f"a{i}_in_dx")
        else:
            j = i - n_a
            dw("b_w_o", j, s["ob"], dr1b, f"b{j}_o_dw")
            do_tok = _mm(dr1b, W["b_w_o"], tb=True, b_lead=(j,), name=f"b{j}_o_dx")
            do_hm = do_tok.reshape(T, H, HEAD_DIM).transpose(1, 0, 2)
            delta, dob = _attn_bwd_prep(do_hm, s["oc"], name=f"b{j}_prep")
            to_groups = lambda a: jnp.stack([_hm_to_group(a, d) for d in dils])
            dq, dkc, dkp, dvc, dvp, dbias = _attn_bwd(s["qg"], kg, vg, bias, to_groups(dob), to_groups(s["lse"]),
                                                      to_groups(delta), name=f"b{j}_attn_bwd")
            dbias_parts.append(dbias)
            dk, dv = _attn_kv_merge(dkc, dkp, dvc, dvp, name=f"b{j}_kv_merge")
            dk_parts += [_group_to_tok(dk[g], d).reshape(T, HD) for g, d in enumerate(dils)]
            dv_parts += [_group_to_tok(dv[g], d).reshape(T, HD) for g, d in enumerate(dils)]
            dq_tok = jnp.stack([_group_to_tok(dq[g], d) for g, d in enumerate(dils)], axis=1).reshape(T, N_GROUPS * HD)
            dw("b_w_q", j, s["xb"], dq_tok, f"b{j}_q_dw")
            G = _mm(dq_tok, W["b_w_q"], tb=True, b_lead=(j,), add=dr1, add_scale=alpha, name=f"b{j}_q_dx")
            if j == 0:
                dkv = jnp.concatenate([_addn(dk_parts, BF16, name="dk_sum"), _addn(dv_parts, BF16, name="dv_sum")], axis=1)
                dw("kv_w", 0, s["xb"], dkv, "kv_dw")
                G = _mm(dkv, W["kv_w"], tb=True, add=G, add_scale=1.0, name="kv_dx")
    rel = _rel_grad(dbias_parts, name="rel_grad")
    grel = rel[:, :, :H].transpose(1, 0, 2).reshape(REL_BUCKETS, N_GROUPS * H)
    gsmall = {k: jnp.stack(v) for k, v in gp.items()}
    gsmall["rel_table"] = grel
    return loss_row, G, gw, gsmall


BIG = (("a_w_in", COL), ("a_w_out", ROW), ("kv_w", ROW), ("b_w_q", COL), ("b_w_o", COL),
       ("ffn_w_up", COL), ("ffn_w_down", ROW))
SMALL_SHARDED = ("a_ln_g", "a_ln_b", "ffn_conv_w", "ln_g", "ln_b")
SMALL_REPLICATED = ("a_w_s", "a_b_s", "rel_table", "ffn_conv_b")
WEIGHTS = ("a_w_in", "a_ln_g", "a_ln_b", "a_w_s", "a_b_s", "a_w_out", "kv_w", "b_w_q", "b_w_o", "rel_table",
           "ffn_w_up", "ffn_conv_w", "ffn_conv_b", "ffn_w_down", "ln_g", "ln_b")


def kernel(x, a_w_in, a_ln_g, a_ln_b, a_w_s, a_b_s, a_w_out, kv_w, b_w_q, b_w_o, rel_table, ffn_w_up, ffn_conv_w, ffn_conv_b, ffn_w_down, ln_g, ln_b, loss_target, m_a_w_in, m_a_ln_g, m_a_ln_b, m_a_w_s, m_a_b_s, m_a_w_out, m_kv_w, m_b_w_q, m_b_w_o, m_rel_table, m_ffn_w_up, m_ffn_conv_w, m_ffn_conv_b, m_ffn_w_down, m_ln_g, m_ln_b, v_a_w_in, v_a_ln_g, v_a_ln_b, v_a_w_s, v_a_b_s, v_a_w_out, v_kv_w, v_b_w_q, v_b_w_o, v_rel_table, v_ffn_w_up, v_ffn_conv_w, v_ffn_conv_b, v_ffn_w_down, v_ln_g, v_ln_b):
    w = dict(a_w_in=a_w_in, a_ln_g=a_ln_g, a_ln_b=a_ln_b, a_w_s=a_w_s, a_b_s=a_b_s, a_w_out=a_w_out, kv_w=kv_w,
             b_w_q=b_w_q, b_w_o=b_w_o, rel_table=rel_table, ffn_w_up=ffn_w_up, ffn_conv_w=ffn_conv_w,
             ffn_conv_b=ffn_conv_b, ffn_w_down=ffn_w_down, ln_g=ln_g, ln_b=ln_b)
    m = dict(a_w_in=m_a_w_in, a_ln_g=m_a_ln_g, a_ln_b=m_a_ln_b, a_w_s=m_a_w_s, a_b_s=m_a_b_s, a_w_out=m_a_w_out,
             kv_w=m_kv_w, b_w_q=m_b_w_q, b_w_o=m_b_w_o, rel_table=m_rel_table, ffn_w_up=m_ffn_w_up,
             ffn_conv_w=m_ffn_conv_w, ffn_conv_b=m_ffn_conv_b, ffn_w_down=m_ffn_w_down, ln_g=m_ln_g, ln_b=m_ln_b)
    v = dict(a_w_in=v_a_w_in, a_ln_g=v_a_ln_g, a_ln_b=v_a_ln_b, a_w_s=v_a_w_s, a_b_s=v_a_b_s, a_w_out=v_a_w_out,
             kv_w=v_kv_w, b_w_q=v_b_w_q, b_w_o=v_b_w_o, rel_table=v_rel_table, ffn_w_up=v_ffn_w_up,
             ffn_conv_w=v_ffn_conv_w, ffn_conv_b=v_ffn_conv_b, ffn_w_down=v_ffn_w_down, ln_g=v_ln_g, ln_b=v_ln_b)
    chip = 2 * lax.axis_index("x") + lax.axis_index("y")

    big_names = [n for n, _ in BIG]
    big_axes = [a for _, a in BIG]
    shards = [w[n].astype(BF16) for n in big_names]
    shards = [s.reshape((1,) + s.shape) if s.ndim == 2 else s for s in shards]
    small_shapes = [w[n].shape for n in SMALL_SHARDED]
    gathered = _gather_weights(shards, big_axes, _pack([w[n] for n in SMALL_SHARDED]), name="gather_weights")
    W = dict(zip(big_names, gathered[:-1]))
    W["kv_w"] = W["kv_w"].reshape(W["kv_w"].shape[1:])
    P = {n: w[n] for n in SMALL_REPLICATED}
    per_chip = [_unpack(gathered[-1][j], small_shapes) for j in range(N_CHIPS)]
    for i, n in enumerate(SMALL_SHARDED):
        P[n] = jnp.concatenate([per_chip[j][i] for j in range(N_CHIPS)], axis=-1)

    loss_row, grad_x, gw, gsmall = _local_step(x[0], loss_target[0], W, P)
    loss = lax.psum(loss_row[0, 0], ("x", "y", "c"))

    small_names = list(SMALL_SHARDED) + list(SMALL_REPLICATED)
    small_pack = _pack([gsmall[n] for n in small_names])
    slots = _scatter_grads([gw[n] for n in big_names], big_axes, small_pack, name="scatter_grads")
    sums = [_sum_slots(b, name=f"sum_{n}") for n, b in zip(big_names, slots[:-1])]
    small_sum = _sum_slots(slots[-1], name="sum_small")
    pairs = _pair_gather(sums, name="pair_gather")
    grad = {n: p.reshape(w[n].shape) for n, p in zip(big_names, pairs)}
    full_small = _unpack(small_sum, [gsmall[n].shape for n in small_names])
    for n, gfull in zip(small_names, full_small):
        if n in SMALL_SHARDED:
            width = w[n].shape[-1]
            grad[n] = lax.dynamic_slice_in_dim(gfull, chip * width, width, axis=-1)
        else:
            grad[n] = gfull

    delta, new_m, new_v = {}, {}, {}
    for n, _ in BIG:
        delta[n], new_m[n], new_v[n] = _adamw(w[n], grad[n], m[n], v[n], name=f"adamw_{n}")
    shapes = [w[n].shape for n in small_names]
    packed = _adamw(_pack([w[n] for n in small_names]), _pack([grad[n] for n in small_names]),
                    _pack([m[n] for n in small_names]), _pack([v[n] for n in small_names]), name="adamw_small")
    for out, res in zip((delta, new_m, new_v), packed):
        for n, a in zip(small_names, _unpack(res, shapes)):
            out[n] = a

    return (loss, grad_x[None], *[grad[n] for n in WEIGHTS], *[delta[n] for n in WEIGHTS],
            *[new_m[n] for n in WEIGHTS], *[new_v[n] for n in WEIGHTS])
```

```python
import functools
import math

import numpy as np
import jax
import jax.numpy as jnp
from jax import lax
from jax.experimental import pallas as pl
from jax.experimental.pallas import tpu as pltpu

F32 = jnp.float32
BF16 = jnp.bfloat16
MESH = pl.DeviceIdType.MESH

CHUNK = 128
HEAD_DIM = 64
DILATED_GROUPS = ((128, 1), (512, 4), (2048, 16))
N_GROUPS = len(DILATED_GROUPS)
BLK = 128
REL_BUCKETS = 32
REL_MAX_DIST = 2048
LN_EPS = 1e-5
NEG = -1e30
ADAM_LR = 0.001
ADAM_B1 = 0.9
ADAM_B2 = 0.999
ADAM_EPS = 1e-08
ADAM_WD = 0.01
ADAM_STEP = 10

N_CHIPS = 4
N_DEV = 8
LANES = 128
SUBLANES = 8
VMEM_LIMIT = 48 * 1024 * 1024

_SQRT_HALF = 0.7071067811865476
_INV_SQRT_2PI = 0.3989422804014327


def _cp(sem=None, vmem=VMEM_LIMIT):
    return pltpu.CompilerParams(dimension_semantics=sem, vmem_limit_bytes=vmem)


def _tile(dim, target, align=LANES):
    if dim <= target:
        return dim
    t = (target // align) * align
    while t >= align:
        if dim % t == 0:
            return t
        t -= align
    return dim


def _gelu(x):
    return 0.5 * x * (1.0 + lax.erf(x * _SQRT_HALF))


def _gelu_grad(x):
    return 0.5 * (1.0 + lax.erf(x * _SQRT_HALF)) + x * (_INV_SQRT_2PI * jnp.exp(-0.5 * x * x))


def _dot(a, b, ca, cb):
    return lax.dot_general(a, b, (((ca,), (cb,)), ((), ())), preferred_element_type=F32)


def _mn_tile(dim):
    return max(_tile(dim, 1024), _tile(dim, 1408))


def _mm(a, b, *, name, ta=False, tb=False, out_dtype=F32, tiles=None,
        a_lead=(), b_lead=(), a_halves=False, b_halves=False, add=None, add_scale=1.0, out_into=None):
    a2, b2 = a.shape[-2:], b.shape[-2:]
    M, K = (a2[1], a2[0]) if ta else a2
    Kb, N = (b2[1], b2[0]) if tb else b2
    if a_halves:
        K = 2 * K
    if b_halves:
        N = 2 * N
    assert K == Kb, (name, a.shape, b.shape)
    if tiles is None:
        k_target = 1024 if ta else (1408 if a_halves else (K if K <= 2816 else 2048))
        tm, tn = _mn_tile(M), _mn_tile(N // 2 if b_halves else N)
        tk = _tile(K // 2 if a_halves else K, k_target)
    else:
        tm = _tile(M, tiles[0])
        tn = _tile(N // 2 if b_halves else N, tiles[1])
        tk = _tile(K // 2 if a_halves else K, tiles[2])
    nm, nn, nk = M // tm, N // tn, K // tk
    nkh, nnh = nk // 2, nn // 2

    def lead(idx, rest):
        return tuple(idx) + tuple(rest)

    sq_a = (None,) * (a.ndim - 2)
    sq_b = (None,) * (b.ndim - 2)
    if a_halves:
        assert not ta and a.ndim == 3
        a_spec = pl.BlockSpec((None, tm, tk), lambda i, j, k: (k // nkh, i, k % nkh))
    elif ta:
        a_spec = pl.BlockSpec(sq_a + (tk, tm), lambda i, j, k: lead(a_lead, (k, i)))
    else:
        a_spec = pl.BlockSpec(sq_a + (tm, tk), lambda i, j, k: lead(a_lead, (i, k)))
    if b_halves:
        assert not tb and b.ndim == 3
        b_spec = pl.BlockSpec((None, tk, tn), lambda i, j, k: (j // nnh, k, j % nnh))
    elif tb:
        b_spec = pl.BlockSpec(sq_b + (tn, tk), lambda i, j, k: lead(b_lead, (j, k)))
    else:
        b_spec = pl.BlockSpec(sq_b + (tk, tn), lambda i, j, k: lead(b_lead, (k, j)))
    mn_spec = pl.BlockSpec((tm, tn), lambda i, j, k: (i, j))
    in_specs = [a_spec, b_spec]
    args = [a, b]
    if add is not None:
        in_specs.append(mn_spec)
        args.append(add)
    aliases = {}
    if out_into is None:
        o_spec, out_shape = mn_spec, jax.ShapeDtypeStruct((M, N), out_dtype)
    else:
        buf, n_layers, layer = out_into
        o_spec = pl.BlockSpec((None, tm, tn), lambda i, j, k: (layer, i, j))
        out_shape = jax.ShapeDtypeStruct((n_layers, M, N), out_dtype)
        if buf is not None:
            aliases = {len(args): 0}
            in_specs.append(pl.BlockSpec(memory_space=pl.ANY))
            args.append(buf)
    n_in = len(args)
    ca, cb = (0 if ta else 1), (1 if tb else 0)

    def body(*refs):
        a_ref, b_ref = refs[:2]
        add_ref = refs[2] if add is not None else None
        o_ref = refs[n_in]

        def finish(acc):
            if add_ref is not None:
                acc = acc + add_scale * add_ref[...]
            o_ref[...] = acc.astype(o_ref.dtype)

        if nk == 1:
            finish(_dot(a_ref[...], b_ref[...], ca, cb))
        else:
            acc_ref = refs[-1]
            k = pl.program_id(2)

            @pl.when(k == 0)
            def _():
                acc_ref[...] = jnp.zeros_like(acc_ref)

            acc_ref[...] += _dot(a_ref[...], b_ref[...], ca, cb)

            @pl.when(k == nk - 1)
            def _():
                finish(acc_ref[...])

    return pl.pallas_call(
        body, name=name, grid=(nm, nn, nk),
        in_specs=in_specs, out_specs=o_spec, out_shape=out_shape,
        input_output_aliases=aliases,
        scratch_shapes=[] if nk == 1 else [pltpu.VMEM((tm, tn), F32)],
        compiler_params=_cp(("parallel", "parallel", "arbitrary")),
    )(*args)


def _ln_stats(r):
    mu = jnp.mean(r, axis=-1, keepdims=True)
    xc = r - mu
    var = jnp.mean(xc * xc, axis=-1, keepdims=True)
    rstd = lax.rsqrt(var + LN_EPS)
    return xc * rstd, rstd


def _add_ln(x, h, g, b, alpha, *, name):
    T, D = x.shape
    tr = _tile(T, 512, SUBLANES)
    row = pl.BlockSpec((tr, D), lambda i: (i, 0))
    vec = pl.BlockSpec((1, D), lambda i: (0, 0))

    def body(x_ref, h_ref, g_ref, b_ref, o_ref, ob_ref):
        xhat, _ = _ln_stats(alpha * x_ref[...] + h_ref[...])
        y = xhat * g_ref[...] + b_ref[...]
        o_ref[...] = y
        ob_ref[...] = y.astype(BF16)

    return pl.pallas_call(
        body, name=name, grid=(T // tr,),
        in_specs=[row, row, vec, vec], out_specs=[row, row],
        out_shape=[jax.ShapeDtypeStruct((T, D), F32), jax.ShapeDtypeStruct((T, D), BF16)],
        compiler_params=_cp(("parallel",)),
    )(x, h, g.reshape(1, D), b.reshape(1, D))


def _ln_bwd(dy, x, h, g, alpha, *, name):
    T, D = x.shape
    tr = _tile(T, 512, SUBLANES)
    row = pl.BlockSpec((tr, D), lambda i: (i, 0))
    vec = pl.BlockSpec((1, D), lambda i: (0, 0))

    def body(dy_ref, x_ref, h_ref, g_ref, dr_ref, drb_ref, dg_ref, db_ref):
        @pl.when(pl.program_id(0) == 0)
        def _():
            dg_ref[...] = jnp.zeros_like(dg_ref)
            db_ref[...] = jnp.zeros_like(db_ref)

        xhat, rstd = _ln_stats(alpha * x_ref[...] + h_ref[...])
        dyv = dy_ref[...]
        dxhat = dyv * g_ref[...]
        c1 = jnp.mean(dxhat, axis=-1, keepdims=True)
        c2 = jnp.mean(dxhat * xhat, axis=-1, keepdims=True)
        dr = rstd * (dxhat - c1 - xhat * c2)
        dr_ref[...] = dr
        drb_ref[...] = dr.astype(BF16)
        dg_ref[...] += jnp.sum(dyv * xhat, axis=0, keepdims=True)
        db_ref[...] += jnp.sum(dyv, axis=0, keepdims=True)

    return pl.pallas_call(
        body, name=name, grid=(T // tr,),
        in_specs=[row, row, row, vec], out_specs=[row, row, vec, vec],
        out_shape=[jax.ShapeDtypeStruct((T, D), F32), jax.ShapeDtypeStruct((T, D), BF16),
                   jax.ShapeDtypeStruct((1, D), F32), jax.ShapeDtypeStruct((1, D), F32)],
        compiler_params=_cp(("arbitrary",)),
    )(dy, x, h, g.reshape(1, D))


def _sgu_pieces(zp, lg, lb, E):
    z = _gelu(zp)
    u, v = z[:, :E], z[:, E:]
    vhat, rstd = _ln_stats(v)
    vn = vhat * lg + lb
    return u, vhat, rstd, vn


def _tril_mask():
    t = lax.broadcasted_iota(jnp.int32, (CHUNK, CHUNK), 0)
    s = lax.broadcasted_iota(jnp.int32, (CHUNK, CHUNK), 1)
    return s <= t


def _sgu_fwd(zp, ws, bs, lg, lb, *, name):
    T, E2 = zp.shape
    E = E2 // 2
    G = ws.shape[0]
    cw = E // G

    def body(zp_ref, ws_ref, bs_ref, lg_ref, lb_ref, y_ref):
        u, _, _, vn = _sgu_pieces(zp_ref[...], lg_ref[...], lb_ref[...], E)
        vnb = vn.astype(BF16)
        tril = _tril_mask()
        for g in range(G):
            cols = slice(g * cw, (g + 1) * cw)
            w = jnp.where(tril, ws_ref[g], 0.0).astype(BF16)
            sv = _dot(w, vnb[:, cols], 1, 0) + bs_ref[g]
            y_ref[:, cols] = (u[:, cols] * sv).astype(BF16)

    return pl.pallas_call(
        body, name=name, grid=(T // CHUNK,),
        in_specs=[pl.BlockSpec((CHUNK, E2), lambda n: (n, 0)),
                  pl.BlockSpec((G, CHUNK, CHUNK), lambda n: (0, 0, 0)),
                  pl.BlockSpec((G, CHUNK, 1), lambda n: (0, 0, 0)),
                  pl.BlockSpec((1, E), lambda n: (0, 0)),
                  pl.BlockSpec((1, E), lambda n: (0, 0))],
        out_specs=pl.BlockSpec((CHUNK, E), lambda n: (n, 0)),
        out_shape=jax.ShapeDtypeStruct((T, E), BF16),
        compiler_params=_cp(("parallel",)),
    )(zp, ws, bs.reshape(G, CHUNK, 1), lg.reshape(1, E), lb.reshape(1, E))


def _sgu_bwd(zp, dy, ws, bs, lg, lb, *, name):
    T, E2 = zp.shape
    E = E2 // 2
    G = ws.shape[0]
    cw = E // G
    assert G <= LANES

    def body(zp_ref, dy_ref, ws_ref, bs_ref, lg_ref, lb_ref,
             dzp_ref, dws_ref, dbs_ref, dlg_ref, dlb_ref, dvn_ref):
        @pl.when(pl.program_id(0) == 0)
        def _():
            dws_ref[...] = jnp.zeros_like(dws_ref)
            dbs_ref[...] = jnp.zeros_like(dbs_ref)
            dlg_ref[...] = jnp.zeros_like(dlg_ref)
            dlb_ref[...] = jnp.zeros_like(dlb_ref)

        zpv = zp_ref[...]
        lgv = lg_ref[...]
        u, vhat, rstd, vn = _sgu_pieces(zpv, lgv, lb_ref[...], E)
        gp = _gelu_grad(zpv)
        vnb = vn.astype(BF16)
        tril = _tril_mask()
        lane = lax.broadcasted_iota(jnp.int32, (CHUNK, LANES), 1)
        dbs = dbs_ref[...]
        for g in range(G):
            cols = slice(g * cw, (g + 1) * cw)
            w = jnp.where(tril, ws_ref[g], 0.0).astype(BF16)
            sv = _dot(w, vnb[:, cols], 1, 0) + bs_ref[g]
            dyg = dy_ref[:, cols]
            dzp_ref[:, cols] = (dyg * sv * gp[:, cols]).astype(BF16)
            dsv = dyg * u[:, cols]
            dsvb = dsv.astype(BF16)
            dws_ref[g] += jnp.where(tril, _dot(dsvb, vnb[:, cols], 1, 1), 0.0)
            dvn_ref[:, cols] = _dot(w, dsvb, 0, 0)
            dbs = dbs + jnp.where(lane == g, jnp.sum(dsv, axis=1, keepdims=True), 0.0)
        dbs_ref[...] = dbs
        dvn = dvn_ref[...]
        dlg_ref[...] += jnp.sum(dvn * vhat, axis=0, keepdims=True)
        dlb_ref[...] += jnp.sum(dvn, axis=0, keepdims=True)
        dvhat = dvn * lgv
        c1 = jnp.mean(dvhat, axis=-1, keepdims=True)
        c2 = jnp.mean(dvhat * vhat, axis=-1, keepdims=True)
        dv = rstd * (dvhat - c1 - vhat * c2)
        dzp_ref[:, E:] = (dv * gp[:, E:]).astype(BF16)

    vecE = pl.BlockSpec((1, E), lambda n: (0, 0))
    return pl.pallas_call(
        body, name=name, grid=(T // CHUNK,),
        in_specs=[pl.BlockSpec((CHUNK, E2), lambda n: (n, 0)),
                  pl.BlockSpec((CHUNK, E), lambda n: (n, 0)),
                  pl.BlockSpec((G, CHUNK, CHUNK), lambda n: (0, 0, 0)),
                  pl.BlockSpec((G, CHUNK, 1), lambda n: (0, 0, 0)),
                  vecE, vecE],
        out_specs=[pl.BlockSpec((CHUNK, E2), lambda n: (n, 0)),
                   pl.BlockSpec((G, CHUNK, CHUNK), lambda n: (0, 0, 0)),
                   pl.BlockSpec((CHUNK, LANES), lambda n: (0, 0)),
                   vecE, vecE],
        out_shape=[jax.ShapeDtypeStruct((T, E2), BF16),
                   jax.ShapeDtypeStruct((G, CHUNK, CHUNK), F32),
                   jax.ShapeDtypeStruct((CHUNK, LANES), F32),
                   jax.ShapeDtypeStruct((1, E), F32), jax.ShapeDtypeStruct((1, E), F32)],
        scratch_shapes=[pltpu.VMEM((CHUNK, E), F32)],
        compiler_params=_cp(("arbitrary",)),
    )(zp, dy, ws, bs.reshape(G, CHUNK, 1), lg.reshape(1, E), lb.reshape(1, E))


def _delay(h, k):
    t = lax.broadcasted_iota(jnp.int32, h.shape, 0)
    return jnp.where(t >= k, pltpu.roll(h, k, 0), 0.0)


def _advance(d, k):
    T = d.shape[0]
    t = lax.broadcasted_iota(jnp.int32, d.shape, 0)
    return jnp.where(t < T - k, pltpu.roll(d, T - k, 0), 0.0)


def _conv3(h, w_ref, b_ref):
    return w_ref[2:3, :] * h + w_ref[1:2, :] * _delay(h, 1) + w_ref[0:1, :] * _delay(h, 2) + b_ref[...]


def _convglu_fwd(hup, cw, cb, *, name):
    T, F2 = hup.shape
    F = F2 // 2
    tc = LANES
    nt = F // tc

    def body(ha_ref, hg_ref, wa_ref, wg_ref, ba_ref, bg_ref, o_ref):
        a = _conv3(ha_ref[...], wa_ref, ba_ref)
        g = _conv3(hg_ref[...], wg_ref, bg_ref)
        o_ref[...] = (_gelu(a) * g).astype(BF16)

    col = lambda off: pl.BlockSpec((T, tc), lambda j: (0, j + off))
    w3 = lambda off: pl.BlockSpec((3, tc), lambda j: (0, j + off))
    b1 = lambda off: pl.BlockSpec((1, tc), lambda j: (0, j + off))
    return pl.pallas_call(
        body, name=name, grid=(nt,),
        in_specs=[col(0), col(nt), w3(0), w3(nt), b1(0), b1(nt)],
        out_specs=pl.BlockSpec((T, tc), lambda j: (0, j)),
        out_shape=jax.ShapeDtypeStruct((T, F), BF16),
        compiler_params=_cp(("parallel",)),
    )(hup, hup, cw, cw, cb.reshape(1, F2), cb.reshape(1, F2))


def _convglu_bwd(hup, dact, cw, cb, *, name):
    T, F2 = hup.shape
    F = F2 // 2
    tc = LANES
    nt = F // tc

    def half(h, w_ref, d, dh_ref, dw_ref, db_ref, i):
        dh_ref[i] = (w_ref[2:3, :] * d + w_ref[1:2, :] * _advance(d, 1) + w_ref[0:1, :] * _advance(d, 2)).astype(BF16)
        dw_ref[i, 0:1, :] = jnp.sum(d * _delay(h, 2), axis=0, keepdims=True)
        dw_ref[i, 1:2, :] = jnp.sum(d * _delay(h, 1), axis=0, keepdims=True)
        dw_ref[i, 2:3, :] = jnp.sum(d * h, axis=0, keepdims=True)
        db_ref[i] = jnp.sum(d, axis=0, keepdims=True)

    def body(ha_ref, hg_ref, d_ref, wa_ref, wg_ref, ba_ref, bg_ref, dh_ref, dw_ref, db_ref):
        ha, hg = ha_ref[...], hg_ref[...]
        a = _conv3(ha, wa_ref, ba_ref)
        g = _conv3(hg, wg_ref, bg_ref)
        d = d_ref[...]
        half(ha, wa_ref, d * g * _gelu_grad(a), dh_ref, dw_ref, db_ref, 0)
        half(hg, wg_ref, d * _gelu(a), dh_ref, dw_ref, db_ref, 1)

    col = lambda off: pl.BlockSpec((T, tc), lambda j: (0, j + off))
    w3 = lambda off: pl.BlockSpec((3, tc), lambda j: (0, j + off))
    b1 = lambda off: pl.BlockSpec((1, tc), lambda j: (0, j + off))
    return pl.pallas_call(
        body, name=name, grid=(nt,),
        in_specs=[col(0), col(nt), col(0), w3(0), w3(nt), b1(0), b1(nt)],
        out_specs=[pl.BlockSpec((2, T, tc), lambda j: (0, 0, j)),
                   pl.BlockSpec((2, 3, tc), lambda j: (0, 0, j)),
                   pl.BlockSpec((2, 1, tc), lambda j: (0, 0, j))],
        out_shape=[jax.ShapeDtypeStruct((2, T, F), BF16),
                   jax.ShapeDtypeStruct((2, 3, F), F32),
                   jax.ShapeDtypeStruct((2, 1, F), F32)],
        compiler_params=_cp(("parallel",)),
    )(hup, hup, dact, cw, cw, cb.reshape(1, F2), cb.reshape(1, F2))


def _bucket_table():
    iq = np.arange(BLK)[:, None]
    ik = np.arange(2 * BLK)[None, :]
    delta = iq + BLK - ik
    out = []
    for win, dil in DILATED_GROUPS:
        valid = (delta >= 0) & (delta <= win // dil)
        n = (np.clip(delta, 0, None) * dil).astype(np.int32)
        max_exact = REL_BUCKETS // 2
        nf = np.maximum(n, 1).astype(np.float32)
        large = max_exact + (np.log(nf / np.float32(max_exact)) / np.float32(math.log(REL_MAX_DIST / max_exact))
                             * np.float32(REL_BUCKETS - max_exact)).astype(np.int32)
        large = np.minimum(large, REL_BUCKETS - 1)
        out.append(np.where(valid, np.where(n < max_exact, n, large), -1))
    return np.stack(out).astype(np.int32)


def _band_bias(rel_table, H, *, name):
    bkt = jnp.asarray(_bucket_table())

    def body(tab_ref, bk_ref, o_ref):
        g = pl.program_id(0)
        bk = bk_ref[...]
        prev = lax.broadcasted_iota(jnp.int32, bk.shape, 1) < BLK
        for h in range(H):
            def bucket(c, acc):
                return jnp.where(bk == c, tab_ref[c, g * H + h], acc)

            b = lax.fori_loop(0, REL_BUCKETS, bucket, jnp.zeros(bk.shape, F32))
            b = jnp.where(bk >= 0, b, NEG)
            o_ref[h, 0] = b
            o_ref[h, 1] = jnp.where(prev, NEG, b)

    return pl.pallas_call(
        body, name=name, grid=(N_GROUPS,),
        in_specs=[pl.BlockSpec(memory_space=pltpu.SMEM),
                  pl.BlockSpec((None, BLK, 2 * BLK), lambda g: (g, 0, 0))],
        out_specs=pl.BlockSpec((None, H, 2, BLK, 2 * BLK), lambda g: (g, 0, 0, 0, 0)),
        out_shape=jax.ShapeDtypeStruct((N_GROUPS, H, 2, BLK, 2 * BLK), F32),
        compiler_params=_cp(("parallel",)),
    )(rel_table, bkt)


PAIR = 2 * HEAD_DIM


def _head_masks():
    lane = lax.broadcasted_iota(jnp.int32, (BLK, PAIR), 1)
    return lane < HEAD_DIM, lane >= HEAD_DIM


def _only(mask, a):
    return jnp.where(mask, a, jnp.zeros_like(a))


def _scores(qm, kp, kc, bias):
    scale = HEAD_DIM ** -0.5
    sp = _dot(qm, kp, 1, 1) * scale + bias[:, :BLK]
    sc = _dot(qm, kc, 1, 1) * scale + bias[:, BLK:]
    return sp, sc


def _attn_specs(nb, dil, g, H, HD, qb):
    nbg = nb // dil
    cur = lambda col: pl.BlockSpec((BLK, HD), lambda b: (qb(b), col))
    prev = lambda col: pl.BlockSpec((BLK, HD), lambda b: (jnp.maximum(qb(b) - 1, 0), col))
    stat = pl.BlockSpec((BLK, LANES), lambda b: (qb(b), 0))
    bias = pl.BlockSpec((None, H, None, BLK, 2 * BLK),
                        lambda b: (g, 0, jnp.where(qb(b) % nbg == 0, 1, 0), 0, 0))
    return cur, prev, stat, bias


def _attn_fwd(q, k, v, bias, g, dil, *, name):
    (qa, qc), (ka, kc_), (va, vc_) = q, k, v
    T = qa.shape[0]
    H = bias.shape[1]
    HD = H * HEAD_DIM
    assert H % 2 == 0 and H <= LANES and T % (dil * BLK) == 0
    nb = T // BLK
    cur, prev, stat, bspec = _attn_specs(nb, dil, g, H, HD, lambda b: b)

    def body(q_ref, kp_ref, kc_ref, vp_ref, vc_ref, b_ref, o_ref, l_ref):
        masks = _head_masks()
        lane = lax.broadcasted_iota(jnp.int32, (BLK, LANES), 1)
        lse = jnp.zeros((BLK, LANES), F32)
        for p in range(H // 2):
            cols = slice(p * PAIR, (p + 1) * PAIR)
            qv, kp, kc, vp, vc = q_ref[:, cols], kp_ref[:, cols], kc_ref[:, cols], vp_ref[:, cols], vc_ref[:, cols]
            o = jnp.zeros((BLK, PAIR), F32)
            for e in range(2):
                h = 2 * p + e
                sp, sc = _scores(_only(masks[e], qv), kp, kc, b_ref[h])
                m = jnp.maximum(jnp.max(sp, axis=-1, keepdims=True), jnp.max(sc, axis=-1, keepdims=True))
                pp, pc = jnp.exp(sp - m), jnp.exp(sc - m)
                den = jnp.sum(pp, axis=-1, keepdims=True) + jnp.sum(pc, axis=-1, keepdims=True)
                o = o + _dot((pp / den).astype(BF16), _only(masks[e], vp), 1, 0)
                o = o + _dot((pc / den).astype(BF16), _only(masks[e], vc), 1, 0)
                lse = jnp.where(lane == h, m + jnp.log(den), lse)
            o_ref[:, cols] = o
        l_ref[...] = lse

    return pl.pallas_call(
        body, name=name, grid=(nb,),
        in_specs=[cur(qc), prev(kc_), cur(kc_), prev(vc_), cur(vc_), bspec],
        out_specs=[cur(0), stat],
        out_shape=[jax.ShapeDtypeStruct((T, HD), F32), jax.ShapeDtypeStruct((T, LANES), F32)],
        compiler_params=_cp(("parallel",)),
    )(qa, ka, ka, va, va, bias)


def _attn_combine(os, ls, *, name):
    n = len(os)
    T, HD = os[0].shape
    H = HD // HEAD_DIM
    tr = _tile(T, 512, SUBLANES)
    wide = pl.BlockSpec((tr, HD), lambda i: (i, 0))
    stat = pl.BlockSpec((tr, LANES), lambda i: (i, 0))

    def body(*refs):
        o_refs, l_refs = refs[:n], refs[n:2 * n]
        oc_ref, ocb_ref, lt_ref = refs[2 * n:]
        ls_ = [r[...] for r in l_refs]
        m = ls_[0]
        for l in ls_[1:]:
            m = jnp.maximum(m, l)
        ws = [jnp.exp(l - m) for l in ls_]
        den = ws[0]
        for w in ws[1:]:
            den = den + w
        lt_ref[...] = m + jnp.log(den)
        ws = [w / den for w in ws]
        low = lax.broadcasted_iota(jnp.int32, (tr, PAIR), 1) < HEAD_DIM
        for p in range(H // 2):
            cols = slice(p * PAIR, (p + 1) * PAIR)
            acc = jnp.zeros((tr, PAIR), F32)
            for g in range(n):
                w = jnp.where(low, ws[g][:, 2 * p:2 * p + 1], ws[g][:, 2 * p + 1:2 * p + 2])
                acc = acc + w * o_refs[g][:, cols]
            oc_ref[:, cols] = acc
            ocb_ref[:, cols] = acc.astype(BF16)

    return pl.pallas_call(
        body, name=name, grid=(T // tr,),
        in_specs=[wide] * n + [stat] * n, out_specs=[wide, wide, stat],
        out_shape=[jax.ShapeDtypeStruct((T, HD), F32), jax.ShapeDtypeStruct((T, HD), BF16),
                   jax.ShapeDtypeStruct((T, LANES), F32)],
        compiler_params=_cp(("parallel",)),
    )(*os, *ls)


def _attn_bwd_prep(do, oc, *, name):
    T, HD = do.shape
    H = HD // HEAD_DIM
    tr = _tile(T, 512, SUBLANES)
    wide = pl.BlockSpec((tr, HD), lambda i: (i, 0))
    stat = pl.BlockSpec((tr, LANES), lambda i: (i, 0))

    def body(do_ref, oc_ref, dl_ref, dob_ref):
        low = lax.broadcasted_iota(jnp.int32, (tr, PAIR), 1) < HEAD_DIM
        lane = lax.broadcasted_iota(jnp.int32, (tr, LANES), 1)
        dl = jnp.zeros((tr, LANES), F32)
        for p in range(H // 2):
            cols = slice(p * PAIR, (p + 1) * PAIR)
            d = do_ref[:, cols]
            prod = d * oc_ref[:, cols]
            dl = jnp.where(lane == 2 * p, jnp.sum(jnp.where(low, prod, 0.0), axis=-1, keepdims=True), dl)
            dl = jnp.where(lane == 2 * p + 1, jnp.sum(jnp.where(low, 0.0, prod), axis=-1, keepdims=True), dl)
            dob_ref[:, cols] = d.astype(BF16)
        dl_ref[...] = dl

    return pl.pallas_call(
        body, name=name, grid=(T // tr,),
        in_specs=[wide, wide], out_specs=[stat, wide],
        out_shape=[jax.ShapeDtypeStruct((T, LANES), F32), jax.ShapeDtypeStruct((T, HD), BF16)],
        compiler_params=_cp(("parallel",)),
    )(do, oc)


def _attn_bwd(q, k, v, bias, g, dil, do, lse, delta, *, name):
    (qa, qc), (ka, kc_), (va, vc_) = q, k, v
    T, HD = do.shape
    H = HD // HEAD_DIM
    nb = T // BLK
    scale = HEAD_DIM ** -0.5
    cur, prev, stat, bspec = _attn_specs(nb, dil, g, H, HD, lambda b: jnp.minimum(b, nb - 1))
    late = pl.BlockSpec((BLK, HD), lambda b: (jnp.maximum(b - 1, 0), 0))

    def body(q_ref, kp_ref, kc_ref, vp_ref, vc_ref, b_ref, do_ref, l_ref, dl_ref,
             dq_ref, dk_ref, dv_ref, db_ref, ck_ref, cv_ref):
        b = pl.program_id(0)

        @pl.when(b == 0)
        def _():
            db_ref[...] = jnp.zeros_like(db_ref)
            ck_ref[...] = jnp.zeros_like(ck_ref)
            cv_ref[...] = jnp.zeros_like(cv_ref)

        @pl.when(b < nb)
        def _():
            masks = _head_masks()
            for p in range(H // 2):
                cols = slice(p * PAIR, (p + 1) * PAIR)
                qv, kp, kc, vp, vc, dov = (q_ref[:, cols], kp_ref[:, cols], kc_ref[:, cols], vp_ref[:, cols],
                                           vc_ref[:, cols], do_ref[:, cols])
                zero = jnp.zeros((BLK, PAIR), F32)
                dq, dkp, dkc, dvp, dvc = zero, zero, zero, zero, zero
                for e in range(2):
                    h = 2 * p + e
                    qm, dom = _only(masks[e], qv), _only(masks[e], dov)
                    sp, sc = _scores(qm, kp, kc, b_ref[h])
                    l, dl = l_ref[:, h:h + 1], dl_ref[:, h:h + 1]
                    pp, pc = jnp.exp(sp - l), jnp.exp(sc - l)
                    dsp = pp * (_dot(dom, vp, 1, 1) - dl)
                    dsc = pc * (_dot(dom, vc, 1, 1) - dl)
                    db_ref[h, :, :BLK] += dsp
                    db_ref[h, :, BLK:] += dsc
                    dspb, dscb = dsp.astype(BF16), dsc.astype(BF16)
                    dq = dq + _dot(dspb, _only(masks[e], kp), 1, 0) + _dot(dscb, _only(masks[e], kc), 1, 0)
                    dkp = dkp + _dot(dspb, qm, 0, 0)
                    dkc = dkc + _dot(dscb, qm, 0, 0)
                    dvp = dvp + _dot(pp.astype(BF16), dom, 0, 0)
                    dvc = dvc + _dot(pc.astype(BF16), dom, 0, 0)
                dq_ref[:, cols] = (scale * dq).astype(BF16)
                dk_ref[:, cols] = ck_ref[:, cols] + scale * dkp
                dv_ref[:, cols] = cv_ref[:, cols] + dvp
                ck_ref[:, cols] = scale * dkc
                cv_ref[:, cols] = dvc

        @pl.when(b == nb)
        def _():
            dk_ref[...] = ck_ref[...]
            dv_ref[...] = cv_ref[...]

    f32 = jax.ShapeDtypeStruct((T, HD), F32)
    return pl.pallas_call(
        body, name=name, grid=(nb + 1,),
        in_specs=[cur(qc), prev(kc_), cur(kc_), prev(vc_), cur(vc_), bspec, cur(0), stat, stat],
        out_specs=[cur(0), late, late, pl.BlockSpec((H, BLK, 2 * BLK), lambda b: (0, 0, 0))],
        out_shape=[jax.ShapeDtypeStruct((T, HD), BF16), f32, f32,
                   jax.ShapeDtypeStruct((H, BLK, 2 * BLK), F32)],
        scratch_shapes=[pltpu.VMEM((BLK, HD), F32), pltpu.VMEM((BLK, HD), F32)],
        compiler_params=_cp(("arbitrary",)),
    )(qa, ka, ka, va, va, bias, do, lse, delta)


def _rel_grad(dbs, *, name):
    H = dbs[0].shape[1]
    n = len(dbs)
    assert H <= LANES
    bkt = jnp.asarray(_bucket_table())

    def body(*refs):
        bk = refs[n][...]
        o_ref = refs[n + 1]
        db = refs[0][...]
        for r in refs[1:n]:
            db = db + r[...]
        row = lax.broadcasted_iota(jnp.int32, (REL_BUCKETS, LANES), 0)
        col = lax.broadcasted_iota(jnp.int32, (REL_BUCKETS, LANES), 1)

        def bucket(c, acc):
            sel = bk == c
            for h in range(H):
                val = jnp.sum(jnp.where(sel, db[h], 0.0))
                acc = jnp.where((row == c) & (col == h), val, acc)
            return acc

        o_ref[...] = lax.fori_loop(0, REL_BUCKETS, bucket, jnp.zeros((REL_BUCKETS, LANES), F32))

    dspec = pl.BlockSpec((None, H, BLK, 2 * BLK), lambda g: (g, 0, 0, 0))
    return pl.pallas_call(
        body, name=name, grid=(N_GROUPS,),
        in_specs=[dspec] * n + [pl.BlockSpec((None, BLK, 2 * BLK), lambda g: (g, 0, 0))],
        out_specs=pl.BlockSpec((None, REL_BUCKETS, LANES), lambda g: (g, 0, 0)),
        out_shape=jax.ShapeDtypeStruct((N_GROUPS, REL_BUCKETS, LANES), F32),
        compiler_params=_cp(("parallel",)),
    )(*dbs, bkt)


def _to_group(a, d):
    T, C = a.shape
    return a if d == 1 else a.reshape(T // d, d, C).transpose(1, 0, 2).reshape(T, C)


def _from_group(a, d):
    T, C = a.shape
    return a if d == 1 else a.reshape(d, T // d, C).transpose(1, 0, 2).reshape(T, C)


def _rows_view(a):
    a2 = a.reshape(-1, a.shape[-1])
    R, C = a2.shape
    tr = _tile(R, max(SUBLANES, (1 << 18) // C), SUBLANES)
    return a2, R, C, tr


def _addn(xs, out_dtype, *, name):
    shape = xs[0].shape
    x2s = [_rows_view(x)[0] for x in xs]
    _, R, C, tr = _rows_view(xs[0])
    spec = pl.BlockSpec((tr, C), lambda i: (i, 0))

    def body(*refs):
        acc = refs[0][...].astype(F32)
        for r in refs[1:-1]:
            acc = acc + r[...].astype(F32)
        refs[-1][...] = acc.astype(out_dtype)

    out = pl.pallas_call(
        body, name=name, grid=(R // tr,),
        in_specs=[spec] * len(xs), out_specs=spec,
        out_shape=jax.ShapeDtypeStruct((R, C), out_dtype),
        compiler_params=_cp(("parallel",)),
    )(*x2s)
    return out.reshape(shape)


def _slot_sum(b_ref, own, my_id):
    acc = None
    for s in range(b_ref.shape[0]):
        term = jnp.where(my_id == s, own, b_ref[s].astype(F32))
        acc = term if acc is None else acc + term
    return acc


def _part_geometry(shape, axis):
    L, R, C = shape
    if axis == COL:
        C //= N_CHIPS
    else:
        R //= N_CHIPS
    if L == 1:
        R //= 2
    return max(L // 2, 1), R, C


def _sum_parts(buf, grad, axis, where, *, name):
    n, Lp, Rp, Cp = buf.shape
    L = grad.shape[0]
    assert (Lp, Rp, Cp) == _part_geometry(grad.shape, axis)
    tr = _tile(Rp, max(2 * SUBLANES, (1 << 17) // Cp), 2 * SUBLANES)
    nr = Rp // tr

    def own_map(l, i, w):
        if L == 1:
            return (0, (2 * w[0] + w[1]) * nr + i, 0)
        if axis == COL:
            return (w[1] * Lp + l, i, w[0])
        return (w[1] * Lp + l, w[0] * nr + i, 0)

    def body(w_ref, b_ref, g_ref, o_ref):
        o_ref[...] = _slot_sum(b_ref, g_ref[...].astype(F32), w_ref[2])

    return pl.pallas_call(
        body, name=name,
        grid_spec=pltpu.PrefetchScalarGridSpec(
            num_scalar_prefetch=1, grid=(Lp, nr),
            in_specs=[pl.BlockSpec((n, None, tr, Cp), lambda l, i, w: (0, l, i, 0)),
                      pl.BlockSpec((None, tr, Cp), own_map)],
            out_specs=pl.BlockSpec((None, None, tr, Cp), lambda l, i, w: (w[1], l, i, 0))),
        out_shape=jax.ShapeDtypeStruct((2, Lp, Rp, Cp), F32),
        compiler_params=_cp(("parallel", "parallel")),
    )(where, buf, grad)


def _sum_small(buf, own, where, *, name):
    n, R, C = buf.shape
    tr = _tile(R, 512, SUBLANES)

    def body(w_ref, b_ref, g_ref, o_ref):
        o_ref[...] = _slot_sum(b_ref, g_ref[...], w_ref[2])

    return pl.pallas_call(
        body, name=name,
        grid_spec=pltpu.PrefetchScalarGridSpec(
            num_scalar_prefetch=1, grid=(R // tr,),
            in_specs=[pl.BlockSpec((n, tr, C), lambda i, w: (0, i, 0)),
                      pl.BlockSpec((tr, C), lambda i, w: (i, 0))],
            out_specs=pl.BlockSpec((tr, C), lambda i, w: (i, 0))),
        out_shape=jax.ShapeDtypeStruct((R, C), F32),
        compiler_params=_cp(("parallel",)),
    )(where, buf, own)


def _place_shard(shard, axis, where, out_dtype, *, name):
    L, R, C = shard.shape
    tr = _tile(R, max(2 * SUBLANES, (1 << 18) // C), 2 * SUBLANES)
    nr = R // tr
    if axis == COL:
        full, out_map = (L, R, C * N_CHIPS), (lambda l, i, w: (l, i, w[0]))
    else:
        full, out_map = (L, R * N_CHIPS, C), (lambda l, i, w: (l, w[0] * nr + i, 0))

    def body(w_ref, x_ref, o_ref):
        o_ref[...] = x_ref[...].astype(out_dtype)

    return pl.pallas_call(
        body, name=name,
        grid_spec=pltpu.PrefetchScalarGridSpec(
            num_scalar_prefetch=1, grid=(L, nr),
            in_specs=[pl.BlockSpec((None, tr, C), lambda l, i, w: (l, i, 0))],
            out_specs=pl.BlockSpec((None, tr, C), out_map)),
        out_shape=jax.ShapeDtypeStruct(full, out_dtype),
        compiler_params=_cp(("parallel", "parallel")),
    )(where, shard)


def _loss_grad(y, tgt, *, name):
    T, D = y.shape
    tr = _tile(T, 512, SUBLANES)
    row = pl.BlockSpec((tr, D), lambda i: (i, 0))

    def body(y_ref, t_ref, dy_ref, l_ref, acc_ref):
        i = pl.program_id(0)

        @pl.when(i == 0)
        def _():
            acc_ref[...] = jnp.zeros_like(acc_ref)

        err = y_ref[...] - t_ref[...]
        dy_ref[...] = err * (1.0 / D)
        acc_ref[...] += jnp.sum(err * err, axis=0, keepdims=True)

        @pl.when(i == T // tr - 1)
        def _():
            tot = jnp.sum(acc_ref[...], axis=1, keepdims=True) * (0.5 / D)
            l_ref[...] = jnp.broadcast_to(tot, (1, LANES))

    return pl.pallas_call(
        body, name=name, grid=(T // tr,),
        in_specs=[row, row],
        out_specs=[row, pl.BlockSpec((1, LANES), lambda i: (0, 0))],
        out_shape=[jax.ShapeDtypeStruct((T, D), F32), jax.ShapeDtypeStruct((1, LANES), F32)],
        scratch_shapes=[pltpu.VMEM((1, D), F32)],
        compiler_params=_cp(("arbitrary",)),
    )(y, tgt)


def _adamw(w, g, m, v, *, name):
    shape = w.shape
    w2, R, C, tr = _rows_view(w)
    spec = pl.BlockSpec((tr, C), lambda i: (i, 0))

    def body(w_ref, g_ref, m_ref, v_ref, d_ref, nm_ref, nv_ref):
        gv = g_ref[...]
        nm = ADAM_B1 * m_ref[...] + (1.0 - ADAM_B1) * gv
        nv = ADAM_B2 * v_ref[...] + (1.0 - ADAM_B2) * (gv * gv)
        m_hat = nm / (1.0 - ADAM_B1 ** ADAM_STEP)
        v_hat = nv / (1.0 - ADAM_B2 ** ADAM_STEP)
        d_ref[...] = -ADAM_LR * (m_hat / (jnp.sqrt(v_hat) + ADAM_EPS) + ADAM_WD * w_ref[...])
        nm_ref[...] = nm
        nv_ref[...] = nv

    shp = jax.ShapeDtypeStruct((R, C), F32)
    outs = pl.pallas_call(
        body, name=name, grid=(R // tr,),
        in_specs=[spec] * 4, out_specs=[spec] * 3, out_shape=[shp] * 3,
        compiler_params=_cp(("parallel",)),
    )(w2, g.reshape(R, C), m.reshape(R, C), v.reshape(R, C))
    return tuple(o.reshape(shape) for o in outs)


def _pack(arrs):
    flat = jnp.concatenate([a.reshape(-1).astype(F32) for a in arrs])
    unit = SUBLANES * LANES
    pad = (-flat.shape[0]) % unit
    return jnp.pad(flat, (0, pad)).reshape(-1, LANES)


def _unpack(buf, shapes):
    flat = buf.reshape(-1)
    out, off = [], 0
    for s in shapes:
        n = int(np.prod(s))
        out.append(flat[off:off + n].reshape(s))
        off += n
    return out


def _me():
    return lax.axis_index("x"), lax.axis_index("y"), lax.axis_index("c")


def _flip(pos, k):
    x, y, c = pos
    return (1 - x if k & 4 else x, 1 - y if k & 2 else y, 1 - c if k & 1 else c)


HBM = pl.BlockSpec(memory_space=pltpu.HBM)

COL, ROW = -1, -2


def _shard_window(ref, axis, j, n):
    start = pl.multiple_of(j * n, n)
    if axis == COL:
        return ref.at[:, :, pl.ds(start, n)]
    return ref.at[:, pl.ds(start, n), :]


def _gather_weights(fulls, axes, *, name):
    n = len(fulls)
    peers = (2, 4, 6)

    def body(*refs):
        bufs = refs[n:2 * n]
        send_sems, recv_sems = refs[2 * n:]
        me = _me()

        def window(i, chip):
            return _shard_window(bufs[i], axes[i], chip, fulls[i].shape[axes[i]] // N_CHIPS)

        def copy(pi, i, chip):
            return pltpu.make_async_remote_copy(
                src_ref=window(i, chip), dst_ref=window(i, chip),
                send_sem=send_sems.at[pi * n + i], recv_sem=recv_sems.at[pi * n + i],
                device_id=_flip(me, peers[pi]), device_id_type=MESH)

        sends = [copy(pi, i, 2 * me[0] + me[1]) for pi in range(len(peers)) for i in range(n)]
        for cp in sends:
            cp.start()
        for pi, k in enumerate(peers):
            px, py, _ = _flip(me, k)
            for i in range(n):
                copy(pi, i, 2 * px + py).wait_recv()
        for cp in sends:
            cp.wait_send()

    return pl.pallas_call(
        body, name=name,
        in_specs=[HBM] * n, out_specs=[HBM] * n,
        out_shape=[jax.ShapeDtypeStruct(f.shape, f.dtype) for f in fulls],
        input_output_aliases={i: i for i in range(n)},
        scratch_shapes=[pltpu.SemaphoreType.DMA((len(peers) * n,)),
                        pltpu.SemaphoreType.DMA((len(peers) * n,))],
    )(*fulls)


def _scatter_grads(grads, axes, small, *, name):
    n = len(grads)
    part_shapes = [_part_geometry(gr.shape, ax) for gr, ax in zip(grads, axes)]
    out_shapes = [jax.ShapeDtypeStruct((N_DEV,) + s, gr.dtype) for s, gr in zip(part_shapes, grads)]
    out_shapes.append(jax.ShapeDtypeStruct((N_DEV,) + small.shape, small.dtype))
    n_t = n + 1

    def body(*refs):
        srcs, dsts = refs[:n_t], refs[n_t:2 * n_t]
        send_sems, recv_sems = refs[2 * n_t:]
        me = _me()
        my_id = 4 * me[0] + 2 * me[1] + me[2]

        def part(i, pos):
            if i == n:
                return srcs[i]
            px, py, pc = pos
            L = srcs[i].shape[0]
            Lp, Rp, Cp = part_shapes[i]
            chip = 2 * px + py
            if L == 1:
                assert axes[i] == ROW
                return srcs[i].at[:, pl.ds(pl.multiple_of((2 * chip + pc) * Rp, Rp), Rp), :]
            lay = pl.ds(pc * Lp, Lp)
            if axes[i] == COL:
                return srcs[i].at[lay, :, pl.ds(pl.multiple_of(chip * Cp, Cp), Cp)]
            return srcs[i].at[lay, pl.ds(pl.multiple_of(chip * Rp, Rp), Rp), :]

        sends = []
        for k in range(1, N_DEV):
            peer = _flip(me, k)
            for i in range(n_t):
                sends.append(pltpu.make_async_remote_copy(
                    src_ref=part(i, peer), dst_ref=dsts[i].at[my_id],
                    send_sem=send_sems.at[(k - 1) * n_t + i], recv_sem=recv_sems.at[(k - 1) * n_t + i],
                    device_id=peer, device_id_type=MESH))
        for cp in sends:
            cp.start()
        for k in range(1, N_DEV):
            px, py, pc = _flip(me, k)
            for i in range(n_t):
                pltpu.make_async_remote_copy(
                    src_ref=part(i, me), dst_ref=dsts[i].at[4 * px + 2 * py + pc],
                    send_sem=send_sems.at[(k - 1) * n_t + i], recv_sem=recv_sems.at[(k - 1) * n_t + i],
                    device_id=(px, py, pc), device_id_type=MESH).wait_recv()
        for cp in sends:
            cp.wait_send()

    n_sem = (N_DEV - 1) * n_t
    return pl.pallas_call(
        body, name=name,
        in_specs=[HBM] * n_t, out_specs=[HBM] * n_t, out_shape=out_shapes,
        scratch_shapes=[pltpu.SemaphoreType.DMA((n_sem,)), pltpu.SemaphoreType.DMA((n_sem,))],
    )(*grads, small)


def _pair_gather(halves, *, name):
    n = len(halves)

    def body(*refs):
        bufs = refs[n:2 * n]
        send_sems, recv_sems = refs[2 * n:]
        me = _me()
        c = me[2]
        sib = _flip(me, 1)

        def copy(i, half):
            return pltpu.make_async_remote_copy(
                src_ref=bufs[i].at[half], dst_ref=bufs[i].at[half], send_sem=send_sems.at[i],
                recv_sem=recv_sems.at[i], device_id=sib, device_id_type=MESH)

        sends = [copy(i, c) for i in range(n)]
        for cp in sends:
            cp.start()
        for i in range(n):
            copy(i, 1 - c).wait_recv()
        for cp in sends:
            cp.wait_send()

    return pl.pallas_call(
        body, name=name,
        in_specs=[HBM] * n, out_specs=[HBM] * n,
        out_shape=[jax.ShapeDtypeStruct(h.shape, h.dtype) for h in halves],
        input_output_aliases={i: i for i in range(n)},
        scratch_shapes=[pltpu.SemaphoreType.DMA((n,)), pltpu.SemaphoreType.DMA((n,))],
    )(*halves)


def _local_step(x, tgt, W, P):
    T, D = x.shape
    depth = P["ln_g"].shape[0]
    n_a = W["a_w_in"].shape[0]
    alpha = (2 * depth) ** 0.25
    H = W["kv_w"].shape[1] // (2 * HEAD_DIM)
    HD = H * HEAD_DIM
    dils = [d for _, d in DILATED_GROUPS]
    assert T % (dils[-1] * BLK) == 0

    bias = _band_bias(P["rel_table"], H, name="band_bias")
    saved = []
    xf, xb = x, x.astype(BF16)
    kg = vg = None
    for i in range(depth):
        s = {"x": xf, "xb": xb}
        if i < n_a:
            s["zp"] = _mm(xb, W["a_w_in"], b_lead=(i,), name=f"a{i}_in")
            s["y"] = _sgu_fwd(s["zp"], P["a_w_s"][i], P["a_b_s"][i], P["a_ln_g"][i], P["a_ln_b"][i], name=f"a{i}_sgu")
            s["h"] = _mm(s["y"], W["a_w_out"], b_lead=(i,), name=f"a{i}_out")
        else:
            j = i - n_a
            if j == 0:
                kv = _mm(xb, W["kv_w"], out_dtype=BF16, name="kv_proj")
                kg = [(kv, 0) if d == 1 else (_to_group(kv[:, :HD], d), 0) for d in dils]
                vg = [(kv, 1) if d == 1 else (_to_group(kv[:, HD:], d), 0) for d in dils]
            q = _mm(xb, W["b_w_q"], b_lead=(j,), out_dtype=BF16, name=f"b{j}_q")
            s["qg"] = [(q, g) if d == 1 else (_to_group(q[:, g * HD:(g + 1) * HD], d), 0) for g, d in enumerate(dils)]
            os, ls = [], []
            for g, d in enumerate(dils):
                o_g, l_g = _attn_fwd(s["qg"][g], kg[g], vg[g], bias, g, d, name=f"b{j}_attn{g}")
                os.append(_from_group(o_g, d))
                ls.append(_from_group(l_g, d))
            s["oc"], s["ob"], s["lse"] = _attn_combine(os, ls, name=f"b{j}_comb")
            s["h"] = _mm(s["ob"], W["b_w_o"], b_lead=(j,), name=f"b{j}_o")
        s["x1"], s["x1b"] = _add_ln(xf, s["h"], P["ln_g"][i, 0], P["ln_b"][i, 0], alpha, name=f"l{i}_ln1")
        s["hup"] = _mm(s["x1b"], W["ffn_w_up"], b_lead=(i,), name=f"l{i}_up")
        s["act"] = _convglu_fwd(s["hup"], P["ffn_conv_w"][i], P["ffn_conv_b"][i], name=f"l{i}_glu")
        s["f"] = _mm(s["act"], W["ffn_w_down"], b_lead=(i,), name=f"l{i}_down")
        xf, xb = _add_ln(s["x1"], s["f"], P["ln_g"][i, 1], P["ln_b"][i, 1], alpha, name=f"l{i}_ln2")
        saved.append(s)

    G, loss_row = _loss_grad(xf, tgt, name="loss")

    gw = {}

    def dw(key, layer, a, b, name, **kw):
        n_layers = W[key].shape[0] if W[key].ndim == 3 else 1
        gw[key] = _mm(a, b, ta=True, out_dtype=BF16, out_into=(gw.get(key), n_layers, layer), name=name, **kw)

    gp = {k: [None] * n_a for k in ("a_ln_g", "a_ln_b", "a_w_s", "a_b_s")}
    gp.update({k: [None] * depth for k in ("ffn_conv_w", "ffn_conv_b", "ln_g", "ln_b")})
    dk_parts, dv_parts, dbias_parts = [], [], []
    for i in reversed(range(depth)):
        s = saved[i]
        dr2, dr2b, dg2, db2 = _ln_bwd(G, s["x1"], s["f"], P["ln_g"][i, 1], alpha, name=f"l{i}_ln2_bwd")
        dw("ffn_w_down", i, s["act"], dr2b, f"l{i}_down_dw")
        dact = _mm(dr2b, W["ffn_w_down"], tb=True, b_lead=(i,), name=f"l{i}_down_dx")
        dhup, dcw, dcb = _convglu_bwd(s["hup"], dact, P["ffn_conv_w"][i], P["ffn_conv_b"][i], name=f"l{i}_glu_bwd")
        gp["ffn_conv_w"][i] = dcw.transpose(1, 0, 2).reshape(dcw.shape[1], -1)
        gp["ffn_conv_b"][i] = dcb.reshape(-1)
        dw("ffn_w_up", i, s["x1b"], dhup, f"l{i}_up_dw", b_halves=True)
        G1 = _mm(dhup, W["ffn_w_up"], tb=True, a_halves=True, b_lead=(i,), add=dr2, add_scale=alpha, name=f"l{i}_up_dx")
        dr1, dr1b, dg1, db1 = _ln_bwd(G1, s["x"], s["h"], P["ln_g"][i, 0], alpha, name=f"l{i}_ln1_bwd")
        gp["ln_g"][i] = jnp.concatenate([dg1, dg2], axis=0)
        gp["ln_b"][i] = jnp.concatenate([db1, db2], axis=0)
        if i < n_a:
            dw("a_w_out", i, s["y"], dr1b, f"a{i}_out_dw")
            dy = _mm(dr1b, W["a_w_out"], tb=True, b_lead=(i,), name=f"a{i}_out_dx")
            dzp, dws, dbs, dlg, dlb = _sgu_bwd(s["zp"], dy, P["a_w_s"][i], P["a_b_s"][i], P["a_ln_g"][i],
                                               P["a_ln_b"][i], name=f"a{i}_sgu_bwd")
            gp["a_w_s"][i], gp["a_b_s"][i] = dws, dbs[:, :dws.shape[0]].T
            gp["a_ln_g"][i], gp["a_ln_b"][i] = dlg[0], dlb[0]
            dw("a_w_in", i, s["xb"], dzp, f"a{i}_in_dw")
            G = _mm(dzp, W["a_w_in"], tb=True, b_lead=(i,), add=dr1, add_scale=alpha, name=f"a{i}_in_dx")
        else:
            j = i - n_a
            dw("b_w_o", j, s["ob"], dr1b, f"b{j}_o_dw")
            do_tok = _mm(dr1b, W["b_w_o"], tb=True, b_lead=(j,), name=f"b{j}_o_dx")
            delta, dob = _attn_bwd_prep(do_tok, s["oc"], name=f"b{j}_prep")
            dqs, dbs = [], []
            for g, d in enumerate(dils):
                dq, dk, dv, db = _attn_bwd(s["qg"][g], kg[g], vg[g], bias, g, d, _to_group(dob, d),
                                           _to_group(s["lse"], d), _to_group(delta, d), name=f"b{j}_attn{g}_bwd")
                dqs.append(_from_group(dq, d))
                dk_parts.append(_from_group(dk, d))
                dv_parts.append(_from_group(dv, d))
                dbs.append(db)
            dbias_parts.append(jnp.stack(dbs))
            dq_tok = jnp.concatenate(dqs, axis=1)
            dw("b_w_q", j, s["xb"], dq_tok, f"b{j}_q_dw")
            G = _mm(dq_tok, W["b_w_q"], tb=True, b_lead=(j,), add=dr1, add_scale=alpha, name=f"b{j}_q_dx")
            if j == 0:
                dkv = jnp.concatenate([_addn(dk_parts, BF16, name="dk_sum"), _addn(dv_parts, BF16, name="dv_sum")], axis=1)
                dw("kv_w", 0, s["xb"], dkv, "kv_dw")
                G = _mm(dkv, W["kv_w"], tb=True, add=G, add_scale=1.0, name="kv_dx")
    rel = _rel_grad(dbias_parts, name="rel_grad")
    grel = rel[:, :, :H].transpose(1, 0, 2).reshape(REL_BUCKETS, N_GROUPS * H)
    gsmall = {k: jnp.stack(v) for k, v in gp.items()}
    gsmall["rel_table"] = grel
    return loss_row, G, gw, gsmall


BIG = (("a_w_in", COL), ("a_w_out", ROW), ("kv_w", ROW), ("b_w_q", COL), ("b_w_o", COL),
       ("ffn_w_up", COL), ("ffn_w_down", ROW))
SMALL_SHARDED = ("a_ln_g", "a_ln_b", "ffn_conv_w", "ln_g", "ln_b")
SMALL_REPLICATED = ("a_w_s", "a_b_s", "rel_table", "ffn_conv_b")
WEIGHTS = ("a_w_in", "a_ln_g", "a_ln_b", "a_w_s", "a_b_s", "a_w_out", "kv_w", "b_w_q", "b_w_o", "rel_table",
           "ffn_w_up", "ffn_conv_w", "ffn_conv_b", "ffn_w_down", "ln_g", "ln_b")


def kernel(x, a_w_in, a_ln_g, a_ln_b, a_w_s, a_b_s, a_w_out, kv_w, b_w_q, b_w_o, rel_table, ffn_w_up, ffn_conv_w, ffn_conv_b, ffn_w_down, ln_g, ln_b, loss_target, m_a_w_in, m_a_ln_g, m_a_ln_b, m_a_w_s, m_a_b_s, m_a_w_out, m_kv_w, m_b_w_q, m_b_w_o, m_rel_table, m_ffn_w_up, m_ffn_conv_w, m_ffn_conv_b, m_ffn_w_down, m_ln_g, m_ln_b, v_a_w_in, v_a_ln_g, v_a_ln_b, v_a_w_s, v_a_b_s, v_a_w_out, v_kv_w, v_b_w_q, v_b_w_o, v_rel_table, v_ffn_w_up, v_ffn_conv_w, v_ffn_conv_b, v_ffn_w_down, v_ln_g, v_ln_b):
    w = dict(a_w_in=a_w_in, a_ln_g=a_ln_g, a_ln_b=a_ln_b, a_w_s=a_w_s, a_b_s=a_b_s, a_w_out=a_w_out, kv_w=kv_w,
             b_w_q=b_w_q, b_w_o=b_w_o, rel_table=rel_table, ffn_w_up=ffn_w_up, ffn_conv_w=ffn_conv_w,
             ffn_conv_b=ffn_conv_b, ffn_w_down=ffn_w_down, ln_g=ln_g, ln_b=ln_b)
    m = dict(a_w_in=m_a_w_in, a_ln_g=m_a_ln_g, a_ln_b=m_a_ln_b, a_w_s=m_a_w_s, a_b_s=m_a_b_s, a_w_out=m_a_w_out,
             kv_w=m_kv_w, b_w_q=m_b_w_q, b_w_o=m_b_w_o, rel_table=m_rel_table, ffn_w_up=m_ffn_w_up,
             ffn_conv_w=m_ffn_conv_w, ffn_conv_b=m_ffn_conv_b, ffn_w_down=m_ffn_w_down, ln_g=m_ln_g, ln_b=m_ln_b)
    v = dict(a_w_in=v_a_w_in, a_ln_g=v_a_ln_g, a_ln_b=v_a_ln_b, a_w_s=v_a_w_s, a_b_s=v_a_b_s, a_w_out=v_a_w_out,
             kv_w=v_kv_w, b_w_q=v_b_w_q, b_w_o=v_b_w_o, rel_table=v_rel_table, ffn_w_up=v_ffn_w_up,
             ffn_conv_w=v_ffn_conv_w, ffn_conv_b=v_ffn_conv_b, ffn_w_down=v_ffn_w_down, ln_g=v_ln_g, ln_b=v_ln_b)
    chip = 2 * lax.axis_index("x") + lax.axis_index("y")

    big_names = [n for n, _ in BIG]
    big_axes = [a for _, a in BIG]
    where = jnp.stack([chip, lax.axis_index("c"), 2 * chip + lax.axis_index("c")]).astype(jnp.int32)
    shards = [w[n].reshape((1,) + w[n].shape) if w[n].ndim == 2 else w[n] for n in big_names]
    small_shapes = [w[n].shape for n in SMALL_SHARDED]
    small = _pack([w[n] for n in SMALL_SHARDED])
    fulls = [_place_shard(s, ax, where, BF16, name=f"place_{n}") for s, ax, n in zip(shards, big_axes, big_names)]
    fulls.append(_place_shard(small[None], ROW, where, F32, name="place_small"))
    gathered = _gather_weights(fulls, big_axes + [ROW], name="gather_weights")
    W = dict(zip(big_names, gathered[:-1]))
    W["kv_w"] = W["kv_w"].reshape(W["kv_w"].shape[1:])
    P = {n: w[n] for n in SMALL_REPLICATED}
    small_all = gathered[-1].reshape((N_CHIPS,) + small.shape)
    per_chip = [_unpack(small_all[j], small_shapes) for j in range(N_CHIPS)]
    for i, n in enumerate(SMALL_SHARDED):
        P[n] = jnp.concatenate([per_chip[j][i] for j in range(N_CHIPS)], axis=-1)

    loss_row, grad_x, gw, gsmall = _local_step(x[0], loss_target[0], W, P)
    loss = lax.psum(loss_row[0, 0], ("x", "y", "c"))

    small_names = list(SMALL_SHARDED) + list(SMALL_REPLICATED)
    small_pack = _pack([gsmall[n] for n in small_names])
    slots = _scatter_grads([gw[n] for n in big_names], big_axes, small_pack, name="scatter_grads")
    sums = [_sum_parts(b, gw[n], ax, where, name=f"sum_{n}") for (n, ax), b in zip(BIG, slots[:-1])]
    small_sum = _sum_small(slots[-1], small_pack, where, name="sum_small")
    pairs = _pair_gather(sums, name="pair_gather")
    grad = {n: p.reshape(w[n].shape) for n, p in zip(big_names, pairs)}
    full_small = _unpack(small_sum, [gsmall[n].shape for n in small_names])
    for n, gfull in zip(small_names, full_small):
        if n in SMALL_SHARDED:
            width = w[n].shape[-1]
            grad[n] = lax.dynamic_slice_in_dim(gfull, chip * width, width, axis=-1)
        else:
            grad[n] = gfull

    delta, new_m, new_v = {}, {}, {}
    for n, _ in BIG:
        delta[n], new_m[n], new_v[n] = _adamw(w[n], grad[n], m[n], v[n], name=f"adamw_{n}")
    shapes = [w[n].shape for n in small_names]
    packed = _adamw(_pack([w[n] for n in small_names]), _pack([grad[n] for n in small_names]),
                    _pack([m[n] for n in small_names]), _pack([v[n] for n in small_names]), name="adamw_small")
    for out, res in zip((delta, new_m, new_v), packed):
        for n, a in zip(small_names, _unpack(res, shapes)):
            out[n] = a

    return (loss, grad_x[None], *[grad[n] for n in WEIGHTS], *[delta[n] for n in WEIGHTS],
            *[new_m[n] for n in WEIGHTS], *[new_v[n] for n in WEIGHTS])
```

```python
import functools
import math

import numpy as np
import jax
import jax.numpy as jnp
from jax import lax
from jax.experimental import pallas as pl
from jax.experimental.pallas import tpu as pltpu

F32 = jnp.float32
BF16 = jnp.bfloat16
MESH = pl.DeviceIdType.MESH

CHUNK = 128
HEAD_DIM = 64
DILATED_GROUPS = ((128, 1), (512, 4), (2048, 16))
N_GROUPS = len(DILATED_GROUPS)
BLK = 128
REL_BUCKETS = 32
REL_MAX_DIST = 2048
LN_EPS = 1e-5
NEG = -1e30
ADAM_LR = 0.001
ADAM_B1 = 0.9
ADAM_B2 = 0.999
ADAM_EPS = 1e-08
ADAM_WD = 0.01
ADAM_STEP = 10

N_CHIPS = 4
N_DEV = 8
LANES = 128
SUBLANES = 8
VMEM_LIMIT = 48 * 1024 * 1024

_SQRT_HALF = 0.7071067811865476
_INV_SQRT_2PI = 0.3989422804014327


def _cp(sem=None, vmem=VMEM_LIMIT):
    return pltpu.CompilerParams(dimension_semantics=sem, vmem_limit_bytes=vmem)


def _tile(dim, target, align=LANES):
    if dim <= target:
        return dim
    t = (target // align) * align
    while t >= align:
        if dim % t == 0:
            return t
        t -= align
    return dim


def _gelu(x):
    return 0.5 * x * (1.0 + lax.erf(x * _SQRT_HALF))


def _gelu_grad(x):
    return 0.5 * (1.0 + lax.erf(x * _SQRT_HALF)) + x * (_INV_SQRT_2PI * jnp.exp(-0.5 * x * x))


def _dot(a, b, ca, cb):
    return lax.dot_general(a, b, (((ca,), (cb,)), ((), ())), preferred_element_type=F32)


def _mn_tile(dim):
    return max(_tile(dim, 1024), _tile(dim, 1408))


def _mm(a, b, *, name, ta=False, tb=False, out_dtype=F32, tiles=None,
        a_lead=(), b_lead=(), a_halves=False, b_halves=False, add=None, add_scale=1.0, out_into=None):
    a2, b2 = a.shape[-2:], b.shape[-2:]
    M, K = (a2[1], a2[0]) if ta else a2
    Kb, N = (b2[1], b2[0]) if tb else b2
    if a_halves:
        K = 2 * K
    if b_halves:
        N = 2 * N
    assert K == Kb, (name, a.shape, b.shape)
    if tiles is None:
        k_target = 1024 if ta else (1408 if a_halves else (K if K <= 2816 else 2048))
        tm, tn = _mn_tile(M), _mn_tile(N // 2 if b_halves else N)
        tk = _tile(K // 2 if a_halves else K, k_target)
    else:
        tm = _tile(M, tiles[0])
        tn = _tile(N // 2 if b_halves else N, tiles[1])
        tk = _tile(K // 2 if a_halves else K, tiles[2])
    nm, nn, nk = M // tm, N // tn, K // tk
    nkh, nnh = nk // 2, nn // 2

    def lead(idx, rest):
        return tuple(idx) + tuple(rest)

    sq_a = (None,) * (a.ndim - 2)
    sq_b = (None,) * (b.ndim - 2)
    if a_halves:
        assert not ta and a.ndim == 3
        a_spec = pl.BlockSpec((None, tm, tk), lambda i, j, k: (k // nkh, i, k % nkh))
    elif ta:
        a_spec = pl.BlockSpec(sq_a + (tk, tm), lambda i, j, k: lead(a_lead, (k, i)))
    else:
        a_spec = pl.BlockSpec(sq_a + (tm, tk), lambda i, j, k: lead(a_lead, (i, k)))
    if b_halves:
        assert not tb and b.ndim == 3
        b_spec = pl.BlockSpec((None, tk, tn), lambda i, j, k: (j // nnh, k, j % nnh))
    elif tb:
        b_spec = pl.BlockSpec(sq_b + (tn, tk), lambda i, j, k: lead(b_lead, (j, k)))
    else:
        b_spec = pl.BlockSpec(sq_b + (tk, tn), lambda i, j, k: lead(b_lead, (k, j)))
    mn_spec = pl.BlockSpec((tm, tn), lambda i, j, k: (i, j))
    in_specs = [a_spec, b_spec]
    args = [a, b]
    if add is not None:
        in_specs.append(mn_spec)
        args.append(add)
    aliases = {}
    if out_into is None:
        o_spec, out_shape = mn_spec, jax.ShapeDtypeStruct((M, N), out_dtype)
    else:
        buf, n_layers, layer = out_into
        o_spec = pl.BlockSpec((None, tm, tn), lambda i, j, k: (layer, i, j))
        out_shape = jax.ShapeDtypeStruct((n_layers, M, N), out_dtype)
        if buf is not None:
            aliases = {len(args): 0}
            in_specs.append(pl.BlockSpec(memory_space=pl.ANY))
            args.append(buf)
    n_in = len(args)
    ca, cb = (0 if ta else 1), (1 if tb else 0)

    def body(*refs):
        a_ref, b_ref = refs[:2]
        add_ref = refs[2] if add is not None else None
        o_ref = refs[n_in]

        def finish(acc):
            if add_ref is not None:
                acc = acc + add_scale * add_ref[...]
            o_ref[...] = acc.astype(o_ref.dtype)

        if nk == 1:
            finish(_dot(a_ref[...], b_ref[...], ca, cb))
        else:
            acc_ref = refs[-1]
            k = pl.program_id(2)

            @pl.when(k == 0)
            def _():
                acc_ref[...] = jnp.zeros_like(acc_ref)

            acc_ref[...] += _dot(a_ref[...], b_ref[...], ca, cb)

            @pl.when(k == nk - 1)
            def _():
                finish(acc_ref[...])

    return pl.pallas_call(
        body, name=name, grid=(nm, nn, nk),
        in_specs=in_specs, out_specs=o_spec, out_shape=out_shape,
        input_output_aliases=aliases,
        scratch_shapes=[] if nk == 1 else [pltpu.VMEM((tm, tn), F32)],
        compiler_params=_cp(("parallel", "parallel", "arbitrary")),
    )(*args)


def _ln_stats(r):
    mu = jnp.mean(r, axis=-1, keepdims=True)
    xc = r - mu
    var = jnp.mean(xc * xc, axis=-1, keepdims=True)
    rstd = lax.rsqrt(var + LN_EPS)
    return xc * rstd, rstd


def _add_ln(x, h, g, b, alpha, *, name):
    T, D = x.shape
    tr = _tile(T, 512, SUBLANES)
    row = pl.BlockSpec((tr, D), lambda i: (i, 0))
    vec = pl.BlockSpec((1, D), lambda i: (0, 0))

    def body(x_ref, h_ref, g_ref, b_ref, o_ref, ob_ref):
        xhat, _ = _ln_stats(alpha * x_ref[...] + h_ref[...])
        y = xhat * g_ref[...] + b_ref[...]
        o_ref[...] = y
        ob_ref[...] = y.astype(BF16)

    return pl.pallas_call(
        body, name=name, grid=(T // tr,),
        in_specs=[row, row, vec, vec], out_specs=[row, row],
        out_shape=[jax.ShapeDtypeStruct((T, D), F32), jax.ShapeDtypeStruct((T, D), BF16)],
        compiler_params=_cp(("parallel",)),
    )(x, h, g.reshape(1, D), b.reshape(1, D))


def _ln_bwd(dy, x, h, g, alpha, *, name):
    T, D = x.shape
    tr = _tile(T, 512, SUBLANES)
    row = pl.BlockSpec((tr, D), lambda i: (i, 0))
    vec = pl.BlockSpec((1, D), lambda i: (0, 0))

    def body(dy_ref, x_ref, h_ref, g_ref, dr_ref, drb_ref, dg_ref, db_ref):
        @pl.when(pl.program_id(0) == 0)
        def _():
            dg_ref[...] = jnp.zeros_like(dg_ref)
            db_ref[...] = jnp.zeros_like(db_ref)

        xhat, rstd = _ln_stats(alpha * x_ref[...] + h_ref[...])
        dyv = dy_ref[...]
        dxhat = dyv * g_ref[...]
        c1 = jnp.mean(dxhat, axis=-1, keepdims=True)
        c2 = jnp.mean(dxhat * xhat, axis=-1, keepdims=True)
        dr = rstd * (dxhat - c1 - xhat * c2)
        dr_ref[...] = dr
        drb_ref[...] = dr.astype(BF16)
        dg_ref[...] += jnp.sum(dyv * xhat, axis=0, keepdims=True)
        db_ref[...] += jnp.sum(dyv, axis=0, keepdims=True)

    return pl.pallas_call(
        body, name=name, grid=(T // tr,),
        in_specs=[row, row, row, vec], out_specs=[row, row, vec, vec],
        out_shape=[jax.ShapeDtypeStruct((T, D), F32), jax.ShapeDtypeStruct((T, D), BF16),
                   jax.ShapeDtypeStruct((1, D), F32), jax.ShapeDtypeStruct((1, D), F32)],
        compiler_params=_cp(("arbitrary",)),
    )(dy, x, h, g.reshape(1, D))


def _sgu_pieces(zp, lg, lb, E):
    z = _gelu(zp)
    u, v = z[:, :E], z[:, E:]
    vhat, rstd = _ln_stats(v)
    vn = vhat * lg + lb
    return u, vhat, rstd, vn


def _tril_mask():
    t = lax.broadcasted_iota(jnp.int32, (CHUNK, CHUNK), 0)
    s = lax.broadcasted_iota(jnp.int32, (CHUNK, CHUNK), 1)
    return s <= t


def _sgu_fwd(zp, ws, bs, lg, lb, *, name):
    T, E2 = zp.shape
    E = E2 // 2
    G = ws.shape[0]
    cw = E // G

    def body(zp_ref, ws_ref, bs_ref, lg_ref, lb_ref, y_ref):
        u, _, _, vn = _sgu_pieces(zp_ref[...], lg_ref[...], lb_ref[...], E)
        vnb = vn.astype(BF16)
        tril = _tril_mask()
        for g in range(G):
            cols = slice(g * cw, (g + 1) * cw)
            w = jnp.where(tril, ws_ref[g], 0.0).astype(BF16)
            sv = _dot(w, vnb[:, cols], 1, 0) + bs_ref[g]
            y_ref[:, cols] = (u[:, cols] * sv).astype(BF16)

    return pl.pallas_call(
        body, name=name, grid=(T // CHUNK,),
        in_specs=[pl.BlockSpec((CHUNK, E2), lambda n: (n, 0)),
                  pl.BlockSpec((G, CHUNK, CHUNK), lambda n: (0, 0, 0)),
                  pl.BlockSpec((G, CHUNK, 1), lambda n: (0, 0, 0)),
                  pl.BlockSpec((1, E), lambda n: (0, 0)),
                  pl.BlockSpec((1, E), lambda n: (0, 0))],
        out_specs=pl.BlockSpec((CHUNK, E), lambda n: (n, 0)),
        out_shape=jax.ShapeDtypeStruct((T, E), BF16),
        compiler_params=_cp(("parallel",)),
    )(zp, ws, bs.reshape(G, CHUNK, 1), lg.reshape(1, E), lb.reshape(1, E))


def _sgu_bwd(zp, dy, ws, bs, lg, lb, *, name):
    T, E2 = zp.shape
    E = E2 // 2
    G = ws.shape[0]
    cw = E // G
    assert G <= LANES

    def body(zp_ref, dy_ref, ws_ref, bs_ref, lg_ref, lb_ref,
             dzp_ref, dws_ref, dbs_ref, dlg_ref, dlb_ref, dvn_ref):
        @pl.when(pl.program_id(0) == 0)
        def _():
            dws_ref[...] = jnp.zeros_like(dws_ref)
            dbs_ref[...] = jnp.zeros_like(dbs_ref)
            dlg_ref[...] = jnp.zeros_like(dlg_ref)
            dlb_ref[...] = jnp.zeros_like(dlb_ref)

        zpv = zp_ref[...]
        lgv = lg_ref[...]
        u, vhat, rstd, vn = _sgu_pieces(zpv, lgv, lb_ref[...], E)
        gp = _gelu_grad(zpv)
        vnb = vn.astype(BF16)
        tril = _tril_mask()
        lane = lax.broadcasted_iota(jnp.int32, (CHUNK, LANES), 1)
        dbs = dbs_ref[...]
        for g in range(G):
            cols = slice(g * cw, (g + 1) * cw)
            w = jnp.where(tril, ws_ref[g], 0.0).astype(BF16)
            sv = _dot(w, vnb[:, cols], 1, 0) + bs_ref[g]
            dyg = dy_ref[:, cols]
            dzp_ref[:, cols] = (dyg * sv * gp[:, cols]).astype(BF16)
            dsv = dyg * u[:, cols]
            dsvb = dsv.astype(BF16)
            dws_ref[g] += jnp.where(tril, _dot(dsvb, vnb[:, cols], 1, 1), 0.0)
            dvn_ref[:, cols] = _dot(w, dsvb, 0, 0)
            dbs = dbs + jnp.where(lane == g, jnp.sum(dsv, axis=1, keepdims=True), 0.0)
        dbs_ref[...] = dbs
        dvn = dvn_ref[...]
        dlg_ref[...] += jnp.sum(dvn * vhat, axis=0, keepdims=True)
        dlb_ref[...] += jnp.sum(dvn, axis=0, keepdims=True)
        dvhat = dvn * lgv
        c1 = jnp.mean(dvhat, axis=-1, keepdims=True)
        c2 = jnp.mean(dvhat * vhat, axis=-1, keepdims=True)
        dv = rstd * (dvhat - c1 - vhat * c2)
        dzp_ref[:, E:] = (dv * gp[:, E:]).astype(BF16)

    vecE = pl.BlockSpec((1, E), lambda n: (0, 0))
    return pl.pallas_call(
        body, name=name, grid=(T // CHUNK,),
        in_specs=[pl.BlockSpec((CHUNK, E2), lambda n: (n, 0)),
                  pl.BlockSpec((CHUNK, E), lambda n: (n, 0)),
                  pl.BlockSpec((G, CHUNK, CHUNK), lambda n: (0, 0, 0)),
                  pl.BlockSpec((G, CHUNK, 1), lambda n: (0, 0, 0)),
                  vecE, vecE],
        out_specs=[pl.BlockSpec((CHUNK, E2), lambda n: (n, 0)),
                   pl.BlockSpec((G, CHUNK, CHUNK), lambda n: (0, 0, 0)),
                   pl.BlockSpec((CHUNK, LANES), lambda n: (0, 0)),
                   vecE, vecE],
        out_shape=[jax.ShapeDtypeStruct((T, E2), BF16),
                   jax.ShapeDtypeStruct((G, CHUNK, CHUNK), F32),
                   jax.ShapeDtypeStruct((CHUNK, LANES), F32),
                   jax.ShapeDtypeStruct((1, E), F32), jax.ShapeDtypeStruct((1, E), F32)],
        scratch_shapes=[pltpu.VMEM((CHUNK, E), F32)],
        compiler_params=_cp(("arbitrary",)),
    )(zp, dy, ws, bs.reshape(G, CHUNK, 1), lg.reshape(1, E), lb.reshape(1, E))


def _delay(h, k):
    t = lax.broadcasted_iota(jnp.int32, h.shape, 0)
    return jnp.where(t >= k, pltpu.roll(h, k, 0), 0.0)


def _advance(d, k):
    T = d.shape[0]
    t = lax.broadcasted_iota(jnp.int32, d.shape, 0)
    return jnp.where(t < T - k, pltpu.roll(d, T - k, 0), 0.0)


def _conv3(h, w_ref, b_ref):
    return w_ref[2:3, :] * h + w_ref[1:2, :] * _delay(h, 1) + w_ref[0:1, :] * _delay(h, 2) + b_ref[...]


def _convglu_fwd(hup, cw, cb, *, name):
    T, F2 = hup.shape
    F = F2 // 2
    tc = LANES
    nt = F // tc

    def body(ha_ref, hg_ref, wa_ref, wg_ref, ba_ref, bg_ref, o_ref):
        a = _conv3(ha_ref[...], wa_ref, ba_ref)
        g = _conv3(hg_ref[...], wg_ref, bg_ref)
        o_ref[...] = (_gelu(a) * g).astype(BF16)

    col = lambda off: pl.BlockSpec((T, tc), lambda j: (0, j + off))
    w3 = lambda off: pl.BlockSpec((3, tc), lambda j: (0, j + off))
    b1 = lambda off: pl.BlockSpec((1, tc), lambda j: (0, j + off))
    return pl.pallas_call(
        body, name=name, grid=(nt,),
        in_specs=[col(0), col(nt), w3(0), w3(nt), b1(0), b1(nt)],
        out_specs=pl.BlockSpec((T, tc), lambda j: (0, j)),
        out_shape=jax.ShapeDtypeStruct((T, F), BF16),
        compiler_params=_cp(("parallel",)),
    )(hup, hup, cw, cw, cb.reshape(1, F2), cb.reshape(1, F2))


def _convglu_bwd(hup, dact, cw, cb, *, name):
    T, F2 = hup.shape
    F = F2 // 2
    tc = LANES
    nt = F // tc

    def half(h, w_ref, d, dh_ref, dw_ref, db_ref, i):
        dh_ref[i] = (w_ref[2:3, :] * d + w_ref[1:2, :] * _advance(d, 1) + w_ref[0:1, :] * _advance(d, 2)).astype(BF16)
        dw_ref[i, 0:1, :] = jnp.sum(d * _delay(h, 2), axis=0, keepdims=True)
        dw_ref[i, 1:2, :] = jnp.sum(d * _delay(h, 1), axis=0, keepdims=True)
        dw_ref[i, 2:3, :] = jnp.sum(d * h, axis=0, keepdims=True)
        db_ref[i] = jnp.sum(d, axis=0, keepdims=True)

    def body(ha_ref, hg_ref, d_ref, wa_ref, wg_ref, ba_ref, bg_ref, dh_ref, dw_ref, db_ref):
        ha, hg = ha_ref[...], hg_ref[...]
        a = _conv3(ha, wa_ref, ba_ref)
        g = _conv3(hg, wg_ref, bg_ref)
        d = d_ref[...]
        half(ha, wa_ref, d * g * _gelu_grad(a), dh_ref, dw_ref, db_ref, 0)
        half(hg, wg_ref, d * _gelu(a), dh_ref, dw_ref, db_ref, 1)

    col = lambda off: pl.BlockSpec((T, tc), lambda j: (0, j + off))
    w3 = lambda off: pl.BlockSpec((3, tc), lambda j: (0, j + off))
    b1 = lambda off: pl.BlockSpec((1, tc), lambda j: (0, j + off))
    return pl.pallas_call(
        body, name=name, grid=(nt,),
        in_specs=[col(0), col(nt), col(0), w3(0), w3(nt), b1(0), b1(nt)],
        out_specs=[pl.BlockSpec((2, T, tc), lambda j: (0, 0, j)),
                   pl.BlockSpec((2, 3, tc), lambda j: (0, 0, j)),
                   pl.BlockSpec((2, 1, tc), lambda j: (0, 0, j))],
        out_shape=[jax.ShapeDtypeStruct((2, T, F), BF16),
                   jax.ShapeDtypeStruct((2, 3, F), F32),
                   jax.ShapeDtypeStruct((2, 1, F), F32)],
        compiler_params=_cp(("parallel",)),
    )(hup, hup, dact, cw, cw, cb.reshape(1, F2), cb.reshape(1, F2))


def _bucket_table():
    iq = np.arange(BLK)[:, None]
    ik = np.arange(2 * BLK)[None, :]
    delta = iq + BLK - ik
    out = []
    for win, dil in DILATED_GROUPS:
        valid = (delta >= 0) & (delta <= win // dil)
        n = (np.clip(delta, 0, None) * dil).astype(np.int32)
        max_exact = REL_BUCKETS // 2
        nf = np.maximum(n, 1).astype(np.float32)
        large = max_exact + (np.log(nf / np.float32(max_exact)) / np.float32(math.log(REL_MAX_DIST / max_exact))
                             * np.float32(REL_BUCKETS - max_exact)).astype(np.int32)
        large = np.minimum(large, REL_BUCKETS - 1)
        out.append(np.where(valid, np.where(n < max_exact, n, large), -1))
    return np.stack(out).astype(np.int32)


def _band_bias(rel_table, H, *, name):
    bkt = jnp.asarray(_bucket_table())

    def body(tab_ref, bk_ref, o_ref):
        g = pl.program_id(0)
        bk = bk_ref[...]
        prev = lax.broadcasted_iota(jnp.int32, bk.shape, 1) < BLK
        for h in range(H):
            def bucket(c, acc):
                return jnp.where(bk == c, tab_ref[c, g * H + h], acc)

            b = lax.fori_loop(0, REL_BUCKETS, bucket, jnp.zeros(bk.shape, F32))
            b = jnp.where(bk >= 0, b, NEG)
            o_ref[h, 0] = b
            o_ref[h, 1] = jnp.where(prev, NEG, b)

    return pl.pallas_call(
        body, name=name, grid=(N_GROUPS,),
        in_specs=[pl.BlockSpec(memory_space=pltpu.SMEM),
                  pl.BlockSpec((None, BLK, 2 * BLK), lambda g: (g, 0, 0))],
        out_specs=pl.BlockSpec((None, H, 2, BLK, 2 * BLK), lambda g: (g, 0, 0, 0, 0)),
        out_shape=jax.ShapeDtypeStruct((N_GROUPS, H, 2, BLK, 2 * BLK), F32),
        compiler_params=_cp(("parallel",)),
    )(rel_table, bkt)


PAIR = 2 * HEAD_DIM


def _head_masks():
    lane = lax.broadcasted_iota(jnp.int32, (BLK, PAIR), 1)
    return lane < HEAD_DIM, lane >= HEAD_DIM


def _only(mask, a):
    return jnp.where(mask, a, jnp.zeros_like(a))


def _scores(qm, kp, kc, bias):
    scale = HEAD_DIM ** -0.5
    sp = _dot(qm, kp, 1, 1) * scale + bias[:, :BLK]
    sc = _dot(qm, kc, 1, 1) * scale + bias[:, BLK:]
    return sp, sc


def _attn_specs(nb, dil, g, H, HD, qb):
    nbg = nb // dil
    cur = lambda col: pl.BlockSpec((BLK, HD), lambda b: (qb(b), col))
    prev = lambda col: pl.BlockSpec((BLK, HD), lambda b: (jnp.maximum(qb(b) - 1, 0), col))
    stat = pl.BlockSpec((BLK, LANES), lambda b: (qb(b), 0))
    bias = pl.BlockSpec((None, H, None, BLK, 2 * BLK),
                        lambda b: (g, 0, jnp.where(qb(b) % nbg == 0, 1, 0), 0, 0))
    return cur, prev, stat, bias


def _attn_fwd(q, k, v, bias, g, dil, *, name):
    (qa, qc), (ka, kc_), (va, vc_) = q, k, v
    T = qa.shape[0]
    H = bias.shape[1]
    HD = H * HEAD_DIM
    assert H % 2 == 0 and H <= LANES and T % (dil * BLK) == 0
    nb = T // BLK
    cur, prev, stat, bspec = _attn_specs(nb, dil, g, H, HD, lambda b: b)

    def body(q_ref, kp_ref, kc_ref, vp_ref, vc_ref, b_ref, o_ref, l_ref):
        masks = _head_masks()
        lane = lax.broadcasted_iota(jnp.int32, (BLK, LANES), 1)
        lse = jnp.zeros((BLK, LANES), F32)
        for p in range(H // 2):
            cols = slice(p * PAIR, (p + 1) * PAIR)
            qv, kp, kc, vp, vc = q_ref[:, cols], kp_ref[:, cols], kc_ref[:, cols], vp_ref[:, cols], vc_ref[:, cols]
            o = jnp.zeros((BLK, PAIR), F32)
            for e in range(2):
                h = 2 * p + e
                sp, sc = _scores(_only(masks[e], qv), kp, kc, b_ref[h])
                m = jnp.maximum(jnp.max(sp, axis=-1, keepdims=True), jnp.max(sc, axis=-1, keepdims=True))
                pp, pc = jnp.exp(sp - m), jnp.exp(sc - m)
                den = jnp.sum(pp, axis=-1, keepdims=True) + jnp.sum(pc, axis=-1, keepdims=True)
                o = o + _dot((pp / den).astype(BF16), _only(masks[e], vp), 1, 0)
                o = o + _dot((pc / den).astype(BF16), _only(masks[e], vc), 1, 0)
                lse = jnp.where(lane == h, m + jnp.log(den), lse)
            o_ref[:, cols] = o
        l_ref[...] = lse

    return pl.pallas_call(
        body, name=name, grid=(nb,),
        in_specs=[cur(qc), prev(kc_), cur(kc_), prev(vc_), cur(vc_), bspec],
        out_specs=[cur(0), stat],
        out_shape=[jax.ShapeDtypeStruct((T, HD), F32), jax.ShapeDtypeStruct((T, LANES), F32)],
        compiler_params=_cp(("parallel",)),
    )(qa, ka, ka, va, va, bias)


def _attn_combine(os, ls, *, name):
    n = len(os)
    T, HD = os[0].shape
    H = HD // HEAD_DIM
    tr = _tile(T, 512, SUBLANES)
    wide = pl.BlockSpec((tr, HD), lambda i: (i, 0))
    stat = pl.BlockSpec((tr, LANES), lambda i: (i, 0))

    def body(*refs):
        o_refs, l_refs = refs[:n], refs[n:2 * n]
        oc_ref, ocb_ref, lt_ref = refs[2 * n:]
        ls_ = [r[...] for r in l_refs]
        m = ls_[0]
        for l in ls_[1:]:
            m = jnp.maximum(m, l)
        ws = [jnp.exp(l - m) for l in ls_]
        den = ws[0]
        for w in ws[1:]:
            den = den + w
        lt_ref[...] = m + jnp.log(den)
        ws = [w / den for w in ws]
        low = lax.broadcasted_iota(jnp.int32, (tr, PAIR), 1) < HEAD_DIM
        for p in range(H // 2):
            cols = slice(p * PAIR, (p + 1) * PAIR)
            acc = jnp.zeros((tr, PAIR), F32)
            for g in range(n):
                w = jnp.where(low, ws[g][:, 2 * p:2 * p + 1], ws[g][:, 2 * p + 1:2 * p + 2])
                acc = acc + w * o_refs[g][:, cols]
            oc_ref[:, cols] = acc
            ocb_ref[:, cols] = acc.astype(BF16)

    return pl.pallas_call(
        body, name=name, grid=(T // tr,),
        in_specs=[wide] * n + [stat] * n, out_specs=[wide, wide, stat],
        out_shape=[jax.ShapeDtypeStruct((T, HD), F32), jax.ShapeDtypeStruct((T, HD), BF16),
                   jax.ShapeDtypeStruct((T, LANES), F32)],
        compiler_params=_cp(("parallel",)),
    )(*os, *ls)


def _attn_bwd_prep(do, oc, *, name):
    T, HD = do.shape
    H = HD // HEAD_DIM
    tr = _tile(T, 512, SUBLANES)
    wide = pl.BlockSpec((tr, HD), lambda i: (i, 0))
    stat = pl.BlockSpec((tr, LANES), lambda i: (i, 0))

    def body(do_ref, oc_ref, dl_ref, dob_ref):
        low = lax.broadcasted_iota(jnp.int32, (tr, PAIR), 1) < HEAD_DIM
        lane = lax.broadcasted_iota(jnp.int32, (tr, LANES), 1)
        dl = jnp.zeros((tr, LANES), F32)
        for p in range(H // 2):
            cols = slice(p * PAIR, (p + 1) * PAIR)
            d = do_ref[:, cols]
            prod = d * oc_ref[:, cols]
            dl = jnp.where(lane == 2 * p, jnp.sum(jnp.where(low, prod, 0.0), axis=-1, keepdims=True), dl)
            dl = jnp.where(lane == 2 * p + 1, jnp.sum(jnp.where(low, 0.0, prod), axis=-1, keepdims=True), dl)
            dob_ref[:, cols] = d.astype(BF16)
        dl_ref[...] = dl

    return pl.pallas_call(
        body, name=name, grid=(T // tr,),
        in_specs=[wide, wide], out_specs=[stat, wide],
        out_shape=[jax.ShapeDtypeStruct((T, LANES), F32), jax.ShapeDtypeStruct((T, HD), BF16)],
        compiler_params=_cp(("parallel",)),
    )(do, oc)


def _attn_bwd(q, k, v, bias, g, dil, do, lse, delta, *, name):
    (qa, qc), (ka, kc_), (va, vc_) = q, k, v
    T, HD = do.shape
    H = HD // HEAD_DIM
    nb = T // BLK
    scale = HEAD_DIM ** -0.5
    cur, prev, stat, bspec = _attn_specs(nb, dil, g, H, HD, lambda b: jnp.minimum(b, nb - 1))
    late = pl.BlockSpec((BLK, HD), lambda b: (jnp.maximum(b - 1, 0), 0))

    def body(q_ref, kp_ref, kc_ref, vp_ref, vc_ref, b_ref, do_ref, l_ref, dl_ref,
             dq_ref, dk_ref, dv_ref, db_ref, ck_ref, cv_ref):
        b = pl.program_id(0)

        @pl.when(b == 0)
        def _():
            db_ref[...] = jnp.zeros_like(db_ref)
            ck_ref[...] = jnp.zeros_like(ck_ref)
            cv_ref[...] = jnp.zeros_like(cv_ref)

        @pl.when(b < nb)
        def _():
            masks = _head_masks()
            for p in range(H // 2):
                cols = slice(p * PAIR, (p + 1) * PAIR)
                qv, kp, kc, vp, vc, dov = (q_ref[:, cols], kp_ref[:, cols], kc_ref[:, cols], vp_ref[:, cols],
                                           vc_ref[:, cols], do_ref[:, cols])
                zero = jnp.zeros((BLK, PAIR), F32)
                dq, dkp, dkc, dvp, dvc = zero, zero, zero, zero, zero
                for e in range(2):
                    h = 2 * p + e
                    qm, dom = _only(masks[e], qv), _only(masks[e], dov)
                    sp, sc = _scores(qm, kp, kc, b_ref[h])
                    l, dl = l_ref[:, h:h + 1], dl_ref[:, h:h + 1]
                    pp, pc = jnp.exp(sp - l), jnp.exp(sc - l)
                    dsp = pp * (_dot(dom, vp, 1, 1) - dl)
                    dsc = pc * (_dot(dom, vc, 1, 1) - dl)
                    db_ref[h, :, :BLK] += dsp
                    db_ref[h, :, BLK:] += dsc
                    dspb, dscb = dsp.astype(BF16), dsc.astype(BF16)
                    dq = dq + _dot(dspb, _only(masks[e], kp), 1, 0) + _dot(dscb, _only(masks[e], kc), 1, 0)
                    dkp = dkp + _dot(dspb, qm, 0, 0)
                    dkc = dkc + _dot(dscb, qm, 0, 0)
                    dvp = dvp + _dot(pp.astype(BF16), dom, 0, 0)
                    dvc = dvc + _dot(pc.astype(BF16), dom, 0, 0)
                dq_ref[:, cols] = (scale * dq).astype(BF16)
                dk_ref[:, cols] = ck_ref[:, cols] + scale * dkp
                dv_ref[:, cols] = cv_ref[:, cols] + dvp
                ck_ref[:, cols] = scale * dkc
                cv_ref[:, cols] = dvc

        @pl.when(b == nb)
        def _():
            dk_ref[...] = ck_ref[...]
            dv_ref[...] = cv_ref[...]

    f32 = jax.ShapeDtypeStruct((T, HD), F32)
    return pl.pallas_call(
        body, name=name, grid=(nb + 1,),
        in_specs=[cur(qc), prev(kc_), cur(kc_), prev(vc_), cur(vc_), bspec, cur(0), stat, stat],
        out_specs=[cur(0), late, late, pl.BlockSpec((H, BLK, 2 * BLK), lambda b: (0, 0, 0))],
        out_shape=[jax.ShapeDtypeStruct((T, HD), BF16), f32, f32,
                   jax.ShapeDtypeStruct((H, BLK, 2 * BLK), F32)],
        scratch_shapes=[pltpu.VMEM((BLK, HD), F32), pltpu.VMEM((BLK, HD), F32)],
        compiler_params=_cp(("arbitrary",)),
    )(qa, ka, ka, va, va, bias, do, lse, delta)


def _rel_grad(dbs, *, name):
    H = dbs[0].shape[1]
    n = len(dbs)
    assert H <= LANES
    bkt = jnp.asarray(_bucket_table())

    def body(*refs):
        bk = refs[n][...]
        o_ref = refs[n + 1]
        db = refs[0][...]
        for r in refs[1:n]:
            db = db + r[...]
        row = lax.broadcasted_iota(jnp.int32, (REL_BUCKETS, LANES), 0)
        col = lax.broadcasted_iota(jnp.int32, (REL_BUCKETS, LANES), 1)

        def bucket(c, acc):
            sel = bk == c
            for h in range(H):
                val = jnp.sum(jnp.where(sel, db[h], 0.0))
                acc = jnp.where((row == c) & (col == h), val, acc)
            return acc

        o_ref[...] = lax.fori_loop(0, REL_BUCKETS, bucket, jnp.zeros((REL_BUCKETS, LANES), F32))

    dspec = pl.BlockSpec((None, H, BLK, 2 * BLK), lambda g: (g, 0, 0, 0))
    return pl.pallas_call(
        body, name=name, grid=(N_GROUPS,),
        in_specs=[dspec] * n + [pl.BlockSpec((None, BLK, 2 * BLK), lambda g: (g, 0, 0))],
        out_specs=pl.BlockSpec((None, REL_BUCKETS, LANES), lambda g: (g, 0, 0)),
        out_shape=jax.ShapeDtypeStruct((N_GROUPS, REL_BUCKETS, LANES), F32),
        compiler_params=_cp(("parallel",)),
    )(*dbs, bkt)


def _to_group(a, d):
    T, C = a.shape
    return a if d == 1 else a.reshape(T // d, d, C).transpose(1, 0, 2).reshape(T, C)


def _from_group(a, d):
    T, C = a.shape
    return a if d == 1 else a.reshape(d, T // d, C).transpose(1, 0, 2).reshape(T, C)


def _rows_view(a):
    a2 = a.reshape(-1, a.shape[-1])
    R, C = a2.shape
    tr = _tile(R, max(SUBLANES, (1 << 18) // C), SUBLANES)
    return a2, R, C, tr


def _addn(xs, out_dtype, *, name):
    shape = xs[0].shape
    x2s = [_rows_view(x)[0] for x in xs]
    _, R, C, tr = _rows_view(xs[0])
    spec = pl.BlockSpec((tr, C), lambda i: (i, 0))

    def body(*refs):
        acc = refs[0][...].astype(F32)
        for r in refs[1:-1]:
            acc = acc + r[...].astype(F32)
        refs[-1][...] = acc.astype(out_dtype)

    out = pl.pallas_call(
        body, name=name, grid=(R // tr,),
        in_specs=[spec] * len(xs), out_specs=spec,
        out_shape=jax.ShapeDtypeStruct((R, C), out_dtype),
        compiler_params=_cp(("parallel",)),
    )(*x2s)
    return out.reshape(shape)


def _slot_sum(b_ref, own, my_id):
    acc = None
    for s in range(b_ref.shape[0]):
        term = jnp.where(my_id == s, own, b_ref[s].astype(F32))
        acc = term if acc is None else acc + term
    return acc


def _part_geometry(shape, axis):
    L, R, C = shape
    if axis == COL:
        C //= N_CHIPS
    else:
        R //= N_CHIPS
    if L == 1:
        R //= 2
    return max(L // 2, 1), R, C


def _sum_parts(buf, grad, axis, where, *, name):
    n, Lp, Rp, Cp = buf.shape
    L = grad.shape[0]
    assert (Lp, Rp, Cp) == _part_geometry(grad.shape, axis)
    tr = _tile(Rp, max(2 * SUBLANES, (1 << 17) // Cp), 2 * SUBLANES)
    nr = Rp // tr

    def own_map(l, i, w):
        if L == 1:
            return (0, (2 * w[0] + w[1]) * nr + i, 0)
        if axis == COL:
            return (w[1] * Lp + l, i, w[0])
        return (w[1] * Lp + l, w[0] * nr + i, 0)

    def body(w_ref, b_ref, g_ref, o_ref):
        o_ref[...] = _slot_sum(b_ref, g_ref[...].astype(F32), w_ref[2])

    return pl.pallas_call(
        body, name=name,
        grid_spec=pltpu.PrefetchScalarGridSpec(
            num_scalar_prefetch=1, grid=(Lp, nr),
            in_specs=[pl.BlockSpec((n, None, tr, Cp), lambda l, i, w: (0, l, i, 0)),
                      pl.BlockSpec((None, tr, Cp), own_map)],
            out_specs=pl.BlockSpec((None, None, tr, Cp), lambda l, i, w: (w[1], l, i, 0))),
        out_shape=jax.ShapeDtypeStruct((2, Lp, Rp, Cp), F32),
        compiler_params=_cp(("parallel", "parallel")),
    )(where, buf, grad)


def _sum_small(buf, own, where, *, name):
    n, R, C = buf.shape
    tr = _tile(R, 512, SUBLANES)

    def body(w_ref, b_ref, g_ref, o_ref):
        o_ref[...] = _slot_sum(b_ref, g_ref[...], w_ref[2])

    return pl.pallas_call(
        body, name=name,
        grid_spec=pltpu.PrefetchScalarGridSpec(
            num_scalar_prefetch=1, grid=(R // tr,),
            in_specs=[pl.BlockSpec((n, tr, C), lambda i, w: (0, i, 0)),
                      pl.BlockSpec((tr, C), lambda i, w: (i, 0))],
            out_specs=pl.BlockSpec((tr, C), lambda i, w: (i, 0))),
        out_shape=jax.ShapeDtypeStruct((R, C), F32),
        compiler_params=_cp(("parallel",)),
    )(where, buf, own)


def _place_shard(shard, layer, axis, where, out_dtype, *, name):
    _, R, C = shard.shape
    tr = _tile(R, max(2 * SUBLANES, (1 << 18) // C), 2 * SUBLANES)
    nr = R // tr
    if axis == COL:
        full, out_map = (1, R, C * N_CHIPS), (lambda i, w: (0, i, w[0]))
    else:
        full, out_map = (1, R * N_CHIPS, C), (lambda i, w: (0, w[0] * nr + i, 0))

    def body(w_ref, x_ref, o_ref):
        o_ref[...] = x_ref[...].astype(out_dtype)

    return pl.pallas_call(
        body, name=name,
        grid_spec=pltpu.PrefetchScalarGridSpec(
            num_scalar_prefetch=1, grid=(nr,),
            in_specs=[pl.BlockSpec((None, tr, C), lambda i, w: (layer, i, 0))],
            out_specs=pl.BlockSpec((None, tr, C), out_map)),
        out_shape=jax.ShapeDtypeStruct(full, out_dtype),
        compiler_params=_cp(("parallel",)),
    )(where, shard)


def _loss_grad(y, tgt, *, name):
    T, D = y.shape
    tr = _tile(T, 512, SUBLANES)
    row = pl.BlockSpec((tr, D), lambda i: (i, 0))

    def body(y_ref, t_ref, dy_ref, l_ref, acc_ref):
        i = pl.program_id(0)

        @pl.when(i == 0)
        def _():
            acc_ref[...] = jnp.zeros_like(acc_ref)

        err = y_ref[...] - t_ref[...]
        dy_ref[...] = err * (1.0 / D)
        acc_ref[...] += jnp.sum(err * err, axis=0, keepdims=True)

        @pl.when(i == T // tr - 1)
        def _():
            tot = jnp.sum(acc_ref[...], axis=1, keepdims=True) * (0.5 / D)
            l_ref[...] = jnp.broadcast_to(tot, (1, LANES))

    return pl.pallas_call(
        body, name=name, grid=(T // tr,),
        in_specs=[row, row],
        out_specs=[row, pl.BlockSpec((1, LANES), lambda i: (0, 0))],
        out_shape=[jax.ShapeDtypeStruct((T, D), F32), jax.ShapeDtypeStruct((1, LANES), F32)],
        scratch_shapes=[pltpu.VMEM((1, D), F32)],
        compiler_params=_cp(("arbitrary",)),
    )(y, tgt)


def _adamw(w, g, m, v, *, name):
    shape = w.shape
    w2, R, C, tr = _rows_view(w)
    spec = pl.BlockSpec((tr, C), lambda i: (i, 0))

    def body(w_ref, g_ref, m_ref, v_ref, d_ref, nm_ref, nv_ref):
        gv = g_ref[...]
        nm = ADAM_B1 * m_ref[...] + (1.0 - ADAM_B1) * gv
        nv = ADAM_B2 * v_ref[...] + (1.0 - ADAM_B2) * (gv * gv)
        m_hat = nm / (1.0 - ADAM_B1 ** ADAM_STEP)
        v_hat = nv / (1.0 - ADAM_B2 ** ADAM_STEP)
        d_ref[...] = -ADAM_LR * (m_hat / (jnp.sqrt(v_hat) + ADAM_EPS) + ADAM_WD * w_ref[...])
        nm_ref[...] = nm
        nv_ref[...] = nv

    shp = jax.ShapeDtypeStruct((R, C), F32)
    outs = pl.pallas_call(
        body, name=name, grid=(R // tr,),
        in_specs=[spec] * 4, out_specs=[spec] * 3, out_shape=[shp] * 3,
        compiler_params=_cp(("parallel",)),
    )(w2, g.reshape(R, C), m.reshape(R, C), v.reshape(R, C))
    return tuple(o.reshape(shape) for o in outs)


def _pack(arrs):
    flat = jnp.concatenate([a.reshape(-1).astype(F32) for a in arrs])
    unit = SUBLANES * LANES
    pad = (-flat.shape[0]) % unit
    return jnp.pad(flat, (0, pad)).reshape(-1, LANES)


def _unpack(buf, shapes):
    flat = buf.reshape(-1)
    out, off = [], 0
    for s in shapes:
        n = int(np.prod(s))
        out.append(flat[off:off + n].reshape(s))
        off += n
    return out


def _me():
    return lax.axis_index("x"), lax.axis_index("y"), lax.axis_index("c")


def _flip(pos, k):
    x, y, c = pos
    return (1 - x if k & 4 else x, 1 - y if k & 2 else y, 1 - c if k & 1 else c)


HBM = pl.BlockSpec(memory_space=pltpu.HBM)

COL, ROW = -1, -2


def _shard_window(ref, axis, j, n):
    start = pl.multiple_of(j * n, n)
    if axis == COL:
        return ref.at[:, :, pl.ds(start, n)]
    return ref.at[:, pl.ds(start, n), :]


GATHER_PEERS = (2, 4, 6)
SEM = pl.BlockSpec(memory_space=pltpu.SEMAPHORE)
EFFECT = pltpu.SideEffectType.DATAFLOW_SIDE_EFFECTING


def _gather_copy(buf, axis, i, pi, chip, me, send_sems, recv_sems):
    win = _shard_window(buf, axis, chip, buf.shape[axis] // N_CHIPS)
    k = len(GATHER_PEERS) * i + pi
    return pltpu.make_async_remote_copy(
        src_ref=win, dst_ref=win, send_sem=send_sems.at[k], recv_sem=recv_sems.at[k],
        device_id=_flip(me, GATHER_PEERS[pi]), device_id_type=MESH)


def _gather_start(fulls, axes, *, name):
    n = len(fulls)
    n_sem = len(GATHER_PEERS) * n

    def body(*refs):
        send_sems, recv_sems = refs[n], refs[n + 1]
        bufs = refs[n + 2:]
        me = _me()
        for i in range(n):
            for pi in range(len(GATHER_PEERS)):
                _gather_copy(bufs[i], axes[i], i, pi, 2 * me[0] + me[1], me, send_sems, recv_sems).start()

    outs = pl.pallas_call(
        body, name=name,
        in_specs=[HBM] * n, out_specs=[SEM, SEM] + [HBM] * n,
        out_shape=[pltpu.SemaphoreType.DMA((n_sem,)), pltpu.SemaphoreType.DMA((n_sem,))]
        + [pltpu.HBM(f.shape, f.dtype) for f in fulls],
        input_output_aliases={i: 2 + i for i in range(n)},
        compiler_params=pltpu.CompilerParams(has_side_effects=EFFECT),
    )(*[pltpu.with_memory_space_constraint(f, pltpu.HBM) for f in fulls])
    return outs[0], outs[1], list(outs[2:])


def _gather_wait(send_sems, recv_sems, bufs, axes, idxs, after, *, name):
    m = len(bufs)

    def body(*refs):
        ss, rs = refs[m], refs[m + 1]
        outs = refs[m + 3:]
        me = _me()
        for t, i in enumerate(idxs):
            for pi, k in enumerate(GATHER_PEERS):
                px, py, _ = _flip(me, k)
                _gather_copy(outs[t], axes[t], i, pi, 2 * me[0] + me[1], me, ss, rs).wait_send()
                _gather_copy(outs[t], axes[t], i, pi, 2 * px + py, me, ss, rs).wait_recv()

    return pl.pallas_call(
        body, name=name,
        in_specs=[HBM] * m + [SEM, SEM, pl.BlockSpec(memory_space=pl.ANY)], out_specs=[HBM] * m,
        out_shape=[pltpu.HBM(b.shape, b.dtype) for b in bufs],
        input_output_aliases={t: t for t in range(m)},
        compiler_params=pltpu.CompilerParams(has_side_effects=EFFECT),
    )(*bufs, send_sems, recv_sems, after)


def _scatter_grads(grads, axes, small, *, name):
    n = len(grads)
    part_shapes = [_part_geometry(gr.shape, ax) for gr, ax in zip(grads, axes)]
    out_shapes = [jax.ShapeDtypeStruct((N_DEV,) + s, gr.dtype) for s, gr in zip(part_shapes, grads)]
    out_shapes.append(jax.ShapeDtypeStruct((N_DEV,) + small.shape, small.dtype))
    n_t = n + 1

    def body(*refs):
        srcs, dsts = refs[:n_t], refs[n_t:2 * n_t]
        send_sems, recv_sems = refs[2 * n_t:]
        me = _me()
        my_id = 4 * me[0] + 2 * me[1] + me[2]

        def part(i, pos):
            if i == n:
                return srcs[i]
            px, py, pc = pos
            L = srcs[i].shape[0]
            Lp, Rp, Cp = part_shapes[i]
            chip = 2 * px + py
            if L == 1:
                assert axes[i] == ROW
                return srcs[i].at[:, pl.ds(pl.multiple_of((2 * chip + pc) * Rp, Rp), Rp), :]
            lay = pl.ds(pc * Lp, Lp)
            if axes[i] == COL:
                return srcs[i].at[lay, :, pl.ds(pl.multiple_of(chip * Cp, Cp), Cp)]
            return srcs[i].at[lay, pl.ds(pl.multiple_of(chip * Rp, Rp), Rp), :]

        sends = []
        for k in range(1, N_DEV):
            peer = _flip(me, k)
            for i in range(n_t):
                sends.append(pltpu.make_async_remote_copy(
                    src_ref=part(i, peer), dst_ref=dsts[i].at[my_id],
                    send_sem=send_sems.at[(k - 1) * n_t + i], recv_sem=recv_sems.at[(k - 1) * n_t + i],
                    device_id=peer, device_id_type=MESH))
        for cp in sends:
            cp.start()
        for k in range(1, N_DEV):
            px, py, pc = _flip(me, k)
            for i in range(n_t):
                pltpu.make_async_remote_copy(
                    src_ref=part(i, me), dst_ref=dsts[i].at[4 * px + 2 * py + pc],
                    send_sem=send_sems.at[(k - 1) * n_t + i], recv_sem=recv_sems.at[(k - 1) * n_t + i],
                    device_id=(px, py, pc), device_id_type=MESH).wait_recv()
        for cp in sends:
            cp.wait_send()

    n_sem = (N_DEV - 1) * n_t
    return pl.pallas_call(
        body, name=name,
        in_specs=[HBM] * n_t, out_specs=[HBM] * n_t, out_shape=out_shapes,
        scratch_shapes=[pltpu.SemaphoreType.DMA((n_sem,)), pltpu.SemaphoreType.DMA((n_sem,))],
    )(*grads, small)


def _pair_gather(halves, *, name):
    n = len(halves)

    def body(*refs):
        bufs = refs[n:2 * n]
        send_sems, recv_sems = refs[2 * n:]
        me = _me()
        c = me[2]
        sib = _flip(me, 1)

        def copy(i, half):
            return pltpu.make_async_remote_copy(
                src_ref=bufs[i].at[half], dst_ref=bufs[i].at[half], send_sem=send_sems.at[i],
                recv_sem=recv_sems.at[i], device_id=sib, device_id_type=MESH)

        sends = [copy(i, c) for i in range(n)]
        for cp in sends:
            cp.start()
        for i in range(n):
            copy(i, 1 - c).wait_recv()
        for cp in sends:
            cp.wait_send()

    return pl.pallas_call(
        body, name=name,
        in_specs=[HBM] * n, out_specs=[HBM] * n,
        out_shape=[jax.ShapeDtypeStruct(h.shape, h.dtype) for h in halves],
        input_output_aliases={i: i for i in range(n)},
        scratch_shapes=[pltpu.SemaphoreType.DMA((n,)), pltpu.SemaphoreType.DMA((n,))],
    )(*halves)


def _local_step(x, tgt, P, fetch, n_layers):
    T, D = x.shape
    P = dict(P)
    W = {}
    depth = P["ffn_conv_b"].shape[0]
    n_a = P["a_w_s"].shape[0]
    alpha = (2 * depth) ** 0.25
    H = P["rel_table"].shape[1] // N_GROUPS
    HD = H * HEAD_DIM
    dils = [d for _, d in DILATED_GROUPS]
    assert T % (dils[-1] * BLK) == 0

    def arrive(stage, after):
        w_new, p_new = fetch(stage, after)
        W.update(w_new)
        P.update(p_new)

    bias = _band_bias(P["rel_table"], H, name="band_bias")
    saved = []
    xf, xb = x, x.astype(BF16)
    kg = vg = None
    for i in range(depth):
        s = {"x": xf, "xb": xb}
        arrive(f"{i}a", xf)
        if i < n_a:
            s["zp"] = _mm(xb, W["a_w_in", i], name=f"a{i}_in")
            arrive(f"{i}b", s["zp"])
            s["y"] = _sgu_fwd(s["zp"], P["a_w_s"][i], P["a_b_s"][i], P["a_ln_g"][i], P["a_ln_b"][i], name=f"a{i}_sgu")
            s["h"] = _mm(s["y"], W["a_w_out", i], name=f"a{i}_out")
        else:
            j = i - n_a
            if j == 0:
                kv = _mm(xb, W["kv_w", 0], out_dtype=BF16, name="kv_proj")
                kg = [(kv, 0) if d == 1 else (_to_group(kv[:, :HD], d), 0) for d in dils]
                vg = [(kv, 1) if d == 1 else (_to_group(kv[:, HD:], d), 0) for d in dils]
            q = _mm(xb, W["b_w_q", j], out_dtype=BF16, name=f"b{j}_q")
            s["qg"] = [(q, g) if d == 1 else (_to_group(q[:, g * HD:(g + 1) * HD], d), 0) for g, d in enumerate(dils)]
            os, ls = [], []
            for g, d in enumerate(dils):
                o_g, l_g = _attn_fwd(s["qg"][g], kg[g], vg[g], bias, g, d, name=f"b{j}_attn{g}")
                os.append(_from_group(o_g, d))
                ls.append(_from_group(l_g, d))
            s["oc"], s["ob"], s["lse"] = _attn_combine(os, ls, name=f"b{j}_comb")
            s["h"] = _mm(s["ob"], W["b_w_o", j], name=f"b{j}_o")
        s["x1"], s["x1b"] = _add_ln(xf, s["h"], P["ln_g"][i, 0], P["ln_b"][i, 0], alpha, name=f"l{i}_ln1")
        s["hup"] = _mm(s["x1b"], W["ffn_w_up", i], name=f"l{i}_up")
        s["act"] = _convglu_fwd(s["hup"], P["ffn_conv_w"][i], P["ffn_conv_b"][i], name=f"l{i}_glu")
        s["f"] = _mm(s["act"], W["ffn_w_down", i], name=f"l{i}_down")
        xf, xb = _add_ln(s["x1"], s["f"], P["ln_g"][i, 1], P["ln_b"][i, 1], alpha, name=f"l{i}_ln2")
        saved.append(s)

    G, loss_row = _loss_grad(xf, tgt, name="loss")

    gw = {}

    def dw(key, layer, a, b, name, **kw):
        gw[key] = _mm(a, b, ta=True, out_dtype=BF16, out_into=(gw.get(key), n_layers[key], layer), name=name, **kw)

    gp = {k: [None] * n_a for k in ("a_ln_g", "a_ln_b", "a_w_s", "a_b_s")}
    gp.update({k: [None] * depth for k in ("ffn_conv_w", "ffn_conv_b", "ln_g", "ln_b")})
    dk_parts, dv_parts, dbias_parts = [], [], []
    for i in reversed(range(depth)):
        s = saved[i]
        dr2, dr2b, dg2, db2 = _ln_bwd(G, s["x1"], s["f"], P["ln_g"][i, 1], alpha, name=f"l{i}_ln2_bwd")
        dw("ffn_w_down", i, s["act"], dr2b, f"l{i}_down_dw")
        dact = _mm(dr2b, W["ffn_w_down", i], tb=True, name=f"l{i}_down_dx")
        dhup, dcw, dcb = _convglu_bwd(s["hup"], dact, P["ffn_conv_w"][i], P["ffn_conv_b"][i], name=f"l{i}_glu_bwd")
        gp["ffn_conv_w"][i] = dcw.transpose(1, 0, 2).reshape(dcw.shape[1], -1)
        gp["ffn_conv_b"][i] = dcb.reshape(-1)
        dw("ffn_w_up", i, s["x1b"], dhup, f"l{i}_up_dw", b_halves=True)
        G1 = _mm(dhup, W["ffn_w_up", i], tb=True, a_halves=True, add=dr2, add_scale=alpha, name=f"l{i}_up_dx")
        dr1, dr1b, dg1, db1 = _ln_bwd(G1, s["x"], s["h"], P["ln_g"][i, 0], alpha, name=f"l{i}_ln1_bwd")
        gp["ln_g"][i] = jnp.concatenate([dg1, dg2], axis=0)
        gp["ln_b"][i] = jnp.concatenate([db1, db2], axis=0)
        if i < n_a:
            dw("a_w_out", i, s["y"], dr1b, f"a{i}_out_dw")
            dy = _mm(dr1b, W["a_w_out", i], tb=True, name=f"a{i}_out_dx")
            dzp, dws, dbs, dlg, dlb = _sgu_bwd(s["zp"], dy, P["a_w_s"][i], P["a_b_s"][i], P["a_ln_g"][i],
                                               P["a_ln_b"][i], name=f"a{i}_sgu_bwd")
            gp["a_w_s"][i], gp["a_b_s"][i] = dws, dbs[:, :dws.shape[0]].T
            gp["a_ln_g"][i], gp["a_ln_b"][i] = dlg[0], dlb[0]
            dw("a_w_in", i, s["xb"], dzp, f"a{i}_in_dw")
            G = _mm(dzp, W["a_w_in", i], tb=True, add=dr1, add_scale=alpha, name=f"a{i}_in_dx")
        else:
            j = i - n_a
            dw("b_w_o", j, s["ob"], dr1b, f"b{j}_o_dw")
            do_tok = _mm(dr1b, W["b_w_o", j], tb=True, name=f"b{j}_o_dx")
            delta, dob = _attn_bwd_prep(do_tok, s["oc"], name=f"b{j}_prep")
            dqs, dbs = [], []
            for g, d in enumerate(dils):
                dq, dk, dv, db = _attn_bwd(s["qg"][g], kg[g], vg[g], bias, g, d, _to_group(dob, d),
                                           _to_group(s["lse"], d), _to_group(delta, d), name=f"b{j}_attn{g}_bwd")
                dqs.append(_from_group(dq, d))
                dk_parts.append(_from_group(dk, d))
                dv_parts.append(_from_group(dv, d))
                dbs.append(db)
            dbias_parts.append(jnp.stack(dbs))
            dq_tok = jnp.concatenate(dqs, axis=1)
            dw("b_w_q", j, s["xb"], dq_tok, f"b{j}_q_dw")
            G = _mm(dq_tok, W["b_w_q", j], tb=True, add=dr1, add_scale=alpha, name=f"b{j}_q_dx")
            if j == 0:
                dkv = jnp.concatenate([_addn(dk_parts, BF16, name="dk_sum"), _addn(dv_parts, BF16, name="dv_sum")], axis=1)
                dw("kv_w", 0, s["xb"], dkv, "kv_dw")
                G = _mm(dkv, W["kv_w", 0], tb=True, add=G, add_scale=1.0, name="kv_dx")
    rel = _rel_grad(dbias_parts, name="rel_grad")
    grel = rel[:, :, :H].transpose(1, 0, 2).reshape(REL_BUCKETS, N_GROUPS * H)
    gsmall = {k: jnp.stack(v) for k, v in gp.items()}
    gsmall["rel_table"] = grel
    return loss_row, G, gw, gsmall


BIG = (("a_w_in", COL), ("a_w_out", ROW), ("kv_w", ROW), ("b_w_q", COL), ("b_w_o", COL),
       ("ffn_w_up", COL), ("ffn_w_down", ROW))
SMALL_SHARDED = ("a_ln_g", "a_ln_b", "ffn_conv_w", "ln_g", "ln_b")
SMALL_REPLICATED = ("a_w_s", "a_b_s", "rel_table", "ffn_conv_b")
WEIGHTS = ("a_w_in", "a_ln_g", "a_ln_b", "a_w_s", "a_b_s", "a_w_out", "kv_w", "b_w_q", "b_w_o", "rel_table",
           "ffn_w_up", "ffn_conv_w", "ffn_conv_b", "ffn_w_down", "ln_g", "ln_b")


def kernel(x, a_w_in, a_ln_g, a_ln_b, a_w_s, a_b_s, a_w_out, kv_w, b_w_q, b_w_o, rel_table, ffn_w_up, ffn_conv_w, ffn_conv_b, ffn_w_down, ln_g, ln_b, loss_target, m_a_w_in, m_a_ln_g, m_a_ln_b, m_a_w_s, m_a_b_s, m_a_w_out, m_kv_w, m_b_w_q, m_b_w_o, m_rel_table, m_ffn_w_up, m_ffn_conv_w, m_ffn_conv_b, m_ffn_w_down, m_ln_g, m_ln_b, v_a_w_in, v_a_ln_g, v_a_ln_b, v_a_w_s, v_a_b_s, v_a_w_out, v_kv_w, v_b_w_q, v_b_w_o, v_rel_table, v_ffn_w_up, v_ffn_conv_w, v_ffn_conv_b, v_ffn_w_down, v_ln_g, v_ln_b):
    w = dict(a_w_in=a_w_in, a_ln_g=a_ln_g, a_ln_b=a_ln_b, a_w_s=a_w_s, a_b_s=a_b_s, a_w_out=a_w_out, kv_w=kv_w,
             b_w_q=b_w_q, b_w_o=b_w_o, rel_table=rel_table, ffn_w_up=ffn_w_up, ffn_conv_w=ffn_conv_w,
             ffn_conv_b=ffn_conv_b, ffn_w_down=ffn_w_down, ln_g=ln_g, ln_b=ln_b)
    m = dict(a_w_in=m_a_w_in, a_ln_g=m_a_ln_g, a_ln_b=m_a_ln_b, a_w_s=m_a_w_s, a_b_s=m_a_b_s, a_w_out=m_a_w_out,
             kv_w=m_kv_w, b_w_q=m_b_w_q, b_w_o=m_b_w_o, rel_table=m_rel_table, ffn_w_up=m_ffn_w_up,
             ffn_conv_w=m_ffn_conv_w, ffn_conv_b=m_ffn_conv_b, ffn_w_down=m_ffn_w_down, ln_g=m_ln_g, ln_b=m_ln_b)
    v = dict(a_w_in=v_a_w_in, a_ln_g=v_a_ln_g, a_ln_b=v_a_ln_b, a_w_s=v_a_w_s, a_b_s=v_a_b_s, a_w_out=v_a_w_out,
             kv_w=v_kv_w, b_w_q=v_b_w_q, b_w_o=v_b_w_o, rel_table=v_rel_table, ffn_w_up=v_ffn_w_up,
             ffn_conv_w=v_ffn_conv_w, ffn_conv_b=v_ffn_conv_b, ffn_w_down=v_ffn_w_down, ln_g=v_ln_g, ln_b=v_ln_b)
    chip = 2 * lax.axis_index("x") + lax.axis_index("y")

    big_names = [n for n, _ in BIG]
    big_axes = [a for _, a in BIG]
    where = jnp.stack([chip, lax.axis_index("c"), 2 * chip + lax.axis_index("c")]).astype(jnp.int32)
    shards = [w[n].reshape((1,) + w[n].shape) if w[n].ndim == 2 else w[n] for n in big_names]
    small_shapes = [w[n].shape for n in SMALL_SHARDED]
    small = _pack([w[n] for n in SMALL_SHARDED])
    shard_of = dict(zip(big_names, shards))
    axis_of = dict(BIG)
    n_layers = {n: s.shape[0] for n, s in shard_of.items()}
    n_a = w["a_w_s"].shape[0]
    stages = {"0a": [("a_w_in", 0), ("small", 0)], "0b": [("a_w_out", 0), ("ffn_w_up", 0), ("ffn_w_down", 0)]}
    for i in range(1, w["ffn_w_up"].shape[0]):
        if i < n_a:
            mixer = [("a_w_in", i), ("a_w_out", i)]
        else:
            mixer = ([("kv_w", 0)] if i == n_a else []) + [("b_w_q", i - n_a), ("b_w_o", i - n_a)]
        stages[f"{i}a"] = mixer + [("ffn_w_up", i), ("ffn_w_down", i)]
    order = [key for st in stages.values() for key in st]
    assert len(order) == sum(n_layers.values()) + 1
    placed, axes = [], []
    for n, l in order:
        if n == "small":
            placed.append(_place_shard(small[None], 0, ROW, where, F32, name="place_small"))
            axes.append(ROW)
        else:
            placed.append(_place_shard(shard_of[n], l, axis_of[n], where, BF16, name=f"place_{n}{l}"))
            axes.append(axis_of[n])
    send_sems, recv_sems, flying = _gather_start(placed, axes, name="gather_start")

    def fetch(stage, after):
        if stage not in stages:
            return {}, {}
        idxs = [order.index(key) for key in stages[stage]]
        landed = _gather_wait(send_sems, recv_sems, [flying[i] for i in idxs], [axes[i] for i in idxs], idxs, after,
                              name=f"gather_wait_{stage}")
        W_new, P_new = {}, {}
        for key, arr in zip(stages[stage], landed):
            if key[0] == "small":
                small_all = arr.reshape((N_CHIPS,) + small.shape)
                per_chip = [_unpack(small_all[j], small_shapes) for j in range(N_CHIPS)]
                for i, n in enumerate(SMALL_SHARDED):
                    P_new[n] = jnp.concatenate([per_chip[j][i] for j in range(N_CHIPS)], axis=-1)
            else:
                W_new[key] = arr.reshape(arr.shape[1:])
        return W_new, P_new

    loss_row, grad_x, gw, gsmall = _local_step(x[0], loss_target[0], {n: w[n] for n in SMALL_REPLICATED}, fetch,
                                               n_layers)
    loss = lax.psum(loss_row[0, 0], ("x", "y", "c"))

    small_names = list(SMALL_SHARDED) + list(SMALL_REPLICATED)
    small_pack = _pack([gsmall[n] for n in small_names])
    slots = _scatter_grads([gw[n] for n in big_names], big_axes, small_pack, name="scatter_grads")
    sums = [_sum_parts(b, gw[n], ax, where, name=f"sum_{n}") for (n, ax), b in zip(BIG, slots[:-1])]
    small_sum = _sum_small(slots[-1], small_pack, where, name="sum_small")
    pairs = _pair_gather(sums, name="pair_gather")
    grad = {n: p.reshape(w[n].shape) for n, p in zip(big_names, pairs)}
    full_small = _unpack(small_sum, [gsmall[n].shape for n in small_names])
    for n, gfull in zip(small_names, full_small):
        if n in SMALL_SHARDED:
            width = w[n].shape[-1]
            grad[n] = lax.dynamic_slice_in_dim(gfull, chip * width, width, axis=-1)
        else:
            grad[n] = gfull

    delta, new_m, new_v = {}, {}, {}
    for n, _ in BIG:
        delta[n], new_m[n], new_v[n] = _adamw(w[n], grad[n], m[n], v[n], name=f"adamw_{n}")
    shapes = [w[n].shape for n in small_names]
    packed = _adamw(_pack([w[n] for n in small_names]), _pack([grad[n] for n in small_names]),
                    _pack([m[n] for n in small_names]), _pack([v[n] for n in small_names]), name="adamw_small")
    for out, res in zip((delta, new_m, new_v), packed):
        for n, a in zip(small_names, _unpack(res, shapes)):
            out[n] = a

    return (loss, grad_x[None], *[grad[n] for n in WEIGHTS], *[delta[n] for n in WEIGHTS],
            *[new_m[n] for n in WEIGHTS], *[new_v[n] for n in WEIGHTS])
```

```python
import functools
import math

import numpy as np
import jax
import jax.numpy as jnp
from jax import lax
from jax.experimental import pallas as pl
from jax.experimental.pallas import tpu as pltpu

F32 = jnp.float32
BF16 = jnp.bfloat16
MESH = pl.DeviceIdType.MESH

CHUNK = 128
HEAD_DIM = 64
DILATED_GROUPS = ((128, 1), (512, 4), (2048, 16))
N_GROUPS = len(DILATED_GROUPS)
BLK = 128
REL_BUCKETS = 32
REL_MAX_DIST = 2048
LN_EPS = 1e-5
NEG = -1e30
ADAM_LR = 0.001
ADAM_B1 = 0.9
ADAM_B2 = 0.999
ADAM_EPS = 1e-08
ADAM_WD = 0.01
ADAM_STEP = 10

N_CHIPS = 4
N_DEV = 8
LANES = 128
SUBLANES = 8
VMEM_LIMIT = 48 * 1024 * 1024

_SQRT_HALF = 0.7071067811865476
_INV_SQRT_2PI = 0.3989422804014327


def _cp(sem=None, vmem=VMEM_LIMIT):
    return pltpu.CompilerParams(dimension_semantics=sem, vmem_limit_bytes=vmem)


def _tile(dim, target, align=LANES):
    if dim <= target:
        return dim
    t = (target // align) * align
    while t >= align:
        if dim % t == 0:
            return t
        t -= align
    return dim


def _gelu(x):
    return 0.5 * x * (1.0 + lax.erf(x * _SQRT_HALF))


def _gelu_grad(x):
    return 0.5 * (1.0 + lax.erf(x * _SQRT_HALF)) + x * (_INV_SQRT_2PI * jnp.exp(-0.5 * x * x))


def _dot(a, b, ca, cb):
    return lax.dot_general(a, b, (((ca,), (cb,)), ((), ())), preferred_element_type=F32)


def _mn_tile(dim):
    return max(_tile(dim, 1024), _tile(dim, 1408))


def _mm(a, b, *, name, ta=False, tb=False, out_dtype=F32, tiles=None,
        a_lead=(), b_lead=(), a_halves=False, b_halves=False, add=None, add_scale=1.0, out_into=None, after=None):
    a2, b2 = a.shape[-2:], b.shape[-2:]
    M, K = (a2[1], a2[0]) if ta else a2
    Kb, N = (b2[1], b2[0]) if tb else b2
    if a_halves:
        K = 2 * K
    if b_halves:
        N = 2 * N
    assert K == Kb, (name, a.shape, b.shape)
    if tiles is None:
        k_target = 1024 if ta else (1408 if a_halves else (K if K <= 2816 else 2048))
        tm, tn = _mn_tile(M), _mn_tile(N // 2 if b_halves else N)
        tk = _tile(K // 2 if a_halves else K, k_target)
    else:
        tm = _tile(M, tiles[0])
        tn = _tile(N // 2 if b_halves else N, tiles[1])
        tk = _tile(K // 2 if a_halves else K, tiles[2])
    nm, nn, nk = M // tm, N // tn, K // tk
    nkh, nnh = nk // 2, nn // 2

    def lead(idx, rest):
        return tuple(idx) + tuple(rest)

    sq_a = (None,) * (a.ndim - 2)
    sq_b = (None,) * (b.ndim - 2)
    if a_halves:
        assert not ta and a.ndim == 3
        a_spec = pl.BlockSpec((None, tm, tk), lambda i, j, k: (k // nkh, i, k % nkh))
    elif ta:
        a_spec = pl.BlockSpec(sq_a + (tk, tm), lambda i, j, k: lead(a_lead, (k, i)))
    else:
        a_spec = pl.BlockSpec(sq_a + (tm, tk), lambda i, j, k: lead(a_lead, (i, k)))
    if b_halves:
        assert not tb and b.ndim == 3
        b_spec = pl.BlockSpec((None, tk, tn), lambda i, j, k: (j // nnh, k, j % nnh))
    elif tb:
        b_spec = pl.BlockSpec(sq_b + (tn, tk), lambda i, j, k: lead(b_lead, (j, k)))
    else:
        b_spec = pl.BlockSpec(sq_b + (tk, tn), lambda i, j, k: lead(b_lead, (k, j)))
    mn_spec = pl.BlockSpec((tm, tn), lambda i, j, k: (i, j))
    in_specs = [a_spec, b_spec]
    args = [a, b]
    if add is not None:
        in_specs.append(mn_spec)
        args.append(add)
    aliases = {}
    if out_into is None:
        o_spec, out_shape = mn_spec, jax.ShapeDtypeStruct((M, N), out_dtype)
    else:
        buf, n_layers, layer = out_into
        o_spec = pl.BlockSpec((None, tm, tn), lambda i, j, k: (layer, i, j))
        out_shape = jax.ShapeDtypeStruct((n_layers, M, N), out_dtype)
        if buf is not None:
            aliases = {len(args): 0}
            in_specs.append(pl.BlockSpec(memory_space=pl.ANY))
            args.append(buf)
    if after is not None:
        in_specs.append(pl.BlockSpec(memory_space=pl.ANY))
        args.append(after)
    n_in = len(args)
    ca, cb = (0 if ta else 1), (1 if tb else 0)

    def body(*refs):
        a_ref, b_ref = refs[:2]
        add_ref = refs[2] if add is not None else None
        o_ref = refs[n_in]

        def finish(acc):
            if add_ref is not None:
                acc = acc + add_scale * add_ref[...]
            o_ref[...] = acc.astype(o_ref.dtype)

        if nk == 1:
            finish(_dot(a_ref[...], b_ref[...], ca, cb))
        else:
            acc_ref = refs[-1]
            k = pl.program_id(2)

            @pl.when(k == 0)
            def _():
                acc_ref[...] = jnp.zeros_like(acc_ref)

            acc_ref[...] += _dot(a_ref[...], b_ref[...], ca, cb)

            @pl.when(k == nk - 1)
            def _():
                finish(acc_ref[...])

    return pl.pallas_call(
        body, name=name, grid=(nm, nn, nk),
        in_specs=in_specs, out_specs=o_spec, out_shape=out_shape,
        input_output_aliases=aliases,
        scratch_shapes=[] if nk == 1 else [pltpu.VMEM((tm, tn), F32)],
        compiler_params=_cp(("parallel", "parallel", "arbitrary")),
    )(*args)


def _ln_stats(r):
    mu = jnp.mean(r, axis=-1, keepdims=True)
    xc = r - mu
    var = jnp.mean(xc * xc, axis=-1, keepdims=True)
    rstd = lax.rsqrt(var + LN_EPS)
    return xc * rstd, rstd


def _add_ln(x, h, g, b, alpha, *, name):
    T, D = x.shape
    tr = _tile(T, 512, SUBLANES)
    row = pl.BlockSpec((tr, D), lambda i: (i, 0))
    vec = pl.BlockSpec((1, D), lambda i: (0, 0))

    def body(x_ref, h_ref, g_ref, b_ref, o_ref, ob_ref):
        xhat, _ = _ln_stats(alpha * x_ref[...] + h_ref[...])
        y = xhat * g_ref[...] + b_ref[...]
        o_ref[...] = y
        ob_ref[...] = y.astype(BF16)

    return pl.pallas_call(
        body, name=name, grid=(T // tr,),
        in_specs=[row, row, vec, vec], out_specs=[row, row],
        out_shape=[jax.ShapeDtypeStruct((T, D), F32), jax.ShapeDtypeStruct((T, D), BF16)],
        compiler_params=_cp(("parallel",)),
    )(x, h, g.reshape(1, D), b.reshape(1, D))


def _ln_bwd(dy, x, h, g, alpha, *, name):
    T, D = x.shape
    tr = _tile(T, 512, SUBLANES)
    row = pl.BlockSpec((tr, D), lambda i: (i, 0))
    vec = pl.BlockSpec((1, D), lambda i: (0, 0))

    def body(dy_ref, x_ref, h_ref, g_ref, dr_ref, drb_ref, dg_ref, db_ref):
        @pl.when(pl.program_id(0) == 0)
        def _():
            dg_ref[...] = jnp.zeros_like(dg_ref)
            db_ref[...] = jnp.zeros_like(db_ref)

        xhat, rstd = _ln_stats(alpha * x_ref[...] + h_ref[...])
        dyv = dy_ref[...]
        dxhat = dyv * g_ref[...]
        c1 = jnp.mean(dxhat, axis=-1, keepdims=True)
        c2 = jnp.mean(dxhat * xhat, axis=-1, keepdims=True)
        dr = rstd * (dxhat - c1 - xhat * c2)
        dr_ref[...] = dr
        drb_ref[...] = dr.astype(BF16)
        dg_ref[...] += jnp.sum(dyv * xhat, axis=0, keepdims=True)
        db_ref[...] += jnp.sum(dyv, axis=0, keepdims=True)

    return pl.pallas_call(
        body, name=name, grid=(T // tr,),
        in_specs=[row, row, row, vec], out_specs=[row, row, vec, vec],
        out_shape=[jax.ShapeDtypeStruct((T, D), F32), jax.ShapeDtypeStruct((T, D), BF16),
                   jax.ShapeDtypeStruct((1, D), F32), jax.ShapeDtypeStruct((1, D), F32)],
        compiler_params=_cp(("arbitrary",)),
    )(dy, x, h, g.reshape(1, D))


def _sgu_pieces(zp, lg, lb, E):
    z = _gelu(zp)
    u, v = z[:, :E], z[:, E:]
    vhat, rstd = _ln_stats(v)
    vn = vhat * lg + lb
    return u, vhat, rstd, vn


def _tril_mask():
    t = lax.broadcasted_iota(jnp.int32, (CHUNK, CHUNK), 0)
    s = lax.broadcasted_iota(jnp.int32, (CHUNK, CHUNK), 1)
    return s <= t


def _sgu_fwd(zp, ws, bs, lg, lb, *, name):
    T, E2 = zp.shape
    E = E2 // 2
    G = ws.shape[0]
    cw = E // G

    def body(zp_ref, ws_ref, bs_ref, lg_ref, lb_ref, y_ref):
        u, _, _, vn = _sgu_pieces(zp_ref[...], lg_ref[...], lb_ref[...], E)
        vnb = vn.astype(BF16)
        tril = _tril_mask()
        for g in range(G):
            cols = slice(g * cw, (g + 1) * cw)
            w = jnp.where(tril, ws_ref[g], 0.0).astype(BF16)
            sv = _dot(w, vnb[:, cols], 1, 0) + bs_ref[g]
            y_ref[:, cols] = (u[:, cols] * sv).astype(BF16)

    return pl.pallas_call(
        body, name=name, grid=(T // CHUNK,),
        in_specs=[pl.BlockSpec((CHUNK, E2), lambda n: (n, 0)),
                  pl.BlockSpec((G, CHUNK, CHUNK), lambda n: (0, 0, 0)),
                  pl.BlockSpec((G, CHUNK, 1), lambda n: (0, 0, 0)),
                  pl.BlockSpec((1, E), lambda n: (0, 0)),
                  pl.BlockSpec((1, E), lambda n: (0, 0))],
        out_specs=pl.BlockSpec((CHUNK, E), lambda n: (n, 0)),
        out_shape=jax.ShapeDtypeStruct((T, E), BF16),
        compiler_params=_cp(("parallel",)),
    )(zp, ws, bs.reshape(G, CHUNK, 1), lg.reshape(1, E), lb.reshape(1, E))


def _sgu_bwd(zp, dy, ws, bs, lg, lb, *, name):
    T, E2 = zp.shape
    E = E2 // 2
    G = ws.shape[0]
    cw = E // G
    assert G <= LANES

    def body(zp_ref, dy_ref, ws_ref, bs_ref, lg_ref, lb_ref,
             dzp_ref, dws_ref, dbs_ref, dlg_ref, dlb_ref, dvn_ref):
        @pl.when(pl.program_id(0) == 0)
        def _():
            dws_ref[...] = jnp.zeros_like(dws_ref)
            dbs_ref[...] = jnp.zeros_like(dbs_ref)
            dlg_ref[...] = jnp.zeros_like(dlg_ref)
            dlb_ref[...] = jnp.zeros_like(dlb_ref)

        zpv = zp_ref[...]
        lgv = lg_ref[...]
        u, vhat, rstd, vn = _sgu_pieces(zpv, lgv, lb_ref[...], E)
        gp = _gelu_grad(zpv)
        vnb = vn.astype(BF16)
        tril = _tril_mask()
        lane = lax.broadcasted_iota(jnp.int32, (CHUNK, LANES), 1)
        dbs = dbs_ref[...]
        for g in range(G):
            cols = slice(g * cw, (g + 1) * cw)
            w = jnp.where(tril, ws_ref[g], 0.0).astype(BF16)
            sv = _dot(w, vnb[:, cols], 1, 0) + bs_ref[g]
            dyg = dy_ref[:, cols]
            dzp_ref[:, cols] = (dyg * sv * gp[:, cols]).astype(BF16)
            dsv = dyg * u[:, cols]
            dsvb = dsv.astype(BF16)
            dws_ref[g] += jnp.where(tril, _dot(dsvb, vnb[:, cols], 1, 1), 0.0)
            dvn_ref[:, cols] = _dot(w, dsvb, 0, 0)
            dbs = dbs + jnp.where(lane == g, jnp.sum(dsv, axis=1, keepdims=True), 0.0)
        dbs_ref[...] = dbs
        dvn = dvn_ref[...]
        dlg_ref[...] += jnp.sum(dvn * vhat, axis=0, keepdims=True)
        dlb_ref[...] += jnp.sum(dvn, axis=0, keepdims=True)
        dvhat = dvn * lgv
        c1 = jnp.mean(dvhat, axis=-1, keepdims=True)
        c2 = jnp.mean(dvhat * vhat, axis=-1, keepdims=True)
        dv = rstd * (dvhat - c1 - vhat * c2)
        dzp_ref[:, E:] = (dv * gp[:, E:]).astype(BF16)

    vecE = pl.BlockSpec((1, E), lambda n: (0, 0))
    return pl.pallas_call(
        body, name=name, grid=(T // CHUNK,),
        in_specs=[pl.BlockSpec((CHUNK, E2), lambda n: (n, 0)),
                  pl.BlockSpec((CHUNK, E), lambda n: (n, 0)),
                  pl.BlockSpec((G, CHUNK, CHUNK), lambda n: (0, 0, 0)),
                  pl.BlockSpec((G, CHUNK, 1), lambda n: (0, 0, 0)),
                  vecE, vecE],
        out_specs=[pl.BlockSpec((CHUNK, E2), lambda n: (n, 0)),
                   pl.BlockSpec((G, CHUNK, CHUNK), lambda n: (0, 0, 0)),
                   pl.BlockSpec((CHUNK, LANES), lambda n: (0, 0)),
                   vecE, vecE],
        out_shape=[jax.ShapeDtypeStruct((T, E2), BF16),
                   jax.ShapeDtypeStruct((G, CHUNK, CHUNK), F32),
                   jax.ShapeDtypeStruct((CHUNK, LANES), F32),
                   jax.ShapeDtypeStruct((1, E), F32), jax.ShapeDtypeStruct((1, E), F32)],
        scratch_shapes=[pltpu.VMEM((CHUNK, E), F32)],
        compiler_params=_cp(("arbitrary",)),
    )(zp, dy, ws, bs.reshape(G, CHUNK, 1), lg.reshape(1, E), lb.reshape(1, E))


def _delay(h, k):
    t = lax.broadcasted_iota(jnp.int32, h.shape, 0)
    return jnp.where(t >= k, pltpu.roll(h, k, 0), 0.0)


def _advance(d, k):
    T = d.shape[0]
    t = lax.broadcasted_iota(jnp.int32, d.shape, 0)
    return jnp.where(t < T - k, pltpu.roll(d, T - k, 0), 0.0)


def _conv3(h, w_ref, b_ref):
    return w_ref[2:3, :] * h + w_ref[1:2, :] * _delay(h, 1) + w_ref[0:1, :] * _delay(h, 2) + b_ref[...]


def _convglu_fwd(hup, cw, cb, *, name):
    T, F2 = hup.shape
    F = F2 // 2
    tc = LANES
    nt = F // tc

    def body(ha_ref, hg_ref, wa_ref, wg_ref, ba_ref, bg_ref, o_ref):
        a = _conv3(ha_ref[...], wa_ref, ba_ref)
        g = _conv3(hg_ref[...], wg_ref, bg_ref)
        o_ref[...] = (_gelu(a) * g).astype(BF16)

    col = lambda off: pl.BlockSpec((T, tc), lambda j: (0, j + off))
    w3 = lambda off: pl.BlockSpec((3, tc), lambda j: (0, j + off))
    b1 = lambda off: pl.BlockSpec((1, tc), lambda j: (0, j + off))
    return pl.pallas_call(
        body, name=name, grid=(nt,),
        in_specs=[col(0), col(nt), w3(0), w3(nt), b1(0), b1(nt)],
        out_specs=pl.BlockSpec((T, tc), lambda j: (0, j)),
        out_shape=jax.ShapeDtypeStruct((T, F), BF16),
        compiler_params=_cp(("parallel",)),
    )(hup, hup, cw, cw, cb.reshape(1, F2), cb.reshape(1, F2))


def _convglu_bwd(hup, dact, cw, cb, *, name):
    T, F2 = hup.shape
    F = F2 // 2
    tc = LANES
    nt = F // tc

    def half(h, w_ref, d, dh_ref, dw_ref, db_ref, i):
        dh_ref[i] = (w_ref[2:3, :] * d + w_ref[1:2, :] * _advance(d, 1) + w_ref[0:1, :] * _advance(d, 2)).astype(BF16)
        dw_ref[i, 0:1, :] = jnp.sum(d * _delay(h, 2), axis=0, keepdims=True)
        dw_ref[i, 1:2, :] = jnp.sum(d * _delay(h, 1), axis=0, keepdims=True)
        dw_ref[i, 2:3, :] = jnp.sum(d * h, axis=0, keepdims=True)
        db_ref[i] = jnp.sum(d, axis=0, keepdims=True)

    def body(ha_ref, hg_ref, d_ref, wa_ref, wg_ref, ba_ref, bg_ref, dh_ref, dw_ref, db_ref):
        ha, hg = ha_ref[...], hg_ref[...]
        a = _conv3(ha, wa_ref, ba_ref)
        g = _conv3(hg, wg_ref, bg_ref)
        d = d_ref[...]
        half(ha, wa_ref, d * g * _gelu_grad(a), dh_ref, dw_ref, db_ref, 0)
        half(hg, wg_ref, d * _gelu(a), dh_ref, dw_ref, db_ref, 1)

    col = lambda off: pl.BlockSpec((T, tc), lambda j: (0, j + off))
    w3 = lambda off: pl.BlockSpec((3, tc), lambda j: (0, j + off))
    b1 = lambda off: pl.BlockSpec((1, tc), lambda j: (0, j + off))
    return pl.pallas_call(
        body, name=name, grid=(nt,),
        in_specs=[col(0), col(nt), col(0), w3(0), w3(nt), b1(0), b1(nt)],
        out_specs=[pl.BlockSpec((2, T, tc), lambda j: (0, 0, j)),
                   pl.BlockSpec((2, 3, tc), lambda j: (0, 0, j)),
                   pl.BlockSpec((2, 1, tc), lambda j: (0, 0, j))],
        out_shape=[jax.ShapeDtypeStruct((2, T, F), BF16),
                   jax.ShapeDtypeStruct((2, 3, F), F32),
                   jax.ShapeDtypeStruct((2, 1, F), F32)],
        compiler_params=_cp(("parallel",)),
    )(hup, hup, dact, cw, cw, cb.reshape(1, F2), cb.reshape(1, F2))


def _bucket_table():
    iq = np.arange(BLK)[:, None]
    ik = np.arange(2 * BLK)[None, :]
    delta = iq + BLK - ik
    out = []
    for win, dil in DILATED_GROUPS:
        valid = (delta >= 0) & (delta <= win // dil)
        n = (np.clip(delta, 0, None) * dil).astype(np.int32)
        max_exact = REL_BUCKETS // 2
        nf = np.maximum(n, 1).astype(np.float32)
        large = max_exact + (np.log(nf / np.float32(max_exact)) / np.float32(math.log(REL_MAX_DIST / max_exact))
                             * np.float32(REL_BUCKETS - max_exact)).astype(np.int32)
        large = np.minimum(large, REL_BUCKETS - 1)
        out.append(np.where(valid, np.where(n < max_exact, n, large), -1))
    return np.stack(out).astype(np.int32)


def _band_bias(rel_table, H, *, name):
    bkt = jnp.asarray(_bucket_table())

    def body(tab_ref, bk_ref, o_ref):
        g = pl.program_id(0)
        bk = bk_ref[...]
        prev = lax.broadcasted_iota(jnp.int32, bk.shape, 1) < BLK
        for h in range(H):
            def bucket(c, acc):
                return jnp.where(bk == c, tab_ref[c, g * H + h], acc)

            b = lax.fori_loop(0, REL_BUCKETS, bucket, jnp.zeros(bk.shape, F32))
            b = jnp.where(bk >= 0, b, NEG)
            o_ref[h, 0] = b
            o_ref[h, 1] = jnp.where(prev, NEG, b)

    return pl.pallas_call(
        body, name=name, grid=(N_GROUPS,),
        in_specs=[pl.BlockSpec(memory_space=pltpu.SMEM),
                  pl.BlockSpec((None, BLK, 2 * BLK), lambda g: (g, 0, 0))],
        out_specs=pl.BlockSpec((None, H, 2, BLK, 2 * BLK), lambda g: (g, 0, 0, 0, 0)),
        out_shape=jax.ShapeDtypeStruct((N_GROUPS, H, 2, BLK, 2 * BLK), F32),
        compiler_params=_cp(("parallel",)),
    )(rel_table, bkt)


PAIR = 2 * HEAD_DIM


def _head_masks():
    lane = lax.broadcasted_iota(jnp.int32, (BLK, PAIR), 1)
    return lane < HEAD_DIM, lane >= HEAD_DIM


def _only(mask, a):
    return jnp.where(mask, a, jnp.zeros_like(a))


def _scores(qm, kp, kc, bias):
    scale = HEAD_DIM ** -0.5
    sp = _dot(qm, kp, 1, 1) * scale + bias[:, :BLK]
    sc = _dot(qm, kc, 1, 1) * scale + bias[:, BLK:]
    return sp, sc


def _attn_specs(nb, dil, g, H, HD, qb):
    nbg = nb // dil
    cur = lambda col: pl.BlockSpec((BLK, HD), lambda b: (qb(b), col))
    prev = lambda col: pl.BlockSpec((BLK, HD), lambda b: (jnp.maximum(qb(b) - 1, 0), col))
    stat = pl.BlockSpec((BLK, LANES), lambda b: (qb(b), 0))
    bias = pl.BlockSpec((None, H, None, BLK, 2 * BLK),
                        lambda b: (g, 0, jnp.where(qb(b) % nbg == 0, 1, 0), 0, 0))
    return cur, prev, stat, bias


def _attn_fwd(q, k, v, bias, g, dil, *, name):
    (qa, qc), (ka, kc_), (va, vc_) = q, k, v
    T = qa.shape[0]
    H = bias.shape[1]
    HD = H * HEAD_DIM
    assert H % 2 == 0 and H <= LANES and T % (dil * BLK) == 0
    nb = T // BLK
    cur, prev, stat, bspec = _attn_specs(nb, dil, g, H, HD, lambda b: b)

    def body(q_ref, kp_ref, kc_ref, vp_ref, vc_ref, b_ref, o_ref, l_ref):
        masks = _head_masks()
        lane = lax.broadcasted_iota(jnp.int32, (BLK, LANES), 1)
        lse = jnp.zeros((BLK, LANES), F32)
        for p in range(H // 2):
            cols = slice(p * PAIR, (p + 1) * PAIR)
            qv, kp, kc, vp, vc = q_ref[:, cols], kp_ref[:, cols], kc_ref[:, cols], vp_ref[:, cols], vc_ref[:, cols]
            o = jnp.zeros((BLK, PAIR), F32)
            for e in range(2):
                h = 2 * p + e
                sp, sc = _scores(_only(masks[e], qv), kp, kc, b_ref[h])
                m = jnp.maximum(jnp.max(sp, axis=-1, keepdims=True), jnp.max(sc, axis=-1, keepdims=True))
                pp, pc = jnp.exp(sp - m), jnp.exp(sc - m)
                den = jnp.sum(pp, axis=-1, keepdims=True) + jnp.sum(pc, axis=-1, keepdims=True)
                o = o + _dot((pp / den).astype(BF16), _only(masks[e], vp), 1, 0)
                o = o + _dot((pc / den).astype(BF16), _only(masks[e], vc), 1, 0)
                lse = jnp.where(lane == h, m + jnp.log(den), lse)
            o_ref[:, cols] = o
        l_ref[...] = lse

    return pl.pallas_call(
        body, name=name, grid=(nb,),
        in_specs=[cur(qc), prev(kc_), cur(kc_), prev(vc_), cur(vc_), bspec],
        out_specs=[cur(0), stat],
        out_shape=[jax.ShapeDtypeStruct((T, HD), F32), jax.ShapeDtypeStruct((T, LANES), F32)],
        compiler_params=_cp(("parallel",)),
    )(qa, ka, ka, va, va, bias)


def _attn_combine(os, ls, *, name):
    n = len(os)
    T, HD = os[0].shape
    H = HD // HEAD_DIM
    tr = _tile(T, 512, SUBLANES)
    wide = pl.BlockSpec((tr, HD), lambda i: (i, 0))
    stat = pl.BlockSpec((tr, LANES), lambda i: (i, 0))

    def body(*refs):
        o_refs, l_refs = refs[:n], refs[n:2 * n]
        oc_ref, ocb_ref, lt_ref = refs[2 * n:]
        ls_ = [r[...] for r in l_refs]
        m = ls_[0]
        for l in ls_[1:]:
            m = jnp.maximum(m, l)
        ws = [jnp.exp(l - m) for l in ls_]
        den = ws[0]
        for w in ws[1:]:
            den = den + w
        lt_ref[...] = m + jnp.log(den)
        ws = [w / den for w in ws]
        low = lax.broadcasted_iota(jnp.int32, (tr, PAIR), 1) < HEAD_DIM
        for p in range(H // 2):
            cols = slice(p * PAIR, (p + 1) * PAIR)
            acc = jnp.zeros((tr, PAIR), F32)
            for g in range(n):
                w = jnp.where(low, ws[g][:, 2 * p:2 * p + 1], ws[g][:, 2 * p + 1:2 * p + 2])
                acc = acc + w * o_refs[g][:, cols]
            oc_ref[:, cols] = acc
            ocb_ref[:, cols] = acc.astype(BF16)

    return pl.pallas_call(
        body, name=name, grid=(T // tr,),
        in_specs=[wide] * n + [stat] * n, out_specs=[wide, wide, stat],
        out_shape=[jax.ShapeDtypeStruct((T, HD), F32), jax.ShapeDtypeStruct((T, HD), BF16),
                   jax.ShapeDtypeStruct((T, LANES), F32)],
        compiler_params=_cp(("parallel",)),
    )(*os, *ls)


def _attn_bwd_prep(do, oc, *, name):
    T, HD = do.shape
    H = HD // HEAD_DIM
    tr = _tile(T, 512, SUBLANES)
    wide = pl.BlockSpec((tr, HD), lambda i: (i, 0))
    stat = pl.BlockSpec((tr, LANES), lambda i: (i, 0))

    def body(do_ref, oc_ref, dl_ref, dob_ref):
        low = lax.broadcasted_iota(jnp.int32, (tr, PAIR), 1) < HEAD_DIM
        lane = lax.broadcasted_iota(jnp.int32, (tr, LANES), 1)
        dl = jnp.zeros((tr, LANES), F32)
        for p in range(H // 2):
            cols = slice(p * PAIR, (p + 1) * PAIR)
            d = do_ref[:, cols]
            prod = d * oc_ref[:, cols]
            dl = jnp.where(lane == 2 * p, jnp.sum(jnp.where(low, prod, 0.0), axis=-1, keepdims=True), dl)
            dl = jnp.where(lane == 2 * p + 1, jnp.sum(jnp.where(low, 0.0, prod), axis=-1, keepdims=True), dl)
            dob_ref[:, cols] = d.astype(BF16)
        dl_ref[...] = dl

    return pl.pallas_call(
        body, name=name, grid=(T // tr,),
        in_specs=[wide, wide], out_specs=[stat, wide],
        out_shape=[jax.ShapeDtypeStruct((T, LANES), F32), jax.ShapeDtypeStruct((T, HD), BF16)],
        compiler_params=_cp(("parallel",)),
    )(do, oc)


def _attn_bwd(q, k, v, bias, g, dil, do, lse, delta, *, name):
    (qa, qc), (ka, kc_), (va, vc_) = q, k, v
    T, HD = do.shape
    H = HD // HEAD_DIM
    nb = T // BLK
    scale = HEAD_DIM ** -0.5
    cur, prev, stat, bspec = _attn_specs(nb, dil, g, H, HD, lambda b: jnp.minimum(b, nb - 1))
    late = pl.BlockSpec((BLK, HD), lambda b: (jnp.maximum(b - 1, 0), 0))

    def body(q_ref, kp_ref, kc_ref, vp_ref, vc_ref, b_ref, do_ref, l_ref, dl_ref,
             dq_ref, dk_ref, dv_ref, db_ref, ck_ref, cv_ref):
        b = pl.program_id(0)

        @pl.when(b == 0)
        def _():
            db_ref[...] = jnp.zeros_like(db_ref)
            ck_ref[...] = jnp.zeros_like(ck_ref)
            cv_ref[...] = jnp.zeros_like(cv_ref)

        @pl.when(b < nb)
        def _():
            masks = _head_masks()
            for p in range(H // 2):
                cols = slice(p * PAIR, (p + 1) * PAIR)
                qv, kp, kc, vp, vc, dov = (q_ref[:, cols], kp_ref[:, cols], kc_ref[:, cols], vp_ref[:, cols],
                                           vc_ref[:, cols], do_ref[:, cols])
                zero = jnp.zeros((BLK, PAIR), F32)
                dq, dkp, dkc, dvp, dvc = zero, zero, zero, zero, zero
                for e in range(2):
                    h = 2 * p + e
                    qm, dom = _only(masks[e], qv), _only(masks[e], dov)
                    sp, sc = _scores(qm, kp, kc, b_ref[h])
                    l, dl = l_ref[:, h:h + 1], dl_ref[:, h:h + 1]
                    pp, pc = jnp.exp(sp - l), jnp.exp(sc - l)
                    dsp = pp * (_dot(dom, vp, 1, 1) - dl)
                    dsc = pc * (_dot(dom, vc, 1, 1) - dl)
                    db_ref[h, :, :BLK] += dsp
                    db_ref[h, :, BLK:] += dsc
                    dspb, dscb = dsp.astype(BF16), dsc.astype(BF16)
                    dq = dq + _dot(dspb, _only(masks[e], kp), 1, 0) + _dot(dscb, _only(masks[e], kc), 1, 0)
                    dkp = dkp + _dot(dspb, qm, 0, 0)
                    dkc = dkc + _dot(dscb, qm, 0, 0)
                    dvp = dvp + _dot(pp.astype(BF16), dom, 0, 0)
                    dvc = dvc + _dot(pc.astype(BF16), dom, 0, 0)
                dq_ref[:, cols] = (scale * dq).astype(BF16)
                dk_ref[:, cols] = ck_ref[:, cols] + scale * dkp
                dv_ref[:, cols] = cv_ref[:, cols] + dvp
                ck_ref[:, cols] = scale * dkc
                cv_ref[:, cols] = dvc

        @pl.when(b == nb)
        def _():
            dk_ref[...] = ck_ref[...]
            dv_ref[...] = cv_ref[...]

    f32 = jax.ShapeDtypeStruct((T, HD), F32)
    return pl.pallas_call(
        body, name=name, grid=(nb + 1,),
        in_specs=[cur(qc), prev(kc_), cur(kc_), prev(vc_), cur(vc_), bspec, cur(0), stat, stat],
        out_specs=[cur(0), late, late, pl.BlockSpec((H, BLK, 2 * BLK), lambda b: (0, 0, 0))],
        out_shape=[jax.ShapeDtypeStruct((T, HD), BF16), f32, f32,
                   jax.ShapeDtypeStruct((H, BLK, 2 * BLK), F32)],
        scratch_shapes=[pltpu.VMEM((BLK, HD), F32), pltpu.VMEM((BLK, HD), F32)],
        compiler_params=_cp(("arbitrary",)),
    )(qa, ka, ka, va, va, bias, do, lse, delta)


def _rel_grad(dbs, *, name):
    H = dbs[0].shape[1]
    n = len(dbs)
    assert H <= LANES
    bkt = jnp.asarray(_bucket_table())

    def body(*refs):
        bk = refs[n][...]
        o_ref = refs[n + 1]
        db = refs[0][...]
        for r in refs[1:n]:
            db = db + r[...]
        row = lax.broadcasted_iota(jnp.int32, (REL_BUCKETS, LANES), 0)
        col = lax.broadcasted_iota(jnp.int32, (REL_BUCKETS, LANES), 1)

        def bucket(c, acc):
            sel = bk == c
            for h in range(H):
                val = jnp.sum(jnp.where(sel, db[h], 0.0))
                acc = jnp.where((row == c) & (col == h), val, acc)
            return acc

        o_ref[...] = lax.fori_loop(0, REL_BUCKETS, bucket, jnp.zeros((REL_BUCKETS, LANES), F32))

    dspec = pl.BlockSpec((None, H, BLK, 2 * BLK), lambda g: (g, 0, 0, 0))
    return pl.pallas_call(
        body, name=name, grid=(N_GROUPS,),
        in_specs=[dspec] * n + [pl.BlockSpec((None, BLK, 2 * BLK), lambda g: (g, 0, 0))],
        out_specs=pl.BlockSpec((None, REL_BUCKETS, LANES), lambda g: (g, 0, 0)),
        out_shape=jax.ShapeDtypeStruct((N_GROUPS, REL_BUCKETS, LANES), F32),
        compiler_params=_cp(("parallel",)),
    )(*dbs, bkt)


def _to_group(a, d):
    T, C = a.shape
    return a if d == 1 else a.reshape(T // d, d, C).transpose(1, 0, 2).reshape(T, C)


def _from_group(a, d):
    T, C = a.shape
    return a if d == 1 else a.reshape(d, T // d, C).transpose(1, 0, 2).reshape(T, C)


def _rows_view(a):
    a2 = a.reshape(-1, a.shape[-1])
    R, C = a2.shape
    tr = _tile(R, max(SUBLANES, (1 << 18) // C), SUBLANES)
    return a2, R, C, tr


def _addn(xs, out_dtype, *, name):
    shape = xs[0].shape
    x2s = [_rows_view(x)[0] for x in xs]
    _, R, C, tr = _rows_view(xs[0])
    spec = pl.BlockSpec((tr, C), lambda i: (i, 0))

    def body(*refs):
        acc = refs[0][...].astype(F32)
        for r in refs[1:-1]:
            acc = acc + r[...].astype(F32)
        refs[-1][...] = acc.astype(out_dtype)

    out = pl.pallas_call(
        body, name=name, grid=(R // tr,),
        in_specs=[spec] * len(xs), out_specs=spec,
        out_shape=jax.ShapeDtypeStruct((R, C), out_dtype),
        compiler_params=_cp(("parallel",)),
    )(*x2s)
    return out.reshape(shape)


def _slot_sum(b_ref, own, my_id):
    acc = None
    for s in range(b_ref.shape[0]):
        term = jnp.where(my_id == s, own, b_ref[s].astype(F32))
        acc = term if acc is None else acc + term
    return acc


def _part_geometry(shape, axis):
    _, R, C = shape
    if axis == COL:
        return R // 2, C // N_CHIPS
    return R // (2 * N_CHIPS), C


def _sum_parts(buf, grad, axis, where, *, name):
    n, L, Rp, Cp = buf.shape
    assert (Rp, Cp) == _part_geometry(grad.shape, axis)
    tr = _tile(Rp, max(2 * SUBLANES, (1 << 17) // Cp), 2 * SUBLANES)
    nr = Rp // tr

    def own_map(l, i, w):
        if axis == COL:
            return (l, w[1] * nr + i, w[0])
        return (l, (2 * w[0] + w[1]) * nr + i, 0)

    def body(w_ref, b_ref, g_ref, o_ref):
        o_ref[...] = _slot_sum(b_ref, g_ref[...].astype(F32), w_ref[2])

    return pl.pallas_call(
        body, name=name,
        grid_spec=pltpu.PrefetchScalarGridSpec(
            num_scalar_prefetch=1, grid=(L, nr),
            in_specs=[pl.BlockSpec((n, None, tr, Cp), lambda l, i, w: (0, l, i, 0)),
                      pl.BlockSpec((None, tr, Cp), own_map)],
            out_specs=pl.BlockSpec((None, None, tr, Cp), lambda l, i, w: (l, w[1], i, 0))),
        out_shape=jax.ShapeDtypeStruct((L, 2, Rp, Cp), F32),
        compiler_params=_cp(("parallel", "parallel")),
    )(where, buf, grad)


def _sum_small(buf, own, where, *, name):
    n, R, C = buf.shape
    tr = _tile(R, 512, SUBLANES)

    def body(w_ref, b_ref, g_ref, o_ref):
        o_ref[...] = _slot_sum(b_ref, g_ref[...], w_ref[2])

    return pl.pallas_call(
        body, name=name,
        grid_spec=pltpu.PrefetchScalarGridSpec(
            num_scalar_prefetch=1, grid=(R // tr,),
            in_specs=[pl.BlockSpec((n, tr, C), lambda i, w: (0, i, 0)),
                      pl.BlockSpec((tr, C), lambda i, w: (i, 0))],
            out_specs=pl.BlockSpec((tr, C), lambda i, w: (i, 0))),
        out_shape=jax.ShapeDtypeStruct((R, C), F32),
        compiler_params=_cp(("parallel",)),
    )(where, buf, own)


def _place_shard(shard, layer, axis, where, out_dtype, *, name):
    _, R, C = shard.shape
    tr = _tile(R, max(2 * SUBLANES, (1 << 18) // C), 2 * SUBLANES)
    nr = R // tr
    if axis == COL:
        full, out_map = (1, R, C * N_CHIPS), (lambda i, w: (0, i, w[0]))
    else:
        full, out_map = (1, R * N_CHIPS, C), (lambda i, w: (0, w[0] * nr + i, 0))

    def body(w_ref, x_ref, o_ref):
        o_ref[...] = x_ref[...].astype(out_dtype)

    return pl.pallas_call(
        body, name=name,
        grid_spec=pltpu.PrefetchScalarGridSpec(
            num_scalar_prefetch=1, grid=(nr,),
            in_specs=[pl.BlockSpec((None, tr, C), lambda i, w: (layer, i, 0))],
            out_specs=pl.BlockSpec((None, tr, C), out_map)),
        out_shape=jax.ShapeDtypeStruct(full, out_dtype),
        compiler_params=_cp(("parallel",)),
    )(where, shard)


def _loss_grad(y, tgt, *, name):
    T, D = y.shape
    tr = _tile(T, 512, SUBLANES)
    row = pl.BlockSpec((tr, D), lambda i: (i, 0))

    def body(y_ref, t_ref, dy_ref, l_ref, acc_ref):
        i = pl.program_id(0)

        @pl.when(i == 0)
        def _():
            acc_ref[...] = jnp.zeros_like(acc_ref)

        err = y_ref[...] - t_ref[...]
        dy_ref[...] = err * (1.0 / D)
        acc_ref[...] += jnp.sum(err * err, axis=0, keepdims=True)

        @pl.when(i == T // tr - 1)
        def _():
            tot = jnp.sum(acc_ref[...], axis=1, keepdims=True) * (0.5 / D)
            l_ref[...] = jnp.broadcast_to(tot, (1, LANES))

    return pl.pallas_call(
        body, name=name, grid=(T // tr,),
        in_specs=[row, row],
        out_specs=[row, pl.BlockSpec((1, LANES), lambda i: (0, 0))],
        out_shape=[jax.ShapeDtypeStruct((T, D), F32), jax.ShapeDtypeStruct((1, LANES), F32)],
        scratch_shapes=[pltpu.VMEM((1, D), F32)],
        compiler_params=_cp(("arbitrary",)),
    )(y, tgt)


def _adamw(w, g, m, v, *, name):
    shape = w.shape
    w2, R, C, tr = _rows_view(w)
    spec = pl.BlockSpec((tr, C), lambda i: (i, 0))

    def body(w_ref, g_ref, m_ref, v_ref, d_ref, nm_ref, nv_ref):
        gv = g_ref[...]
        nm = ADAM_B1 * m_ref[...] + (1.0 - ADAM_B1) * gv
        nv = ADAM_B2 * v_ref[...] + (1.0 - ADAM_B2) * (gv * gv)
        m_hat = nm / (1.0 - ADAM_B1 ** ADAM_STEP)
        v_hat = nv / (1.0 - ADAM_B2 ** ADAM_STEP)
        d_ref[...] = -ADAM_LR * (m_hat / (jnp.sqrt(v_hat) + ADAM_EPS) + ADAM_WD * w_ref[...])
        nm_ref[...] = nm
        nv_ref[...] = nv

    shp = jax.ShapeDtypeStruct((R, C), F32)
    outs = pl.pallas_call(
        body, name=name, grid=(R // tr,),
        in_specs=[spec] * 4, out_specs=[spec] * 3, out_shape=[shp] * 3,
        compiler_params=_cp(("parallel",)),
    )(w2, g.reshape(R, C), m.reshape(R, C), v.reshape(R, C))
    return tuple(o.reshape(shape) for o in outs)


def _pack(arrs):
    flat = jnp.concatenate([a.reshape(-1).astype(F32) for a in arrs])
    unit = SUBLANES * LANES
    pad = (-flat.shape[0]) % unit
    return jnp.pad(flat, (0, pad)).reshape(-1, LANES)


def _unpack(buf, shapes):
    flat = buf.reshape(-1)
    out, off = [], 0
    for s in shapes:
        n = int(np.prod(s))
        out.append(flat[off:off + n].reshape(s))
        off += n
    return out


def _me():
    return lax.axis_index("x"), lax.axis_index("y"), lax.axis_index("c")


def _flip(pos, k):
    x, y, c = pos
    return (1 - x if k & 4 else x, 1 - y if k & 2 else y, 1 - c if k & 1 else c)


HBM = pl.BlockSpec(memory_space=pltpu.HBM)

COL, ROW = -1, -2


def _shard_window(ref, axis, j, n):
    start = pl.multiple_of(j * n, n)
    if axis == COL:
        return ref.at[:, :, pl.ds(start, n)]
    return ref.at[:, pl.ds(start, n), :]


GATHER_PEERS = (2, 4, 6)
SEM = pl.BlockSpec(memory_space=pltpu.SEMAPHORE)
EFFECT = pltpu.SideEffectType.DATAFLOW_SIDE_EFFECTING


def _gather_copy(buf, axis, i, pi, chip, me, send_sems, recv_sems):
    win = _shard_window(buf, axis, chip, buf.shape[axis] // N_CHIPS)
    k = len(GATHER_PEERS) * i + pi
    return pltpu.make_async_remote_copy(
        src_ref=win, dst_ref=win, send_sem=send_sems.at[k], recv_sem=recv_sems.at[k],
        device_id=_flip(me, GATHER_PEERS[pi]), device_id_type=MESH)


def _gather_start(fulls, axes, *, name):
    n = len(fulls)
    n_sem = len(GATHER_PEERS) * n

    def body(*refs):
        send_sems, recv_sems = refs[n], refs[n + 1]
        bufs = refs[n + 2:]
        me = _me()
        for i in range(n):
            for pi in range(len(GATHER_PEERS)):
                _gather_copy(bufs[i], axes[i], i, pi, 2 * me[0] + me[1], me, send_sems, recv_sems).start()

    outs = pl.pallas_call(
        body, name=name,
        in_specs=[HBM] * n, out_specs=[SEM, SEM] + [HBM] * n,
        out_shape=[pltpu.SemaphoreType.DMA((n_sem,)), pltpu.SemaphoreType.DMA((n_sem,))]
        + [pltpu.HBM(f.shape, f.dtype) for f in fulls],
        input_output_aliases={i: 2 + i for i in range(n)},
        compiler_params=pltpu.CompilerParams(has_side_effects=EFFECT),
    )(*[pltpu.with_memory_space_constraint(f, pltpu.HBM) for f in fulls])
    return outs[0], outs[1], list(outs[2:])


def _gather_wait(send_sems, recv_sems, bufs, axes, idxs, after, *, name):
    m = len(bufs)

    def body(*refs):
        ss, rs = refs[m], refs[m + 1]
        outs = refs[m + 3:]
        me = _me()
        for t, i in enumerate(idxs):
            for pi, k in enumerate(GATHER_PEERS):
                px, py, _ = _flip(me, k)
                _gather_copy(outs[t], axes[t], i, pi, 2 * me[0] + me[1], me, ss, rs).wait_send()
                _gather_copy(outs[t], axes[t], i, pi, 2 * px + py, me, ss, rs).wait_recv()

    return pl.pallas_call(
        body, name=name,
        in_specs=[HBM] * m + [SEM, SEM, pl.BlockSpec(memory_space=pl.ANY)], out_specs=[HBM] * m,
        out_shape=[pltpu.HBM(b.shape, b.dtype) for b in bufs],
        input_output_aliases={t: t for t in range(m)},
        compiler_params=pltpu.CompilerParams(has_side_effects=EFFECT),
    )(*bufs, send_sems, recv_sems, after)


def _grad_part(ref, layer, axis, pos):
    px, py, pc = pos
    Rp, Cp = _part_geometry(ref.shape, axis)
    chip = 2 * px + py
    if axis == COL:
        return ref.at[layer, pl.ds(pl.multiple_of(pc * Rp, Rp), Rp), pl.ds(pl.multiple_of(chip * Cp, Cp), Cp)]
    return ref.at[layer, pl.ds(pl.multiple_of((2 * chip + pc) * Rp, Rp), Rp), :]


def _scatter_copy(grad, slots, layer, axis, i, k, src_pos, dst_slot, me, send_sems, recv_sems):
    sem = (N_DEV - 1) * i + k - 1
    return pltpu.make_async_remote_copy(
        src_ref=_grad_part(grad, layer, axis, src_pos), dst_ref=slots.at[dst_slot, layer],
        send_sem=send_sems.at[sem], recv_sem=recv_sems.at[sem],
        device_id=_flip(me, k), device_id_type=MESH)


def _scatter_start(grads, slots, layers, axes, *, name):
    n = len(grads)
    n_sem = (N_DEV - 1) * n

    def body(*refs):
        send_sems, recv_sems = refs[2 * n], refs[2 * n + 1]
        g_refs, s_refs = refs[2 * n + 2:3 * n + 2], refs[3 * n + 2:4 * n + 2]
        token = refs[4 * n + 2]
        me = _me()
        my_id = 4 * me[0] + 2 * me[1] + me[2]
        for i in range(n):
            for k in range(1, N_DEV):
                _scatter_copy(g_refs[i], s_refs[i], layers[i], axes[i], i, k, _flip(me, k), my_id, me,
                              send_sems, recv_sems).start()
        token[...] = jnp.zeros_like(token)

    arrays = list(grads) + list(slots)
    outs = pl.pallas_call(
        body, name=name,
        in_specs=[HBM] * (2 * n),
        out_specs=[SEM, SEM] + [HBM] * (2 * n) + [pl.BlockSpec(memory_space=pltpu.VMEM)],
        out_shape=[pltpu.SemaphoreType.DMA((n_sem,)), pltpu.SemaphoreType.DMA((n_sem,))]
        + [pltpu.HBM(a.shape, a.dtype) for a in arrays] + [jax.ShapeDtypeStruct((SUBLANES, LANES), F32)],
        input_output_aliases={i: 2 + i for i in range(2 * n)},
        compiler_params=pltpu.CompilerParams(has_side_effects=EFFECT),
    )(*[pltpu.with_memory_space_constraint(a, pltpu.HBM) for a in arrays])
    return outs[0], outs[1], list(outs[2:2 + n]), list(outs[2 + n:2 + 2 * n]), outs[-1]


def _scatter_wait(sems, grads, slots, axes, plan, *, name):
    n = len(grads)
    flat_sems = [s for pair in sems for s in pair]

    def body(*refs):
        sem_refs = refs[2 * n:2 * n + len(flat_sems)]
        g_refs, s_refs = refs[2 * n + len(flat_sems):3 * n + len(flat_sems)], refs[3 * n + len(flat_sems):]
        me = _me()
        for j, items in enumerate(plan):
            ss, rs = sem_refs[2 * j], sem_refs[2 * j + 1]
            for i, (a, layer) in enumerate(items):
                for k in range(1, N_DEV):
                    px, py, pc = _flip(me, k)
                    cp = _scatter_copy(g_refs[a], s_refs[a], layer, axes[a], i, k, me, 4 * px + 2 * py + pc, me, ss, rs)
                    cp.wait_send()
                    cp.wait_recv()

    arrays = list(grads) + list(slots)
    outs = pl.pallas_call(
        body, name=name,
        in_specs=[HBM] * (2 * n) + [SEM] * len(flat_sems), out_specs=[HBM] * (2 * n),
        out_shape=[pltpu.HBM(a.shape, a.dtype) for a in arrays],
        input_output_aliases={i: i for i in range(2 * n)},
        compiler_params=pltpu.CompilerParams(has_side_effects=EFFECT),
    )(*arrays, *flat_sems)
    return list(outs[:n]), list(outs[n:])


def _gather_small(small, *, name):
    def body(src, dst, send_sems, recv_sems):
        me = _me()
        my_id = 4 * me[0] + 2 * me[1] + me[2]

        def copy(k, slot):
            return pltpu.make_async_remote_copy(
                src_ref=src, dst_ref=dst.at[slot], send_sem=send_sems.at[k - 1], recv_sem=recv_sems.at[k - 1],
                device_id=_flip(me, k), device_id_type=MESH)

        sends = [copy(k, my_id) for k in range(1, N_DEV)]
        for cp in sends:
            cp.start()
        for k in range(1, N_DEV):
            px, py, pc = _flip(me, k)
            copy(k, 4 * px + 2 * py + pc).wait_recv()
        for cp in sends:
            cp.wait_send()

    return pl.pallas_call(
        body, name=name, in_specs=[HBM], out_specs=HBM,
        out_shape=jax.ShapeDtypeStruct((N_DEV,) + small.shape, small.dtype),
        scratch_shapes=[pltpu.SemaphoreType.DMA((N_DEV - 1,)), pltpu.SemaphoreType.DMA((N_DEV - 1,))],
    )(small)


def _pair_gather(halves, *, name):
    n = len(halves)

    def body(*refs):
        bufs = refs[n:2 * n]
        send_sems, recv_sems = refs[2 * n:]
        me = _me()
        c = me[2]
        sib = _flip(me, 1)

        def copy(i, half):
            return pltpu.make_async_remote_copy(
                src_ref=bufs[i].at[:, half], dst_ref=bufs[i].at[:, half], send_sem=send_sems.at[i],
                recv_sem=recv_sems.at[i], device_id=sib, device_id_type=MESH)

        sends = [copy(i, c) for i in range(n)]
        for cp in sends:
            cp.start()
        for i in range(n):
            copy(i, 1 - c).wait_recv()
        for cp in sends:
            cp.wait_send()

    return pl.pallas_call(
        body, name=name,
        in_specs=[HBM] * n, out_specs=[HBM] * n,
        out_shape=[jax.ShapeDtypeStruct(h.shape, h.dtype) for h in halves],
        input_output_aliases={i: i for i in range(n)},
        scratch_shapes=[pltpu.SemaphoreType.DMA((n,)), pltpu.SemaphoreType.DMA((n,))],
    )(*halves)


def _local_step(x, tgt, P, fetch, ship, n_layers):
    T, D = x.shape
    P = dict(P)
    W = {}
    depth = P["ffn_conv_b"].shape[0]
    n_a = P["a_w_s"].shape[0]
    alpha = (2 * depth) ** 0.25
    H = P["rel_table"].shape[1] // N_GROUPS
    HD = H * HEAD_DIM
    dils = [d for _, d in DILATED_GROUPS]
    assert T % (dils[-1] * BLK) == 0

    def arrive(stage, after):
        w_new, p_new = fetch(stage, after)
        W.update(w_new)
        P.update(p_new)

    bias = _band_bias(P["rel_table"], H, name="band_bias")
    saved = []
    xf, xb = x, x.astype(BF16)
    kg = vg = None
    for i in range(depth):
        s = {"x": xf, "xb": xb}
        arrive(f"{i}a", xf)
        if i < n_a:
            s["zp"] = _mm(xb, W["a_w_in", i], name=f"a{i}_in")
            arrive(f"{i}b", s["zp"])
            s["y"] = _sgu_fwd(s["zp"], P["a_w_s"][i], P["a_b_s"][i], P["a_ln_g"][i], P["a_ln_b"][i], name=f"a{i}_sgu")
            s["h"] = _mm(s["y"], W["a_w_out", i], name=f"a{i}_out")
        else:
            j = i - n_a
            if j == 0:
                kv = _mm(xb, W["kv_w", 0], out_dtype=BF16, name="kv_proj")
                kg = [(kv, 0) if d == 1 else (_to_group(kv[:, :HD], d), 0) for d in dils]
                vg = [(kv, 1) if d == 1 else (_to_group(kv[:, HD:], d), 0) for d in dils]
            q = _mm(xb, W["b_w_q", j], out_dtype=BF16, name=f"b{j}_q")
            s["qg"] = [(q, g) if d == 1 else (_to_group(q[:, g * HD:(g + 1) * HD], d), 0) for g, d in enumerate(dils)]
            os, ls = [], []
            for g, d in enumerate(dils):
                o_g, l_g = _attn_fwd(s["qg"][g], kg[g], vg[g], bias, g, d, name=f"b{j}_attn{g}")
                os.append(_from_group(o_g, d))
                ls.append(_from_group(l_g, d))
            s["oc"], s["ob"], s["lse"] = _attn_combine(os, ls, name=f"b{j}_comb")
            s["h"] = _mm(s["ob"], W["b_w_o", j], name=f"b{j}_o")
        s["x1"], s["x1b"] = _add_ln(xf, s["h"], P["ln_g"][i, 0], P["ln_b"][i, 0], alpha, name=f"l{i}_ln1")
        s["hup"] = _mm(s["x1b"], W["ffn_w_up", i], name=f"l{i}_up")
        s["act"] = _convglu_fwd(s["hup"], P["ffn_conv_w"][i], P["ffn_conv_b"][i], name=f"l{i}_glu")
        s["f"] = _mm(s["act"], W["ffn_w_down", i], name=f"l{i}_down")
        xf, xb = _add_ln(s["x1"], s["f"], P["ln_g"][i, 1], P["ln_b"][i, 1], alpha, name=f"l{i}_ln2")
        saved.append(s)

    G, loss_row = _loss_grad(xf, tgt, name="loss")

    gw = {}

    def dw(key, layer, a, b, name, **kw):
        gw[key] = _mm(a, b, ta=True, out_dtype=BF16, out_into=(gw.get(key), n_layers[key], layer), name=name, **kw)

    gp = {k: [None] * n_a for k in ("a_ln_g", "a_ln_b", "a_w_s", "a_b_s")}
    gp.update({k: [None] * depth for k in ("ffn_conv_w", "ffn_conv_b", "ln_g", "ln_b")})
    dk_parts, dv_parts, dbias_parts = [], [], []
    for i in reversed(range(depth)):
        s = saved[i]
        dr2, dr2b, dg2, db2 = _ln_bwd(G, s["x1"], s["f"], P["ln_g"][i, 1], alpha, name=f"l{i}_ln2_bwd")
        dw("ffn_w_down", i, s["act"], dr2b, f"l{i}_down_dw")
        dact = _mm(dr2b, W["ffn_w_down", i], tb=True, name=f"l{i}_down_dx")
        dhup, dcw, dcb = _convglu_bwd(s["hup"], dact, P["ffn_conv_w"][i], P["ffn_conv_b"][i], name=f"l{i}_glu_bwd")
        gp["ffn_conv_w"][i] = dcw.transpose(1, 0, 2).reshape(dcw.shape[1], -1)
        gp["ffn_conv_b"][i] = dcb.reshape(-1)
        dw("ffn_w_up", i, s["x1b"], dhup, f"l{i}_up_dw", b_halves=True)
        token = ship(gw, [("ffn_w_down", i), ("ffn_w_up", i)])
        G1 = _mm(dhup, W["ffn_w_up", i], tb=True, a_halves=True, add=dr2, add_scale=alpha, after=token,
                 name=f"l{i}_up_dx")
        dr1, dr1b, dg1, db1 = _ln_bwd(G1, s["x"], s["h"], P["ln_g"][i, 0], alpha, name=f"l{i}_ln1_bwd")
        gp["ln_g"][i] = jnp.concatenate([dg1, dg2], axis=0)
        gp["ln_b"][i] = jnp.concatenate([db1, db2], axis=0)
        if i < n_a:
            dw("a_w_out", i, s["y"], dr1b, f"a{i}_out_dw")
            dy = _mm(dr1b, W["a_w_out", i], tb=True, name=f"a{i}_out_dx")
            dzp, dws, dbs, dlg, dlb = _sgu_bwd(s["zp"], dy, P["a_w_s"][i], P["a_b_s"][i], P["a_ln_g"][i],
                                               P["a_ln_b"][i], name=f"a{i}_sgu_bwd")
            gp["a_w_s"][i], gp["a_b_s"][i] = dws, dbs[:, :dws.shape[0]].T
            gp["a_ln_g"][i], gp["a_ln_b"][i] = dlg[0], dlb[0]
            dw("a_w_in", i, s["xb"], dzp, f"a{i}_in_dw")
            token = ship(gw, [("a_w_out", i), ("a_w_in", i)])
            G = _mm(dzp, W["a_w_in", i], tb=True, add=dr1, add_scale=alpha, after=token, name=f"a{i}_in_dx")
        else:
            j = i - n_a
            dw("b_w_o", j, s["ob"], dr1b, f"b{j}_o_dw")
            do_tok = _mm(dr1b, W["b_w_o", j], tb=True, name=f"b{j}_o_dx")
            delta, dob = _attn_bwd_prep(do_tok, s["oc"], name=f"b{j}_prep")
            dqs, dbs = [], []
            for g, d in enumerate(dils):
                dq, dk, dv, db = _attn_bwd(s["qg"][g], kg[g], vg[g], bias, g, d, _to_group(dob, d),
                                           _to_group(s["lse"], d), _to_group(delta, d), name=f"b{j}_attn{g}_bwd")
                dqs.append(_from_group(dq, d))
                dk_parts.append(_from_group(dk, d))
                dv_parts.append(_from_group(dv, d))
                dbs.append(db)
            dbias_parts.append(jnp.stack(dbs))
            dq_tok = jnp.concatenate(dqs, axis=1)
            dw("b_w_q", j, s["xb"], dq_tok, f"b{j}_q_dw")
            token = ship(gw, [("b_w_o", j), ("b_w_q", j)])
            G = _mm(dq_tok, W["b_w_q", j], tb=True, add=dr1, add_scale=alpha, after=token, name=f"b{j}_q_dx")
            if j == 0:
                dkv = jnp.concatenate([_addn(dk_parts, BF16, name="dk_sum"), _addn(dv_parts, BF16, name="dv_sum")], axis=1)
                dw("kv_w", 0, s["xb"], dkv, "kv_dw")
                token = ship(gw, [("kv_w", 0)])
                G = _mm(dkv, W["kv_w", 0], tb=True, add=G, add_scale=1.0, after=token, name="kv_dx")
    rel = _rel_grad(dbias_parts, name="rel_grad")
    grel = rel[:, :, :H].transpose(1, 0, 2).reshape(REL_BUCKETS, N_GROUPS * H)
    gsmall = {k: jnp.stack(v) for k, v in gp.items()}
    gsmall["rel_table"] = grel
    return loss_row, G, gw, gsmall


BIG = (("a_w_in", COL), ("a_w_out", ROW), ("kv_w", ROW), ("b_w_q", COL), ("b_w_o", COL),
       ("ffn_w_up", COL), ("ffn_w_down", ROW))
SMALL_SHARDED = ("a_ln_g", "a_ln_b", "ffn_conv_w", "ln_g", "ln_b")
SMALL_REPLICATED = ("a_w_s", "a_b_s", "rel_table", "ffn_conv_b")
WEIGHTS = ("a_w_in", "a_ln_g", "a_ln_b", "a_w_s", "a_b_s", "a_w_out", "kv_w", "b_w_q", "b_w_o", "rel_table",
           "ffn_w_up", "ffn_conv_w", "ffn_conv_b", "ffn_w_down", "ln_g", "ln_b")


def kernel(x, a_w_in, a_ln_g, a_ln_b, a_w_s, a_b_s, a_w_out, kv_w, b_w_q, b_w_o, rel_table, ffn_w_up, ffn_conv_w, ffn_conv_b, ffn_w_down, ln_g, ln_b, loss_target, m_a_w_in, m_a_ln_g, m_a_ln_b, m_a_w_s, m_a_b_s, m_a_w_out, m_kv_w, m_b_w_q, m_b_w_o, m_rel_table, m_ffn_w_up, m_ffn_conv_w, m_ffn_conv_b, m_ffn_w_down, m_ln_g, m_ln_b, v_a_w_in, v_a_ln_g, v_a_ln_b, v_a_w_s, v_a_b_s, v_a_w_out, v_kv_w, v_b_w_q, v_b_w_o, v_rel_table, v_ffn_w_up, v_ffn_conv_w, v_ffn_conv_b, v_ffn_w_down, v_ln_g, v_ln_b):
    w = dict(a_w_in=a_w_in, a_ln_g=a_ln_g, a_ln_b=a_ln_b, a_w_s=a_w_s, a_b_s=a_b_s, a_w_out=a_w_out, kv_w=kv_w,
             b_w_q=b_w_q, b_w_o=b_w_o, rel_table=rel_table, ffn_w_up=ffn_w_up, ffn_conv_w=ffn_conv_w,
             ffn_conv_b=ffn_conv_b, ffn_w_down=ffn_w_down, ln_g=ln_g, ln_b=ln_b)
    m = dict(a_w_in=m_a_w_in, a_ln_g=m_a_ln_g, a_ln_b=m_a_ln_b, a_w_s=m_a_w_s, a_b_s=m_a_b_s, a_w_out=m_a_w_out,
             kv_w=m_kv_w, b_w_q=m_b_w_q, b_w_o=m_b_w_o, rel_table=m_rel_table, ffn_w_up=m_ffn_w_up,
             ffn_conv_w=m_ffn_conv_w, ffn_conv_b=m_ffn_conv_b, ffn_w_down=m_ffn_w_down, ln_g=m_ln_g, ln_b=m_ln_b)
    v = dict(a_w_in=v_a_w_in, a_ln_g=v_a_ln_g, a_ln_b=v_a_ln_b, a_w_s=v_a_w_s, a_b_s=v_a_b_s, a_w_out=v_a_w_out,
             kv_w=v_kv_w, b_w_q=v_b_w_q, b_w_o=v_b_w_o, rel_table=v_rel_table, ffn_w_up=v_ffn_w_up,
             ffn_conv_w=v_ffn_conv_w, ffn_conv_b=v_ffn_conv_b, ffn_w_down=v_ffn_w_down, ln_g=v_ln_g, ln_b=v_ln_b)
    chip = 2 * lax.axis_index("x") + lax.axis_index("y")

    big_names = [n for n, _ in BIG]
    big_axes = [a for _, a in BIG]
    where = jnp.stack([chip, lax.axis_index("c"), 2 * chip + lax.axis_index("c")]).astype(jnp.int32)
    shards = [w[n].reshape((1,) + w[n].shape) if w[n].ndim == 2 else w[n] for n in big_names]
    small_shapes = [w[n].shape for n in SMALL_SHARDED]
    small = _pack([w[n] for n in SMALL_SHARDED])
    shard_of = dict(zip(big_names, shards))
    axis_of = dict(BIG)
    n_layers = {n: s.shape[0] for n, s in shard_of.items()}
    n_a = w["a_w_s"].shape[0]
    stages = {"0a": [("a_w_in", 0), ("small", 0)], "0b": [("a_w_out", 0), ("ffn_w_up", 0), ("ffn_w_down", 0)]}
    for i in range(1, w["ffn_w_up"].shape[0]):
        if i < n_a:
            mixer = [("a_w_in", i), ("a_w_out", i)]
        else:
            mixer = ([("kv_w", 0)] if i == n_a else []) + [("b_w_q", i - n_a), ("b_w_o", i - n_a)]
        stages[f"{i}a"] = mixer + [("ffn_w_up", i), ("ffn_w_down", i)]
    order = [key for st in stages.values() for key in st]
    assert len(order) == sum(n_layers.values()) + 1
    placed, axes = [], []
    for n, l in order:
        if n == "small":
            placed.append(_place_shard(small[None], 0, ROW, where, F32, name="place_small"))
            axes.append(ROW)
        else:
            placed.append(_place_shard(shard_of[n], l, axis_of[n], where, BF16, name=f"place_{n}{l}"))
            axes.append(axis_of[n])
    send_sems, recv_sems, flying = _gather_start(placed, axes, name="gather_start")

    def fetch(stage, after):
        if stage not in stages:
            return {}, {}
        idxs = [order.index(key) for key in stages[stage]]
        landed = _gather_wait(send_sems, recv_sems, [flying[i] for i in idxs], [axes[i] for i in idxs], idxs, after,
                              name=f"gather_wait_{stage}")
        W_new, P_new = {}, {}
        for key, arr in zip(stages[stage], landed):
            if key[0] == "small":
                small_all = arr.reshape((N_CHIPS,) + small.shape)
                per_chip = [_unpack(small_all[j], small_shapes) for j in range(N_CHIPS)]
                for i, n in enumerate(SMALL_SHARDED):
                    P_new[n] = jnp.concatenate([per_chip[j][i] for j in range(N_CHIPS)], axis=-1)
            else:
                W_new[key] = arr.reshape(arr.shape[1:])
        return W_new, P_new

    full_shape = {n: (n_layers[n],) + placed[order.index((n, 0))].shape[1:] for n in big_names}
    slots = {n: lax.empty((N_DEV, n_layers[n]) + _part_geometry(full_shape[n], axis_of[n]), BF16) for n in big_names}
    sems, plan = [], []

    def ship(gw, items):
        names = [n for n, _ in items]
        send, recv, grads, landing, token = _scatter_start(
            [gw[n] for n in names], [slots[n] for n in names], [l for _, l in items], [axis_of[n] for n in names],
            name="scatter_start_" + "_".join(f"{n}{l}" for n, l in items))
        for n, g, s in zip(names, grads, landing):
            gw[n], slots[n] = g, s
        sems.append((send, recv))
        plan.append([(big_names.index(n), l) for n, l in items])
        return token

    loss_row, grad_x, gw, gsmall = _local_step(x[0], loss_target[0], {n: w[n] for n in SMALL_REPLICATED}, fetch,
                                               ship, n_layers)
    loss = lax.psum(loss_row[0, 0], ("x", "y", "c"))

    small_names = list(SMALL_SHARDED) + list(SMALL_REPLICATED)
    small_pack = _pack([gsmall[n] for n in small_names])
    own, landed = _scatter_wait(sems, [gw[n] for n in big_names], [slots[n] for n in big_names], big_axes, plan,
                                name="scatter_wait")
    sums = [_sum_parts(b, g, ax, where, name=f"sum_{n}") for (n, ax), b, g in zip(BIG, landed, own)]
    pairs = _pair_gather(sums, name="pair_gather")
    small_sum = _sum_small(_gather_small(small_pack, name="gather_small"), small_pack, where, name="sum_small")
    grad = {n: p.reshape(w[n].shape) for n, p in zip(big_names, pairs)}
    full_small = _unpack(small_sum, [gsmall[n].shape for n in small_names])
    for n, gfull in zip(small_names, full_small):
        if n in SMALL_SHARDED:
            width = w[n].shape[-1]
            grad[n] = lax.dynamic_slice_in_dim(gfull, chip * width, width, axis=-1)
        else:
            grad[n] = gfull

    delta, new_m, new_v = {}, {}, {}
    for n, _ in BIG:
        delta[n], new_m[n], new_v[n] = _adamw(w[n], grad[n], m[n], v[n], name=f"adamw_{n}")
    shapes = [w[n].shape for n in small_names]
    packed = _adamw(_pack([w[n] for n in small_names]), _pack([grad[n] for n in small_names]),
                    _pack([m[n] for n in small_names]), _pack([v[n] for n in small_names]), name="adamw_small")
    for out, res in zip((delta, new_m, new_v), packed):
        for n, a in zip(small_names, _unpack(res, shapes)):
            out[n] = a

    return (loss, grad_x[None], *[grad[n] for n in WEIGHTS], *[delta[n] for n in WEIGHTS],
            *[new_m[n] for n in WEIGHTS], *[new_v[n] for n in WEIGHTS])
```

```python
import functools
import math

import numpy as np
import jax
import jax.numpy as jnp
from jax import lax
from jax.experimental import pallas as pl
from jax.experimental.pallas import tpu as pltpu

F32 = jnp.float32
BF16 = jnp.bfloat16
MESH = pl.DeviceIdType.MESH

CHUNK = 128
HEAD_DIM = 64
DILATED_GROUPS = ((128, 1), (512, 4), (2048, 16))
N_GROUPS = len(DILATED_GROUPS)
BLK = 128
REL_BUCKETS = 32
REL_MAX_DIST = 2048
LN_EPS = 1e-5
NEG = -1e30
ADAM_LR = 0.001
ADAM_B1 = 0.9
ADAM_B2 = 0.999
ADAM_EPS = 1e-08
ADAM_WD = 0.01
ADAM_STEP = 10

N_CHIPS = 4
N_DEV = 8
LANES = 128
SUBLANES = 8
VMEM_LIMIT = 48 * 1024 * 1024

_SQRT_HALF = 0.7071067811865476
_INV_SQRT_2PI = 0.3989422804014327


def _cp(sem=None, vmem=VMEM_LIMIT):
    return pltpu.CompilerParams(dimension_semantics=sem, vmem_limit_bytes=vmem)


def _tile(dim, target, align=LANES):
    if dim <= target:
        return dim
    t = (target // align) * align
    while t >= align:
        if dim % t == 0:
            return t
        t -= align
    return dim


def _gelu(x):
    return 0.5 * x * (1.0 + lax.erf(x * _SQRT_HALF))


def _gelu_grad(x):
    return 0.5 * (1.0 + lax.erf(x * _SQRT_HALF)) + x * (_INV_SQRT_2PI * jnp.exp(-0.5 * x * x))


def _dot(a, b, ca, cb):
    return lax.dot_general(a, b, (((ca,), (cb,)), ((), ())), preferred_element_type=F32)


def _mn_tile(dim):
    return max(_tile(dim, 1024), _tile(dim, 1408))


def _mm(a, b, *, name, ta=False, tb=False, out_dtype=F32, tiles=None,
        a_lead=(), b_lead=(), a_halves=False, b_halves=False, add=None, add_scale=1.0, out_into=None, after=None):
    a2, b2 = a.shape[-2:], b.shape[-2:]
    M, K = (a2[1], a2[0]) if ta else a2
    Kb, N = (b2[1], b2[0]) if tb else b2
    if a_halves:
        K = 2 * K
    if b_halves:
        N = 2 * N
    assert K == Kb, (name, a.shape, b.shape)
    if tiles is None:
        k_target = 1024 if ta else (1408 if a_halves else (K if K <= 2816 else 2048))
        tm, tn = _mn_tile(M), _mn_tile(N // 2 if b_halves else N)
        tk = _tile(K // 2 if a_halves else K, k_target)
    else:
        tm = _tile(M, tiles[0])
        tn = _tile(N // 2 if b_halves else N, tiles[1])
        tk = _tile(K // 2 if a_halves else K, tiles[2])
    nm, nn, nk = M // tm, N // tn, K // tk
    nkh, nnh = nk // 2, nn // 2

    def lead(idx, rest):
        return tuple(idx) + tuple(rest)

    sq_a = (None,) * (a.ndim - 2)
    sq_b = (None,) * (b.ndim - 2)
    if a_halves:
        assert not ta and a.ndim == 3
        a_spec = pl.BlockSpec((None, tm, tk), lambda i, j, k: (k // nkh, i, k % nkh))
    elif ta:
        a_spec = pl.BlockSpec(sq_a + (tk, tm), lambda i, j, k: lead(a_lead, (k, i)))
    else:
        a_spec = pl.BlockSpec(sq_a + (tm, tk), lambda i, j, k: lead(a_lead, (i, k)))
    if b_halves:
        assert not tb and b.ndim == 3
        b_spec = pl.BlockSpec((None, tk, tn), lambda i, j, k: (j // nnh, k, j % nnh))
    elif tb:
        b_spec = pl.BlockSpec(sq_b + (tn, tk), lambda i, j, k: lead(b_lead, (j, k)))
    else:
        b_spec = pl.BlockSpec(sq_b + (tk, tn), lambda i, j, k: lead(b_lead, (k, j)))
    mn_spec = pl.BlockSpec((tm, tn), lambda i, j, k: (i, j))
    in_specs = [a_spec, b_spec]
    args = [a, b]
    if add is not None:
        in_specs.append(mn_spec)
        args.append(add)
    aliases = {}
    if out_into is None:
        o_spec, out_shape = mn_spec, jax.ShapeDtypeStruct((M, N), out_dtype)
    else:
        buf, n_layers, layer = out_into
        o_spec = pl.BlockSpec((None, tm, tn), lambda i, j, k: (layer, i, j))
        out_shape = jax.ShapeDtypeStruct((n_layers, M, N), out_dtype)
        if buf is not None:
            aliases = {len(args): 0}
            in_specs.append(pl.BlockSpec(memory_space=pl.ANY))
            args.append(buf)
    if after is not None:
        in_specs.append(pl.BlockSpec(memory_space=pl.ANY))
        args.append(after)
    n_in = len(args)
    ca, cb = (0 if ta else 1), (1 if tb else 0)

    def body(*refs):
        a_ref, b_ref = refs[:2]
        add_ref = refs[2] if add is not None else None
        o_ref = refs[n_in]

        def finish(acc):
            if add_ref is not None:
                acc = acc + add_scale * add_ref[...]
            o_ref[...] = acc.astype(o_ref.dtype)

        if nk == 1:
            finish(_dot(a_ref[...], b_ref[...], ca, cb))
        else:
            acc_ref = refs[-1]
            k = pl.program_id(2)

            @pl.when(k == 0)
            def _():
                acc_ref[...] = jnp.zeros_like(acc_ref)

            acc_ref[...] += _dot(a_ref[...], b_ref[...], ca, cb)

            @pl.when(k == nk - 1)
            def _():
                finish(acc_ref[...])

    return pl.pallas_call(
        body, name=name, grid=(nm, nn, nk),
        in_specs=in_specs, out_specs=o_spec, out_shape=out_shape,
        input_output_aliases=aliases,
        scratch_shapes=[] if nk == 1 else [pltpu.VMEM((tm, tn), F32)],
        compiler_params=_cp(("parallel", "parallel", "arbitrary")),
    )(*args)


def _ln_stats(r):
    mu = jnp.mean(r, axis=-1, keepdims=True)
    xc = r - mu
    var = jnp.mean(xc * xc, axis=-1, keepdims=True)
    rstd = lax.rsqrt(var + LN_EPS)
    return xc * rstd, rstd


def _add_ln(x, h, g, b, alpha, *, name):
    T, D = x.shape
    tr = _tile(T, 512, SUBLANES)
    row = pl.BlockSpec((tr, D), lambda i: (i, 0))
    vec = pl.BlockSpec((1, D), lambda i: (0, 0))

    def body(x_ref, h_ref, g_ref, b_ref, o_ref, ob_ref):
        xhat, _ = _ln_stats(alpha * x_ref[...] + h_ref[...])
        y = xhat * g_ref[...] + b_ref[...]
        o_ref[...] = y
        ob_ref[...] = y.astype(BF16)

    return pl.pallas_call(
        body, name=name, grid=(T // tr,),
        in_specs=[row, row, vec, vec], out_specs=[row, row],
        out_shape=[jax.ShapeDtypeStruct((T, D), F32), jax.ShapeDtypeStruct((T, D), BF16)],
        compiler_params=_cp(("parallel",)),
    )(x, h, g.reshape(1, D), b.reshape(1, D))


def _ln_bwd(dy, x, h, g, alpha, *, name):
    T, D = x.shape
    tr = _tile(T, 512, SUBLANES)
    row = pl.BlockSpec((tr, D), lambda i: (i, 0))
    vec = pl.BlockSpec((1, D), lambda i: (0, 0))

    def body(dy_ref, x_ref, h_ref, g_ref, dr_ref, drb_ref, dg_ref, db_ref):
        @pl.when(pl.program_id(0) == 0)
        def _():
            dg_ref[...] = jnp.zeros_like(dg_ref)
            db_ref[...] = jnp.zeros_like(db_ref)

        xhat, rstd = _ln_stats(alpha * x_ref[...] + h_ref[...])
        dyv = dy_ref[...]
        dxhat = dyv * g_ref[...]
        c1 = jnp.mean(dxhat, axis=-1, keepdims=True)
        c2 = jnp.mean(dxhat * xhat, axis=-1, keepdims=True)
        dr = rstd * (dxhat - c1 - xhat * c2)
        dr_ref[...] = dr
        drb_ref[...] = dr.astype(BF16)
        dg_ref[...] += jnp.sum(dyv * xhat, axis=0, keepdims=True)
        db_ref[...] += jnp.sum(dyv, axis=0, keepdims=True)

    return pl.pallas_call(
        body, name=name, grid=(T // tr,),
        in_specs=[row, row, row, vec], out_specs=[row, row, vec, vec],
        out_shape=[jax.ShapeDtypeStruct((T, D), F32), jax.ShapeDtypeStruct((T, D), BF16),
                   jax.ShapeDtypeStruct((1, D), F32), jax.ShapeDtypeStruct((1, D), F32)],
        compiler_params=_cp(("arbitrary",)),
    )(dy, x, h, g.reshape(1, D))


def _sgu_pieces(zp, lg, lb, E):
    z = _gelu(zp)
    u, v = z[:, :E], z[:, E:]
    vhat, rstd = _ln_stats(v)
    vn = vhat * lg + lb
    return u, vhat, rstd, vn


def _tril_mask():
    t = lax.broadcasted_iota(jnp.int32, (CHUNK, CHUNK), 0)
    s = lax.broadcasted_iota(jnp.int32, (CHUNK, CHUNK), 1)
    return s <= t


def _sgu_fwd(zp, ws, bs, lg, lb, *, name):
    T, E2 = zp.shape
    E = E2 // 2
    G = ws.shape[0]
    cw = E // G

    def body(zp_ref, ws_ref, bs_ref, lg_ref, lb_ref, y_ref):
        u, _, _, vn = _sgu_pieces(zp_ref[...], lg_ref[...], lb_ref[...], E)
        vnb = vn.astype(BF16)
        tril = _tril_mask()
        for g in range(G):
            cols = slice(g * cw, (g + 1) * cw)
            w = jnp.where(tril, ws_ref[g], 0.0).astype(BF16)
            sv = _dot(w, vnb[:, cols], 1, 0) + bs_ref[g]
            y_ref[:, cols] = (u[:, cols] * sv).astype(BF16)

    return pl.pallas_call(
        body, name=name, grid=(T // CHUNK,),
        in_specs=[pl.BlockSpec((CHUNK, E2), lambda n: (n, 0)),
                  pl.BlockSpec((G, CHUNK, CHUNK), lambda n: (0, 0, 0)),
                  pl.BlockSpec((G, CHUNK, 1), lambda n: (0, 0, 0)),
                  pl.BlockSpec((1, E), lambda n: (0, 0)),
                  pl.BlockSpec((1, E), lambda n: (0, 0))],
        out_specs=pl.BlockSpec((CHUNK, E), lambda n: (n, 0)),
        out_shape=jax.ShapeDtypeStruct((T, E), BF16),
        compiler_params=_cp(("parallel",)),
    )(zp, ws, bs.reshape(G, CHUNK, 1), lg.reshape(1, E), lb.reshape(1, E))


def _sgu_bwd(zp, dy, ws, bs, lg, lb, *, name):
    T, E2 = zp.shape
    E = E2 // 2
    G = ws.shape[0]
    cw = E // G
    assert G <= LANES

    def body(zp_ref, dy_ref, ws_ref, bs_ref, lg_ref, lb_ref,
             dzp_ref, dws_ref, dbs_ref, dlg_ref, dlb_ref, dvn_ref):
        @pl.when(pl.program_id(0) == 0)
        def _():
            dws_ref[...] = jnp.zeros_like(dws_ref)
            dbs_ref[...] = jnp.zeros_like(dbs_ref)
            dlg_ref[...] = jnp.zeros_like(dlg_ref)
            dlb_ref[...] = jnp.zeros_like(dlb_ref)

        zpv = zp_ref[...]
        lgv = lg_ref[...]
        u, vhat, rstd, vn = _sgu_pieces(zpv, lgv, lb_ref[...], E)
        gp = _gelu_grad(zpv)
        vnb = vn.astype(BF16)
        tril = _tril_mask()
        lane = lax.broadcasted_iota(jnp.int32, (CHUNK, LANES), 1)
        dbs = dbs_ref[...]
        for g in range(G):
            cols = slice(g * cw, (g + 1) * cw)
            w = jnp.where(tril, ws_ref[g], 0.0).astype(BF16)
            sv = _dot(w, vnb[:, cols], 1, 0) + bs_ref[g]
            dyg = dy_ref[:, cols]
            dzp_ref[:, cols] = (dyg * sv * gp[:, cols]).astype(BF16)
            dsv = dyg * u[:, cols]
            dsvb = dsv.astype(BF16)
            dws_ref[g] += jnp.where(tril, _dot(dsvb, vnb[:, cols], 1, 1), 0.0)
            dvn_ref[:, cols] = _dot(w, dsvb, 0, 0)
            dbs = dbs + jnp.where(lane == g, jnp.sum(dsv, axis=1, keepdims=True), 0.0)
        dbs_ref[...] = dbs
        dvn = dvn_ref[...]
        dlg_ref[...] += jnp.sum(dvn * vhat, axis=0, keepdims=True)
        dlb_ref[...] += jnp.sum(dvn, axis=0, keepdims=True)
        dvhat = dvn * lgv
        c1 = jnp.mean(dvhat, axis=-1, keepdims=True)
        c2 = jnp.mean(dvhat * vhat, axis=-1, keepdims=True)
        dv = rstd * (dvhat - c1 - vhat * c2)
        dzp_ref[:, E:] = (dv * gp[:, E:]).astype(BF16)

    vecE = pl.BlockSpec((1, E), lambda n: (0, 0))
    return pl.pallas_call(
        body, name=name, grid=(T // CHUNK,),
        in_specs=[pl.BlockSpec((CHUNK, E2), lambda n: (n, 0)),
                  pl.BlockSpec((CHUNK, E), lambda n: (n, 0)),
                  pl.BlockSpec((G, CHUNK, CHUNK), lambda n: (0, 0, 0)),
                  pl.BlockSpec((G, CHUNK, 1), lambda n: (0, 0, 0)),
                  vecE, vecE],
        out_specs=[pl.BlockSpec((CHUNK, E2), lambda n: (n, 0)),
                   pl.BlockSpec((G, CHUNK, CHUNK), lambda n: (0, 0, 0)),
                   pl.BlockSpec((CHUNK, LANES), lambda n: (0, 0)),
                   vecE, vecE],
        out_shape=[jax.ShapeDtypeStruct((T, E2), BF16),
                   jax.ShapeDtypeStruct((G, CHUNK, CHUNK), F32),
                   jax.ShapeDtypeStruct((CHUNK, LANES), F32),
                   jax.ShapeDtypeStruct((1, E), F32), jax.ShapeDtypeStruct((1, E), F32)],
        scratch_shapes=[pltpu.VMEM((CHUNK, E), F32)],
        compiler_params=_cp(("arbitrary",)),
    )(zp, dy, ws, bs.reshape(G, CHUNK, 1), lg.reshape(1, E), lb.reshape(1, E))


def _delay(h, k):
    t = lax.broadcasted_iota(jnp.int32, h.shape, 0)
    return jnp.where(t >= k, pltpu.roll(h, k, 0), 0.0)


def _advance(d, k):
    T = d.shape[0]
    t = lax.broadcasted_iota(jnp.int32, d.shape, 0)
    return jnp.where(t < T - k, pltpu.roll(d, T - k, 0), 0.0)


def _conv3(h, w_ref, b_ref):
    return w_ref[2:3, :] * h + w_ref[1:2, :] * _delay(h, 1) + w_ref[0:1, :] * _delay(h, 2) + b_ref[...]


def _convglu_fwd(hup, cw, cb, *, name):
    T, F2 = hup.shape
    F = F2 // 2
    tc = LANES
    nt = F // tc

    def body(ha_ref, hg_ref, wa_ref, wg_ref, ba_ref, bg_ref, o_ref):
        a = _conv3(ha_ref[...], wa_ref, ba_ref)
        g = _conv3(hg_ref[...], wg_ref, bg_ref)
        o_ref[...] = (_gelu(a) * g).astype(BF16)

    col = lambda off: pl.BlockSpec((T, tc), lambda j: (0, j + off))
    w3 = lambda off: pl.BlockSpec((3, tc), lambda j: (0, j + off))
    b1 = lambda off: pl.BlockSpec((1, tc), lambda j: (0, j + off))
    return pl.pallas_call(
        body, name=name, grid=(nt,),
        in_specs=[col(0), col(nt), w3(0), w3(nt), b1(0), b1(nt)],
        out_specs=pl.BlockSpec((T, tc), lambda j: (0, j)),
        out_shape=jax.ShapeDtypeStruct((T, F), BF16),
        compiler_params=_cp(("parallel",)),
    )(hup, hup, cw, cw, cb.reshape(1, F2), cb.reshape(1, F2))


def _convglu_bwd(hup, dact, cw, cb, *, name):
    T, F2 = hup.shape
    F = F2 // 2
    tc = LANES
    nt = F // tc

    def half(h, w_ref, d, dh_ref, dw_ref, db_ref, i):
        d1, d2 = _advance(d, 1), _advance(d, 2)
        dh_ref[i] = (w_ref[2:3, :] * d + w_ref[1:2, :] * d1 + w_ref[0:1, :] * d2).astype(BF16)
        dw_ref[i, 0:1, :] = jnp.sum(d2 * h, axis=0, keepdims=True)
        dw_ref[i, 1:2, :] = jnp.sum(d1 * h, axis=0, keepdims=True)
        dw_ref[i, 2:3, :] = jnp.sum(d * h, axis=0, keepdims=True)
        db_ref[i] = jnp.sum(d, axis=0, keepdims=True)

    def body(ha_ref, hg_ref, d_ref, wa_ref, wg_ref, ba_ref, bg_ref, dh_ref, dw_ref, db_ref):
        ha, hg = ha_ref[...], hg_ref[...]
        a = _conv3(ha, wa_ref, ba_ref)
        g = _conv3(hg, wg_ref, bg_ref)
        d = d_ref[...]
        half(ha, wa_ref, d * g * _gelu_grad(a), dh_ref, dw_ref, db_ref, 0)
        half(hg, wg_ref, d * _gelu(a), dh_ref, dw_ref, db_ref, 1)

    col = lambda off: pl.BlockSpec((T, tc), lambda j: (0, j + off))
    w3 = lambda off: pl.BlockSpec((3, tc), lambda j: (0, j + off))
    b1 = lambda off: pl.BlockSpec((1, tc), lambda j: (0, j + off))
    return pl.pallas_call(
        body, name=name, grid=(nt,),
        in_specs=[col(0), col(nt), col(0), w3(0), w3(nt), b1(0), b1(nt)],
        out_specs=[pl.BlockSpec((2, T, tc), lambda j: (0, 0, j)),
                   pl.BlockSpec((2, 3, tc), lambda j: (0, 0, j)),
                   pl.BlockSpec((2, 1, tc), lambda j: (0, 0, j))],
        out_shape=[jax.ShapeDtypeStruct((2, T, F), BF16),
                   jax.ShapeDtypeStruct((2, 3, F), F32),
                   jax.ShapeDtypeStruct((2, 1, F), F32)],
        compiler_params=_cp(("parallel",)),
    )(hup, hup, dact, cw, cw, cb.reshape(1, F2), cb.reshape(1, F2))


def _bucket_table():
    iq = np.arange(BLK)[:, None]
    ik = np.arange(2 * BLK)[None, :]
    delta = iq + BLK - ik
    out = []
    for win, dil in DILATED_GROUPS:
        valid = (delta >= 0) & (delta <= win // dil)
        n = (np.clip(delta, 0, None) * dil).astype(np.int32)
        max_exact = REL_BUCKETS // 2
        nf = np.maximum(n, 1).astype(np.float32)
        large = max_exact + (np.log(nf / np.float32(max_exact)) / np.float32(math.log(REL_MAX_DIST / max_exact))
                             * np.float32(REL_BUCKETS - max_exact)).astype(np.int32)
        large = np.minimum(large, REL_BUCKETS - 1)
        out.append(np.where(valid, np.where(n < max_exact, n, large), -1))
    return np.stack(out).astype(np.int32)


def _band_bias(rel_table, H, *, name):
    bkt = jnp.asarray(_bucket_table())

    def body(tab_ref, bk_ref, o_ref):
        g = pl.program_id(0)
        bk = bk_ref[...]
        prev = lax.broadcasted_iota(jnp.int32, bk.shape, 1) < BLK
        for h in range(H):
            def bucket(c, acc):
                return jnp.where(bk == c, tab_ref[c, g * H + h], acc)

            b = lax.fori_loop(0, REL_BUCKETS, bucket, jnp.zeros(bk.shape, F32))
            b = jnp.where(bk >= 0, b, NEG)
            o_ref[h, 0] = b
            o_ref[h, 1] = jnp.where(prev, NEG, b)

    return pl.pallas_call(
        body, name=name, grid=(N_GROUPS,),
        in_specs=[pl.BlockSpec(memory_space=pltpu.SMEM),
                  pl.BlockSpec((None, BLK, 2 * BLK), lambda g: (g, 0, 0))],
        out_specs=pl.BlockSpec((None, H, 2, BLK, 2 * BLK), lambda g: (g, 0, 0, 0, 0)),
        out_shape=jax.ShapeDtypeStruct((N_GROUPS, H, 2, BLK, 2 * BLK), F32),
        compiler_params=_cp(("parallel",)),
    )(rel_table, bkt)


PAIR = 2 * HEAD_DIM


def _bdot(a, b, ca, cb):
    return lax.dot_general(a, b, (((ca,), (cb,)), ((0,), (0,))), preferred_element_type=F32)


def _heads(ref, H, own_columns_only):
    low = lax.broadcasted_iota(jnp.int32, (BLK, PAIR), 1) < HEAD_DIM
    out = []
    for h in range(H):
        a = ref[:, (h // 2) * PAIR:(h // 2 + 1) * PAIR]
        if own_columns_only:
            a = jnp.where(low if h % 2 == 0 else jnp.logical_not(low), a, jnp.zeros_like(a))
        out.append(a)
    return jnp.stack(out)


def _pair_sums(o):
    return [o[2 * p] + o[2 * p + 1] for p in range(o.shape[0] // 2)]


def _store_pairs(ref, o, dtype, scale=None):
    for p, v in enumerate(_pair_sums(o)):
        ref[:, p * PAIR:(p + 1) * PAIR] = (v if scale is None else scale * v).astype(dtype)


def _stat_tile(s):
    lane = lax.broadcasted_iota(jnp.int32, (BLK, LANES), 1)
    t = jnp.zeros((BLK, LANES), F32)
    for h in range(s.shape[0]):
        t = jnp.where(lane == h, s[h], t)
    return t


def _stat_cols(ref, H):
    return jnp.stack([ref[:, h:h + 1] for h in range(H)])


def _scores(qm, kp, kc, bias):
    scale = HEAD_DIM ** -0.5
    sp = _bdot(qm, kp, 2, 2) * scale + bias[:, :, :BLK]
    sc = _bdot(qm, kc, 2, 2) * scale + bias[:, :, BLK:]
    return sp, sc


def _attn_specs(nb, dil, g, H, HD, qb):
    nbg = nb // dil
    cur = lambda col: pl.BlockSpec((BLK, HD), lambda b: (qb(b), col))
    prev = lambda col: pl.BlockSpec((BLK, HD), lambda b: (jnp.maximum(qb(b) - 1, 0), col))
    stat = pl.BlockSpec((BLK, LANES), lambda b: (qb(b), 0))
    bias = pl.BlockSpec((None, H, None, BLK, 2 * BLK),
                        lambda b: (g, 0, jnp.where(qb(b) % nbg == 0, 1, 0), 0, 0))
    return cur, prev, stat, bias


def _attn_fwd(q, k, v, bias, g, dil, *, name):
    (qa, qc), (ka, kc_), (va, vc_) = q, k, v
    T = qa.shape[0]
    H = bias.shape[1]
    HD = H * HEAD_DIM
    assert H % 2 == 0 and H <= LANES and T % (dil * BLK) == 0
    nb = T // BLK
    cur, prev, stat, bspec = _attn_specs(nb, dil, g, H, HD, lambda b: b)

    def body(q_ref, kp_ref, kc_ref, vp_ref, vc_ref, b_ref, o_ref, l_ref):
        qm, kp, kc = _heads(q_ref, H, True), _heads(kp_ref, H, False), _heads(kc_ref, H, False)
        sp, sc = _scores(qm, kp, kc, b_ref[...])
        m = jnp.maximum(jnp.max(sp, axis=-1, keepdims=True), jnp.max(sc, axis=-1, keepdims=True))
        pp, pc = jnp.exp(sp - m), jnp.exp(sc - m)
        den = jnp.sum(pp, axis=-1, keepdims=True) + jnp.sum(pc, axis=-1, keepdims=True)
        o = (_bdot((pp / den).astype(BF16), _heads(vp_ref, H, True), 2, 1)
             + _bdot((pc / den).astype(BF16), _heads(vc_ref, H, True), 2, 1))
        _store_pairs(o_ref, o, F32)
        l_ref[...] = _stat_tile(m + jnp.log(den))

    return pl.pallas_call(
        body, name=name, grid=(nb,),
        in_specs=[cur(qc), prev(kc_), cur(kc_), prev(vc_), cur(vc_), bspec],
        out_specs=[cur(0), stat],
        out_shape=[jax.ShapeDtypeStruct((T, HD), F32), jax.ShapeDtypeStruct((T, LANES), F32)],
        compiler_params=_cp(("parallel",)),
    )(qa, ka, ka, va, va, bias)


def _attn_combine(os, ls, *, name):
    n = len(os)
    T, HD = os[0].shape
    H = HD // HEAD_DIM
    tr = _tile(T, 512, SUBLANES)
    wide = pl.BlockSpec((tr, HD), lambda i: (i, 0))
    stat = pl.BlockSpec((tr, LANES), lambda i: (i, 0))

    def body(*refs):
        o_refs, l_refs = refs[:n], refs[n:2 * n]
        oc_ref, ocb_ref, lt_ref = refs[2 * n:]
        ls_ = [r[...] for r in l_refs]
        m = ls_[0]
        for l in ls_[1:]:
            m = jnp.maximum(m, l)
        ws = [jnp.exp(l - m) for l in ls_]
        den = ws[0]
        for w in ws[1:]:
            den = den + w
        lt_ref[...] = m + jnp.log(den)
        ws = [w / den for w in ws]
        low = lax.broadcasted_iota(jnp.int32, (tr, PAIR), 1) < HEAD_DIM
        for p in range(H // 2):
            cols = slice(p * PAIR, (p + 1) * PAIR)
            acc = jnp.zeros((tr, PAIR), F32)
            for g in range(n):
                w = jnp.where(low, ws[g][:, 2 * p:2 * p + 1], ws[g][:, 2 * p + 1:2 * p + 2])
                acc = acc + w * o_refs[g][:, cols]
            oc_ref[:, cols] = acc
            ocb_ref[:, cols] = acc.astype(BF16)

    return pl.pallas_call(
        body, name=name, grid=(T // tr,),
        in_specs=[wide] * n + [stat] * n, out_specs=[wide, wide, stat],
        out_shape=[jax.ShapeDtypeStruct((T, HD), F32), jax.ShapeDtypeStruct((T, HD), BF16),
                   jax.ShapeDtypeStruct((T, LANES), F32)],
        compiler_params=_cp(("parallel",)),
    )(*os, *ls)


def _attn_bwd_prep(do, oc, *, name):
    T, HD = do.shape
    H = HD // HEAD_DIM
    tr = _tile(T, 512, SUBLANES)
    wide = pl.BlockSpec((tr, HD), lambda i: (i, 0))
    stat = pl.BlockSpec((tr, LANES), lambda i: (i, 0))

    def body(do_ref, oc_ref, dl_ref, dob_ref):
        low = lax.broadcasted_iota(jnp.int32, (tr, PAIR), 1) < HEAD_DIM
        lane = lax.broadcasted_iota(jnp.int32, (tr, LANES), 1)
        dl = jnp.zeros((tr, LANES), F32)
        for p in range(H // 2):
            cols = slice(p * PAIR, (p + 1) * PAIR)
            d = do_ref[:, cols]
            prod = d * oc_ref[:, cols]
            dl = jnp.where(lane == 2 * p, jnp.sum(jnp.where(low, prod, 0.0), axis=-1, keepdims=True), dl)
            dl = jnp.where(lane == 2 * p + 1, jnp.sum(jnp.where(low, 0.0, prod), axis=-1, keepdims=True), dl)
            dob_ref[:, cols] = d.astype(BF16)
        dl_ref[...] = dl

    return pl.pallas_call(
        body, name=name, grid=(T // tr,),
        in_specs=[wide, wide], out_specs=[stat, wide],
        out_shape=[jax.ShapeDtypeStruct((T, LANES), F32), jax.ShapeDtypeStruct((T, HD), BF16)],
        compiler_params=_cp(("parallel",)),
    )(do, oc)


def _attn_bwd(q, k, v, bias, g, dil, do, lse, delta, *, name):
    (qa, qc), (ka, kc_), (va, vc_) = q, k, v
    T, HD = do.shape
    H = HD // HEAD_DIM
    nb = T // BLK
    scale = HEAD_DIM ** -0.5
    cur, prev, stat, bspec = _attn_specs(nb, dil, g, H, HD, lambda b: jnp.minimum(b, nb - 1))
    late = pl.BlockSpec((BLK, HD), lambda b: (jnp.maximum(b - 1, 0), 0))

    def body(q_ref, kp_ref, kc_ref, vp_ref, vc_ref, b_ref, do_ref, l_ref, dl_ref,
             dq_ref, dk_ref, dv_ref, db_ref, ck_ref, cv_ref):
        b = pl.program_id(0)

        @pl.when(b == 0)
        def _():
            db_ref[...] = jnp.zeros_like(db_ref)
            ck_ref[...] = jnp.zeros_like(ck_ref)
            cv_ref[...] = jnp.zeros_like(cv_ref)

        @pl.when(b < nb)
        def _():
            qm, dom = _heads(q_ref, H, True), _heads(do_ref, H, True)
            sp, sc = _scores(qm, _heads(kp_ref, H, False), _heads(kc_ref, H, False), b_ref[...])
            l, dl = _stat_cols(l_ref, H), _stat_cols(dl_ref, H)
            pp, pc = jnp.exp(sp - l), jnp.exp(sc - l)
            dsp = pp * (_bdot(dom, _heads(vp_ref, H, False), 2, 2) - dl)
            dsc = pc * (_bdot(dom, _heads(vc_ref, H, False), 2, 2) - dl)
            db_ref[:, :, :BLK] += dsp
            db_ref[:, :, BLK:] += dsc
            dspb, dscb = dsp.astype(BF16), dsc.astype(BF16)
            _store_pairs(dq_ref, _bdot(dspb, _heads(kp_ref, H, True), 2, 1)
                         + _bdot(dscb, _heads(kc_ref, H, True), 2, 1), BF16, scale)
            dkp = _pair_sums(_bdot(dspb, qm, 1, 1))
            dkc = _pair_sums(_bdot(dscb, qm, 1, 1))
            dvp = _pair_sums(_bdot(pp.astype(BF16), dom, 1, 1))
            dvc = _pair_sums(_bdot(pc.astype(BF16), dom, 1, 1))
            for p in range(H // 2):
                cols = slice(p * PAIR, (p + 1) * PAIR)
                dk_ref[:, cols] = ck_ref[:, cols] + scale * dkp[p]
                dv_ref[:, cols] = cv_ref[:, cols] + dvp[p]
                ck_ref[:, cols] = scale * dkc[p]
                cv_ref[:, cols] = dvc[p]

        @pl.when(b == nb)
        def _():
            dk_ref[...] = ck_ref[...]
            dv_ref[...] = cv_ref[...]

    f32 = jax.ShapeDtypeStruct((T, HD), F32)
    return pl.pallas_call(
        body, name=name, grid=(nb + 1,),
        in_specs=[cur(qc), prev(kc_), cur(kc_), prev(vc_), cur(vc_), bspec, cur(0), stat, stat],
        out_specs=[cur(0), late, late, pl.BlockSpec((H, BLK, 2 * BLK), lambda b: (0, 0, 0))],
        out_shape=[jax.ShapeDtypeStruct((T, HD), BF16), f32, f32,
                   jax.ShapeDtypeStruct((H, BLK, 2 * BLK), F32)],
        scratch_shapes=[pltpu.VMEM((BLK, HD), F32), pltpu.VMEM((BLK, HD), F32)],
        compiler_params=_cp(("arbitrary",)),
    )(qa, ka, ka, va, va, bias, do, lse, delta)


def _rel_grad(dbs, *, name):
    H = dbs[0].shape[1]
    n = len(dbs)
    assert H <= LANES
    bkt = jnp.asarray(_bucket_table())

    def body(*refs):
        bk = refs[n][...]
        o_ref = refs[n + 1]
        db = refs[0][...]
        for r in refs[1:n]:
            db = db + r[...]
        row = lax.broadcasted_iota(jnp.int32, (REL_BUCKETS, LANES), 0)
        col = lax.broadcasted_iota(jnp.int32, (REL_BUCKETS, LANES), 1)

        def bucket(c, acc):
            sel = bk == c
            for h in range(H):
                val = jnp.sum(jnp.where(sel, db[h], 0.0))
                acc = jnp.where((row == c) & (col == h), val, acc)
            return acc

        o_ref[...] = lax.fori_loop(0, REL_BUCKETS, bucket, jnp.zeros((REL_BUCKETS, LANES), F32))

    dspec = pl.BlockSpec((None, H, BLK, 2 * BLK), lambda g: (g, 0, 0, 0))
    return pl.pallas_call(
        body, name=name, grid=(N_GROUPS,),
        in_specs=[dspec] * n + [pl.BlockSpec((None, BLK, 2 * BLK), lambda g: (g, 0, 0))],
        out_specs=pl.BlockSpec((None, REL_BUCKETS, LANES), lambda g: (g, 0, 0)),
        out_shape=jax.ShapeDtypeStruct((N_GROUPS, REL_BUCKETS, LANES), F32),
        compiler_params=_cp(("parallel",)),
    )(*dbs, bkt)


def _to_group(a, d):
    T, C = a.shape
    return a if d == 1 else a.reshape(T // d, d, C).transpose(1, 0, 2).reshape(T, C)


def _from_group(a, d):
    T, C = a.shape
    return a if d == 1 else a.reshape(d, T // d, C).transpose(1, 0, 2).reshape(T, C)


def _rows_view(a):
    a2 = a.reshape(-1, a.shape[-1])
    R, C = a2.shape
    tr = _tile(R, max(SUBLANES, (1 << 18) // C), SUBLANES)
    return a2, R, C, tr


def _addn(xs, out_dtype, *, name):
    shape = xs[0].shape
    x2s = [_rows_view(x)[0] for x in xs]
    _, R, C, tr = _rows_view(xs[0])
    spec = pl.BlockSpec((tr, C), lambda i: (i, 0))

    def body(*refs):
        acc = refs[0][...].astype(F32)
        for r in refs[1:-1]:
            acc = acc + r[...].astype(F32)
        refs[-1][...] = acc.astype(out_dtype)

    out = pl.pallas_call(
        body, name=name, grid=(R // tr,),
        in_specs=[spec] * len(xs), out_specs=spec,
        out_shape=jax.ShapeDtypeStruct((R, C), out_dtype),
        compiler_params=_cp(("parallel",)),
    )(*x2s)
    return out.reshape(shape)


def _slot_sum(b_ref, own, my_id):
    acc = None
    for s in range(b_ref.shape[0]):
        term = jnp.where(my_id == s, own, b_ref[s].astype(F32))
        acc = term if acc is None else acc + term
    return acc


def _part_geometry(shape, axis):
    _, R, C = shape
    if axis == COL:
        return R // 2, C // N_CHIPS
    return R // (2 * N_CHIPS), C


def _sum_parts(buf, grad, axis, where, *, name):
    n, L, Rp, Cp = buf.shape
    assert (Rp, Cp) == _part_geometry(grad.shape, axis)
    tr = _tile(Rp, max(2 * SUBLANES, (1 << 17) // Cp), 2 * SUBLANES)
    nr = Rp // tr

    def own_map(l, i, w):
        if axis == COL:
            return (l, w[1] * nr + i, w[0])
        return (l, (2 * w[0] + w[1]) * nr + i, 0)

    def body(w_ref, b_ref, g_ref, o_ref):
        o_ref[...] = _slot_sum(b_ref, g_ref[...].astype(F32), w_ref[2])

    return pl.pallas_call(
        body, name=name,
        grid_spec=pltpu.PrefetchScalarGridSpec(
            num_scalar_prefetch=1, grid=(L, nr),
            in_specs=[pl.BlockSpec((n, None, tr, Cp), lambda l, i, w: (0, l, i, 0)),
                      pl.BlockSpec((None, tr, Cp), own_map)],
            out_specs=pl.BlockSpec((None, None, tr, Cp), lambda l, i, w: (l, w[1], i, 0))),
        out_shape=jax.ShapeDtypeStruct((L, 2, Rp, Cp), F32),
        compiler_params=_cp(("parallel", "parallel")),
    )(where, buf, grad)


def _sum_small(buf, own, where, *, name):
    n, R, C = buf.shape
    tr = _tile(R, 512, SUBLANES)

    def body(w_ref, b_ref, g_ref, o_ref):
        o_ref[...] = _slot_sum(b_ref, g_ref[...], w_ref[2])

    return pl.pallas_call(
        body, name=name,
        grid_spec=pltpu.PrefetchScalarGridSpec(
            num_scalar_prefetch=1, grid=(R // tr,),
            in_specs=[pl.BlockSpec((n, tr, C), lambda i, w: (0, i, 0)),
                      pl.BlockSpec((tr, C), lambda i, w: (i, 0))],
            out_specs=pl.BlockSpec((tr, C), lambda i, w: (i, 0))),
        out_shape=jax.ShapeDtypeStruct((R, C), F32),
        compiler_params=_cp(("parallel",)),
    )(where, buf, own)


def _place_shard(shard, layer, axis, where, out_dtype, *, name):
    _, R, C = shard.shape
    tr = _tile(R, max(2 * SUBLANES, (1 << 18) // C), 2 * SUBLANES)
    nr = R // tr
    if axis == COL:
        full, out_map = (1, R, C * N_CHIPS), (lambda i, w: (0, i, w[0]))
    else:
        full, out_map = (1, R * N_CHIPS, C), (lambda i, w: (0, w[0] * nr + i, 0))

    def body(w_ref, x_ref, o_ref):
        o_ref[...] = x_ref[...].astype(out_dtype)

    return pl.pallas_call(
        body, name=name,
        grid_spec=pltpu.PrefetchScalarGridSpec(
            num_scalar_prefetch=1, grid=(nr,),
            in_specs=[pl.BlockSpec((None, tr, C), lambda i, w: (layer, i, 0))],
            out_specs=pl.BlockSpec((None, tr, C), out_map)),
        out_shape=jax.ShapeDtypeStruct(full, out_dtype),
        compiler_params=_cp(("parallel",)),
    )(where, shard)


def _loss_grad(y, tgt, *, name):
    T, D = y.shape
    tr = _tile(T, 512, SUBLANES)
    row = pl.BlockSpec((tr, D), lambda i: (i, 0))

    def body(y_ref, t_ref, dy_ref, l_ref, acc_ref):
        i = pl.program_id(0)

        @pl.when(i == 0)
        def _():
            acc_ref[...] = jnp.zeros_like(acc_ref)

        err = y_ref[...] - t_ref[...]
        dy_ref[...] = err * (1.0 / D)
        acc_ref[...] += jnp.sum(err * err, axis=0, keepdims=True)

        @pl.when(i == T // tr - 1)
        def _():
            tot = jnp.sum(acc_ref[...], axis=1, keepdims=True) * (0.5 / D)
            l_ref[...] = jnp.broadcast_to(tot, (1, LANES))

    return pl.pallas_call(
        body, name=name, grid=(T // tr,),
        in_specs=[row, row],
        out_specs=[row, pl.BlockSpec((1, LANES), lambda i: (0, 0))],
        out_shape=[jax.ShapeDtypeStruct((T, D), F32), jax.ShapeDtypeStruct((1, LANES), F32)],
        scratch_shapes=[pltpu.VMEM((1, D), F32)],
        compiler_params=_cp(("arbitrary",)),
    )(y, tgt)


def _adamw(w, g, m, v, *, name):
    shape = w.shape
    w2, R, C, tr = _rows_view(w)
    spec = pl.BlockSpec((tr, C), lambda i: (i, 0))

    def body(w_ref, g_ref, m_ref, v_ref, d_ref, nm_ref, nv_ref):
        gv = g_ref[...]
        nm = ADAM_B1 * m_ref[...] + (1.0 - ADAM_B1) * gv
        nv = ADAM_B2 * v_ref[...] + (1.0 - ADAM_B2) * (gv * gv)
        m_hat = nm / (1.0 - ADAM_B1 ** ADAM_STEP)
        v_hat = nv / (1.0 - ADAM_B2 ** ADAM_STEP)
        d_ref[...] = -ADAM_LR * (m_hat / (jnp.sqrt(v_hat) + ADAM_EPS) + ADAM_WD * w_ref[...])
        nm_ref[...] = nm
        nv_ref[...] = nv

    shp = jax.ShapeDtypeStruct((R, C), F32)
    outs = pl.pallas_call(
        body, name=name, grid=(R // tr,),
        in_specs=[spec] * 4, out_specs=[spec] * 3, out_shape=[shp] * 3,
        compiler_params=_cp(("parallel",)),
    )(w2, g.reshape(R, C), m.reshape(R, C), v.reshape(R, C))
    return tuple(o.reshape(shape) for o in outs)


def _pack(arrs):
    flat = jnp.concatenate([a.reshape(-1).astype(F32) for a in arrs])
    unit = SUBLANES * LANES
    pad = (-flat.shape[0]) % unit
    return jnp.pad(flat, (0, pad)).reshape(-1, LANES)


def _unpack(buf, shapes):
    flat = buf.reshape(-1)
    out, off = [], 0
    for s in shapes:
        n = int(np.prod(s))
        out.append(flat[off:off + n].reshape(s))
        off += n
    return out


def _me():
    return lax.axis_index("x"), lax.axis_index("y"), lax.axis_index("c")


def _flip(pos, k):
    x, y, c = pos
    return (1 - x if k & 4 else x, 1 - y if k & 2 else y, 1 - c if k & 1 else c)


HBM = pl.BlockSpec(memory_space=pltpu.HBM)

COL, ROW = -1, -2


def _shard_window(ref, axis, j, n):
    start = pl.multiple_of(j * n, n)
    if axis == COL:
        return ref.at[:, :, pl.ds(start, n)]
    return ref.at[:, pl.ds(start, n), :]


GATHER_PEERS = (2, 4, 6)
SEM = pl.BlockSpec(memory_space=pltpu.SEMAPHORE)
EFFECT = pltpu.SideEffectType.DATAFLOW_SIDE_EFFECTING


def _gather_copy(buf, axis, i, pi, chip, me, send_sems, recv_sems):
    win = _shard_window(buf, axis, chip, buf.shape[axis] // N_CHIPS)
    k = len(GATHER_PEERS) * i + pi
    return pltpu.make_async_remote_copy(
        src_ref=win, dst_ref=win, send_sem=send_sems.at[k], recv_sem=recv_sems.at[k],
        device_id=_flip(me, GATHER_PEERS[pi]), device_id_type=MESH)


def _gather_start(fulls, axes, *, name):
    n = len(fulls)
    n_sem = len(GATHER_PEERS) * n

    def body(*refs):
        send_sems, recv_sems = refs[n], refs[n + 1]
        bufs = refs[n + 2:2 * n + 2]
        token = refs[2 * n + 2]
        me = _me()
        for i in range(n):
            for pi in range(len(GATHER_PEERS)):
                _gather_copy(bufs[i], axes[i], i, pi, 2 * me[0] + me[1], me, send_sems, recv_sems).start()
        token[...] = jnp.zeros_like(token)

    outs = pl.pallas_call(
        body, name=name,
        in_specs=[HBM] * n, out_specs=[SEM, SEM] + [HBM] * n + [pl.BlockSpec(memory_space=pltpu.VMEM)],
        out_shape=[pltpu.SemaphoreType.DMA((n_sem,)), pltpu.SemaphoreType.DMA((n_sem,))]
        + [pltpu.HBM(f.shape, f.dtype) for f in fulls] + [jax.ShapeDtypeStruct((SUBLANES, LANES), F32)],
        input_output_aliases={i: 2 + i for i in range(n)},
        compiler_params=pltpu.CompilerParams(has_side_effects=EFFECT),
    )(*[pltpu.with_memory_space_constraint(f, pltpu.HBM) for f in fulls])
    return outs[0], outs[1], list(outs[2:2 + n]), outs[-1]


def _gather_wait(send_sems, recv_sems, bufs, axes, idxs, after, *, name):
    m = len(bufs)

    def body(*refs):
        ss, rs = refs[m], refs[m + 1]
        outs = refs[m + 3:]
        me = _me()
        for t, i in enumerate(idxs):
            for pi, k in enumerate(GATHER_PEERS):
                px, py, _ = _flip(me, k)
                _gather_copy(outs[t], axes[t], i, pi, 2 * me[0] + me[1], me, ss, rs).wait_send()
                _gather_copy(outs[t], axes[t], i, pi, 2 * px + py, me, ss, rs).wait_recv()

    return pl.pallas_call(
        body, name=name,
        in_specs=[HBM] * m + [SEM, SEM, pl.BlockSpec(memory_space=pl.ANY)], out_specs=[HBM] * m,
        out_shape=[pltpu.HBM(b.shape, b.dtype) for b in bufs],
        input_output_aliases={t: t for t in range(m)},
        compiler_params=pltpu.CompilerParams(has_side_effects=EFFECT),
    )(*bufs, send_sems, recv_sems, after)


def _grad_part(ref, layer, axis, pos):
    px, py, pc = pos
    Rp, Cp = _part_geometry(ref.shape, axis)
    chip = 2 * px + py
    if axis == COL:
        return ref.at[layer, pl.ds(pl.multiple_of(pc * Rp, Rp), Rp), pl.ds(pl.multiple_of(chip * Cp, Cp), Cp)]
    return ref.at[layer, pl.ds(pl.multiple_of((2 * chip + pc) * Rp, Rp), Rp), :]


def _scatter_copy(grad, slots, layer, axis, i, k, src_pos, dst_slot, me, send_sems, recv_sems):
    sem = (N_DEV - 1) * i + k - 1
    return pltpu.make_async_remote_copy(
        src_ref=_grad_part(grad, layer, axis, src_pos), dst_ref=slots.at[dst_slot, layer],
        send_sem=send_sems.at[sem], recv_sem=recv_sems.at[sem],
        device_id=_flip(me, k), device_id_type=MESH)


def _scatter_start(grads, slots, layers, axes, *, name):
    n = len(grads)
    n_sem = (N_DEV - 1) * n

    def body(*refs):
        send_sems, recv_sems = refs[2 * n], refs[2 * n + 1]
        g_refs, s_refs = refs[2 * n + 2:3 * n + 2], refs[3 * n + 2:4 * n + 2]
        token = refs[4 * n + 2]
        me = _me()
        my_id = 4 * me[0] + 2 * me[1] + me[2]
        for i in range(n):
            for k in range(1, N_DEV):
                _scatter_copy(g_refs[i], s_refs[i], layers[i], axes[i], i, k, _flip(me, k), my_id, me,
                              send_sems, recv_sems).start()
        token[...] = jnp.zeros_like(token)

    arrays = list(grads) + list(slots)
    outs = pl.pallas_call(
        body, name=name,
        in_specs=[HBM] * (2 * n),
        out_specs=[SEM, SEM] + [HBM] * (2 * n) + [pl.BlockSpec(memory_space=pltpu.VMEM)],
        out_shape=[pltpu.SemaphoreType.DMA((n_sem,)), pltpu.SemaphoreType.DMA((n_sem,))]
        + [pltpu.HBM(a.shape, a.dtype) for a in arrays] + [jax.ShapeDtypeStruct((SUBLANES, LANES), F32)],
        input_output_aliases={i: 2 + i for i in range(2 * n)},
        compiler_params=pltpu.CompilerParams(has_side_effects=EFFECT),
    )(*[pltpu.with_memory_space_constraint(a, pltpu.HBM) for a in arrays])
    return outs[0], outs[1], list(outs[2:2 + n]), list(outs[2 + n:2 + 2 * n]), outs[-1]


def _scatter_wait(sems, grads, slots, axes, plan, *, name):
    n = len(grads)
    flat_sems = [s for pair in sems for s in pair]

    def body(*refs):
        sem_refs = refs[2 * n:2 * n + len(flat_sems)]
        g_refs, s_refs = refs[2 * n + len(flat_sems):3 * n + len(flat_sems)], refs[3 * n + len(flat_sems):]
        me = _me()
        for j, items in enumerate(plan):
            ss, rs = sem_refs[2 * j], sem_refs[2 * j + 1]
            for i, (a, layer) in enumerate(items):
                for k in range(1, N_DEV):
                    px, py, pc = _flip(me, k)
                    cp = _scatter_copy(g_refs[a], s_refs[a], layer, axes[a], i, k, me, 4 * px + 2 * py + pc, me, ss, rs)
                    cp.wait_send()
                    cp.wait_recv()

    arrays = list(grads) + list(slots)
    outs = pl.pallas_call(
        body, name=name,
        in_specs=[HBM] * (2 * n) + [SEM] * len(flat_sems), out_specs=[HBM] * (2 * n),
        out_shape=[pltpu.HBM(a.shape, a.dtype) for a in arrays],
        input_output_aliases={i: i for i in range(2 * n)},
        compiler_params=pltpu.CompilerParams(has_side_effects=EFFECT),
    )(*arrays, *flat_sems)
    return list(outs[:n]), list(outs[n:])


def _gather_small(small, *, name):
    def body(src, dst, send_sems, recv_sems):
        me = _me()
        my_id = 4 * me[0] + 2 * me[1] + me[2]

        def copy(k, slot):
            return pltpu.make_async_remote_copy(
                src_ref=src, dst_ref=dst.at[slot], send_sem=send_sems.at[k - 1], recv_sem=recv_sems.at[k - 1],
                device_id=_flip(me, k), device_id_type=MESH)

        sends = [copy(k, my_id) for k in range(1, N_DEV)]
        for cp in sends:
            cp.start()
        for k in range(1, N_DEV):
            px, py, pc = _flip(me, k)
            copy(k, 4 * px + 2 * py + pc).wait_recv()
        for cp in sends:
            cp.wait_send()

    return pl.pallas_call(
        body, name=name, in_specs=[HBM], out_specs=HBM,
        out_shape=jax.ShapeDtypeStruct((N_DEV,) + small.shape, small.dtype),
        scratch_shapes=[pltpu.SemaphoreType.DMA((N_DEV - 1,)), pltpu.SemaphoreType.DMA((N_DEV - 1,))],
    )(small)


def _pair_gather(halves, *, name):
    n = len(halves)

    def body(*refs):
        bufs = refs[n:2 * n]
        send_sems, recv_sems = refs[2 * n:]
        me = _me()
        c = me[2]
        sib = _flip(me, 1)

        def copy(i, half):
            return pltpu.make_async_remote_copy(
                src_ref=bufs[i].at[:, half], dst_ref=bufs[i].at[:, half], send_sem=send_sems.at[i],
                recv_sem=recv_sems.at[i], device_id=sib, device_id_type=MESH)

        sends = [copy(i, c) for i in range(n)]
        for cp in sends:
            cp.start()
        for i in range(n):
            copy(i, 1 - c).wait_recv()
        for cp in sends:
            cp.wait_send()

    return pl.pallas_call(
        body, name=name,
        in_specs=[HBM] * n, out_specs=[HBM] * n,
        out_shape=[jax.ShapeDtypeStruct(h.shape, h.dtype) for h in halves],
        input_output_aliases={i: i for i in range(n)},
        scratch_shapes=[pltpu.SemaphoreType.DMA((n,)), pltpu.SemaphoreType.DMA((n,))],
    )(*halves)


def _local_step(x, tgt, P, fetch, ship, n_layers):
    T, D = x.shape
    P = dict(P)
    W = {}
    depth = P["ffn_conv_b"].shape[0]
    n_a = P["a_w_s"].shape[0]
    alpha = (2 * depth) ** 0.25
    H = P["rel_table"].shape[1] // N_GROUPS
    HD = H * HEAD_DIM
    dils = [d for _, d in DILATED_GROUPS]
    assert T % (dils[-1] * BLK) == 0

    def arrive(stage, after):
        w_new, p_new = fetch(stage, after)
        W.update(w_new)
        P.update(p_new)

    bias = _band_bias(P["rel_table"], H, name="band_bias")
    saved = []
    xf, xb = x, x.astype(BF16)
    kg = vg = None
    for i in range(depth):
        s = {"x": xf, "xb": xb}
        arrive(f"{i}a", xf)
        if i < n_a:
            s["zp"] = _mm(xb, W["a_w_in", i], name=f"a{i}_in")
            arrive(f"{i}b", s["zp"])
            s["y"] = _sgu_fwd(s["zp"], P["a_w_s"][i], P["a_b_s"][i], P["a_ln_g"][i], P["a_ln_b"][i], name=f"a{i}_sgu")
            s["h"] = _mm(s["y"], W["a_w_out", i], name=f"a{i}_out")
        else:
            j = i - n_a
            if j == 0:
                kv = _mm(xb, W["kv_w", 0], out_dtype=BF16, name="kv_proj")
                kg = [(kv, 0) if d == 1 else (_to_group(kv[:, :HD], d), 0) for d in dils]
                vg = [(kv, 1) if d == 1 else (_to_group(kv[:, HD:], d), 0) for d in dils]
            q = _mm(xb, W["b_w_q", j], out_dtype=BF16, name=f"b{j}_q")
            s["qg"] = [(q, g) if d == 1 else (_to_group(q[:, g * HD:(g + 1) * HD], d), 0) for g, d in enumerate(dils)]
            os, ls = [], []
            for g, d in enumerate(dils):
                o_g, l_g = _attn_fwd(s["qg"][g], kg[g], vg[g], bias, g, d, name=f"b{j}_attn{g}")
                os.append(_from_group(o_g, d))
                ls.append(_from_group(l_g, d))
            s["oc"], s["ob"], s["lse"] = _attn_combine(os, ls, name=f"b{j}_comb")
            s["h"] = _mm(s["ob"], W["b_w_o", j], name=f"b{j}_o")
        s["x1"], s["x1b"] = _add_ln(xf, s["h"], P["ln_g"][i, 0], P["ln_b"][i, 0], alpha, name=f"l{i}_ln1")
        s["hup"] = _mm(s["x1b"], W["ffn_w_up", i], name=f"l{i}_up")
        s["act"] = _convglu_fwd(s["hup"], P["ffn_conv_w"][i], P["ffn_conv_b"][i], name=f"l{i}_glu")
        s["f"] = _mm(s["act"], W["ffn_w_down", i], name=f"l{i}_down")
        xf, xb = _add_ln(s["x1"], s["f"], P["ln_g"][i, 1], P["ln_b"][i, 1], alpha, name=f"l{i}_ln2")
        saved.append(s)

    G, loss_row = _loss_grad(xf, tgt, name="loss")

    gw = {}

    def dw(key, layer, a, b, name, **kw):
        gw[key] = _mm(a, b, ta=True, out_dtype=BF16, out_into=(gw.get(key), n_layers[key], layer), name=name, **kw)

    gp = {k: [None] * n_a for k in ("a_ln_g", "a_ln_b", "a_w_s", "a_b_s")}
    gp.update({k: [None] * depth for k in ("ffn_conv_w", "ffn_conv_b", "ln_g", "ln_b")})
    dk_parts, dv_parts, dbias_parts = [], [], []
    for i in reversed(range(depth)):
        s = saved[i]
        dr2, dr2b, dg2, db2 = _ln_bwd(G, s["x1"], s["f"], P["ln_g"][i, 1], alpha, name=f"l{i}_ln2_bwd")
        dw("ffn_w_down", i, s["act"], dr2b, f"l{i}_down_dw")
        dact = _mm(dr2b, W["ffn_w_down", i], tb=True, name=f"l{i}_down_dx")
        dhup, dcw, dcb = _convglu_bwd(s["hup"], dact, P["ffn_conv_w"][i], P["ffn_conv_b"][i], name=f"l{i}_glu_bwd")
        gp["ffn_conv_w"][i] = dcw.transpose(1, 0, 2).reshape(dcw.shape[1], -1)
        gp["ffn_conv_b"][i] = dcb.reshape(-1)
        dw("ffn_w_up", i, s["x1b"], dhup, f"l{i}_up_dw", b_halves=True)
        token = ship(gw, [("ffn_w_down", i), ("ffn_w_up", i)])
        G1 = _mm(dhup, W["ffn_w_up", i], tb=True, a_halves=True, add=dr2, add_scale=alpha, after=token,
                 name=f"l{i}_up_dx")
        dr1, dr1b, dg1, db1 = _ln_bwd(G1, s["x"], s["h"], P["ln_g"][i, 0], alpha, name=f"l{i}_ln1_bwd")
        gp["ln_g"][i] = jnp.concatenate([dg1, dg2], axis=0)
        gp["ln_b"][i] = jnp.concatenate([db1, db2], axis=0)
        if i < n_a:
            dw("a_w_out", i, s["y"], dr1b, f"a{i}_out_dw")
            dy = _mm(dr1b, W["a_w_out", i], tb=True, name=f"a{i}_out_dx")
            dzp, dws, dbs, dlg, dlb = _sgu_bwd(s["zp"], dy, P["a_w_s"][i], P["a_b_s"][i], P["a_ln_g"][i],
                                               P["a_ln_b"][i], name=f"a{i}_sgu_bwd")
            gp["a_w_s"][i], gp["a_b_s"][i] = dws, dbs[:, :dws.shape[0]].T
            gp["a_ln_g"][i], gp["a_ln_b"][i] = dlg[0], dlb[0]
            dw("a_w_in", i, s["xb"], dzp, f"a{i}_in_dw")
            token = ship(gw, [("a_w_out", i), ("a_w_in", i)])
            G = _mm(dzp, W["a_w_in", i], tb=True, add=dr1, add_scale=alpha, after=token, name=f"a{i}_in_dx")
        else:
            j = i - n_a
            dw("b_w_o", j, s["ob"], dr1b, f"b{j}_o_dw")
            do_tok = _mm(dr1b, W["b_w_o", j], tb=True, name=f"b{j}_o_dx")
            delta, dob = _attn_bwd_prep(do_tok, s["oc"], name=f"b{j}_prep")
            dqs, dbs = [], []
            for g, d in enumerate(dils):
                dq, dk, dv, db = _attn_bwd(s["qg"][g], kg[g], vg[g], bias, g, d, _to_group(dob, d),
                                           _to_group(s["lse"], d), _to_group(delta, d), name=f"b{j}_attn{g}_bwd")
                dqs.append(_from_group(dq, d))
                dk_parts.append(_from_group(dk, d))
                dv_parts.append(_from_group(dv, d))
                dbs.append(db)
            dbias_parts.append(jnp.stack(dbs))
            dq_tok = jnp.concatenate(dqs, axis=1)
            dw("b_w_q", j, s["xb"], dq_tok, f"b{j}_q_dw")
            token = ship(gw, [("b_w_o", j), ("b_w_q", j)])
            G = _mm(dq_tok, W["b_w_q", j], tb=True, add=dr1, add_scale=alpha, after=token, name=f"b{j}_q_dx")
            if j == 0:
                dkv = jnp.concatenate([_addn(dk_parts, BF16, name="dk_sum"), _addn(dv_parts, BF16, name="dv_sum")], axis=1)
                dw("kv_w", 0, s["xb"], dkv, "kv_dw")
                token = ship(gw, [("kv_w", 0)])
                G = _mm(dkv, W["kv_w", 0], tb=True, add=G, add_scale=1.0, after=token, name="kv_dx")
    rel = _rel_grad(dbias_parts, name="rel_grad")
    grel = rel[:, :, :H].transpose(1, 0, 2).reshape(REL_BUCKETS, N_GROUPS * H)
    gsmall = {k: jnp.stack(v) for k, v in gp.items()}
    gsmall["rel_table"] = grel
    return loss_row, G, gw, gsmall


BIG = (("a_w_in", COL), ("a_w_out", ROW), ("kv_w", ROW), ("b_w_q", COL), ("b_w_o", COL),
       ("ffn_w_up", COL), ("ffn_w_down", ROW))
SMALL_SHARDED = ("a_ln_g", "a_ln_b", "ffn_conv_w", "ln_g", "ln_b")
SMALL_REPLICATED = ("a_w_s", "a_b_s", "rel_table", "ffn_conv_b")
WEIGHTS = ("a_w_in", "a_ln_g", "a_ln_b", "a_w_s", "a_b_s", "a_w_out", "kv_w", "b_w_q", "b_w_o", "rel_table",
           "ffn_w_up", "ffn_conv_w", "ffn_conv_b", "ffn_w_down", "ln_g", "ln_b")


def kernel(x, a_w_in, a_ln_g, a_ln_b, a_w_s, a_b_s, a_w_out, kv_w, b_w_q, b_w_o, rel_table, ffn_w_up, ffn_conv_w, ffn_conv_b, ffn_w_down, ln_g, ln_b, loss_target, m_a_w_in, m_a_ln_g, m_a_ln_b, m_a_w_s, m_a_b_s, m_a_w_out, m_kv_w, m_b_w_q, m_b_w_o, m_rel_table, m_ffn_w_up, m_ffn_conv_w, m_ffn_conv_b, m_ffn_w_down, m_ln_g, m_ln_b, v_a_w_in, v_a_ln_g, v_a_ln_b, v_a_w_s, v_a_b_s, v_a_w_out, v_kv_w, v_b_w_q, v_b_w_o, v_rel_table, v_ffn_w_up, v_ffn_conv_w, v_ffn_conv_b, v_ffn_w_down, v_ln_g, v_ln_b):
    w = dict(a_w_in=a_w_in, a_ln_g=a_ln_g, a_ln_b=a_ln_b, a_w_s=a_w_s, a_b_s=a_b_s, a_w_out=a_w_out, kv_w=kv_w,
             b_w_q=b_w_q, b_w_o=b_w_o, rel_table=rel_table, ffn_w_up=ffn_w_up, ffn_conv_w=ffn_conv_w,
             ffn_conv_b=ffn_conv_b, ffn_w_down=ffn_w_down, ln_g=ln_g, ln_b=ln_b)
    m = dict(a_w_in=m_a_w_in, a_ln_g=m_a_ln_g, a_ln_b=m_a_ln_b, a_w_s=m_a_w_s, a_b_s=m_a_b_s, a_w_out=m_a_w_out,
             kv_w=m_kv_w, b_w_q=m_b_w_q, b_w_o=m_b_w_o, rel_table=m_rel_table, ffn_w_up=m_ffn_w_up,
             ffn_conv_w=m_ffn_conv_w, ffn_conv_b=m_ffn_conv_b, ffn_w_down=m_ffn_w_down, ln_g=m_ln_g, ln_b=m_ln_b)
    v = dict(a_w_in=v_a_w_in, a_ln_g=v_a_ln_g, a_ln_b=v_a_ln_b, a_w_s=v_a_w_s, a_b_s=v_a_b_s, a_w_out=v_a_w_out,
             kv_w=v_kv_w, b_w_q=v_b_w_q, b_w_o=v_b_w_o, rel_table=v_rel_table, ffn_w_up=v_ffn_w_up,
             ffn_conv_w=v_ffn_conv_w, ffn_conv_b=v_ffn_conv_b, ffn_w_down=v_ffn_w_down, ln_g=v_ln_g, ln_b=v_ln_b)
    chip = 2 * lax.axis_index("x") + lax.axis_index("y")

    big_names = [n for n, _ in BIG]
    big_axes = [a for _, a in BIG]
    where = jnp.stack([chip, lax.axis_index("c"), 2 * chip + lax.axis_index("c")]).astype(jnp.int32)
    shards = [w[n].reshape((1,) + w[n].shape) if w[n].ndim == 2 else w[n] for n in big_names]
    small_shapes = [w[n].shape for n in SMALL_SHARDED]
    small = _pack([w[n] for n in SMALL_SHARDED])
    shard_of = dict(zip(big_names, shards))
    axis_of = dict(BIG)
    n_layers = {n: s.shape[0] for n, s in shard_of.items()}
    n_a = w["a_w_s"].shape[0]
    stages = {"0a": [("a_w_in", 0), ("small", 0)], "0b": [("a_w_out", 0), ("ffn_w_up", 0), ("ffn_w_down", 0)]}
    for i in range(1, w["ffn_w_up"].shape[0]):
        if i < n_a:
            mixer = [("a_w_in", i), ("a_w_out", i)]
        else:
            mixer = ([("kv_w", 0)] if i == n_a else []) + [("b_w_q", i - n_a), ("b_w_o", i - n_a)]
        stages[f"{i}a"] = mixer + [("ffn_w_up", i), ("ffn_w_down", i)]
    assert sum(len(st) for st in stages.values()) == sum(n_layers.values()) + 1
    flights, full_shape = {}, {}
    for batch, names in enumerate((["0a", "0b"], [st for st in stages if st not in ("0a", "0b")])):
        order = [key for st in names for key in stages[st]]
        placed, axes = [], []
        for n, l in order:
            if n == "small":
                placed.append(_place_shard(small[None], 0, ROW, where, F32, name="place_small"))
                axes.append(ROW)
            else:
                placed.append(_place_shard(shard_of[n], l, axis_of[n], where, BF16, name=f"place_{n}{l}"))
                axes.append(axis_of[n])
                full_shape[n] = (n_layers[n],) + placed[-1].shape[1:]
        send_sems, recv_sems, flying, all_started = _gather_start(placed, axes, name=f"gather_start{batch}")
        for st in names:
            flights[st] = (send_sems, recv_sems, flying, axes, order)

    def fetch(stage, after):
        if stage not in stages:
            return {}, {}
        send_sems, recv_sems, flying, axes, order = flights[stage]
        if stage == "0a":
            after = all_started
        idxs = [order.index(key) for key in stages[stage]]
        landed = _gather_wait(send_sems, recv_sems, [flying[i] for i in idxs], [axes[i] for i in idxs], idxs, after,
                              name=f"gather_wait_{stage}")
        W_new, P_new = {}, {}
        for key, arr in zip(stages[stage], landed):
            if key[0] == "small":
                small_all = arr.reshape((N_CHIPS,) + small.shape)
                per_chip = [_unpack(small_all[j], small_shapes) for j in range(N_CHIPS)]
                for i, n in enumerate(SMALL_SHARDED):
                    P_new[n] = jnp.concatenate([per_chip[j][i] for j in range(N_CHIPS)], axis=-1)
            else:
                W_new[key] = arr.reshape(arr.shape[1:])
        return W_new, P_new

    slots ={n: lax.empty((N_DEV, n_layers[n]) + _part_geometry(full_shape[n], axis_of[n]), BF16) for n in big_names}
    sems, plan = [], []

    def ship(gw, items):
        names = [n for n, _ in items]
        send, recv, grads, landing, token = _scatter_start(
            [gw[n] for n in names], [slots[n] for n in names], [l for _, l in items], [axis_of[n] for n in names],
            name="scatter_start_" + "_".join(f"{n}{l}" for n, l in items))
        for n, g, s in zip(names, grads, landing):
            gw[n], slots[n] = g, s
        sems.append((send, recv))
        plan.append([(big_names.index(n), l) for n, l in items])
        return token

    loss_row, grad_x, gw, gsmall = _local_step(x[0], loss_target[0], {n: w[n] for n in SMALL_REPLICATED}, fetch,
                                               ship, n_layers)
    loss = lax.psum(loss_row[0, 0], ("x", "y", "c"))

    small_names = list(SMALL_SHARDED) + list(SMALL_REPLICATED)
    small_pack = _pack([gsmall[n] for n in small_names])
    own, landed = _scatter_wait(sems, [gw[n] for n in big_names], [slots[n] for n in big_names], big_axes, plan,
                                name="scatter_wait")
    sums = [_sum_parts(b, g, ax, where, name=f"sum_{n}") for (n, ax), b, g in zip(BIG, landed, own)]
    pairs = _pair_gather(sums, name="pair_gather")
    small_sum = _sum_small(_gather_small(small_pack, name="gather_small"), small_pack, where, name="sum_small")
    grad = {n: p.reshape(w[n].shape) for n, p in zip(big_names, pairs)}
    full_small = _unpack(small_sum, [gsmall[n].shape for n in small_names])
    for n, gfull in zip(small_names, full_small):
        if n in SMALL_SHARDED:
            width = w[n].shape[-1]
            grad[n] = lax.dynamic_slice_in_dim(gfull, chip * width, width, axis=-1)
        else:
            grad[n] = gfull

    delta, new_m, new_v = {}, {}, {}
    for n, _ in BIG:
        delta[n], new_m[n], new_v[n] = _adamw(w[n], grad[n], m[n], v[n], name=f"adamw_{n}")
    shapes = [w[n].shape for n in small_names]
    packed = _adamw(_pack([w[n] for n in small_names]), _pack([grad[n] for n in small_names]),
                    _pack([m[n] for n in small_names]), _pack([v[n] for n in small_names]), name="adamw_small")
    for out, res in zip((delta, new_m, new_v), packed):
        for n, a in zip(small_names, _unpack(res, shapes)):
            out[n] = a

    return (loss, grad_x[None], *[grad[n] for n in WEIGHTS], *[delta[n] for n in WEIGHTS],
            *[new_m[n] for n in WEIGHTS], *[new_v[n] for n in WEIGHTS])
```

```python
import functools
import math

import numpy as np
import jax
import jax.numpy as jnp
from jax import lax
from jax.experimental import pallas as pl
from jax.experimental.pallas import tpu as pltpu

F32 = jnp.float32
BF16 = jnp.bfloat16
MESH = pl.DeviceIdType.MESH

CHUNK = 128
HEAD_DIM = 64
DILATED_GROUPS = ((128, 1), (512, 4), (2048, 16))
N_GROUPS = len(DILATED_GROUPS)
BLK = 128
REL_BUCKETS = 32
REL_MAX_DIST = 2048
LN_EPS = 1e-5
NEG = -1e30
ADAM_LR = 0.001
ADAM_B1 = 0.9
ADAM_B2 = 0.999
ADAM_EPS = 1e-08
ADAM_WD = 0.01
ADAM_STEP = 10

N_CHIPS = 4
N_DEV = 8
LANES = 128
SUBLANES = 8
VMEM_LIMIT = 48 * 1024 * 1024

_SQRT_HALF = 0.7071067811865476
_INV_SQRT_2PI = 0.3989422804014327


def _cp(sem=None, vmem=VMEM_LIMIT):
    return pltpu.CompilerParams(dimension_semantics=sem, vmem_limit_bytes=vmem)


def _tile(dim, target, align=LANES):
    if dim <= target:
        return dim
    t = (target // align) * align
    while t >= align:
        if dim % t == 0:
            return t
        t -= align
    return dim


def _gelu(x):
    return 0.5 * x * (1.0 + lax.erf(x * _SQRT_HALF))


def _gelu_grad(x):
    return 0.5 * (1.0 + lax.erf(x * _SQRT_HALF)) + x * (_INV_SQRT_2PI * jnp.exp(-0.5 * x * x))


def _dot(a, b, ca, cb):
    return lax.dot_general(a, b, (((ca,), (cb,)), ((), ())), preferred_element_type=F32)


def _mn_tile(dim):
    return max(_tile(dim, 1024), _tile(dim, 1408))


def _mm(a, b, *, name, ta=False, tb=False, out_dtype=F32, tiles=None,
        a_lead=(), b_lead=(), a_halves=False, b_halves=False, add=None, add_scale=1.0, out_into=None, after=None):
    a2, b2 = a.shape[-2:], b.shape[-2:]
    M, K = (a2[1], a2[0]) if ta else a2
    Kb, N = (b2[1], b2[0]) if tb else b2
    if a_halves:
        K = 2 * K
    if b_halves:
        N = 2 * N
    assert K == Kb, (name, a.shape, b.shape)
    if tiles is None:
        k_target = 1408 if a_halves else (K if K <= 2816 else 2048)
        tm, tn = _mn_tile(M), _mn_tile(N // 2 if b_halves else N)
        tk = _tile(K // 2 if a_halves else K, k_target)
    else:
        tm = _tile(M, tiles[0])
        tn = _tile(N // 2 if b_halves else N, tiles[1])
        tk = _tile(K // 2 if a_halves else K, tiles[2])
    nm, nn, nk = M // tm, N // tn, K // tk
    nkh, nnh = nk // 2, nn // 2

    def lead(idx, rest):
        return tuple(idx) + tuple(rest)

    sq_a = (None,) * (a.ndim - 2)
    sq_b = (None,) * (b.ndim - 2)
    if a_halves:
        assert not ta and a.ndim == 3
        a_spec = pl.BlockSpec((None, tm, tk), lambda i, j, k: (k // nkh, i, k % nkh))
    elif ta:
        a_spec = pl.BlockSpec(sq_a + (tk, tm), lambda i, j, k: lead(a_lead, (k, i)))
    else:
        a_spec = pl.BlockSpec(sq_a + (tm, tk), lambda i, j, k: lead(a_lead, (i, k)))
    if b_halves:
        assert not tb and b.ndim == 3
        b_spec = pl.BlockSpec((None, tk, tn), lambda i, j, k: (j // nnh, k, j % nnh))
    elif tb:
        b_spec = pl.BlockSpec(sq_b + (tn, tk), lambda i, j, k: lead(b_lead, (j, k)))
    else:
        b_spec = pl.BlockSpec(sq_b + (tk, tn), lambda i, j, k: lead(b_lead, (k, j)))
    mn_spec = pl.BlockSpec((tm, tn), lambda i, j, k: (i, j))
    in_specs = [a_spec, b_spec]
    args = [a, b]
    if add is not None:
        in_specs.append(mn_spec)
        args.append(add)
    aliases = {}
    if out_into is None:
        o_spec, out_shape = mn_spec, jax.ShapeDtypeStruct((M, N), out_dtype)
    else:
        buf, n_layers, layer = out_into
        o_spec = pl.BlockSpec((None, tm, tn), lambda i, j, k: (layer, i, j))
        out_shape = jax.ShapeDtypeStruct((n_layers, M, N), out_dtype)
        if buf is not None:
            aliases = {len(args): 0}
            in_specs.append(pl.BlockSpec(memory_space=pl.ANY))
            args.append(buf)
    if after is not None:
        in_specs.append(pl.BlockSpec(memory_space=pl.ANY))
        args.append(after)
    n_in = len(args)
    ca, cb = (0 if ta else 1), (1 if tb else 0)

    def body(*refs):
        a_ref, b_ref = refs[:2]
        add_ref = refs[2] if add is not None else None
        o_ref = refs[n_in]

        def finish(acc):
            if add_ref is not None:
                acc = acc + add_scale * add_ref[...]
            o_ref[...] = acc.astype(o_ref.dtype)

        if nk == 1:
            finish(_dot(a_ref[...], b_ref[...], ca, cb))
        else:
            acc_ref = refs[-1]
            k = pl.program_id(2)

            @pl.when(k == 0)
            def _():
                acc_ref[...] = jnp.zeros_like(acc_ref)

            acc_ref[...] += _dot(a_ref[...], b_ref[...], ca, cb)

            @pl.when(k == nk - 1)
            def _():
                finish(acc_ref[...])

    return pl.pallas_call(
        body, name=name, grid=(nm, nn, nk),
        in_specs=in_specs, out_specs=o_spec, out_shape=out_shape,
        input_output_aliases=aliases,
        scratch_shapes=[] if nk == 1 else [pltpu.VMEM((tm, tn), F32)],
        compiler_params=_cp(("parallel", "parallel", "arbitrary")),
    )(*args)


def _ln_stats(r):
    mu = jnp.mean(r, axis=-1, keepdims=True)
    xc = r - mu
    var = jnp.mean(xc * xc, axis=-1, keepdims=True)
    rstd = lax.rsqrt(var + LN_EPS)
    return xc * rstd, rstd


def _add_ln(x, h, g, b, alpha, *, name):
    T, D = x.shape
    tr = _tile(T, 512, SUBLANES)
    row = pl.BlockSpec((tr, D), lambda i: (i, 0))
    vec = pl.BlockSpec((1, D), lambda i: (0, 0))

    def body(x_ref, h_ref, g_ref, b_ref, o_ref, ob_ref):
        xhat, _ = _ln_stats(alpha * x_ref[...] + h_ref[...])
        y = xhat * g_ref[...] + b_ref[...]
        o_ref[...] = y
        ob_ref[...] = y.astype(BF16)

    return pl.pallas_call(
        body, name=name, grid=(T // tr,),
        in_specs=[row, row, vec, vec], out_specs=[row, row],
        out_shape=[jax.ShapeDtypeStruct((T, D), F32), jax.ShapeDtypeStruct((T, D), BF16)],
        compiler_params=_cp(("parallel",)),
    )(x, h, g.reshape(1, D), b.reshape(1, D))


def _ln_bwd(dy, x, h, g, alpha, *, name):
    T, D = x.shape
    tr = _tile(T, 512, SUBLANES)
    row = pl.BlockSpec((tr, D), lambda i: (i, 0))
    vec = pl.BlockSpec((1, D), lambda i: (0, 0))

    def body(dy_ref, x_ref, h_ref, g_ref, dr_ref, drb_ref, dg_ref, db_ref):
        @pl.when(pl.program_id(0) == 0)
        def _():
            dg_ref[...] = jnp.zeros_like(dg_ref)
            db_ref[...] = jnp.zeros_like(db_ref)

        xhat, rstd = _ln_stats(alpha * x_ref[...] + h_ref[...])
        dyv = dy_ref[...]
        dxhat = dyv * g_ref[...]
        c1 = jnp.mean(dxhat, axis=-1, keepdims=True)
        c2 = jnp.mean(dxhat * xhat, axis=-1, keepdims=True)
        dr = rstd * (dxhat - c1 - xhat * c2)
        dr_ref[...] = dr
        drb_ref[...] = dr.astype(BF16)
        dg_ref[...] += jnp.sum(dyv * xhat, axis=0, keepdims=True)
        db_ref[...] += jnp.sum(dyv, axis=0, keepdims=True)

    return pl.pallas_call(
        body, name=name, grid=(T // tr,),
        in_specs=[row, row, row, vec], out_specs=[row, row, vec, vec],
        out_shape=[jax.ShapeDtypeStruct((T, D), F32), jax.ShapeDtypeStruct((T, D), BF16),
                   jax.ShapeDtypeStruct((1, D), F32), jax.ShapeDtypeStruct((1, D), F32)],
        compiler_params=_cp(("arbitrary",)),
    )(dy, x, h, g.reshape(1, D))


def _sgu_pieces(zp, lg, lb, E):
    z = _gelu(zp)
    u, v = z[:, :E], z[:, E:]
    vhat, rstd = _ln_stats(v)
    vn = vhat * lg + lb
    return u, vhat, rstd, vn


def _tril_mask():
    t = lax.broadcasted_iota(jnp.int32, (CHUNK, CHUNK), 0)
    s = lax.broadcasted_iota(jnp.int32, (CHUNK, CHUNK), 1)
    return s <= t


def _sgu_fwd(zp, ws, bs, lg, lb, *, name):
    T, E2 = zp.shape
    E = E2 // 2
    G = ws.shape[0]
    cw = E // G

    def body(zp_ref, ws_ref, bs_ref, lg_ref, lb_ref, y_ref):
        u, _, _, vn = _sgu_pieces(zp_ref[...], lg_ref[...], lb_ref[...], E)
        vnb = vn.astype(BF16)
        tril = _tril_mask()
        for g in range(G):
            cols = slice(g * cw, (g + 1) * cw)
            w = jnp.where(tril, ws_ref[g], 0.0).astype(BF16)
            sv = _dot(w, vnb[:, cols], 1, 0) + bs_ref[g]
            y_ref[:, cols] = (u[:, cols] * sv).astype(BF16)

    return pl.pallas_call(
        body, name=name, grid=(T // CHUNK,),
        in_specs=[pl.BlockSpec((CHUNK, E2), lambda n: (n, 0)),
                  pl.BlockSpec((G, CHUNK, CHUNK), lambda n: (0, 0, 0)),
                  pl.BlockSpec((G, CHUNK, 1), lambda n: (0, 0, 0)),
                  pl.BlockSpec((1, E), lambda n: (0, 0)),
                  pl.BlockSpec((1, E), lambda n: (0, 0))],
        out_specs=pl.BlockSpec((CHUNK, E), lambda n: (n, 0)),
        out_shape=jax.ShapeDtypeStruct((T, E), BF16),
        compiler_params=_cp(("parallel",)),
    )(zp, ws, bs.reshape(G, CHUNK, 1), lg.reshape(1, E), lb.reshape(1, E))


def _sgu_bwd(zp, dy, ws, bs, lg, lb, *, name):
    T, E2 = zp.shape
    E = E2 // 2
    G = ws.shape[0]
    cw = E // G
    assert G <= LANES

    def body(zp_ref, dy_ref, ws_ref, bs_ref, lg_ref, lb_ref,
             dzp_ref, dws_ref, dbs_ref, dlg_ref, dlb_ref, dvn_ref):
        @pl.when(pl.program_id(0) == 0)
        def _():
            dws_ref[...] = jnp.zeros_like(dws_ref)
            dbs_ref[...] = jnp.zeros_like(dbs_ref)
            dlg_ref[...] = jnp.zeros_like(dlg_ref)
            dlb_ref[...] = jnp.zeros_like(dlb_ref)

        zpv = zp_ref[...]
        lgv = lg_ref[...]
        u, vhat, rstd, vn = _sgu_pieces(zpv, lgv, lb_ref[...], E)
        gp = _gelu_grad(zpv)
        vnb = vn.astype(BF16)
        tril = _tril_mask()
        lane = lax.broadcasted_iota(jnp.int32, (CHUNK, LANES), 1)
        dbs = dbs_ref[...]
        for g in range(G):
            cols = slice(g * cw, (g + 1) * cw)
            w = jnp.where(tril, ws_ref[g], 0.0).astype(BF16)
            sv = _dot(w, vnb[:, cols], 1, 0) + bs_ref[g]
            dyg = dy_ref[:, cols]
            dzp_ref[:, cols] = (dyg * sv * gp[:, cols]).astype(BF16)
            dsv = dyg * u[:, cols]
            dsvb = dsv.astype(BF16)
            dws_ref[g] += jnp.where(tril, _dot(dsvb, vnb[:, cols], 1, 1), 0.0)
            dvn_ref[:, cols] = _dot(w, dsvb, 0, 0)
            dbs = dbs + jnp.where(lane == g, jnp.sum(dsv, axis=1, keepdims=True), 0.0)
        dbs_ref[...] = dbs
        dvn = dvn_ref[...]
        dlg_ref[...] += jnp.sum(dvn * vhat, axis=0, keepdims=True)
        dlb_ref[...] += jnp.sum(dvn, axis=0, keepdims=True)
        dvhat = dvn * lgv
        c1 = jnp.mean(dvhat, axis=-1, keepdims=True)
        c2 = jnp.mean(dvhat * vhat, axis=-1, keepdims=True)
        dv = rstd * (dvhat - c1 - vhat * c2)
        dzp_ref[:, E:] = (dv * gp[:, E:]).astype(BF16)

    vecE = pl.BlockSpec((1, E), lambda n: (0, 0))
    return pl.pallas_call(
        body, name=name, grid=(T // CHUNK,),
        in_specs=[pl.BlockSpec((CHUNK, E2), lambda n: (n, 0)),
                  pl.BlockSpec((CHUNK, E), lambda n: (n, 0)),
                  pl.BlockSpec((G, CHUNK, CHUNK), lambda n: (0, 0, 0)),
                  pl.BlockSpec((G, CHUNK, 1), lambda n: (0, 0, 0)),
                  vecE, vecE],
        out_specs=[pl.BlockSpec((CHUNK, E2), lambda n: (n, 0)),
                   pl.BlockSpec((G, CHUNK, CHUNK), lambda n: (0, 0, 0)),
                   pl.BlockSpec((CHUNK, LANES), lambda n: (0, 0)),
                   vecE, vecE],
        out_shape=[jax.ShapeDtypeStruct((T, E2), BF16),
                   jax.ShapeDtypeStruct((G, CHUNK, CHUNK), F32),
                   jax.ShapeDtypeStruct((CHUNK, LANES), F32),
                   jax.ShapeDtypeStruct((1, E), F32), jax.ShapeDtypeStruct((1, E), F32)],
        scratch_shapes=[pltpu.VMEM((CHUNK, E), F32)],
        compiler_params=_cp(("arbitrary",)),
    )(zp, dy, ws, bs.reshape(G, CHUNK, 1), lg.reshape(1, E), lb.reshape(1, E))


def _delay(h, k):
    t = lax.broadcasted_iota(jnp.int32, h.shape, 0)
    return jnp.where(t >= k, pltpu.roll(h, k, 0), 0.0)


def _advance(d, k):
    T = d.shape[0]
    t = lax.broadcasted_iota(jnp.int32, d.shape, 0)
    return jnp.where(t < T - k, pltpu.roll(d, T - k, 0), 0.0)


def _conv3(h, w_ref, b_ref):
    return w_ref[2:3, :] * h + w_ref[1:2, :] * _delay(h, 1) + w_ref[0:1, :] * _delay(h, 2) + b_ref[...]


def _convglu_fwd(hup, cw, cb, *, name):
    T, F2 = hup.shape
    F = F2 // 2
    tc = LANES
    nt = F // tc

    def body(ha_ref, hg_ref, wa_ref, wg_ref, ba_ref, bg_ref, o_ref):
        a = _conv3(ha_ref[...], wa_ref, ba_ref)
        g = _conv3(hg_ref[...], wg_ref, bg_ref)
        o_ref[...] = (_gelu(a) * g).astype(BF16)

    col = lambda off: pl.BlockSpec((T, tc), lambda j: (0, j + off))
    w3 = lambda off: pl.BlockSpec((3, tc), lambda j: (0, j + off))
    b1 = lambda off: pl.BlockSpec((1, tc), lambda j: (0, j + off))
    return pl.pallas_call(
        body, name=name, grid=(nt,),
        in_specs=[col(0), col(nt), w3(0), w3(nt), b1(0), b1(nt)],
        out_specs=pl.BlockSpec((T, tc), lambda j: (0, j)),
        out_shape=jax.ShapeDtypeStruct((T, F), BF16),
        compiler_params=_cp(("parallel",)),
    )(hup, hup, cw, cw, cb.reshape(1, F2), cb.reshape(1, F2))


def _convglu_bwd(hup, dact, cw, cb, *, name):
    T, F2 = hup.shape
    F = F2 // 2
    tc = LANES
    nt = F // tc

    def half(h, w_ref, d, dh_ref, dw_ref, db_ref, i):
        d1, d2 = _advance(d, 1), _advance(d, 2)
        dh_ref[i] = (w_ref[2:3, :] * d + w_ref[1:2, :] * d1 + w_ref[0:1, :] * d2).astype(BF16)
        dw_ref[i, 0:1, :] = jnp.sum(d2 * h, axis=0, keepdims=True)
        dw_ref[i, 1:2, :] = jnp.sum(d1 * h, axis=0, keepdims=True)
        dw_ref[i, 2:3, :] = jnp.sum(d * h, axis=0, keepdims=True)
        db_ref[i] = jnp.sum(d, axis=0, keepdims=True)

    def body(ha_ref, hg_ref, d_ref, wa_ref, wg_ref, ba_ref, bg_ref, dh_ref, dw_ref, db_ref):
        ha, hg = ha_ref[...], hg_ref[...]
        a = _conv3(ha, wa_ref, ba_ref)
        g = _conv3(hg, wg_ref, bg_ref)
        d = d_ref[...]
        half(ha, wa_ref, d * g * _gelu_grad(a), dh_ref, dw_ref, db_ref, 0)
        half(hg, wg_ref, d * _gelu(a), dh_ref, dw_ref, db_ref, 1)

    col = lambda off: pl.BlockSpec((T, tc), lambda j: (0, j + off))
    w3 = lambda off: pl.BlockSpec((3, tc), lambda j: (0, j + off))
    b1 = lambda off: pl.BlockSpec((1, tc), lambda j: (0, j + off))
    return pl.pallas_call(
        body, name=name, grid=(nt,),
        in_specs=[col(0), col(nt), col(0), w3(0), w3(nt), b1(0), b1(nt)],
        out_specs=[pl.BlockSpec((2, T, tc), lambda j: (0, 0, j)),
                   pl.BlockSpec((2, 3, tc), lambda j: (0, 0, j)),
                   pl.BlockSpec((2, 1, tc), lambda j: (0, 0, j))],
        out_shape=[jax.ShapeDtypeStruct((2, T, F), BF16),
                   jax.ShapeDtypeStruct((2, 3, F), F32),
                   jax.ShapeDtypeStruct((2, 1, F), F32)],
        compiler_params=_cp(("parallel",)),
    )(hup, hup, dact, cw, cw, cb.reshape(1, F2), cb.reshape(1, F2))


def _bucket_table():
    iq = np.arange(BLK)[:, None]
    ik = np.arange(2 * BLK)[None, :]
    delta = iq + BLK - ik
    out = []
    for win, dil in DILATED_GROUPS:
        valid = (delta >= 0) & (delta <= win // dil)
        n = (np.clip(delta, 0, None) * dil).astype(np.int32)
        max_exact = REL_BUCKETS // 2
        nf = np.maximum(n, 1).astype(np.float32)
        large = max_exact + (np.log(nf / np.float32(max_exact)) / np.float32(math.log(REL_MAX_DIST / max_exact))
                             * np.float32(REL_BUCKETS - max_exact)).astype(np.int32)
        large = np.minimum(large, REL_BUCKETS - 1)
        out.append(np.where(valid, np.where(n < max_exact, n, large), -1))
    return np.stack(out).astype(np.int32)


def _band_bias(rel_table, H, *, name):
    bkt = jnp.asarray(_bucket_table())

    def body(tab_ref, bk_ref, o_ref):
        g = pl.program_id(0)
        bk = bk_ref[...]
        prev = lax.broadcasted_iota(jnp.int32, bk.shape, 1) < BLK
        for h in range(H):
            def bucket(c, acc):
                return jnp.where(bk == c, tab_ref[c, g * H + h], acc)

            b = lax.fori_loop(0, REL_BUCKETS, bucket, jnp.zeros(bk.shape, F32))
            b = jnp.where(bk >= 0, b, NEG)
            o_ref[h, 0] = b
            o_ref[h, 1] = jnp.where(prev, NEG, b)

    return pl.pallas_call(
        body, name=name, grid=(N_GROUPS,),
        in_specs=[pl.BlockSpec(memory_space=pltpu.SMEM),
                  pl.BlockSpec((None, BLK, 2 * BLK), lambda g: (g, 0, 0))],
        out_specs=pl.BlockSpec((None, H, 2, BLK, 2 * BLK), lambda g: (g, 0, 0, 0, 0)),
        out_shape=jax.ShapeDtypeStruct((N_GROUPS, H, 2, BLK, 2 * BLK), F32),
        compiler_params=_cp(("parallel",)),
    )(rel_table, bkt)


PAIR = 2 * HEAD_DIM


def _bdot(a, b, ca, cb):
    return lax.dot_general(a, b, (((ca,), (cb,)), ((0,), (0,))), preferred_element_type=F32)


def _heads(ref, H, own_columns_only):
    low = lax.broadcasted_iota(jnp.int32, (BLK, PAIR), 1) < HEAD_DIM
    out = []
    for h in range(H):
        a = ref[:, (h // 2) * PAIR:(h // 2 + 1) * PAIR]
        if own_columns_only:
            a = jnp.where(low if h % 2 == 0 else jnp.logical_not(low), a, jnp.zeros_like(a))
        out.append(a)
    return jnp.stack(out)


def _pair_sums(o):
    return [o[2 * p] + o[2 * p + 1] for p in range(o.shape[0] // 2)]


def _store_pairs(ref, o, dtype, scale=None):
    for p, v in enumerate(_pair_sums(o)):
        ref[:, p * PAIR:(p + 1) * PAIR] = (v if scale is None else scale * v).astype(dtype)


def _stat_tile(s):
    lane = lax.broadcasted_iota(jnp.int32, (BLK, LANES), 1)
    t = jnp.zeros((BLK, LANES), F32)
    for h in range(s.shape[0]):
        t = jnp.where(lane == h, s[h], t)
    return t


def _stat_cols(ref, H):
    return jnp.stack([ref[:, h:h + 1] for h in range(H)])


def _scores(qm, kp, kc, bias):
    scale = HEAD_DIM ** -0.5
    sp = _bdot(qm, kp, 2, 2) * scale + bias[:, :, :BLK]
    sc = _bdot(qm, kc, 2, 2) * scale + bias[:, :, BLK:]
    return sp, sc


def _attn_specs(nb, dil, g, H, HD, qb):
    nbg = nb // dil
    cur = lambda col: pl.BlockSpec((BLK, HD), lambda b: (qb(b), col))
    prev = lambda col: pl.BlockSpec((BLK, HD), lambda b: (jnp.maximum(qb(b) - 1, 0), col))
    stat = pl.BlockSpec((BLK, LANES), lambda b: (qb(b), 0))
    bias = pl.BlockSpec((None, H, None, BLK, 2 * BLK),
                        lambda b: (g, 0, jnp.where(qb(b) % nbg == 0, 1, 0), 0, 0))
    return cur, prev, stat, bias


def _attn_fwd(q, k, v, bias, g, dil, *, name):
    (qa, qc), (ka, kc_), (va, vc_) = q, k, v
    T = qa.shape[0]
    H = bias.shape[1]
    HD = H * HEAD_DIM
    assert H % 2 == 0 and H <= LANES and T % (dil * BLK) == 0
    nb = T // BLK
    cur, prev, stat, bspec = _attn_specs(nb, dil, g, H, HD, lambda b: b)

    def body(q_ref, kp_ref, kc_ref, vp_ref, vc_ref, b_ref, o_ref, l_ref):
        qm, kp, kc = _heads(q_ref, H, True), _heads(kp_ref, H, False), _heads(kc_ref, H, False)
        sp, sc = _scores(qm, kp, kc, b_ref[...])
        m = jnp.maximum(jnp.max(sp, axis=-1, keepdims=True), jnp.max(sc, axis=-1, keepdims=True))
        pp, pc = jnp.exp(sp - m), jnp.exp(sc - m)
        den = jnp.sum(pp, axis=-1, keepdims=True) + jnp.sum(pc, axis=-1, keepdims=True)
        o = (_bdot((pp / den).astype(BF16), _heads(vp_ref, H, True), 2, 1)
             + _bdot((pc / den).astype(BF16), _heads(vc_ref, H, True), 2, 1))
        _store_pairs(o_ref, o, F32)
        l_ref[...] = _stat_tile(m + jnp.log(den))

    return pl.pallas_call(
        body, name=name, grid=(nb,),
        in_specs=[cur(qc), prev(kc_), cur(kc_), prev(vc_), cur(vc_), bspec],
        out_specs=[cur(0), stat],
        out_shape=[jax.ShapeDtypeStruct((T, HD), F32), jax.ShapeDtypeStruct((T, LANES), F32)],
        compiler_params=_cp(("parallel",)),
    )(qa, ka, ka, va, va, bias)


def _attn_combine(os, ls, *, name):
    n = len(os)
    T, HD = os[0].shape
    H = HD // HEAD_DIM
    tr = _tile(T, 512, SUBLANES)
    wide = pl.BlockSpec((tr, HD), lambda i: (i, 0))
    stat = pl.BlockSpec((tr, LANES), lambda i: (i, 0))

    def body(*refs):
        o_refs, l_refs = refs[:n], refs[n:2 * n]
        oc_ref, ocb_ref, lt_ref = refs[2 * n:]
        ls_ = [r[...] for r in l_refs]
        m = ls_[0]
        for l in ls_[1:]:
            m = jnp.maximum(m, l)
        ws = [jnp.exp(l - m) for l in ls_]
        den = ws[0]
        for w in ws[1:]:
            den = den + w
        lt_ref[...] = m + jnp.log(den)
        ws = [w / den for w in ws]
        low = lax.broadcasted_iota(jnp.int32, (tr, PAIR), 1) < HEAD_DIM
        for p in range(H // 2):
            cols = slice(p * PAIR, (p + 1) * PAIR)
            acc = jnp.zeros((tr, PAIR), F32)
            for g in range(n):
                w = jnp.where(low, ws[g][:, 2 * p:2 * p + 1], ws[g][:, 2 * p + 1:2 * p + 2])
                acc = acc + w * o_refs[g][:, cols]
            oc_ref[:, cols] = acc
            ocb_ref[:, cols] = acc.astype(BF16)

    return pl.pallas_call(
        body, name=name, grid=(T // tr,),
        in_specs=[wide] * n + [stat] * n, out_specs=[wide, wide, stat],
        out_shape=[jax.ShapeDtypeStruct((T, HD), F32), jax.ShapeDtypeStruct((T, HD), BF16),
                   jax.ShapeDtypeStruct((T, LANES), F32)],
        compiler_params=_cp(("parallel",)),
    )(*os, *ls)


def _attn_bwd_prep(do, oc, *, name):
    T, HD = do.shape
    H = HD // HEAD_DIM
    tr = _tile(T, 512, SUBLANES)
    wide = pl.BlockSpec((tr, HD), lambda i: (i, 0))
    stat = pl.BlockSpec((tr, LANES), lambda i: (i, 0))

    def body(do_ref, oc_ref, dl_ref, dob_ref):
        low = lax.broadcasted_iota(jnp.int32, (tr, PAIR), 1) < HEAD_DIM
        lane = lax.broadcasted_iota(jnp.int32, (tr, LANES), 1)
        dl = jnp.zeros((tr, LANES), F32)
        for p in range(H // 2):
            cols = slice(p * PAIR, (p + 1) * PAIR)
            d = do_ref[:, cols]
            prod = d * oc_ref[:, cols]
            dl = jnp.where(lane == 2 * p, jnp.sum(jnp.where(low, prod, 0.0), axis=-1, keepdims=True), dl)
            dl = jnp.where(lane == 2 * p + 1, jnp.sum(jnp.where(low, 0.0, prod), axis=-1, keepdims=True), dl)
            dob_ref[:, cols] = d.astype(BF16)
        dl_ref[...] = dl

    return pl.pallas_call(
        body, name=name, grid=(T // tr,),
        in_specs=[wide, wide], out_specs=[stat, wide],
        out_shape=[jax.ShapeDtypeStruct((T, LANES), F32), jax.ShapeDtypeStruct((T, HD), BF16)],
        compiler_params=_cp(("parallel",)),
    )(do, oc)


def _attn_bwd(q, k, v, bias, g, dil, do, lse, delta, *, name):
    (qa, qc), (ka, kc_), (va, vc_) = q, k, v
    T, HD = do.shape
    H = HD // HEAD_DIM
    nb = T // BLK
    scale = HEAD_DIM ** -0.5
    cur, prev, stat, bspec = _attn_specs(nb, dil, g, H, HD, lambda b: jnp.minimum(b, nb - 1))
    late = pl.BlockSpec((BLK, HD), lambda b: (jnp.maximum(b - 1, 0), 0))

    def body(q_ref, kp_ref, kc_ref, vp_ref, vc_ref, b_ref, do_ref, l_ref, dl_ref,
             dq_ref, dk_ref, dv_ref, db_ref, ck_ref, cv_ref):
        b = pl.program_id(0)

        @pl.when(b == 0)
        def _():
            db_ref[...] = jnp.zeros_like(db_ref)
            ck_ref[...] = jnp.zeros_like(ck_ref)
            cv_ref[...] = jnp.zeros_like(cv_ref)

        @pl.when(b < nb)
        def _():
            qm, dom = _heads(q_ref, H, True), _heads(do_ref, H, True)
            sp, sc = _scores(qm, _heads(kp_ref, H, False), _heads(kc_ref, H, False), b_ref[...])
            l, dl = _stat_cols(l_ref, H), _stat_cols(dl_ref, H)
            pp, pc = jnp.exp(sp - l), jnp.exp(sc - l)
            dsp = pp * (_bdot(dom, _heads(vp_ref, H, False), 2, 2) - dl)
            dsc = pc * (_bdot(dom, _heads(vc_ref, H, False), 2, 2) - dl)
            db_ref[:, :, :BLK] += dsp
            db_ref[:, :, BLK:] += dsc
            dspb, dscb = dsp.astype(BF16), dsc.astype(BF16)
            _store_pairs(dq_ref, _bdot(dspb, _heads(kp_ref, H, True), 2, 1)
                         + _bdot(dscb, _heads(kc_ref, H, True), 2, 1), BF16, scale)
            dkp = _pair_sums(_bdot(dspb, qm, 1, 1))
            dkc = _pair_sums(_bdot(dscb, qm, 1, 1))
            dvp = _pair_sums(_bdot(pp.astype(BF16), dom, 1, 1))
            dvc = _pair_sums(_bdot(pc.astype(BF16), dom, 1, 1))
            for p in range(H // 2):
                cols = slice(p * PAIR, (p + 1) * PAIR)
                dk_ref[:, cols] = ck_ref[:, cols] + scale * dkp[p]
                dv_ref[:, cols] = cv_ref[:, cols] + dvp[p]
                ck_ref[:, cols] = scale * dkc[p]
                cv_ref[:, cols] = dvc[p]

        @pl.when(b == nb)
        def _():
            dk_ref[...] = ck_ref[...]
            dv_ref[...] = cv_ref[...]

    f32 = jax.ShapeDtypeStruct((T, HD), F32)
    return pl.pallas_call(
        body, name=name, grid=(nb + 1,),
        in_specs=[cur(qc), prev(kc_), cur(kc_), prev(vc_), cur(vc_), bspec, cur(0), stat, stat],
        out_specs=[cur(0), late, late, pl.BlockSpec((H, BLK, 2 * BLK), lambda b: (0, 0, 0))],
        out_shape=[jax.ShapeDtypeStruct((T, HD), BF16), f32, f32,
                   jax.ShapeDtypeStruct((H, BLK, 2 * BLK), F32)],
        scratch_shapes=[pltpu.VMEM((BLK, HD), F32), pltpu.VMEM((BLK, HD), F32)],
        compiler_params=_cp(("arbitrary",)),
    )(qa, ka, ka, va, va, bias, do, lse, delta)


def _rel_grad(dbs, *, name):
    H = dbs[0].shape[1]
    n = len(dbs)
    assert H <= LANES
    bkt = jnp.asarray(_bucket_table())

    def body(*refs):
        bk = refs[n][...]
        o_ref = refs[n + 1]
        db = refs[0][...]
        for r in refs[1:n]:
            db = db + r[...]
        row = lax.broadcasted_iota(jnp.int32, (REL_BUCKETS, LANES), 0)
        col = lax.broadcasted_iota(jnp.int32, (REL_BUCKETS, LANES), 1)

        def bucket(c, acc):
            sel = bk == c
            for h in range(H):
                val = jnp.sum(jnp.where(sel, db[h], 0.0))
                acc = jnp.where((row == c) & (col == h), val, acc)
            return acc

        o_ref[...] = lax.fori_loop(0, REL_BUCKETS, bucket, jnp.zeros((REL_BUCKETS, LANES), F32))

    dspec = pl.BlockSpec((None, H, BLK, 2 * BLK), lambda g: (g, 0, 0, 0))
    return pl.pallas_call(
        body, name=name, grid=(N_GROUPS,),
        in_specs=[dspec] * n + [pl.BlockSpec((None, BLK, 2 * BLK), lambda g: (g, 0, 0))],
        out_specs=pl.BlockSpec((None, REL_BUCKETS, LANES), lambda g: (g, 0, 0)),
        out_shape=jax.ShapeDtypeStruct((N_GROUPS, REL_BUCKETS, LANES), F32),
        compiler_params=_cp(("parallel",)),
    )(*dbs, bkt)


def _to_group(a, d):
    T, C = a.shape
    return a if d == 1 else a.reshape(T // d, d, C).transpose(1, 0, 2).reshape(T, C)


def _from_group(a, d):
    T, C = a.shape
    return a if d == 1 else a.reshape(d, T // d, C).transpose(1, 0, 2).reshape(T, C)


def _rows_view(a):
    a2 = a.reshape(-1, a.shape[-1])
    R, C = a2.shape
    tr = _tile(R, max(SUBLANES, (1 << 18) // C), SUBLANES)
    return a2, R, C, tr


def _addn(xs, out_dtype, *, name):
    shape = xs[0].shape
    x2s = [_rows_view(x)[0] for x in xs]
    _, R, C, tr = _rows_view(xs[0])
    spec = pl.BlockSpec((tr, C), lambda i: (i, 0))

    def body(*refs):
        acc = refs[0][...].astype(F32)
        for r in refs[1:-1]:
            acc = acc + r[...].astype(F32)
        refs[-1][...] = acc.astype(out_dtype)

    out = pl.pallas_call(
        body, name=name, grid=(R // tr,),
        in_specs=[spec] * len(xs), out_specs=spec,
        out_shape=jax.ShapeDtypeStruct((R, C), out_dtype),
        compiler_params=_cp(("parallel",)),
    )(*x2s)
    return out.reshape(shape)


def _slot_sum(b_ref, own, my_id):
    acc = None
    for s in range(b_ref.shape[0]):
        term = jnp.where(my_id == s, own, b_ref[s].astype(F32))
        acc = term if acc is None else acc + term
    return acc


def _part_geometry(shape, axis):
    _, R, C = shape
    if axis == COL:
        return R // 2, C // N_CHIPS
    return R // (2 * N_CHIPS), C


def _sum_parts(buf, grad, axis, where, *, name):
    n, L, Rp, Cp = buf.shape
    assert (Rp, Cp) == _part_geometry(grad.shape, axis)
    tr = _tile(Rp, max(2 * SUBLANES, (1 << 17) // Cp), 2 * SUBLANES)
    nr = Rp // tr

    def own_map(l, i, w):
        if axis == COL:
            return (l, w[1] * nr + i, w[0])
        return (l, (2 * w[0] + w[1]) * nr + i, 0)

    def body(w_ref, b_ref, g_ref, o_ref):
        o_ref[...] = _slot_sum(b_ref, g_ref[...].astype(F32), w_ref[2])

    return pl.pallas_call(
        body, name=name,
        grid_spec=pltpu.PrefetchScalarGridSpec(
            num_scalar_prefetch=1, grid=(L, nr),
            in_specs=[pl.BlockSpec((n, None, tr, Cp), lambda l, i, w: (0, l, i, 0)),
                      pl.BlockSpec((None, tr, Cp), own_map)],
            out_specs=pl.BlockSpec((None, None, tr, Cp), lambda l, i, w: (l, w[1], i, 0))),
        out_shape=jax.ShapeDtypeStruct((L, 2, Rp, Cp), F32),
        compiler_params=_cp(("parallel", "parallel")),
    )(where, buf, grad)


def _sum_small(buf, own, where, *, name):
    n, R, C = buf.shape
    tr = _tile(R, 512, SUBLANES)

    def body(w_ref, b_ref, g_ref, o_ref):
        o_ref[...] = _slot_sum(b_ref, g_ref[...], w_ref[2])

    return pl.pallas_call(
        body, name=name,
        grid_spec=pltpu.PrefetchScalarGridSpec(
            num_scalar_prefetch=1, grid=(R // tr,),
            in_specs=[pl.BlockSpec((n, tr, C), lambda i, w: (0, i, 0)),
                      pl.BlockSpec((tr, C), lambda i, w: (i, 0))],
            out_specs=pl.BlockSpec((tr, C), lambda i, w: (i, 0))),
        out_shape=jax.ShapeDtypeStruct((R, C), F32),
        compiler_params=_cp(("parallel",)),
    )(where, buf, own)


def _place_shard(shard, layer, axis, where, out_dtype, *, name, after=None):
    _, R, C = shard.shape
    tr = _tile(R, max(2 * SUBLANES, (1 << 18) // C), 2 * SUBLANES)
    nr = R // tr
    if axis == COL:
        full, out_map = (1, R, C * N_CHIPS), (lambda i, w: (0, i, w[0]))
    else:
        full, out_map = (1, R * N_CHIPS, C), (lambda i, w: (0, w[0] * nr + i, 0))

    def body(w_ref, x_ref, *rest):
        rest[-1][...] = x_ref[...].astype(out_dtype)

    extra = [] if after is None else [after]
    return pl.pallas_call(
        body, name=name,
        grid_spec=pltpu.PrefetchScalarGridSpec(
            num_scalar_prefetch=1, grid=(nr,),
            in_specs=[pl.BlockSpec((None, tr, C), lambda i, w: (layer, i, 0))]
            + [pl.BlockSpec(memory_space=pl.ANY)] * len(extra),
            out_specs=pl.BlockSpec((None, tr, C), out_map)),
        out_shape=jax.ShapeDtypeStruct(full, out_dtype),
        compiler_params=_cp(("parallel",)),
    )(where, shard, *extra)


def _loss_grad(y, tgt, *, name):
    T, D = y.shape
    tr = _tile(T, 512, SUBLANES)
    row = pl.BlockSpec((tr, D), lambda i: (i, 0))

    def body(y_ref, t_ref, dy_ref, l_ref, acc_ref):
        i = pl.program_id(0)

        @pl.when(i == 0)
        def _():
            acc_ref[...] = jnp.zeros_like(acc_ref)

        err = y_ref[...] - t_ref[...]
        dy_ref[...] = err * (1.0 / D)
        acc_ref[...] += jnp.sum(err * err, axis=0, keepdims=True)

        @pl.when(i == T // tr - 1)
        def _():
            tot = jnp.sum(acc_ref[...], axis=1, keepdims=True) * (0.5 / D)
            l_ref[...] = jnp.broadcast_to(tot, (1, LANES))

    return pl.pallas_call(
        body, name=name, grid=(T // tr,),
        in_specs=[row, row],
        out_specs=[row, pl.BlockSpec((1, LANES), lambda i: (0, 0))],
        out_shape=[jax.ShapeDtypeStruct((T, D), F32), jax.ShapeDtypeStruct((1, LANES), F32)],
        scratch_shapes=[pltpu.VMEM((1, D), F32)],
        compiler_params=_cp(("arbitrary",)),
    )(y, tgt)


def _adamw(w, g, m, v, *, name):
    shape = w.shape
    w2, R, C, tr = _rows_view(w)
    spec = pl.BlockSpec((tr, C), lambda i: (i, 0))

    def body(w_ref, g_ref, m_ref, v_ref, d_ref, nm_ref, nv_ref):
        gv = g_ref[...]
        nm = ADAM_B1 * m_ref[...] + (1.0 - ADAM_B1) * gv
        nv = ADAM_B2 * v_ref[...] + (1.0 - ADAM_B2) * (gv * gv)
        m_hat = nm / (1.0 - ADAM_B1 ** ADAM_STEP)
        v_hat = nv / (1.0 - ADAM_B2 ** ADAM_STEP)
        d_ref[...] = -ADAM_LR * (m_hat / (jnp.sqrt(v_hat) + ADAM_EPS) + ADAM_WD * w_ref[...])
        nm_ref[...] = nm
        nv_ref[...] = nv

    shp = jax.ShapeDtypeStruct((R, C), F32)
    outs = pl.pallas_call(
        body, name=name, grid=(R // tr,),
        in_specs=[spec] * 4, out_specs=[spec] * 3, out_shape=[shp] * 3,
        compiler_params=_cp(("parallel",)),
    )(w2, g.reshape(R, C), m.reshape(R, C), v.reshape(R, C))
    return tuple(o.reshape(shape) for o in outs)


def _pack(arrs):
    flat = jnp.concatenate([a.reshape(-1).astype(F32) for a in arrs])
    unit = SUBLANES * LANES
    pad = (-flat.shape[0]) % unit
    return jnp.pad(flat, (0, pad)).reshape(-1, LANES)


def _unpack(buf, shapes):
    flat = buf.reshape(-1)
    out, off = [], 0
    for s in shapes:
        n = int(np.prod(s))
        out.append(flat[off:off + n].reshape(s))
        off += n
    return out


def _me():
    return lax.axis_index("x"), lax.axis_index("y"), lax.axis_index("c")


def _flip(pos, k):
    x, y, c = pos
    return (1 - x if k & 4 else x, 1 - y if k & 2 else y, 1 - c if k & 1 else c)


HBM = pl.BlockSpec(memory_space=pltpu.HBM)

COL, ROW = -1, -2


def _shard_window(ref, axis, j, n):
    start = pl.multiple_of(j * n, n)
    if axis == COL:
        return ref.at[:, :, pl.ds(start, n)]
    return ref.at[:, pl.ds(start, n), :]


GATHER_PEERS = (2, 4, 6)
SEM = pl.BlockSpec(memory_space=pltpu.SEMAPHORE)
EFFECT = pltpu.SideEffectType.DATAFLOW_SIDE_EFFECTING


def _gather_copy(buf, axis, i, pi, chip, me, send_sems, recv_sems):
    win = _shard_window(buf, axis, chip, buf.shape[axis] // N_CHIPS)
    k = len(GATHER_PEERS) * i + pi
    return pltpu.make_async_remote_copy(
        src_ref=win, dst_ref=win, send_sem=send_sems.at[k], recv_sem=recv_sems.at[k],
        device_id=_flip(me, GATHER_PEERS[pi]), device_id_type=MESH)


def _gather_start(fulls, axes, *, name):
    n = len(fulls)
    n_sem = len(GATHER_PEERS) * n

    def body(*refs):
        send_sems, recv_sems = refs[n], refs[n + 1]
        bufs = refs[n + 2:2 * n + 2]
        token = refs[2 * n + 2]
        me = _me()
        for i in range(n):
            for pi in range(len(GATHER_PEERS)):
                _gather_copy(bufs[i], axes[i], i, pi, 2 * me[0] + me[1], me, send_sems, recv_sems).start()
        token[...] = jnp.zeros_like(token)

    outs = pl.pallas_call(
        body, name=name,
        in_specs=[HBM] * n, out_specs=[SEM, SEM] + [HBM] * n + [pl.BlockSpec(memory_space=pltpu.VMEM)],
        out_shape=[pltpu.SemaphoreType.DMA((n_sem,)), pltpu.SemaphoreType.DMA((n_sem,))]
        + [pltpu.HBM(f.shape, f.dtype) for f in fulls] + [jax.ShapeDtypeStruct((SUBLANES, LANES), F32)],
        input_output_aliases={i: 2 + i for i in range(n)},
        compiler_params=pltpu.CompilerParams(has_side_effects=EFFECT),
    )(*[pltpu.with_memory_space_constraint(f, pltpu.HBM) for f in fulls])
    return outs[0], outs[1], list(outs[2:2 + n]), outs[-1]


def _gather_wait(send_sems, recv_sems, bufs, axes, idxs, after, *, name):
    m = len(bufs)

    def body(*refs):
        ss, rs = refs[m], refs[m + 1]
        outs = refs[m + 3:]
        me = _me()
        for t, i in enumerate(idxs):
            for pi, k in enumerate(GATHER_PEERS):
                px, py, _ = _flip(me, k)
                _gather_copy(outs[t], axes[t], i, pi, 2 * me[0] + me[1], me, ss, rs).wait_send()
                _gather_copy(outs[t], axes[t], i, pi, 2 * px + py, me, ss, rs).wait_recv()

    return pl.pallas_call(
        body, name=name,
        in_specs=[HBM] * m + [SEM, SEM, pl.BlockSpec(memory_space=pl.ANY)], out_specs=[HBM] * m,
        out_shape=[pltpu.HBM(b.shape, b.dtype) for b in bufs],
        input_output_aliases={t: t for t in range(m)},
        compiler_params=pltpu.CompilerParams(has_side_effects=EFFECT),
    )(*bufs, send_sems, recv_sems, after)


def _grad_part(ref, layer, axis, pos):
    px, py, pc = pos
    Rp, Cp = _part_geometry(ref.shape, axis)
    chip = 2 * px + py
    if axis == COL:
        return ref.at[layer, pl.ds(pl.multiple_of(pc * Rp, Rp), Rp), pl.ds(pl.multiple_of(chip * Cp, Cp), Cp)]
    return ref.at[layer, pl.ds(pl.multiple_of((2 * chip + pc) * Rp, Rp), Rp), :]


def _scatter_copy(grad, slots, layer, axis, i, k, src_pos, dst_slot, me, send_sems, recv_sems):
    sem = (N_DEV - 1) * i + k - 1
    return pltpu.make_async_remote_copy(
        src_ref=_grad_part(grad, layer, axis, src_pos), dst_ref=slots.at[dst_slot, layer],
        send_sem=send_sems.at[sem], recv_sem=recv_sems.at[sem],
        device_id=_flip(me, k), device_id_type=MESH)


def _scatter_start(grads, slots, layers, axes, *, name):
    n = len(grads)
    n_sem = (N_DEV - 1) * n

    def body(*refs):
        send_sems, recv_sems = refs[2 * n], refs[2 * n + 1]
        g_refs, s_refs = refs[2 * n + 2:3 * n + 2], refs[3 * n + 2:4 * n + 2]
        token = refs[4 * n + 2]
        me = _me()
        my_id = 4 * me[0] + 2 * me[1] + me[2]
        for i in range(n):
            for k in range(1, N_DEV):
                _scatter_copy(g_refs[i], s_refs[i], layers[i], axes[i], i, k, _flip(me, k), my_id, me,
                              send_sems, recv_sems).start()
        token[...] = jnp.zeros_like(token)

    arrays = list(grads) + list(slots)
    outs = pl.pallas_call(
        body, name=name,
        in_specs=[HBM] * (2 * n),
        out_specs=[SEM, SEM] + [HBM] * (2 * n) + [pl.BlockSpec(memory_space=pltpu.VMEM)],
        out_shape=[pltpu.SemaphoreType.DMA((n_sem,)), pltpu.SemaphoreType.DMA((n_sem,))]
        + [pltpu.HBM(a.shape, a.dtype) for a in arrays] + [jax.ShapeDtypeStruct((SUBLANES, LANES), F32)],
        input_output_aliases={i: 2 + i for i in range(2 * n)},
        compiler_params=pltpu.CompilerParams(has_side_effects=EFFECT),
    )(*[pltpu.with_memory_space_constraint(a, pltpu.HBM) for a in arrays])
    return outs[0], outs[1], list(outs[2:2 + n]), list(outs[2 + n:2 + 2 * n]), outs[-1]


def _scatter_wait(sems, grads, slots, axes, plan, *, name):
    n = len(grads)
    flat_sems = [s for pair in sems for s in pair]

    def body(*refs):
        sem_refs = refs[2 * n:2 * n + len(flat_sems)]
        g_refs, s_refs = refs[2 * n + len(flat_sems):3 * n + len(flat_sems)], refs[3 * n + len(flat_sems):]
        me = _me()
        for j, items in enumerate(plan):
            ss, rs = sem_refs[2 * j], sem_refs[2 * j + 1]
            for i, (a, layer) in enumerate(items):
                for k in range(1, N_DEV):
                    px, py, pc = _flip(me, k)
                    cp = _scatter_copy(g_refs[a], s_refs[a], layer, axes[a], i, k, me, 4 * px + 2 * py + pc, me, ss, rs)
                    cp.wait_send()
                    cp.wait_recv()

    arrays = list(grads) + list(slots)
    outs = pl.pallas_call(
        body, name=name,
        in_specs=[HBM] * (2 * n) + [SEM] * len(flat_sems), out_specs=[HBM] * (2 * n),
        out_shape=[pltpu.HBM(a.shape, a.dtype) for a in arrays],
        input_output_aliases={i: i for i in range(2 * n)},
        compiler_params=pltpu.CompilerParams(has_side_effects=EFFECT),
    )(*arrays, *flat_sems)
    return list(outs[:n]), list(outs[n:])


def _gather_small(small, *, name):
    def body(src, dst, send_sems, recv_sems):
        me = _me()
        my_id = 4 * me[0] + 2 * me[1] + me[2]

        def copy(k, slot):
            return pltpu.make_async_remote_copy(
                src_ref=src, dst_ref=dst.at[slot], send_sem=send_sems.at[k - 1], recv_sem=recv_sems.at[k - 1],
                device_id=_flip(me, k), device_id_type=MESH)

        sends = [copy(k, my_id) for k in range(1, N_DEV)]
        for cp in sends:
            cp.start()
        for k in range(1, N_DEV):
            px, py, pc = _flip(me, k)
            copy(k, 4 * px + 2 * py + pc).wait_recv()
        for cp in sends:
            cp.wait_send()

    return pl.pallas_call(
        body, name=name, in_specs=[HBM], out_specs=HBM,
        out_shape=jax.ShapeDtypeStruct((N_DEV,) + small.shape, small.dtype),
        scratch_shapes=[pltpu.SemaphoreType.DMA((N_DEV - 1,)), pltpu.SemaphoreType.DMA((N_DEV - 1,))],
    )(small)


def _pair_gather(halves, *, name):
    n = len(halves)

    def body(*refs):
        bufs = refs[n:2 * n]
        send_sems, recv_sems = refs[2 * n:]
        me = _me()
        c = me[2]
        sib = _flip(me, 1)

        def copy(i, half):
            return pltpu.make_async_remote_copy(
                src_ref=bufs[i].at[:, half], dst_ref=bufs[i].at[:, half], send_sem=send_sems.at[i],
                recv_sem=recv_sems.at[i], device_id=sib, device_id_type=MESH)

        sends = [copy(i, c) for i in range(n)]
        for cp in sends:
            cp.start()
        for i in range(n):
            copy(i, 1 - c).wait_recv()
        for cp in sends:
            cp.wait_send()

    return pl.pallas_call(
        body, name=name,
        in_specs=[HBM] * n, out_specs=[HBM] * n,
        out_shape=[jax.ShapeDtypeStruct(h.shape, h.dtype) for h in halves],
        input_output_aliases={i: i for i in range(n)},
        scratch_shapes=[pltpu.SemaphoreType.DMA((n,)), pltpu.SemaphoreType.DMA((n,))],
    )(*halves)


def _local_step(x, tgt, P, fetch, ship, n_layers):
    T, D = x.shape
    P = dict(P)
    W = {}
    depth = P["ffn_conv_b"].shape[0]
    n_a = P["a_w_s"].shape[0]
    alpha = (2 * depth) ** 0.25
    H = P["rel_table"].shape[1] // N_GROUPS
    HD = H * HEAD_DIM
    dils = [d for _, d in DILATED_GROUPS]
    assert T % (dils[-1] * BLK) == 0

    def arrive(stage, after):
        w_new, p_new = fetch(stage, after)
        W.update(w_new)
        P.update(p_new)

    bias = _band_bias(P["rel_table"], H, name="band_bias")
    saved = []
    xf, xb = x, x.astype(BF16)
    kg = vg = None
    for i in range(depth):
        s = {"x": xf, "xb": xb}
        arrive(f"{i}a", xf)
        if i < n_a:
            s["zp"] = _mm(xb, W["a_w_in", i], name=f"a{i}_in")
            arrive(f"{i}b", s["zp"])
            s["y"] = _sgu_fwd(s["zp"], P["a_w_s"][i], P["a_b_s"][i], P["a_ln_g"][i], P["a_ln_b"][i], name=f"a{i}_sgu")
            s["h"] = _mm(s["y"], W["a_w_out", i], name=f"a{i}_out")
        else:
            j = i - n_a
            if j == 0:
                kv = _mm(xb, W["kv_w", 0], out_dtype=BF16, name="kv_proj")
                kg = [(kv, 0) if d == 1 else (_to_group(kv[:, :HD], d), 0) for d in dils]
                vg = [(kv, 1) if d == 1 else (_to_group(kv[:, HD:], d), 0) for d in dils]
            q = _mm(xb, W["b_w_q", j], out_dtype=BF16, name=f"b{j}_q")
            s["qg"] = [(q, g) if d == 1 else (_to_group(q[:, g * HD:(g + 1) * HD], d), 0) for g, d in enumerate(dils)]
            os, ls = [], []
            for g, d in enumerate(dils):
                o_g, l_g = _attn_fwd(s["qg"][g], kg[g], vg[g], bias, g, d, name=f"b{j}_attn{g}")
                os.append(_from_group(o_g, d))
                ls.append(_from_group(l_g, d))
            s["oc"], s["ob"], s["lse"] = _attn_combine(os, ls, name=f"b{j}_comb")
            s["h"] = _mm(s["ob"], W["b_w_o", j], name=f"b{j}_o")
        s["x1"], s["x1b"] = _add_ln(xf, s["h"], P["ln_g"][i, 0], P["ln_b"][i, 0], alpha, name=f"l{i}_ln1")
        arrive(f"{i}c", s["x1b"])
        s["hup"] = _mm(s["x1b"], W["ffn_w_up", i], name=f"l{i}_up")
        arrive(f"{i}d", s["hup"])
        s["act"] = _convglu_fwd(s["hup"], P["ffn_conv_w"][i], P["ffn_conv_b"][i], name=f"l{i}_glu")
        s["f"] = _mm(s["act"], W["ffn_w_down", i], name=f"l{i}_down")
        xf, xb = _add_ln(s["x1"], s["f"], P["ln_g"][i, 1], P["ln_b"][i, 1], alpha, name=f"l{i}_ln2")
        saved.append(s)

    G, loss_row = _loss_grad(xf, tgt, name="loss")

    gw = {}

    def dw(key, layer, a, b, name, **kw):
        gw[key] = _mm(a, b, ta=True, out_dtype=BF16, out_into=(gw.get(key), n_layers[key], layer), name=name, **kw)

    gp = {k: [None] * n_a for k in ("a_ln_g", "a_ln_b", "a_w_s", "a_b_s")}
    gp.update({k: [None] * depth for k in ("ffn_conv_w", "ffn_conv_b", "ln_g", "ln_b")})
    dk_parts, dv_parts, dbias_parts = [], [], []
    for i in reversed(range(depth)):
        s = saved[i]
        dr2, dr2b, dg2, db2 = _ln_bwd(G, s["x1"], s["f"], P["ln_g"][i, 1], alpha, name=f"l{i}_ln2_bwd")
        dw("ffn_w_down", i, s["act"], dr2b, f"l{i}_down_dw")
        dact = _mm(dr2b, W["ffn_w_down", i], tb=True, name=f"l{i}_down_dx")
        dhup, dcw, dcb = _convglu_bwd(s["hup"], dact, P["ffn_conv_w"][i], P["ffn_conv_b"][i], name=f"l{i}_glu_bwd")
        gp["ffn_conv_w"][i] = dcw.transpose(1, 0, 2).reshape(dcw.shape[1], -1)
        gp["ffn_conv_b"][i] = dcb.reshape(-1)
        dw("ffn_w_up", i, s["x1b"], dhup, f"l{i}_up_dw", b_halves=True)
        token = ship(gw, [("ffn_w_down", i), ("ffn_w_up", i)])
        G1 = _mm(dhup, W["ffn_w_up", i], tb=True, a_halves=True, add=dr2, add_scale=alpha, after=token,
                 name=f"l{i}_up_dx")
        dr1, dr1b, dg1, db1 = _ln_bwd(G1, s["x"], s["h"], P["ln_g"][i, 0], alpha, name=f"l{i}_ln1_bwd")
        gp["ln_g"][i] = jnp.concatenate([dg1, dg2], axis=0)
        gp["ln_b"][i] = jnp.concatenate([db1, db2], axis=0)
        if i < n_a:
            dw("a_w_out", i, s["y"], dr1b, f"a{i}_out_dw")
            token = ship(gw, [("a_w_out", i)])
            dy = _mm(dr1b, W["a_w_out", i], tb=True, after=token, name=f"a{i}_out_dx")
            dzp, dws, dbs, dlg, dlb = _sgu_bwd(s["zp"], dy, P["a_w_s"][i], P["a_b_s"][i], P["a_ln_g"][i],
                                               P["a_ln_b"][i], name=f"a{i}_sgu_bwd")
            gp["a_w_s"][i], gp["a_b_s"][i] = dws, dbs[:, :dws.shape[0]].T
            gp["a_ln_g"][i], gp["a_ln_b"][i] = dlg[0], dlb[0]
            dw("a_w_in", i, s["xb"], dzp, f"a{i}_in_dw")
            token = ship(gw, [("a_w_in", i)])
            G = _mm(dzp, W["a_w_in", i], tb=True, add=dr1, add_scale=alpha, after=token, name=f"a{i}_in_dx")
        else:
            j = i - n_a
            dw("b_w_o", j, s["ob"], dr1b, f"b{j}_o_dw")
            do_tok = _mm(dr1b, W["b_w_o", j], tb=True, name=f"b{j}_o_dx")
            delta, dob = _attn_bwd_prep(do_tok, s["oc"], name=f"b{j}_prep")
            dqs, dbs = [], []
            for g, d in enumerate(dils):
                dq, dk, dv, db = _attn_bwd(s["qg"][g], kg[g], vg[g], bias, g, d, _to_group(dob, d),
                                           _to_group(s["lse"], d), _to_group(delta, d), name=f"b{j}_attn{g}_bwd")
                dqs.append(_from_group(dq, d))
                dk_parts.append(_from_group(dk, d))
                dv_parts.append(_from_group(dv, d))
                dbs.append(db)
            dbias_parts.append(jnp.stack(dbs))
            dq_tok = jnp.concatenate(dqs, axis=1)
            dw("b_w_q", j, s["xb"], dq_tok, f"b{j}_q_dw")
            token = ship(gw, [("b_w_o", j), ("b_w_q", j)])
            G = _mm(dq_tok, W["b_w_q", j], tb=True, add=dr1, add_scale=alpha, after=token, name=f"b{j}_q_dx")
            if j == 0:
                dkv = jnp.concatenate([_addn(dk_parts, BF16, name="dk_sum"), _addn(dv_parts, BF16, name="dv_sum")], axis=1)
                dw("kv_w", 0, s["xb"], dkv, "kv_dw")
                token = ship(gw, [("kv_w", 0)])
                G = _mm(dkv, W["kv_w", 0], tb=True, add=G, add_scale=1.0, after=token, name="kv_dx")
    rel = _rel_grad(dbias_parts, name="rel_grad")
    grel = rel[:, :, :H].transpose(1, 0, 2).reshape(REL_BUCKETS, N_GROUPS * H)
    gsmall = {k: jnp.stack(v) for k, v in gp.items()}
    gsmall["rel_table"] = grel
    return loss_row, G, gw, gsmall


BIG = (("a_w_in", COL), ("a_w_out", ROW), ("kv_w", ROW), ("b_w_q", COL), ("b_w_o", COL),
       ("ffn_w_up", COL), ("ffn_w_down", ROW))
SMALL_SHARDED = ("a_ln_g", "a_ln_b", "ffn_conv_w", "ln_g", "ln_b")
SMALL_REPLICATED = ("a_w_s", "a_b_s", "rel_table", "ffn_conv_b")
WEIGHTS = ("a_w_in", "a_ln_g", "a_ln_b", "a_w_s", "a_b_s", "a_w_out", "kv_w", "b_w_q", "b_w_o", "rel_table",
           "ffn_w_up", "ffn_conv_w", "ffn_conv_b", "ffn_w_down", "ln_g", "ln_b")


def kernel(x, a_w_in, a_ln_g, a_ln_b, a_w_s, a_b_s, a_w_out, kv_w, b_w_q, b_w_o, rel_table, ffn_w_up, ffn_conv_w, ffn_conv_b, ffn_w_down, ln_g, ln_b, loss_target, m_a_w_in, m_a_ln_g, m_a_ln_b, m_a_w_s, m_a_b_s, m_a_w_out, m_kv_w, m_b_w_q, m_b_w_o, m_rel_table, m_ffn_w_up, m_ffn_conv_w, m_ffn_conv_b, m_ffn_w_down, m_ln_g, m_ln_b, v_a_w_in, v_a_ln_g, v_a_ln_b, v_a_w_s, v_a_b_s, v_a_w_out, v_kv_w, v_b_w_q, v_b_w_o, v_rel_table, v_ffn_w_up, v_ffn_conv_w, v_ffn_conv_b, v_ffn_w_down, v_ln_g, v_ln_b):
    w = dict(a_w_in=a_w_in, a_ln_g=a_ln_g, a_ln_b=a_ln_b, a_w_s=a_w_s, a_b_s=a_b_s, a_w_out=a_w_out, kv_w=kv_w,
             b_w_q=b_w_q, b_w_o=b_w_o, rel_table=rel_table, ffn_w_up=ffn_w_up, ffn_conv_w=ffn_conv_w,
             ffn_conv_b=ffn_conv_b, ffn_w_down=ffn_w_down, ln_g=ln_g, ln_b=ln_b)
    m = dict(a_w_in=m_a_w_in, a_ln_g=m_a_ln_g, a_ln_b=m_a_ln_b, a_w_s=m_a_w_s, a_b_s=m_a_b_s, a_w_out=m_a_w_out,
             kv_w=m_kv_w, b_w_q=m_b_w_q, b_w_o=m_b_w_o, rel_table=m_rel_table, ffn_w_up=m_ffn_w_up,
             ffn_conv_w=m_ffn_conv_w, ffn_conv_b=m_ffn_conv_b, ffn_w_down=m_ffn_w_down, ln_g=m_ln_g, ln_b=m_ln_b)
    v = dict(a_w_in=v_a_w_in, a_ln_g=v_a_ln_g, a_ln_b=v_a_ln_b, a_w_s=v_a_w_s, a_b_s=v_a_b_s, a_w_out=v_a_w_out,
             kv_w=v_kv_w, b_w_q=v_b_w_q, b_w_o=v_b_w_o, rel_table=v_rel_table, ffn_w_up=v_ffn_w_up,
             ffn_conv_w=v_ffn_conv_w, ffn_conv_b=v_ffn_conv_b, ffn_w_down=v_ffn_w_down, ln_g=v_ln_g, ln_b=v_ln_b)
    chip = 2 * lax.axis_index("x") + lax.axis_index("y")

    big_names = [n for n, _ in BIG]
    big_axes = [a for _, a in BIG]
    where = jnp.stack([chip, lax.axis_index("c"), 2 * chip + lax.axis_index("c")]).astype(jnp.int32)
    shards = [w[n].reshape((1,) + w[n].shape) if w[n].ndim == 2 else w[n] for n in big_names]
    small_shapes = [w[n].shape for n in SMALL_SHARDED]
    small = _pack([w[n] for n in SMALL_SHARDED])
    shard_of = dict(zip(big_names, shards))
    axis_of = dict(BIG)
    n_layers = {n: s.shape[0] for n, s in shard_of.items()}
    n_a = w["a_w_s"].shape[0]
    stages = {"0a": [("a_w_in", 0), ("small", 0)], "0b": [("a_w_out", 0)], "0c": [("ffn_w_up", 0)],
              "0d": [("ffn_w_down", 0)]}
    for i in range(1, w["ffn_w_up"].shape[0]):
        if i < n_a:
            mixer = [("a_w_in", i), ("a_w_out", i)]
        else:
            mixer = ([("kv_w", 0)] if i == n_a else []) + [("b_w_q", i - n_a), ("b_w_o", i - n_a)]
        stages[f"{i}a"] = mixer + [("ffn_w_up", i), ("ffn_w_down", i)]
    assert sum(len(st) for st in stages.values()) == sum(n_layers.values()) + 1
    flights, full_shape, all_started = {}, {}, None
    for batch, names in enumerate(([st for st in stages if st[0] == "0"], [st for st in stages if st[0] != "0"])):
        order = [key for st in names for key in stages[st]]
        placed, axes = [], []
        for n, l in order:
            if n == "small":
                placed.append(_place_shard(small[None], 0, ROW, where, F32, name="place_small"))
                axes.append(ROW)
            else:
                placed.append(_place_shard(shard_of[n], l, axis_of[n], where, BF16, name=f"place_{n}{l}",
                                           after=all_started))
                axes.append(axis_of[n])
                full_shape[n] = (n_layers[n],) + placed[-1].shape[1:]
        send_sems, recv_sems, flying, all_started = _gather_start(placed, axes, name=f"gather_start{batch}")
        for st in names:
            flights[st] = (send_sems, recv_sems, flying, axes, order)

    def fetch(stage, after):
        if stage not in stages:
            return {}, {}
        send_sems, recv_sems, flying, axes, order = flights[stage]
        if stage == "0a":
            after = all_started
        idxs = [order.index(key) for key in stages[stage]]
        landed = _gather_wait(send_sems, recv_sems, [flying[i] for i in idxs], [axes[i] for i in idxs], idxs, after,
                              name=f"gather_wait_{stage}")
        W_new, P_new = {}, {}
        for key, arr in zip(stages[stage], landed):
            if key[0] == "small":
                small_all = arr.reshape((N_CHIPS,) + small.shape)
                per_chip = [_unpack(small_all[j], small_shapes) for j in range(N_CHIPS)]
                for i, n in enumerate(SMALL_SHARDED):
                    P_new[n] = jnp.concatenate([per_chip[j][i] for j in range(N_CHIPS)], axis=-1)
            else:
                W_new[key] = arr.reshape(arr.shape[1:])
        return W_new, P_new

    slots ={n: lax.empty((N_DEV, n_layers[n]) + _part_geometry(full_shape[n], axis_of[n]), BF16) for n in big_names}
    sems, plan = [], []

    def ship(gw, items):
        names = [n for n, _ in items]
        send, recv, grads, landing, token = _scatter_start(
            [gw[n] for n in names], [slots[n] for n in names], [l for _, l in items], [axis_of[n] for n in names],
            name="scatter_start_" + "_".join(f"{n}{l}" for n, l in items))
        for n, g, s in zip(names, grads, landing):
            gw[n], slots[n] = g, s
        sems.append((send, recv))
        plan.append([(big_names.index(n), l) for n, l in items])
        return token

    loss_row, grad_x, gw, gsmall = _local_step(x[0], loss_target[0], {n: w[n] for n in SMALL_REPLICATED}, fetch,
                                               ship, n_layers)
    loss = lax.psum(loss_row[0, 0], ("x", "y", "c"))

    small_names = list(SMALL_SHARDED) + list(SMALL_REPLICATED)
    small_pack = _pack([gsmall[n] for n in small_names])
    own, landed = _scatter_wait(sems, [gw[n] for n in big_names], [slots[n] for n in big_names], big_axes, plan,
                                name="scatter_wait")
    sums = [_sum_parts(b, g, ax, where, name=f"sum_{n}") for (n, ax), b, g in zip(BIG, landed, own)]
    pairs = _pair_gather(sums, name="pair_gather")
    small_sum = _sum_small(_gather_small(small_pack, name="gather_small"), small_pack, where, name="sum_small")
    grad = {n: p.reshape(w[n].shape) for n, p in zip(big_names, pairs)}
    full_small = _unpack(small_sum, [gsmall[n].shape for n in small_names])
    for n, gfull in zip(small_names, full_small):
        if n in SMALL_SHARDED:
            width = w[n].shape[-1]
            grad[n] = lax.dynamic_slice_in_dim(gfull, chip * width, width, axis=-1)
        else:
            grad[n] = gfull

    delta, new_m, new_v = {}, {}, {}
    for n, _ in BIG:
        delta[n], new_m[n], new_v[n] = _adamw(w[n], grad[n], m[n], v[n], name=f"adamw_{n}")
    shapes = [w[n].shape for n in small_names]
    packed = _adamw(_pack([w[n] for n in small_names]), _pack([grad[n] for n in small_names]),
                    _pack([m[n] for n in small_names]), _pack([v[n] for n in small_names]), name="adamw_small")
    for out, res in zip((delta, new_m, new_v), packed):
        for n, a in zip(small_names, _unpack(res, shapes)):
            out[n] = a

    return (loss, grad_x[None], *[grad[n] for n in WEIGHTS], *[delta[n] for n in WEIGHTS],
            *[new_m[n] for n in WEIGHTS], *[new_v[n] for n in WEIGHTS])
```

```python
import functools
import math

import numpy as np
import jax
import jax.numpy as jnp
from jax import lax
from jax.experimental import pallas as pl
from jax.experimental.pallas import tpu as pltpu

F32 = jnp.float32
BF16 = jnp.bfloat16
MESH = pl.DeviceIdType.MESH

CHUNK = 128
HEAD_DIM = 64
DILATED_GROUPS = ((128, 1), (512, 4), (2048, 16))
N_GROUPS = len(DILATED_GROUPS)
BLK = 128
REL_BUCKETS = 32
REL_MAX_DIST = 2048
LN_EPS = 1e-5
NEG = -1e30
ADAM_LR = 0.001
ADAM_B1 = 0.9
ADAM_B2 = 0.999
ADAM_EPS = 1e-08
ADAM_WD = 0.01
ADAM_STEP = 10

N_CHIPS = 4
N_DEV = 8
LANES = 128
SUBLANES = 8
VMEM_LIMIT = 48 * 1024 * 1024

_SQRT_HALF = 0.7071067811865476
_INV_SQRT_2PI = 0.3989422804014327


def _cp(sem=None, vmem=VMEM_LIMIT):
    return pltpu.CompilerParams(dimension_semantics=sem, vmem_limit_bytes=vmem)


def _tile(dim, target, align=LANES):
    if dim <= target:
        return dim
    t = (target // align) * align
    while t >= align:
        if dim % t == 0:
            return t
        t -= align
    return dim


def _gelu(x):
    return 0.5 * x * (1.0 + lax.erf(x * _SQRT_HALF))


def _gelu_grad(x):
    return 0.5 * (1.0 + lax.erf(x * _SQRT_HALF)) + x * (_INV_SQRT_2PI * jnp.exp(-0.5 * x * x))


def _dot(a, b, ca, cb):
    return lax.dot_general(a, b, (((ca,), (cb,)), ((), ())), preferred_element_type=F32)


def _mn_tile(dim):
    return max(_tile(dim, 1024), _tile(dim, 1408))


def _mm(a, b, *, name, ta=False, tb=False, out_dtype=F32, tiles=None,
        a_lead=(), b_lead=(), a_halves=False, b_halves=False, add=None, add_scale=1.0, out_into=None, after=None):
    a2, b2 = a.shape[-2:], b.shape[-2:]
    M, K = (a2[1], a2[0]) if ta else a2
    Kb, N = (b2[1], b2[0]) if tb else b2
    if a_halves:
        K = 2 * K
    if b_halves:
        N = 2 * N
    assert K == Kb, (name, a.shape, b.shape)
    if tiles is None:
        k_target = 1408 if a_halves else (K if K <= 2816 else 2048)
        tm, tn = _mn_tile(M), _mn_tile(N // 2 if b_halves else N)
        tk = _tile(K // 2 if a_halves else K, k_target)
    else:
        tm = _tile(M, tiles[0])
        tn = _tile(N // 2 if b_halves else N, tiles[1])
        tk = _tile(K // 2 if a_halves else K, tiles[2])
    nm, nn, nk = M // tm, N // tn, K // tk
    nkh, nnh = nk // 2, nn // 2

    def lead(idx, rest):
        return tuple(idx) + tuple(rest)

    sq_a = (None,) * (a.ndim - 2)
    sq_b = (None,) * (b.ndim - 2)
    if a_halves:
        assert not ta and a.ndim == 3
        a_spec = pl.BlockSpec((None, tm, tk), lambda i, j, k: (k // nkh, i, k % nkh))
    elif ta:
        a_spec = pl.BlockSpec(sq_a + (tk, tm), lambda i, j, k: lead(a_lead, (k, i)))
    else:
        a_spec = pl.BlockSpec(sq_a + (tm, tk), lambda i, j, k: lead(a_lead, (i, k)))
    if b_halves:
        assert not tb and b.ndim == 3
        b_spec = pl.BlockSpec((None, tk, tn), lambda i, j, k: (j // nnh, k, j % nnh))
    elif tb:
        b_spec = pl.BlockSpec(sq_b + (tn, tk), lambda i, j, k: lead(b_lead, (j, k)))
    else:
        b_spec = pl.BlockSpec(sq_b + (tk, tn), lambda i, j, k: lead(b_lead, (k, j)))
    mn_spec = pl.BlockSpec((tm, tn), lambda i, j, k: (i, j))
    in_specs = [a_spec, b_spec]
    args = [a, b]
    if add is not None:
        in_specs.append(mn_spec)
        args.append(add)
    aliases = {}
    if out_into is None:
        o_spec, out_shape = mn_spec, jax.ShapeDtypeStruct((M, N), out_dtype)
    else:
        buf, n_layers, layer = out_into
        o_spec = pl.BlockSpec((None, tm, tn), lambda i, j, k: (layer, i, j))
        out_shape = jax.ShapeDtypeStruct((n_layers, M, N), out_dtype)
        if buf is not None:
            aliases = {len(args): 0}
            in_specs.append(pl.BlockSpec(memory_space=pl.ANY))
            args.append(buf)
    if after is not None:
        in_specs.append(pl.BlockSpec(memory_space=pl.ANY))
        args.append(after)
    n_in = len(args)
    ca, cb = (0 if ta else 1), (1 if tb else 0)

    def body(*refs):
        a_ref, b_ref = refs[:2]
        add_ref = refs[2] if add is not None else None
        o_ref = refs[n_in]

        def finish(acc):
            if add_ref is not None:
                acc = acc + add_scale * add_ref[...]
            o_ref[...] = acc.astype(o_ref.dtype)

        if nk == 1:
            finish(_dot(a_ref[...], b_ref[...], ca, cb))
        else:
            acc_ref = refs[-1]
            k = pl.program_id(2)

            @pl.when(k == 0)
            def _():
                acc_ref[...] = jnp.zeros_like(acc_ref)

            acc_ref[...] += _dot(a_ref[...], b_ref[...], ca, cb)

            @pl.when(k == nk - 1)
            def _():
                finish(acc_ref[...])

    return pl.pallas_call(
        body, name=name, grid=(nm, nn, nk),
        in_specs=in_specs, out_specs=o_spec, out_shape=out_shape,
        input_output_aliases=aliases,
        scratch_shapes=[] if nk == 1 else [pltpu.VMEM((tm, tn), F32)],
        compiler_params=_cp(("parallel", "parallel", "arbitrary")),
    )(*args)


def _ln_stats(r):
    mu = jnp.mean(r, axis=-1, keepdims=True)
    xc = r - mu
    var = jnp.mean(xc * xc, axis=-1, keepdims=True)
    rstd = lax.rsqrt(var + LN_EPS)
    return xc * rstd, rstd


def _add_ln(x, h, g, b, alpha, *, name):
    T, D = x.shape
    tr = _tile(T, 512, SUBLANES)
    row = pl.BlockSpec((tr, D), lambda i: (i, 0))
    vec = pl.BlockSpec((1, D), lambda i: (0, 0))

    def body(x_ref, h_ref, g_ref, b_ref, o_ref, ob_ref):
        xhat, _ = _ln_stats(alpha * x_ref[...] + h_ref[...])
        y = xhat * g_ref[...] + b_ref[...]
        o_ref[...] = y
        ob_ref[...] = y.astype(BF16)

    return pl.pallas_call(
        body, name=name, grid=(T // tr,),
        in_specs=[row, row, vec, vec], out_specs=[row, row],
        out_shape=[jax.ShapeDtypeStruct((T, D), F32), jax.ShapeDtypeStruct((T, D), BF16)],
        compiler_params=_cp(("parallel",)),
    )(x, h, g.reshape(1, D), b.reshape(1, D))


def _ln_bwd(dy, x, h, g, alpha, *, name):
    T, D = x.shape
    tr = _tile(T, 512, SUBLANES)
    row = pl.BlockSpec((tr, D), lambda i: (i, 0))
    vec = pl.BlockSpec((1, D), lambda i: (0, 0))

    def body(dy_ref, x_ref, h_ref, g_ref, dr_ref, drb_ref, dg_ref, db_ref):
        @pl.when(pl.program_id(0) == 0)
        def _():
            dg_ref[...] = jnp.zeros_like(dg_ref)
            db_ref[...] = jnp.zeros_like(db_ref)

        xhat, rstd = _ln_stats(alpha * x_ref[...] + h_ref[...])
        dyv = dy_ref[...]
        dxhat = dyv * g_ref[...]
        c1 = jnp.mean(dxhat, axis=-1, keepdims=True)
        c2 = jnp.mean(dxhat * xhat, axis=-1, keepdims=True)
        dr = rstd * (dxhat - c1 - xhat * c2)
        dr_ref[...] = dr
        drb_ref[...] = dr.astype(BF16)
        dg_ref[...] += jnp.sum(dyv * xhat, axis=0, keepdims=True)
        db_ref[...] += jnp.sum(dyv, axis=0, keepdims=True)

    return pl.pallas_call(
        body, name=name, grid=(T // tr,),
        in_specs=[row, row, row, vec], out_specs=[row, row, vec, vec],
        out_shape=[jax.ShapeDtypeStruct((T, D), F32), jax.ShapeDtypeStruct((T, D), BF16),
                   jax.ShapeDtypeStruct((1, D), F32), jax.ShapeDtypeStruct((1, D), F32)],
        compiler_params=_cp(("arbitrary",)),
    )(dy, x, h, g.reshape(1, D))


def _sgu_pieces(zp, lg, lb, E):
    z = _gelu(zp)
    u, v = z[:, :E], z[:, E:]
    vhat, rstd = _ln_stats(v)
    vn = vhat * lg + lb
    return u, vhat, rstd, vn


def _tril_mask():
    t = lax.broadcasted_iota(jnp.int32, (CHUNK, CHUNK), 0)
    s = lax.broadcasted_iota(jnp.int32, (CHUNK, CHUNK), 1)
    return s <= t


def _sgu_fwd(zp, ws, bs, lg, lb, *, name):
    T, E2 = zp.shape
    E = E2 // 2
    G = ws.shape[0]
    cw = E // G

    def body(zp_ref, ws_ref, bs_ref, lg_ref, lb_ref, y_ref):
        u, _, _, vn = _sgu_pieces(zp_ref[...], lg_ref[...], lb_ref[...], E)
        vnb = vn.astype(BF16)
        tril = _tril_mask()
        for g in range(G):
            cols = slice(g * cw, (g + 1) * cw)
            w = jnp.where(tril, ws_ref[g], 0.0).astype(BF16)
            sv = _dot(w, vnb[:, cols], 1, 0) + bs_ref[g]
            y_ref[:, cols] = (u[:, cols] * sv).astype(BF16)

    return pl.pallas_call(
        body, name=name, grid=(T // CHUNK,),
        in_specs=[pl.BlockSpec((CHUNK, E2), lambda n: (n, 0)),
                  pl.BlockSpec((G, CHUNK, CHUNK), lambda n: (0, 0, 0)),
                  pl.BlockSpec((G, CHUNK, 1), lambda n: (0, 0, 0)),
                  pl.BlockSpec((1, E), lambda n: (0, 0)),
                  pl.BlockSpec((1, E), lambda n: (0, 0))],
        out_specs=pl.BlockSpec((CHUNK, E), lambda n: (n, 0)),
        out_shape=jax.ShapeDtypeStruct((T, E), BF16),
        compiler_params=_cp(("parallel",)),
    )(zp, ws, bs.reshape(G, CHUNK, 1), lg.reshape(1, E), lb.reshape(1, E))


def _sgu_bwd(zp, dy, ws, bs, lg, lb, *, name):
    T, E2 = zp.shape
    E = E2 // 2
    G = ws.shape[0]
    cw = E // G
    assert G <= LANES

    def body(zp_ref, dy_ref, ws_ref, bs_ref, lg_ref, lb_ref,
             dzp_ref, dws_ref, dbs_ref, dlg_ref, dlb_ref, dvn_ref):
        @pl.when(pl.program_id(0) == 0)
        def _():
            dws_ref[...] = jnp.zeros_like(dws_ref)
            dbs_ref[...] = jnp.zeros_like(dbs_ref)
            dlg_ref[...] = jnp.zeros_like(dlg_ref)
            dlb_ref[...] = jnp.zeros_like(dlb_ref)

        zpv = zp_ref[...]
        lgv = lg_ref[...]
        u, vhat, rstd, vn = _sgu_pieces(zpv, lgv, lb_ref[...], E)
        gp = _gelu_grad(zpv)
        vnb = vn.astype(BF16)
        tril = _tril_mask()
        lane = lax.broadcasted_iota(jnp.int32, (CHUNK, LANES), 1)
        dbs = dbs_ref[...]
        for g in range(G):
            cols = slice(g * cw, (g + 1) * cw)
            w = jnp.where(tril, ws_ref[g], 0.0).astype(BF16)
            sv = _dot(w, vnb[:, cols], 1, 0) + bs_ref[g]
            dyg = dy_ref[:, cols]
            dzp_ref[:, cols] = (dyg * sv * gp[:, cols]).astype(BF16)
            dsv = dyg * u[:, cols]
            dsvb = dsv.astype(BF16)
            dws_ref[g] += jnp.where(tril, _dot(dsvb, vnb[:, cols], 1, 1), 0.0)
            dvn_ref[:, cols] = _dot(w, dsvb, 0, 0)
            dbs = dbs + jnp.where(lane == g, jnp.sum(dsv, axis=1, keepdims=True), 0.0)
        dbs_ref[...] = dbs
        dvn = dvn_ref[...]
        dlg_ref[...] += jnp.sum(dvn * vhat, axis=0, keepdims=True)
        dlb_ref[...] += jnp.sum(dvn, axis=0, keepdims=True)
        dvhat = dvn * lgv
        c1 = jnp.mean(dvhat, axis=-1, keepdims=True)
        c2 = jnp.mean(dvhat * vhat, axis=-1, keepdims=True)
        dv = rstd * (dvhat - c1 - vhat * c2)
        dzp_ref[:, E:] = (dv * gp[:, E:]).astype(BF16)

    vecE = pl.BlockSpec((1, E), lambda n: (0, 0))
    return pl.pallas_call(
        body, name=name, grid=(T // CHUNK,),
        in_specs=[pl.BlockSpec((CHUNK, E2), lambda n: (n, 0)),
                  pl.BlockSpec((CHUNK, E), lambda n: (n, 0)),
                  pl.BlockSpec((G, CHUNK, CHUNK), lambda n: (0, 0, 0)),
                  pl.BlockSpec((G, CHUNK, 1), lambda n: (0, 0, 0)),
                  vecE, vecE],
        out_specs=[pl.BlockSpec((CHUNK, E2), lambda n: (n, 0)),
                   pl.BlockSpec((G, CHUNK, CHUNK), lambda n: (0, 0, 0)),
                   pl.BlockSpec((CHUNK, LANES), lambda n: (0, 0)),
                   vecE, vecE],
        out_shape=[jax.ShapeDtypeStruct((T, E2), BF16),
                   jax.ShapeDtypeStruct((G, CHUNK, CHUNK), F32),
                   jax.ShapeDtypeStruct((CHUNK, LANES), F32),
                   jax.ShapeDtypeStruct((1, E), F32), jax.ShapeDtypeStruct((1, E), F32)],
        scratch_shapes=[pltpu.VMEM((CHUNK, E), F32)],
        compiler_params=_cp(("arbitrary",)),
    )(zp, dy, ws, bs.reshape(G, CHUNK, 1), lg.reshape(1, E), lb.reshape(1, E))


def _delay(h, k):
    t = lax.broadcasted_iota(jnp.int32, h.shape, 0)
    return jnp.where(t >= k, pltpu.roll(h, k, 0), 0.0)


def _advance(d, k):
    T = d.shape[0]
    t = lax.broadcasted_iota(jnp.int32, d.shape, 0)
    return jnp.where(t < T - k, pltpu.roll(d, T - k, 0), 0.0)


def _conv3(h, w_ref, b_ref):
    return w_ref[2:3, :] * h + w_ref[1:2, :] * _delay(h, 1) + w_ref[0:1, :] * _delay(h, 2) + b_ref[...]


def _convglu_fwd(hup, cw, cb, *, name):
    T, F2 = hup.shape
    F = F2 // 2
    tc = LANES
    nt = F // tc

    def body(ha_ref, hg_ref, wa_ref, wg_ref, ba_ref, bg_ref, o_ref):
        a = _conv3(ha_ref[...], wa_ref, ba_ref)
        g = _conv3(hg_ref[...], wg_ref, bg_ref)
        o_ref[...] = (_gelu(a) * g).astype(BF16)

    col = lambda off: pl.BlockSpec((T, tc), lambda j: (0, j + off))
    w3 = lambda off: pl.BlockSpec((3, tc), lambda j: (0, j + off))
    b1 = lambda off: pl.BlockSpec((1, tc), lambda j: (0, j + off))
    return pl.pallas_call(
        body, name=name, grid=(nt,),
        in_specs=[col(0), col(nt), w3(0), w3(nt), b1(0), b1(nt)],
        out_specs=pl.BlockSpec((T, tc), lambda j: (0, j)),
        out_shape=jax.ShapeDtypeStruct((T, F), BF16),
        compiler_params=_cp(("parallel",)),
    )(hup, hup, cw, cw, cb.reshape(1, F2), cb.reshape(1, F2))


def _convglu_bwd(hup, dact, cw, cb, *, name):
    T, F2 = hup.shape
    F = F2 // 2
    tc = LANES
    nt = F // tc

    def half(h, w_ref, d, dh_ref, dw_ref, db_ref, i):
        d1, d2 = _advance(d, 1), _advance(d, 2)
        dh_ref[i] = (w_ref[2:3, :] * d + w_ref[1:2, :] * d1 + w_ref[0:1, :] * d2).astype(BF16)
        dw_ref[i, 0:1, :] = jnp.sum(d2 * h, axis=0, keepdims=True)
        dw_ref[i, 1:2, :] = jnp.sum(d1 * h, axis=0, keepdims=True)
        dw_ref[i, 2:3, :] = jnp.sum(d * h, axis=0, keepdims=True)
        db_ref[i] = jnp.sum(d, axis=0, keepdims=True)

    def body(ha_ref, hg_ref, d_ref, wa_ref, wg_ref, ba_ref, bg_ref, dh_ref, dw_ref, db_ref):
        ha, hg = ha_ref[...], hg_ref[...]
        a = _conv3(ha, wa_ref, ba_ref)
        g = _conv3(hg, wg_ref, bg_ref)
        d = d_ref[...]
        half(ha, wa_ref, d * g * _gelu_grad(a), dh_ref, dw_ref, db_ref, 0)
        half(hg, wg_ref, d * _gelu(a), dh_ref, dw_ref, db_ref, 1)

    col = lambda off: pl.BlockSpec((T, tc), lambda j: (0, j + off))
    w3 = lambda off: pl.BlockSpec((3, tc), lambda j: (0, j + off))
    b1 = lambda off: pl.BlockSpec((1, tc), lambda j: (0, j + off))
    return pl.pallas_call(
        body, name=name, grid=(nt,),
        in_specs=[col(0), col(nt), col(0), w3(0), w3(nt), b1(0), b1(nt)],
        out_specs=[pl.BlockSpec((2, T, tc), lambda j: (0, 0, j)),
                   pl.BlockSpec((2, 3, tc), lambda j: (0, 0, j)),
                   pl.BlockSpec((2, 1, tc), lambda j: (0, 0, j))],
        out_shape=[jax.ShapeDtypeStruct((2, T, F), BF16),
                   jax.ShapeDtypeStruct((2, 3, F), F32),
                   jax.ShapeDtypeStruct((2, 1, F), F32)],
        compiler_params=_cp(("parallel",)),
    )(hup, hup, dact, cw, cw, cb.reshape(1, F2), cb.reshape(1, F2))


def _bucket_table():
    iq = np.arange(BLK)[:, None]
    ik = np.arange(2 * BLK)[None, :]
    delta = iq + BLK - ik
    out = []
    for win, dil in DILATED_GROUPS:
        valid = (delta >= 0) & (delta <= win // dil)
        n = (np.clip(delta, 0, None) * dil).astype(np.int32)
        max_exact = REL_BUCKETS // 2
        nf = np.maximum(n, 1).astype(np.float32)
        large = max_exact + (np.log(nf / np.float32(max_exact)) / np.float32(math.log(REL_MAX_DIST / max_exact))
                             * np.float32(REL_BUCKETS - max_exact)).astype(np.int32)
        large = np.minimum(large, REL_BUCKETS - 1)
        out.append(np.where(valid, np.where(n < max_exact, n, large), -1))
    return np.stack(out).astype(np.int32)


def _band_bias(rel_table, H, *, name):
    bkt = jnp.asarray(_bucket_table())

    def body(tab_ref, bk_ref, o_ref):
        g = pl.program_id(0)
        bk = bk_ref[...]
        prev = lax.broadcasted_iota(jnp.int32, bk.shape, 1) < BLK
        for h in range(H):
            def bucket(c, acc):
                return jnp.where(bk == c, tab_ref[c, g * H + h], acc)

            b = lax.fori_loop(0, REL_BUCKETS, bucket, jnp.zeros(bk.shape, F32))
            b = jnp.where(bk >= 0, b, NEG)
            o_ref[h, 0] = b
            o_ref[h, 1] = jnp.where(prev, NEG, b)

    return pl.pallas_call(
        body, name=name, grid=(N_GROUPS,),
        in_specs=[pl.BlockSpec(memory_space=pltpu.SMEM),
                  pl.BlockSpec((None, BLK, 2 * BLK), lambda g: (g, 0, 0))],
        out_specs=pl.BlockSpec((None, H, 2, BLK, 2 * BLK), lambda g: (g, 0, 0, 0, 0)),
        out_shape=jax.ShapeDtypeStruct((N_GROUPS, H, 2, BLK, 2 * BLK), F32),
        compiler_params=_cp(("parallel",)),
    )(rel_table, bkt)


PAIR = 2 * HEAD_DIM


def _bdot(a, b, ca, cb):
    return lax.dot_general(a, b, (((ca,), (cb,)), ((0,), (0,))), preferred_element_type=F32)


def _heads(ref, H, own_columns_only):
    low = lax.broadcasted_iota(jnp.int32, (BLK, PAIR), 1) < HEAD_DIM
    out = []
    for h in range(H):
        a = ref[:, (h // 2) * PAIR:(h // 2 + 1) * PAIR]
        if own_columns_only:
            a = jnp.where(low if h % 2 == 0 else jnp.logical_not(low), a, jnp.zeros_like(a))
        out.append(a)
    return jnp.stack(out)


def _pair_sums(o):
    return [o[2 * p] + o[2 * p + 1] for p in range(o.shape[0] // 2)]


def _store_pairs(ref, o, dtype, scale=None):
    for p, v in enumerate(_pair_sums(o)):
        ref[:, p * PAIR:(p + 1) * PAIR] = (v if scale is None else scale * v).astype(dtype)


def _stat_tile(s):
    lane = lax.broadcasted_iota(jnp.int32, (BLK, LANES), 1)
    t = jnp.zeros((BLK, LANES), F32)
    for h in range(s.shape[0]):
        t = jnp.where(lane == h, s[h], t)
    return t


def _stat_cols(ref, H):
    return jnp.stack([ref[:, h:h + 1] for h in range(H)])


def _scores(qm, kp, kc, bias):
    scale = HEAD_DIM ** -0.5
    sp = _bdot(qm, kp, 2, 2) * scale + bias[:, :, :BLK]
    sc = _bdot(qm, kc, 2, 2) * scale + bias[:, :, BLK:]
    return sp, sc


def _attn_specs(nb, dil, g, H, HD, qb):
    nbg = nb // dil
    cur = lambda col: pl.BlockSpec((BLK, HD), lambda b: (qb(b), col))
    prev = lambda col: pl.BlockSpec((BLK, HD), lambda b: (jnp.maximum(qb(b) - 1, 0), col))
    stat = pl.BlockSpec((BLK, LANES), lambda b: (qb(b), 0))
    bias = pl.BlockSpec((None, H, None, BLK, 2 * BLK),
                        lambda b: (g, 0, jnp.where(qb(b) % nbg == 0, 1, 0), 0, 0))
    return cur, prev, stat, bias


def _attn_fwd(q, k, v, bias, g, dil, *, name):
    (qa, qc), (ka, kc_), (va, vc_) = q, k, v
    T = qa.shape[0]
    H = bias.shape[1]
    HD = H * HEAD_DIM
    assert H % 2 == 0 and H <= LANES and T % (dil * BLK) == 0
    nb = T // BLK
    cur, prev, stat, bspec = _attn_specs(nb, dil, g, H, HD, lambda b: b)

    def body(q_ref, kp_ref, kc_ref, vp_ref, vc_ref, b_ref, o_ref, l_ref):
        qm, kp, kc = _heads(q_ref, H, True), _heads(kp_ref, H, False), _heads(kc_ref, H, False)
        sp, sc = _scores(qm, kp, kc, b_ref[...])
        m = jnp.maximum(jnp.max(sp, axis=-1, keepdims=True), jnp.max(sc, axis=-1, keepdims=True))
        pp, pc = jnp.exp(sp - m), jnp.exp(sc - m)
        den = jnp.sum(pp, axis=-1, keepdims=True) + jnp.sum(pc, axis=-1, keepdims=True)
        o = (_bdot((pp / den).astype(BF16), _heads(vp_ref, H, True), 2, 1)
             + _bdot((pc / den).astype(BF16), _heads(vc_ref, H, True), 2, 1))
        _store_pairs(o_ref, o, F32)
        l_ref[...] = _stat_tile(m + jnp.log(den))

    return pl.pallas_call(
        body, name=name, grid=(nb,),
        in_specs=[cur(qc), prev(kc_), cur(kc_), prev(vc_), cur(vc_), bspec],
        out_specs=[cur(0), stat],
        out_shape=[jax.ShapeDtypeStruct((T, HD), F32), jax.ShapeDtypeStruct((T, LANES), F32)],
        compiler_params=_cp(("parallel",)),
    )(qa, ka, ka, va, va, bias)


def _attn_combine(os, ls, *, name):
    n = len(os)
    T, HD = os[0].shape
    H = HD // HEAD_DIM
    tr = _tile(T, 512, SUBLANES)
    wide = pl.BlockSpec((tr, HD), lambda i: (i, 0))
    stat = pl.BlockSpec((tr, LANES), lambda i: (i, 0))

    def body(*refs):
        o_refs, l_refs = refs[:n], refs[n:2 * n]
        oc_ref, ocb_ref, lt_ref = refs[2 * n:]
        ls_ = [r[...] for r in l_refs]
        m = ls_[0]
        for l in ls_[1:]:
            m = jnp.maximum(m, l)
        ws = [jnp.exp(l - m) for l in ls_]
        den = ws[0]
        for w in ws[1:]:
            den = den + w
        lt_ref[...] = m + jnp.log(den)
        ws = [w / den for w in ws]
        low = lax.broadcasted_iota(jnp.int32, (tr, PAIR), 1) < HEAD_DIM
        for p in range(H // 2):
            cols = slice(p * PAIR, (p + 1) * PAIR)
            acc = jnp.zeros((tr, PAIR), F32)
            for g in range(n):
                w = jnp.where(low, ws[g][:, 2 * p:2 * p + 1], ws[g][:, 2 * p + 1:2 * p + 2])
                acc = acc + w * o_refs[g][:, cols]
            oc_ref[:, cols] = acc
            ocb_ref[:, cols] = acc.astype(BF16)

    return pl.pallas_call(
        body, name=name, grid=(T // tr,),
        in_specs=[wide] * n + [stat] * n, out_specs=[wide, wide, stat],
        out_shape=[jax.ShapeDtypeStruct((T, HD), F32), jax.ShapeDtypeStruct((T, HD), BF16),
                   jax.ShapeDtypeStruct((T, LANES), F32)],
        compiler_params=_cp(("parallel",)),
    )(*os, *ls)


def _attn_bwd_prep(do, oc, *, name):
    T, HD = do.shape
    H = HD // HEAD_DIM
    tr = _tile(T, 512, SUBLANES)
    wide = pl.BlockSpec((tr, HD), lambda i: (i, 0))
    stat = pl.BlockSpec((tr, LANES), lambda i: (i, 0))

    def body(do_ref, oc_ref, dl_ref, dob_ref):
        low = lax.broadcasted_iota(jnp.int32, (tr, PAIR), 1) < HEAD_DIM
        lane = lax.broadcasted_iota(jnp.int32, (tr, LANES), 1)
        dl = jnp.zeros((tr, LANES), F32)
        for p in range(H // 2):
            cols = slice(p * PAIR, (p + 1) * PAIR)
            d = do_ref[:, cols]
            prod = d * oc_ref[:, cols]
            dl = jnp.where(lane == 2 * p, jnp.sum(jnp.where(low, prod, 0.0), axis=-1, keepdims=True), dl)
            dl = jnp.where(lane == 2 * p + 1, jnp.sum(jnp.where(low, 0.0, prod), axis=-1, keepdims=True), dl)
            dob_ref[:, cols] = d.astype(BF16)
        dl_ref[...] = dl

    return pl.pallas_call(
        body, name=name, grid=(T // tr,),
        in_specs=[wide, wide], out_specs=[stat, wide],
        out_shape=[jax.ShapeDtypeStruct((T, LANES), F32), jax.ShapeDtypeStruct((T, HD), BF16)],
        compiler_params=_cp(("parallel",)),
    )(do, oc)


def _attn_bwd(q, k, v, bias, g, dil, do, lse, delta, *, name):
    (qa, qc), (ka, kc_), (va, vc_) = q, k, v
    T, HD = do.shape
    H = HD // HEAD_DIM
    nb = T // BLK
    scale = HEAD_DIM ** -0.5
    cur, prev, stat, bspec = _attn_specs(nb, dil, g, H, HD, lambda b: jnp.minimum(b, nb - 1))
    late = pl.BlockSpec((BLK, HD), lambda b: (jnp.maximum(b - 1, 0), 0))

    def body(q_ref, kp_ref, kc_ref, vp_ref, vc_ref, b_ref, do_ref, l_ref, dl_ref,
             dq_ref, dk_ref, dv_ref, db_ref, ck_ref, cv_ref):
        b = pl.program_id(0)

        @pl.when(b == 0)
        def _():
            db_ref[...] = jnp.zeros_like(db_ref)
            ck_ref[...] = jnp.zeros_like(ck_ref)
            cv_ref[...] = jnp.zeros_like(cv_ref)

        @pl.when(b < nb)
        def _():
            qm, dom = _heads(q_ref, H, True), _heads(do_ref, H, True)
            sp, sc = _scores(qm, _heads(kp_ref, H, False), _heads(kc_ref, H, False), b_ref[...])
            l, dl = _stat_cols(l_ref, H), _stat_cols(dl_ref, H)
            pp, pc = jnp.exp(sp - l), jnp.exp(sc - l)
            dsp = pp * (_bdot(dom, _heads(vp_ref, H, False), 2, 2) - dl)
            dsc = pc * (_bdot(dom, _heads(vc_ref, H, False), 2, 2) - dl)
            db_ref[:, :, :BLK] += dsp
            db_ref[:, :, BLK:] += dsc
            dspb, dscb = dsp.astype(BF16), dsc.astype(BF16)
            _store_pairs(dq_ref, _bdot(dspb, _heads(kp_ref, H, True), 2, 1)
                         + _bdot(dscb, _heads(kc_ref, H, True), 2, 1), BF16, scale)
            dkp = _pair_sums(_bdot(dspb, qm, 1, 1))
            dkc = _pair_sums(_bdot(dscb, qm, 1, 1))
            dvp = _pair_sums(_bdot(pp.astype(BF16), dom, 1, 1))
            dvc = _pair_sums(_bdot(pc.astype(BF16), dom, 1, 1))
            for p in range(H // 2):
                cols = slice(p * PAIR, (p + 1) * PAIR)
                dk_ref[:, cols] = ck_ref[:, cols] + scale * dkp[p]
                dv_ref[:, cols] = cv_ref[:, cols] + dvp[p]
                ck_ref[:, cols] = scale * dkc[p]
                cv_ref[:, cols] = dvc[p]

        @pl.when(b == nb)
        def _():
            dk_ref[...] = ck_ref[...]
            dv_ref[...] = cv_ref[...]

    f32 = jax.ShapeDtypeStruct((T, HD), F32)
    return pl.pallas_call(
        body, name=name, grid=(nb + 1,),
        in_specs=[cur(qc), prev(kc_), cur(kc_), prev(vc_), cur(vc_), bspec, cur(0), stat, stat],
        out_specs=[cur(0), late, late, pl.BlockSpec((H, BLK, 2 * BLK), lambda b: (0, 0, 0))],
        out_shape=[jax.ShapeDtypeStruct((T, HD), BF16), f32, f32,
                   jax.ShapeDtypeStruct((H, BLK, 2 * BLK), F32)],
        scratch_shapes=[pltpu.VMEM((BLK, HD), F32), pltpu.VMEM((BLK, HD), F32)],
        compiler_params=_cp(("arbitrary",)),
    )(qa, ka, ka, va, va, bias, do, lse, delta)


def _rel_grad(dbs, *, name):
    H = dbs[0].shape[1]
    n = len(dbs)
    assert H <= LANES
    bkt = jnp.asarray(_bucket_table())

    def body(*refs):
        bk = refs[n][...]
        o_ref = refs[n + 1]
        db = refs[0][...]
        for r in refs[1:n]:
            db = db + r[...]
        row = lax.broadcasted_iota(jnp.int32, (REL_BUCKETS, LANES), 0)
        col = lax.broadcasted_iota(jnp.int32, (REL_BUCKETS, LANES), 1)

        def bucket(c, acc):
            sel = bk == c
            for h in range(H):
                val = jnp.sum(jnp.where(sel, db[h], 0.0))
                acc = jnp.where((row == c) & (col == h), val, acc)
            return acc

        o_ref[...] = lax.fori_loop(0, REL_BUCKETS, bucket, jnp.zeros((REL_BUCKETS, LANES), F32))

    dspec = pl.BlockSpec((None, H, BLK, 2 * BLK), lambda g: (g, 0, 0, 0))
    return pl.pallas_call(
        body, name=name, grid=(N_GROUPS,),
        in_specs=[dspec] * n + [pl.BlockSpec((None, BLK, 2 * BLK), lambda g: (g, 0, 0))],
        out_specs=pl.BlockSpec((None, REL_BUCKETS, LANES), lambda g: (g, 0, 0)),
        out_shape=jax.ShapeDtypeStruct((N_GROUPS, REL_BUCKETS, LANES), F32),
        compiler_params=_cp(("parallel",)),
    )(*dbs, bkt)


def _to_group(a, d):
    T, C = a.shape
    return a if d == 1 else a.reshape(T // d, d, C).transpose(1, 0, 2).reshape(T, C)


def _from_group(a, d):
    T, C = a.shape
    return a if d == 1 else a.reshape(d, T // d, C).transpose(1, 0, 2).reshape(T, C)


def _rows_view(a):
    a2 = a.reshape(-1, a.shape[-1])
    R, C = a2.shape
    tr = _tile(R, max(SUBLANES, (1 << 18) // C), SUBLANES)
    return a2, R, C, tr


def _addn(xs, out_dtype, *, name):
    shape = xs[0].shape
    x2s = [_rows_view(x)[0] for x in xs]
    _, R, C, tr = _rows_view(xs[0])
    spec = pl.BlockSpec((tr, C), lambda i: (i, 0))

    def body(*refs):
        acc = refs[0][...].astype(F32)
        for r in refs[1:-1]:
            acc = acc + r[...].astype(F32)
        refs[-1][...] = acc.astype(out_dtype)

    out = pl.pallas_call(
        body, name=name, grid=(R // tr,),
        in_specs=[spec] * len(xs), out_specs=spec,
        out_shape=jax.ShapeDtypeStruct((R, C), out_dtype),
        compiler_params=_cp(("parallel",)),
    )(*x2s)
    return out.reshape(shape)


def _slot_sum(b_ref, own, my_id):
    acc = None
    for s in range(b_ref.shape[0]):
        term = jnp.where(my_id == s, own, b_ref[s].astype(F32))
        acc = term if acc is None else acc + term
    return acc


def _part_geometry(shape, axis):
    _, R, C = shape
    if axis == COL:
        return R // 2, C // N_CHIPS
    return R // (2 * N_CHIPS), C


def _sum_parts(buf, grad, axis, where, *, name):
    n, L, Rp, Cp = buf.shape
    assert (Rp, Cp) == _part_geometry(grad.shape, axis)
    tr = _tile(Rp, max(2 * SUBLANES, (1 << 17) // Cp), 2 * SUBLANES)
    nr = Rp // tr

    def own_map(l, i, w):
        if axis == COL:
            return (l, w[1] * nr + i, w[0])
        return (l, (2 * w[0] + w[1]) * nr + i, 0)

    def body(w_ref, b_ref, g_ref, o_ref):
        o_ref[...] = _slot_sum(b_ref, g_ref[...].astype(F32), w_ref[2])

    return pl.pallas_call(
        body, name=name,
        grid_spec=pltpu.PrefetchScalarGridSpec(
            num_scalar_prefetch=1, grid=(L, nr),
            in_specs=[pl.BlockSpec((n, None, tr, Cp), lambda l, i, w: (0, l, i, 0)),
                      pl.BlockSpec((None, tr, Cp), own_map)],
            out_specs=pl.BlockSpec((None, None, tr, Cp), lambda l, i, w: (l, w[1], i, 0))),
        out_shape=jax.ShapeDtypeStruct((L, 2, Rp, Cp), F32),
        compiler_params=_cp(("parallel", "parallel")),
    )(where, buf, grad)


def _sum_small(buf, own, where, *, name):
    n, R, C = buf.shape
    tr = _tile(R, 512, SUBLANES)

    def body(w_ref, b_ref, g_ref, o_ref):
        o_ref[...] = _slot_sum(b_ref, g_ref[...], w_ref[2])

    return pl.pallas_call(
        body, name=name,
        grid_spec=pltpu.PrefetchScalarGridSpec(
            num_scalar_prefetch=1, grid=(R // tr,),
            in_specs=[pl.BlockSpec((n, tr, C), lambda i, w: (0, i, 0)),
                      pl.BlockSpec((tr, C), lambda i, w: (i, 0))],
            out_specs=pl.BlockSpec((tr, C), lambda i, w: (i, 0))),
        out_shape=jax.ShapeDtypeStruct((R, C), F32),
        compiler_params=_cp(("parallel",)),
    )(where, buf, own)


def _place_shard(shard, layer, axis, where, out_dtype, *, name, after=None):
    _, R, C = shard.shape
    tr = _tile(R, max(2 * SUBLANES, (1 << 18) // C), 2 * SUBLANES)
    nr = R // tr
    if axis == COL:
        full, out_map = (1, R, C * N_CHIPS), (lambda i, w: (0, i, w[0]))
    else:
        full, out_map = (1, R * N_CHIPS, C), (lambda i, w: (0, w[0] * nr + i, 0))

    def body(w_ref, x_ref, *rest):
        rest[-1][...] = x_ref[...].astype(out_dtype)

    extra = [] if after is None else [after]
    return pl.pallas_call(
        body, name=name,
        grid_spec=pltpu.PrefetchScalarGridSpec(
            num_scalar_prefetch=1, grid=(nr,),
            in_specs=[pl.BlockSpec((None, tr, C), lambda i, w: (layer, i, 0))]
            + [pl.BlockSpec(memory_space=pl.ANY)] * len(extra),
            out_specs=pl.BlockSpec((None, tr, C), out_map)),
        out_shape=jax.ShapeDtypeStruct(full, out_dtype),
        compiler_params=_cp(("parallel",)),
    )(where, shard, *extra)


def _loss_grad(y, tgt, *, name):
    T, D = y.shape
    tr = _tile(T, 512, SUBLANES)
    row = pl.BlockSpec((tr, D), lambda i: (i, 0))

    def body(y_ref, t_ref, dy_ref, l_ref, acc_ref):
        i = pl.program_id(0)

        @pl.when(i == 0)
        def _():
            acc_ref[...] = jnp.zeros_like(acc_ref)

        err = y_ref[...] - t_ref[...]
        dy_ref[...] = err * (1.0 / D)
        acc_ref[...] += jnp.sum(err * err, axis=0, keepdims=True)

        @pl.when(i == T // tr - 1)
        def _():
            tot = jnp.sum(acc_ref[...], axis=1, keepdims=True) * (0.5 / D)
            l_ref[...] = jnp.broadcast_to(tot, (1, LANES))

    return pl.pallas_call(
        body, name=name, grid=(T // tr,),
        in_specs=[row, row],
        out_specs=[row, pl.BlockSpec((1, LANES), lambda i: (0, 0))],
        out_shape=[jax.ShapeDtypeStruct((T, D), F32), jax.ShapeDtypeStruct((1, LANES), F32)],
        scratch_shapes=[pltpu.VMEM((1, D), F32)],
        compiler_params=_cp(("arbitrary",)),
    )(y, tgt)


def _adamw(w, g, m, v, *, name):
    shape = w.shape
    w2, R, C, tr = _rows_view(w)
    spec = pl.BlockSpec((tr, C), lambda i: (i, 0))

    def body(w_ref, g_ref, m_ref, v_ref, d_ref, nm_ref, nv_ref):
        gv = g_ref[...]
        nm = ADAM_B1 * m_ref[...] + (1.0 - ADAM_B1) * gv
        nv = ADAM_B2 * v_ref[...] + (1.0 - ADAM_B2) * (gv * gv)
        m_hat = nm / (1.0 - ADAM_B1 ** ADAM_STEP)
        v_hat = nv / (1.0 - ADAM_B2 ** ADAM_STEP)
        d_ref[...] = -ADAM_LR * (m_hat / (jnp.sqrt(v_hat) + ADAM_EPS) + ADAM_WD * w_ref[...])
        nm_ref[...] = nm
        nv_ref[...] = nv

    shp = jax.ShapeDtypeStruct((R, C), F32)
    outs = pl.pallas_call(
        body, name=name, grid=(R // tr,),
        in_specs=[spec] * 4, out_specs=[spec] * 3, out_shape=[shp] * 3,
        compiler_params=_cp(("parallel",)),
    )(w2, g.reshape(R, C), m.reshape(R, C), v.reshape(R, C))
    return tuple(o.reshape(shape) for o in outs)


def _pack(arrs):
    flat = jnp.concatenate([a.reshape(-1).astype(F32) for a in arrs])
    unit = SUBLANES * LANES
    pad = (-flat.shape[0]) % unit
    return jnp.pad(flat, (0, pad)).reshape(-1, LANES)


def _unpack(buf, shapes):
    flat = buf.reshape(-1)
    out, off = [], 0
    for s in shapes:
        n = int(np.prod(s))
        out.append(flat[off:off + n].reshape(s))
        off += n
    return out


def _me():
    return lax.axis_index("x"), lax.axis_index("y"), lax.axis_index("c")


def _flip(pos, k):
    x, y, c = pos
    return (1 - x if k & 4 else x, 1 - y if k & 2 else y, 1 - c if k & 1 else c)


HBM = pl.BlockSpec(memory_space=pltpu.HBM)

COL, ROW = -1, -2


def _shard_window(ref, axis, j, n):
    start = pl.multiple_of(j * n, n)
    if axis == COL:
        return ref.at[:, :, pl.ds(start, n)]
    return ref.at[:, pl.ds(start, n), :]


GATHER_PEERS = (2, 4, 6)
SEM = pl.BlockSpec(memory_space=pltpu.SEMAPHORE)
EFFECT = pltpu.SideEffectType.DATAFLOW_SIDE_EFFECTING


def _gather_copy(buf, axis, i, pi, chip, me, send_sems, recv_sems):
    win = _shard_window(buf, axis, chip, buf.shape[axis] // N_CHIPS)
    k = len(GATHER_PEERS) * i + pi
    return pltpu.make_async_remote_copy(
        src_ref=win, dst_ref=win, send_sem=send_sems.at[k], recv_sem=recv_sems.at[k],
        device_id=_flip(me, GATHER_PEERS[pi]), device_id_type=MESH)


def _gather_start(fulls, axes, *, name):
    n = len(fulls)
    n_sem = len(GATHER_PEERS) * n

    def body(*refs):
        send_sems, recv_sems = refs[n], refs[n + 1]
        bufs = refs[n + 2:2 * n + 2]
        token = refs[2 * n + 2]
        me = _me()
        for i in range(n):
            for pi in range(len(GATHER_PEERS)):
                _gather_copy(bufs[i], axes[i], i, pi, 2 * me[0] + me[1], me, send_sems, recv_sems).start()
        token[...] = jnp.zeros_like(token)

    outs = pl.pallas_call(
        body, name=name,
        in_specs=[HBM] * n, out_specs=[SEM, SEM] + [HBM] * n + [pl.BlockSpec(memory_space=pltpu.VMEM)],
        out_shape=[pltpu.SemaphoreType.DMA((n_sem,)), pltpu.SemaphoreType.DMA((n_sem,))]
        + [pltpu.HBM(f.shape, f.dtype) for f in fulls] + [jax.ShapeDtypeStruct((SUBLANES, LANES), F32)],
        input_output_aliases={i: 2 + i for i in range(n)},
        compiler_params=pltpu.CompilerParams(has_side_effects=EFFECT),
    )(*[pltpu.with_memory_space_constraint(f, pltpu.HBM) for f in fulls])
    return outs[0], outs[1], list(outs[2:2 + n]), outs[-1]


def _gather_wait(send_sems, recv_sems, bufs, axes, idxs, after, *, name):
    m = len(bufs)

    def body(*refs):
        ss, rs = refs[m], refs[m + 1]
        outs = refs[m + 3:]
        me = _me()
        for t, i in enumerate(idxs):
            for pi, k in enumerate(GATHER_PEERS):
                px, py, _ = _flip(me, k)
                _gather_copy(outs[t], axes[t], i, pi, 2 * me[0] + me[1], me, ss, rs).wait_send()
                _gather_copy(outs[t], axes[t], i, pi, 2 * px + py, me, ss, rs).wait_recv()

    return pl.pallas_call(
        body, name=name,
        in_specs=[HBM] * m + [SEM, SEM, pl.BlockSpec(memory_space=pl.ANY)], out_specs=[HBM] * m,
        out_shape=[pltpu.HBM(b.shape, b.dtype) for b in bufs],
        input_output_aliases={t: t for t in range(m)},
        compiler_params=pltpu.CompilerParams(has_side_effects=EFFECT),
    )(*bufs, send_sems, recv_sems, after)


def _grad_part(ref, layer, axis, pos):
    px, py, pc = pos
    Rp, Cp = _part_geometry(ref.shape, axis)
    chip = 2 * px + py
    if axis == COL:
        return ref.at[layer, pl.ds(pl.multiple_of(pc * Rp, Rp), Rp), pl.ds(pl.multiple_of(chip * Cp, Cp), Cp)]
    return ref.at[layer, pl.ds(pl.multiple_of((2 * chip + pc) * Rp, Rp), Rp), :]


def _scatter_copy(grad, slots, layer, axis, i, k, src_pos, dst_slot, me, send_sems, recv_sems):
    sem = (N_DEV - 1) * i + k - 1
    return pltpu.make_async_remote_copy(
        src_ref=_grad_part(grad, layer, axis, src_pos), dst_ref=slots.at[dst_slot, layer],
        send_sem=send_sems.at[sem], recv_sem=recv_sems.at[sem],
        device_id=_flip(me, k), device_id_type=MESH)


def _scatter_start(grads, slots, layers, axes, *, name):
    n = len(grads)
    n_sem = (N_DEV - 1) * n

    def body(*refs):
        send_sems, recv_sems = refs[2 * n], refs[2 * n + 1]
        g_refs, s_refs = refs[2 * n + 2:3 * n + 2], refs[3 * n + 2:4 * n + 2]
        token = refs[4 * n + 2]
        me = _me()
        my_id = 4 * me[0] + 2 * me[1] + me[2]
        for i in range(n):
            for k in range(1, N_DEV):
                _scatter_copy(g_refs[i], s_refs[i], layers[i], axes[i], i, k, _flip(me, k), my_id, me,
                              send_sems, recv_sems).start()
        token[...] = jnp.zeros_like(token)

    arrays = list(grads) + list(slots)
    outs = pl.pallas_call(
        body, name=name,
        in_specs=[HBM] * (2 * n),
        out_specs=[SEM, SEM] + [HBM] * (2 * n) + [pl.BlockSpec(memory_space=pltpu.VMEM)],
        out_shape=[pltpu.SemaphoreType.DMA((n_sem,)), pltpu.SemaphoreType.DMA((n_sem,))]
        + [pltpu.HBM(a.shape, a.dtype) for a in arrays] + [jax.ShapeDtypeStruct((SUBLANES, LANES), F32)],
        input_output_aliases={i: 2 + i for i in range(2 * n)},
        compiler_params=pltpu.CompilerParams(has_side_effects=EFFECT),
    )(*[pltpu.with_memory_space_constraint(a, pltpu.HBM) for a in arrays])
    return outs[0], outs[1], list(outs[2:2 + n]), list(outs[2 + n:2 + 2 * n]), outs[-1]


def _scatter_wait(sems, grads, slots, axes, plan, *, name):
    n = len(grads)
    flat_sems = [s for pair in sems for s in pair]

    def body(*refs):
        sem_refs = refs[2 * n:2 * n + len(flat_sems)]
        g_refs, s_refs = refs[2 * n + len(flat_sems):3 * n + len(flat_sems)], refs[3 * n + len(flat_sems):]
        me = _me()
        for j, items in enumerate(plan):
            ss, rs = sem_refs[2 * j], sem_refs[2 * j + 1]
            for i, (a, layer) in enumerate(items):
                for k in range(1, N_DEV):
                    px, py, pc = _flip(me, k)
                    cp = _scatter_copy(g_refs[a], s_refs[a], layer, axes[a], i, k, me, 4 * px + 2 * py + pc, me, ss, rs)
                    cp.wait_send()
                    cp.wait_recv()

    arrays = list(grads) + list(slots)
    outs = pl.pallas_call(
        body, name=name,
        in_specs=[HBM] * (2 * n) + [SEM] * len(flat_sems), out_specs=[HBM] * (2 * n),
        out_shape=[pltpu.HBM(a.shape, a.dtype) for a in arrays],
        input_output_aliases={i: i for i in range(2 * n)},
        compiler_params=pltpu.CompilerParams(has_side_effects=EFFECT),
    )(*arrays, *flat_sems)
    return list(outs[:n]), list(outs[n:])


def _small_copy(src, dst, k, slot, me, send_sems, recv_sems):
    return pltpu.make_async_remote_copy(
        src_ref=src, dst_ref=dst.at[slot], send_sem=send_sems.at[k - 1], recv_sem=recv_sems.at[k - 1],
        device_id=_flip(me, k), device_id_type=MESH)


def _small_start(small, slots, *, name):
    def body(src_in, dst_in, send_sems, recv_sems, src, dst):
        me = _me()
        for k in range(1, N_DEV):
            _small_copy(src, dst, k, 4 * me[0] + 2 * me[1] + me[2], me, send_sems, recv_sems).start()

    arrays = [small, slots]
    return pl.pallas_call(
        body, name=name, in_specs=[HBM, HBM], out_specs=[SEM, SEM, HBM, HBM],
        out_shape=[pltpu.SemaphoreType.DMA((N_DEV - 1,)), pltpu.SemaphoreType.DMA((N_DEV - 1,))]
        + [pltpu.HBM(a.shape, a.dtype) for a in arrays],
        input_output_aliases={0: 2, 1: 3},
        compiler_params=pltpu.CompilerParams(has_side_effects=EFFECT),
    )(*[pltpu.with_memory_space_constraint(a, pltpu.HBM) for a in arrays])


def _small_wait(send_sems, recv_sems, small, slots, after, *, name):
    def body(src_in, dst_in, ss, rs, after_ref, src, dst):
        me = _me()
        for k in range(1, N_DEV):
            px, py, pc = _flip(me, k)
            cp = _small_copy(src, dst, k, 4 * px + 2 * py + pc, me, ss, rs)
            cp.wait_send()
            cp.wait_recv()

    return pl.pallas_call(
        body, name=name, in_specs=[HBM, HBM, SEM, SEM, pl.BlockSpec(memory_space=pl.ANY)], out_specs=[HBM, HBM],
        out_shape=[pltpu.HBM(small.shape, small.dtype), pltpu.HBM(slots.shape, slots.dtype)],
        input_output_aliases={0: 0, 1: 1},
        compiler_params=pltpu.CompilerParams(has_side_effects=EFFECT),
    )(small, slots, send_sems, recv_sems, after)


def _pair_gather(halves, *, name):
    n = len(halves)

    def body(*refs):
        bufs = refs[n:2 * n]
        send_sems, recv_sems = refs[2 * n:]
        me = _me()
        c = me[2]
        sib = _flip(me, 1)

        def copy(i, half):
            return pltpu.make_async_remote_copy(
                src_ref=bufs[i].at[:, half], dst_ref=bufs[i].at[:, half], send_sem=send_sems.at[i],
                recv_sem=recv_sems.at[i], device_id=sib, device_id_type=MESH)

        sends = [copy(i, c) for i in range(n)]
        for cp in sends:
            cp.start()
        for i in range(n):
            copy(i, 1 - c).wait_recv()
        for cp in sends:
            cp.wait_send()

    return pl.pallas_call(
        body, name=name,
        in_specs=[HBM] * n, out_specs=[HBM] * n,
        out_shape=[jax.ShapeDtypeStruct(h.shape, h.dtype) for h in halves],
        input_output_aliases={i: i for i in range(n)},
        scratch_shapes=[pltpu.SemaphoreType.DMA((n,)), pltpu.SemaphoreType.DMA((n,))],
    )(*halves)


def _local_step(x, tgt, P, fetch, ship, n_layers):
    T, D = x.shape
    P = dict(P)
    W = {}
    depth = P["ffn_conv_b"].shape[0]
    n_a = P["a_w_s"].shape[0]
    alpha = (2 * depth) ** 0.25
    H = P["rel_table"].shape[1] // N_GROUPS
    HD = H * HEAD_DIM
    dils = [d for _, d in DILATED_GROUPS]
    assert T % (dils[-1] * BLK) == 0

    def arrive(stage, after):
        w_new, p_new = fetch(stage, after)
        W.update(w_new)
        P.update(p_new)

    bias = _band_bias(P["rel_table"], H, name="band_bias")
    saved = []
    xf, xb = x, x.astype(BF16)
    kg = vg = None
    for i in range(depth):
        s = {"x": xf, "xb": xb}
        arrive(f"{i}a", xf)
        if i < n_a:
            s["zp"] = _mm(xb, W["a_w_in", i], name=f"a{i}_in")
            arrive(f"{i}b", s["zp"])
            s["y"] = _sgu_fwd(s["zp"], P["a_w_s"][i], P["a_b_s"][i], P["a_ln_g"][i], P["a_ln_b"][i], name=f"a{i}_sgu")
            s["h"] = _mm(s["y"], W["a_w_out", i], name=f"a{i}_out")
        else:
            j = i - n_a
            if j == 0:
                kv = _mm(xb, W["kv_w", 0], out_dtype=BF16, name="kv_proj")
                kg = [(kv, 0) if d == 1 else (_to_group(kv[:, :HD], d), 0) for d in dils]
                vg = [(kv, 1) if d == 1 else (_to_group(kv[:, HD:], d), 0) for d in dils]
            q = _mm(xb, W["b_w_q", j], out_dtype=BF16, name=f"b{j}_q")
            s["qg"] = [(q, g) if d == 1 else (_to_group(q[:, g * HD:(g + 1) * HD], d), 0) for g, d in enumerate(dils)]
            os, ls = [], []
            for g, d in enumerate(dils):
                o_g, l_g = _attn_fwd(s["qg"][g], kg[g], vg[g], bias, g, d, name=f"b{j}_attn{g}")
                os.append(_from_group(o_g, d))
                ls.append(_from_group(l_g, d))
            s["oc"], s["ob"], s["lse"] = _attn_combine(os, ls, name=f"b{j}_comb")
            arrive(f"{i}b", s["ob"])
            s["h"] = _mm(s["ob"], W["b_w_o", j], name=f"b{j}_o")
        s["x1"], s["x1b"] = _add_ln(xf, s["h"], P["ln_g"][i, 0], P["ln_b"][i, 0], alpha, name=f"l{i}_ln1")
        arrive(f"{i}c", s["x1b"])
        s["hup"] = _mm(s["x1b"], W["ffn_w_up", i], name=f"l{i}_up")
        arrive(f"{i}d", s["hup"])
        s["act"] = _convglu_fwd(s["hup"], P["ffn_conv_w"][i], P["ffn_conv_b"][i], name=f"l{i}_glu")
        s["f"] = _mm(s["act"], W["ffn_w_down", i], name=f"l{i}_down")
        xf, xb = _add_ln(s["x1"], s["f"], P["ln_g"][i, 1], P["ln_b"][i, 1], alpha, name=f"l{i}_ln2")
        saved.append(s)

    G, loss_row = _loss_grad(xf, tgt, name="loss")

    gw = {}

    def dw(key, layer, a, b, name, **kw):
        gw[key] = _mm(a, b, ta=True, out_dtype=BF16, out_into=(gw.get(key), n_layers[key], layer), name=name, **kw)

    gp = {k: [None] * n_a for k in ("a_ln_g", "a_ln_b", "a_w_s", "a_b_s")}
    gp.update({k: [None] * depth for k in ("ffn_conv_w", "ffn_conv_b", "ln_g", "ln_b")})
    dk_parts, dv_parts, dbias_parts = [], [], []
    for i in reversed(range(depth)):
        s = saved[i]
        dr2, dr2b, dg2, db2 = _ln_bwd(G, s["x1"], s["f"], P["ln_g"][i, 1], alpha, name=f"l{i}_ln2_bwd")
        dw("ffn_w_down", i, s["act"], dr2b, f"l{i}_down_dw")
        dact = _mm(dr2b, W["ffn_w_down", i], tb=True, name=f"l{i}_down_dx")
        dhup, dcw, dcb = _convglu_bwd(s["hup"], dact, P["ffn_conv_w"][i], P["ffn_conv_b"][i], name=f"l{i}_glu_bwd")
        gp["ffn_conv_w"][i] = dcw.transpose(1, 0, 2).reshape(dcw.shape[1], -1)
        gp["ffn_conv_b"][i] = dcb.reshape(-1)
        dw("ffn_w_up", i, s["x1b"], dhup, f"l{i}_up_dw", b_halves=True)
        token = ship(gw, [("ffn_w_down", i), ("ffn_w_up", i)])
        G1 = _mm(dhup, W["ffn_w_up", i], tb=True, a_halves=True, add=dr2, add_scale=alpha, after=token,
                 name=f"l{i}_up_dx")
        dr1, dr1b, dg1, db1 = _ln_bwd(G1, s["x"], s["h"], P["ln_g"][i, 0], alpha, name=f"l{i}_ln1_bwd")
        gp["ln_g"][i] = jnp.concatenate([dg1, dg2], axis=0)
        gp["ln_b"][i] = jnp.concatenate([db1, db2], axis=0)
        if i < n_a:
            dw("a_w_out", i, s["y"], dr1b, f"a{i}_out_dw")
            token = ship(gw, [("a_w_out", i)])
            dy = _mm(dr1b, W["a_w_out", i], tb=True, after=token, name=f"a{i}_out_dx")
            dzp, dws, dbs, dlg, dlb = _sgu_bwd(s["zp"], dy, P["a_w_s"][i], P["a_b_s"][i], P["a_ln_g"][i],
                                               P["a_ln_b"][i], name=f"a{i}_sgu_bwd")
            gp["a_w_s"][i], gp["a_b_s"][i] = dws, dbs[:, :dws.shape[0]].T
            gp["a_ln_g"][i], gp["a_ln_b"][i] = dlg[0], dlb[0]
            dw("a_w_in", i, s["xb"], dzp, f"a{i}_in_dw")
            token = ship(gw, [("a_w_in", i)])
            G = _mm(dzp, W["a_w_in", i], tb=True, add=dr1, add_scale=alpha, after=token, name=f"a{i}_in_dx")
        else:
            j = i - n_a
            dw("b_w_o", j, s["ob"], dr1b, f"b{j}_o_dw")
            do_tok = _mm(dr1b, W["b_w_o", j], tb=True, name=f"b{j}_o_dx")
            delta, dob = _attn_bwd_prep(do_tok, s["oc"], name=f"b{j}_prep")
            dqs, dbs = [], []
            for g, d in enumerate(dils):
                dq, dk, dv, db = _attn_bwd(s["qg"][g], kg[g], vg[g], bias, g, d, _to_group(dob, d),
                                           _to_group(s["lse"], d), _to_group(delta, d), name=f"b{j}_attn{g}_bwd")
                dqs.append(_from_group(dq, d))
                dk_parts.append(_from_group(dk, d))
                dv_parts.append(_from_group(dv, d))
                dbs.append(db)
            dbias_parts.append(jnp.stack(dbs))
            dq_tok = jnp.concatenate(dqs, axis=1)
            dw("b_w_q", j, s["xb"], dq_tok, f"b{j}_q_dw")
            token = ship(gw, [("b_w_o", j), ("b_w_q", j)])
            G = _mm(dq_tok, W["b_w_q", j], tb=True, add=dr1, add_scale=alpha, after=token, name=f"b{j}_q_dx")
            if j == 0:
                dkv = jnp.concatenate([_addn(dk_parts, BF16, name="dk_sum"), _addn(dv_parts, BF16, name="dv_sum")], axis=1)
                dw("kv_w", 0, s["xb"], dkv, "kv_dw")
                token = ship(gw, [("kv_w", 0)])
                G = _mm(dkv, W["kv_w", 0], tb=True, add=G, add_scale=1.0, after=token, name="kv_dx")
    rel = _rel_grad(dbias_parts, name="rel_grad")
    grel = rel[:, :, :H].transpose(1, 0, 2).reshape(REL_BUCKETS, N_GROUPS * H)
    gsmall = {k: jnp.stack(v) for k, v in gp.items()}
    gsmall["rel_table"] = grel
    return loss_row, G, gw, gsmall


BIG = (("a_w_in", COL), ("a_w_out", ROW), ("kv_w", ROW), ("b_w_q", COL), ("b_w_o", COL),
       ("ffn_w_up", COL), ("ffn_w_down", ROW))
SMALL_SHARDED = ("a_ln_g", "a_ln_b", "ffn_conv_w", "ln_g", "ln_b")
SMALL_REPLICATED = ("a_w_s", "a_b_s", "rel_table", "ffn_conv_b")
WEIGHTS = ("a_w_in", "a_ln_g", "a_ln_b", "a_w_s", "a_b_s", "a_w_out", "kv_w", "b_w_q", "b_w_o", "rel_table",
           "ffn_w_up", "ffn_conv_w", "ffn_conv_b", "ffn_w_down", "ln_g", "ln_b")


def kernel(x, a_w_in, a_ln_g, a_ln_b, a_w_s, a_b_s, a_w_out, kv_w, b_w_q, b_w_o, rel_table, ffn_w_up, ffn_conv_w, ffn_conv_b, ffn_w_down, ln_g, ln_b, loss_target, m_a_w_in, m_a_ln_g, m_a_ln_b, m_a_w_s, m_a_b_s, m_a_w_out, m_kv_w, m_b_w_q, m_b_w_o, m_rel_table, m_ffn_w_up, m_ffn_conv_w, m_ffn_conv_b, m_ffn_w_down, m_ln_g, m_ln_b, v_a_w_in, v_a_ln_g, v_a_ln_b, v_a_w_s, v_a_b_s, v_a_w_out, v_kv_w, v_b_w_q, v_b_w_o, v_rel_table, v_ffn_w_up, v_ffn_conv_w, v_ffn_conv_b, v_ffn_w_down, v_ln_g, v_ln_b):
    w = dict(a_w_in=a_w_in, a_ln_g=a_ln_g, a_ln_b=a_ln_b, a_w_s=a_w_s, a_b_s=a_b_s, a_w_out=a_w_out, kv_w=kv_w,
             b_w_q=b_w_q, b_w_o=b_w_o, rel_table=rel_table, ffn_w_up=ffn_w_up, ffn_conv_w=ffn_conv_w,
             ffn_conv_b=ffn_conv_b, ffn_w_down=ffn_w_down, ln_g=ln_g, ln_b=ln_b)
    m = dict(a_w_in=m_a_w_in, a_ln_g=m_a_ln_g, a_ln_b=m_a_ln_b, a_w_s=m_a_w_s, a_b_s=m_a_b_s, a_w_out=m_a_w_out,
             kv_w=m_kv_w, b_w_q=m_b_w_q, b_w_o=m_b_w_o, rel_table=m_rel_table, ffn_w_up=m_ffn_w_up,
             ffn_conv_w=m_ffn_conv_w, ffn_conv_b=m_ffn_conv_b, ffn_w_down=m_ffn_w_down, ln_g=m_ln_g, ln_b=m_ln_b)
    v = dict(a_w_in=v_a_w_in, a_ln_g=v_a_ln_g, a_ln_b=v_a_ln_b, a_w_s=v_a_w_s, a_b_s=v_a_b_s, a_w_out=v_a_w_out,
             kv_w=v_kv_w, b_w_q=v_b_w_q, b_w_o=v_b_w_o, rel_table=v_rel_table, ffn_w_up=v_ffn_w_up,
             ffn_conv_w=v_ffn_conv_w, ffn_conv_b=v_ffn_conv_b, ffn_w_down=v_ffn_w_down, ln_g=v_ln_g, ln_b=v_ln_b)
    chip = 2 * lax.axis_index("x") + lax.axis_index("y")

    big_names = [n for n, _ in BIG]
    big_axes = [a for _, a in BIG]
    where = jnp.stack([chip, lax.axis_index("c"), 2 * chip + lax.axis_index("c")]).astype(jnp.int32)
    shards = [w[n].reshape((1,) + w[n].shape) if w[n].ndim == 2 else w[n] for n in big_names]
    small_shapes = [w[n].shape for n in SMALL_SHARDED]
    small = _pack([w[n] for n in SMALL_SHARDED])
    shard_of = dict(zip(big_names, shards))
    axis_of = dict(BIG)
    n_layers = {n: s.shape[0] for n, s in shard_of.items()}
    n_a = w["a_w_s"].shape[0]
    stages = {}
    for i in range(w["ffn_w_up"].shape[0]):
        if i < n_a:
            stages[f"{i}a"], stages[f"{i}b"] = [("a_w_in", i)], [("a_w_out", i)]
        else:
            stages[f"{i}a"] = ([("kv_w", 0)] if i == n_a else []) + [("b_w_q", i - n_a)]
            stages[f"{i}b"] = [("b_w_o", i - n_a)]
        stages[f"{i}c"], stages[f"{i}d"] = [("ffn_w_up", i)], [("ffn_w_down", i)]
    stages["0a"].append(("small", 0))
    assert sum(len(st) for st in stages.values()) == sum(n_layers.values()) + 1
    flights, full_shape, all_started = {}, {}, None
    for batch, names in enumerate(([st for st in stages if st[0] == "0"], [st for st in stages if st[0] != "0"])):
        order = [key for st in names for key in stages[st]]
        placed, axes = [], []
        for n, l in order:
            if n == "small":
                placed.append(_place_shard(small[None], 0, ROW, where, F32, name="place_small"))
                axes.append(ROW)
            else:
                placed.append(_place_shard(shard_of[n], l, axis_of[n], where, BF16, name=f"place_{n}{l}",
                                           after=all_started))
                axes.append(axis_of[n])
                full_shape[n] = (n_layers[n],) + placed[-1].shape[1:]
        send_sems, recv_sems, flying, all_started = _gather_start(placed, axes, name=f"gather_start{batch}")
        for st in names:
            flights[st] = (send_sems, recv_sems, flying, axes, order)

    def fetch(stage, after):
        if stage not in stages:
            return {}, {}
        send_sems, recv_sems, flying, axes, order = flights[stage]
        if stage == "0a":
            after = all_started
        idxs = [order.index(key) for key in stages[stage]]
        landed = _gather_wait(send_sems, recv_sems, [flying[i] for i in idxs], [axes[i] for i in idxs], idxs, after,
                              name=f"gather_wait_{stage}")
        W_new, P_new = {}, {}
        for key, arr in zip(stages[stage], landed):
            if key[0] == "small":
                small_all = arr.reshape((N_CHIPS,) + small.shape)
                per_chip = [_unpack(small_all[j], small_shapes) for j in range(N_CHIPS)]
                for i, n in enumerate(SMALL_SHARDED):
                    P_new[n] = jnp.concatenate([per_chip[j][i] for j in range(N_CHIPS)], axis=-1)
            else:
                W_new[key] = arr.reshape(arr.shape[1:])
        return W_new, P_new

    slots ={n: lax.empty((N_DEV, n_layers[n]) + _part_geometry(full_shape[n], axis_of[n]), BF16) for n in big_names}
    sems, plan = [], []

    def ship(gw, items):
        names = [n for n, _ in items]
        send, recv, grads, landing, token = _scatter_start(
            [gw[n] for n in names], [slots[n] for n in names], [l for _, l in items], [axis_of[n] for n in names],
            name="scatter_start_" + "_".join(f"{n}{l}" for n, l in items))
        for n, g, s in zip(names, grads, landing):
            gw[n], slots[n] = g, s
        sems.append((send, recv))
        plan.append([(big_names.index(n), l) for n, l in items])
        return token

    loss_row, grad_x, gw, gsmall = _local_step(x[0], loss_target[0], {n: w[n] for n in SMALL_REPLICATED}, fetch,
                                               ship, n_layers)
    loss = lax.psum(loss_row[0, 0], ("x", "y", "c"))

    small_names = list(SMALL_SHARDED) + list(SMALL_REPLICATED)
    small_pack = _pack([gsmall[n] for n in small_names])
    small_sems = _small_start(small_pack, lax.empty((N_DEV,) + small_pack.shape, F32), name="small_start")
    own, landed = _scatter_wait(sems, [gw[n] for n in big_names], [slots[n] for n in big_names], big_axes, plan,
                                name="scatter_wait")
    sums = [_sum_parts(b, g, ax, where, name=f"sum_{n}") for (n, ax), b, g in zip(BIG, landed, own)]
    pairs = _pair_gather(sums, name="pair_gather")
    grad = {n: p.reshape(w[n].shape) for n, p in zip(big_names, pairs)}

    delta, new_m, new_v = {}, {}, {}
    for n, _ in BIG:
        delta[n], new_m[n], new_v[n] = _adamw(w[n], grad[n], m[n], v[n], name=f"adamw_{n}")
    small_own, small_slots = _small_wait(*small_sems, new_v[big_names[-1]], name="small_wait")
    small_sum = _sum_small(small_slots, small_own, where, name="sum_small")
    full_small = _unpack(small_sum, [gsmall[n].shape for n in small_names])
    for n, gfull in zip(small_names, full_small):
        if n in SMALL_SHARDED:
            width = w[n].shape[-1]
            grad[n] = lax.dynamic_slice_in_dim(gfull, chip * width, width, axis=-1)
        else:
            grad[n] = gfull
    shapes = [w[n].shape for n in small_names]
    packed = _adamw(_pack([w[n] for n in small_names]), _pack([grad[n] for n in small_names]),
                    _pack([m[n] for n in small_names]), _pack([v[n] for n in small_names]), name="adamw_small")
    for out, res in zip((delta, new_m, new_v), packed):
        for n, a in zip(small_names, _unpack(res, shapes)):
            out[n] = a

    return (loss, grad_x[None], *[grad[n] for n in WEIGHTS], *[delta[n] for n in WEIGHTS],
            *[new_m[n] for n in WEIGHTS], *[new_v[n] for n in WEIGHTS])
```

```python
import functools
import math

import numpy as np
import jax
import jax.numpy as jnp
from jax import lax
from jax.experimental import pallas as pl
from jax.experimental.pallas import tpu as pltpu

F32 = jnp.float32
BF16 = jnp.bfloat16
MESH = pl.DeviceIdType.MESH

CHUNK = 128
HEAD_DIM = 64
DILATED_GROUPS = ((128, 1), (512, 4), (2048, 16))
N_GROUPS = len(DILATED_GROUPS)
BLK = 128
REL_BUCKETS = 32
REL_MAX_DIST = 2048
LN_EPS = 1e-5
NEG = -1e30
ADAM_LR = 0.001
ADAM_B1 = 0.9
ADAM_B2 = 0.999
ADAM_EPS = 1e-08
ADAM_WD = 0.01
ADAM_STEP = 10

N_CHIPS = 4
N_DEV = 8
LANES = 128
SUBLANES = 8
VMEM_LIMIT = 48 * 1024 * 1024

_SQRT_HALF = 0.7071067811865476
_INV_SQRT_2PI = 0.3989422804014327


def _cp(sem=None, vmem=VMEM_LIMIT):
    return pltpu.CompilerParams(dimension_semantics=sem, vmem_limit_bytes=vmem)


def _tile(dim, target, align=LANES):
    if dim <= target:
        return dim
    t = (target // align) * align
    while t >= align:
        if dim % t == 0:
            return t
        t -= align
    return dim


def _gelu(x):
    return 0.5 * x * (1.0 + lax.erf(x * _SQRT_HALF))


def _gelu_grad(x):
    return 0.5 * (1.0 + lax.erf(x * _SQRT_HALF)) + x * (_INV_SQRT_2PI * jnp.exp(-0.5 * x * x))


def _dot(a, b, ca, cb):
    return lax.dot_general(a, b, (((ca,), (cb,)), ((), ())), preferred_element_type=F32)


def _mn_tile(dim):
    return max(_tile(dim, 1024), _tile(dim, 1408))


def _mm(a, b, *, name, ta=False, tb=False, out_dtype=F32, tiles=None,
        a_lead=(), b_lead=(), a_halves=False, b_halves=False, add=None, add_scale=1.0, out_into=None, after=None):
    a2, b2 = a.shape[-2:], b.shape[-2:]
    M, K = (a2[1], a2[0]) if ta else a2
    Kb, N = (b2[1], b2[0]) if tb else b2
    if a_halves:
        K = 2 * K
    if b_halves:
        N = 2 * N
    assert K == Kb, (name, a.shape, b.shape)
    if tiles is None:
        k_target = 1408 if a_halves else (K if K <= 2816 else 2048)
        tm, tn = _mn_tile(M), _mn_tile(N // 2 if b_halves else N)
        tk = _tile(K // 2 if a_halves else K, k_target)
    else:
        tm = _tile(M, tiles[0])
        tn = _tile(N // 2 if b_halves else N, tiles[1])
        tk = _tile(K // 2 if a_halves else K, tiles[2])
    nm, nn, nk = M // tm, N // tn, K // tk
    nkh, nnh = nk // 2, nn // 2

    def lead(idx, rest):
        return tuple(idx) + tuple(rest)

    sq_a = (None,) * (a.ndim - 2)
    sq_b = (None,) * (b.ndim - 2)
    if a_halves:
        assert not ta and a.ndim == 3
        a_spec = pl.BlockSpec((None, tm, tk), lambda i, j, k: (k // nkh, i, k % nkh))
    elif ta:
        a_spec = pl.BlockSpec(sq_a + (tk, tm), lambda i, j, k: lead(a_lead, (k, i)))
    else:
        a_spec = pl.BlockSpec(sq_a + (tm, tk), lambda i, j, k: lead(a_lead, (i, k)))
    if b_halves:
        assert not tb and b.ndim == 3
        b_spec = pl.BlockSpec((None, tk, tn), lambda i, j, k: (j // nnh, k, j % nnh))
    elif tb:
        b_spec = pl.BlockSpec(sq_b + (tn, tk), lambda i, j, k: lead(b_lead, (j, k)))
    else:
        b_spec = pl.BlockSpec(sq_b + (tk, tn), lambda i, j, k: lead(b_lead, (k, j)))
    mn_spec = pl.BlockSpec((tm, tn), lambda i, j, k: (i, j))
    in_specs = [a_spec, b_spec]
    args = [a, b]
    if add is not None:
        in_specs.append(mn_spec)
        args.append(add)
    aliases = {}
    if out_into is None:
        o_spec, out_shape = mn_spec, jax.ShapeDtypeStruct((M, N), out_dtype)
    else:
        buf, n_layers, layer = out_into
        o_spec = pl.BlockSpec((None, tm, tn), lambda i, j, k: (layer, i, j))
        out_shape = jax.ShapeDtypeStruct((n_layers, M, N), out_dtype)
        if buf is not None:
            aliases = {len(args): 0}
            in_specs.append(pl.BlockSpec(memory_space=pl.ANY))
            args.append(buf)
    if after is not None:
        in_specs.append(pl.BlockSpec(memory_space=pl.ANY))
        args.append(after)
    n_in = len(args)
    ca, cb = (0 if ta else 1), (1 if tb else 0)

    def body(*refs):
        a_ref, b_ref = refs[:2]
        add_ref = refs[2] if add is not None else None
        o_ref = refs[n_in]

        def finish(acc):
            if add_ref is not None:
                acc = acc + add_scale * add_ref[...]
            o_ref[...] = acc.astype(o_ref.dtype)

        if nk == 1:
            finish(_dot(a_ref[...], b_ref[...], ca, cb))
        else:
            acc_ref = refs[-1]
            k = pl.program_id(2)

            @pl.when(k == 0)
            def _():
                acc_ref[...] = jnp.zeros_like(acc_ref)

            acc_ref[...] += _dot(a_ref[...], b_ref[...], ca, cb)

            @pl.when(k == nk - 1)
            def _():
                finish(acc_ref[...])

    return pl.pallas_call(
        body, name=name, grid=(nm, nn, nk),
        in_specs=in_specs, out_specs=o_spec, out_shape=out_shape,
        input_output_aliases=aliases,
        scratch_shapes=[] if nk == 1 else [pltpu.VMEM((tm, tn), F32)],
        compiler_params=_cp(("parallel", "parallel", "arbitrary")),
    )(*args)


def _ln_stats(r):
    mu = jnp.mean(r, axis=-1, keepdims=True)
    xc = r - mu
    var = jnp.mean(xc * xc, axis=-1, keepdims=True)
    rstd = lax.rsqrt(var + LN_EPS)
    return xc * rstd, rstd


def _add_ln(x, h, g, b, alpha, *, name):
    T, D = x.shape
    tr = _tile(T, 512, SUBLANES)
    row = pl.BlockSpec((tr, D), lambda i: (i, 0))
    vec = pl.BlockSpec((1, D), lambda i: (0, 0))

    def body(x_ref, h_ref, g_ref, b_ref, o_ref, ob_ref):
        xhat, _ = _ln_stats(alpha * x_ref[...] + h_ref[...])
        y = xhat * g_ref[...] + b_ref[...]
        o_ref[...] = y
        ob_ref[...] = y.astype(BF16)

    return pl.pallas_call(
        body, name=name, grid=(T // tr,),
        in_specs=[row, row, vec, vec], out_specs=[row, row],
        out_shape=[jax.ShapeDtypeStruct((T, D), F32), jax.ShapeDtypeStruct((T, D), BF16)],
        compiler_params=_cp(("parallel",)),
    )(x, h, g.reshape(1, D), b.reshape(1, D))


def _ln_bwd(dy, x, h, g, alpha, *, name):
    T, D = x.shape
    tr = _tile(T, 512, SUBLANES)
    row = pl.BlockSpec((tr, D), lambda i: (i, 0))
    vec = pl.BlockSpec((1, D), lambda i: (0, 0))

    def body(dy_ref, x_ref, h_ref, g_ref, dr_ref, drb_ref, dg_ref, db_ref):
        @pl.when(pl.program_id(0) == 0)
        def _():
            dg_ref[...] = jnp.zeros_like(dg_ref)
            db_ref[...] = jnp.zeros_like(db_ref)

        xhat, rstd = _ln_stats(alpha * x_ref[...] + h_ref[...])
        dyv = dy_ref[...]
        dxhat = dyv * g_ref[...]
        c1 = jnp.mean(dxhat, axis=-1, keepdims=True)
        c2 = jnp.mean(dxhat * xhat, axis=-1, keepdims=True)
        dr = rstd * (dxhat - c1 - xhat * c2)
        dr_ref[...] = dr
        drb_ref[...] = dr.astype(BF16)
        dg_ref[...] += jnp.sum(dyv * xhat, axis=0, keepdims=True)
        db_ref[...] += jnp.sum(dyv, axis=0, keepdims=True)

    return pl.pallas_call(
        body, name=name, grid=(T // tr,),
        in_specs=[row, row, row, vec], out_specs=[row, row, vec, vec],
        out_shape=[jax.ShapeDtypeStruct((T, D), F32), jax.ShapeDtypeStruct((T, D), BF16),
                   jax.ShapeDtypeStruct((1, D), F32), jax.ShapeDtypeStruct((1, D), F32)],
        compiler_params=_cp(("arbitrary",)),
    )(dy, x, h, g.reshape(1, D))


def _sgu_pieces(zp, lg, lb, E):
    z = _gelu(zp)
    u, v = z[:, :E], z[:, E:]
    vhat, rstd = _ln_stats(v)
    vn = vhat * lg + lb
    return u, vhat, rstd, vn


def _tril_mask():
    t = lax.broadcasted_iota(jnp.int32, (CHUNK, CHUNK), 0)
    s = lax.broadcasted_iota(jnp.int32, (CHUNK, CHUNK), 1)
    return s <= t


def _sgu_fwd(zp, ws, bs, lg, lb, *, name):
    T, E2 = zp.shape
    E = E2 // 2
    G = ws.shape[0]
    cw = E // G

    def body(zp_ref, ws_ref, bs_ref, lg_ref, lb_ref, y_ref):
        u, _, _, vn = _sgu_pieces(zp_ref[...].astype(F32), lg_ref[...], lb_ref[...], E)
        vnb = vn.astype(BF16)
        tril = _tril_mask()
        for g in range(G):
            cols = slice(g * cw, (g + 1) * cw)
            w = jnp.where(tril, ws_ref[g], 0.0).astype(BF16)
            sv = _dot(w, vnb[:, cols], 1, 0) + bs_ref[g]
            y_ref[:, cols] = (u[:, cols] * sv).astype(BF16)

    return pl.pallas_call(
        body, name=name, grid=(T // CHUNK,),
        in_specs=[pl.BlockSpec((CHUNK, E2), lambda n: (n, 0)),
                  pl.BlockSpec((G, CHUNK, CHUNK), lambda n: (0, 0, 0)),
                  pl.BlockSpec((G, CHUNK, 1), lambda n: (0, 0, 0)),
                  pl.BlockSpec((1, E), lambda n: (0, 0)),
                  pl.BlockSpec((1, E), lambda n: (0, 0))],
        out_specs=pl.BlockSpec((CHUNK, E), lambda n: (n, 0)),
        out_shape=jax.ShapeDtypeStruct((T, E), BF16),
        compiler_params=_cp(("parallel",)),
    )(zp, ws, bs.reshape(G, CHUNK, 1), lg.reshape(1, E), lb.reshape(1, E))


def _sgu_bwd(zp, dy, ws, bs, lg, lb, *, name):
    T, E2 = zp.shape
    E = E2 // 2
    G = ws.shape[0]
    cw = E // G
    assert G <= LANES

    def body(zp_ref, dy_ref, ws_ref, bs_ref, lg_ref, lb_ref,
             dzp_ref, dws_ref, dbs_ref, dlg_ref, dlb_ref, dvn_ref):
        @pl.when(pl.program_id(0) == 0)
        def _():
            dws_ref[...] = jnp.zeros_like(dws_ref)
            dbs_ref[...] = jnp.zeros_like(dbs_ref)
            dlg_ref[...] = jnp.zeros_like(dlg_ref)
            dlb_ref[...] = jnp.zeros_like(dlb_ref)

        zpv = zp_ref[...].astype(F32)
        lgv = lg_ref[...]
        u, vhat, rstd, vn = _sgu_pieces(zpv, lgv, lb_ref[...], E)
        gp = _gelu_grad(zpv)
        vnb = vn.astype(BF16)
        tril = _tril_mask()
        lane = lax.broadcasted_iota(jnp.int32, (CHUNK, LANES), 1)
        dbs = dbs_ref[...]
        for g in range(G):
            cols = slice(g * cw, (g + 1) * cw)
            w = jnp.where(tril, ws_ref[g], 0.0).astype(BF16)
            sv = _dot(w, vnb[:, cols], 1, 0) + bs_ref[g]
            dyg = dy_ref[:, cols].astype(F32)
            dzp_ref[:, cols] = (dyg * sv * gp[:, cols]).astype(BF16)
            dsv = dyg * u[:, cols]
            dsvb = dsv.astype(BF16)
            dws_ref[g] += jnp.where(tril, _dot(dsvb, vnb[:, cols], 1, 1), 0.0)
            dvn_ref[:, cols] = _dot(w, dsvb, 0, 0)
            dbs = dbs + jnp.where(lane == g, jnp.sum(dsv, axis=1, keepdims=True), 0.0)
        dbs_ref[...] = dbs
        dvn = dvn_ref[...]
        dlg_ref[...] += jnp.sum(dvn * vhat, axis=0, keepdims=True)
        dlb_ref[...] += jnp.sum(dvn, axis=0, keepdims=True)
        dvhat = dvn * lgv
        c1 = jnp.mean(dvhat, axis=-1, keepdims=True)
        c2 = jnp.mean(dvhat * vhat, axis=-1, keepdims=True)
        dv = rstd * (dvhat - c1 - vhat * c2)
        dzp_ref[:, E:] = (dv * gp[:, E:]).astype(BF16)

    vecE = pl.BlockSpec((1, E), lambda n: (0, 0))
    return pl.pallas_call(
        body, name=name, grid=(T // CHUNK,),
        in_specs=[pl.BlockSpec((CHUNK, E2), lambda n: (n, 0)),
                  pl.BlockSpec((CHUNK, E), lambda n: (n, 0)),
                  pl.BlockSpec((G, CHUNK, CHUNK), lambda n: (0, 0, 0)),
                  pl.BlockSpec((G, CHUNK, 1), lambda n: (0, 0, 0)),
                  vecE, vecE],
        out_specs=[pl.BlockSpec((CHUNK, E2), lambda n: (n, 0)),
                   pl.BlockSpec((G, CHUNK, CHUNK), lambda n: (0, 0, 0)),
                   pl.BlockSpec((CHUNK, LANES), lambda n: (0, 0)),
                   vecE, vecE],
        out_shape=[jax.ShapeDtypeStruct((T, E2), BF16),
                   jax.ShapeDtypeStruct((G, CHUNK, CHUNK), F32),
                   jax.ShapeDtypeStruct((CHUNK, LANES), F32),
                   jax.ShapeDtypeStruct((1, E), F32), jax.ShapeDtypeStruct((1, E), F32)],
        scratch_shapes=[pltpu.VMEM((CHUNK, E), F32)],
        compiler_params=_cp(("arbitrary",)),
    )(zp, dy, ws, bs.reshape(G, CHUNK, 1), lg.reshape(1, E), lb.reshape(1, E))


def _delay(h, k):
    t = lax.broadcasted_iota(jnp.int32, h.shape, 0)
    return jnp.where(t >= k, pltpu.roll(h, k, 0), 0.0)


def _advance(d, k):
    T = d.shape[0]
    t = lax.broadcasted_iota(jnp.int32, d.shape, 0)
    return jnp.where(t < T - k, pltpu.roll(d, T - k, 0), 0.0)


def _conv3(h, w_ref, b_ref):
    return w_ref[2:3, :] * h + w_ref[1:2, :] * _delay(h, 1) + w_ref[0:1, :] * _delay(h, 2) + b_ref[...]


def _convglu_fwd(hup, cw, cb, *, name):
    T, F2 = hup.shape
    F = F2 // 2
    tc = LANES
    nt = F // tc

    def body(ha_ref, hg_ref, wa_ref, wg_ref, ba_ref, bg_ref, o_ref):
        a = _conv3(ha_ref[...].astype(F32), wa_ref, ba_ref)
        g = _conv3(hg_ref[...].astype(F32), wg_ref, bg_ref)
        o_ref[...] = (_gelu(a) * g).astype(BF16)

    col = lambda off: pl.BlockSpec((T, tc), lambda j: (0, j + off))
    w3 = lambda off: pl.BlockSpec((3, tc), lambda j: (0, j + off))
    b1 = lambda off: pl.BlockSpec((1, tc), lambda j: (0, j + off))
    return pl.pallas_call(
        body, name=name, grid=(nt,),
        in_specs=[col(0), col(nt), w3(0), w3(nt), b1(0), b1(nt)],
        out_specs=pl.BlockSpec((T, tc), lambda j: (0, j)),
        out_shape=jax.ShapeDtypeStruct((T, F), BF16),
        compiler_params=_cp(("parallel",)),
    )(hup, hup, cw, cw, cb.reshape(1, F2), cb.reshape(1, F2))


def _convglu_bwd(hup, dact, cw, cb, *, name):
    T, F2 = hup.shape
    F = F2 // 2
    tc = LANES
    nt = F // tc

    def half(h, w_ref, d, dh_ref, dw_ref, db_ref, i):
        d1, d2 = _advance(d, 1), _advance(d, 2)
        dh_ref[i] = (w_ref[2:3, :] * d + w_ref[1:2, :] * d1 + w_ref[0:1, :] * d2).astype(BF16)
        dw_ref[i, 0:1, :] = jnp.sum(d2 * h, axis=0, keepdims=True)
        dw_ref[i, 1:2, :] = jnp.sum(d1 * h, axis=0, keepdims=True)
        dw_ref[i, 2:3, :] = jnp.sum(d * h, axis=0, keepdims=True)
        db_ref[i] = jnp.sum(d, axis=0, keepdims=True)

    def body(ha_ref, hg_ref, d_ref, wa_ref, wg_ref, ba_ref, bg_ref, dh_ref, dw_ref, db_ref):
        ha, hg = ha_ref[...].astype(F32), hg_ref[...].astype(F32)
        a = _conv3(ha, wa_ref, ba_ref)
        g = _conv3(hg, wg_ref, bg_ref)
        d = d_ref[...].astype(F32)
        half(ha, wa_ref, d * g * _gelu_grad(a), dh_ref, dw_ref, db_ref, 0)
        half(hg, wg_ref, d * _gelu(a), dh_ref, dw_ref, db_ref, 1)

    col = lambda off: pl.BlockSpec((T, tc), lambda j: (0, j + off))
    w3 = lambda off: pl.BlockSpec((3, tc), lambda j: (0, j + off))
    b1 = lambda off: pl.BlockSpec((1, tc), lambda j: (0, j + off))
    return pl.pallas_call(
        body, name=name, grid=(nt,),
        in_specs=[col(0), col(nt), col(0), w3(0), w3(nt), b1(0), b1(nt)],
        out_specs=[pl.BlockSpec((2, T, tc), lambda j: (0, 0, j)),
                   pl.BlockSpec((2, 3, tc), lambda j: (0, 0, j)),
                   pl.BlockSpec((2, 1, tc), lambda j: (0, 0, j))],
        out_shape=[jax.ShapeDtypeStruct((2, T, F), BF16),
                   jax.ShapeDtypeStruct((2, 3, F), F32),
                   jax.ShapeDtypeStruct((2, 1, F), F32)],
        compiler_params=_cp(("parallel",)),
    )(hup, hup, dact, cw, cw, cb.reshape(1, F2), cb.reshape(1, F2))


def _bucket_table():
    iq = np.arange(BLK)[:, None]
    ik = np.arange(2 * BLK)[None, :]
    delta = iq + BLK - ik
    out = []
    for win, dil in DILATED_GROUPS:
        valid = (delta >= 0) & (delta <= win // dil)
        n = (np.clip(delta, 0, None) * dil).astype(np.int32)
        max_exact = REL_BUCKETS // 2
        nf = np.maximum(n, 1).astype(np.float32)
        large = max_exact + (np.log(nf / np.float32(max_exact)) / np.float32(math.log(REL_MAX_DIST / max_exact))
                             * np.float32(REL_BUCKETS - max_exact)).astype(np.int32)
        large = np.minimum(large, REL_BUCKETS - 1)
        out.append(np.where(valid, np.where(n < max_exact, n, large), -1))
    return np.stack(out).astype(np.int32)


def _band_bias(rel_table, H, *, name):
    bkt = jnp.asarray(_bucket_table())

    def body(tab_ref, bk_ref, o_ref):
        g = pl.program_id(0)
        bk = bk_ref[...]
        prev = lax.broadcasted_iota(jnp.int32, bk.shape, 1) < BLK
        for h in range(H):
            def bucket(c, acc):
                return jnp.where(bk == c, tab_ref[c, g * H + h], acc)

            b = lax.fori_loop(0, REL_BUCKETS, bucket, jnp.zeros(bk.shape, F32))
            b = jnp.where(bk >= 0, b, NEG)
            o_ref[h, 0] = b
            o_ref[h, 1] = jnp.where(prev, NEG, b)

    return pl.pallas_call(
        body, name=name, grid=(N_GROUPS,),
        in_specs=[pl.BlockSpec(memory_space=pltpu.SMEM),
                  pl.BlockSpec((None, BLK, 2 * BLK), lambda g: (g, 0, 0))],
        out_specs=pl.BlockSpec((None, H, 2, BLK, 2 * BLK), lambda g: (g, 0, 0, 0, 0)),
        out_shape=jax.ShapeDtypeStruct((N_GROUPS, H, 2, BLK, 2 * BLK), F32),
        compiler_params=_cp(("parallel",)),
    )(rel_table, bkt)


PAIR = 2 * HEAD_DIM


def _bdot(a, b, ca, cb):
    return lax.dot_general(a, b, (((ca,), (cb,)), ((0,), (0,))), preferred_element_type=F32)


def _heads(ref, H, own_columns_only):
    low = lax.broadcasted_iota(jnp.int32, (BLK, PAIR), 1) < HEAD_DIM
    out = []
    for h in range(H):
        a = ref[:, (h // 2) * PAIR:(h // 2 + 1) * PAIR]
        if own_columns_only:
            a = jnp.where(low if h % 2 == 0 else jnp.logical_not(low), a, jnp.zeros_like(a))
        out.append(a)
    return jnp.stack(out)


def _pair_sums(o):
    return [o[2 * p] + o[2 * p + 1] for p in range(o.shape[0] // 2)]


def _store_pairs(ref, o, dtype, scale=None):
    for p, v in enumerate(_pair_sums(o)):
        ref[:, p * PAIR:(p + 1) * PAIR] = (v if scale is None else scale * v).astype(dtype)


def _stat_tile(s):
    lane = lax.broadcasted_iota(jnp.int32, (BLK, LANES), 1)
    t = jnp.zeros((BLK, LANES), F32)
    for h in range(s.shape[0]):
        t = jnp.where(lane == h, s[h], t)
    return t


def _stat_cols(ref, H):
    return jnp.stack([ref[:, h:h + 1] for h in range(H)])


def _scores(qm, kp, kc, bias):
    scale = HEAD_DIM ** -0.5
    sp = _bdot(qm, kp, 2, 2) * scale + bias[:, :, :BLK]
    sc = _bdot(qm, kc, 2, 2) * scale + bias[:, :, BLK:]
    return sp, sc


def _attn_specs(nb, dil, g, H, HD, qb):
    nbg = nb // dil
    cur = lambda col: pl.BlockSpec((BLK, HD), lambda b: (qb(b), col))
    prev = lambda col: pl.BlockSpec((BLK, HD), lambda b: (jnp.maximum(qb(b) - 1, 0), col))
    stat = pl.BlockSpec((BLK, LANES), lambda b: (qb(b), 0))
    bias = pl.BlockSpec((None, H, None, BLK, 2 * BLK),
                        lambda b: (g, 0, jnp.where(qb(b) % nbg == 0, 1, 0), 0, 0))
    return cur, prev, stat, bias


def _attn_fwd(q, k, v, bias, g, dil, *, name):
    (qa, qc), (ka, kc_), (va, vc_) = q, k, v
    T = qa.shape[0]
    H = bias.shape[1]
    HD = H * HEAD_DIM
    assert H % 2 == 0 and H <= LANES and T % (dil * BLK) == 0
    nb = T // BLK
    cur, prev, stat, bspec = _attn_specs(nb, dil, g, H, HD, lambda b: b)

    def body(q_ref, kp_ref, kc_ref, vp_ref, vc_ref, b_ref, o_ref, l_ref):
        qm, kp, kc = _heads(q_ref, H, True), _heads(kp_ref, H, False), _heads(kc_ref, H, False)
        sp, sc = _scores(qm, kp, kc, b_ref[...])
        m = jnp.maximum(jnp.max(sp, axis=-1, keepdims=True), jnp.max(sc, axis=-1, keepdims=True))
        pp, pc = jnp.exp(sp - m), jnp.exp(sc - m)
        den = jnp.sum(pp, axis=-1, keepdims=True) + jnp.sum(pc, axis=-1, keepdims=True)
        o = (_bdot((pp / den).astype(BF16), _heads(vp_ref, H, True), 2, 1)
             + _bdot((pc / den).astype(BF16), _heads(vc_ref, H, True), 2, 1))
        _store_pairs(o_ref, o, F32)
        l_ref[...] = _stat_tile(m + jnp.log(den))

    return pl.pallas_call(
        body, name=name, grid=(nb,),
        in_specs=[cur(qc), prev(kc_), cur(kc_), prev(vc_), cur(vc_), bspec],
        out_specs=[cur(0), stat],
        out_shape=[jax.ShapeDtypeStruct((T, HD), F32), jax.ShapeDtypeStruct((T, LANES), F32)],
        compiler_params=_cp(("parallel",)),
    )(qa, ka, ka, va, va, bias)


def _attn_combine(os, ls, *, name):
    n = len(os)
    T, HD = os[0].shape
    H = HD // HEAD_DIM
    tr = _tile(T, 512, SUBLANES)
    wide = pl.BlockSpec((tr, HD), lambda i: (i, 0))
    stat = pl.BlockSpec((tr, LANES), lambda i: (i, 0))

    def body(*refs):
        o_refs, l_refs = refs[:n], refs[n:2 * n]
        oc_ref, ocb_ref, lt_ref = refs[2 * n:]
        ls_ = [r[...] for r in l_refs]
        m = ls_[0]
        for l in ls_[1:]:
            m = jnp.maximum(m, l)
        ws = [jnp.exp(l - m) for l in ls_]
        den = ws[0]
        for w in ws[1:]:
            den = den + w
        lt_ref[...] = m + jnp.log(den)
        ws = [w / den for w in ws]
        low = lax.broadcasted_iota(jnp.int32, (tr, PAIR), 1) < HEAD_DIM
        for p in range(H // 2):
            cols = slice(p * PAIR, (p + 1) * PAIR)
            acc = jnp.zeros((tr, PAIR), F32)
            for g in range(n):
                w = jnp.where(low, ws[g][:, 2 * p:2 * p + 1], ws[g][:, 2 * p + 1:2 * p + 2])
                acc = acc + w * o_refs[g][:, cols]
            oc_ref[:, cols] = acc
            ocb_ref[:, cols] = acc.astype(BF16)

    return pl.pallas_call(
        body, name=name, grid=(T // tr,),
        in_specs=[wide] * n + [stat] * n, out_specs=[wide, wide, stat],
        out_shape=[jax.ShapeDtypeStruct((T, HD), F32), jax.ShapeDtypeStruct((T, HD), BF16),
                   jax.ShapeDtypeStruct((T, LANES), F32)],
        compiler_params=_cp(("parallel",)),
    )(*os, *ls)


def _attn_bwd_prep(do, oc, *, name):
    T, HD = do.shape
    H = HD // HEAD_DIM
    tr = _tile(T, 512, SUBLANES)
    wide = pl.BlockSpec((tr, HD), lambda i: (i, 0))
    stat = pl.BlockSpec((tr, LANES), lambda i: (i, 0))

    def body(do_ref, oc_ref, dl_ref, dob_ref):
        low = lax.broadcasted_iota(jnp.int32, (tr, PAIR), 1) < HEAD_DIM
        lane = lax.broadcasted_iota(jnp.int32, (tr, LANES), 1)
        dl = jnp.zeros((tr, LANES), F32)
        for p in range(H // 2):
            cols = slice(p * PAIR, (p + 1) * PAIR)
            d = do_ref[:, cols]
            prod = d * oc_ref[:, cols]
            dl = jnp.where(lane == 2 * p, jnp.sum(jnp.where(low, prod, 0.0), axis=-1, keepdims=True), dl)
            dl = jnp.where(lane == 2 * p + 1, jnp.sum(jnp.where(low, 0.0, prod), axis=-1, keepdims=True), dl)
            dob_ref[:, cols] = d.astype(BF16)
        dl_ref[...] = dl

    return pl.pallas_call(
        body, name=name, grid=(T // tr,),
        in_specs=[wide, wide], out_specs=[stat, wide],
        out_shape=[jax.ShapeDtypeStruct((T, LANES), F32), jax.ShapeDtypeStruct((T, HD), BF16)],
        compiler_params=_cp(("parallel",)),
    )(do, oc)


def _attn_bwd(q, k, v, bias, g, dil, do, lse, delta, *, name):
    (qa, qc), (ka, kc_), (va, vc_) = q, k, v
    T, HD = do.shape
    H = HD // HEAD_DIM
    nb = T // BLK
    scale = HEAD_DIM ** -0.5
    cur, prev, stat, bspec = _attn_specs(nb, dil, g, H, HD, lambda b: jnp.minimum(b, nb - 1))
    late = pl.BlockSpec((BLK, HD), lambda b: (jnp.maximum(b - 1, 0), 0))

    def body(q_ref, kp_ref, kc_ref, vp_ref, vc_ref, b_ref, do_ref, l_ref, dl_ref,
             dq_ref, dk_ref, dv_ref, db_ref, ck_ref, cv_ref):
        b = pl.program_id(0)

        @pl.when(b == 0)
        def _():
            db_ref[...] = jnp.zeros_like(db_ref)
            ck_ref[...] = jnp.zeros_like(ck_ref)
            cv_ref[...] = jnp.zeros_like(cv_ref)

        @pl.when(b < nb)
        def _():
            qm, dom = _heads(q_ref, H, True), _heads(do_ref, H, True)
            sp, sc = _scores(qm, _heads(kp_ref, H, False), _heads(kc_ref, H, False), b_ref[...])
            l, dl = _stat_cols(l_ref, H), _stat_cols(dl_ref, H)
            pp, pc = jnp.exp(sp - l), jnp.exp(sc - l)
            dsp = pp * (_bdot(dom, _heads(vp_ref, H, False), 2, 2) - dl)
            dsc = pc * (_bdot(dom, _heads(vc_ref, H, False), 2, 2) - dl)
            db_ref[:, :, :BLK] += dsp
            db_ref[:, :, BLK:] += dsc
            dspb, dscb = dsp.astype(BF16), dsc.astype(BF16)
            _store_pairs(dq_ref, _bdot(dspb, _heads(kp_ref, H, True), 2, 1)
                         + _bdot(dscb, _heads(kc_ref, H, True), 2, 1), BF16, scale)
            dkp = _pair_sums(_bdot(dspb, qm, 1, 1))
            dkc = _pair_sums(_bdot(dscb, qm, 1, 1))
            dvp = _pair_sums(_bdot(pp.astype(BF16), dom, 1, 1))
            dvc = _pair_sums(_bdot(pc.astype(BF16), dom, 1, 1))
            for p in range(H // 2):
                cols = slice(p * PAIR, (p + 1) * PAIR)
                dk_ref[:, cols] = ck_ref[:, cols] + scale * dkp[p]
                dv_ref[:, cols] = cv_ref[:, cols] + dvp[p]
                ck_ref[:, cols] = scale * dkc[p]
                cv_ref[:, cols] = dvc[p]

        @pl.when(b == nb)
        def _():
            dk_ref[...] = ck_ref[...]
            dv_ref[...] = cv_ref[...]

    f32 = jax.ShapeDtypeStruct((T, HD), F32)
    return pl.pallas_call(
        body, name=name, grid=(nb + 1,),
        in_specs=[cur(qc), prev(kc_), cur(kc_), prev(vc_), cur(vc_), bspec, cur(0), stat, stat],
        out_specs=[cur(0), late, late, pl.BlockSpec((H, BLK, 2 * BLK), lambda b: (0, 0, 0))],
        out_shape=[jax.ShapeDtypeStruct((T, HD), BF16), f32, f32,
                   jax.ShapeDtypeStruct((H, BLK, 2 * BLK), F32)],
        scratch_shapes=[pltpu.VMEM((BLK, HD), F32), pltpu.VMEM((BLK, HD), F32)],
        compiler_params=_cp(("arbitrary",)),
    )(qa, ka, ka, va, va, bias, do, lse, delta)


def _rel_grad(dbs, *, name):
    H = dbs[0].shape[1]
    n = len(dbs)
    assert H <= LANES
    bkt = jnp.asarray(_bucket_table())

    def body(*refs):
        bk = refs[n][...]
        o_ref = refs[n + 1]
        db = refs[0][...]
        for r in refs[1:n]:
            db = db + r[...]
        row = lax.broadcasted_iota(jnp.int32, (REL_BUCKETS, LANES), 0)
        col = lax.broadcasted_iota(jnp.int32, (REL_BUCKETS, LANES), 1)

        def bucket(c, acc):
            sel = bk == c
            for h in range(H):
                val = jnp.sum(jnp.where(sel, db[h], 0.0))
                acc = jnp.where((row == c) & (col == h), val, acc)
            return acc

        o_ref[...] = lax.fori_loop(0, REL_BUCKETS, bucket, jnp.zeros((REL_BUCKETS, LANES), F32))

    dspec = pl.BlockSpec((None, H, BLK, 2 * BLK), lambda g: (g, 0, 0, 0))
    return pl.pallas_call(
        body, name=name, grid=(N_GROUPS,),
        in_specs=[dspec] * n + [pl.BlockSpec((None, BLK, 2 * BLK), lambda g: (g, 0, 0))],
        out_specs=pl.BlockSpec((None, REL_BUCKETS, LANES), lambda g: (g, 0, 0)),
        out_shape=jax.ShapeDtypeStruct((N_GROUPS, REL_BUCKETS, LANES), F32),
        compiler_params=_cp(("parallel",)),
    )(*dbs, bkt)


def _to_group(a, d):
    T, C = a.shape
    return a if d == 1 else a.reshape(T // d, d, C).transpose(1, 0, 2).reshape(T, C)


def _from_group(a, d):
    T, C = a.shape
    return a if d == 1 else a.reshape(d, T // d, C).transpose(1, 0, 2).reshape(T, C)


def _rows_view(a):
    a2 = a.reshape(-1, a.shape[-1])
    R, C = a2.shape
    tr = _tile(R, max(SUBLANES, (1 << 18) // C), SUBLANES)
    return a2, R, C, tr


def _addn(xs, out_dtype, *, name):
    shape = xs[0].shape
    x2s = [_rows_view(x)[0] for x in xs]
    _, R, C, tr = _rows_view(xs[0])
    spec = pl.BlockSpec((tr, C), lambda i: (i, 0))

    def body(*refs):
        acc = refs[0][...].astype(F32)
        for r in refs[1:-1]:
            acc = acc + r[...].astype(F32)
        refs[-1][...] = acc.astype(out_dtype)

    out = pl.pallas_call(
        body, name=name, grid=(R // tr,),
        in_specs=[spec] * len(xs), out_specs=spec,
        out_shape=jax.ShapeDtypeStruct((R, C), out_dtype),
        compiler_params=_cp(("parallel",)),
    )(*x2s)
    return out.reshape(shape)


def _slot_sum(b_ref, own, my_id):
    acc = None
    for s in range(b_ref.shape[0]):
        term = jnp.where(my_id == s, own, b_ref[s].astype(F32))
        acc = term if acc is None else acc + term
    return acc


def _part_geometry(shape, axis):
    _, R, C = shape
    if axis == COL:
        return R // 2, C // N_CHIPS
    return R // (2 * N_CHIPS), C


def _sum_parts(buf, grad, axis, where, *, name):
    n, L, Rp, Cp = buf.shape
    assert (Rp, Cp) == _part_geometry(grad.shape, axis)
    tr = _tile(Rp, max(2 * SUBLANES, (1 << 17) // Cp), 2 * SUBLANES)
    nr = Rp // tr

    def own_map(l, i, w):
        if axis == COL:
            return (l, w[1] * nr + i, w[0])
        return (l, (2 * w[0] + w[1]) * nr + i, 0)

    def body(w_ref, b_ref, g_ref, o_ref):
        o_ref[...] = _slot_sum(b_ref, g_ref[...].astype(F32), w_ref[2])

    return pl.pallas_call(
        body, name=name,
        grid_spec=pltpu.PrefetchScalarGridSpec(
            num_scalar_prefetch=1, grid=(L, nr),
            in_specs=[pl.BlockSpec((n, None, tr, Cp), lambda l, i, w: (0, l, i, 0)),
                      pl.BlockSpec((None, tr, Cp), own_map)],
            out_specs=pl.BlockSpec((None, None, tr, Cp), lambda l, i, w: (l, w[1], i, 0))),
        out_shape=jax.ShapeDtypeStruct((L, 2, Rp, Cp), F32),
        compiler_params=_cp(("parallel", "parallel")),
    )(where, buf, grad)


def _sum_small(buf, own, where, *, name):
    n, R, C = buf.shape
    tr = _tile(R, 512, SUBLANES)

    def body(w_ref, b_ref, g_ref, o_ref):
        o_ref[...] = _slot_sum(b_ref, g_ref[...], w_ref[2])

    return pl.pallas_call(
        body, name=name,
        grid_spec=pltpu.PrefetchScalarGridSpec(
            num_scalar_prefetch=1, grid=(R // tr,),
            in_specs=[pl.BlockSpec((n, tr, C), lambda i, w: (0, i, 0)),
                      pl.BlockSpec((tr, C), lambda i, w: (i, 0))],
            out_specs=pl.BlockSpec((tr, C), lambda i, w: (i, 0))),
        out_shape=jax.ShapeDtypeStruct((R, C), F32),
        compiler_params=_cp(("parallel",)),
    )(where, buf, own)


def _place_shard(shard, layer, axis, where, out_dtype, *, name, after=None):
    _, R, C = shard.shape
    tr = _tile(R, max(2 * SUBLANES, (1 << 18) // C), 2 * SUBLANES)
    nr = R // tr
    if axis == COL:
        full, out_map = (1, R, C * N_CHIPS), (lambda i, w: (0, i, w[0]))
    else:
        full, out_map = (1, R * N_CHIPS, C), (lambda i, w: (0, w[0] * nr + i, 0))

    def body(w_ref, x_ref, *rest):
        rest[-1][...] = x_ref[...].astype(out_dtype)

    extra = [] if after is None else [after]
    return pl.pallas_call(
        body, name=name,
        grid_spec=pltpu.PrefetchScalarGridSpec(
            num_scalar_prefetch=1, grid=(nr,),
            in_specs=[pl.BlockSpec((None, tr, C), lambda i, w: (layer, i, 0))]
            + [pl.BlockSpec(memory_space=pl.ANY)] * len(extra),
            out_specs=pl.BlockSpec((None, tr, C), out_map)),
        out_shape=jax.ShapeDtypeStruct(full, out_dtype),
        compiler_params=_cp(("parallel",)),
    )(where, shard, *extra)


def _loss_grad(y, tgt, *, name):
    T, D = y.shape
    tr = _tile(T, 512, SUBLANES)
    row = pl.BlockSpec((tr, D), lambda i: (i, 0))

    def body(y_ref, t_ref, dy_ref, l_ref, acc_ref):
        i = pl.program_id(0)

        @pl.when(i == 0)
        def _():
            acc_ref[...] = jnp.zeros_like(acc_ref)

        err = y_ref[...] - t_ref[...]
        dy_ref[...] = err * (1.0 / D)
        acc_ref[...] += jnp.sum(err * err, axis=0, keepdims=True)

        @pl.when(i == T // tr - 1)
        def _():
            tot = jnp.sum(acc_ref[...], axis=1, keepdims=True) * (0.5 / D)
            l_ref[...] = jnp.broadcast_to(tot, (1, LANES))

    return pl.pallas_call(
        body, name=name, grid=(T // tr,),
        in_specs=[row, row],
        out_specs=[row, pl.BlockSpec((1, LANES), lambda i: (0, 0))],
        out_shape=[jax.ShapeDtypeStruct((T, D), F32), jax.ShapeDtypeStruct((1, LANES), F32)],
        scratch_shapes=[pltpu.VMEM((1, D), F32)],
        compiler_params=_cp(("arbitrary",)),
    )(y, tgt)


def _adamw(w, g, m, v, *, name):
    shape = w.shape
    w2, R, C, tr = _rows_view(w)
    spec = pl.BlockSpec((tr, C), lambda i: (i, 0))

    def body(w_ref, g_ref, m_ref, v_ref, d_ref, nm_ref, nv_ref):
        gv = g_ref[...]
        nm = ADAM_B1 * m_ref[...] + (1.0 - ADAM_B1) * gv
        nv = ADAM_B2 * v_ref[...] + (1.0 - ADAM_B2) * (gv * gv)
        m_hat = nm / (1.0 - ADAM_B1 ** ADAM_STEP)
        v_hat = nv / (1.0 - ADAM_B2 ** ADAM_STEP)
        d_ref[...] = -ADAM_LR * (m_hat / (jnp.sqrt(v_hat) + ADAM_EPS) + ADAM_WD * w_ref[...])
        nm_ref[...] = nm
        nv_ref[...] = nv

    shp = jax.ShapeDtypeStruct((R, C), F32)
    outs = pl.pallas_call(
        body, name=name, grid=(R // tr,),
        in_specs=[spec] * 4, out_specs=[spec] * 3, out_shape=[shp] * 3,
        compiler_params=_cp(("parallel",)),
    )(w2, g.reshape(R, C), m.reshape(R, C), v.reshape(R, C))
    return tuple(o.reshape(shape) for o in outs)


def _pack(arrs):
    flat = jnp.concatenate([a.reshape(-1).astype(F32) for a in arrs])
    unit = SUBLANES * LANES
    pad = (-flat.shape[0]) % unit
    return jnp.pad(flat, (0, pad)).reshape(-1, LANES)


def _unpack(buf, shapes):
    flat = buf.reshape(-1)
    out, off = [], 0
    for s in shapes:
        n = int(np.prod(s))
        out.append(flat[off:off + n].reshape(s))
        off += n
    return out


def _me():
    return lax.axis_index("x"), lax.axis_index("y"), lax.axis_index("c")


def _flip(pos, k):
    x, y, c = pos
    return (1 - x if k & 4 else x, 1 - y if k & 2 else y, 1 - c if k & 1 else c)


HBM = pl.BlockSpec(memory_space=pltpu.HBM)

COL, ROW = -1, -2


def _shard_window(ref, axis, j, n):
    start = pl.multiple_of(j * n, n)
    if axis == COL:
        return ref.at[:, :, pl.ds(start, n)]
    return ref.at[:, pl.ds(start, n), :]


GATHER_PEERS = (2, 4, 6)
SEM = pl.BlockSpec(memory_space=pltpu.SEMAPHORE)
EFFECT = pltpu.SideEffectType.DATAFLOW_SIDE_EFFECTING


def _gather_copy(buf, axis, i, pi, chip, me, send_sems, recv_sems):
    win = _shard_window(buf, axis, chip, buf.shape[axis] // N_CHIPS)
    k = len(GATHER_PEERS) * i + pi
    return pltpu.make_async_remote_copy(
        src_ref=win, dst_ref=win, send_sem=send_sems.at[k], recv_sem=recv_sems.at[k],
        device_id=_flip(me, GATHER_PEERS[pi]), device_id_type=MESH)


def _gather_start(fulls, axes, *, name):
    n = len(fulls)
    n_sem = len(GATHER_PEERS) * n

    def body(*refs):
        send_sems, recv_sems = refs[n], refs[n + 1]
        bufs = refs[n + 2:2 * n + 2]
        token = refs[2 * n + 2]
        me = _me()
        for i in range(n):
            for pi in range(len(GATHER_PEERS)):
                _gather_copy(bufs[i], axes[i], i, pi, 2 * me[0] + me[1], me, send_sems, recv_sems).start()
        token[...] = jnp.zeros_like(token)

    outs = pl.pallas_call(
        body, name=name,
        in_specs=[HBM] * n, out_specs=[SEM, SEM] + [HBM] * n + [pl.BlockSpec(memory_space=pltpu.VMEM)],
        out_shape=[pltpu.SemaphoreType.DMA((n_sem,)), pltpu.SemaphoreType.DMA((n_sem,))]
        + [pltpu.HBM(f.shape, f.dtype) for f in fulls] + [jax.ShapeDtypeStruct((SUBLANES, LANES), F32)],
        input_output_aliases={i: 2 + i for i in range(n)},
        compiler_params=pltpu.CompilerParams(has_side_effects=EFFECT),
    )(*[pltpu.with_memory_space_constraint(f, pltpu.HBM) for f in fulls])
    return outs[0], outs[1], list(outs[2:2 + n]), outs[-1]


def _gather_wait(send_sems, recv_sems, bufs, axes, idxs, after, *, name):
    m = len(bufs)

    def body(*refs):
        ss, rs = refs[m], refs[m + 1]
        outs = refs[m + 3:]
        me = _me()
        for t, i in enumerate(idxs):
            for pi, k in enumerate(GATHER_PEERS):
                px, py, _ = _flip(me, k)
                _gather_copy(outs[t], axes[t], i, pi, 2 * me[0] + me[1], me, ss, rs).wait_send()
                _gather_copy(outs[t], axes[t], i, pi, 2 * px + py, me, ss, rs).wait_recv()

    return pl.pallas_call(
        body, name=name,
        in_specs=[HBM] * m + [SEM, SEM, pl.BlockSpec(memory_space=pl.ANY)], out_specs=[HBM] * m,
        out_shape=[pltpu.HBM(b.shape, b.dtype) for b in bufs],
        input_output_aliases={t: t for t in range(m)},
        compiler_params=pltpu.CompilerParams(has_side_effects=EFFECT),
    )(*bufs, send_sems, recv_sems, after)


def _grad_part(ref, layer, axis, pos):
    px, py, pc = pos
    Rp, Cp = _part_geometry(ref.shape, axis)
    chip = 2 * px + py
    if axis == COL:
        return ref.at[layer, pl.ds(pl.multiple_of(pc * Rp, Rp), Rp), pl.ds(pl.multiple_of(chip * Cp, Cp), Cp)]
    return ref.at[layer, pl.ds(pl.multiple_of((2 * chip + pc) * Rp, Rp), Rp), :]


def _scatter_copy(grad, slots, layer, axis, i, k, src_pos, dst_slot, me, send_sems, recv_sems):
    sem = (N_DEV - 1) * i + k - 1
    return pltpu.make_async_remote_copy(
        src_ref=_grad_part(grad, layer, axis, src_pos), dst_ref=slots.at[dst_slot, layer],
        send_sem=send_sems.at[sem], recv_sem=recv_sems.at[sem],
        device_id=_flip(me, k), device_id_type=MESH)


def _scatter_start(grads, slots, layers, axes, *, name):
    n = len(grads)
    n_sem = (N_DEV - 1) * n

    def body(*refs):
        send_sems, recv_sems = refs[2 * n], refs[2 * n + 1]
        g_refs, s_refs = refs[2 * n + 2:3 * n + 2], refs[3 * n + 2:4 * n + 2]
        token = refs[4 * n + 2]
        me = _me()
        my_id = 4 * me[0] + 2 * me[1] + me[2]
        for i in range(n):
            for k in range(1, N_DEV):
                _scatter_copy(g_refs[i], s_refs[i], layers[i], axes[i], i, k, _flip(me, k), my_id, me,
                              send_sems, recv_sems).start()
        token[...] = jnp.zeros_like(token)

    arrays = list(grads) + list(slots)
    outs = pl.pallas_call(
        body, name=name,
        in_specs=[HBM] * (2 * n),
        out_specs=[SEM, SEM] + [HBM] * (2 * n) + [pl.BlockSpec(memory_space=pltpu.VMEM)],
        out_shape=[pltpu.SemaphoreType.DMA((n_sem,)), pltpu.SemaphoreType.DMA((n_sem,))]
        + [pltpu.HBM(a.shape, a.dtype) for a in arrays] + [jax.ShapeDtypeStruct((SUBLANES, LANES), F32)],
        input_output_aliases={i: 2 + i for i in range(2 * n)},
        compiler_params=pltpu.CompilerParams(has_side_effects=EFFECT),
    )(*[pltpu.with_memory_space_constraint(a, pltpu.HBM) for a in arrays])
    return outs[0], outs[1], list(outs[2:2 + n]), list(outs[2 + n:2 + 2 * n]), outs[-1]


def _scatter_wait(sems, grads, slots, axes, plan, *, name):
    n = len(grads)
    flat_sems = [s for pair in sems for s in pair]

    def body(*refs):
        sem_refs = refs[2 * n:2 * n + len(flat_sems)]
        g_refs, s_refs = refs[2 * n + len(flat_sems):3 * n + len(flat_sems)], refs[3 * n + len(flat_sems):]
        me = _me()
        for j, items in enumerate(plan):
            ss, rs = sem_refs[2 * j], sem_refs[2 * j + 1]
            for i, (a, layer) in enumerate(items):
                for k in range(1, N_DEV):
                    px, py, pc = _flip(me, k)
                    cp = _scatter_copy(g_refs[a], s_refs[a], layer, axes[a], i, k, me, 4 * px + 2 * py + pc, me, ss, rs)
                    cp.wait_send()
                    cp.wait_recv()

    arrays = list(grads) + list(slots)
    outs = pl.pallas_call(
        body, name=name,
        in_specs=[HBM] * (2 * n) + [SEM] * len(flat_sems), out_specs=[HBM] * (2 * n),
        out_shape=[pltpu.HBM(a.shape, a.dtype) for a in arrays],
        input_output_aliases={i: i for i in range(2 * n)},
        compiler_params=pltpu.CompilerParams(has_side_effects=EFFECT),
    )(*arrays, *flat_sems)
    return list(outs[:n]), list(outs[n:])


def _small_copy(src, dst, k, slot, me, send_sems, recv_sems):
    return pltpu.make_async_remote_copy(
        src_ref=src, dst_ref=dst.at[slot], send_sem=send_sems.at[k - 1], recv_sem=recv_sems.at[k - 1],
        device_id=_flip(me, k), device_id_type=MESH)


def _small_start(small, slots, *, name):
    def body(src_in, dst_in, send_sems, recv_sems, src, dst):
        me = _me()
        for k in range(1, N_DEV):
            _small_copy(src, dst, k, 4 * me[0] + 2 * me[1] + me[2], me, send_sems, recv_sems).start()

    arrays = [small, slots]
    return pl.pallas_call(
        body, name=name, in_specs=[HBM, HBM], out_specs=[SEM, SEM, HBM, HBM],
        out_shape=[pltpu.SemaphoreType.DMA((N_DEV - 1,)), pltpu.SemaphoreType.DMA((N_DEV - 1,))]
        + [pltpu.HBM(a.shape, a.dtype) for a in arrays],
        input_output_aliases={0: 2, 1: 3},
        compiler_params=pltpu.CompilerParams(has_side_effects=EFFECT),
    )(*[pltpu.with_memory_space_constraint(a, pltpu.HBM) for a in arrays])


def _small_wait(send_sems, recv_sems, small, slots, after, *, name):
    def body(src_in, dst_in, ss, rs, after_ref, src, dst):
        me = _me()
        for k in range(1, N_DEV):
            px, py, pc = _flip(me, k)
            cp = _small_copy(src, dst, k, 4 * px + 2 * py + pc, me, ss, rs)
            cp.wait_send()
            cp.wait_recv()

    return pl.pallas_call(
        body, name=name, in_specs=[HBM, HBM, SEM, SEM, pl.BlockSpec(memory_space=pl.ANY)], out_specs=[HBM, HBM],
        out_shape=[pltpu.HBM(small.shape, small.dtype), pltpu.HBM(slots.shape, slots.dtype)],
        input_output_aliases={0: 0, 1: 1},
        compiler_params=pltpu.CompilerParams(has_side_effects=EFFECT),
    )(small, slots, send_sems, recv_sems, after)


def _pair_gather(halves, *, name):
    n = len(halves)

    def body(*refs):
        bufs = refs[n:2 * n]
        send_sems, recv_sems = refs[2 * n:]
        me = _me()
        c = me[2]
        sib = _flip(me, 1)

        def copy(i, half):
            return pltpu.make_async_remote_copy(
                src_ref=bufs[i].at[:, half], dst_ref=bufs[i].at[:, half], send_sem=send_sems.at[i],
                recv_sem=recv_sems.at[i], device_id=sib, device_id_type=MESH)

        sends = [copy(i, c) for i in range(n)]
        for cp in sends:
            cp.start()
        for i in range(n):
            copy(i, 1 - c).wait_recv()
        for cp in sends:
            cp.wait_send()

    return pl.pallas_call(
        body, name=name,
        in_specs=[HBM] * n, out_specs=[HBM] * n,
        out_shape=[jax.ShapeDtypeStruct(h.shape, h.dtype) for h in halves],
        input_output_aliases={i: i for i in range(n)},
        scratch_shapes=[pltpu.SemaphoreType.DMA((n,)), pltpu.SemaphoreType.DMA((n,))],
    )(*halves)


def _local_step(x, tgt, P, fetch, ship, n_layers):
    T, D = x.shape
    P = dict(P)
    W = {}
    depth = P["ffn_conv_b"].shape[0]
    n_a = P["a_w_s"].shape[0]
    alpha = (2 * depth) ** 0.25
    H = P["rel_table"].shape[1] // N_GROUPS
    HD = H * HEAD_DIM
    dils = [d for _, d in DILATED_GROUPS]
    assert T % (dils[-1] * BLK) == 0

    def arrive(stage, after):
        w_new, p_new = fetch(stage, after)
        W.update(w_new)
        P.update(p_new)

    bias = _band_bias(P["rel_table"], H, name="band_bias")
    saved = []
    xf, xb = x, x.astype(BF16)
    kg = vg = None
    for i in range(depth):
        s = {"x": xf, "xb": xb}
        arrive(f"{i}a", xf)
        if i < n_a:
            s["zp"] = _mm(xb, W["a_w_in", i], out_dtype=BF16, name=f"a{i}_in")
            arrive(f"{i}b", s["zp"])
            s["y"] = _sgu_fwd(s["zp"], P["a_w_s"][i], P["a_b_s"][i], P["a_ln_g"][i], P["a_ln_b"][i], name=f"a{i}_sgu")
            s["h"] = _mm(s["y"], W["a_w_out", i], name=f"a{i}_out")
        else:
            j = i - n_a
            if j == 0:
                kv = _mm(xb, W["kv_w", 0], out_dtype=BF16, name="kv_proj")
                kg = [(kv, 0) if d == 1 else (_to_group(kv[:, :HD], d), 0) for d in dils]
                vg = [(kv, 1) if d == 1 else (_to_group(kv[:, HD:], d), 0) for d in dils]
            q = _mm(xb, W["b_w_q", j], out_dtype=BF16, name=f"b{j}_q")
            s["qg"] = [(q, g) if d == 1 else (_to_group(q[:, g * HD:(g + 1) * HD], d), 0) for g, d in enumerate(dils)]
            os, ls = [], []
            for g, d in enumerate(dils):
                o_g, l_g = _attn_fwd(s["qg"][g], kg[g], vg[g], bias, g, d, name=f"b{j}_attn{g}")
                os.append(_from_group(o_g, d))
                ls.append(_from_group(l_g, d))
            s["oc"], s["ob"], s["lse"] = _attn_combine(os, ls, name=f"b{j}_comb")
            arrive(f"{i}b", s["ob"])
            s["h"] = _mm(s["ob"], W["b_w_o", j], name=f"b{j}_o")
        s["x1"], s["x1b"] = _add_ln(xf, s["h"], P["ln_g"][i, 0], P["ln_b"][i, 0], alpha, name=f"l{i}_ln1")
        arrive(f"{i}c", s["x1b"])
        s["hup"] = _mm(s["x1b"], W["ffn_w_up", i], out_dtype=BF16, name=f"l{i}_up")
        arrive(f"{i}d", s["hup"])
        s["act"] = _convglu_fwd(s["hup"], P["ffn_conv_w"][i], P["ffn_conv_b"][i], name=f"l{i}_glu")
        s["f"] = _mm(s["act"], W["ffn_w_down", i], name=f"l{i}_down")
        xf, xb = _add_ln(s["x1"], s["f"], P["ln_g"][i, 1], P["ln_b"][i, 1], alpha, name=f"l{i}_ln2")
        saved.append(s)

    G, loss_row = _loss_grad(xf, tgt, name="loss")

    gw = {}

    def dw(key, layer, a, b, name, **kw):
        gw[key] = _mm(a, b, ta=True, out_dtype=BF16, out_into=(gw.get(key), n_layers[key], layer), name=name, **kw)

    gp = {k: [None] * n_a for k in ("a_ln_g", "a_ln_b", "a_w_s", "a_b_s")}
    gp.update({k: [None] * depth for k in ("ffn_conv_w", "ffn_conv_b", "ln_g", "ln_b")})
    dk_parts, dv_parts, dbias_parts = [], [], []
    for i in reversed(range(depth)):
        s = saved[i]
        dr2, dr2b, dg2, db2 = _ln_bwd(G, s["x1"], s["f"], P["ln_g"][i, 1], alpha, name=f"l{i}_ln2_bwd")
        dw("ffn_w_down", i, s["act"], dr2b, f"l{i}_down_dw")
        dact = _mm(dr2b, W["ffn_w_down", i], tb=True, out_dtype=BF16, name=f"l{i}_down_dx")
        dhup, dcw, dcb = _convglu_bwd(s["hup"], dact, P["ffn_conv_w"][i], P["ffn_conv_b"][i], name=f"l{i}_glu_bwd")
        gp["ffn_conv_w"][i] = dcw.transpose(1, 0, 2).reshape(dcw.shape[1], -1)
        gp["ffn_conv_b"][i] = dcb.reshape(-1)
        dw("ffn_w_up", i, s["x1b"], dhup, f"l{i}_up_dw", b_halves=True)
        token = ship(gw, [("ffn_w_down", i), ("ffn_w_up", i)])
        G1 = _mm(dhup, W["ffn_w_up", i], tb=True, a_halves=True, add=dr2, add_scale=alpha, after=token,
                 name=f"l{i}_up_dx")
        dr1, dr1b, dg1, db1 = _ln_bwd(G1, s["x"], s["h"], P["ln_g"][i, 0], alpha, name=f"l{i}_ln1_bwd")
        gp["ln_g"][i] = jnp.concatenate([dg1, dg2], axis=0)
        gp["ln_b"][i] = jnp.concatenate([db1, db2], axis=0)
        if i < n_a:
            dw("a_w_out", i, s["y"], dr1b, f"a{i}_out_dw")
            token = ship(gw, [("a_w_out", i)])
            dy = _mm(dr1b, W["a_w_out", i], tb=True, out_dtype=BF16, after=token, name=f"a{i}_out_dx")
            dzp, dws, dbs, dlg, dlb = _sgu_bwd(s["zp"], dy, P["a_w_s"][i], P["a_b_s"][i], P["a_ln_g"][i],
                                               P["a_ln_b"][i], name=f"a{i}_sgu_bwd")
            gp["a_w_s"][i], gp["a_b_s"][i] = dws, dbs[:, :dws.shape[0]].T
            gp["a_ln_g"][i], gp["a_ln_b"][i] = dlg[0], dlb[0]
            dw("a_w_in", i, s["xb"], dzp, f"a{i}_in_dw")
            token = ship(gw, [("a_w_in", i)])
            G = _mm(dzp, W["a_w_in", i], tb=True, add=dr1, add_scale=alpha, after=token, name=f"a{i}_in_dx")
        else:
            j = i - n_a
            dw("b_w_o", j, s["ob"], dr1b, f"b{j}_o_dw")
            do_tok = _mm(dr1b, W["b_w_o", j], tb=True, name=f"b{j}_o_dx")
            delta, dob = _attn_bwd_prep(do_tok, s["oc"], name=f"b{j}_prep")
            dqs, dbs = [], []
            for g, d in enumerate(dils):
                dq, dk, dv, db = _attn_bwd(s["qg"][g], kg[g], vg[g], bias, g, d, _to_group(dob, d),
                                           _to_group(s["lse"], d), _to_group(delta, d), name=f"b{j}_attn{g}_bwd")
                dqs.append(_from_group(dq, d))
                dk_parts.append(_from_group(dk, d))
                dv_parts.append(_from_group(dv, d))
                dbs.append(db)
            dbias_parts.append(jnp.stack(dbs))
            dq_tok = jnp.concatenate(dqs, axis=1)
            dw("b_w_q", j, s["xb"], dq_tok, f"b{j}_q_dw")
            token = ship(gw, [("b_w_o", j), ("b_w_q", j)])
            G = _mm(dq_tok, W["b_w_q", j], tb=True, add=dr1, add_scale=alpha, after=token, name=f"b{j}_q_dx")
            if j == 0:
                dkv = jnp.concatenate([_addn(dk_parts, BF16, name="dk_sum"), _addn(dv_parts, BF16, name="dv_sum")], axis=1)
                dw("kv_w", 0, s["xb"], dkv, "kv_dw")
                token = ship(gw, [("kv_w", 0)])
                G = _mm(dkv, W["kv_w", 0], tb=True, add=G, add_scale=1.0, after=token, name="kv_dx")
    rel = _rel_grad(dbias_parts, name="rel_grad")
    grel = rel[:, :, :H].transpose(1, 0, 2).reshape(REL_BUCKETS, N_GROUPS * H)
    gsmall = {k: jnp.stack(v) for k, v in gp.items()}
    gsmall["rel_table"] = grel
    return loss_row, G, gw, gsmall


BIG = (("a_w_in", COL), ("a_w_out", ROW), ("kv_w", ROW), ("b_w_q", COL), ("b_w_o", COL),
       ("ffn_w_up", COL), ("ffn_w_down", ROW))
SMALL_SHARDED = ("a_ln_g", "a_ln_b", "ffn_conv_w", "ln_g", "ln_b")
SMALL_REPLICATED = ("a_w_s", "a_b_s", "rel_table", "ffn_conv_b")
WEIGHTS = ("a_w_in", "a_ln_g", "a_ln_b", "a_w_s", "a_b_s", "a_w_out", "kv_w", "b_w_q", "b_w_o", "rel_table",
           "ffn_w_up", "ffn_conv_w", "ffn_conv_b", "ffn_w_down", "ln_g", "ln_b")


def kernel(x, a_w_in, a_ln_g, a_ln_b, a_w_s, a_b_s, a_w_out, kv_w, b_w_q, b_w_o, rel_table, ffn_w_up, ffn_conv_w, ffn_conv_b, ffn_w_down, ln_g, ln_b, loss_target, m_a_w_in, m_a_ln_g, m_a_ln_b, m_a_w_s, m_a_b_s, m_a_w_out, m_kv_w, m_b_w_q, m_b_w_o, m_rel_table, m_ffn_w_up, m_ffn_conv_w, m_ffn_conv_b, m_ffn_w_down, m_ln_g, m_ln_b, v_a_w_in, v_a_ln_g, v_a_ln_b, v_a_w_s, v_a_b_s, v_a_w_out, v_kv_w, v_b_w_q, v_b_w_o, v_rel_table, v_ffn_w_up, v_ffn_conv_w, v_ffn_conv_b, v_ffn_w_down, v_ln_g, v_ln_b):
    w = dict(a_w_in=a_w_in, a_ln_g=a_ln_g, a_ln_b=a_ln_b, a_w_s=a_w_s, a_b_s=a_b_s, a_w_out=a_w_out, kv_w=kv_w,
             b_w_q=b_w_q, b_w_o=b_w_o, rel_table=rel_table, ffn_w_up=ffn_w_up, ffn_conv_w=ffn_conv_w,
             ffn_conv_b=ffn_conv_b, ffn_w_down=ffn_w_down, ln_g=ln_g, ln_b=ln_b)
    m = dict(a_w_in=m_a_w_in, a_ln_g=m_a_ln_g, a_ln_b=m_a_ln_b, a_w_s=m_a_w_s, a_b_s=m_a_b_s, a_w_out=m_a_w_out,
             kv_w=m_kv_w, b_w_q=m_b_w_q, b_w_o=m_b_w_o, rel_table=m_rel_table, ffn_w_up=m_ffn_w_up,
             ffn_conv_w=m_ffn_conv_w, ffn_conv_b=m_ffn_conv_b, ffn_w_down=m_ffn_w_down, ln_g=m_ln_g, ln_b=m_ln_b)
    v = dict(a_w_in=v_a_w_in, a_ln_g=v_a_ln_g, a_ln_b=v_a_ln_b, a_w_s=v_a_w_s, a_b_s=v_a_b_s, a_w_out=v_a_w_out,
             kv_w=v_kv_w, b_w_q=v_b_w_q, b_w_o=v_b_w_o, rel_table=v_rel_table, ffn_w_up=v_ffn_w_up,
             ffn_conv_w=v_ffn_conv_w, ffn_conv_b=v_ffn_conv_b, ffn_w_down=v_ffn_w_down, ln_g=v_ln_g, ln_b=v_ln_b)
    chip = 2 * lax.axis_index("x") + lax.axis_index("y")

    big_names = [n for n, _ in BIG]
    big_axes = [a for _, a in BIG]
    where = jnp.stack([chip, lax.axis_index("c"), 2 * chip + lax.axis_index("c")]).astype(jnp.int32)
    shards = [w[n].reshape((1,) + w[n].shape) if w[n].ndim == 2 else w[n] for n in big_names]
    small_shapes = [w[n].shape for n in SMALL_SHARDED]
    small = _pack([w[n] for n in SMALL_SHARDED])
    shard_of = dict(zip(big_names, shards))
    axis_of = dict(BIG)
    n_layers = {n: s.shape[0] for n, s in shard_of.items()}
    n_a = w["a_w_s"].shape[0]
    stages = {}
    for i in range(w["ffn_w_up"].shape[0]):
        if i < n_a:
            stages[f"{i}a"], stages[f"{i}b"] = [("a_w_in", i)], [("a_w_out", i)]
        else:
            stages[f"{i}a"] = ([("kv_w", 0)] if i == n_a else []) + [("b_w_q", i - n_a)]
            stages[f"{i}b"] = [("b_w_o", i - n_a)]
        stages[f"{i}c"], stages[f"{i}d"] = [("ffn_w_up", i)], [("ffn_w_down", i)]
    stages["0a"].append(("small", 0))
    assert sum(len(st) for st in stages.values()) == sum(n_layers.values()) + 1
    flights, full_shape, all_started = {}, {}, None
    for batch, names in enumerate(([st for st in stages if st[0] == "0"], [st for st in stages if st[0] != "0"])):
        order = [key for st in names for key in stages[st]]
        placed, axes = [], []
        for n, l in order:
            if n == "small":
                placed.append(_place_shard(small[None], 0, ROW, where, F32, name="place_small"))
                axes.append(ROW)
            else:
                placed.append(_place_shard(shard_of[n], l, axis_of[n], where, BF16, name=f"place_{n}{l}",
                                           after=all_started))
                axes.append(axis_of[n])
                full_shape[n] = (n_layers[n],) + placed[-1].shape[1:]
        send_sems, recv_sems, flying, all_started = _gather_start(placed, axes, name=f"gather_start{batch}")
        for st in names:
            flights[st] = (send_sems, recv_sems, flying, axes, order)

    def fetch(stage, after):
        if stage not in stages:
            return {}, {}
        send_sems, recv_sems, flying, axes, order = flights[stage]
        if stage == "0a":
            after = all_started
        idxs = [order.index(key) for key in stages[stage]]
        landed = _gather_wait(send_sems, recv_sems, [flying[i] for i in idxs], [axes[i] for i in idxs], idxs, after,
                              name=f"gather_wait_{stage}")
        W_new, P_new = {}, {}
        for key, arr in zip(stages[stage], landed):
            if key[0] == "small":
                small_all = arr.reshape((N_CHIPS,) + small.shape)
                per_chip = [_unpack(small_all[j], small_shapes) for j in range(N_CHIPS)]
                for i, n in enumerate(SMALL_SHARDED):
                    P_new[n] = jnp.concatenate([per_chip[j][i] for j in range(N_CHIPS)], axis=-1)
            else:
                W_new[key] = arr.reshape(arr.shape[1:])
        return W_new, P_new

    slots ={n: lax.empty((N_DEV, n_layers[n]) + _part_geometry(full_shape[n], axis_of[n]), BF16) for n in big_names}
    sems, plan = [], []

    def ship(gw, items):
        names = [n for n, _ in items]
        send, recv, grads, landing, token = _scatter_start(
            [gw[n] for n in names], [slots[n] for n in names], [l for _, l in items], [axis_of[n] for n in names],
            name="scatter_start_" + "_".join(f"{n}{l}" for n, l in items))
        for n, g, s in zip(names, grads, landing):
            gw[n], slots[n] = g, s
        sems.append((send, recv))
        plan.append([(big_names.index(n), l) for n, l in items])
        return token

    loss_row, grad_x, gw, gsmall = _local_step(x[0], loss_target[0], {n: w[n] for n in SMALL_REPLICATED}, fetch,
                                               ship, n_layers)

    small_names = list(SMALL_SHARDED) + list(SMALL_REPLICATED)
    small_pack = _pack([gsmall[n] for n in small_names] + [loss_row[0, :1]])
    small_sems = _small_start(small_pack, lax.empty((N_DEV,) + small_pack.shape, F32), name="small_start")
    own, landed = _scatter_wait(sems, [gw[n] for n in big_names], [slots[n] for n in big_names], big_axes, plan,
                                name="scatter_wait")
    sums = [_sum_parts(b, g, ax, where, name=f"sum_{n}") for (n, ax), b, g in zip(BIG, landed, own)]
    pairs = _pair_gather(sums, name="pair_gather")
    grad = {n: p.reshape(w[n].shape) for n, p in zip(big_names, pairs)}

    delta, new_m, new_v = {}, {}, {}
    for n, _ in BIG:
        delta[n], new_m[n], new_v[n] = _adamw(w[n], grad[n], m[n], v[n], name=f"adamw_{n}")
    small_own, small_slots = _small_wait(*small_sems, new_v[big_names[-1]], name="small_wait")
    small_sum = _sum_small(small_slots, small_own, where, name="sum_small")
    full_small = _unpack(small_sum, [gsmall[n].shape for n in small_names] + [(1,)])
    loss = full_small[-1][0]
    for n, gfull in zip(small_names, full_small):
        if n in SMALL_SHARDED:
            width = w[n].shape[-1]
            grad[n] = lax.dynamic_slice_in_dim(gfull, chip * width, width, axis=-1)
        else:
            grad[n] = gfull
    shapes = [w[n].shape for n in small_names]
    packed = _adamw(_pack([w[n] for n in small_names]), _pack([grad[n] for n in small_names]),
                    _pack([m[n] for n in small_names]), _pack([v[n] for n in small_names]), name="adamw_small")
    for out, res in zip((delta, new_m, new_v), packed):
        for n, a in zip(small_names, _unpack(res, shapes)):
            out[n] = a

    return (loss, grad_x[None], *[grad[n] for n in WEIGHTS], *[delta[n] for n in WEIGHTS],
            *[new_m[n] for n in WEIGHTS], *[new_v[n] for n in WEIGHTS])
```

```python
import functools
import math

import numpy as np
import jax
import jax.numpy as jnp
from jax import lax
from jax.experimental import pallas as pl
from jax.experimental.pallas import tpu as pltpu

F32 = jnp.float32
BF16 = jnp.bfloat16
MESH = pl.DeviceIdType.MESH

CHUNK = 128
HEAD_DIM = 64
DILATED_GROUPS = ((128, 1), (512, 4), (2048, 16))
N_GROUPS = len(DILATED_GROUPS)
BLK = 128
REL_BUCKETS = 32
REL_MAX_DIST = 2048
LN_EPS = 1e-5
NEG = -1e30
ADAM_LR = 0.001
ADAM_B1 = 0.9
ADAM_B2 = 0.999
ADAM_EPS = 1e-08
ADAM_WD = 0.01
ADAM_STEP = 10

N_CHIPS = 4
N_DEV = 8
LANES = 128
SUBLANES = 8
VMEM_LIMIT = 48 * 1024 * 1024

_SQRT_HALF = 0.7071067811865476
_INV_SQRT_2PI = 0.3989422804014327


def _cp(sem=None, vmem=VMEM_LIMIT):
    return pltpu.CompilerParams(dimension_semantics=sem, vmem_limit_bytes=vmem)


def _tile(dim, target, align=LANES):
    if dim <= target:
        return dim
    t = (target // align) * align
    while t >= align:
        if dim % t == 0:
            return t
        t -= align
    return dim


def _gelu(x):
    return 0.5 * x * (1.0 + lax.erf(x * _SQRT_HALF))


def _gelu_grad(x):
    return 0.5 * (1.0 + lax.erf(x * _SQRT_HALF)) + x * (_INV_SQRT_2PI * jnp.exp(-0.5 * x * x))


def _dot(a, b, ca, cb):
    return lax.dot_general(a, b, (((ca,), (cb,)), ((), ())), preferred_element_type=F32)


def _mn_tile(dim):
    return max(_tile(dim, 1024), _tile(dim, 1408))


def _mm(a, b, *, name, ta=False, tb=False, out_dtype=F32, tiles=None,
        a_lead=(), b_lead=(), a_halves=False, b_halves=False, add=None, add_scale=1.0, out_into=None, after=None):
    a2, b2 = a.shape[-2:], b.shape[-2:]
    M, K = (a2[1], a2[0]) if ta else a2
    Kb, N = (b2[1], b2[0]) if tb else b2
    if a_halves:
        K = 2 * K
    if b_halves:
        N = 2 * N
    assert K == Kb, (name, a.shape, b.shape)
    if tiles is None:
        k_target = 1408 if a_halves else (K if K <= 2816 else 2048)
        tm, tn = _mn_tile(M), _mn_tile(N // 2 if b_halves else N)
        tk = _tile(K // 2 if a_halves else K, k_target)
    else:
        tm = _tile(M, tiles[0])
        tn = _tile(N // 2 if b_halves else N, tiles[1])
        tk = _tile(K // 2 if a_halves else K, tiles[2])
    nm, nn, nk = M // tm, N // tn, K // tk
    nkh, nnh = nk // 2, nn // 2

    def lead(idx, rest):
        return tuple(idx) + tuple(rest)

    sq_a = (None,) * (a.ndim - 2)
    sq_b = (None,) * (b.ndim - 2)
    if a_halves:
        assert not ta and a.ndim == 3
        a_spec = pl.BlockSpec((None, tm, tk), lambda i, j, k: (k // nkh, i, k % nkh))
    elif ta:
        a_spec = pl.BlockSpec(sq_a + (tk, tm), lambda i, j, k: lead(a_lead, (k, i)))
    else:
        a_spec = pl.BlockSpec(sq_a + (tm, tk), lambda i, j, k: lead(a_lead, (i, k)))
    if b_halves:
        assert not tb and b.ndim == 3
        b_spec = pl.BlockSpec((None, tk, tn), lambda i, j, k: (j // nnh, k, j % nnh))
    elif tb:
        b_spec = pl.BlockSpec(sq_b + (tn, tk), lambda i, j, k: lead(b_lead, (j, k)))
    else:
        b_spec = pl.BlockSpec(sq_b + (tk, tn), lambda i, j, k: lead(b_lead, (k, j)))
    mn_spec = pl.BlockSpec((tm, tn), lambda i, j, k: (i, j))
    in_specs = [a_spec, b_spec]
    args = [a, b]
    if add is not None:
        in_specs.append(mn_spec)
        args.append(add)
    aliases = {}
    if out_into is None:
        o_spec, out_shape = mn_spec, jax.ShapeDtypeStruct((M, N), out_dtype)
    else:
        buf, n_layers, layer = out_into
        o_spec = pl.BlockSpec((None, tm, tn), lambda i, j, k: (layer, i, j))
        out_shape = jax.ShapeDtypeStruct((n_layers, M, N), out_dtype)
        if buf is not None:
            aliases = {len(args): 0}
            in_specs.append(pl.BlockSpec(memory_space=pl.ANY))
            args.append(buf)
    if after is not None:
        in_specs.append(pl.BlockSpec(memory_space=pl.ANY))
        args.append(after)
    n_in = len(args)
    ca, cb = (0 if ta else 1), (1 if tb else 0)

    def body(*refs):
        a_ref, b_ref = refs[:2]
        add_ref = refs[2] if add is not None else None
        o_ref = refs[n_in]

        def finish(acc):
            if add_ref is not None:
                acc = acc + add_scale * add_ref[...]
            o_ref[...] = acc.astype(o_ref.dtype)

        if nk == 1:
            finish(_dot(a_ref[...], b_ref[...], ca, cb))
        else:
            acc_ref = refs[-1]
            k = pl.program_id(2)

            @pl.when(k == 0)
            def _():
                acc_ref[...] = jnp.zeros_like(acc_ref)

            acc_ref[...] += _dot(a_ref[...], b_ref[...], ca, cb)

            @pl.when(k == nk - 1)
            def _():
                finish(acc_ref[...])

    return pl.pallas_call(
        body, name=name, grid=(nm, nn, nk),
        in_specs=in_specs, out_specs=o_spec, out_shape=out_shape,
        input_output_aliases=aliases,
        scratch_shapes=[] if nk == 1 else [pltpu.VMEM((tm, tn), F32)],
        compiler_params=_cp(("parallel", "parallel", "arbitrary")),
    )(*args)


def _ln_stats(r):
    mu = jnp.mean(r, axis=-1, keepdims=True)
    xc = r - mu
    var = jnp.mean(xc * xc, axis=-1, keepdims=True)
    rstd = lax.rsqrt(var + LN_EPS)
    return xc * rstd, rstd


def _add_ln(x, h, g, b, alpha, *, name):
    T, D = x.shape
    tr = _tile(T, 512, SUBLANES)
    row = pl.BlockSpec((tr, D), lambda i: (i, 0))
    vec = pl.BlockSpec((1, D), lambda i: (0, 0))

    def body(x_ref, h_ref, g_ref, b_ref, o_ref, ob_ref):
        xhat, _ = _ln_stats(alpha * x_ref[...] + h_ref[...])
        y = xhat * g_ref[...] + b_ref[...]
        o_ref[...] = y
        ob_ref[...] = y.astype(BF16)

    return pl.pallas_call(
        body, name=name, grid=(T // tr,),
        in_specs=[row, row, vec, vec], out_specs=[row, row],
        out_shape=[jax.ShapeDtypeStruct((T, D), F32), jax.ShapeDtypeStruct((T, D), BF16)],
        compiler_params=_cp(("parallel",)),
    )(x, h, g.reshape(1, D), b.reshape(1, D))


def _ln_bwd(dy, x, h, g, alpha, *, name):
    T, D = x.shape
    tr = _tile(T, 512, SUBLANES)
    row = pl.BlockSpec((tr, D), lambda i: (i, 0))
    vec = pl.BlockSpec((1, D), lambda i: (0, 0))

    def body(dy_ref, x_ref, h_ref, g_ref, dr_ref, drb_ref, dg_ref, db_ref):
        @pl.when(pl.program_id(0) == 0)
        def _():
            dg_ref[...] = jnp.zeros_like(dg_ref)
            db_ref[...] = jnp.zeros_like(db_ref)

        xhat, rstd = _ln_stats(alpha * x_ref[...] + h_ref[...])
        dyv = dy_ref[...]
        dxhat = dyv * g_ref[...]
        c1 = jnp.mean(dxhat, axis=-1, keepdims=True)
        c2 = jnp.mean(dxhat * xhat, axis=-1, keepdims=True)
        dr = rstd * (dxhat - c1 - xhat * c2)
        dr_ref[...] = dr
        drb_ref[...] = dr.astype(BF16)
        dg_ref[...] += jnp.sum(dyv * xhat, axis=0, keepdims=True)
        db_ref[...] += jnp.sum(dyv, axis=0, keepdims=True)

    return pl.pallas_call(
        body, name=name, grid=(T // tr,),
        in_specs=[row, row, row, vec], out_specs=[row, row, vec, vec],
        out_shape=[jax.ShapeDtypeStruct((T, D), F32), jax.ShapeDtypeStruct((T, D), BF16),
                   jax.ShapeDtypeStruct((1, D), F32), jax.ShapeDtypeStruct((1, D), F32)],
        compiler_params=_cp(("arbitrary",)),
    )(dy, x, h, g.reshape(1, D))


def _sgu_pieces(zp, lg, lb, E):
    z = _gelu(zp)
    u, v = z[:, :E], z[:, E:]
    vhat, rstd = _ln_stats(v)
    vn = vhat * lg + lb
    return u, vhat, rstd, vn


def _tril_mask():
    t = lax.broadcasted_iota(jnp.int32, (CHUNK, CHUNK), 0)
    s = lax.broadcasted_iota(jnp.int32, (CHUNK, CHUNK), 1)
    return s <= t


def _sgu_fwd(zp, ws, bs, lg, lb, *, name):
    T, E2 = zp.shape
    E = E2 // 2
    G = ws.shape[0]
    cw = E // G

    def body(zp_ref, ws_ref, bs_ref, lg_ref, lb_ref, y_ref):
        u, _, _, vn = _sgu_pieces(zp_ref[...].astype(F32), lg_ref[...], lb_ref[...], E)
        vnb = vn.astype(BF16)
        tril = _tril_mask()
        for g in range(G):
            cols = slice(g * cw, (g + 1) * cw)
            w = jnp.where(tril, ws_ref[g], 0.0).astype(BF16)
            sv = _dot(w, vnb[:, cols], 1, 0) + bs_ref[g]
            y_ref[:, cols] = (u[:, cols] * sv).astype(BF16)

    return pl.pallas_call(
        body, name=name, grid=(T // CHUNK,),
        in_specs=[pl.BlockSpec((CHUNK, E2), lambda n: (n, 0)),
                  pl.BlockSpec((G, CHUNK, CHUNK), lambda n: (0, 0, 0)),
                  pl.BlockSpec((G, CHUNK, 1), lambda n: (0, 0, 0)),
                  pl.BlockSpec((1, E), lambda n: (0, 0)),
                  pl.BlockSpec((1, E), lambda n: (0, 0))],
        out_specs=pl.BlockSpec((CHUNK, E), lambda n: (n, 0)),
        out_shape=jax.ShapeDtypeStruct((T, E), BF16),
        compiler_params=_cp(("parallel",)),
    )(zp, ws, bs.reshape(G, CHUNK, 1), lg.reshape(1, E), lb.reshape(1, E))


def _sgu_bwd(zp, dy, ws, bs, lg, lb, *, name):
    T, E2 = zp.shape
    E = E2 // 2
    G = ws.shape[0]
    cw = E // G
    assert G <= LANES

    def body(zp_ref, dy_ref, ws_ref, bs_ref, lg_ref, lb_ref,
             dzp_ref, dws_ref, dbs_ref, dlg_ref, dlb_ref, dvn_ref):
        @pl.when(pl.program_id(0) == 0)
        def _():
            dws_ref[...] = jnp.zeros_like(dws_ref)
            dbs_ref[...] = jnp.zeros_like(dbs_ref)
            dlg_ref[...] = jnp.zeros_like(dlg_ref)
            dlb_ref[...] = jnp.zeros_like(dlb_ref)

        zpv = zp_ref[...].astype(F32)
        lgv = lg_ref[...]
        u, vhat, rstd, vn = _sgu_pieces(zpv, lgv, lb_ref[...], E)
        gp = _gelu_grad(zpv)
        vnb = vn.astype(BF16)
        tril = _tril_mask()
        lane = lax.broadcasted_iota(jnp.int32, (CHUNK, LANES), 1)
        dbs = dbs_ref[...]
        for g in range(G):
            cols = slice(g * cw, (g + 1) * cw)
            w = jnp.where(tril, ws_ref[g], 0.0).astype(BF16)
            sv = _dot(w, vnb[:, cols], 1, 0) + bs_ref[g]
            dyg = dy_ref[:, cols].astype(F32)
            dzp_ref[:, cols] = (dyg * sv * gp[:, cols]).astype(BF16)
            dsv = dyg * u[:, cols]
            dsvb = dsv.astype(BF16)
            dws_ref[g] += jnp.where(tril, _dot(dsvb, vnb[:, cols], 1, 1), 0.0)
            dvn_ref[:, cols] = _dot(w, dsvb, 0, 0)
            dbs = dbs + jnp.where(lane == g, jnp.sum(dsv, axis=1, keepdims=True), 0.0)
        dbs_ref[...] = dbs
        dvn = dvn_ref[...]
        dlg_ref[...] += jnp.sum(dvn * vhat, axis=0, keepdims=True)
        dlb_ref[...] += jnp.sum(dvn, axis=0, keepdims=True)
        dvhat = dvn * lgv
        c1 = jnp.mean(dvhat, axis=-1, keepdims=True)
        c2 = jnp.mean(dvhat * vhat, axis=-1, keepdims=True)
        dv = rstd * (dvhat - c1 - vhat * c2)
        dzp_ref[:, E:] = (dv * gp[:, E:]).astype(BF16)

    vecE = pl.BlockSpec((1, E), lambda n: (0, 0))
    return pl.pallas_call(
        body, name=name, grid=(T // CHUNK,),
        in_specs=[pl.BlockSpec((CHUNK, E2), lambda n: (n, 0)),
                  pl.BlockSpec((CHUNK, E), lambda n: (n, 0)),
                  pl.BlockSpec((G, CHUNK, CHUNK), lambda n: (0, 0, 0)),
                  pl.BlockSpec((G, CHUNK, 1), lambda n: (0, 0, 0)),
                  vecE, vecE],
        out_specs=[pl.BlockSpec((CHUNK, E2), lambda n: (n, 0)),
                   pl.BlockSpec((G, CHUNK, CHUNK), lambda n: (0, 0, 0)),
                   pl.BlockSpec((CHUNK, LANES), lambda n: (0, 0)),
                   vecE, vecE],
        out_shape=[jax.ShapeDtypeStruct((T, E2), BF16),
                   jax.ShapeDtypeStruct((G, CHUNK, CHUNK), F32),
                   jax.ShapeDtypeStruct((CHUNK, LANES), F32),
                   jax.ShapeDtypeStruct((1, E), F32), jax.ShapeDtypeStruct((1, E), F32)],
        scratch_shapes=[pltpu.VMEM((CHUNK, E), F32)],
        compiler_params=_cp(("arbitrary",)),
    )(zp, dy, ws, bs.reshape(G, CHUNK, 1), lg.reshape(1, E), lb.reshape(1, E))


def _delay(h, k):
    t = lax.broadcasted_iota(jnp.int32, h.shape, 0)
    return jnp.where(t >= k, pltpu.roll(h, k, 0), 0.0)


def _advance(d, k):
    T = d.shape[0]
    t = lax.broadcasted_iota(jnp.int32, d.shape, 0)
    return jnp.where(t < T - k, pltpu.roll(d, T - k, 0), 0.0)


def _conv3(h, w_ref, b_ref):
    return w_ref[2:3, :] * h + w_ref[1:2, :] * _delay(h, 1) + w_ref[0:1, :] * _delay(h, 2) + b_ref[...]


def _convglu_fwd(hup, cw, cb, *, name):
    T, F2 = hup.shape
    F = F2 // 2
    tc = LANES
    nt = F // tc

    def body(ha_ref, hg_ref, wa_ref, wg_ref, ba_ref, bg_ref, o_ref):
        a = _conv3(ha_ref[...].astype(F32), wa_ref, ba_ref)
        g = _conv3(hg_ref[...].astype(F32), wg_ref, bg_ref)
        o_ref[...] = (_gelu(a) * g).astype(BF16)

    col = lambda off: pl.BlockSpec((T, tc), lambda j: (0, j + off))
    w3 = lambda off: pl.BlockSpec((3, tc), lambda j: (0, j + off))
    b1 = lambda off: pl.BlockSpec((1, tc), lambda j: (0, j + off))
    return pl.pallas_call(
        body, name=name, grid=(nt,),
        in_specs=[col(0), col(nt), w3(0), w3(nt), b1(0), b1(nt)],
        out_specs=pl.BlockSpec((T, tc), lambda j: (0, j)),
        out_shape=jax.ShapeDtypeStruct((T, F), BF16),
        compiler_params=_cp(("parallel",)),
    )(hup, hup, cw, cw, cb.reshape(1, F2), cb.reshape(1, F2))


def _convglu_bwd(hup, dact, cw, cb, *, name):
    T, F2 = hup.shape
    F = F2 // 2
    tc = LANES
    nt = F // tc

    def half(h, w_ref, d, dh_ref, dw_ref, db_ref, i):
        d1, d2 = _advance(d, 1), _advance(d, 2)
        dh_ref[i] = (w_ref[2:3, :] * d + w_ref[1:2, :] * d1 + w_ref[0:1, :] * d2).astype(BF16)
        dw_ref[i, 0:1, :] = jnp.sum(d2 * h, axis=0, keepdims=True)
        dw_ref[i, 1:2, :] = jnp.sum(d1 * h, axis=0, keepdims=True)
        dw_ref[i, 2:3, :] = jnp.sum(d * h, axis=0, keepdims=True)
        db_ref[i] = jnp.sum(d, axis=0, keepdims=True)

    def body(ha_ref, hg_ref, d_ref, wa_ref, wg_ref, ba_ref, bg_ref, dh_ref, dw_ref, db_ref):
        ha, hg = ha_ref[...].astype(F32), hg_ref[...].astype(F32)
        a = _conv3(ha, wa_ref, ba_ref)
        g = _conv3(hg, wg_ref, bg_ref)
        d = d_ref[...].astype(F32)
        half(ha, wa_ref, d * g * _gelu_grad(a), dh_ref, dw_ref, db_ref, 0)
        half(hg, wg_ref, d * _gelu(a), dh_ref, dw_ref, db_ref, 1)

    col = lambda off: pl.BlockSpec((T, tc), lambda j: (0, j + off))
    w3 = lambda off: pl.BlockSpec((3, tc), lambda j: (0, j + off))
    b1 = lambda off: pl.BlockSpec((1, tc), lambda j: (0, j + off))
    return pl.pallas_call(
        body, name=name, grid=(nt,),
        in_specs=[col(0), col(nt), col(0), w3(0), w3(nt), b1(0), b1(nt)],
        out_specs=[pl.BlockSpec((2, T, tc), lambda j: (0, 0, j)),
                   pl.BlockSpec((2, 3, tc), lambda j: (0, 0, j)),
                   pl.BlockSpec((2, 1, tc), lambda j: (0, 0, j))],
        out_shape=[jax.ShapeDtypeStruct((2, T, F), BF16),
                   jax.ShapeDtypeStruct((2, 3, F), F32),
                   jax.ShapeDtypeStruct((2, 1, F), F32)],
        compiler_params=_cp(("parallel",)),
    )(hup, hup, dact, cw, cw, cb.reshape(1, F2), cb.reshape(1, F2))


def _bucket_table():
    iq = np.arange(BLK)[:, None]
    ik = np.arange(2 * BLK)[None, :]
    delta = iq + BLK - ik
    out = []
    for win, dil in DILATED_GROUPS:
        valid = (delta >= 0) & (delta <= win // dil)
        n = (np.clip(delta, 0, None) * dil).astype(np.int32)
        max_exact = REL_BUCKETS // 2
        nf = np.maximum(n, 1).astype(np.float32)
        large = max_exact + (np.log(nf / np.float32(max_exact)) / np.float32(math.log(REL_MAX_DIST / max_exact))
                             * np.float32(REL_BUCKETS - max_exact)).astype(np.int32)
        large = np.minimum(large, REL_BUCKETS - 1)
        out.append(np.where(valid, np.where(n < max_exact, n, large), -1))
    return np.stack(out).astype(np.int32)


def _band_bias(rel_table, H, *, name):
    bkt = jnp.asarray(_bucket_table())

    def body(tab_ref, bk_ref, o_ref):
        g = pl.program_id(0)
        bk = bk_ref[...]
        prev = lax.broadcasted_iota(jnp.int32, bk.shape, 1) < BLK
        for h in range(H):
            def bucket(c, acc):
                return jnp.where(bk == c, tab_ref[c, g * H + h], acc)

            b = lax.fori_loop(0, REL_BUCKETS, bucket, jnp.zeros(bk.shape, F32))
            b = jnp.where(bk >= 0, b, NEG)
            o_ref[h, 0] = b
            o_ref[h, 1] = jnp.where(prev, NEG, b)

    return pl.pallas_call(
        body, name=name, grid=(N_GROUPS,),
        in_specs=[pl.BlockSpec(memory_space=pltpu.SMEM),
                  pl.BlockSpec((None, BLK, 2 * BLK), lambda g: (g, 0, 0))],
        out_specs=pl.BlockSpec((None, H, 2, BLK, 2 * BLK), lambda g: (g, 0, 0, 0, 0)),
        out_shape=jax.ShapeDtypeStruct((N_GROUPS, H, 2, BLK, 2 * BLK), F32),
        compiler_params=_cp(("parallel",)),
    )(rel_table, bkt)


PAIR = 2 * HEAD_DIM


def _bdot(a, b, ca, cb):
    return lax.dot_general(a, b, (((ca,), (cb,)), ((0,), (0,))), preferred_element_type=F32)


def _blocks_per_step(nb, dil):
    return 2 if (nb // dil) % 2 == 0 else 1


def _blk(i):
    return slice(i * BLK, (i + 1) * BLK)


def _heads(ref, i, H, own_columns_only):
    low = lax.broadcasted_iota(jnp.int32, (BLK, PAIR), 1) < HEAD_DIM
    out = []
    for h in range(H):
        a = ref[_blk(i), (h // 2) * PAIR:(h // 2 + 1) * PAIR]
        if own_columns_only:
            a = jnp.where(low if h % 2 == 0 else jnp.logical_not(low), a, jnp.zeros_like(a))
        out.append(a)
    return jnp.stack(out)


def _pair_sums(o):
    return [o[2 * p] + o[2 * p + 1] for p in range(o.shape[0] // 2)]


def _store_pairs(ref, i, o, dtype, scale=None):
    for p, v in enumerate(_pair_sums(o)):
        ref[_blk(i), p * PAIR:(p + 1) * PAIR] = (v if scale is None else scale * v).astype(dtype)


def _stat_tile(s):
    lane = lax.broadcasted_iota(jnp.int32, (BLK, LANES), 1)
    t = jnp.zeros((BLK, LANES), F32)
    for h in range(s.shape[0]):
        t = jnp.where(lane == h, s[h], t)
    return t


def _stat_cols(ref, i, H):
    return jnp.stack([ref[_blk(i), h:h + 1] for h in range(H)])


def _scores(qm, kp, kc, bias):
    scale = HEAD_DIM ** -0.5
    sp = _bdot(qm, kp, 2, 2) * scale + bias[:, :, :BLK]
    sc = _bdot(qm, kc, 2, 2) * scale + bias[:, :, BLK:]
    return sp, sc


def _attn_specs(nb, dil, g, H, HD, sb):
    nbg = nb // dil
    QB = _blocks_per_step(nb, dil)
    cur = lambda col: pl.BlockSpec((QB * BLK, HD), lambda b: (sb(b), col))
    prev = lambda col: pl.BlockSpec((BLK, HD), lambda b: (jnp.maximum(QB * sb(b) - 1, 0), col))
    stat = pl.BlockSpec((QB * BLK, LANES), lambda b: (sb(b), 0))
    bias0 = pl.BlockSpec((None, H, None, BLK, 2 * BLK),
                         lambda b: (g, 0, jnp.where((QB * sb(b)) % nbg == 0, 1, 0), 0, 0))
    bias1 = pl.BlockSpec((None, H, None, BLK, 2 * BLK), lambda b: (g, 0, 0, 0, 0))
    return cur, prev, stat, bias0, bias1


def _attn_fwd(q, k, v, bias, g, dil, *, name):
    (qa, qc), (ka, kc_), (va, vc_) = q, k, v
    T = qa.shape[0]
    H = bias.shape[1]
    HD = H * HEAD_DIM
    assert H % 2 == 0 and H <= LANES and T % (dil * BLK) == 0
    nb = T // BLK
    QB = _blocks_per_step(nb, dil)
    cur, prev, stat, bias0, bias1 = _attn_specs(nb, dil, g, H, HD, lambda b: b)

    def body(q_ref, kp_ref, kc_ref, vp_ref, vc_ref, b0_ref, b1_ref, o_ref, l_ref):
        for i in range(QB):
            kpi, vpi = (kp_ref, vp_ref) if i == 0 else (kc_ref, vc_ref)
            sp, sc = _scores(_heads(q_ref, i, H, True), _heads(kpi, max(i - 1, 0), H, False),
                             _heads(kc_ref, i, H, False), (b0_ref if i == 0 else b1_ref)[...])
            m = jnp.maximum(jnp.max(sp, axis=-1, keepdims=True), jnp.max(sc, axis=-1, keepdims=True))
            pp, pc = jnp.exp(sp - m), jnp.exp(sc - m)
            den = jnp.sum(pp, axis=-1, keepdims=True) + jnp.sum(pc, axis=-1, keepdims=True)
            o = (_bdot((pp / den).astype(BF16), _heads(vpi, max(i - 1, 0), H, True), 2, 1)
                 + _bdot((pc / den).astype(BF16), _heads(vc_ref, i, H, True), 2, 1))
            _store_pairs(o_ref, i, o, F32)
            l_ref[_blk(i), :] = _stat_tile(m + jnp.log(den))

    return pl.pallas_call(
        body, name=name, grid=(nb // QB,),
        in_specs=[cur(qc), prev(kc_), cur(kc_), prev(vc_), cur(vc_), bias0, bias1],
        out_specs=[cur(0), stat],
        out_shape=[jax.ShapeDtypeStruct((T, HD), F32), jax.ShapeDtypeStruct((T, LANES), F32)],
        compiler_params=_cp(("parallel",)),
    )(qa, ka, ka, va, va, bias, bias)


def _attn_combine(os, ls, *, name):
    n = len(os)
    T, HD = os[0].shape
    H = HD // HEAD_DIM
    tr = _tile(T, 512, SUBLANES)
    wide = pl.BlockSpec((tr, HD), lambda i: (i, 0))
    stat = pl.BlockSpec((tr, LANES), lambda i: (i, 0))

    def body(*refs):
        o_refs, l_refs = refs[:n], refs[n:2 * n]
        oc_ref, ocb_ref, lt_ref = refs[2 * n:]
        ls_ = [r[...] for r in l_refs]
        m = ls_[0]
        for l in ls_[1:]:
            m = jnp.maximum(m, l)
        ws = [jnp.exp(l - m) for l in ls_]
        den = ws[0]
        for w in ws[1:]:
            den = den + w
        lt_ref[...] = m + jnp.log(den)
        ws = [w / den for w in ws]
        low = lax.broadcasted_iota(jnp.int32, (tr, PAIR), 1) < HEAD_DIM
        for p in range(H // 2):
            cols = slice(p * PAIR, (p + 1) * PAIR)
            acc = jnp.zeros((tr, PAIR), F32)
            for g in range(n):
                w = jnp.where(low, ws[g][:, 2 * p:2 * p + 1], ws[g][:, 2 * p + 1:2 * p + 2])
                acc = acc + w * o_refs[g][:, cols]
            oc_ref[:, cols] = acc
            ocb_ref[:, cols] = acc.astype(BF16)

    return pl.pallas_call(
        body, name=name, grid=(T // tr,),
        in_specs=[wide] * n + [stat] * n, out_specs=[wide, wide, stat],
        out_shape=[jax.ShapeDtypeStruct((T, HD), F32), jax.ShapeDtypeStruct((T, HD), BF16),
                   jax.ShapeDtypeStruct((T, LANES), F32)],
        compiler_params=_cp(("parallel",)),
    )(*os, *ls)


def _attn_bwd_prep(do, oc, *, name):
    T, HD = do.shape
    H = HD // HEAD_DIM
    tr = _tile(T, 512, SUBLANES)
    wide = pl.BlockSpec((tr, HD), lambda i: (i, 0))
    stat = pl.BlockSpec((tr, LANES), lambda i: (i, 0))

    def body(do_ref, oc_ref, dl_ref, dob_ref):
        low = lax.broadcasted_iota(jnp.int32, (tr, PAIR), 1) < HEAD_DIM
        lane = lax.broadcasted_iota(jnp.int32, (tr, LANES), 1)
        dl = jnp.zeros((tr, LANES), F32)
        for p in range(H // 2):
            cols = slice(p * PAIR, (p + 1) * PAIR)
            d = do_ref[:, cols]
            prod = d * oc_ref[:, cols]
            dl = jnp.where(lane == 2 * p, jnp.sum(jnp.where(low, prod, 0.0), axis=-1, keepdims=True), dl)
            dl = jnp.where(lane == 2 * p + 1, jnp.sum(jnp.where(low, 0.0, prod), axis=-1, keepdims=True), dl)
            dob_ref[:, cols] = d.astype(BF16)
        dl_ref[...] = dl

    return pl.pallas_call(
        body, name=name, grid=(T // tr,),
        in_specs=[wide, wide], out_specs=[stat, wide],
        out_shape=[jax.ShapeDtypeStruct((T, LANES), F32), jax.ShapeDtypeStruct((T, HD), BF16)],
        compiler_params=_cp(("parallel",)),
    )(do, oc)


def _attn_bwd(q, k, v, bias, g, dil, do, lse, delta, *, name):
    (qa, qc), (ka, kc_), (va, vc_) = q, k, v
    T, HD = do.shape
    H = HD // HEAD_DIM
    nb = T // BLK
    scale = HEAD_DIM ** -0.5
    QB = _blocks_per_step(nb, dil)
    ns = nb // QB
    cur, prev, stat, bias0, bias1 = _attn_specs(nb, dil, g, H, HD, lambda b: jnp.minimum(b, ns - 1))
    late = pl.BlockSpec((QB * BLK, HD), lambda b: (jnp.maximum(b - 1, 0), 0))

    def body(q_ref, kp_ref, kc_ref, vp_ref, vc_ref, b0_ref, b1_ref, do_ref, l_ref, dl_ref,
             dq_ref, dk_ref, dv_ref, db_ref, ck_ref, cv_ref):
        b = pl.program_id(0)

        @pl.when(b == 0)
        def _():
            db_ref[...] = jnp.zeros_like(db_ref)
            ck_ref[...] = jnp.zeros_like(ck_ref)
            cv_ref[...] = jnp.zeros_like(cv_ref)

        @pl.when(b < ns)
        def _():
            dk_ref[...] = ck_ref[...]
            dv_ref[...] = cv_ref[...]
            for i in range(QB):
                kpi, vpi, ip = (kp_ref, vp_ref, 0) if i == 0 else (kc_ref, vc_ref, i - 1)
                qm, dom = _heads(q_ref, i, H, True), _heads(do_ref, i, H, True)
                sp, sc = _scores(qm, _heads(kpi, ip, H, False), _heads(kc_ref, i, H, False),
                                 (b0_ref if i == 0 else b1_ref)[...])
                l, dl = _stat_cols(l_ref, i, H), _stat_cols(dl_ref, i, H)
                pp, pc = jnp.exp(sp - l), jnp.exp(sc - l)
                dsp = pp * (_bdot(dom, _heads(vpi, ip, H, False), 2, 2) - dl)
                dsc = pc * (_bdot(dom, _heads(vc_ref, i, H, False), 2, 2) - dl)
                db_ref[:, :, :BLK] += dsp
                db_ref[:, :, BLK:] += dsc
                dspb, dscb = dsp.astype(BF16), dsc.astype(BF16)
                _store_pairs(dq_ref, i, _bdot(dspb, _heads(kpi, ip, H, True), 2, 1)
                             + _bdot(dscb, _heads(kc_ref, i, H, True), 2, 1), BF16, scale)
                dkp = _pair_sums(_bdot(dspb, qm, 1, 1))
                dkc = _pair_sums(_bdot(dscb, qm, 1, 1))
                dvp = _pair_sums(_bdot(pp.astype(BF16), dom, 1, 1))
                dvc = _pair_sums(_bdot(pc.astype(BF16), dom, 1, 1))
                for p in range(H // 2):
                    cols = slice(p * PAIR, (p + 1) * PAIR)
                    if i == 0:
                        dk_ref[_blk(QB - 1), cols] += scale * dkp[p]
                        dv_ref[_blk(QB - 1), cols] += dvp[p]
                    else:
                        ck_ref[_blk(i - 1), cols] += scale * dkp[p]
                        cv_ref[_blk(i - 1), cols] += dvp[p]
                    ck_ref[_blk(i), cols] = scale * dkc[p]
                    cv_ref[_blk(i), cols] = dvc[p]

        @pl.when(b == ns)
        def _():
            dk_ref[...] = ck_ref[...]
            dv_ref[...] = cv_ref[...]

    f32 = jax.ShapeDtypeStruct((T, HD), F32)
    return pl.pallas_call(
        body, name=name, grid=(ns + 1,),
        in_specs=[cur(qc), prev(kc_), cur(kc_), prev(vc_), cur(vc_), bias0, bias1, cur(0), stat, stat],
        out_specs=[cur(0), late, late, pl.BlockSpec((H, BLK, 2 * BLK), lambda b: (0, 0, 0))],
        out_shape=[jax.ShapeDtypeStruct((T, HD), BF16), f32, f32,
                   jax.ShapeDtypeStruct((H, BLK, 2 * BLK), F32)],
        scratch_shapes=[pltpu.VMEM((QB * BLK, HD), F32), pltpu.VMEM((QB * BLK, HD), F32)],
        compiler_params=_cp(("arbitrary",)),
    )(qa, ka, ka, va, va, bias, bias, do, lse, delta)


def _rel_grad(dbs, *, name):
    H = dbs[0].shape[1]
    n = len(dbs)
    assert H <= LANES
    bkt = jnp.asarray(_bucket_table())

    def body(*refs):
        bk = refs[n][...]
        o_ref = refs[n + 1]
        db = refs[0][...]
        for r in refs[1:n]:
            db = db + r[...]
        row = lax.broadcasted_iota(jnp.int32, (REL_BUCKETS, LANES), 0)
        col = lax.broadcasted_iota(jnp.int32, (REL_BUCKETS, LANES), 1)

        def bucket(c, acc):
            sel = bk == c
            for h in range(H):
                val = jnp.sum(jnp.where(sel, db[h], 0.0))
                acc = jnp.where((row == c) & (col == h), val, acc)
            return acc

        o_ref[...] = lax.fori_loop(0, REL_BUCKETS, bucket, jnp.zeros((REL_BUCKETS, LANES), F32))

    dspec = pl.BlockSpec((None, H, BLK, 2 * BLK), lambda g: (g, 0, 0, 0))
    return pl.pallas_call(
        body, name=name, grid=(N_GROUPS,),
        in_specs=[dspec] * n + [pl.BlockSpec((None, BLK, 2 * BLK), lambda g: (g, 0, 0))],
        out_specs=pl.BlockSpec((None, REL_BUCKETS, LANES), lambda g: (g, 0, 0)),
        out_shape=jax.ShapeDtypeStruct((N_GROUPS, REL_BUCKETS, LANES), F32),
        compiler_params=_cp(("parallel",)),
    )(*dbs, bkt)


def _to_group(a, d):
    T, C = a.shape
    return a if d == 1 else a.reshape(T // d, d, C).transpose(1, 0, 2).reshape(T, C)


def _from_group(a, d):
    T, C = a.shape
    return a if d == 1 else a.reshape(d, T // d, C).transpose(1, 0, 2).reshape(T, C)


def _rows_view(a):
    a2 = a.reshape(-1, a.shape[-1])
    R, C = a2.shape
    tr = _tile(R, max(SUBLANES, (1 << 18) // C), SUBLANES)
    return a2, R, C, tr


def _addn(xs, out_dtype, *, name):
    shape = xs[0].shape
    x2s = [_rows_view(x)[0] for x in xs]
    _, R, C, tr = _rows_view(xs[0])
    spec = pl.BlockSpec((tr, C), lambda i: (i, 0))

    def body(*refs):
        acc = refs[0][...].astype(F32)
        for r in refs[1:-1]:
            acc = acc + r[...].astype(F32)
        refs[-1][...] = acc.astype(out_dtype)

    out = pl.pallas_call(
        body, name=name, grid=(R // tr,),
        in_specs=[spec] * len(xs), out_specs=spec,
        out_shape=jax.ShapeDtypeStruct((R, C), out_dtype),
        compiler_params=_cp(("parallel",)),
    )(*x2s)
    return out.reshape(shape)


def _slot_sum(b_ref, own, my_id):
    acc = None
    for s in range(b_ref.shape[0]):
        term = jnp.where(my_id == s, own, b_ref[s].astype(F32))
        acc = term if acc is None else acc + term
    return acc


def _part_geometry(shape, axis):
    _, R, C = shape
    if axis == COL:
        return R // 2, C // N_CHIPS
    return R // (2 * N_CHIPS), C


def _sum_parts(buf, grad, axis, where, *, name):
    n, L, Rp, Cp = buf.shape
    assert (Rp, Cp) == _part_geometry(grad.shape, axis)
    tr = _tile(Rp, max(2 * SUBLANES, (1 << 17) // Cp), 2 * SUBLANES)
    nr = Rp // tr

    def own_map(l, i, w):
        if axis == COL:
            return (l, w[1] * nr + i, w[0])
        return (l, (2 * w[0] + w[1]) * nr + i, 0)

    def body(w_ref, b_ref, g_ref, o_ref):
        o_ref[...] = _slot_sum(b_ref, g_ref[...].astype(F32), w_ref[2])

    return pl.pallas_call(
        body, name=name,
        grid_spec=pltpu.PrefetchScalarGridSpec(
            num_scalar_prefetch=1, grid=(L, nr),
            in_specs=[pl.BlockSpec((n, None, tr, Cp), lambda l, i, w: (0, l, i, 0)),
                      pl.BlockSpec((None, tr, Cp), own_map)],
            out_specs=pl.BlockSpec((None, None, tr, Cp), lambda l, i, w: (l, w[1], i, 0))),
        out_shape=jax.ShapeDtypeStruct((L, 2, Rp, Cp), F32),
        compiler_params=_cp(("parallel", "parallel")),
    )(where, buf, grad)


def _sum_small(buf, own, where, *, name):
    n, R, C = buf.shape
    tr = _tile(R, 512, SUBLANES)

    def body(w_ref, b_ref, g_ref, o_ref):
        o_ref[...] = _slot_sum(b_ref, g_ref[...], w_ref[2])

    return pl.pallas_call(
        body, name=name,
        grid_spec=pltpu.PrefetchScalarGridSpec(
            num_scalar_prefetch=1, grid=(R // tr,),
            in_specs=[pl.BlockSpec((n, tr, C), lambda i, w: (0, i, 0)),
                      pl.BlockSpec((tr, C), lambda i, w: (i, 0))],
            out_specs=pl.BlockSpec((tr, C), lambda i, w: (i, 0))),
        out_shape=jax.ShapeDtypeStruct((R, C), F32),
        compiler_params=_cp(("parallel",)),
    )(where, buf, own)


def _place_shard(shard, layer, axis, where, out_dtype, *, name, after=None):
    _, R, C = shard.shape
    tr = _tile(R, max(2 * SUBLANES, (1 << 18) // C), 2 * SUBLANES)
    nr = R // tr
    if axis == COL:
        full, out_map = (1, R, C * N_CHIPS), (lambda i, w: (0, i, w[0]))
    else:
        full, out_map = (1, R * N_CHIPS, C), (lambda i, w: (0, w[0] * nr + i, 0))

    def body(w_ref, x_ref, *rest):
        rest[-1][...] = x_ref[...].astype(out_dtype)

    extra = [] if after is None else [after]
    return pl.pallas_call(
        body, name=name,
        grid_spec=pltpu.PrefetchScalarGridSpec(
            num_scalar_prefetch=1, grid=(nr,),
            in_specs=[pl.BlockSpec((None, tr, C), lambda i, w: (layer, i, 0))]
            + [pl.BlockSpec(memory_space=pl.ANY)] * len(extra),
            out_specs=pl.BlockSpec((None, tr, C), out_map)),
        out_shape=jax.ShapeDtypeStruct(full, out_dtype),
        compiler_params=_cp(("parallel",)),
    )(where, shard, *extra)


def _loss_grad(y, tgt, *, name):
    T, D = y.shape
    tr = _tile(T, 512, SUBLANES)
    row = pl.BlockSpec((tr, D), lambda i: (i, 0))

    def body(y_ref, t_ref, dy_ref, l_ref, acc_ref):
        i = pl.program_id(0)

        @pl.when(i == 0)
        def _():
            acc_ref[...] = jnp.zeros_like(acc_ref)

        err = y_ref[...] - t_ref[...]
        dy_ref[...] = err * (1.0 / D)
        acc_ref[...] += jnp.sum(err * err, axis=0, keepdims=True)

        @pl.when(i == T // tr - 1)
        def _():
            tot = jnp.sum(acc_ref[...], axis=1, keepdims=True) * (0.5 / D)
            l_ref[...] = jnp.broadcast_to(tot, (1, LANES))

    return pl.pallas_call(
        body, name=name, grid=(T // tr,),
        in_specs=[row, row],
        out_specs=[row, pl.BlockSpec((1, LANES), lambda i: (0, 0))],
        out_shape=[jax.ShapeDtypeStruct((T, D), F32), jax.ShapeDtypeStruct((1, LANES), F32)],
        scratch_shapes=[pltpu.VMEM((1, D), F32)],
        compiler_params=_cp(("arbitrary",)),
    )(y, tgt)


def _adamw(w, g, m, v, *, name):
    shape = w.shape
    w2, R, C, tr = _rows_view(w)
    spec = pl.BlockSpec((tr, C), lambda i: (i, 0))

    def body(w_ref, g_ref, m_ref, v_ref, d_ref, nm_ref, nv_ref):
        gv = g_ref[...]
        nm = ADAM_B1 * m_ref[...] + (1.0 - ADAM_B1) * gv
        nv = ADAM_B2 * v_ref[...] + (1.0 - ADAM_B2) * (gv * gv)
        m_hat = nm / (1.0 - ADAM_B1 ** ADAM_STEP)
        v_hat = nv / (1.0 - ADAM_B2 ** ADAM_STEP)
        d_ref[...] = -ADAM_LR * (m_hat / (jnp.sqrt(v_hat) + ADAM_EPS) + ADAM_WD * w_ref[...])
        nm_ref[...] = nm
        nv_ref[...] = nv

    shp = jax.ShapeDtypeStruct((R, C), F32)
    outs = pl.pallas_call(
        body, name=name, grid=(R // tr,),
        in_specs=[spec] * 4, out_specs=[spec] * 3, out_shape=[shp] * 3,
        compiler_params=_cp(("parallel",)),
    )(w2, g.reshape(R, C), m.reshape(R, C), v.reshape(R, C))
    return tuple(o.reshape(shape) for o in outs)


def _pack(arrs):
    flat = jnp.concatenate([a.reshape(-1).astype(F32) for a in arrs])
    unit = SUBLANES * LANES
    pad = (-flat.shape[0]) % unit
    return jnp.pad(flat, (0, pad)).reshape(-1, LANES)


def _unpack(buf, shapes):
    flat = buf.reshape(-1)
    out, off = [], 0
    for s in shapes:
        n = int(np.prod(s))
        out.append(flat[off:off + n].reshape(s))
        off += n
    return out


def _me():
    return lax.axis_index("x"), lax.axis_index("y"), lax.axis_index("c")


def _flip(pos, k):
    x, y, c = pos
    return (1 - x if k & 4 else x, 1 - y if k & 2 else y, 1 - c if k & 1 else c)


HBM = pl.BlockSpec(memory_space=pltpu.HBM)

COL, ROW = -1, -2


def _shard_window(ref, axis, j, n):
    start = pl.multiple_of(j * n, n)
    if axis == COL:
        return ref.at[:, :, pl.ds(start, n)]
    return ref.at[:, pl.ds(start, n), :]


GATHER_PEERS = (2, 4, 6)
SEM = pl.BlockSpec(memory_space=pltpu.SEMAPHORE)
EFFECT = pltpu.SideEffectType.DATAFLOW_SIDE_EFFECTING


def _gather_copy(buf, axis, i, pi, chip, me, send_sems, recv_sems):
    win = _shard_window(buf, axis, chip, buf.shape[axis] // N_CHIPS)
    k = len(GATHER_PEERS) * i + pi
    return pltpu.make_async_remote_copy(
        src_ref=win, dst_ref=win, send_sem=send_sems.at[k], recv_sem=recv_sems.at[k],
        device_id=_flip(me, GATHER_PEERS[pi]), device_id_type=MESH)


def _gather_start(fulls, axes, *, name):
    n = len(fulls)
    n_sem = len(GATHER_PEERS) * n

    def body(*refs):
        send_sems, recv_sems = refs[n], refs[n + 1]
        bufs = refs[n + 2:2 * n + 2]
        token = refs[2 * n + 2]
        me = _me()
        for i in range(n):
            for pi in range(len(GATHER_PEERS)):
                _gather_copy(bufs[i], axes[i], i, pi, 2 * me[0] + me[1], me, send_sems, recv_sems).start()
        token[...] = jnp.zeros_like(token)

    outs = pl.pallas_call(
        body, name=name,
        in_specs=[HBM] * n, out_specs=[SEM, SEM] + [HBM] * n + [pl.BlockSpec(memory_space=pltpu.VMEM)],
        out_shape=[pltpu.SemaphoreType.DMA((n_sem,)), pltpu.SemaphoreType.DMA((n_sem,))]
        + [pltpu.HBM(f.shape, f.dtype) for f in fulls] + [jax.ShapeDtypeStruct((SUBLANES, LANES), F32)],
        input_output_aliases={i: 2 + i for i in range(n)},
        compiler_params=pltpu.CompilerParams(has_side_effects=EFFECT),
    )(*[pltpu.with_memory_space_constraint(f, pltpu.HBM) for f in fulls])
    return outs[0], outs[1], list(outs[2:2 + n]), outs[-1]


def _gather_wait(send_sems, recv_sems, bufs, axes, idxs, after, *, name):
    m = len(bufs)

    def body(*refs):
        ss, rs = refs[m], refs[m + 1]
        outs = refs[m + 3:]
        me = _me()
        for t, i in enumerate(idxs):
            for pi, k in enumerate(GATHER_PEERS):
                px, py, _ = _flip(me, k)
                _gather_copy(outs[t], axes[t], i, pi, 2 * me[0] + me[1], me, ss, rs).wait_send()
                _gather_copy(outs[t], axes[t], i, pi, 2 * px + py, me, ss, rs).wait_recv()

    return pl.pallas_call(
        body, name=name,
        in_specs=[HBM] * m + [SEM, SEM, pl.BlockSpec(memory_space=pl.ANY)], out_specs=[HBM] * m,
        out_shape=[pltpu.HBM(b.shape, b.dtype) for b in bufs],
        input_output_aliases={t: t for t in range(m)},
        compiler_params=pltpu.CompilerParams(has_side_effects=EFFECT),
    )(*bufs, send_sems, recv_sems, after)


def _grad_part(ref, layer, axis, pos):
    px, py, pc = pos
    Rp, Cp = _part_geometry(ref.shape, axis)
    chip = 2 * px + py
    if axis == COL:
        return ref.at[layer, pl.ds(pl.multiple_of(pc * Rp, Rp), Rp), pl.ds(pl.multiple_of(chip * Cp, Cp), Cp)]
    return ref.at[layer, pl.ds(pl.multiple_of((2 * chip + pc) * Rp, Rp), Rp), :]


def _scatter_copy(grad, slots, layer, axis, i, k, src_pos, dst_slot, me, send_sems, recv_sems):
    sem = (N_DEV - 1) * i + k - 1
    return pltpu.make_async_remote_copy(
        src_ref=_grad_part(grad, layer, axis, src_pos), dst_ref=slots.at[dst_slot, layer],
        send_sem=send_sems.at[sem], recv_sem=recv_sems.at[sem],
        device_id=_flip(me, k), device_id_type=MESH)


def _scatter_start(grads, slots, layers, axes, *, name):
    n = len(grads)
    n_sem = (N_DEV - 1) * n

    def body(*refs):
        send_sems, recv_sems = refs[2 * n], refs[2 * n + 1]
        g_refs, s_refs = refs[2 * n + 2:3 * n + 2], refs[3 * n + 2:4 * n + 2]
        token = refs[4 * n + 2]
        me = _me()
        my_id = 4 * me[0] + 2 * me[1] + me[2]
        for i in range(n):
            for k in range(1, N_DEV):
                _scatter_copy(g_refs[i], s_refs[i], layers[i], axes[i], i, k, _flip(me, k), my_id, me,
                              send_sems, recv_sems).start()
        token[...] = jnp.zeros_like(token)

    arrays = list(grads) + list(slots)
    outs = pl.pallas_call(
        body, name=name,
        in_specs=[HBM] * (2 * n),
        out_specs=[SEM, SEM] + [HBM] * (2 * n) + [pl.BlockSpec(memory_space=pltpu.VMEM)],
        out_shape=[pltpu.SemaphoreType.DMA((n_sem,)), pltpu.SemaphoreType.DMA((n_sem,))]
        + [pltpu.HBM(a.shape, a.dtype) for a in arrays] + [jax.ShapeDtypeStruct((SUBLANES, LANES), F32)],
        input_output_aliases={i: 2 + i for i in range(2 * n)},
        compiler_params=pltpu.CompilerParams(has_side_effects=EFFECT),
    )(*[pltpu.with_memory_space_constraint(a, pltpu.HBM) for a in arrays])
    return outs[0], outs[1], list(outs[2:2 + n]), list(outs[2 + n:2 + 2 * n]), outs[-1]


def _scatter_wait(sems, grads, slots, axes, plan, *, name):
    n = len(grads)
    flat_sems = [s for pair in sems for s in pair]

    def body(*refs):
        sem_refs = refs[2 * n:2 * n + len(flat_sems)]
        g_refs, s_refs = refs[2 * n + len(flat_sems):3 * n + len(flat_sems)], refs[3 * n + len(flat_sems):]
        me = _me()
        for j, items in enumerate(plan):
            ss, rs = sem_refs[2 * j], sem_refs[2 * j + 1]
            for i, (a, layer) in enumerate(items):
                for k in range(1, N_DEV):
                    px, py, pc = _flip(me, k)
                    cp = _scatter_copy(g_refs[a], s_refs[a], layer, axes[a], i, k, me, 4 * px + 2 * py + pc, me, ss, rs)
                    cp.wait_send()
                    cp.wait_recv()

    arrays = list(grads) + list(slots)
    outs = pl.pallas_call(
        body, name=name,
        in_specs=[HBM] * (2 * n) + [SEM] * len(flat_sems), out_specs=[HBM] * (2 * n),
        out_shape=[pltpu.HBM(a.shape, a.dtype) for a in arrays],
        input_output_aliases={i: i for i in range(2 * n)},
        compiler_params=pltpu.CompilerParams(has_side_effects=EFFECT),
    )(*arrays, *flat_sems)
    return list(outs[:n]), list(outs[n:])


def _small_copy(src, dst, k, slot, me, send_sems, recv_sems):
    return pltpu.make_async_remote_copy(
        src_ref=src, dst_ref=dst.at[slot], send_sem=send_sems.at[k - 1], recv_sem=recv_sems.at[k - 1],
        device_id=_flip(me, k), device_id_type=MESH)


def _small_start(small, slots, *, name):
    def body(src_in, dst_in, send_sems, recv_sems, src, dst):
        me = _me()
        for k in range(1, N_DEV):
            _small_copy(src, dst, k, 4 * me[0] + 2 * me[1] + me[2], me, send_sems, recv_sems).start()

    arrays = [small, slots]
    return pl.pallas_call(
        body, name=name, in_specs=[HBM, HBM], out_specs=[SEM, SEM, HBM, HBM],
        out_shape=[pltpu.SemaphoreType.DMA((N_DEV - 1,)), pltpu.SemaphoreType.DMA((N_DEV - 1,))]
        + [pltpu.HBM(a.shape, a.dtype) for a in arrays],
        input_output_aliases={0: 2, 1: 3},
        compiler_params=pltpu.CompilerParams(has_side_effects=EFFECT),
    )(*[pltpu.with_memory_space_constraint(a, pltpu.HBM) for a in arrays])


def _small_wait(send_sems, recv_sems, small, slots, after, *, name):
    def body(src_in, dst_in, ss, rs, after_ref, src, dst):
        me = _me()
        for k in range(1, N_DEV):
            px, py, pc = _flip(me, k)
            cp = _small_copy(src, dst, k, 4 * px + 2 * py + pc, me, ss, rs)
            cp.wait_send()
            cp.wait_recv()

    return pl.pallas_call(
        body, name=name, in_specs=[HBM, HBM, SEM, SEM, pl.BlockSpec(memory_space=pl.ANY)], out_specs=[HBM, HBM],
        out_shape=[pltpu.HBM(small.shape, small.dtype), pltpu.HBM(slots.shape, slots.dtype)],
        input_output_aliases={0: 0, 1: 1},
        compiler_params=pltpu.CompilerParams(has_side_effects=EFFECT),
    )(small, slots, send_sems, recv_sems, after)


def _pair_gather(halves, *, name):
    n = len(halves)

    def body(*refs):
        bufs = refs[n:2 * n]
        send_sems, recv_sems = refs[2 * n:]
        me = _me()
        c = me[2]
        sib = _flip(me, 1)

        def copy(i, half):
            return pltpu.make_async_remote_copy(
                src_ref=bufs[i].at[:, half], dst_ref=bufs[i].at[:, half], send_sem=send_sems.at[i],
                recv_sem=recv_sems.at[i], device_id=sib, device_id_type=MESH)

        sends = [copy(i, c) for i in range(n)]
        for cp in sends:
            cp.start()
        for i in range(n):
            copy(i, 1 - c).wait_recv()
        for cp in sends:
            cp.wait_send()

    return pl.pallas_call(
        body, name=name,
        in_specs=[HBM] * n, out_specs=[HBM] * n,
        out_shape=[jax.ShapeDtypeStruct(h.shape, h.dtype) for h in halves],
        input_output_aliases={i: i for i in range(n)},
        scratch_shapes=[pltpu.SemaphoreType.DMA((n,)), pltpu.SemaphoreType.DMA((n,))],
    )(*halves)


def _local_step(x, tgt, P, fetch, ship, n_layers):
    T, D = x.shape
    P = dict(P)
    W = {}
    depth = P["ffn_conv_b"].shape[0]
    n_a = P["a_w_s"].shape[0]
    alpha = (2 * depth) ** 0.25
    H = P["rel_table"].shape[1] // N_GROUPS
    HD = H * HEAD_DIM
    dils = [d for _, d in DILATED_GROUPS]
    assert T % (dils[-1] * BLK) == 0

    def arrive(stage, after):
        w_new, p_new = fetch(stage, after)
        W.update(w_new)
        P.update(p_new)

    bias = _band_bias(P["rel_table"], H, name="band_bias")
    saved = []
    xf, xb = x, x.astype(BF16)
    kg = vg = None
    for i in range(depth):
        s = {"x": xf, "xb": xb}
        arrive(f"{i}a", xf)
        if i < n_a:
            s["zp"] = _mm(xb, W["a_w_in", i], out_dtype=BF16, name=f"a{i}_in")
            arrive(f"{i}b", s["zp"])
            s["y"] = _sgu_fwd(s["zp"], P["a_w_s"][i], P["a_b_s"][i], P["a_ln_g"][i], P["a_ln_b"][i], name=f"a{i}_sgu")
            s["h"] = _mm(s["y"], W["a_w_out", i], name=f"a{i}_out")
        else:
            j = i - n_a
            if j == 0:
                kv = _mm(xb, W["kv_w", 0], out_dtype=BF16, name="kv_proj")
                kg = [(kv, 0) if d == 1 else (_to_group(kv[:, :HD], d), 0) for d in dils]
                vg = [(kv, 1) if d == 1 else (_to_group(kv[:, HD:], d), 0) for d in dils]
            q = _mm(xb, W["b_w_q", j], out_dtype=BF16, name=f"b{j}_q")
            s["qg"] = [(q, g) if d == 1 else (_to_group(q[:, g * HD:(g + 1) * HD], d), 0) for g, d in enumerate(dils)]
            os, ls = [], []
            for g, d in enumerate(dils):
                o_g, l_g = _attn_fwd(s["qg"][g], kg[g], vg[g], bias, g, d, name=f"b{j}_attn{g}")
                os.append(_from_group(o_g, d))
                ls.append(_from_group(l_g, d))
            s["oc"], s["ob"], s["lse"] = _attn_combine(os, ls, name=f"b{j}_comb")
            arrive(f"{i}b", s["ob"])
            s["h"] = _mm(s["ob"], W["b_w_o", j], name=f"b{j}_o")
        s["x1"], s["x1b"] = _add_ln(xf, s["h"], P["ln_g"][i, 0], P["ln_b"][i, 0], alpha, name=f"l{i}_ln1")
        arrive(f"{i}c", s["x1b"])
        s["hup"] = _mm(s["x1b"], W["ffn_w_up", i], out_dtype=BF16, name=f"l{i}_up")
        arrive(f"{i}d", s["hup"])
        s["act"] = _convglu_fwd(s["hup"], P["ffn_conv_w"][i], P["ffn_conv_b"][i], name=f"l{i}_glu")
        s["f"] = _mm(s["act"], W["ffn_w_down", i], name=f"l{i}_down")
        xf, xb = _add_ln(s["x1"], s["f"], P["ln_g"][i, 1], P["ln_b"][i, 1], alpha, name=f"l{i}_ln2")
        saved.append(s)

    G, loss_row = _loss_grad(xf, tgt, name="loss")

    gw = {}

    def dw(key, layer, a, b, name, **kw):
        gw[key] = _mm(a, b, ta=True, out_dtype=BF16, out_into=(gw.get(key), n_layers[key], layer), name=name, **kw)

    gp = {k: [None] * n_a for k in ("a_ln_g", "a_ln_b", "a_w_s", "a_b_s")}
    gp.update({k: [None] * depth for k in ("ffn_conv_w", "ffn_conv_b", "ln_g", "ln_b")})
    dk_parts, dv_parts, dbias_parts = [], [], []
    for i in reversed(range(depth)):
        s = saved[i]
        dr2, dr2b, dg2, db2 = _ln_bwd(G, s["x1"], s["f"], P["ln_g"][i, 1], alpha, name=f"l{i}_ln2_bwd")
        dw("ffn_w_down", i, s["act"], dr2b, f"l{i}_down_dw")
        dact = _mm(dr2b, W["ffn_w_down", i], tb=True, out_dtype=BF16, name=f"l{i}_down_dx")
        dhup, dcw, dcb = _convglu_bwd(s["hup"], dact, P["ffn_conv_w"][i], P["ffn_conv_b"][i], name=f"l{i}_glu_bwd")
        gp["ffn_conv_w"][i] = dcw.transpose(1, 0, 2).reshape(dcw.shape[1], -1)
        gp["ffn_conv_b"][i] = dcb.reshape(-1)
        dw("ffn_w_up", i, s["x1b"], dhup, f"l{i}_up_dw", b_halves=True)
        token = ship(gw, [("ffn_w_down", i), ("ffn_w_up", i)])
        G1 = _mm(dhup, W["ffn_w_up", i], tb=True, a_halves=True, add=dr2, add_scale=alpha, after=token,
                 name=f"l{i}_up_dx")
        dr1, dr1b, dg1, db1 = _ln_bwd(G1, s["x"], s["h"], P["ln_g"][i, 0], alpha, name=f"l{i}_ln1_bwd")
        gp["ln_g"][i] = jnp.concatenate([dg1, dg2], axis=0)
        gp["ln_b"][i] = jnp.concatenate([db1, db2], axis=0)
        if i < n_a:
            dw("a_w_out", i, s["y"], dr1b, f"a{i}_out_dw")
            token = ship(gw, [("a_w_out", i)])
            dy = _mm(dr1b, W["a_w_out", i], tb=True, out_dtype=BF16, after=token, name=f"a{i}_out_dx")
            dzp, dws, dbs, dlg, dlb = _sgu_bwd(s["zp"], dy, P["a_w_s"][i], P["a_b_s"][i], P["a_ln_g"][i],
                                               P["a_ln_b"][i], name=f"a{i}_sgu_bwd")
            gp["a_w_s"][i], gp["a_b_s"][i] = dws, dbs[:, :dws.shape[0]].T
            gp["a_ln_g"][i], gp["a_ln_b"][i] = dlg[0], dlb[0]
            dw("a_w_in", i, s["xb"], dzp, f"a{i}_in_dw")
            token = ship(gw, [("a_w_in", i)])
            G = _mm(dzp, W["a_w_in", i], tb=True, add=dr1, add_scale=alpha, after=token, name=f"a{i}_in_dx")
        else:
            j = i - n_a
            dw("b_w_o", j, s["ob"], dr1b, f"b{j}_o_dw")
            do_tok = _mm(dr1b, W["b_w_o", j], tb=True, name=f"b{j}_o_dx")
            delta, dob = _attn_bwd_prep(do_tok, s["oc"], name=f"b{j}_prep")
            dqs, dbs = [], []
            for g, d in enumerate(dils):
                dq, dk, dv, db = _attn_bwd(s["qg"][g], kg[g], vg[g], bias, g, d, _to_group(dob, d),
                                           _to_group(s["lse"], d), _to_group(delta, d), name=f"b{j}_attn{g}_bwd")
                dqs.append(_from_group(dq, d))
                dk_parts.append(_from_group(dk, d))
                dv_parts.append(_from_group(dv, d))
                dbs.append(db)
            dbias_parts.append(jnp.stack(dbs))
            dq_tok = jnp.concatenate(dqs, axis=1)
            dw("b_w_q", j, s["xb"], dq_tok, f"b{j}_q_dw")
            token = ship(gw, [("b_w_o", j), ("b_w_q", j)])
            G = _mm(dq_tok, W["b_w_q", j], tb=True, add=dr1, add_scale=alpha, after=token, name=f"b{j}_q_dx")
            if j == 0:
                dkv = jnp.concatenate([_addn(dk_parts, BF16, name="dk_sum"), _addn(dv_parts, BF16, name="dv_sum")], axis=1)
                dw("kv_w", 0, s["xb"], dkv, "kv_dw")
                token = ship(gw, [("kv_w", 0)])
                G = _mm(dkv, W["kv_w", 0], tb=True, add=G, add_scale=1.0, after=token, name="kv_dx")
    rel = _rel_grad(dbias_parts, name="rel_grad")
    grel = rel[:, :, :H].transpose(1, 0, 2).reshape(REL_BUCKETS, N_GROUPS * H)
    gsmall = {k: jnp.stack(v) for k, v in gp.items()}
    gsmall["rel_table"] = grel
    return loss_row, G, gw, gsmall


BIG = (("a_w_in", COL), ("a_w_out", ROW), ("kv_w", ROW), ("b_w_q", COL), ("b_w_o", COL),
       ("ffn_w_up", COL), ("ffn_w_down", ROW))
SMALL_SHARDED = ("a_ln_g", "a_ln_b", "ffn_conv_w", "ln_g", "ln_b")
SMALL_REPLICATED = ("a_w_s", "a_b_s", "rel_table", "ffn_conv_b")
WEIGHTS = ("a_w_in", "a_ln_g", "a_ln_b", "a_w_s", "a_b_s", "a_w_out", "kv_w", "b_w_q", "b_w_o", "rel_table",
           "ffn_w_up", "ffn_conv_w", "ffn_conv_b", "ffn_w_down", "ln_g", "ln_b")


def kernel(x, a_w_in, a_ln_g, a_ln_b, a_w_s, a_b_s, a_w_out, kv_w, b_w_q, b_w_o, rel_table, ffn_w_up, ffn_conv_w, ffn_conv_b, ffn_w_down, ln_g, ln_b, loss_target, m_a_w_in, m_a_ln_g, m_a_ln_b, m_a_w_s, m_a_b_s, m_a_w_out, m_kv_w, m_b_w_q, m_b_w_o, m_rel_table, m_ffn_w_up, m_ffn_conv_w, m_ffn_conv_b, m_ffn_w_down, m_ln_g, m_ln_b, v_a_w_in, v_a_ln_g, v_a_ln_b, v_a_w_s, v_a_b_s, v_a_w_out, v_kv_w, v_b_w_q, v_b_w_o, v_rel_table, v_ffn_w_up, v_ffn_conv_w, v_ffn_conv_b, v_ffn_w_down, v_ln_g, v_ln_b):
    w = dict(a_w_in=a_w_in, a_ln_g=a_ln_g, a_ln_b=a_ln_b, a_w_s=a_w_s, a_b_s=a_b_s, a_w_out=a_w_out, kv_w=kv_w,
             b_w_q=b_w_q, b_w_o=b_w_o, rel_table=rel_table, ffn_w_up=ffn_w_up, ffn_conv_w=ffn_conv_w,
             ffn_conv_b=ffn_conv_b, ffn_w_down=ffn_w_down, ln_g=ln_g, ln_b=ln_b)
    m = dict(a_w_in=m_a_w_in, a_ln_g=m_a_ln_g, a_ln_b=m_a_ln_b, a_w_s=m_a_w_s, a_b_s=m_a_b_s, a_w_out=m_a_w_out,
             kv_w=m_kv_w, b_w_q=m_b_w_q, b_w_o=m_b_w_o, rel_table=m_rel_table, ffn_w_up=m_ffn_w_up,
             ffn_conv_w=m_ffn_conv_w, ffn_conv_b=m_ffn_conv_b, ffn_w_down=m_ffn_w_down, ln_g=m_ln_g, ln_b=m_ln_b)
    v = dict(a_w_in=v_a_w_in, a_ln_g=v_a_ln_g, a_ln_b=v_a_ln_b, a_w_s=v_a_w_s, a_b_s=v_a_b_s, a_w_out=v_a_w_out,
             kv_w=v_kv_w, b_w_q=v_b_w_q, b_w_o=v_b_w_o, rel_table=v_rel_table, ffn_w_up=v_ffn_w_up,
             ffn_conv_w=v_ffn_conv_w, ffn_conv_b=v_ffn_conv_b, ffn_w_down=v_ffn_w_down, ln_g=v_ln_g, ln_b=v_ln_b)
    chip = 2 * lax.axis_index("x") + lax.axis_index("y")

    big_names = [n for n, _ in BIG]
    big_axes = [a for _, a in BIG]
    where = jnp.stack([chip, lax.axis_index("c"), 2 * chip + lax.axis_index("c")]).astype(jnp.int32)
    shards = [w[n].reshape((1,) + w[n].shape) if w[n].ndim == 2 else w[n] for n in big_names]
    small_shapes = [w[n].shape for n in SMALL_SHARDED]
    small = _pack([w[n] for n in SMALL_SHARDED])
    shard_of = dict(zip(big_names, shards))
    axis_of = dict(BIG)
    n_layers = {n: s.shape[0] for n, s in shard_of.items()}
    n_a = w["a_w_s"].shape[0]
    stages = {}
    for i in range(w["ffn_w_up"].shape[0]):
        if i < n_a:
            stages[f"{i}a"], stages[f"{i}b"] = [("a_w_in", i)], [("a_w_out", i)]
        else:
            stages[f"{i}a"] = ([("kv_w", 0)] if i == n_a else []) + [("b_w_q", i - n_a)]
            stages[f"{i}b"] = [("b_w_o", i - n_a)]
        stages[f"{i}c"], stages[f"{i}d"] = [("ffn_w_up", i)], [("ffn_w_down", i)]
    stages["0a"].append(("small", 0))
    assert sum(len(st) for st in stages.values()) == sum(n_layers.values()) + 1
    flights, full_shape, all_started = {}, {}, None
    for batch, names in enumerate(([st for st in stages if st[0] == "0"], [st for st in stages if st[0] != "0"])):
        order = [key for st in names for key in stages[st]]
        placed, axes = [], []
        for n, l in order:
            if n == "small":
                placed.append(_place_shard(small[None], 0, ROW, where, F32, name="place_small"))
                axes.append(ROW)
            else:
                placed.append(_place_shard(shard_of[n], l, axis_of[n], where, BF16, name=f"place_{n}{l}",
                                           after=all_started))
                axes.append(axis_of[n])
                full_shape[n] = (n_layers[n],) + placed[-1].shape[1:]
        send_sems, recv_sems, flying, all_started = _gather_start(placed, axes, name=f"gather_start{batch}")
        for st in names:
            flights[st] = (send_sems, recv_sems, flying, axes, order)

    def fetch(stage, after):
        if stage not in stages:
            return {}, {}
        send_sems, recv_sems, flying, axes, order = flights[stage]
        if stage == "0a":
            after = all_started
        idxs = [order.index(key) for key in stages[stage]]
        landed = _gather_wait(send_sems, recv_sems, [flying[i] for i in idxs], [axes[i] for i in idxs], idxs, after,
                              name=f"gather_wait_{stage}")
        W_new, P_new = {}, {}
        for key, arr in zip(stages[stage], landed):
            if key[0] == "small":
                small_all = arr.reshape((N_CHIPS,) + small.shape)
                per_chip = [_unpack(small_all[j], small_shapes) for j in range(N_CHIPS)]
                for i, n in enumerate(SMALL_SHARDED):
                    P_new[n] = jnp.concatenate([per_chip[j][i] for j in range(N_CHIPS)], axis=-1)
            else:
                W_new[key] = arr.reshape(arr.shape[1:])
        return W_new, P_new

    slots ={n: lax.empty((N_DEV, n_layers[n]) + _part_geometry(full_shape[n], axis_of[n]), BF16) for n in big_names}
    sems, plan = [], []

    def ship(gw, items):
        names = [n for n, _ in items]
        send, recv, grads, landing, token = _scatter_start(
            [gw[n] for n in names], [slots[n] for n in names], [l for _, l in items], [axis_of[n] for n in names],
            name="scatter_start_" + "_".join(f"{n}{l}" for n, l in items))
        for n, g, s in zip(names, grads, landing):
            gw[n], slots[n] = g, s
        sems.append((send, recv))
        plan.append([(big_names.index(n), l) for n, l in items])
        return token

    loss_row, grad_x, gw, gsmall = _local_step(x[0], loss_target[0], {n: w[n] for n in SMALL_REPLICATED}, fetch,
                                               ship, n_layers)

    small_names = list(SMALL_SHARDED) + list(SMALL_REPLICATED)
    small_pack = _pack([gsmall[n] for n in small_names] + [loss_row[0, :1]])
    small_sems = _small_start(small_pack, lax.empty((N_DEV,) + small_pack.shape, F32), name="small_start")
    own, landed = _scatter_wait(sems, [gw[n] for n in big_names], [slots[n] for n in big_names], big_axes, plan,
                                name="scatter_wait")
    sums = [_sum_parts(b, g, ax, where, name=f"sum_{n}") for (n, ax), b, g in zip(BIG, landed, own)]
    pairs = _pair_gather(sums, name="pair_gather")
    grad = {n: p.reshape(w[n].shape) for n, p in zip(big_names, pairs)}

    delta, new_m, new_v = {}, {}, {}
    for n, _ in BIG:
        delta[n], new_m[n], new_v[n] = _adamw(w[n], grad[n], m[n], v[n], name=f"adamw_{n}")
    small_own, small_slots = _small_wait(*small_sems, new_v[big_names[-1]], name="small_wait")
    small_sum = _sum_small(small_slots, small_own, where, name="sum_small")
    full_small = _unpack(small_sum, [gsmall[n].shape for n in small_names] + [(1,)])
    loss = full_small[-1][0]
    for n, gfull in zip(small_names, full_small):
        if n in SMALL_SHARDED:
            width = w[n].shape[-1]
            grad[n] = lax.dynamic_slice_in_dim(gfull, chip * width, width, axis=-1)
        else:
            grad[n] = gfull
    shapes = [w[n].shape for n in small_names]
    packed = _adamw(_pack([w[n] for n in small_names]), _pack([grad[n] for n in small_names]),
                    _pack([m[n] for n in small_names]), _pack([v[n] for n in small_names]), name="adamw_small")
    for out, res in zip((delta, new_m, new_v), packed):
        for n, a in zip(small_names, _unpack(res, shapes)):
            out[n] = a

    return (loss, grad_x[None], *[grad[n] for n in WEIGHTS], *[delta[n] for n in WEIGHTS],
            *[new_m[n] for n in WEIGHTS], *[new_v[n] for n in WEIGHTS])
```

```python
import functools
import math

import numpy as np
import jax
import jax.numpy as jnp
from jax import lax
from jax.experimental import pallas as pl
from jax.experimental.pallas import tpu as pltpu

F32 = jnp.float32
BF16 = jnp.bfloat16
MESH = pl.DeviceIdType.MESH

CHUNK = 128
HEAD_DIM = 64
DILATED_GROUPS = ((128, 1), (512, 4), (2048, 16))
N_GROUPS = len(DILATED_GROUPS)
BLK = 128
REL_BUCKETS = 32
REL_MAX_DIST = 2048
LN_EPS = 1e-5
NEG = -1e30
ADAM_LR = 0.001
ADAM_B1 = 0.9
ADAM_B2 = 0.999
ADAM_EPS = 1e-08
ADAM_WD = 0.01
ADAM_STEP = 10

N_CHIPS = 4
N_DEV = 8
LANES = 128
SUBLANES = 8
VMEM_LIMIT = 48 * 1024 * 1024

_SQRT_HALF = 0.7071067811865476
_INV_SQRT_2PI = 0.3989422804014327


def _cp(sem=None, vmem=VMEM_LIMIT):
    return pltpu.CompilerParams(dimension_semantics=sem, vmem_limit_bytes=vmem)


def _tile(dim, target, align=LANES):
    if dim <= target:
        return dim
    t = (target // align) * align
    while t >= align:
        if dim % t == 0:
            return t
        t -= align
    return dim


def _gelu(x):
    return 0.5 * x * (1.0 + lax.erf(x * _SQRT_HALF))


def _gelu_grad(x):
    return 0.5 * (1.0 + lax.erf(x * _SQRT_HALF)) + x * (_INV_SQRT_2PI * jnp.exp(-0.5 * x * x))


def _dot(a, b, ca, cb):
    return lax.dot_general(a, b, (((ca,), (cb,)), ((), ())), preferred_element_type=F32)


def _mn_tile(dim):
    return max(_tile(dim, 1024), _tile(dim, 1408))


def _mm(a, b, *, name, ta=False, tb=False, out_dtype=F32, tiles=None,
        a_lead=(), b_lead=(), a_halves=False, b_halves=False, add=None, add_scale=1.0, out_into=None, after=None):
    a2, b2 = a.shape[-2:], b.shape[-2:]
    M, K = (a2[1], a2[0]) if ta else a2
    Kb, N = (b2[1], b2[0]) if tb else b2
    if a_halves:
        K = 2 * K
    if b_halves:
        N = 2 * N
    assert K == Kb, (name, a.shape, b.shape)
    if tiles is None:
        k_target = 1408 if a_halves else (K if K <= 2816 else 2048)
        tm, tn = _mn_tile(M), _mn_tile(N // 2 if b_halves else N)
        tk = _tile(K // 2 if a_halves else K, k_target)
    else:
        tm = _tile(M, tiles[0])
        tn = _tile(N // 2 if b_halves else N, tiles[1])
        tk = _tile(K // 2 if a_halves else K, tiles[2])
    nm, nn, nk = M // tm, N // tn, K // tk
    nkh, nnh = nk // 2, nn // 2

    def lead(idx, rest):
        return tuple(idx) + tuple(rest)

    sq_a = (None,) * (a.ndim - 2)
    sq_b = (None,) * (b.ndim - 2)
    if a_halves:
        assert not ta and a.ndim == 3
        a_spec = pl.BlockSpec((None, tm, tk), lambda i, j, k: (k // nkh, i, k % nkh))
    elif ta:
        a_spec = pl.BlockSpec(sq_a + (tk, tm), lambda i, j, k: lead(a_lead, (k, i)))
    else:
        a_spec = pl.BlockSpec(sq_a + (tm, tk), lambda i, j, k: lead(a_lead, (i, k)))
    if b_halves:
        assert not tb and b.ndim == 3
        b_spec = pl.BlockSpec((None, tk, tn), lambda i, j, k: (j // nnh, k, j % nnh))
    elif tb:
        b_spec = pl.BlockSpec(sq_b + (tn, tk), lambda i, j, k: lead(b_lead, (j, k)))
    else:
        b_spec = pl.BlockSpec(sq_b + (tk, tn), lambda i, j, k: lead(b_lead, (k, j)))
    mn_spec = pl.BlockSpec((tm, tn), lambda i, j, k: (i, j))
    in_specs = [a_spec, b_spec]
    args = [a, b]
    if add is not None:
        in_specs.append(mn_spec)
        args.append(add)
    aliases = {}
    if out_into is None:
        o_spec, out_shape = mn_spec, jax.ShapeDtypeStruct((M, N), out_dtype)
    else:
        buf, n_layers, layer = out_into
        o_spec = pl.BlockSpec((None, tm, tn), lambda i, j, k: (layer, i, j))
        out_shape = jax.ShapeDtypeStruct((n_layers, M, N), out_dtype)
        if buf is not None:
            aliases = {len(args): 0}
            in_specs.append(pl.BlockSpec(memory_space=pl.ANY))
            args.append(buf)
    if after is not None:
        in_specs.append(pl.BlockSpec(memory_space=pl.ANY))
        args.append(after)
    n_in = len(args)
    ca, cb = (0 if ta else 1), (1 if tb else 0)

    def body(*refs):
        a_ref, b_ref = refs[:2]
        add_ref = refs[2] if add is not None else None
        o_ref = refs[n_in]

        def finish(acc):
            if add_ref is not None:
                acc = acc + add_scale * add_ref[...]
            o_ref[...] = acc.astype(o_ref.dtype)

        if nk == 1:
            finish(_dot(a_ref[...], b_ref[...], ca, cb))
        else:
            acc_ref = refs[-1]
            k = pl.program_id(2)

            @pl.when(k == 0)
            def _():
                acc_ref[...] = jnp.zeros_like(acc_ref)

            acc_ref[...] += _dot(a_ref[...], b_ref[...], ca, cb)

            @pl.when(k == nk - 1)
            def _():
                finish(acc_ref[...])

    return pl.pallas_call(
        body, name=name, grid=(nm, nn, nk),
        in_specs=in_specs, out_specs=o_spec, out_shape=out_shape,
        input_output_aliases=aliases,
        scratch_shapes=[] if nk == 1 else [pltpu.VMEM((tm, tn), F32)],
        compiler_params=_cp(("parallel", "parallel", "arbitrary")),
    )(*args)


def _ln_stats(r):
    mu = jnp.mean(r, axis=-1, keepdims=True)
    xc = r - mu
    var = jnp.mean(xc * xc, axis=-1, keepdims=True)
    rstd = lax.rsqrt(var + LN_EPS)
    return xc * rstd, rstd


def _add_ln(x, h, g, b, alpha, *, name):
    T, D = x.shape
    tr = _tile(T, 512, SUBLANES)
    row = pl.BlockSpec((tr, D), lambda i: (i, 0))
    vec = pl.BlockSpec((1, D), lambda i: (0, 0))

    def body(x_ref, h_ref, g_ref, b_ref, o_ref, ob_ref):
        xhat, _ = _ln_stats(alpha * x_ref[...] + h_ref[...])
        y = xhat * g_ref[...] + b_ref[...]
        o_ref[...] = y
        ob_ref[...] = y.astype(BF16)

    return pl.pallas_call(
        body, name=name, grid=(T // tr,),
        in_specs=[row, row, vec, vec], out_specs=[row, row],
        out_shape=[jax.ShapeDtypeStruct((T, D), F32), jax.ShapeDtypeStruct((T, D), BF16)],
        compiler_params=_cp(("parallel",)),
    )(x, h, g.reshape(1, D), b.reshape(1, D))


def _ln_bwd(dy, x, h, g, alpha, *, name):
    T, D = x.shape
    tr = _tile(T, 512, SUBLANES)
    row = pl.BlockSpec((tr, D), lambda i: (i, 0))
    vec = pl.BlockSpec((1, D), lambda i: (0, 0))

    def body(dy_ref, x_ref, h_ref, g_ref, dr_ref, drb_ref, dg_ref, db_ref):
        @pl.when(pl.program_id(0) == 0)
        def _():
            dg_ref[...] = jnp.zeros_like(dg_ref)
            db_ref[...] = jnp.zeros_like(db_ref)

        xhat, rstd = _ln_stats(alpha * x_ref[...] + h_ref[...])
        dyv = dy_ref[...]
        dxhat = dyv * g_ref[...]
        c1 = jnp.mean(dxhat, axis=-1, keepdims=True)
        c2 = jnp.mean(dxhat * xhat, axis=-1, keepdims=True)
        dr = rstd * (dxhat - c1 - xhat * c2)
        dr_ref[...] = dr
        drb_ref[...] = dr.astype(BF16)
        dg_ref[...] += jnp.sum(dyv * xhat, axis=0, keepdims=True)
        db_ref[...] += jnp.sum(dyv, axis=0, keepdims=True)

    return pl.pallas_call(
        body, name=name, grid=(T // tr,),
        in_specs=[row, row, row, vec], out_specs=[row, row, vec, vec],
        out_shape=[jax.ShapeDtypeStruct((T, D), F32), jax.ShapeDtypeStruct((T, D), BF16),
                   jax.ShapeDtypeStruct((1, D), F32), jax.ShapeDtypeStruct((1, D), F32)],
        compiler_params=_cp(("arbitrary",)),
    )(dy, x, h, g.reshape(1, D))


def _sgu_pieces(zp, lg, lb, E):
    z = _gelu(zp)
    u, v = z[:, :E], z[:, E:]
    vhat, rstd = _ln_stats(v)
    vn = vhat * lg + lb
    return u, vhat, rstd, vn


def _tril_mask():
    t = lax.broadcasted_iota(jnp.int32, (CHUNK, CHUNK), 0)
    s = lax.broadcasted_iota(jnp.int32, (CHUNK, CHUNK), 1)
    return s <= t


def _sgu_fwd(zp, ws, bs, lg, lb, *, name):
    T, E2 = zp.shape
    E = E2 // 2
    G = ws.shape[0]
    cw = E // G

    def body(zp_ref, ws_ref, bs_ref, lg_ref, lb_ref, y_ref):
        u, _, _, vn = _sgu_pieces(zp_ref[...].astype(F32), lg_ref[...], lb_ref[...], E)
        vnb = vn.astype(BF16)
        tril = _tril_mask()
        for g in range(G):
            cols = slice(g * cw, (g + 1) * cw)
            w = jnp.where(tril, ws_ref[g], 0.0).astype(BF16)
            sv = _dot(w, vnb[:, cols], 1, 0) + bs_ref[g]
            y_ref[:, cols] = (u[:, cols] * sv).astype(BF16)

    return pl.pallas_call(
        body, name=name, grid=(T // CHUNK,),
        in_specs=[pl.BlockSpec((CHUNK, E2), lambda n: (n, 0)),
                  pl.BlockSpec((G, CHUNK, CHUNK), lambda n: (0, 0, 0)),
                  pl.BlockSpec((G, CHUNK, 1), lambda n: (0, 0, 0)),
                  pl.BlockSpec((1, E), lambda n: (0, 0)),
                  pl.BlockSpec((1, E), lambda n: (0, 0))],
        out_specs=pl.BlockSpec((CHUNK, E), lambda n: (n, 0)),
        out_shape=jax.ShapeDtypeStruct((T, E), BF16),
        compiler_params=_cp(("parallel",)),
    )(zp, ws, bs.reshape(G, CHUNK, 1), lg.reshape(1, E), lb.reshape(1, E))


def _sgu_bwd(zp, dy, ws, bs, lg, lb, *, name):
    T, E2 = zp.shape
    E = E2 // 2
    G = ws.shape[0]
    cw = E // G
    assert G <= LANES

    def body(zp_ref, dy_ref, ws_ref, bs_ref, lg_ref, lb_ref,
             dzp_ref, dws_ref, dbs_ref, dlg_ref, dlb_ref, dvn_ref):
        @pl.when(pl.program_id(0) == 0)
        def _():
            dws_ref[...] = jnp.zeros_like(dws_ref)
            dbs_ref[...] = jnp.zeros_like(dbs_ref)
            dlg_ref[...] = jnp.zeros_like(dlg_ref)
            dlb_ref[...] = jnp.zeros_like(dlb_ref)

        zpv = zp_ref[...].astype(F32)
        lgv = lg_ref[...]
        u, vhat, rstd, vn = _sgu_pieces(zpv, lgv, lb_ref[...], E)
        gp = _gelu_grad(zpv)
        vnb = vn.astype(BF16)
        tril = _tril_mask()
        lane = lax.broadcasted_iota(jnp.int32, (CHUNK, LANES), 1)
        dbs = dbs_ref[...]
        for g in range(G):
            cols = slice(g * cw, (g + 1) * cw)
            w = jnp.where(tril, ws_ref[g], 0.0).astype(BF16)
            sv = _dot(w, vnb[:, cols], 1, 0) + bs_ref[g]
            dyg = dy_ref[:, cols].astype(F32)
            dzp_ref[:, cols] = (dyg * sv * gp[:, cols]).astype(BF16)
            dsv = dyg * u[:, cols]
            dsvb = dsv.astype(BF16)
            dws_ref[g] += jnp.where(tril, _dot(dsvb, vnb[:, cols], 1, 1), 0.0)
            dvn_ref[:, cols] = _dot(w, dsvb, 0, 0)
            dbs = dbs + jnp.where(lane == g, jnp.sum(dsv, axis=1, keepdims=True), 0.0)
        dbs_ref[...] = dbs
        dvn = dvn_ref[...]
        dlg_ref[...] += jnp.sum(dvn * vhat, axis=0, keepdims=True)
        dlb_ref[...] += jnp.sum(dvn, axis=0, keepdims=True)
        dvhat = dvn * lgv
        c1 = jnp.mean(dvhat, axis=-1, keepdims=True)
        c2 = jnp.mean(dvhat * vhat, axis=-1, keepdims=True)
        dv = rstd * (dvhat - c1 - vhat * c2)
        dzp_ref[:, E:] = (dv * gp[:, E:]).astype(BF16)

    vecE = pl.BlockSpec((1, E), lambda n: (0, 0))
    return pl.pallas_call(
        body, name=name, grid=(T // CHUNK,),
        in_specs=[pl.BlockSpec((CHUNK, E2), lambda n: (n, 0)),
                  pl.BlockSpec((CHUNK, E), lambda n: (n, 0)),
                  pl.BlockSpec((G, CHUNK, CHUNK), lambda n: (0, 0, 0)),
                  pl.BlockSpec((G, CHUNK, 1), lambda n: (0, 0, 0)),
                  vecE, vecE],
        out_specs=[pl.BlockSpec((CHUNK, E2), lambda n: (n, 0)),
                   pl.BlockSpec((G, CHUNK, CHUNK), lambda n: (0, 0, 0)),
                   pl.BlockSpec((CHUNK, LANES), lambda n: (0, 0)),
                   vecE, vecE],
        out_shape=[jax.ShapeDtypeStruct((T, E2), BF16),
                   jax.ShapeDtypeStruct((G, CHUNK, CHUNK), F32),
                   jax.ShapeDtypeStruct((CHUNK, LANES), F32),
                   jax.ShapeDtypeStruct((1, E), F32), jax.ShapeDtypeStruct((1, E), F32)],
        scratch_shapes=[pltpu.VMEM((CHUNK, E), F32)],
        compiler_params=_cp(("arbitrary",)),
    )(zp, dy, ws, bs.reshape(G, CHUNK, 1), lg.reshape(1, E), lb.reshape(1, E))


def _delay(h, k):
    t = lax.broadcasted_iota(jnp.int32, h.shape, 0)
    return jnp.where(t >= k, pltpu.roll(h, k, 0), 0.0)


def _advance(d, k):
    T = d.shape[0]
    t = lax.broadcasted_iota(jnp.int32, d.shape, 0)
    return jnp.where(t < T - k, pltpu.roll(d, T - k, 0), 0.0)


def _conv3(h, w_ref, b_ref):
    return w_ref[2:3, :] * h + w_ref[1:2, :] * _delay(h, 1) + w_ref[0:1, :] * _delay(h, 2) + b_ref[...]


def _convglu_fwd(hup, cw, cb, *, name):
    T, F2 = hup.shape
    F = F2 // 2
    tc = LANES
    nt = F // tc

    def body(ha_ref, hg_ref, wa_ref, wg_ref, ba_ref, bg_ref, o_ref, a_ref, g_ref):
        a = _conv3(ha_ref[...].astype(F32), wa_ref, ba_ref)
        g = _conv3(hg_ref[...].astype(F32), wg_ref, bg_ref)
        o_ref[...] = (_gelu(a) * g).astype(BF16)
        a_ref[...] = a.astype(BF16)
        g_ref[...] = g.astype(BF16)

    col = lambda off: pl.BlockSpec((T, tc), lambda j: (0, j + off))
    w3 = lambda off: pl.BlockSpec((3, tc), lambda j: (0, j + off))
    b1 = lambda off: pl.BlockSpec((1, tc), lambda j: (0, j + off))
    shp = jax.ShapeDtypeStruct((T, F), BF16)
    return pl.pallas_call(
        body, name=name, grid=(nt,),
        in_specs=[col(0), col(nt), w3(0), w3(nt), b1(0), b1(nt)],
        out_specs=[col(0)] * 3, out_shape=[shp] * 3,
        compiler_params=_cp(("parallel",)),
    )(hup, hup, cw, cw, cb.reshape(1, F2), cb.reshape(1, F2))


def _convglu_bwd(hup, a, g, dact, cw, *, name):
    T, F2 = hup.shape
    F = F2 // 2
    tc = LANES
    nt = F // tc

    def half(h, w_ref, d, dh_ref, dw_ref, db_ref, i):
        d1, d2 = _advance(d, 1), _advance(d, 2)
        dh_ref[i] = (w_ref[2:3, :] * d + w_ref[1:2, :] * d1 + w_ref[0:1, :] * d2).astype(BF16)
        c = _dot(h, jnp.concatenate([d2, d1, d], axis=1).astype(BF16), 0, 0)
        eye = lax.broadcasted_iota(jnp.int32, (tc, tc), 0) == lax.broadcasted_iota(jnp.int32, (tc, tc), 1)
        for k in range(3):
            dw_ref[i, k:k + 1, :] = jnp.sum(jnp.where(eye, c[:, k * tc:(k + 1) * tc], 0.0), axis=0, keepdims=True)
        db_ref[i] = jnp.sum(d, axis=0, keepdims=True)

    def body(ha_ref, hg_ref, a_ref, g_ref, d_ref, wa_ref, wg_ref, dh_ref, dw_ref, db_ref):
        a = a_ref[...].astype(F32)
        d = d_ref[...].astype(F32)
        half(ha_ref[...].astype(BF16), wa_ref, d * g_ref[...].astype(F32) * _gelu_grad(a), dh_ref, dw_ref, db_ref, 0)
        half(hg_ref[...].astype(BF16), wg_ref, d * _gelu(a), dh_ref, dw_ref, db_ref, 1)

    col = lambda off: pl.BlockSpec((T, tc), lambda j: (0, j + off))
    w3 = lambda off: pl.BlockSpec((3, tc), lambda j: (0, j + off))
    return pl.pallas_call(
        body, name=name, grid=(nt,),
        in_specs=[col(0), col(nt), col(0), col(0), col(0), w3(0), w3(nt)],
        out_specs=[pl.BlockSpec((2, T, tc), lambda j: (0, 0, j)),
                   pl.BlockSpec((2, 3, tc), lambda j: (0, 0, j)),
                   pl.BlockSpec((2, 1, tc), lambda j: (0, 0, j))],
        out_shape=[jax.ShapeDtypeStruct((2, T, F), BF16),
                   jax.ShapeDtypeStruct((2, 3, F), F32),
                   jax.ShapeDtypeStruct((2, 1, F), F32)],
        compiler_params=_cp(("parallel",)),
    )(hup, hup, a, g, dact, cw, cw)


def _bucket_table():
    iq = np.arange(BLK)[:, None]
    ik = np.arange(2 * BLK)[None, :]
    delta = iq + BLK - ik
    out = []
    for win, dil in DILATED_GROUPS:
        valid = (delta >= 0) & (delta <= win // dil)
        n = (np.clip(delta, 0, None) * dil).astype(np.int32)
        max_exact = REL_BUCKETS // 2
        nf = np.maximum(n, 1).astype(np.float32)
        large = max_exact + (np.log(nf / np.float32(max_exact)) / np.float32(math.log(REL_MAX_DIST / max_exact))
                             * np.float32(REL_BUCKETS - max_exact)).astype(np.int32)
        large = np.minimum(large, REL_BUCKETS - 1)
        out.append(np.where(valid, np.where(n < max_exact, n, large), -1))
    return np.stack(out).astype(np.int32)


def _band_bias(rel_table, H, *, name):
    bkt = jnp.asarray(_bucket_table())

    def body(tab_ref, bk_ref, o_ref):
        g = pl.program_id(0)
        bk = bk_ref[...]
        prev = lax.broadcasted_iota(jnp.int32, bk.shape, 1) < BLK
        for h in range(H):
            def bucket(c, acc):
                return jnp.where(bk == c, tab_ref[c, g * H + h], acc)

            b = lax.fori_loop(0, REL_BUCKETS, bucket, jnp.zeros(bk.shape, F32))
            b = jnp.where(bk >= 0, b, NEG)
            o_ref[h, 0] = b
            o_ref[h, 1] = jnp.where(prev, NEG, b)

    return pl.pallas_call(
        body, name=name, grid=(N_GROUPS,),
        in_specs=[pl.BlockSpec(memory_space=pltpu.SMEM),
                  pl.BlockSpec((None, BLK, 2 * BLK), lambda g: (g, 0, 0))],
        out_specs=pl.BlockSpec((None, H, 2, BLK, 2 * BLK), lambda g: (g, 0, 0, 0, 0)),
        out_shape=jax.ShapeDtypeStruct((N_GROUPS, H, 2, BLK, 2 * BLK), F32),
        compiler_params=_cp(("parallel",)),
    )(rel_table, bkt)


PAIR = 2 * HEAD_DIM


def _bdot(a, b, ca, cb):
    return lax.dot_general(a, b, (((ca,), (cb,)), ((0,), (0,))), preferred_element_type=F32)


def _blocks_per_step(nb, dil):
    return 2 if (nb // dil) % 2 == 0 else 1


def _blk(i):
    return slice(i * BLK, (i + 1) * BLK)


def _heads(ref, i, H, own_columns_only):
    low = lax.broadcasted_iota(jnp.int32, (BLK, PAIR), 1) < HEAD_DIM
    out = []
    for h in range(H):
        a = ref[_blk(i), (h // 2) * PAIR:(h // 2 + 1) * PAIR]
        if own_columns_only:
            a = jnp.where(low if h % 2 == 0 else jnp.logical_not(low), a, jnp.zeros_like(a))
        out.append(a)
    return jnp.stack(out)


def _pair_sums(o):
    return [o[2 * p] + o[2 * p + 1] for p in range(o.shape[0] // 2)]


def _store_pairs(ref, i, o, dtype, scale=None):
    for p, v in enumerate(_pair_sums(o)):
        ref[_blk(i), p * PAIR:(p + 1) * PAIR] = (v if scale is None else scale * v).astype(dtype)


def _stat_tile(s):
    lane = lax.broadcasted_iota(jnp.int32, (BLK, LANES), 1)
    t = jnp.zeros((BLK, LANES), F32)
    for h in range(s.shape[0]):
        t = jnp.where(lane == h, s[h], t)
    return t


def _stat_cols(ref, i, H):
    return jnp.stack([ref[_blk(i), h:h + 1] for h in range(H)])


def _scores(qm, kp, kc, bias):
    scale = HEAD_DIM ** -0.5
    sp = _bdot(qm, kp, 2, 2) * scale + bias[:, :, :BLK]
    sc = _bdot(qm, kc, 2, 2) * scale + bias[:, :, BLK:]
    return sp, sc


def _attn_specs(nb, dil, g, H, HD, sb):
    nbg = nb // dil
    QB = _blocks_per_step(nb, dil)
    cur = lambda col: pl.BlockSpec((QB * BLK, HD), lambda b: (sb(b), col))
    prev = lambda col: pl.BlockSpec((BLK, HD), lambda b: (jnp.maximum(QB * sb(b) - 1, 0), col))
    stat = pl.BlockSpec((QB * BLK, LANES), lambda b: (sb(b), 0))
    bias0 = pl.BlockSpec((None, H, None, BLK, 2 * BLK),
                         lambda b: (g, 0, jnp.where((QB * sb(b)) % nbg == 0, 1, 0), 0, 0))
    bias1 = pl.BlockSpec((None, H, None, BLK, 2 * BLK), lambda b: (g, 0, 0, 0, 0))
    return cur, prev, stat, bias0, bias1


def _attn_fwd(q, k, v, bias, g, dil, *, name):
    (qa, qc), (ka, kc_), (va, vc_) = q, k, v
    T = qa.shape[0]
    H = bias.shape[1]
    HD = H * HEAD_DIM
    assert H % 2 == 0 and H <= LANES and T % (dil * BLK) == 0
    nb = T // BLK
    QB = _blocks_per_step(nb, dil)
    cur, prev, stat, bias0, bias1 = _attn_specs(nb, dil, g, H, HD, lambda b: b)

    def body(q_ref, kp_ref, kc_ref, vp_ref, vc_ref, b0_ref, b1_ref, o_ref, l_ref):
        for i in range(QB):
            kpi, vpi = (kp_ref, vp_ref) if i == 0 else (kc_ref, vc_ref)
            sp, sc = _scores(_heads(q_ref, i, H, True), _heads(kpi, max(i - 1, 0), H, False),
                             _heads(kc_ref, i, H, False), (b0_ref if i == 0 else b1_ref)[...])
            m = jnp.maximum(jnp.max(sp, axis=-1, keepdims=True), jnp.max(sc, axis=-1, keepdims=True))
            pp, pc = jnp.exp(sp - m), jnp.exp(sc - m)
            den = jnp.sum(pp, axis=-1, keepdims=True) + jnp.sum(pc, axis=-1, keepdims=True)
            o = (_bdot((pp / den).astype(BF16), _heads(vpi, max(i - 1, 0), H, True), 2, 1)
                 + _bdot((pc / den).astype(BF16), _heads(vc_ref, i, H, True), 2, 1))
            _store_pairs(o_ref, i, o, F32)
            l_ref[_blk(i), :] = _stat_tile(m + jnp.log(den))

    return pl.pallas_call(
        body, name=name, grid=(nb // QB,),
        in_specs=[cur(qc), prev(kc_), cur(kc_), prev(vc_), cur(vc_), bias0, bias1],
        out_specs=[cur(0), stat],
        out_shape=[jax.ShapeDtypeStruct((T, HD), F32), jax.ShapeDtypeStruct((T, LANES), F32)],
        compiler_params=_cp(("parallel",)),
    )(qa, ka, ka, va, va, bias, bias)


def _attn_combine(os, ls, *, name):
    n = len(os)
    T, HD = os[0].shape
    H = HD // HEAD_DIM
    tr = _tile(T, 512, SUBLANES)
    wide = pl.BlockSpec((tr, HD), lambda i: (i, 0))
    stat = pl.BlockSpec((tr, LANES), lambda i: (i, 0))

    def body(*refs):
        o_refs, l_refs = refs[:n], refs[n:2 * n]
        oc_ref, ocb_ref, lt_ref = refs[2 * n:]
        ls_ = [r[...] for r in l_refs]
        m = ls_[0]
        for l in ls_[1:]:
            m = jnp.maximum(m, l)
        ws = [jnp.exp(l - m) for l in ls_]
        den = ws[0]
        for w in ws[1:]:
            den = den + w
        lt_ref[...] = m + jnp.log(den)
        ws = [w / den for w in ws]
        low = lax.broadcasted_iota(jnp.int32, (tr, PAIR), 1) < HEAD_DIM
        for p in range(H // 2):
            cols = slice(p * PAIR, (p + 1) * PAIR)
            acc = jnp.zeros((tr, PAIR), F32)
            for g in range(n):
                w = jnp.where(low, ws[g][:, 2 * p:2 * p + 1], ws[g][:, 2 * p + 1:2 * p + 2])
                acc = acc + w * o_refs[g][:, cols]
            oc_ref[:, cols] = acc
            ocb_ref[:, cols] = acc.astype(BF16)

    return pl.pallas_call(
        body, name=name, grid=(T // tr,),
        in_specs=[wide] * n + [stat] * n, out_specs=[wide, wide, stat],
        out_shape=[jax.ShapeDtypeStruct((T, HD), F32), jax.ShapeDtypeStruct((T, HD), BF16),
                   jax.ShapeDtypeStruct((T, LANES), F32)],
        compiler_params=_cp(("parallel",)),
    )(*os, *ls)


def _attn_bwd_prep(do, oc, *, name):
    T, HD = do.shape
    H = HD // HEAD_DIM
    tr = _tile(T, 512, SUBLANES)
    wide = pl.BlockSpec((tr, HD), lambda i: (i, 0))
    stat = pl.BlockSpec((tr, LANES), lambda i: (i, 0))

    def body(do_ref, oc_ref, dl_ref, dob_ref):
        low = lax.broadcasted_iota(jnp.int32, (tr, PAIR), 1) < HEAD_DIM
        lane = lax.broadcasted_iota(jnp.int32, (tr, LANES), 1)
        dl = jnp.zeros((tr, LANES), F32)
        for p in range(H // 2):
            cols = slice(p * PAIR, (p + 1) * PAIR)
            d = do_ref[:, cols]
            prod = d * oc_ref[:, cols]
            dl = jnp.where(lane == 2 * p, jnp.sum(jnp.where(low, prod, 0.0), axis=-1, keepdims=True), dl)
            dl = jnp.where(lane == 2 * p + 1, jnp.sum(jnp.where(low, 0.0, prod), axis=-1, keepdims=True), dl)
            dob_ref[:, cols] = d.astype(BF16)
        dl_ref[...] = dl

    return pl.pallas_call(
        body, name=name, grid=(T // tr,),
        in_specs=[wide, wide], out_specs=[stat, wide],
        out_shape=[jax.ShapeDtypeStruct((T, LANES), F32), jax.ShapeDtypeStruct((T, HD), BF16)],
        compiler_params=_cp(("parallel",)),
    )(do, oc)


def _attn_bwd(q, k, v, bias, g, dil, do, lse, delta, *, name):
    (qa, qc), (ka, kc_), (va, vc_) = q, k, v
    T, HD = do.shape
    H = HD // HEAD_DIM
    nb = T // BLK
    scale = HEAD_DIM ** -0.5
    QB = _blocks_per_step(nb, dil)
    ns = nb // QB
    cur, prev, stat, bias0, bias1 = _attn_specs(nb, dil, g, H, HD, lambda b: jnp.minimum(b, ns - 1))
    late = pl.BlockSpec((QB * BLK, HD), lambda b: (jnp.maximum(b - 1, 0), 0))

    def body(q_ref, kp_ref, kc_ref, vp_ref, vc_ref, b0_ref, b1_ref, do_ref, l_ref, dl_ref,
             dq_ref, dk_ref, dv_ref, db_ref, ck_ref, cv_ref):
        b = pl.program_id(0)

        @pl.when(b == 0)
        def _():
            db_ref[...] = jnp.zeros_like(db_ref)
            ck_ref[...] = jnp.zeros_like(ck_ref)
            cv_ref[...] = jnp.zeros_like(cv_ref)

        @pl.when(b < ns)
        def _():
            dk_ref[...] = ck_ref[...]
            dv_ref[...] = cv_ref[...]
            for i in range(QB):
                kpi, vpi, ip = (kp_ref, vp_ref, 0) if i == 0 else (kc_ref, vc_ref, i - 1)
                qm, dom = _heads(q_ref, i, H, True), _heads(do_ref, i, H, True)
                sp, sc = _scores(qm, _heads(kpi, ip, H, False), _heads(kc_ref, i, H, False),
                                 (b0_ref if i == 0 else b1_ref)[...])
                l, dl = _stat_cols(l_ref, i, H), _stat_cols(dl_ref, i, H)
                pp, pc = jnp.exp(sp - l), jnp.exp(sc - l)
                dsp = pp * (_bdot(dom, _heads(vpi, ip, H, False), 2, 2) - dl)
                dsc = pc * (_bdot(dom, _heads(vc_ref, i, H, False), 2, 2) - dl)
                db_ref[:, :, :BLK] += dsp
                db_ref[:, :, BLK:] += dsc
                dspb, dscb = dsp.astype(BF16), dsc.astype(BF16)
                _store_pairs(dq_ref, i, _bdot(dspb, _heads(kpi, ip, H, True), 2, 1)
                             + _bdot(dscb, _heads(kc_ref, i, H, True), 2, 1), BF16, scale)
                dkp = _pair_sums(_bdot(dspb, qm, 1, 1))
                dkc = _pair_sums(_bdot(dscb, qm, 1, 1))
                dvp = _pair_sums(_bdot(pp.astype(BF16), dom, 1, 1))
                dvc = _pair_sums(_bdot(pc.astype(BF16), dom, 1, 1))
                for p in range(H // 2):
                    cols = slice(p * PAIR, (p + 1) * PAIR)
                    if i == 0:
                        dk_ref[_blk(QB - 1), cols] += scale * dkp[p]
                        dv_ref[_blk(QB - 1), cols] += dvp[p]
                    else:
                        ck_ref[_blk(i - 1), cols] += scale * dkp[p]
                        cv_ref[_blk(i - 1), cols] += dvp[p]
                    ck_ref[_blk(i), cols] = scale * dkc[p]
                    cv_ref[_blk(i), cols] = dvc[p]

        @pl.when(b == ns)
        def _():
            dk_ref[...] = ck_ref[...]
            dv_ref[...] = cv_ref[...]

    f32 = jax.ShapeDtypeStruct((T, HD), F32)
    return pl.pallas_call(
        body, name=name, grid=(ns + 1,),
        in_specs=[cur(qc), prev(kc_), cur(kc_), prev(vc_), cur(vc_), bias0, bias1, cur(0), stat, stat],
        out_specs=[cur(0), late, late, pl.BlockSpec((H, BLK, 2 * BLK), lambda b: (0, 0, 0))],
        out_shape=[jax.ShapeDtypeStruct((T, HD), BF16), f32, f32,
                   jax.ShapeDtypeStruct((H, BLK, 2 * BLK), F32)],
        scratch_shapes=[pltpu.VMEM((QB * BLK, HD), F32), pltpu.VMEM((QB * BLK, HD), F32)],
        compiler_params=_cp(("arbitrary",)),
    )(qa, ka, ka, va, va, bias, bias, do, lse, delta)


def _rel_grad(dbs, *, name):
    H = dbs[0].shape[1]
    n = len(dbs)
    assert H <= LANES
    bkt = jnp.asarray(_bucket_table())

    def body(*refs):
        bk = refs[n][...]
        o_ref = refs[n + 1]
        db = refs[0][...]
        for r in refs[1:n]:
            db = db + r[...]
        row = lax.broadcasted_iota(jnp.int32, (REL_BUCKETS, LANES), 0)
        col = lax.broadcasted_iota(jnp.int32, (REL_BUCKETS, LANES), 1)

        def bucket(c, acc):
            sel = bk == c
            for h in range(H):
                val = jnp.sum(jnp.where(sel, db[h], 0.0))
                acc = jnp.where((row == c) & (col == h), val, acc)
            return acc

        o_ref[...] = lax.fori_loop(0, REL_BUCKETS, bucket, jnp.zeros((REL_BUCKETS, LANES), F32))

    dspec = pl.BlockSpec((None, H, BLK, 2 * BLK), lambda g: (g, 0, 0, 0))
    return pl.pallas_call(
        body, name=name, grid=(N_GROUPS,),
        in_specs=[dspec] * n + [pl.BlockSpec((None, BLK, 2 * BLK), lambda g: (g, 0, 0))],
        out_specs=pl.BlockSpec((None, REL_BUCKETS, LANES), lambda g: (g, 0, 0)),
        out_shape=jax.ShapeDtypeStruct((N_GROUPS, REL_BUCKETS, LANES), F32),
        compiler_params=_cp(("parallel",)),
    )(*dbs, bkt)


def _to_group(a, d):
    T, C = a.shape
    return a if d == 1 else a.reshape(T // d, d, C).transpose(1, 0, 2).reshape(T, C)


def _from_group(a, d):
    T, C = a.shape
    return a if d == 1 else a.reshape(d, T // d, C).transpose(1, 0, 2).reshape(T, C)


def _rows_view(a):
    a2 = a.reshape(-1, a.shape[-1])
    R, C = a2.shape
    tr = _tile(R, max(SUBLANES, (1 << 18) // C), SUBLANES)
    return a2, R, C, tr


def _addn(xs, out_dtype, *, name):
    shape = xs[0].shape
    x2s = [_rows_view(x)[0] for x in xs]
    _, R, C, tr = _rows_view(xs[0])
    spec = pl.BlockSpec((tr, C), lambda i: (i, 0))

    def body(*refs):
        acc = refs[0][...].astype(F32)
        for r in refs[1:-1]:
            acc = acc + r[...].astype(F32)
        refs[-1][...] = acc.astype(out_dtype)

    out = pl.pallas_call(
        body, name=name, grid=(R // tr,),
        in_specs=[spec] * len(xs), out_specs=spec,
        out_shape=jax.ShapeDtypeStruct((R, C), out_dtype),
        compiler_params=_cp(("parallel",)),
    )(*x2s)
    return out.reshape(shape)


def _slot_sum(b_ref, own, my_id):
    acc = None
    for s in range(b_ref.shape[0]):
        term = jnp.where(my_id == s, own, b_ref[s].astype(F32))
        acc = term if acc is None else acc + term
    return acc


def _part_geometry(shape, axis):
    _, R, C = shape
    if axis == COL:
        return R // 2, C // N_CHIPS
    return R // (2 * N_CHIPS), C


def _sum_parts(buf, grad, axis, where, *, name):
    n, L, Rp, Cp = buf.shape
    assert (Rp, Cp) == _part_geometry(grad.shape, axis)
    tr = _tile(Rp, max(2 * SUBLANES, (1 << 17) // Cp), 2 * SUBLANES)
    nr = Rp // tr

    def own_map(l, i, w):
        if axis == COL:
            return (l, w[1] * nr + i, w[0])
        return (l, (2 * w[0] + w[1]) * nr + i, 0)

    def body(w_ref, b_ref, g_ref, o_ref):
        o_ref[...] = _slot_sum(b_ref, g_ref[...].astype(F32), w_ref[2])

    return pl.pallas_call(
        body, name=name,
        grid_spec=pltpu.PrefetchScalarGridSpec(
            num_scalar_prefetch=1, grid=(L, nr),
            in_specs=[pl.BlockSpec((n, None, tr, Cp), lambda l, i, w: (0, l, i, 0)),
                      pl.BlockSpec((None, tr, Cp), own_map)],
            out_specs=pl.BlockSpec((None, None, tr, Cp), lambda l, i, w: (l, w[1], i, 0))),
        out_shape=jax.ShapeDtypeStruct((L, 2, Rp, Cp), F32),
        compiler_params=_cp(("parallel", "parallel")),
    )(where, buf, grad)


def _sum_small(buf, own, where, *, name):
    n, R, C = buf.shape
    tr = _tile(R, 512, SUBLANES)

    def body(w_ref, b_ref, g_ref, o_ref):
        o_ref[...] = _slot_sum(b_ref, g_ref[...], w_ref[2])

    return pl.pallas_call(
        body, name=name,
        grid_spec=pltpu.PrefetchScalarGridSpec(
            num_scalar_prefetch=1, grid=(R // tr,),
            in_specs=[pl.BlockSpec((n, tr, C), lambda i, w: (0, i, 0)),
                      pl.BlockSpec((tr, C), lambda i, w: (i, 0))],
            out_specs=pl.BlockSpec((tr, C), lambda i, w: (i, 0))),
        out_shape=jax.ShapeDtypeStruct((R, C), F32),
        compiler_params=_cp(("parallel",)),
    )(where, buf, own)


def _place_shard(shard, layer, axis, where, out_dtype, *, name, after=None):
    _, R, C = shard.shape
    tr = _tile(R, max(2 * SUBLANES, (1 << 18) // C), 2 * SUBLANES)
    nr = R // tr
    if axis == COL:
        full, out_map = (1, R, C * N_CHIPS), (lambda i, w: (0, i, w[0]))
    else:
        full, out_map = (1, R * N_CHIPS, C), (lambda i, w: (0, w[0] * nr + i, 0))

    def body(w_ref, x_ref, *rest):
        rest[-1][...] = x_ref[...].astype(out_dtype)

    extra = [] if after is None else [after]
    return pl.pallas_call(
        body, name=name,
        grid_spec=pltpu.PrefetchScalarGridSpec(
            num_scalar_prefetch=1, grid=(nr,),
            in_specs=[pl.BlockSpec((None, tr, C), lambda i, w: (layer, i, 0))]
            + [pl.BlockSpec(memory_space=pl.ANY)] * len(extra),
            out_specs=pl.BlockSpec((None, tr, C), out_map)),
        out_shape=jax.ShapeDtypeStruct(full, out_dtype),
        compiler_params=_cp(("parallel",)),
    )(where, shard, *extra)


def _loss_grad(y, tgt, *, name):
    T, D = y.shape
    tr = _tile(T, 512, SUBLANES)
    row = pl.BlockSpec((tr, D), lambda i: (i, 0))

    def body(y_ref, t_ref, dy_ref, l_ref, acc_ref):
        i = pl.program_id(0)

        @pl.when(i == 0)
        def _():
            acc_ref[...] = jnp.zeros_like(acc_ref)

        err = y_ref[...] - t_ref[...]
        dy_ref[...] = err * (1.0 / D)
        acc_ref[...] += jnp.sum(err * err, axis=0, keepdims=True)

        @pl.when(i == T // tr - 1)
        def _():
            tot = jnp.sum(acc_ref[...], axis=1, keepdims=True) * (0.5 / D)
            l_ref[...] = jnp.broadcast_to(tot, (1, LANES))

    return pl.pallas_call(
        body, name=name, grid=(T // tr,),
        in_specs=[row, row],
        out_specs=[row, pl.BlockSpec((1, LANES), lambda i: (0, 0))],
        out_shape=[jax.ShapeDtypeStruct((T, D), F32), jax.ShapeDtypeStruct((1, LANES), F32)],
        scratch_shapes=[pltpu.VMEM((1, D), F32)],
        compiler_params=_cp(("arbitrary",)),
    )(y, tgt)


def _adamw(w, g, m, v, *, name):
    shape = w.shape
    w2, R, C, tr = _rows_view(w)
    spec = pl.BlockSpec((tr, C), lambda i: (i, 0))

    def body(w_ref, g_ref, m_ref, v_ref, d_ref, nm_ref, nv_ref):
        gv = g_ref[...]
        nm = ADAM_B1 * m_ref[...] + (1.0 - ADAM_B1) * gv
        nv = ADAM_B2 * v_ref[...] + (1.0 - ADAM_B2) * (gv * gv)
        m_hat = nm / (1.0 - ADAM_B1 ** ADAM_STEP)
        v_hat = nv / (1.0 - ADAM_B2 ** ADAM_STEP)
        d_ref[...] = -ADAM_LR * (m_hat / (jnp.sqrt(v_hat) + ADAM_EPS) + ADAM_WD * w_ref[...])
        nm_ref[...] = nm
        nv_ref[...] = nv

    shp = jax.ShapeDtypeStruct((R, C), F32)
    outs = pl.pallas_call(
        body, name=name, grid=(R // tr,),
        in_specs=[spec] * 4, out_specs=[spec] * 3, out_shape=[shp] * 3,
        compiler_params=_cp(("parallel",)),
    )(w2, g.reshape(R, C), m.reshape(R, C), v.reshape(R, C))
    return tuple(o.reshape(shape) for o in outs)


def _pack(arrs):
    flat = jnp.concatenate([a.reshape(-1).astype(F32) for a in arrs])
    unit = SUBLANES * LANES
    pad = (-flat.shape[0]) % unit
    return jnp.pad(flat, (0, pad)).reshape(-1, LANES)


def _unpack(buf, shapes):
    flat = buf.reshape(-1)
    out, off = [], 0
    for s in shapes:
        n = int(np.prod(s))
        out.append(flat[off:off + n].reshape(s))
        off += n
    return out


def _me():
    return lax.axis_index("x"), lax.axis_index("y"), lax.axis_index("c")


def _flip(pos, k):
    x, y, c = pos
    return (1 - x if k & 4 else x, 1 - y if k & 2 else y, 1 - c if k & 1 else c)


HBM = pl.BlockSpec(memory_space=pltpu.HBM)

COL, ROW = -1, -2


def _shard_window(ref, axis, j, n):
    start = pl.multiple_of(j * n, n)
    if axis == COL:
        return ref.at[:, :, pl.ds(start, n)]
    return ref.at[:, pl.ds(start, n), :]


GATHER_PEERS = (2, 4, 6)
SEM = pl.BlockSpec(memory_space=pltpu.SEMAPHORE)
EFFECT = pltpu.SideEffectType.DATAFLOW_SIDE_EFFECTING


def _gather_copy(buf, axis, i, pi, chip, me, send_sems, recv_sems):
    win = _shard_window(buf, axis, chip, buf.shape[axis] // N_CHIPS)
    k = len(GATHER_PEERS) * i + pi
    return pltpu.make_async_remote_copy(
        src_ref=win, dst_ref=win, send_sem=send_sems.at[k], recv_sem=recv_sems.at[k],
        device_id=_flip(me, GATHER_PEERS[pi]), device_id_type=MESH)


def _gather_start(fulls, axes, *, name):
    n = len(fulls)
    n_sem = len(GATHER_PEERS) * n

    def body(*refs):
        send_sems, recv_sems = refs[n], refs[n + 1]
        bufs = refs[n + 2:2 * n + 2]
        token = refs[2 * n + 2]
        me = _me()
        for i in range(n):
            for pi in range(len(GATHER_PEERS)):
                _gather_copy(bufs[i], axes[i], i, pi, 2 * me[0] + me[1], me, send_sems, recv_sems).start()
        token[...] = jnp.zeros_like(token)

    outs = pl.pallas_call(
        body, name=name,
        in_specs=[HBM] * n, out_specs=[SEM, SEM] + [HBM] * n + [pl.BlockSpec(memory_space=pltpu.VMEM)],
        out_shape=[pltpu.SemaphoreType.DMA((n_sem,)), pltpu.SemaphoreType.DMA((n_sem,))]
        + [pltpu.HBM(f.shape, f.dtype) for f in fulls] + [jax.ShapeDtypeStruct((SUBLANES, LANES), F32)],
        input_output_aliases={i: 2 + i for i in range(n)},
        compiler_params=pltpu.CompilerParams(has_side_effects=EFFECT),
    )(*[pltpu.with_memory_space_constraint(f, pltpu.HBM) for f in fulls])
    return outs[0], outs[1], list(outs[2:2 + n]), outs[-1]


def _gather_wait(send_sems, recv_sems, bufs, axes, idxs, after, *, name):
    m = len(bufs)

    def body(*refs):
        ss, rs = refs[m], refs[m + 1]
        outs = refs[m + 3:]
        me = _me()
        for t, i in enumerate(idxs):
            for pi, k in enumerate(GATHER_PEERS):
                px, py, _ = _flip(me, k)
                _gather_copy(outs[t], axes[t], i, pi, 2 * me[0] + me[1], me, ss, rs).wait_send()
                _gather_copy(outs[t], axes[t], i, pi, 2 * px + py, me, ss, rs).wait_recv()

    return pl.pallas_call(
        body, name=name,
        in_specs=[HBM] * m + [SEM, SEM, pl.BlockSpec(memory_space=pl.ANY)], out_specs=[HBM] * m,
        out_shape=[pltpu.HBM(b.shape, b.dtype) for b in bufs],
        input_output_aliases={t: t for t in range(m)},
        compiler_params=pltpu.CompilerParams(has_side_effects=EFFECT),
    )(*bufs, send_sems, recv_sems, after)


def _grad_part(ref, layer, axis, pos):
    px, py, pc = pos
    Rp, Cp = _part_geometry(ref.shape, axis)
    chip = 2 * px + py
    if axis == COL:
        return ref.at[layer, pl.ds(pl.multiple_of(pc * Rp, Rp), Rp), pl.ds(pl.multiple_of(chip * Cp, Cp), Cp)]
    return ref.at[layer, pl.ds(pl.multiple_of((2 * chip + pc) * Rp, Rp), Rp), :]


def _scatter_copy(grad, slots, layer, axis, i, k, src_pos, dst_slot, me, send_sems, recv_sems):
    sem = (N_DEV - 1) * i + k - 1
    return pltpu.make_async_remote_copy(
        src_ref=_grad_part(grad, layer, axis, src_pos), dst_ref=slots.at[dst_slot, layer],
        send_sem=send_sems.at[sem], recv_sem=recv_sems.at[sem],
        device_id=_flip(me, k), device_id_type=MESH)


def _scatter_start(grads, slots, layers, axes, *, name):
    n = len(grads)
    n_sem = (N_DEV - 1) * n

    def body(*refs):
        send_sems, recv_sems = refs[2 * n], refs[2 * n + 1]
        g_refs, s_refs = refs[2 * n + 2:3 * n + 2], refs[3 * n + 2:4 * n + 2]
        token = refs[4 * n + 2]
        me = _me()
        my_id = 4 * me[0] + 2 * me[1] + me[2]
        for i in range(n):
            for k in range(1, N_DEV):
                _scatter_copy(g_refs[i], s_refs[i], layers[i], axes[i], i, k, _flip(me, k), my_id, me,
                              send_sems, recv_sems).start()
        token[...] = jnp.zeros_like(token)

    arrays = list(grads) + list(slots)
    outs = pl.pallas_call(
        body, name=name,
        in_specs=[HBM] * (2 * n),
        out_specs=[SEM, SEM] + [HBM] * (2 * n) + [pl.BlockSpec(memory_space=pltpu.VMEM)],
        out_shape=[pltpu.SemaphoreType.DMA((n_sem,)), pltpu.SemaphoreType.DMA((n_sem,))]
        + [pltpu.HBM(a.shape, a.dtype) for a in arrays] + [jax.ShapeDtypeStruct((SUBLANES, LANES), F32)],
        input_output_aliases={i: 2 + i for i in range(2 * n)},
        compiler_params=pltpu.CompilerParams(has_side_effects=EFFECT),
    )(*[pltpu.with_memory_space_constraint(a, pltpu.HBM) for a in arrays])
    return outs[0], outs[1], list(outs[2:2 + n]), list(outs[2 + n:2 + 2 * n]), outs[-1]


def _scatter_wait(sems, grads, slots, axes, plan, *, name):
    n = len(grads)
    flat_sems = [s for pair in sems for s in pair]

    def body(*refs):
        sem_refs = refs[2 * n:2 * n + len(flat_sems)]
        g_refs, s_refs = refs[2 * n + len(flat_sems):3 * n + len(flat_sems)], refs[3 * n + len(flat_sems):]
        me = _me()
        for j, items in enumerate(plan):
            ss, rs = sem_refs[2 * j], sem_refs[2 * j + 1]
            for i, (a, layer) in enumerate(items):
                for k in range(1, N_DEV):
                    px, py, pc = _flip(me, k)
                    cp = _scatter_copy(g_refs[a], s_refs[a], layer, axes[a], i, k, me, 4 * px + 2 * py + pc, me, ss, rs)
                    cp.wait_send()
                    cp.wait_recv()

    arrays = list(grads) + list(slots)
    outs = pl.pallas_call(
        body, name=name,
        in_specs=[HBM] * (2 * n) + [SEM] * len(flat_sems), out_specs=[HBM] * (2 * n),
        out_shape=[pltpu.HBM(a.shape, a.dtype) for a in arrays],
        input_output_aliases={i: i for i in range(2 * n)},
        compiler_params=pltpu.CompilerParams(has_side_effects=EFFECT),
    )(*arrays, *flat_sems)
    return list(outs[:n]), list(outs[n:])


def _small_copy(src, dst, k, slot, me, send_sems, recv_sems):
    return pltpu.make_async_remote_copy(
        src_ref=src, dst_ref=dst.at[slot], send_sem=send_sems.at[k - 1], recv_sem=recv_sems.at[k - 1],
        device_id=_flip(me, k), device_id_type=MESH)


def _small_start(small, slots, *, name):
    def body(src_in, dst_in, send_sems, recv_sems, src, dst):
        me = _me()
        for k in range(1, N_DEV):
            _small_copy(src, dst, k, 4 * me[0] + 2 * me[1] + me[2], me, send_sems, recv_sems).start()

    arrays = [small, slots]
    return pl.pallas_call(
        body, name=name, in_specs=[HBM, HBM], out_specs=[SEM, SEM, HBM, HBM],
        out_shape=[pltpu.SemaphoreType.DMA((N_DEV - 1,)), pltpu.SemaphoreType.DMA((N_DEV - 1,))]
        + [pltpu.HBM(a.shape, a.dtype) for a in arrays],
        input_output_aliases={0: 2, 1: 3},
        compiler_params=pltpu.CompilerParams(has_side_effects=EFFECT),
    )(*[pltpu.with_memory_space_constraint(a, pltpu.HBM) for a in arrays])


def _small_wait(send_sems, recv_sems, small, slots, after, *, name):
    def body(src_in, dst_in, ss, rs, after_ref, src, dst):
        me = _me()
        for k in range(1, N_DEV):
            px, py, pc = _flip(me, k)
            cp = _small_copy(src, dst, k, 4 * px + 2 * py + pc, me, ss, rs)
            cp.wait_send()
            cp.wait_recv()

    return pl.pallas_call(
        body, name=name, in_specs=[HBM, HBM, SEM, SEM, pl.BlockSpec(memory_space=pl.ANY)], out_specs=[HBM, HBM],
        out_shape=[pltpu.HBM(small.shape, small.dtype), pltpu.HBM(slots.shape, slots.dtype)],
        input_output_aliases={0: 0, 1: 1},
        compiler_params=pltpu.CompilerParams(has_side_effects=EFFECT),
    )(small, slots, send_sems, recv_sems, after)


def _pair_gather(halves, *, name):
    n = len(halves)

    def body(*refs):
        bufs = refs[n:2 * n]
        send_sems, recv_sems = refs[2 * n:]
        me = _me()
        c = me[2]
        sib = _flip(me, 1)

        def copy(i, half):
            return pltpu.make_async_remote_copy(
                src_ref=bufs[i].at[:, half], dst_ref=bufs[i].at[:, half], send_sem=send_sems.at[i],
                recv_sem=recv_sems.at[i], device_id=sib, device_id_type=MESH)

        sends = [copy(i, c) for i in range(n)]
        for cp in sends:
            cp.start()
        for i in range(n):
            copy(i, 1 - c).wait_recv()
        for cp in sends:
            cp.wait_send()

    return pl.pallas_call(
        body, name=name,
        in_specs=[HBM] * n, out_specs=[HBM] * n,
        out_shape=[jax.ShapeDtypeStruct(h.shape, h.dtype) for h in halves],
        input_output_aliases={i: i for i in range(n)},
        scratch_shapes=[pltpu.SemaphoreType.DMA((n,)), pltpu.SemaphoreType.DMA((n,))],
    )(*halves)


def _local_step(x, tgt, P, fetch, ship, n_layers):
    T, D = x.shape
    P = dict(P)
    W = {}
    depth = P["ffn_conv_b"].shape[0]
    n_a = P["a_w_s"].shape[0]
    alpha = (2 * depth) ** 0.25
    H = P["rel_table"].shape[1] // N_GROUPS
    HD = H * HEAD_DIM
    dils = [d for _, d in DILATED_GROUPS]
    assert T % (dils[-1] * BLK) == 0

    def arrive(stage, after):
        w_new, p_new = fetch(stage, after)
        W.update(w_new)
        P.update(p_new)

    bias = _band_bias(P["rel_table"], H, name="band_bias")
    saved = []
    xf, xb = x, x.astype(BF16)
    kg = vg = None
    for i in range(depth):
        s = {"x": xf, "xb": xb}
        arrive(f"{i}a", xf)
        if i < n_a:
            s["zp"] = _mm(xb, W["a_w_in", i], out_dtype=BF16, name=f"a{i}_in")
            arrive(f"{i}b", s["zp"])
            s["y"] = _sgu_fwd(s["zp"], P["a_w_s"][i], P["a_b_s"][i], P["a_ln_g"][i], P["a_ln_b"][i], name=f"a{i}_sgu")
            s["h"] = _mm(s["y"], W["a_w_out", i], name=f"a{i}_out")
        else:
            j = i - n_a
            if j == 0:
                kv = _mm(xb, W["kv_w", 0], out_dtype=BF16, name="kv_proj")
                kg = [(kv, 0) if d == 1 else (_to_group(kv[:, :HD], d), 0) for d in dils]
                vg = [(kv, 1) if d == 1 else (_to_group(kv[:, HD:], d), 0) for d in dils]
            q = _mm(xb, W["b_w_q", j], out_dtype=BF16, name=f"b{j}_q")
            s["qg"] = [(q, g) if d == 1 else (_to_group(q[:, g * HD:(g + 1) * HD], d), 0) for g, d in enumerate(dils)]
            os, ls = [], []
            for g, d in enumerate(dils):
                o_g, l_g = _attn_fwd(s["qg"][g], kg[g], vg[g], bias, g, d, name=f"b{j}_attn{g}")
                os.append(_from_group(o_g, d))
                ls.append(_from_group(l_g, d))
            s["oc"], s["ob"], s["lse"] = _attn_combine(os, ls, name=f"b{j}_comb")
            arrive(f"{i}b", s["ob"])
            s["h"] = _mm(s["ob"], W["b_w_o", j], name=f"b{j}_o")
        s["x1"], s["x1b"] = _add_ln(xf, s["h"], P["ln_g"][i, 0], P["ln_b"][i, 0], alpha, name=f"l{i}_ln1")
        arrive(f"{i}c", s["x1b"])
        s["hup"] = _mm(s["x1b"], W["ffn_w_up", i], out_dtype=BF16, name=f"l{i}_up")
        arrive(f"{i}d", s["hup"])
        s["act"], s["ca"], s["cg"] = _convglu_fwd(s["hup"], P["ffn_conv_w"][i], P["ffn_conv_b"][i], name=f"l{i}_glu")
        s["f"] = _mm(s["act"], W["ffn_w_down", i], name=f"l{i}_down")
        xf, xb = _add_ln(s["x1"], s["f"], P["ln_g"][i, 1], P["ln_b"][i, 1], alpha, name=f"l{i}_ln2")
        saved.append(s)

    G, loss_row = _loss_grad(xf, tgt, name="loss")

    gw = {}

    def dw(key, layer, a, b, name, **kw):
        gw[key] = _mm(a, b, ta=True, out_dtype=BF16, out_into=(gw.get(key), n_layers[key], layer), name=name, **kw)

    gp = {k: [None] * n_a for k in ("a_ln_g", "a_ln_b", "a_w_s", "a_b_s")}
    gp.update({k: [None] * depth for k in ("ffn_conv_w", "ffn_conv_b", "ln_g", "ln_b")})
    dk_parts, dv_parts, dbias_parts = [], [], []
    for i in reversed(range(depth)):
        s = saved[i]
        dr2, dr2b, dg2, db2 = _ln_bwd(G, s["x1"], s["f"], P["ln_g"][i, 1], alpha, name=f"l{i}_ln2_bwd")
        dw("ffn_w_down", i, s["act"], dr2b, f"l{i}_down_dw")
        dact = _mm(dr2b, W["ffn_w_down", i], tb=True, out_dtype=BF16, name=f"l{i}_down_dx")
        dhup, dcw, dcb = _convglu_bwd(s["hup"], s["ca"], s["cg"], dact, P["ffn_conv_w"][i], name=f"l{i}_glu_bwd")
        gp["ffn_conv_w"][i] = dcw.transpose(1, 0, 2).reshape(dcw.shape[1], -1)
        gp["ffn_conv_b"][i] = dcb.reshape(-1)
        dw("ffn_w_up", i, s["x1b"], dhup, f"l{i}_up_dw", b_halves=True)
        token = ship(gw, [("ffn_w_down", i), ("ffn_w_up", i)])
        G1 = _mm(dhup, W["ffn_w_up", i], tb=True, a_halves=True, add=dr2, add_scale=alpha, after=token,
                 name=f"l{i}_up_dx")
        dr1, dr1b, dg1, db1 = _ln_bwd(G1, s["x"], s["h"], P["ln_g"][i, 0], alpha, name=f"l{i}_ln1_bwd")
        gp["ln_g"][i] = jnp.concatenate([dg1, dg2], axis=0)
        gp["ln_b"][i] = jnp.concatenate([db1, db2], axis=0)
        if i < n_a:
            dw("a_w_out", i, s["y"], dr1b, f"a{i}_out_dw")
            token = ship(gw, [("a_w_out", i)])
            dy = _mm(dr1b, W["a_w_out", i], tb=True, out_dtype=BF16, after=token, name=f"a{i}_out_dx")
            dzp, dws, dbs, dlg, dlb = _sgu_bwd(s["zp"], dy, P["a_w_s"][i], P["a_b_s"][i], P["a_ln_g"][i],
                                               P["a_ln_b"][i], name=f"a{i}_sgu_bwd")
            gp["a_w_s"][i], gp["a_b_s"][i] = dws, dbs[:, :dws.shape[0]].T
            gp["a_ln_g"][i], gp["a_ln_b"][i] = dlg[0], dlb[0]
            dw("a_w_in", i, s["xb"], dzp, f"a{i}_in_dw")
            token = ship(gw, [("a_w_in", i)])
            G = _mm(dzp, W["a_w_in", i], tb=True, add=dr1, add_scale=alpha, after=token, name=f"a{i}_in_dx")
        else:
            j = i - n_a
            dw("b_w_o", j, s["ob"], dr1b, f"b{j}_o_dw")
            do_tok = _mm(dr1b, W["b_w_o", j], tb=True, name=f"b{j}_o_dx")
            delta, dob = _attn_bwd_prep(do_tok, s["oc"], name=f"b{j}_prep")
            dqs, dbs = [], []
            for g, d in enumerate(dils):
                dq, dk, dv, db = _attn_bwd(s["qg"][g], kg[g], vg[g], bias, g, d, _to_group(dob, d),
                                           _to_group(s["lse"], d), _to_group(delta, d), name=f"b{j}_attn{g}_bwd")
                dqs.append(_from_group(dq, d))
                dk_parts.append(_from_group(dk, d))
                dv_parts.append(_from_group(dv, d))
                dbs.append(db)
            dbias_parts.append(jnp.stack(dbs))
            dq_tok = jnp.concatenate(dqs, axis=1)
            dw("b_w_q", j, s["xb"], dq_tok, f"b{j}_q_dw")
            token = ship(gw, [("b_w_o", j), ("b_w_q", j)])
            G = _mm(dq_tok, W["b_w_q", j], tb=True, add=dr1, add_scale=alpha, after=token, name=f"b{j}_q_dx")
            if j == 0:
                dkv = jnp.concatenate([_addn(dk_parts, BF16, name="dk_sum"), _addn(dv_parts, BF16, name="dv_sum")], axis=1)
                dw("kv_w", 0, s["xb"], dkv, "kv_dw")
                token = ship(gw, [("kv_w", 0)])
                G = _mm(dkv, W["kv_w", 0], tb=True, add=G, add_scale=1.0, after=token, name="kv_dx")
    rel = _rel_grad(dbias_parts, name="rel_grad")
    grel = rel[:, :, :H].transpose(1, 0, 2).reshape(REL_BUCKETS, N_GROUPS * H)
    gsmall = {k: jnp.stack(v) for k, v in gp.items()}
    gsmall["rel_table"] = grel
    return loss_row, G, gw, gsmall


BIG = (("a_w_in", COL), ("a_w_out", ROW), ("kv_w", ROW), ("b_w_q", COL), ("b_w_o", COL),
       ("ffn_w_up", COL), ("ffn_w_down", ROW))
SMALL_SHARDED = ("a_ln_g", "a_ln_b", "ffn_conv_w", "ln_g", "ln_b")
SMALL_REPLICATED = ("a_w_s", "a_b_s", "rel_table", "ffn_conv_b")
WEIGHTS = ("a_w_in", "a_ln_g", "a_ln_b", "a_w_s", "a_b_s", "a_w_out", "kv_w", "b_w_q", "b_w_o", "rel_table",
           "ffn_w_up", "ffn_conv_w", "ffn_conv_b", "ffn_w_down", "ln_g", "ln_b")


def kernel(x, a_w_in, a_ln_g, a_ln_b, a_w_s, a_b_s, a_w_out, kv_w, b_w_q, b_w_o, rel_table, ffn_w_up, ffn_conv_w, ffn_conv_b, ffn_w_down, ln_g, ln_b, loss_target, m_a_w_in, m_a_ln_g, m_a_ln_b, m_a_w_s, m_a_b_s, m_a_w_out, m_kv_w, m_b_w_q, m_b_w_o, m_rel_table, m_ffn_w_up, m_ffn_conv_w, m_ffn_conv_b, m_ffn_w_down, m_ln_g, m_ln_b, v_a_w_in, v_a_ln_g, v_a_ln_b, v_a_w_s, v_a_b_s, v_a_w_out, v_kv_w, v_b_w_q, v_b_w_o, v_rel_table, v_ffn_w_up, v_ffn_conv_w, v_ffn_conv_b, v_ffn_w_down, v_ln_g, v_ln_b):
    w = dict(a_w_in=a_w_in, a_ln_g=a_ln_g, a_ln_b=a_ln_b, a_w_s=a_w_s, a_b_s=a_b_s, a_w_out=a_w_out, kv_w=kv_w,
             b_w_q=b_w_q, b_w_o=b_w_o, rel_table=rel_table, ffn_w_up=ffn_w_up, ffn_conv_w=ffn_conv_w,
             ffn_conv_b=ffn_conv_b, ffn_w_down=ffn_w_down, ln_g=ln_g, ln_b=ln_b)
    m = dict(a_w_in=m_a_w_in, a_ln_g=m_a_ln_g, a_ln_b=m_a_ln_b, a_w_s=m_a_w_s, a_b_s=m_a_b_s, a_w_out=m_a_w_out,
             kv_w=m_kv_w, b_w_q=m_b_w_q, b_w_o=m_b_w_o, rel_table=m_rel_table, ffn_w_up=m_ffn_w_up,
             ffn_conv_w=m_ffn_conv_w, ffn_conv_b=m_ffn_conv_b, ffn_w_down=m_ffn_w_down, ln_g=m_ln_g, ln_b=m_ln_b)
    v = dict(a_w_in=v_a_w_in, a_ln_g=v_a_ln_g, a_ln_b=v_a_ln_b, a_w_s=v_a_w_s, a_b_s=v_a_b_s, a_w_out=v_a_w_out,
             kv_w=v_kv_w, b_w_q=v_b_w_q, b_w_o=v_b_w_o, rel_table=v_rel_table, ffn_w_up=v_ffn_w_up,
             ffn_conv_w=v_ffn_conv_w, ffn_conv_b=v_ffn_conv_b, ffn_w_down=v_ffn_w_down, ln_g=v_ln_g, ln_b=v_ln_b)
    chip = 2 * lax.axis_index("x") + lax.axis_index("y")

    big_names = [n for n, _ in BIG]
    big_axes = [a for _, a in BIG]
    where = jnp.stack([chip, lax.axis_index("c"), 2 * chip + lax.axis_index("c")]).astype(jnp.int32)
    shards = [w[n].reshape((1,) + w[n].shape) if w[n].ndim == 2 else w[n] for n in big_names]
    small_shapes = [w[n].shape for n in SMALL_SHARDED]
    small = _pack([w[n] for n in SMALL_SHARDED])
    shard_of = dict(zip(big_names, shards))
    axis_of = dict(BIG)
    n_layers = {n: s.shape[0] for n, s in shard_of.items()}
    n_a = w["a_w_s"].shape[0]
    stages = {}
    for i in range(w["ffn_w_up"].shape[0]):
        if i < n_a:
            stages[f"{i}a"], stages[f"{i}b"] = [("a_w_in", i)], [("a_w_out", i)]
        else:
            stages[f"{i}a"] = ([("kv_w", 0)] if i == n_a else []) + [("b_w_q", i - n_a)]
            stages[f"{i}b"] = [("b_w_o", i - n_a)]
        stages[f"{i}c"], stages[f"{i}d"] = [("ffn_w_up", i)], [("ffn_w_down", i)]
    stages["0a"].append(("small", 0))
    assert sum(len(st) for st in stages.values()) == sum(n_layers.values()) + 1
    flights, full_shape, all_started = {}, {}, None
    for batch, names in enumerate(([st for st in stages if st[0] == "0"], [st for st in stages if st[0] != "0"])):
        order = [key for st in names for key in stages[st]]
        placed, axes = [], []
        for n, l in order:
            if n == "small":
                placed.append(_place_shard(small[None], 0, ROW, where, F32, name="place_small"))
                axes.append(ROW)
            else:
                placed.append(_place_shard(shard_of[n], l, axis_of[n], where, BF16, name=f"place_{n}{l}",
                                           after=all_started))
                axes.append(axis_of[n])
                full_shape[n] = (n_layers[n],) + placed[-1].shape[1:]
        send_sems, recv_sems, flying, all_started = _gather_start(placed, axes, name=f"gather_start{batch}")
        for st in names:
            flights[st] = (send_sems, recv_sems, flying, axes, order)

    def fetch(stage, after):
        if stage not in stages:
            return {}, {}
        send_sems, recv_sems, flying, axes, order = flights[stage]
        if stage == "0a":
            after = all_started
        idxs = [order.index(key) for key in stages[stage]]
        landed = _gather_wait(send_sems, recv_sems, [flying[i] for i in idxs], [axes[i] for i in idxs], idxs, after,
                              name=f"gather_wait_{stage}")
        W_new, P_new = {}, {}
        for key, arr in zip(stages[stage], landed):
            if key[0] == "small":
                small_all = arr.reshape((N_CHIPS,) + small.shape)
                per_chip = [_unpack(small_all[j], small_shapes) for j in range(N_CHIPS)]
                for i, n in enumerate(SMALL_SHARDED):
                    P_new[n] = jnp.concatenate([per_chip[j][i] for j in range(N_CHIPS)], axis=-1)
            else:
                W_new[key] = arr.reshape(arr.shape[1:])
        return W_new, P_new

    slots ={n: lax.empty((N_DEV, n_layers[n]) + _part_geometry(full_shape[n], axis_of[n]), BF16) for n in big_names}
    sems, plan = [], []

    def ship(gw, items):
        names = [n for n, _ in items]
        send, recv, grads, landing, token = _scatter_start(
            [gw[n] for n in names], [slots[n] for n in names], [l for _, l in items], [axis_of[n] for n in names],
            name="scatter_start_" + "_".join(f"{n}{l}" for n, l in items))
        for n, g, s in zip(names, grads, landing):
            gw[n], slots[n] = g, s
        sems.append((send, recv))
        plan.append([(big_names.index(n), l) for n, l in items])
        return token

    loss_row, grad_x, gw, gsmall = _local_step(x[0], loss_target[0], {n: w[n] for n in SMALL_REPLICATED}, fetch,
                                               ship, n_layers)

    small_names = list(SMALL_SHARDED) + list(SMALL_REPLICATED)
    small_pack = _pack([gsmall[n] for n in small_names] + [loss_row[0, :1]])
    small_sems = _small_start(small_pack, lax.empty((N_DEV,) + small_pack.shape, F32), name="small_start")
    own, landed = _scatter_wait(sems, [gw[n] for n in big_names], [slots[n] for n in big_names], big_axes, plan,
                                name="scatter_wait")
    sums = [_sum_parts(b, g, ax, where, name=f"sum_{n}") for (n, ax), b, g in zip(BIG, landed, own)]
    pairs = _pair_gather(sums, name="pair_gather")
    grad = {n: p.reshape(w[n].shape) for n, p in zip(big_names, pairs)}

    delta, new_m, new_v = {}, {}, {}
    for n, _ in BIG:
        delta[n], new_m[n], new_v[n] = _adamw(w[n], grad[n], m[n], v[n], name=f"adamw_{n}")
    small_own, small_slots = _small_wait(*small_sems, new_v[big_names[-1]], name="small_wait")
    small_sum = _sum_small(small_slots, small_own, where, name="sum_small")
    full_small = _unpack(small_sum, [gsmall[n].shape for n in small_names] + [(1,)])
    loss = full_small[-1][0]
    for n, gfull in zip(small_names, full_small):
        if n in SMALL_SHARDED:
            width = w[n].shape[-1]
            grad[n] = lax.dynamic_slice_in_dim(gfull, chip * width, width, axis=-1)
        else:
            grad[n] = gfull
    shapes = [w[n].shape for n in small_names]
    packed = _adamw(_pack([w[n] for n in small_names]), _pack([grad[n] for n in small_names]),
                    _pack([m[n] for n in small_names]), _pack([v[n] for n in small_names]), name="adamw_small")
    for out, res in zip((delta, new_m, new_v), packed):
        for n, a in zip(small_names, _unpack(res, shapes)):
            out[n] = a

    return (loss, grad_x[None], *[grad[n] for n in WEIGHTS], *[delta[n] for n in WEIGHTS],
            *[new_m[n] for n in WEIGHTS], *[new_v[n] for n in WEIGHTS])
```

```python
import functools
import math

import numpy as np
import jax
import jax.numpy as jnp
from jax import lax
from jax.experimental import pallas as pl
from jax.experimental.pallas import tpu as pltpu

F32 = jnp.float32
BF16 = jnp.bfloat16
MESH = pl.DeviceIdType.MESH

CHUNK = 128
HEAD_DIM = 64
DILATED_GROUPS = ((128, 1), (512, 4), (2048, 16))
N_GROUPS = len(DILATED_GROUPS)
BLK = 128
REL_BUCKETS = 32
REL_MAX_DIST = 2048
LN_EPS = 1e-5
NEG = -1e30
ADAM_LR = 0.001
ADAM_B1 = 0.9
ADAM_B2 = 0.999
ADAM_EPS = 1e-08
ADAM_WD = 0.01
ADAM_STEP = 10

N_CHIPS = 4
N_DEV = 8
LANES = 128
SUBLANES = 8
VMEM_LIMIT = 48 * 1024 * 1024

_SQRT_HALF = 0.7071067811865476
_INV_SQRT_2PI = 0.3989422804014327


def _cp(sem=None, vmem=VMEM_LIMIT):
    return pltpu.CompilerParams(dimension_semantics=sem, vmem_limit_bytes=vmem)


def _tile(dim, target, align=LANES):
    if dim <= target:
        return dim
    t = (target // align) * align
    while t >= align:
        if dim % t == 0:
            return t
        t -= align
    return dim


def _gelu(x):
    return 0.5 * x * (1.0 + lax.erf(x * _SQRT_HALF))


def _gelu_grad(x):
    return 0.5 * (1.0 + lax.erf(x * _SQRT_HALF)) + x * (_INV_SQRT_2PI * jnp.exp(-0.5 * x * x))


def _dot(a, b, ca, cb):
    return lax.dot_general(a, b, (((ca,), (cb,)), ((), ())), preferred_element_type=F32)


def _mn_tile(dim):
    return max(_tile(dim, 1024), _tile(dim, 1408))


def _mm(a, b, *, name, ta=False, tb=False, out_dtype=F32, tiles=None,
        a_lead=(), b_lead=(), a_halves=False, b_halves=False, add=None, add_scale=1.0, out_into=None, after=None):
    a2, b2 = a.shape[-2:], b.shape[-2:]
    M, K = (a2[1], a2[0]) if ta else a2
    Kb, N = (b2[1], b2[0]) if tb else b2
    if a_halves:
        K = 2 * K
    if b_halves:
        N = 2 * N
    assert K == Kb, (name, a.shape, b.shape)
    if tiles is None:
        k_target = 1408 if a_halves else (K if K <= 2816 else 2048)
        tm, tn = _mn_tile(M), _mn_tile(N // 2 if b_halves else N)
        tk = _tile(K // 2 if a_halves else K, k_target)
    else:
        tm = _tile(M, tiles[0])
        tn = _tile(N // 2 if b_halves else N, tiles[1])
        tk = _tile(K // 2 if a_halves else K, tiles[2])
    nm, nn, nk = M // tm, N // tn, K // tk
    nkh, nnh = nk // 2, nn // 2

    def lead(idx, rest):
        return tuple(idx) + tuple(rest)

    sq_a = (None,) * (a.ndim - 2)
    sq_b = (None,) * (b.ndim - 2)
    if a_halves:
        assert not ta and a.ndim == 3
        a_spec = pl.BlockSpec((None, tm, tk), lambda i, j, k: (k // nkh, i, k % nkh))
    elif ta:
        a_spec = pl.BlockSpec(sq_a + (tk, tm), lambda i, j, k: lead(a_lead, (k, i)))
    else:
        a_spec = pl.BlockSpec(sq_a + (tm, tk), lambda i, j, k: lead(a_lead, (i, k)))
    if b_halves:
        assert not tb and b.ndim == 3
        b_spec = pl.BlockSpec((None, tk, tn), lambda i, j, k: (j // nnh, k, j % nnh))
    elif tb:
        b_spec = pl.BlockSpec(sq_b + (tn, tk), lambda i, j, k: lead(b_lead, (j, k)))
    else:
        b_spec = pl.BlockSpec(sq_b + (tk, tn), lambda i, j, k: lead(b_lead, (k, j)))
    mn_spec = pl.BlockSpec((tm, tn), lambda i, j, k: (i, j))
    in_specs = [a_spec, b_spec]
    args = [a, b]
    if add is not None:
        in_specs.append(mn_spec)
        args.append(add)
    aliases = {}
    if out_into is None:
        o_spec, out_shape = mn_spec, jax.ShapeDtypeStruct((M, N), out_dtype)
    else:
        buf, n_layers, layer = out_into
        o_spec = pl.BlockSpec((None, tm, tn), lambda i, j, k: (layer, i, j))
        out_shape = jax.ShapeDtypeStruct((n_layers, M, N), out_dtype)
        if buf is not None:
            aliases = {len(args): 0}
            in_specs.append(pl.BlockSpec(memory_space=pl.ANY))
            args.append(buf)
    if after is not None:
        in_specs.append(pl.BlockSpec(memory_space=pl.ANY))
        args.append(after)
    n_in = len(args)
    ca, cb = (0 if ta else 1), (1 if tb else 0)

    def body(*refs):
        a_ref, b_ref = refs[:2]
        add_ref = refs[2] if add is not None else None
        o_ref = refs[n_in]

        def finish(acc):
            if add_ref is not None:
                acc = acc + add_scale * add_ref[...]
            o_ref[...] = acc.astype(o_ref.dtype)

        if nk == 1:
            finish(_dot(a_ref[...], b_ref[...], ca, cb))
        else:
            acc_ref = refs[-1]
            k = pl.program_id(2)

            @pl.when(k == 0)
            def _():
                acc_ref[...] = jnp.zeros_like(acc_ref)

            acc_ref[...] += _dot(a_ref[...], b_ref[...], ca, cb)

            @pl.when(k == nk - 1)
            def _():
                finish(acc_ref[...])

    return pl.pallas_call(
        body, name=name, grid=(nm, nn, nk),
        in_specs=in_specs, out_specs=o_spec, out_shape=out_shape,
        input_output_aliases=aliases,
        scratch_shapes=[] if nk == 1 else [pltpu.VMEM((tm, tn), F32)],
        compiler_params=_cp(("parallel", "parallel", "arbitrary")),
    )(*args)


def _ln_stats(r):
    mu = jnp.mean(r, axis=-1, keepdims=True)
    xc = r - mu
    var = jnp.mean(xc * xc, axis=-1, keepdims=True)
    rstd = lax.rsqrt(var + LN_EPS)
    return xc * rstd, rstd


def _add_ln(x, h, g, b, alpha, *, name):
    T, D = x.shape
    tr = _tile(T, 512, SUBLANES)
    row = pl.BlockSpec((tr, D), lambda i: (i, 0))
    vec = pl.BlockSpec((1, D), lambda i: (0, 0))

    def body(x_ref, h_ref, g_ref, b_ref, o_ref, ob_ref):
        xhat, _ = _ln_stats(alpha * x_ref[...] + h_ref[...])
        y = xhat * g_ref[...] + b_ref[...]
        o_ref[...] = y
        ob_ref[...] = y.astype(BF16)

    return pl.pallas_call(
        body, name=name, grid=(T // tr,),
        in_specs=[row, row, vec, vec], out_specs=[row, row],
        out_shape=[jax.ShapeDtypeStruct((T, D), F32), jax.ShapeDtypeStruct((T, D), BF16)],
        compiler_params=_cp(("parallel",)),
    )(x, h, g.reshape(1, D), b.reshape(1, D))


def _ln_bwd(dy, x, h, g, alpha, *, name):
    T, D = x.shape
    tr = _tile(T, 512, SUBLANES)
    row = pl.BlockSpec((tr, D), lambda i: (i, 0))
    vec = pl.BlockSpec((1, D), lambda i: (0, 0))

    def body(dy_ref, x_ref, h_ref, g_ref, dr_ref, drb_ref, dg_ref, db_ref):
        @pl.when(pl.program_id(0) == 0)
        def _():
            dg_ref[...] = jnp.zeros_like(dg_ref)
            db_ref[...] = jnp.zeros_like(db_ref)

        xhat, rstd = _ln_stats(alpha * x_ref[...] + h_ref[...])
        dyv = dy_ref[...]
        dxhat = dyv * g_ref[...]
        c1 = jnp.mean(dxhat, axis=-1, keepdims=True)
        c2 = jnp.mean(dxhat * xhat, axis=-1, keepdims=True)
        dr = rstd * (dxhat - c1 - xhat * c2)
        dr_ref[...] = dr
        drb_ref[...] = dr.astype(BF16)
        dg_ref[...] += jnp.sum(dyv * xhat, axis=0, keepdims=True)
        db_ref[...] += jnp.sum(dyv, axis=0, keepdims=True)

    return pl.pallas_call(
        body, name=name, grid=(T // tr,),
        in_specs=[row, row, row, vec], out_specs=[row, row, vec, vec],
        out_shape=[jax.ShapeDtypeStruct((T, D), F32), jax.ShapeDtypeStruct((T, D), BF16),
                   jax.ShapeDtypeStruct((1, D), F32), jax.ShapeDtypeStruct((1, D), F32)],
        compiler_params=_cp(("arbitrary",)),
    )(dy, x, h, g.reshape(1, D))


def _sgu_pieces(zp, lg, lb, E):
    z = _gelu(zp)
    u, v = z[:, :E], z[:, E:]
    vhat, rstd = _ln_stats(v)
    vn = vhat * lg + lb
    return u, vhat, rstd, vn


def _tril_mask():
    t = lax.broadcasted_iota(jnp.int32, (CHUNK, CHUNK), 0)
    s = lax.broadcasted_iota(jnp.int32, (CHUNK, CHUNK), 1)
    return s <= t


def _sgu_fwd(zp, ws, bs, lg, lb, *, name):
    T, E2 = zp.shape
    E = E2 // 2
    G = ws.shape[0]
    cw = E // G

    def body(zp_ref, ws_ref, bs_ref, lg_ref, lb_ref, y_ref):
        u, _, _, vn = _sgu_pieces(zp_ref[...].astype(F32), lg_ref[...], lb_ref[...], E)
        vnb = vn.astype(BF16)
        tril = _tril_mask()
        for g in range(G):
            cols = slice(g * cw, (g + 1) * cw)
            w = jnp.where(tril, ws_ref[g], 0.0).astype(BF16)
            sv = _dot(w, vnb[:, cols], 1, 0) + bs_ref[g]
            y_ref[:, cols] = (u[:, cols] * sv).astype(BF16)

    return pl.pallas_call(
        body, name=name, grid=(T // CHUNK,),
        in_specs=[pl.BlockSpec((CHUNK, E2), lambda n: (n, 0)),
                  pl.BlockSpec((G, CHUNK, CHUNK), lambda n: (0, 0, 0)),
                  pl.BlockSpec((G, CHUNK, 1), lambda n: (0, 0, 0)),
                  pl.BlockSpec((1, E), lambda n: (0, 0)),
                  pl.BlockSpec((1, E), lambda n: (0, 0))],
        out_specs=pl.BlockSpec((CHUNK, E), lambda n: (n, 0)),
        out_shape=jax.ShapeDtypeStruct((T, E), BF16),
        compiler_params=_cp(("parallel",)),
    )(zp, ws, bs.reshape(G, CHUNK, 1), lg.reshape(1, E), lb.reshape(1, E))


def _sgu_bwd(zp, dy, ws, bs, lg, lb, *, name):
    T, E2 = zp.shape
    E = E2 // 2
    G = ws.shape[0]
    cw = E // G
    assert G <= LANES

    def body(zp_ref, dy_ref, ws_ref, bs_ref, lg_ref, lb_ref,
             dzp_ref, dws_ref, dbs_ref, dlg_ref, dlb_ref, dvn_ref):
        @pl.when(pl.program_id(0) == 0)
        def _():
            dws_ref[...] = jnp.zeros_like(dws_ref)
            dbs_ref[...] = jnp.zeros_like(dbs_ref)
            dlg_ref[...] = jnp.zeros_like(dlg_ref)
            dlb_ref[...] = jnp.zeros_like(dlb_ref)

        zpv = zp_ref[...].astype(F32)
        lgv = lg_ref[...]
        u, vhat, rstd, vn = _sgu_pieces(zpv, lgv, lb_ref[...], E)
        gp = _gelu_grad(zpv)
        vnb = vn.astype(BF16)
        tril = _tril_mask()
        lane = lax.broadcasted_iota(jnp.int32, (CHUNK, LANES), 1)
        dbs = dbs_ref[...]
        for g in range(G):
            cols = slice(g * cw, (g + 1) * cw)
            w = jnp.where(tril, ws_ref[g], 0.0).astype(BF16)
            sv = _dot(w, vnb[:, cols], 1, 0) + bs_ref[g]
            dyg = dy_ref[:, cols].astype(F32)
            dzp_ref[:, cols] = (dyg * sv * gp[:, cols]).astype(BF16)
            dsv = dyg * u[:, cols]
            dsvb = dsv.astype(BF16)
            dws_ref[g] += jnp.where(tril, _dot(dsvb, vnb[:, cols], 1, 1), 0.0)
            dvn_ref[:, cols] = _dot(w, dsvb, 0, 0)
            dbs = dbs + jnp.where(lane == g, jnp.sum(dsv, axis=1, keepdims=True), 0.0)
        dbs_ref[...] = dbs
        dvn = dvn_ref[...]
        dlg_ref[...] += jnp.sum(dvn * vhat, axis=0, keepdims=True)
        dlb_ref[...] += jnp.sum(dvn, axis=0, keepdims=True)
        dvhat = dvn * lgv
        c1 = jnp.mean(dvhat, axis=-1, keepdims=True)
        c2 = jnp.mean(dvhat * vhat, axis=-1, keepdims=True)
        dv = rstd * (dvhat - c1 - vhat * c2)
        dzp_ref[:, E:] = (dv * gp[:, E:]).astype(BF16)

    vecE = pl.BlockSpec((1, E), lambda n: (0, 0))
    return pl.pallas_call(
        body, name=name, grid=(T // CHUNK,),
        in_specs=[pl.BlockSpec((CHUNK, E2), lambda n: (n, 0)),
                  pl.BlockSpec((CHUNK, E), lambda n: (n, 0)),
                  pl.BlockSpec((G, CHUNK, CHUNK), lambda n: (0, 0, 0)),
                  pl.BlockSpec((G, CHUNK, 1), lambda n: (0, 0, 0)),
                  vecE, vecE],
        out_specs=[pl.BlockSpec((CHUNK, E2), lambda n: (n, 0)),
                   pl.BlockSpec((G, CHUNK, CHUNK), lambda n: (0, 0, 0)),
                   pl.BlockSpec((CHUNK, LANES), lambda n: (0, 0)),
                   vecE, vecE],
        out_shape=[jax.ShapeDtypeStruct((T, E2), BF16),
                   jax.ShapeDtypeStruct((G, CHUNK, CHUNK), F32),
                   jax.ShapeDtypeStruct((CHUNK, LANES), F32),
                   jax.ShapeDtypeStruct((1, E), F32), jax.ShapeDtypeStruct((1, E), F32)],
        scratch_shapes=[pltpu.VMEM((CHUNK, E), F32)],
        compiler_params=_cp(("arbitrary",)),
    )(zp, dy, ws, bs.reshape(G, CHUNK, 1), lg.reshape(1, E), lb.reshape(1, E))


def _delay(h, k):
    t = lax.broadcasted_iota(jnp.int32, h.shape, 0)
    return jnp.where(t >= k, pltpu.roll(h, k, 0), 0.0)


def _advance(d, k):
    T = d.shape[0]
    t = lax.broadcasted_iota(jnp.int32, d.shape, 0)
    return jnp.where(t < T - k, pltpu.roll(d, T - k, 0), 0.0)


def _conv3(h, w_ref, b_ref):
    return w_ref[2:3, :] * h + w_ref[1:2, :] * _delay(h, 1) + w_ref[0:1, :] * _delay(h, 2) + b_ref[...]


def _convglu_fwd(hup, cw, cb, *, name):
    T, F2 = hup.shape
    F = F2 // 2
    tc = LANES
    nt = F // tc

    def body(ha_ref, hg_ref, wa_ref, wg_ref, ba_ref, bg_ref, o_ref, a_ref, g_ref):
        a = _conv3(ha_ref[...].astype(F32), wa_ref, ba_ref)
        g = _conv3(hg_ref[...].astype(F32), wg_ref, bg_ref)
        o_ref[...] = (_gelu(a) * g).astype(BF16)
        a_ref[...] = a.astype(BF16)
        g_ref[...] = g.astype(BF16)

    col = lambda off: pl.BlockSpec((T, tc), lambda j: (0, j + off))
    w3 = lambda off: pl.BlockSpec((3, tc), lambda j: (0, j + off))
    b1 = lambda off: pl.BlockSpec((1, tc), lambda j: (0, j + off))
    shp = jax.ShapeDtypeStruct((T, F), BF16)
    return pl.pallas_call(
        body, name=name, grid=(nt,),
        in_specs=[col(0), col(nt), w3(0), w3(nt), b1(0), b1(nt)],
        out_specs=[col(0)] * 3, out_shape=[shp] * 3,
        compiler_params=_cp(("parallel",)),
    )(hup, hup, cw, cw, cb.reshape(1, F2), cb.reshape(1, F2))


def _convglu_bwd(hup, a, g, dact, cw, *, name):
    T, F2 = hup.shape
    F = F2 // 2
    tc = LANES
    nt = F // tc

    def half(h, w_ref, d, dh_ref, dw_ref, db_ref, i):
        d1, d2 = _advance(d, 1), _advance(d, 2)
        dh_ref[i] = (w_ref[2:3, :] * d + w_ref[1:2, :] * d1 + w_ref[0:1, :] * d2).astype(BF16)
        c = _dot(h, jnp.concatenate([d2, d1, d], axis=1).astype(BF16), 0, 0)
        eye = lax.broadcasted_iota(jnp.int32, (tc, tc), 0) == lax.broadcasted_iota(jnp.int32, (tc, tc), 1)
        for k in range(3):
            dw_ref[i, k:k + 1, :] = jnp.sum(jnp.where(eye, c[:, k * tc:(k + 1) * tc], 0.0), axis=0, keepdims=True)
        db_ref[i] = jnp.sum(d, axis=0, keepdims=True)

    def body(ha_ref, hg_ref, a_ref, g_ref, d_ref, wa_ref, wg_ref, dh_ref, dw_ref, db_ref):
        a = a_ref[...].astype(F32)
        d = d_ref[...].astype(F32)
        half(ha_ref[...].astype(BF16), wa_ref, d * g_ref[...].astype(F32) * _gelu_grad(a), dh_ref, dw_ref, db_ref, 0)
        half(hg_ref[...].astype(BF16), wg_ref, d * _gelu(a), dh_ref, dw_ref, db_ref, 1)

    col = lambda off: pl.BlockSpec((T, tc), lambda j: (0, j + off))
    w3 = lambda off: pl.BlockSpec((3, tc), lambda j: (0, j + off))
    return pl.pallas_call(
        body, name=name, grid=(nt,),
        in_specs=[col(0), col(nt), col(0), col(0), col(0), w3(0), w3(nt)],
        out_specs=[pl.BlockSpec((2, T, tc), lambda j: (0, 0, j)),
                   pl.BlockSpec((2, 3, tc), lambda j: (0, 0, j)),
                   pl.BlockSpec((2, 1, tc), lambda j: (0, 0, j))],
        out_shape=[jax.ShapeDtypeStruct((2, T, F), BF16),
                   jax.ShapeDtypeStruct((2, 3, F), F32),
                   jax.ShapeDtypeStruct((2, 1, F), F32)],
        compiler_params=_cp(("parallel",)),
    )(hup, hup, a, g, dact, cw, cw)


def _bucket_table():
    iq = np.arange(BLK)[:, None]
    ik = np.arange(2 * BLK)[None, :]
    delta = iq + BLK - ik
    out = []
    for win, dil in DILATED_GROUPS:
        valid = (delta >= 0) & (delta <= win // dil)
        n = (np.clip(delta, 0, None) * dil).astype(np.int32)
        max_exact = REL_BUCKETS // 2
        nf = np.maximum(n, 1).astype(np.float32)
        large = max_exact + (np.log(nf / np.float32(max_exact)) / np.float32(math.log(REL_MAX_DIST / max_exact))
                             * np.float32(REL_BUCKETS - max_exact)).astype(np.int32)
        large = np.minimum(large, REL_BUCKETS - 1)
        out.append(np.where(valid, np.where(n < max_exact, n, large), -1))
    return np.stack(out).astype(np.int32)


def _band_bias(rel_table, H, *, name):
    bkt = jnp.asarray(_bucket_table())

    def body(tab_ref, bk_ref, o_ref):
        g = pl.program_id(0)
        bk = bk_ref[...]
        prev = lax.broadcasted_iota(jnp.int32, bk.shape, 1) < BLK
        for h in range(H):
            def bucket(c, acc):
                return jnp.where(bk == c, tab_ref[c, g * H + h], acc)

            b = lax.fori_loop(0, REL_BUCKETS, bucket, jnp.zeros(bk.shape, F32))
            b = jnp.where(bk >= 0, b, NEG)
            o_ref[h, 0] = b
            o_ref[h, 1] = jnp.where(prev, NEG, b)

    return pl.pallas_call(
        body, name=name, grid=(N_GROUPS,),
        in_specs=[pl.BlockSpec(memory_space=pltpu.SMEM),
                  pl.BlockSpec((None, BLK, 2 * BLK), lambda g: (g, 0, 0))],
        out_specs=pl.BlockSpec((None, H, 2, BLK, 2 * BLK), lambda g: (g, 0, 0, 0, 0)),
        out_shape=jax.ShapeDtypeStruct((N_GROUPS, H, 2, BLK, 2 * BLK), F32),
        compiler_params=_cp(("parallel",)),
    )(rel_table, bkt)


PAIR = 2 * HEAD_DIM


def _bdot(a, b, ca, cb):
    return lax.dot_general(a, b, (((ca,), (cb,)), ((0,), (0,))), preferred_element_type=F32)


def _blocks_per_step(nb, dil):
    return 2 if (nb // dil) % 2 == 0 else 1


def _blk(i):
    return slice(i * BLK, (i + 1) * BLK)


def _heads(ref, i, H, own_columns_only):
    low = lax.broadcasted_iota(jnp.int32, (BLK, PAIR), 1) < HEAD_DIM
    out = []
    for h in range(H):
        a = ref[_blk(i), (h // 2) * PAIR:(h // 2 + 1) * PAIR]
        if own_columns_only:
            a = jnp.where(low if h % 2 == 0 else jnp.logical_not(low), a, jnp.zeros_like(a))
        out.append(a)
    return jnp.stack(out)


def _pair_sums(o):
    return [o[2 * p] + o[2 * p + 1] for p in range(o.shape[0] // 2)]


def _store_pairs(ref, i, o, dtype, scale=None):
    for p, v in enumerate(_pair_sums(o)):
        ref[_blk(i), p * PAIR:(p + 1) * PAIR] = (v if scale is None else scale * v).astype(dtype)


def _stat_tile(s):
    lane = lax.broadcasted_iota(jnp.int32, (BLK, LANES), 1)
    t = jnp.zeros((BLK, LANES), F32)
    for h in range(s.shape[0]):
        t = jnp.where(lane == h, s[h], t)
    return t


def _stat_cols(ref, i, H):
    return jnp.stack([ref[_blk(i), h:h + 1] for h in range(H)])


def _scores(qm, kp, kc, bias):
    scale = HEAD_DIM ** -0.5
    sp = _bdot(qm, kp, 2, 2) * scale + bias[:, :, :BLK]
    sc = _bdot(qm, kc, 2, 2) * scale + bias[:, :, BLK:]
    return sp, sc


def _attn_specs(nb, dil, g, H, HD, sb):
    nbg = nb // dil
    QB = _blocks_per_step(nb, dil)
    cur = lambda col: pl.BlockSpec((QB * BLK, HD), lambda b: (sb(b), col))
    prev = lambda col: pl.BlockSpec((BLK, HD), lambda b: (jnp.maximum(QB * sb(b) - 1, 0), col))
    stat = pl.BlockSpec((QB * BLK, LANES), lambda b: (sb(b), 0))
    bias0 = pl.BlockSpec((None, H, None, BLK, 2 * BLK),
                         lambda b: (g, 0, jnp.where((QB * sb(b)) % nbg == 0, 1, 0), 0, 0))
    bias1 = pl.BlockSpec((None, H, None, BLK, 2 * BLK), lambda b: (g, 0, 0, 0, 0))
    return cur, prev, stat, bias0, bias1


def _attn_fwd(q, k, v, bias, g, dil, *, name):
    (qa, qc), (ka, kc_), (va, vc_) = q, k, v
    T = qa.shape[0]
    H = bias.shape[1]
    HD = H * HEAD_DIM
    assert H % 2 == 0 and H <= LANES and T % (dil * BLK) == 0
    nb = T // BLK
    QB = _blocks_per_step(nb, dil)
    cur, prev, stat, bias0, bias1 = _attn_specs(nb, dil, g, H, HD, lambda b: b)

    def body(q_ref, kp_ref, kc_ref, vp_ref, vc_ref, b0_ref, b1_ref, o_ref, l_ref):
        for i in range(QB):
            kpi, vpi = (kp_ref, vp_ref) if i == 0 else (kc_ref, vc_ref)
            sp, sc = _scores(_heads(q_ref, i, H, True), _heads(kpi, max(i - 1, 0), H, False),
                             _heads(kc_ref, i, H, False), (b0_ref if i == 0 else b1_ref)[...])
            m = jnp.maximum(jnp.max(sp, axis=-1, keepdims=True), jnp.max(sc, axis=-1, keepdims=True))
            pp, pc = jnp.exp(sp - m), jnp.exp(sc - m)
            den = jnp.sum(pp, axis=-1, keepdims=True) + jnp.sum(pc, axis=-1, keepdims=True)
            o = (_bdot((pp / den).astype(BF16), _heads(vpi, max(i - 1, 0), H, True), 2, 1)
                 + _bdot((pc / den).astype(BF16), _heads(vc_ref, i, H, True), 2, 1))
            _store_pairs(o_ref, i, o, F32)
            l_ref[_blk(i), :] = _stat_tile(m + jnp.log(den))

    return pl.pallas_call(
        body, name=name, grid=(nb // QB,),
        in_specs=[cur(qc), prev(kc_), cur(kc_), prev(vc_), cur(vc_), bias0, bias1],
        out_specs=[cur(0), stat],
        out_shape=[jax.ShapeDtypeStruct((T, HD), F32), jax.ShapeDtypeStruct((T, LANES), F32)],
        compiler_params=_cp(("parallel",)),
    )(qa, ka, ka, va, va, bias, bias)


def _attn_combine(os, ls, *, name):
    n = len(os)
    T, HD = os[0].shape
    H = HD // HEAD_DIM
    tr = _tile(T, 512, SUBLANES)
    wide = pl.BlockSpec((tr, HD), lambda i: (i, 0))
    stat = pl.BlockSpec((tr, LANES), lambda i: (i, 0))

    def body(*refs):
        o_refs, l_refs = refs[:n], refs[n:2 * n]
        oc_ref, ocb_ref, lt_ref = refs[2 * n:]
        ls_ = [r[...] for r in l_refs]
        m = ls_[0]
        for l in ls_[1:]:
            m = jnp.maximum(m, l)
        ws = [jnp.exp(l - m) for l in ls_]
        den = ws[0]
        for w in ws[1:]:
            den = den + w
        lt_ref[...] = m + jnp.log(den)
        ws = [w / den for w in ws]
        low = lax.broadcasted_iota(jnp.int32, (tr, PAIR), 1) < HEAD_DIM
        for p in range(H // 2):
            cols = slice(p * PAIR, (p + 1) * PAIR)
            acc = jnp.zeros((tr, PAIR), F32)
            for g in range(n):
                w = jnp.where(low, ws[g][:, 2 * p:2 * p + 1], ws[g][:, 2 * p + 1:2 * p + 2])
                acc = acc + w * o_refs[g][:, cols]
            oc_ref[:, cols] = acc
            ocb_ref[:, cols] = acc.astype(BF16)

    return pl.pallas_call(
        body, name=name, grid=(T // tr,),
        in_specs=[wide] * n + [stat] * n, out_specs=[wide, wide, stat],
        out_shape=[jax.ShapeDtypeStruct((T, HD), F32), jax.ShapeDtypeStruct((T, HD), BF16),
                   jax.ShapeDtypeStruct((T, LANES), F32)],
        compiler_params=_cp(("parallel",)),
    )(*os, *ls)


def _attn_bwd_prep(do, oc, *, name):
    T, HD = do.shape
    H = HD // HEAD_DIM
    tr = _tile(T, 512, SUBLANES)
    wide = pl.BlockSpec((tr, HD), lambda i: (i, 0))
    stat = pl.BlockSpec((tr, LANES), lambda i: (i, 0))

    def body(do_ref, oc_ref, dl_ref, dob_ref):
        low = lax.broadcasted_iota(jnp.int32, (tr, PAIR), 1) < HEAD_DIM
        lane = lax.broadcasted_iota(jnp.int32, (tr, LANES), 1)
        dl = jnp.zeros((tr, LANES), F32)
        for p in range(H // 2):
            cols = slice(p * PAIR, (p + 1) * PAIR)
            d = do_ref[:, cols]
            prod = d * oc_ref[:, cols]
            dl = jnp.where(lane == 2 * p, jnp.sum(jnp.where(low, prod, 0.0), axis=-1, keepdims=True), dl)
            dl = jnp.where(lane == 2 * p + 1, jnp.sum(jnp.where(low, 0.0, prod), axis=-1, keepdims=True), dl)
            dob_ref[:, cols] = d.astype(BF16)
        dl_ref[...] = dl

    return pl.pallas_call(
        body, name=name, grid=(T // tr,),
        in_specs=[wide, wide], out_specs=[stat, wide],
        out_shape=[jax.ShapeDtypeStruct((T, LANES), F32), jax.ShapeDtypeStruct((T, HD), BF16)],
        compiler_params=_cp(("parallel",)),
    )(do, oc)


def _attn_bwd(q, k, v, bias, g, dil, do, lse, delta, *, name):
    (qa, qc), (ka, kc_), (va, vc_) = q, k, v
    T, HD = do.shape
    H = HD // HEAD_DIM
    nb = T // BLK
    scale = HEAD_DIM ** -0.5
    QB = _blocks_per_step(nb, dil)
    ns = nb // QB
    cur, prev, stat, bias0, bias1 = _attn_specs(nb, dil, g, H, HD, lambda b: jnp.minimum(b, ns - 1))
    late = pl.BlockSpec((QB * BLK, HD), lambda b: (jnp.maximum(b - 1, 0), 0))

    def body(q_ref, kp_ref, kc_ref, vp_ref, vc_ref, b0_ref, b1_ref, do_ref, l_ref, dl_ref,
             dq_ref, dk_ref, dv_ref, db_ref, ck_ref, cv_ref):
        b = pl.program_id(0)

        @pl.when(b == 0)
        def _():
            db_ref[...] = jnp.zeros_like(db_ref)
            ck_ref[...] = jnp.zeros_like(ck_ref)
            cv_ref[...] = jnp.zeros_like(cv_ref)

        @pl.when(b < ns)
        def _():
            dk_ref[...] = ck_ref[...]
            dv_ref[...] = cv_ref[...]
            for i in range(QB):
                kpi, vpi, ip = (kp_ref, vp_ref, 0) if i == 0 else (kc_ref, vc_ref, i - 1)
                qm, dom = _heads(q_ref, i, H, True), _heads(do_ref, i, H, True)
                sp, sc = _scores(qm, _heads(kpi, ip, H, False), _heads(kc_ref, i, H, False),
                                 (b0_ref if i == 0 else b1_ref)[...])
                l, dl = _stat_cols(l_ref, i, H), _stat_cols(dl_ref, i, H)
                pp, pc = jnp.exp(sp - l), jnp.exp(sc - l)
                dsp = pp * (_bdot(dom, _heads(vpi, ip, H, False), 2, 2) - dl)
                dsc = pc * (_bdot(dom, _heads(vc_ref, i, H, False), 2, 2) - dl)
                db_ref[:, :, :BLK] += dsp
                db_ref[:, :, BLK:] += dsc
                dspb, dscb = dsp.astype(BF16), dsc.astype(BF16)
                _store_pairs(dq_ref, i, _bdot(dspb, _heads(kpi, ip, H, True), 2, 1)
                             + _bdot(dscb, _heads(kc_ref, i, H, True), 2, 1), BF16, scale)
                dkp = _pair_sums(_bdot(dspb, qm, 1, 1))
                dkc = _pair_sums(_bdot(dscb, qm, 1, 1))
                dvp = _pair_sums(_bdot(pp.astype(BF16), dom, 1, 1))
                dvc = _pair_sums(_bdot(pc.astype(BF16), dom, 1, 1))
                for p in range(H // 2):
                    cols = slice(p * PAIR, (p + 1) * PAIR)
                    if i == 0:
                        dk_ref[_blk(QB - 1), cols] += scale * dkp[p]
                        dv_ref[_blk(QB - 1), cols] += dvp[p]
                    else:
                        ck_ref[_blk(i - 1), cols] += scale * dkp[p]
                        cv_ref[_blk(i - 1), cols] += dvp[p]
                    ck_ref[_blk(i), cols] = scale * dkc[p]
                    cv_ref[_blk(i), cols] = dvc[p]

        @pl.when(b == ns)
        def _():
            dk_ref[...] = ck_ref[...]
            dv_ref[...] = cv_ref[...]

    f32 = jax.ShapeDtypeStruct((T, HD), F32)
    return pl.pallas_call(
        body, name=name, grid=(ns + 1,),
        in_specs=[cur(qc), prev(kc_), cur(kc_), prev(vc_), cur(vc_), bias0, bias1, cur(0), stat, stat],
        out_specs=[cur(0), late, late, pl.BlockSpec((H, BLK, 2 * BLK), lambda b: (0, 0, 0))],
        out_shape=[jax.ShapeDtypeStruct((T, HD), BF16), f32, f32,
                   jax.ShapeDtypeStruct((H, BLK, 2 * BLK), F32)],
        scratch_shapes=[pltpu.VMEM((QB * BLK, HD), F32), pltpu.VMEM((QB * BLK, HD), F32)],
        compiler_params=_cp(("arbitrary",)),
    )(qa, ka, ka, va, va, bias, bias, do, lse, delta)


def _rel_grad(dbs, *, name):
    H = dbs[0].shape[1]
    n = len(dbs)
    assert H <= LANES
    bkt = jnp.asarray(_bucket_table())

    def body(*refs):
        bk = refs[n][...]
        o_ref = refs[n + 1]
        db = refs[0][...]
        for r in refs[1:n]:
            db = db + r[...]
        row = lax.broadcasted_iota(jnp.int32, (REL_BUCKETS, LANES), 0)
        col = lax.broadcasted_iota(jnp.int32, (REL_BUCKETS, LANES), 1)

        def bucket(c, acc):
            sel = bk == c
            for h in range(H):
                val = jnp.sum(jnp.where(sel, db[h], 0.0))
                acc = jnp.where((row == c) & (col == h), val, acc)
            return acc

        o_ref[...] = lax.fori_loop(0, REL_BUCKETS, bucket, jnp.zeros((REL_BUCKETS, LANES), F32))

    dspec = pl.BlockSpec((None, H, BLK, 2 * BLK), lambda g: (g, 0, 0, 0))
    return pl.pallas_call(
        body, name=name, grid=(N_GROUPS,),
        in_specs=[dspec] * n + [pl.BlockSpec((None, BLK, 2 * BLK), lambda g: (g, 0, 0))],
        out_specs=pl.BlockSpec((None, REL_BUCKETS, LANES), lambda g: (g, 0, 0)),
        out_shape=jax.ShapeDtypeStruct((N_GROUPS, REL_BUCKETS, LANES), F32),
        compiler_params=_cp(("parallel",)),
    )(*dbs, bkt)


def _to_group(a, d):
    T, C = a.shape
    return a if d == 1 else a.reshape(T // d, d, C).transpose(1, 0, 2).reshape(T, C)


def _from_group(a, d):
    T, C = a.shape
    return a if d == 1 else a.reshape(d, T // d, C).transpose(1, 0, 2).reshape(T, C)


def _rows_view(a):
    a2 = a.reshape(-1, a.shape[-1])
    R, C = a2.shape
    tr = _tile(R, max(SUBLANES, (1 << 18) // C), SUBLANES)
    return a2, R, C, tr


def _addn(xs, out_dtype, *, name):
    shape = xs[0].shape
    x2s = [_rows_view(x)[0] for x in xs]
    _, R, C, tr = _rows_view(xs[0])
    spec = pl.BlockSpec((tr, C), lambda i: (i, 0))

    def body(*refs):
        acc = refs[0][...].astype(F32)
        for r in refs[1:-1]:
            acc = acc + r[...].astype(F32)
        refs[-1][...] = acc.astype(out_dtype)

    out = pl.pallas_call(
        body, name=name, grid=(R // tr,),
        in_specs=[spec] * len(xs), out_specs=spec,
        out_shape=jax.ShapeDtypeStruct((R, C), out_dtype),
        compiler_params=_cp(("parallel",)),
    )(*x2s)
    return out.reshape(shape)


def _slot_sum(b_ref, own, my_id):
    acc = None
    for s in range(b_ref.shape[0]):
        term = jnp.where(my_id == s, own, b_ref[s].astype(F32))
        acc = term if acc is None else acc + term
    return acc


def _part_geometry(shape, axis):
    _, R, C = shape
    if axis == COL:
        return R // 2, C // N_CHIPS
    return R // (2 * N_CHIPS), C


def _sum_parts(buf, grad, axis, where, *, name):
    n, L, Rp, Cp = buf.shape
    assert (Rp, Cp) == _part_geometry(grad.shape, axis)
    tr = _tile(Rp, max(2 * SUBLANES, (1 << 17) // Cp), 2 * SUBLANES)
    nr = Rp // tr

    def own_map(l, i, w):
        if axis == COL:
            return (l, w[1] * nr + i, w[0])
        return (l, (2 * w[0] + w[1]) * nr + i, 0)

    def body(w_ref, b_ref, g_ref, o_ref):
        o_ref[...] = _slot_sum(b_ref, g_ref[...].astype(F32), w_ref[2])

    return pl.pallas_call(
        body, name=name,
        grid_spec=pltpu.PrefetchScalarGridSpec(
            num_scalar_prefetch=1, grid=(L, nr),
            in_specs=[pl.BlockSpec((n, None, tr, Cp), lambda l, i, w: (0, l, i, 0)),
                      pl.BlockSpec((None, tr, Cp), own_map)],
            out_specs=pl.BlockSpec((None, None, tr, Cp), lambda l, i, w: (l, w[1], i, 0))),
        out_shape=jax.ShapeDtypeStruct((L, 2, Rp, Cp), F32),
        compiler_params=_cp(("parallel", "parallel")),
    )(where, buf, grad)


def _sum_small(buf, own, where, *, name):
    n, R, C = buf.shape
    tr = _tile(R, 512, SUBLANES)

    def body(w_ref, b_ref, g_ref, o_ref):
        o_ref[...] = _slot_sum(b_ref, g_ref[...], w_ref[2])

    return pl.pallas_call(
        body, name=name,
        grid_spec=pltpu.PrefetchScalarGridSpec(
            num_scalar_prefetch=1, grid=(R // tr,),
            in_specs=[pl.BlockSpec((n, tr, C), lambda i, w: (0, i, 0)),
                      pl.BlockSpec((tr, C), lambda i, w: (i, 0))],
            out_specs=pl.BlockSpec((tr, C), lambda i, w: (i, 0))),
        out_shape=jax.ShapeDtypeStruct((R, C), F32),
        compiler_params=_cp(("parallel",)),
    )(where, buf, own)


def _place_shard(shard, layer, axis, where, out_dtype, *, name, after=None):
    _, R, C = shard.shape
    tr = _tile(R, max(2 * SUBLANES, (1 << 18) // C), 2 * SUBLANES)
    nr = R // tr
    if axis == COL:
        full, out_map = (1, R, C * N_CHIPS), (lambda i, w: (0, i, w[0]))
    else:
        full, out_map = (1, R * N_CHIPS, C), (lambda i, w: (0, w[0] * nr + i, 0))

    def body(w_ref, x_ref, *rest):
        rest[-1][...] = x_ref[...].astype(out_dtype)

    extra = [] if after is None else [after]
    return pl.pallas_call(
        body, name=name,
        grid_spec=pltpu.PrefetchScalarGridSpec(
            num_scalar_prefetch=1, grid=(nr,),
            in_specs=[pl.BlockSpec((None, tr, C), lambda i, w: (layer, i, 0))]
            + [pl.BlockSpec(memory_space=pl.ANY)] * len(extra),
            out_specs=pl.BlockSpec((None, tr, C), out_map)),
        out_shape=jax.ShapeDtypeStruct(full, out_dtype),
        compiler_params=_cp(("parallel",)),
    )(where, shard, *extra)


def _loss_grad(y, tgt, *, name):
    T, D = y.shape
    tr = _tile(T, 512, SUBLANES)
    row = pl.BlockSpec((tr, D), lambda i: (i, 0))

    def body(y_ref, t_ref, dy_ref, l_ref, acc_ref):
        i = pl.program_id(0)

        @pl.when(i == 0)
        def _():
            acc_ref[...] = jnp.zeros_like(acc_ref)

        err = y_ref[...] - t_ref[...]
        dy_ref[...] = err * (1.0 / D)
        acc_ref[...] += jnp.sum(err * err, axis=0, keepdims=True)

        @pl.when(i == T // tr - 1)
        def _():
            tot = jnp.sum(acc_ref[...], axis=1, keepdims=True) * (0.5 / D)
            l_ref[...] = jnp.broadcast_to(tot, (1, LANES))

    return pl.pallas_call(
        body, name=name, grid=(T // tr,),
        in_specs=[row, row],
        out_specs=[row, pl.BlockSpec((1, LANES), lambda i: (0, 0))],
        out_shape=[jax.ShapeDtypeStruct((T, D), F32), jax.ShapeDtypeStruct((1, LANES), F32)],
        scratch_shapes=[pltpu.VMEM((1, D), F32)],
        compiler_params=_cp(("arbitrary",)),
    )(y, tgt)


def _adamw(w, g, m, v, *, name):
    shape = w.shape
    w2, R, C, tr = _rows_view(w)
    spec = pl.BlockSpec((tr, C), lambda i: (i, 0))

    def body(w_ref, g_ref, m_ref, v_ref, d_ref, nm_ref, nv_ref):
        gv = g_ref[...]
        nm = ADAM_B1 * m_ref[...] + (1.0 - ADAM_B1) * gv
        nv = ADAM_B2 * v_ref[...] + (1.0 - ADAM_B2) * (gv * gv)
        m_hat = nm / (1.0 - ADAM_B1 ** ADAM_STEP)
        v_hat = nv / (1.0 - ADAM_B2 ** ADAM_STEP)
        d_ref[...] = -ADAM_LR * (m_hat / (jnp.sqrt(v_hat) + ADAM_EPS) + ADAM_WD * w_ref[...])
        nm_ref[...] = nm
        nv_ref[...] = nv

    shp = jax.ShapeDtypeStruct((R, C), F32)
    outs = pl.pallas_call(
        body, name=name, grid=(R // tr,),
        in_specs=[spec] * 4, out_specs=[spec] * 3, out_shape=[shp] * 3,
        compiler_params=_cp(("parallel",)),
    )(w2, g.reshape(R, C), m.reshape(R, C), v.reshape(R, C))
    return tuple(o.reshape(shape) for o in outs)


def _pack(arrs):
    flat = jnp.concatenate([a.reshape(-1).astype(F32) for a in arrs])
    unit = SUBLANES * LANES
    pad = (-flat.shape[0]) % unit
    return jnp.pad(flat, (0, pad)).reshape(-1, LANES)


def _unpack(buf, shapes):
    flat = buf.reshape(-1)
    out, off = [], 0
    for s in shapes:
        n = int(np.prod(s))
        out.append(flat[off:off + n].reshape(s))
        off += n
    return out


def _me():
    return lax.axis_index("x"), lax.axis_index("y"), lax.axis_index("c")


def _flip(pos, k):
    x, y, c = pos
    return (1 - x if k & 4 else x, 1 - y if k & 2 else y, 1 - c if k & 1 else c)


HBM = pl.BlockSpec(memory_space=pltpu.HBM)

COL, ROW = -1, -2


def _shard_window(ref, axis, j, n):
    start = pl.multiple_of(j * n, n)
    if axis == COL:
        return ref.at[:, :, pl.ds(start, n)]
    return ref.at[:, pl.ds(start, n), :]


GATHER_PEERS = (2, 4, 6)
SEM = pl.BlockSpec(memory_space=pltpu.SEMAPHORE)
EFFECT = pltpu.SideEffectType.DATAFLOW_SIDE_EFFECTING


def _gather_copy(buf, axis, i, pi, chip, me, send_sems, recv_sems):
    win = _shard_window(buf, axis, chip, buf.shape[axis] // N_CHIPS)
    k = len(GATHER_PEERS) * i + pi
    return pltpu.make_async_remote_copy(
        src_ref=win, dst_ref=win, send_sem=send_sems.at[k], recv_sem=recv_sems.at[k],
        device_id=_flip(me, GATHER_PEERS[pi]), device_id_type=MESH)


def _gather_start(fulls, axes, *, name):
    n = len(fulls)
    n_sem = len(GATHER_PEERS) * n

    def body(*refs):
        send_sems, recv_sems = refs[n], refs[n + 1]
        bufs = refs[n + 2:2 * n + 2]
        token = refs[2 * n + 2]
        me = _me()
        for i in range(n):
            for pi in range(len(GATHER_PEERS)):
                _gather_copy(bufs[i], axes[i], i, pi, 2 * me[0] + me[1], me, send_sems, recv_sems).start()
        token[...] = jnp.zeros_like(token)

    outs = pl.pallas_call(
        body, name=name,
        in_specs=[HBM] * n, out_specs=[SEM, SEM] + [HBM] * n + [pl.BlockSpec(memory_space=pltpu.VMEM)],
        out_shape=[pltpu.SemaphoreType.DMA((n_sem,)), pltpu.SemaphoreType.DMA((n_sem,))]
        + [pltpu.HBM(f.shape, f.dtype) for f in fulls] + [jax.ShapeDtypeStruct((SUBLANES, LANES), F32)],
        input_output_aliases={i: 2 + i for i in range(n)},
        compiler_params=pltpu.CompilerParams(has_side_effects=EFFECT),
    )(*[pltpu.with_memory_space_constraint(f, pltpu.HBM) for f in fulls])
    return outs[0], outs[1], list(outs[2:2 + n]), outs[-1]


def _gather_wait(send_sems, recv_sems, bufs, axes, idxs, after, *, name):
    m = len(bufs)

    def body(*refs):
        ss, rs = refs[m], refs[m + 1]
        outs = refs[m + 3:]
        me = _me()
        for t, i in enumerate(idxs):
            for pi, k in enumerate(GATHER_PEERS):
                px, py, _ = _flip(me, k)
                _gather_copy(outs[t], axes[t], i, pi, 2 * me[0] + me[1], me, ss, rs).wait_send()
                _gather_copy(outs[t], axes[t], i, pi, 2 * px + py, me, ss, rs).wait_recv()

    return pl.pallas_call(
        body, name=name,
        in_specs=[HBM] * m + [SEM, SEM, pl.BlockSpec(memory_space=pl.ANY)], out_specs=[HBM] * m,
        out_shape=[pltpu.HBM(b.shape, b.dtype) for b in bufs],
        input_output_aliases={t: t for t in range(m)},
        compiler_params=pltpu.CompilerParams(has_side_effects=EFFECT),
    )(*bufs, send_sems, recv_sems, after)


def _grad_part(ref, layer, axis, pos):
    px, py, pc = pos
    Rp, Cp = _part_geometry(ref.shape, axis)
    chip = 2 * px + py
    if axis == COL:
        return ref.at[layer, pl.ds(pl.multiple_of(pc * Rp, Rp), Rp), pl.ds(pl.multiple_of(chip * Cp, Cp), Cp)]
    return ref.at[layer, pl.ds(pl.multiple_of((2 * chip + pc) * Rp, Rp), Rp), :]


def _scatter_copy(grad, slots, layer, axis, i, k, src_pos, dst_slot, me, send_sems, recv_sems):
    sem = (N_DEV - 1) * i + k - 1
    return pltpu.make_async_remote_copy(
        src_ref=_grad_part(grad, layer, axis, src_pos), dst_ref=slots.at[dst_slot, layer],
        send_sem=send_sems.at[sem], recv_sem=recv_sems.at[sem],
        device_id=_flip(me, k), device_id_type=MESH)


def _scatter_start(grads, slots, layers, axes, *, name):
    n = len(grads)
    n_sem = (N_DEV - 1) * n

    def body(*refs):
        send_sems, recv_sems = refs[2 * n], refs[2 * n + 1]
        g_refs, s_refs = refs[2 * n + 2:3 * n + 2], refs[3 * n + 2:4 * n + 2]
        token = refs[4 * n + 2]
        me = _me()
        my_id = 4 * me[0] + 2 * me[1] + me[2]
        for i in range(n):
            for k in range(1, N_DEV):
                _scatter_copy(g_refs[i], s_refs[i], layers[i], axes[i], i, k, _flip(me, k), my_id, me,
                              send_sems, recv_sems).start()
        token[...] = jnp.zeros_like(token)

    arrays = list(grads) + list(slots)
    outs = pl.pallas_call(
        body, name=name,
        in_specs=[HBM] * (2 * n),
        out_specs=[SEM, SEM] + [HBM] * (2 * n) + [pl.BlockSpec(memory_space=pltpu.VMEM)],
        out_shape=[pltpu.SemaphoreType.DMA((n_sem,)), pltpu.SemaphoreType.DMA((n_sem,))]
        + [pltpu.HBM(a.shape, a.dtype) for a in arrays] + [jax.ShapeDtypeStruct((SUBLANES, LANES), F32)],
        input_output_aliases={i: 2 + i for i in range(2 * n)},
        compiler_params=pltpu.CompilerParams(has_side_effects=EFFECT),
    )(*[pltpu.with_memory_space_constraint(a, pltpu.HBM) for a in arrays])
    return outs[0], outs[1], list(outs[2:2 + n]), list(outs[2 + n:2 + 2 * n]), outs[-1]


def _scatter_wait(sems, grads, slots, axes, plan, after, *, name):
    n = len(grads)
    flat_sems = [s for pair in sems for s in pair]

    def body(*refs):
        sem_refs = refs[2 * n:2 * n + len(flat_sems)]
        first_out = 2 * n + len(flat_sems) + 1
        g_refs, s_refs = refs[first_out:first_out + n], refs[first_out + n:]
        me = _me()
        for j, items in enumerate(plan):
            ss, rs = sem_refs[2 * j], sem_refs[2 * j + 1]
            for i, (a, layer) in enumerate(items):
                for k in range(1, N_DEV):
                    px, py, pc = _flip(me, k)
                    cp = _scatter_copy(g_refs[a], s_refs[a], layer, axes[a], i, k, me, 4 * px + 2 * py + pc, me, ss, rs)
                    cp.wait_send()
                    cp.wait_recv()

    arrays = list(grads) + list(slots)
    outs = pl.pallas_call(
        body, name=name,
        in_specs=[HBM] * (2 * n) + [SEM] * len(flat_sems) + [pl.BlockSpec(memory_space=pl.ANY)],
        out_specs=[HBM] * (2 * n),
        out_shape=[pltpu.HBM(a.shape, a.dtype) for a in arrays],
        input_output_aliases={i: i for i in range(2 * n)},
        compiler_params=pltpu.CompilerParams(has_side_effects=EFFECT),
    )(*arrays, *flat_sems, after)
    return list(outs[:n]), list(outs[n:])


def _small_copy(src, dst, k, slot, me, send_sems, recv_sems):
    return pltpu.make_async_remote_copy(
        src_ref=src, dst_ref=dst.at[slot], send_sem=send_sems.at[k - 1], recv_sem=recv_sems.at[k - 1],
        device_id=_flip(me, k), device_id_type=MESH)


def _small_start(small, slots, after, *, name):
    def body(src_in, dst_in, after_ref, send_sems, recv_sems, src, dst):
        me = _me()
        for k in range(1, N_DEV):
            _small_copy(src, dst, k, 4 * me[0] + 2 * me[1] + me[2], me, send_sems, recv_sems).start()

    arrays = [small, slots]
    return pl.pallas_call(
        body, name=name, in_specs=[HBM, HBM, pl.BlockSpec(memory_space=pl.ANY)], out_specs=[SEM, SEM, HBM, HBM],
        out_shape=[pltpu.SemaphoreType.DMA((N_DEV - 1,)), pltpu.SemaphoreType.DMA((N_DEV - 1,))]
        + [pltpu.HBM(a.shape, a.dtype) for a in arrays],
        input_output_aliases={0: 2, 1: 3},
        compiler_params=pltpu.CompilerParams(has_side_effects=EFFECT),
    )(*[pltpu.with_memory_space_constraint(a, pltpu.HBM) for a in arrays], after)


def _small_wait(send_sems, recv_sems, small, slots, after, *, name):
    def body(src_in, dst_in, ss, rs, after_ref, src, dst):
        me = _me()
        for k in range(1, N_DEV):
            px, py, pc = _flip(me, k)
            cp = _small_copy(src, dst, k, 4 * px + 2 * py + pc, me, ss, rs)
            cp.wait_send()
            cp.wait_recv()

    return pl.pallas_call(
        body, name=name, in_specs=[HBM, HBM, SEM, SEM, pl.BlockSpec(memory_space=pl.ANY)], out_specs=[HBM, HBM],
        out_shape=[pltpu.HBM(small.shape, small.dtype), pltpu.HBM(slots.shape, slots.dtype)],
        input_output_aliases={0: 0, 1: 1},
        compiler_params=pltpu.CompilerParams(has_side_effects=EFFECT),
    )(small, slots, send_sems, recv_sems, after)


def _pair_gather(halves, *, name):
    n = len(halves)

    def body(*refs):
        bufs = refs[n:2 * n]
        send_sems, recv_sems = refs[2 * n:]
        me = _me()
        c = me[2]
        sib = _flip(me, 1)

        def copy(i, half):
            return pltpu.make_async_remote_copy(
                src_ref=bufs[i].at[:, half], dst_ref=bufs[i].at[:, half], send_sem=send_sems.at[i],
                recv_sem=recv_sems.at[i], device_id=sib, device_id_type=MESH)

        sends = [copy(i, c) for i in range(n)]
        for cp in sends:
            cp.start()
        for i in range(n):
            copy(i, 1 - c).wait_recv()
        for cp in sends:
            cp.wait_send()

    return pl.pallas_call(
        body, name=name,
        in_specs=[HBM] * n, out_specs=[HBM] * n,
        out_shape=[jax.ShapeDtypeStruct(h.shape, h.dtype) for h in halves],
        input_output_aliases={i: i for i in range(n)},
        scratch_shapes=[pltpu.SemaphoreType.DMA((n,)), pltpu.SemaphoreType.DMA((n,))],
    )(*halves)


def _local_step(x, tgt, P, fetch, ship, n_layers):
    T, D = x.shape
    P = dict(P)
    W = {}
    depth = P["ffn_conv_b"].shape[0]
    n_a = P["a_w_s"].shape[0]
    alpha = (2 * depth) ** 0.25
    H = P["rel_table"].shape[1] // N_GROUPS
    HD = H * HEAD_DIM
    dils = [d for _, d in DILATED_GROUPS]
    assert T % (dils[-1] * BLK) == 0

    def arrive(stage, after):
        w_new, p_new = fetch(stage, after)
        W.update(w_new)
        P.update(p_new)

    bias = _band_bias(P["rel_table"], H, name="band_bias")
    saved = []
    xf, xb = x, x.astype(BF16)
    kg = vg = None
    for i in range(depth):
        s = {"x": xf, "xb": xb}
        arrive(f"{i}a", xf)
        if i < n_a:
            s["zp"] = _mm(xb, W["a_w_in", i], out_dtype=BF16, name=f"a{i}_in")
            arrive(f"{i}b", s["zp"])
            s["y"] = _sgu_fwd(s["zp"], P["a_w_s"][i], P["a_b_s"][i], P["a_ln_g"][i], P["a_ln_b"][i], name=f"a{i}_sgu")
            s["h"] = _mm(s["y"], W["a_w_out", i], name=f"a{i}_out")
        else:
            j = i - n_a
            if j == 0:
                kv = _mm(xb, W["kv_w", 0], out_dtype=BF16, name="kv_proj")
                kg = [(kv, 0) if d == 1 else (_to_group(kv[:, :HD], d), 0) for d in dils]
                vg = [(kv, 1) if d == 1 else (_to_group(kv[:, HD:], d), 0) for d in dils]
            q = _mm(xb, W["b_w_q", j], out_dtype=BF16, name=f"b{j}_q")
            s["qg"] = [(q, g) if d == 1 else (_to_group(q[:, g * HD:(g + 1) * HD], d), 0) for g, d in enumerate(dils)]
            os, ls = [], []
            for g, d in enumerate(dils):
                o_g, l_g = _attn_fwd(s["qg"][g], kg[g], vg[g], bias, g, d, name=f"b{j}_attn{g}")
                os.append(_from_group(o_g, d))
                ls.append(_from_group(l_g, d))
            s["oc"], s["ob"], s["lse"] = _attn_combine(os, ls, name=f"b{j}_comb")
            arrive(f"{i}b", s["ob"])
            s["h"] = _mm(s["ob"], W["b_w_o", j], name=f"b{j}_o")
        s["x1"], s["x1b"] = _add_ln(xf, s["h"], P["ln_g"][i, 0], P["ln_b"][i, 0], alpha, name=f"l{i}_ln1")
        arrive(f"{i}c", s["x1b"])
        s["hup"] = _mm(s["x1b"], W["ffn_w_up", i], out_dtype=BF16, name=f"l{i}_up")
        arrive(f"{i}d", s["hup"])
        s["act"], s["ca"], s["cg"] = _convglu_fwd(s["hup"], P["ffn_conv_w"][i], P["ffn_conv_b"][i], name=f"l{i}_glu")
        s["f"] = _mm(s["act"], W["ffn_w_down", i], name=f"l{i}_down")
        xf, xb = _add_ln(s["x1"], s["f"], P["ln_g"][i, 1], P["ln_b"][i, 1], alpha, name=f"l{i}_ln2")
        saved.append(s)

    G, loss_row = _loss_grad(xf, tgt, name="loss")

    gw = {}

    def dw(key, layer, a, b, name, **kw):
        gw[key] = _mm(a, b, ta=True, out_dtype=BF16, out_into=(gw.get(key), n_layers[key], layer), name=name, **kw)

    gp = {k: [None] * n_a for k in ("a_ln_g", "a_ln_b", "a_w_s", "a_b_s")}
    gp.update({k: [None] * depth for k in ("ffn_conv_w", "ffn_conv_b", "ln_g", "ln_b")})
    dk_parts, dv_parts, dbias_parts = [], [], []
    for i in reversed(range(depth)):
        s = saved[i]
        dr2, dr2b, dg2, db2 = _ln_bwd(G, s["x1"], s["f"], P["ln_g"][i, 1], alpha, name=f"l{i}_ln2_bwd")
        dw("ffn_w_down", i, s["act"], dr2b, f"l{i}_down_dw")
        dact = _mm(dr2b, W["ffn_w_down", i], tb=True, out_dtype=BF16, name=f"l{i}_down_dx")
        dhup, dcw, dcb = _convglu_bwd(s["hup"], s["ca"], s["cg"], dact, P["ffn_conv_w"][i], name=f"l{i}_glu_bwd")
        gp["ffn_conv_w"][i] = dcw.transpose(1, 0, 2).reshape(dcw.shape[1], -1)
        gp["ffn_conv_b"][i] = dcb.reshape(-1)
        dw("ffn_w_up", i, s["x1b"], dhup, f"l{i}_up_dw", b_halves=True)
        token = ship(gw, [("ffn_w_down", i), ("ffn_w_up", i)])
        G1 = _mm(dhup, W["ffn_w_up", i], tb=True, a_halves=True, add=dr2, add_scale=alpha, after=token,
                 name=f"l{i}_up_dx")
        dr1, dr1b, dg1, db1 = _ln_bwd(G1, s["x"], s["h"], P["ln_g"][i, 0], alpha, name=f"l{i}_ln1_bwd")
        gp["ln_g"][i] = jnp.concatenate([dg1, dg2], axis=0)
        gp["ln_b"][i] = jnp.concatenate([db1, db2], axis=0)
        if i < n_a:
            dw("a_w_out", i, s["y"], dr1b, f"a{i}_out_dw")
            token = ship(gw, [("a_w_out", i)])
            dy = _mm(dr1b, W["a_w_out", i], tb=True, out_dtype=BF16, after=token, name=f"a{i}_out_dx")
            dzp, dws, dbs, dlg, dlb = _sgu_bwd(s["zp"], dy, P["a_w_s"][i], P["a_b_s"][i], P["a_ln_g"][i],
                                               P["a_ln_b"][i], name=f"a{i}_sgu_bwd")
            gp["a_w_s"][i], gp["a_b_s"][i] = dws, dbs[:, :dws.shape[0]].T
            gp["a_ln_g"][i], gp["a_ln_b"][i] = dlg[0], dlb[0]
            dw("a_w_in", i, s["xb"], dzp, f"a{i}_in_dw")
            token = ship(gw, [("a_w_in", i)])
            G = _mm(dzp, W["a_w_in", i], tb=True, add=dr1, add_scale=alpha, after=token, name=f"a{i}_in_dx")
        else:
            j = i - n_a
            dw("b_w_o", j, s["ob"], dr1b, f"b{j}_o_dw")
            do_tok = _mm(dr1b, W["b_w_o", j], tb=True, name=f"b{j}_o_dx")
            delta, dob = _attn_bwd_prep(do_tok, s["oc"], name=f"b{j}_prep")
            dqs, dbs = [], []
            for g, d in enumerate(dils):
                dq, dk, dv, db = _attn_bwd(s["qg"][g], kg[g], vg[g], bias, g, d, _to_group(dob, d),
                                           _to_group(s["lse"], d), _to_group(delta, d), name=f"b{j}_attn{g}_bwd")
                dqs.append(_from_group(dq, d))
                dk_parts.append(_from_group(dk, d))
                dv_parts.append(_from_group(dv, d))
                dbs.append(db)
            dbias_parts.append(jnp.stack(dbs))
            dq_tok = jnp.concatenate(dqs, axis=1)
            dw("b_w_q", j, s["xb"], dq_tok, f"b{j}_q_dw")
            token = ship(gw, [("b_w_o", j), ("b_w_q", j)])
            G = _mm(dq_tok, W["b_w_q", j], tb=True, add=dr1, add_scale=alpha, after=token, name=f"b{j}_q_dx")
            if j == 0:
                dkv = jnp.concatenate([_addn(dk_parts, BF16, name="dk_sum"), _addn(dv_parts, BF16, name="dv_sum")], axis=1)
                dw("kv_w", 0, s["xb"], dkv, "kv_dw")
                token = ship(gw, [("kv_w", 0)])
                G = _mm(dkv, W["kv_w", 0], tb=True, add=G, add_scale=1.0, after=token, name="kv_dx")
    rel = _rel_grad(dbias_parts, name="rel_grad")
    grel = rel[:, :, :H].transpose(1, 0, 2).reshape(REL_BUCKETS, N_GROUPS * H)
    gsmall = {k: jnp.stack(v) for k, v in gp.items()}
    gsmall["rel_table"] = grel
    return loss_row, G, gw, gsmall


BIG = (("a_w_in", COL), ("a_w_out", ROW), ("kv_w", ROW), ("b_w_q", COL), ("b_w_o", COL),
       ("ffn_w_up", COL), ("ffn_w_down", ROW))
SMALL_SHARDED = ("a_ln_g", "a_ln_b", "ffn_conv_w", "ln_g", "ln_b")
SMALL_REPLICATED = ("a_w_s", "a_b_s", "rel_table", "ffn_conv_b")
WEIGHTS = ("a_w_in", "a_ln_g", "a_ln_b", "a_w_s", "a_b_s", "a_w_out", "kv_w", "b_w_q", "b_w_o", "rel_table",
           "ffn_w_up", "ffn_conv_w", "ffn_conv_b", "ffn_w_down", "ln_g", "ln_b")


def kernel(x, a_w_in, a_ln_g, a_ln_b, a_w_s, a_b_s, a_w_out, kv_w, b_w_q, b_w_o, rel_table, ffn_w_up, ffn_conv_w, ffn_conv_b, ffn_w_down, ln_g, ln_b, loss_target, m_a_w_in, m_a_ln_g, m_a_ln_b, m_a_w_s, m_a_b_s, m_a_w_out, m_kv_w, m_b_w_q, m_b_w_o, m_rel_table, m_ffn_w_up, m_ffn_conv_w, m_ffn_conv_b, m_ffn_w_down, m_ln_g, m_ln_b, v_a_w_in, v_a_ln_g, v_a_ln_b, v_a_w_s, v_a_b_s, v_a_w_out, v_kv_w, v_b_w_q, v_b_w_o, v_rel_table, v_ffn_w_up, v_ffn_conv_w, v_ffn_conv_b, v_ffn_w_down, v_ln_g, v_ln_b):
    w = dict(a_w_in=a_w_in, a_ln_g=a_ln_g, a_ln_b=a_ln_b, a_w_s=a_w_s, a_b_s=a_b_s, a_w_out=a_w_out, kv_w=kv_w,
             b_w_q=b_w_q, b_w_o=b_w_o, rel_table=rel_table, ffn_w_up=ffn_w_up, ffn_conv_w=ffn_conv_w,
             ffn_conv_b=ffn_conv_b, ffn_w_down=ffn_w_down, ln_g=ln_g, ln_b=ln_b)
    m = dict(a_w_in=m_a_w_in, a_ln_g=m_a_ln_g, a_ln_b=m_a_ln_b, a_w_s=m_a_w_s, a_b_s=m_a_b_s, a_w_out=m_a_w_out,
             kv_w=m_kv_w, b_w_q=m_b_w_q, b_w_o=m_b_w_o, rel_table=m_rel_table, ffn_w_up=m_ffn_w_up,
             ffn_conv_w=m_ffn_conv_w, ffn_conv_b=m_ffn_conv_b, ffn_w_down=m_ffn_w_down, ln_g=m_ln_g, ln_b=m_ln_b)
    v = dict(a_w_in=v_a_w_in, a_ln_g=v_a_ln_g, a_ln_b=v_a_ln_b, a_w_s=v_a_w_s, a_b_s=v_a_b_s, a_w_out=v_a_w_out,
             kv_w=v_kv_w, b_w_q=v_b_w_q, b_w_o=v_b_w_o, rel_table=v_rel_table, ffn_w_up=v_ffn_w_up,
             ffn_conv_w=v_ffn_conv_w, ffn_conv_b=v_ffn_conv_b, ffn_w_down=v_ffn_w_down, ln_g=v_ln_g, ln_b=v_ln_b)
    chip = 2 * lax.axis_index("x") + lax.axis_index("y")

    big_names = [n for n, _ in BIG]
    big_axes = [a for _, a in BIG]
    where = jnp.stack([chip, lax.axis_index("c"), 2 * chip + lax.axis_index("c")]).astype(jnp.int32)
    shards = [w[n].reshape((1,) + w[n].shape) if w[n].ndim == 2 else w[n] for n in big_names]
    small_shapes = [w[n].shape for n in SMALL_SHARDED]
    small = _pack([w[n] for n in SMALL_SHARDED])
    shard_of = dict(zip(big_names, shards))
    axis_of = dict(BIG)
    n_layers = {n: s.shape[0] for n, s in shard_of.items()}
    n_a = w["a_w_s"].shape[0]
    stages = {}
    for i in range(w["ffn_w_up"].shape[0]):
        if i < n_a:
            stages[f"{i}a"], stages[f"{i}b"] = [("a_w_in", i)], [("a_w_out", i)]
        else:
            stages[f"{i}a"] = ([("kv_w", 0)] if i == n_a else []) + [("b_w_q", i - n_a)]
            stages[f"{i}b"] = [("b_w_o", i - n_a)]
        stages[f"{i}c"], stages[f"{i}d"] = [("ffn_w_up", i)], [("ffn_w_down", i)]
    stages["0a"].append(("small", 0))
    assert sum(len(st) for st in stages.values()) == sum(n_layers.values()) + 1
    flights, full_shape, all_started = {}, {}, None
    for batch, names in enumerate(([st for st in stages if st[0] == "0"], [st for st in stages if st[0] != "0"])):
        order = [key for st in names for key in stages[st]]
        placed, axes = [], []
        for n, l in order:
            if n == "small":
                placed.append(_place_shard(small[None], 0, ROW, where, F32, name="place_small"))
                axes.append(ROW)
            else:
                placed.append(_place_shard(shard_of[n], l, axis_of[n], where, BF16, name=f"place_{n}{l}",
                                           after=all_started))
                axes.append(axis_of[n])
                full_shape[n] = (n_layers[n],) + placed[-1].shape[1:]
        send_sems, recv_sems, flying, all_started = _gather_start(placed, axes, name=f"gather_start{batch}")
        for st in names:
            flights[st] = (send_sems, recv_sems, flying, axes, order)

    def fetch(stage, after):
        if stage not in stages:
            return {}, {}
        send_sems, recv_sems, flying, axes, order = flights[stage]
        if stage == "0a":
            after = all_started
        idxs = [order.index(key) for key in stages[stage]]
        landed = _gather_wait(send_sems, recv_sems, [flying[i] for i in idxs], [axes[i] for i in idxs], idxs, after,
                              name=f"gather_wait_{stage}")
        W_new, P_new = {}, {}
        for key, arr in zip(stages[stage], landed):
            if key[0] == "small":
                small_all = arr.reshape((N_CHIPS,) + small.shape)
                per_chip = [_unpack(small_all[j], small_shapes) for j in range(N_CHIPS)]
                for i, n in enumerate(SMALL_SHARDED):
                    P_new[n] = jnp.concatenate([per_chip[j][i] for j in range(N_CHIPS)], axis=-1)
            else:
                W_new[key] = arr.reshape(arr.shape[1:])
        return W_new, P_new

    slots ={n: lax.empty((N_DEV, n_layers[n]) + _part_geometry(full_shape[n], axis_of[n]), BF16) for n in big_names}
    sems, plan, tokens = [], [], []

    def ship(gw, items):
        names = [n for n, _ in items]
        send, recv, grads, landing, token = _scatter_start(
            [gw[n] for n in names], [slots[n] for n in names], [l for _, l in items], [axis_of[n] for n in names],
            name="scatter_start_" + "_".join(f"{n}{l}" for n, l in items))
        for n, g, s in zip(names, grads, landing):
            gw[n], slots[n] = g, s
        sems.append((send, recv))
        plan.append([(big_names.index(n), l) for n, l in items])
        tokens.append(token)
        return token

    loss_row, grad_x, gw, gsmall = _local_step(x[0], loss_target[0], {n: w[n] for n in SMALL_REPLICATED}, fetch,
                                               ship, n_layers)

    small_names = list(SMALL_SHARDED) + list(SMALL_REPLICATED)
    small_pack = _pack([gsmall[n] for n in small_names] + [loss_row[0, :1]])
    small_sems = _small_start(small_pack, lax.empty((N_DEV,) + small_pack.shape, F32), tokens[-1],
                              name="small_start")
    own, landed = _scatter_wait(sems, [gw[n] for n in big_names], [slots[n] for n in big_names], big_axes, plan,
                                grad_x, name="scatter_wait")
    sums = [_sum_parts(b, g, ax, where, name=f"sum_{n}") for (n, ax), b, g in zip(BIG, landed, own)]
    pairs = _pair_gather(sums, name="pair_gather")
    grad = {n: p.reshape(w[n].shape) for n, p in zip(big_names, pairs)}

    delta, new_m, new_v = {}, {}, {}
    for n, _ in BIG:
        delta[n], new_m[n], new_v[n] = _adamw(w[n], grad[n], m[n], v[n], name=f"adamw_{n}")
    small_own, small_slots = _small_wait(*small_sems, new_v[big_names[-1]], name="small_wait")
    small_sum = _sum_small(small_slots, small_own, where, name="sum_small")
    full_small = _unpack(small_sum, [gsmall[n].shape for n in small_names] + [(1,)])
    loss = full_small[-1][0]
    for n, gfull in zip(small_names, full_small):
        if n in SMALL_SHARDED:
            width = w[n].shape[-1]
            grad[n] = lax.dynamic_slice_in_dim(gfull, chip * width, width, axis=-1)
        else:
            grad[n] = gfull
    shapes = [w[n].shape for n in small_names]
    packed = _adamw(_pack([w[n] for n in small_names]), _pack([grad[n] for n in small_names]),
                    _pack([m[n] for n in small_names]), _pack([v[n] for n in small_names]), name="adamw_small")
    for out, res in zip((delta, new_m, new_v), packed):
        for n, a in zip(small_names, _unpack(res, shapes)):
            out[n] = a

    return (loss, grad_x[None], *[grad[n] for n in WEIGHTS], *[delta[n] for n in WEIGHTS],
            *[new_m[n] for n in WEIGHTS], *[new_v[n] for n in WEIGHTS])
```

```python
import functools
import math

import numpy as np
import jax
import jax.numpy as jnp
from jax import lax
from jax.experimental import pallas as pl
from jax.experimental.pallas import tpu as pltpu

F32 = jnp.float32
BF16 = jnp.bfloat16
MESH = pl.DeviceIdType.MESH

CHUNK = 128
HEAD_DIM = 64
DILATED_GROUPS = ((128, 1), (512, 4), (2048, 16))
N_GROUPS = len(DILATED_GROUPS)
BLK = 128
REL_BUCKETS = 32
REL_MAX_DIST = 2048
LN_EPS = 1e-5
NEG = -1e30
ADAM_LR = 0.001
ADAM_B1 = 0.9
ADAM_B2 = 0.999
ADAM_EPS = 1e-08
ADAM_WD = 0.01
ADAM_STEP = 10

N_CHIPS = 4
N_DEV = 8
LANES = 128
SUBLANES = 8
VMEM_LIMIT = 48 * 1024 * 1024

_SQRT_HALF = 0.7071067811865476
_INV_SQRT_2PI = 0.3989422804014327


def _cp(sem=None, vmem=VMEM_LIMIT):
    return pltpu.CompilerParams(dimension_semantics=sem, vmem_limit_bytes=vmem)


def _tile(dim, target, align=LANES):
    if dim <= target:
        return dim
    t = (target // align) * align
    while t >= align:
        if dim % t == 0:
            return t
        t -= align
    return dim


def _gelu(x):
    return 0.5 * x * (1.0 + lax.erf(x * _SQRT_HALF))


def _gelu_grad(x):
    return 0.5 * (1.0 + lax.erf(x * _SQRT_HALF)) + x * (_INV_SQRT_2PI * jnp.exp(-0.5 * x * x))


def _dot(a, b, ca, cb):
    return lax.dot_general(a, b, (((ca,), (cb,)), ((), ())), preferred_element_type=F32)


def _mn_tile(dim):
    return max(_tile(dim, 1024), _tile(dim, 1408))


def _mm(a, b, *, name, ta=False, tb=False, out_dtype=F32, tiles=None,
        a_lead=(), b_lead=(), a_halves=False, b_halves=False, add=None, add_scale=1.0, out_into=None, after=None):
    a2, b2 = a.shape[-2:], b.shape[-2:]
    M, K = (a2[1], a2[0]) if ta else a2
    Kb, N = (b2[1], b2[0]) if tb else b2
    if a_halves:
        K = 2 * K
    if b_halves:
        N = 2 * N
    assert K == Kb, (name, a.shape, b.shape)
    if tiles is None:
        k_target = 1408 if a_halves else (K if K <= 2816 else 2048)
        tm, tn = _mn_tile(M), _mn_tile(N // 2 if b_halves else N)
        tk = _tile(K // 2 if a_halves else K, k_target)
    else:
        tm = _tile(M, tiles[0])
        tn = _tile(N // 2 if b_halves else N, tiles[1])
        tk = _tile(K // 2 if a_halves else K, tiles[2])
    nm, nn, nk = M // tm, N // tn, K // tk
    nkh, nnh = nk // 2, nn // 2

    def lead(idx, rest):
        return tuple(idx) + tuple(rest)

    sq_a = (None,) * (a.ndim - 2)
    sq_b = (None,) * (b.ndim - 2)
    if a_halves:
        assert not ta and a.ndim == 3
        a_spec = pl.BlockSpec((None, tm, tk), lambda i, j, k: (k // nkh, i, k % nkh))
    elif ta:
        a_spec = pl.BlockSpec(sq_a + (tk, tm), lambda i, j, k: lead(a_lead, (k, i)))
    else:
        a_spec = pl.BlockSpec(sq_a + (tm, tk), lambda i, j, k: lead(a_lead, (i, k)))
    if b_halves:
        assert not tb and b.ndim == 3
        b_spec = pl.BlockSpec((None, tk, tn), lambda i, j, k: (j // nnh, k, j % nnh))
    elif tb:
        b_spec = pl.BlockSpec(sq_b + (tn, tk), lambda i, j, k: lead(b_lead, (j, k)))
    else:
        b_spec = pl.BlockSpec(sq_b + (tk, tn), lambda i, j, k: lead(b_lead, (k, j)))
    mn_spec = pl.BlockSpec((tm, tn), lambda i, j, k: (i, j))
    in_specs = [a_spec, b_spec]
    args = [a, b]
    if add is not None:
        in_specs.append(mn_spec)
        args.append(add)
    aliases = {}
    if out_into is None:
        o_spec, out_shape = mn_spec, jax.ShapeDtypeStruct((M, N), out_dtype)
    else:
        buf, n_layers, layer = out_into
        o_spec = pl.BlockSpec((None, tm, tn), lambda i, j, k: (layer, i, j))
        out_shape = jax.ShapeDtypeStruct((n_layers, M, N), out_dtype)
        if buf is not None:
            aliases = {len(args): 0}
            in_specs.append(pl.BlockSpec(memory_space=pl.ANY))
            args.append(buf)
    if after is not None:
        in_specs.append(pl.BlockSpec(memory_space=pl.ANY))
        args.append(after)
    n_in = len(args)
    ca, cb = (0 if ta else 1), (1 if tb else 0)

    def body(*refs):
        a_ref, b_ref = refs[:2]
        add_ref = refs[2] if add is not None else None
        o_ref = refs[n_in]

        def finish(acc):
            if add_ref is not None:
                acc = acc + add_scale * add_ref[...]
            o_ref[...] = acc.astype(o_ref.dtype)

        if nk == 1:
            finish(_dot(a_ref[...], b_ref[...], ca, cb))
        else:
            acc_ref = refs[-1]
            k = pl.program_id(2)

            @pl.when(k == 0)
            def _():
                acc_ref[...] = jnp.zeros_like(acc_ref)

            acc_ref[...] += _dot(a_ref[...], b_ref[...], ca, cb)

            @pl.when(k == nk - 1)
            def _():
                finish(acc_ref[...])

    return pl.pallas_call(
        body, name=name, grid=(nm, nn, nk),
        in_specs=in_specs, out_specs=o_spec, out_shape=out_shape,
        input_output_aliases=aliases,
        scratch_shapes=[] if nk == 1 else [pltpu.VMEM((tm, tn), F32)],
        compiler_params=_cp(("parallel", "parallel", "arbitrary")),
    )(*args)


def _ln_stats(r):
    mu = jnp.mean(r, axis=-1, keepdims=True)
    xc = r - mu
    var = jnp.mean(xc * xc, axis=-1, keepdims=True)
    rstd = lax.rsqrt(var + LN_EPS)
    return xc * rstd, rstd


def _add_ln(x, h, g, b, alpha, *, name):
    T, D = x.shape
    tr = _tile(T, 512, SUBLANES)
    row = pl.BlockSpec((tr, D), lambda i: (i, 0))
    vec = pl.BlockSpec((1, D), lambda i: (0, 0))

    def body(x_ref, h_ref, g_ref, b_ref, o_ref, ob_ref):
        xhat, _ = _ln_stats(alpha * x_ref[...] + h_ref[...])
        y = xhat * g_ref[...] + b_ref[...]
        o_ref[...] = y
        ob_ref[...] = y.astype(BF16)

    return pl.pallas_call(
        body, name=name, grid=(T // tr,),
        in_specs=[row, row, vec, vec], out_specs=[row, row],
        out_shape=[jax.ShapeDtypeStruct((T, D), F32), jax.ShapeDtypeStruct((T, D), BF16)],
        compiler_params=_cp(("parallel",)),
    )(x, h, g.reshape(1, D), b.reshape(1, D))


def _ln_bwd(dy, x, h, g, alpha, *, name):
    T, D = x.shape
    tr = _tile(T, 512, SUBLANES)
    row = pl.BlockSpec((tr, D), lambda i: (i, 0))
    vec = pl.BlockSpec((1, D), lambda i: (0, 0))

    def body(dy_ref, x_ref, h_ref, g_ref, dr_ref, drb_ref, dg_ref, db_ref):
        @pl.when(pl.program_id(0) == 0)
        def _():
            dg_ref[...] = jnp.zeros_like(dg_ref)
            db_ref[...] = jnp.zeros_like(db_ref)

        xhat, rstd = _ln_stats(alpha * x_ref[...] + h_ref[...])
        dyv = dy_ref[...]
        dxhat = dyv * g_ref[...]
        c1 = jnp.mean(dxhat, axis=-1, keepdims=True)
        c2 = jnp.mean(dxhat * xhat, axis=-1, keepdims=True)
        dr = rstd * (dxhat - c1 - xhat * c2)
        dr_ref[...] = dr
        drb_ref[...] = dr.astype(BF16)
        dg_ref[...] += jnp.sum(dyv * xhat, axis=0, keepdims=True)
        db_ref[...] += jnp.sum(dyv, axis=0, keepdims=True)

    return pl.pallas_call(
        body, name=name, grid=(T // tr,),
        in_specs=[row, row, row, vec], out_specs=[row, row, vec, vec],
        out_shape=[jax.ShapeDtypeStruct((T, D), F32), jax.ShapeDtypeStruct((T, D), BF16),
                   jax.ShapeDtypeStruct((1, D), F32), jax.ShapeDtypeStruct((1, D), F32)],
        compiler_params=_cp(("arbitrary",)),
    )(dy, x, h, g.reshape(1, D))


def _sgu_pieces(zp, lg, lb, E):
    z = _gelu(zp)
    u, v = z[:, :E], z[:, E:]
    vhat, rstd = _ln_stats(v)
    vn = vhat * lg + lb
    return u, vhat, rstd, vn


def _tril_mask():
    t = lax.broadcasted_iota(jnp.int32, (CHUNK, CHUNK), 0)
    s = lax.broadcasted_iota(jnp.int32, (CHUNK, CHUNK), 1)
    return s <= t


def _sgu_fwd(zp, ws, bs, lg, lb, *, name):
    T, E2 = zp.shape
    E = E2 // 2
    G = ws.shape[0]
    cw = E // G

    def body(zp_ref, ws_ref, bs_ref, lg_ref, lb_ref, y_ref):
        u, _, _, vn = _sgu_pieces(zp_ref[...].astype(F32), lg_ref[...], lb_ref[...], E)
        vnb = vn.astype(BF16)
        tril = _tril_mask()
        for g in range(G):
            cols = slice(g * cw, (g + 1) * cw)
            w = jnp.where(tril, ws_ref[g], 0.0).astype(BF16)
            sv = _dot(w, vnb[:, cols], 1, 0) + bs_ref[g]
            y_ref[:, cols] = (u[:, cols] * sv).astype(BF16)

    return pl.pallas_call(
        body, name=name, grid=(T // CHUNK,),
        in_specs=[pl.BlockSpec((CHUNK, E2), lambda n: (n, 0)),
                  pl.BlockSpec((G, CHUNK, CHUNK), lambda n: (0, 0, 0)),
                  pl.BlockSpec((G, CHUNK, 1), lambda n: (0, 0, 0)),
                  pl.BlockSpec((1, E), lambda n: (0, 0)),
                  pl.BlockSpec((1, E), lambda n: (0, 0))],
        out_specs=pl.BlockSpec((CHUNK, E), lambda n: (n, 0)),
        out_shape=jax.ShapeDtypeStruct((T, E), BF16),
        compiler_params=_cp(("parallel",)),
    )(zp, ws, bs.reshape(G, CHUNK, 1), lg.reshape(1, E), lb.reshape(1, E))


def _sgu_bwd(zp, dy, ws, bs, lg, lb, *, name):
    T, E2 = zp.shape
    E = E2 // 2
    G = ws.shape[0]
    cw = E // G
    assert G <= LANES

    def body(zp_ref, dy_ref, ws_ref, bs_ref, lg_ref, lb_ref,
             dzp_ref, dws_ref, dbs_ref, dlg_ref, dlb_ref, dvn_ref):
        @pl.when(pl.program_id(0) == 0)
        def _():
            dws_ref[...] = jnp.zeros_like(dws_ref)
            dbs_ref[...] = jnp.zeros_like(dbs_ref)
            dlg_ref[...] = jnp.zeros_like(dlg_ref)
            dlb_ref[...] = jnp.zeros_like(dlb_ref)

        zpv = zp_ref[...].astype(F32)
        lgv = lg_ref[...]
        u, vhat, rstd, vn = _sgu_pieces(zpv, lgv, lb_ref[...], E)
        gp = _gelu_grad(zpv)
        vnb = vn.astype(BF16)
        tril = _tril_mask()
        lane = lax.broadcasted_iota(jnp.int32, (CHUNK, LANES), 1)
        dbs = dbs_ref[...]
        for g in range(G):
            cols = slice(g * cw, (g + 1) * cw)
            w = jnp.where(tril, ws_ref[g], 0.0).astype(BF16)
            sv = _dot(w, vnb[:, cols], 1, 0) + bs_ref[g]
            dyg = dy_ref[:, cols].astype(F32)
            dzp_ref[:, cols] = (dyg * sv * gp[:, cols]).astype(BF16)
            dsv = dyg * u[:, cols]
            dsvb = dsv.astype(BF16)
            dws_ref[g] += jnp.where(tril, _dot(dsvb, vnb[:, cols], 1, 1), 0.0)
            dvn_ref[:, cols] = _dot(w, dsvb, 0, 0)
            dbs = dbs + jnp.where(lane == g, jnp.sum(dsv, axis=1, keepdims=True), 0.0)
        dbs_ref[...] = dbs
        dvn = dvn_ref[...]
        dlg_ref[...] += jnp.sum(dvn * vhat, axis=0, keepdims=True)
        dlb_ref[...] += jnp.sum(dvn, axis=0, keepdims=True)
        dvhat = dvn * lgv
        c1 = jnp.mean(dvhat, axis=-1, keepdims=True)
        c2 = jnp.mean(dvhat * vhat, axis=-1, keepdims=True)
        dv = rstd * (dvhat - c1 - vhat * c2)
        dzp_ref[:, E:] = (dv * gp[:, E:]).astype(BF16)

    vecE = pl.BlockSpec((1, E), lambda n: (0, 0))
    return pl.pallas_call(
        body, name=name, grid=(T // CHUNK,),
        in_specs=[pl.BlockSpec((CHUNK, E2), lambda n: (n, 0)),
                  pl.BlockSpec((CHUNK, E), lambda n: (n, 0)),
                  pl.BlockSpec((G, CHUNK, CHUNK), lambda n: (0, 0, 0)),
                  pl.BlockSpec((G, CHUNK, 1), lambda n: (0, 0, 0)),
                  vecE, vecE],
        out_specs=[pl.BlockSpec((CHUNK, E2), lambda n: (n, 0)),
                   pl.BlockSpec((G, CHUNK, CHUNK), lambda n: (0, 0, 0)),
                   pl.BlockSpec((CHUNK, LANES), lambda n: (0, 0)),
                   vecE, vecE],
        out_shape=[jax.ShapeDtypeStruct((T, E2), BF16),
                   jax.ShapeDtypeStruct((G, CHUNK, CHUNK), F32),
                   jax.ShapeDtypeStruct((CHUNK, LANES), F32),
                   jax.ShapeDtypeStruct((1, E), F32), jax.ShapeDtypeStruct((1, E), F32)],
        scratch_shapes=[pltpu.VMEM((CHUNK, E), F32)],
        compiler_params=_cp(("arbitrary",)),
    )(zp, dy, ws, bs.reshape(G, CHUNK, 1), lg.reshape(1, E), lb.reshape(1, E))


def _delay(h, k):
    t = lax.broadcasted_iota(jnp.int32, h.shape, 0)
    return jnp.where(t >= k, pltpu.roll(h, k, 0), 0.0)


def _advance(d, k):
    T = d.shape[0]
    t = lax.broadcasted_iota(jnp.int32, d.shape, 0)
    return jnp.where(t < T - k, pltpu.roll(d, T - k, 0), 0.0)


def _conv3(h, w_ref, b_ref):
    return w_ref[2:3, :] * h + w_ref[1:2, :] * _delay(h, 1) + w_ref[0:1, :] * _delay(h, 2) + b_ref[...]


def _convglu_fwd(hup, cw, cb, *, name):
    T, F2 = hup.shape
    F = F2 // 2
    tc = LANES
    nt = F // tc

    def body(ha_ref, hg_ref, wa_ref, wg_ref, ba_ref, bg_ref, o_ref, a_ref, g_ref):
        a = _conv3(ha_ref[...].astype(F32), wa_ref, ba_ref)
        g = _conv3(hg_ref[...].astype(F32), wg_ref, bg_ref)
        o_ref[...] = (_gelu(a) * g).astype(BF16)
        a_ref[...] = a.astype(BF16)
        g_ref[...] = g.astype(BF16)

    col = lambda off: pl.BlockSpec((T, tc), lambda j: (0, j + off))
    w3 = lambda off: pl.BlockSpec((3, tc), lambda j: (0, j + off))
    b1 = lambda off: pl.BlockSpec((1, tc), lambda j: (0, j + off))
    shp = jax.ShapeDtypeStruct((T, F), BF16)
    return pl.pallas_call(
        body, name=name, grid=(nt,),
        in_specs=[col(0), col(nt), w3(0), w3(nt), b1(0), b1(nt)],
        out_specs=[col(0)] * 3, out_shape=[shp] * 3,
        compiler_params=_cp(("parallel",)),
    )(hup, hup, cw, cw, cb.reshape(1, F2), cb.reshape(1, F2))


def _convglu_bwd(hup, a, g, dact, cw, *, name):
    T, F2 = hup.shape
    F = F2 // 2
    tc = LANES
    nt = F // tc

    def half(h, w_ref, d, dh_ref, dw_ref, db_ref, i):
        d1, d2 = _advance(d, 1), _advance(d, 2)
        dh_ref[i] = (w_ref[2:3, :] * d + w_ref[1:2, :] * d1 + w_ref[0:1, :] * d2).astype(BF16)
        c = _dot(h, jnp.concatenate([d2, d1, d], axis=1).astype(BF16), 0, 0)
        eye = lax.broadcasted_iota(jnp.int32, (tc, tc), 0) == lax.broadcasted_iota(jnp.int32, (tc, tc), 1)
        for k in range(3):
            dw_ref[i, k:k + 1, :] = jnp.sum(jnp.where(eye, c[:, k * tc:(k + 1) * tc], 0.0), axis=0, keepdims=True)
        db_ref[i] = jnp.sum(d, axis=0, keepdims=True)

    def body(ha_ref, hg_ref, a_ref, g_ref, d_ref, wa_ref, wg_ref, dh_ref, dw_ref, db_ref):
        a = a_ref[...].astype(F32)
        d = d_ref[...].astype(F32)
        half(ha_ref[...].astype(BF16), wa_ref, d * g_ref[...].astype(F32) * _gelu_grad(a), dh_ref, dw_ref, db_ref, 0)
        half(hg_ref[...].astype(BF16), wg_ref, d * _gelu(a), dh_ref, dw_ref, db_ref, 1)

    col = lambda off: pl.BlockSpec((T, tc), lambda j: (0, j + off))
    w3 = lambda off: pl.BlockSpec((3, tc), lambda j: (0, j + off))
    return pl.pallas_call(
        body, name=name, grid=(nt,),
        in_specs=[col(0), col(nt), col(0), col(0), col(0), w3(0), w3(nt)],
        out_specs=[pl.BlockSpec((2, T, tc), lambda j: (0, 0, j)),
                   pl.BlockSpec((2, 3, tc), lambda j: (0, 0, j)),
                   pl.BlockSpec((2, 1, tc), lambda j: (0, 0, j))],
        out_shape=[jax.ShapeDtypeStruct((2, T, F), BF16),
                   jax.ShapeDtypeStruct((2, 3, F), F32),
                   jax.ShapeDtypeStruct((2, 1, F), F32)],
        compiler_params=_cp(("parallel",)),
    )(hup, hup, a, g, dact, cw, cw)


def _bucket_table():
    iq = np.arange(BLK)[:, None]
    ik = np.arange(2 * BLK)[None, :]
    delta = iq + BLK - ik
    out = []
    for win, dil in DILATED_GROUPS:
        valid = (delta >= 0) & (delta <= win // dil)
        n = (np.clip(delta, 0, None) * dil).astype(np.int32)
        max_exact = REL_BUCKETS // 2
        nf = np.maximum(n, 1).astype(np.float32)
        large = max_exact + (np.log(nf / np.float32(max_exact)) / np.float32(math.log(REL_MAX_DIST / max_exact))
                             * np.float32(REL_BUCKETS - max_exact)).astype(np.int32)
        large = np.minimum(large, REL_BUCKETS - 1)
        out.append(np.where(valid, np.where(n < max_exact, n, large), -1))
    return np.stack(out).astype(np.int32)


def _band_bias(rel_table, H, *, name):
    bkt = jnp.asarray(_bucket_table())

    def body(tab_ref, bk_ref, o_ref):
        g = pl.program_id(0)
        bk = bk_ref[...]
        prev = lax.broadcasted_iota(jnp.int32, bk.shape, 1) < BLK
        for h in range(H):
            def bucket(c, acc):
                return jnp.where(bk == c, tab_ref[c, g * H + h], acc)

            b = lax.fori_loop(0, REL_BUCKETS, bucket, jnp.zeros(bk.shape, F32))
            b = jnp.where(bk >= 0, b, NEG)
            o_ref[h, 0] = b
            o_ref[h, 1] = jnp.where(prev, NEG, b)

    return pl.pallas_call(
        body, name=name, grid=(N_GROUPS,),
        in_specs=[pl.BlockSpec(memory_space=pltpu.SMEM),
                  pl.BlockSpec((None, BLK, 2 * BLK), lambda g: (g, 0, 0))],
        out_specs=pl.BlockSpec((None, H, 2, BLK, 2 * BLK), lambda g: (g, 0, 0, 0, 0)),
        out_shape=jax.ShapeDtypeStruct((N_GROUPS, H, 2, BLK, 2 * BLK), F32),
        compiler_params=_cp(("parallel",)),
    )(rel_table, bkt)


PAIR = 2 * HEAD_DIM


def _bdot(a, b, ca, cb):
    return lax.dot_general(a, b, (((ca,), (cb,)), ((0,), (0,))), preferred_element_type=F32)


def _blocks_per_step(nb, dil):
    return 2 if (nb // dil) % 2 == 0 else 1


def _blk(i):
    return slice(i * BLK, (i + 1) * BLK)


def _heads(ref, i, H, own_columns_only):
    low = lax.broadcasted_iota(jnp.int32, (BLK, PAIR), 1) < HEAD_DIM
    out = []
    for h in range(H):
        a = ref[_blk(i), (h // 2) * PAIR:(h // 2 + 1) * PAIR]
        if own_columns_only:
            a = jnp.where(low if h % 2 == 0 else jnp.logical_not(low), a, jnp.zeros_like(a))
        out.append(a)
    return jnp.stack(out)


def _pair_sums(o):
    return [o[2 * p] + o[2 * p + 1] for p in range(o.shape[0] // 2)]


def _store_pairs(ref, i, o, dtype, scale=None):
    for p, v in enumerate(_pair_sums(o)):
        ref[_blk(i), p * PAIR:(p + 1) * PAIR] = (v if scale is None else scale * v).astype(dtype)


def _stat_tile(s):
    lane = lax.broadcasted_iota(jnp.int32, (BLK, LANES), 1)
    t = jnp.zeros((BLK, LANES), F32)
    for h in range(s.shape[0]):
        t = jnp.where(lane == h, s[h], t)
    return t


def _stat_cols(ref, i, H):
    return jnp.stack([ref[_blk(i), h:h + 1] for h in range(H)])


def _scores(qm, kp, kc, bias):
    scale = HEAD_DIM ** -0.5
    sp = _bdot(qm, kp, 2, 2) * scale + bias[:, :, :BLK]
    sc = _bdot(qm, kc, 2, 2) * scale + bias[:, :, BLK:]
    return sp, sc


def _attn_specs(nb, dil, g, H, HD, sb):
    nbg = nb // dil
    QB = _blocks_per_step(nb, dil)
    cur = lambda col: pl.BlockSpec((QB * BLK, HD), lambda b: (sb(b), col))
    prev = lambda col: pl.BlockSpec((BLK, HD), lambda b: (jnp.maximum(QB * sb(b) - 1, 0), col))
    stat = pl.BlockSpec((QB * BLK, LANES), lambda b: (sb(b), 0))
    bias0 = pl.BlockSpec((None, H, None, BLK, 2 * BLK),
                         lambda b: (g, 0, jnp.where((QB * sb(b)) % nbg == 0, 1, 0), 0, 0))
    bias1 = pl.BlockSpec((None, H, None, BLK, 2 * BLK), lambda b: (g, 0, 0, 0, 0))
    return cur, prev, stat, bias0, bias1


def _attn_fwd(q, k, v, bias, g, dil, *, name):
    (qa, qc), (ka, kc_), (va, vc_) = q, k, v
    T = qa.shape[0]
    H = bias.shape[1]
    HD = H * HEAD_DIM
    assert H % 2 == 0 and H <= LANES and T % (dil * BLK) == 0
    nb = T // BLK
    QB = _blocks_per_step(nb, dil)
    cur, prev, stat, bias0, bias1 = _attn_specs(nb, dil, g, H, HD, lambda b: b)

    def body(q_ref, kp_ref, kc_ref, vp_ref, vc_ref, b0_ref, b1_ref, o_ref, l_ref):
        for i in range(QB):
            kpi, vpi = (kp_ref, vp_ref) if i == 0 else (kc_ref, vc_ref)
            sp, sc = _scores(_heads(q_ref, i, H, True), _heads(kpi, max(i - 1, 0), H, False),
                             _heads(kc_ref, i, H, False), (b0_ref if i == 0 else b1_ref)[...])
            m = jnp.maximum(jnp.max(sp, axis=-1, keepdims=True), jnp.max(sc, axis=-1, keepdims=True))
            pp, pc = jnp.exp(sp - m), jnp.exp(sc - m)
            den = jnp.sum(pp, axis=-1, keepdims=True) + jnp.sum(pc, axis=-1, keepdims=True)
            o = (_bdot((pp / den).astype(BF16), _heads(vpi, max(i - 1, 0), H, True), 2, 1)
                 + _bdot((pc / den).astype(BF16), _heads(vc_ref, i, H, True), 2, 1))
            _store_pairs(o_ref, i, o, F32)
            l_ref[_blk(i), :] = _stat_tile(m + jnp.log(den))

    return pl.pallas_call(
        body, name=name, grid=(nb // QB,),
        in_specs=[cur(qc), prev(kc_), cur(kc_), prev(vc_), cur(vc_), bias0, bias1],
        out_specs=[cur(0), stat],
        out_shape=[jax.ShapeDtypeStruct((T, HD), F32), jax.ShapeDtypeStruct((T, LANES), F32)],
        compiler_params=_cp(("parallel",)),
    )(qa, ka, ka, va, va, bias, bias)


def _grouped_spec(tr, d, C):
    return pl.BlockSpec((d, tr // d, C), lambda i: (0, i, 0))


def _as_grouped(a, d):
    T, C = a.shape
    return a.reshape(d, T // d, C)


def _to_token_slabs(ref, slabs, d):
    n_r = ref.shape[1]
    for j in range(ref.shape[2] // LANES):
        for r in range(d):
            slabs.at[j][pl.ds(r, n_r, stride=d), :] = ref[r, :, j * LANES:(j + 1) * LANES].astype(F32)


def _from_token_slabs(slabs, ref, d):
    n_r = ref.shape[1]
    for j in range(ref.shape[2] // LANES):
        for r in range(d):
            ref[r, :, j * LANES:(j + 1) * LANES] = slabs.at[j][pl.ds(r, n_r, stride=d), :].astype(ref.dtype)


def _attn_combine(os, ls, dils, *, name):
    n = len(os)
    T, HD = os[0].shape
    H = HD // HEAD_DIM
    tr = _tile(T, 512, SUBLANES)
    n_sl = HD // LANES
    assert PAIR == LANES

    def body(*refs):
        o_refs, l_refs = refs[:n], refs[n:2 * n]
        oc_ref, ocb_ref = refs[2 * n:2 * n + 2]
        lt_refs = refs[2 * n + 2:3 * n + 2]
        o_scr, l_scr = refs[3 * n + 2], refs[3 * n + 3]
        for g in range(n):
            _to_token_slabs(o_refs[g], o_scr.at[g], dils[g])
            _to_token_slabs(l_refs[g], l_scr.at[g], dils[g])
        ls_ = [l_scr[g, 0] for g in range(n)]
        m = ls_[0]
        for l in ls_[1:]:
            m = jnp.maximum(m, l)
        ws = [jnp.exp(l - m) for l in ls_]
        den = ws[0]
        for w in ws[1:]:
            den = den + w
        l_scr[n, 0] = m + jnp.log(den)
        for g in range(n):
            _from_token_slabs(l_scr.at[n], lt_refs[g], dils[g])
        ws = [w / den for w in ws]
        low = lax.broadcasted_iota(jnp.int32, (tr, PAIR), 1) < HEAD_DIM
        for p in range(H // 2):
            acc = jnp.zeros((tr, PAIR), F32)
            for g in range(n):
                w = jnp.where(low, ws[g][:, 2 * p:2 * p + 1], ws[g][:, 2 * p + 1:2 * p + 2])
                acc = acc + w * o_scr[g, p]
            oc_ref[:, p * PAIR:(p + 1) * PAIR] = acc
            ocb_ref[:, p * PAIR:(p + 1) * PAIR] = acc.astype(BF16)

    wide = pl.BlockSpec((tr, HD), lambda i: (i, 0))
    lts = pl.pallas_call(
        body, name=name, grid=(T // tr,),
        in_specs=[_grouped_spec(tr, d, HD) for d in dils] + [_grouped_spec(tr, d, LANES) for d in dils],
        out_specs=[wide, wide] + [_grouped_spec(tr, d, LANES) for d in dils],
        out_shape=[jax.ShapeDtypeStruct((T, HD), F32), jax.ShapeDtypeStruct((T, HD), BF16)]
        + [jax.ShapeDtypeStruct((d, T // d, LANES), F32) for d in dils],
        scratch_shapes=[pltpu.VMEM((n, n_sl, tr, LANES), F32), pltpu.VMEM((n + 1, 1, tr, LANES), F32)],
        compiler_params=_cp(("parallel",)),
    )(*[_as_grouped(o, d) for o, d in zip(os, dils)], *[_as_grouped(l, d) for l, d in zip(ls, dils)])
    return lts[0], lts[1], [l.reshape(T, LANES) for l in lts[2:]]


def _attn_bwd_prep(do, oc, dils, *, name):
    T, HD = do.shape
    H = HD // HEAD_DIM
    n = len(dils)
    tr = _tile(T, 512, SUBLANES)
    n_sl = HD // LANES
    wide = pl.BlockSpec((tr, HD), lambda i: (i, 0))

    def body(*refs):
        do_ref, oc_ref = refs[:2]
        dl_refs, dob_refs = refs[2:2 + n], refs[2 + n:2 + 2 * n]
        do_scr, dl_scr = refs[2 + 2 * n], refs[3 + 2 * n]
        low = lax.broadcasted_iota(jnp.int32, (tr, PAIR), 1) < HEAD_DIM
        lane = lax.broadcasted_iota(jnp.int32, (tr, LANES), 1)
        dl = jnp.zeros((tr, LANES), F32)
        for p in range(H // 2):
            d = do_ref[:, p * PAIR:(p + 1) * PAIR]
            do_scr[p] = d
            prod = d * oc_ref[:, p * PAIR:(p + 1) * PAIR]
            dl = jnp.where(lane == 2 * p, jnp.sum(jnp.where(low, prod, 0.0), axis=-1, keepdims=True), dl)
            dl = jnp.where(lane == 2 * p + 1, jnp.sum(jnp.where(low, 0.0, prod), axis=-1, keepdims=True), dl)
        dl_scr[0] = dl
        for g in range(n):
            _from_token_slabs(dl_scr, dl_refs[g], dils[g])
            _from_token_slabs(do_scr, dob_refs[g], dils[g])

    outs = pl.pallas_call(
        body, name=name, grid=(T // tr,),
        in_specs=[wide, wide],
        out_specs=[_grouped_spec(tr, d, LANES) for d in dils] + [_grouped_spec(tr, d, HD) for d in dils],
        out_shape=[jax.ShapeDtypeStruct((d, T // d, LANES), F32) for d in dils]
        + [jax.ShapeDtypeStruct((d, T // d, HD), BF16) for d in dils],
        scratch_shapes=[pltpu.VMEM((n_sl, tr, LANES), F32), pltpu.VMEM((1, tr, LANES), F32)],
        compiler_params=_cp(("parallel",)),
    )(do, oc)
    return [a.reshape(T, LANES) for a in outs[:n]], [a.reshape(T, HD) for a in outs[n:]]


def _attn_bwd(q, k, v, bias, g, dil, do, lse, delta, *, name):
    (qa, qc), (ka, kc_), (va, vc_) = q, k, v
    T, HD = do.shape
    H = HD // HEAD_DIM
    nb = T // BLK
    scale = HEAD_DIM ** -0.5
    QB = _blocks_per_step(nb, dil)
    ns = nb // QB
    cur, prev, stat, bias0, bias1 = _attn_specs(nb, dil, g, H, HD, lambda b: jnp.minimum(b, ns - 1))
    late = pl.BlockSpec((QB * BLK, HD), lambda b: (jnp.maximum(b - 1, 0), 0))

    def body(q_ref, kp_ref, kc_ref, vp_ref, vc_ref, b0_ref, b1_ref, do_ref, l_ref, dl_ref,
             dq_ref, dk_ref, dv_ref, db_ref, ck_ref, cv_ref):
        b = pl.program_id(0)

        @pl.when(b == 0)
        def _():
            db_ref[...] = jnp.zeros_like(db_ref)
            ck_ref[...] = jnp.zeros_like(ck_ref)
            cv_ref[...] = jnp.zeros_like(cv_ref)

        @pl.when(b < ns)
        def _():
            dk_ref[...] = ck_ref[...]
            dv_ref[...] = cv_ref[...]
            for i in range(QB):
                kpi, vpi, ip = (kp_ref, vp_ref, 0) if i == 0 else (kc_ref, vc_ref, i - 1)
                qm, dom = _heads(q_ref, i, H, True), _heads(do_ref, i, H, True)
                sp, sc = _scores(qm, _heads(kpi, ip, H, False), _heads(kc_ref, i, H, False),
                                 (b0_ref if i == 0 else b1_ref)[...])
                l, dl = _stat_cols(l_ref, i, H), _stat_cols(dl_ref, i, H)
                pp, pc = jnp.exp(sp - l), jnp.exp(sc - l)
                dsp = pp * (_bdot(dom, _heads(vpi, ip, H, False), 2, 2) - dl)
                dsc = pc * (_bdot(dom, _heads(vc_ref, i, H, False), 2, 2) - dl)
                db_ref[:, :, :BLK] += dsp
                db_ref[:, :, BLK:] += dsc
                dspb, dscb = dsp.astype(BF16), dsc.astype(BF16)
                _store_pairs(dq_ref, i, _bdot(dspb, _heads(kpi, ip, H, True), 2, 1)
                             + _bdot(dscb, _heads(kc_ref, i, H, True), 2, 1), BF16, scale)
                dkp = _pair_sums(_bdot(dspb, qm, 1, 1))
                dkc = _pair_sums(_bdot(dscb, qm, 1, 1))
                dvp = _pair_sums(_bdot(pp.astype(BF16), dom, 1, 1))
                dvc = _pair_sums(_bdot(pc.astype(BF16), dom, 1, 1))
                for p in range(H // 2):
                    cols = slice(p * PAIR, (p + 1) * PAIR)
                    if i == 0:
                        dk_ref[_blk(QB - 1), cols] += scale * dkp[p]
                        dv_ref[_blk(QB - 1), cols] += dvp[p]
                    else:
                        ck_ref[_blk(i - 1), cols] += scale * dkp[p]
                        cv_ref[_blk(i - 1), cols] += dvp[p]
                    ck_ref[_blk(i), cols] = scale * dkc[p]
                    cv_ref[_blk(i), cols] = dvc[p]

        @pl.when(b == ns)
        def _():
            dk_ref[...] = ck_ref[...]
            dv_ref[...] = cv_ref[...]

    f32 = jax.ShapeDtypeStruct((T, HD), F32)
    return pl.pallas_call(
        body, name=name, grid=(ns + 1,),
        in_specs=[cur(qc), prev(kc_), cur(kc_), prev(vc_), cur(vc_), bias0, bias1, cur(0), stat, stat],
        out_specs=[cur(0), late, late, pl.BlockSpec((H, BLK, 2 * BLK), lambda b: (0, 0, 0))],
        out_shape=[jax.ShapeDtypeStruct((T, HD), BF16), f32, f32,
                   jax.ShapeDtypeStruct((H, BLK, 2 * BLK), F32)],
        scratch_shapes=[pltpu.VMEM((QB * BLK, HD), F32), pltpu.VMEM((QB * BLK, HD), F32)],
        compiler_params=_cp(("arbitrary",)),
    )(qa, ka, ka, va, va, bias, bias, do, lse, delta)


def _rel_grad(dbs, *, name):
    H = dbs[0].shape[1]
    n = len(dbs)
    assert H <= LANES
    bkt = jnp.asarray(_bucket_table())

    def body(*refs):
        bk = refs[n][...]
        o_ref = refs[n + 1]
        db = refs[0][...]
        for r in refs[1:n]:
            db = db + r[...]
        row = lax.broadcasted_iota(jnp.int32, (REL_BUCKETS, LANES), 0)
        col = lax.broadcasted_iota(jnp.int32, (REL_BUCKETS, LANES), 1)

        def bucket(c, acc):
            sel = bk == c
            for h in range(H):
                val = jnp.sum(jnp.where(sel, db[h], 0.0))
                acc = jnp.where((row == c) & (col == h), val, acc)
            return acc

        o_ref[...] = lax.fori_loop(0, REL_BUCKETS, bucket, jnp.zeros((REL_BUCKETS, LANES), F32))

    dspec = pl.BlockSpec((None, H, BLK, 2 * BLK), lambda g: (g, 0, 0, 0))
    return pl.pallas_call(
        body, name=name, grid=(N_GROUPS,),
        in_specs=[dspec] * n + [pl.BlockSpec((None, BLK, 2 * BLK), lambda g: (g, 0, 0))],
        out_specs=pl.BlockSpec((None, REL_BUCKETS, LANES), lambda g: (g, 0, 0)),
        out_shape=jax.ShapeDtypeStruct((N_GROUPS, REL_BUCKETS, LANES), F32),
        compiler_params=_cp(("parallel",)),
    )(*dbs, bkt)


def _to_group(a, d):
    T, C = a.shape
    return a if d == 1 else a.reshape(T // d, d, C).transpose(1, 0, 2).reshape(T, C)


def _from_group(a, d):
    T, C = a.shape
    return a if d == 1 else a.reshape(d, T // d, C).transpose(1, 0, 2).reshape(T, C)


def _rows_view(a):
    a2 = a.reshape(-1, a.shape[-1])
    R, C = a2.shape
    tr = _tile(R, max(SUBLANES, (1 << 18) // C), SUBLANES)
    return a2, R, C, tr


def _addn(xs, out_dtype, *, name):
    shape = xs[0].shape
    x2s = [_rows_view(x)[0] for x in xs]
    _, R, C, tr = _rows_view(xs[0])
    spec = pl.BlockSpec((tr, C), lambda i: (i, 0))

    def body(*refs):
        acc = refs[0][...].astype(F32)
        for r in refs[1:-1]:
            acc = acc + r[...].astype(F32)
        refs[-1][...] = acc.astype(out_dtype)

    out = pl.pallas_call(
        body, name=name, grid=(R // tr,),
        in_specs=[spec] * len(xs), out_specs=spec,
        out_shape=jax.ShapeDtypeStruct((R, C), out_dtype),
        compiler_params=_cp(("parallel",)),
    )(*x2s)
    return out.reshape(shape)


def _addn_grouped(parts, out_dtype, *, name):
    T, C = parts[0][0].shape
    tr = _tile(T, 512, SUBLANES)
    assert all(tr % d == 0 for _, d in parts)

    assert C % LANES == 0

    def body(*refs):
        o_ref, acc_ref = refs[-2], refs[-1]
        acc_ref[...] = jnp.zeros_like(acc_ref)
        for ref, (_, d) in zip(refs, parts):
            for r in range(d):
                for j in range(C // LANES):
                    acc_ref.at[j][pl.ds(r, tr // d, stride=d), :] += ref[r, :, j * LANES:(j + 1) * LANES]
        for j in range(C // LANES):
            o_ref[:, j * LANES:(j + 1) * LANES] = acc_ref[j].astype(out_dtype)

    return pl.pallas_call(
        body, name=name, grid=(T // tr,),
        in_specs=[pl.BlockSpec((d, tr // d, C), lambda i: (0, i, 0)) for _, d in parts],
        out_specs=pl.BlockSpec((tr, C), lambda i: (i, 0)),
        out_shape=jax.ShapeDtypeStruct((T, C), out_dtype),
        scratch_shapes=[pltpu.VMEM((C // LANES, tr, LANES), F32)],
        compiler_params=_cp(("parallel",)),
    )(*[a.reshape(d, T // d, C) for a, d in parts])


def _slot_sum(b_ref, own, my_id):
    acc = None
    for s in range(b_ref.shape[0]):
        term = jnp.where(my_id == s, own, b_ref[s].astype(F32))
        acc = term if acc is None else acc + term
    return acc


def _part_geometry(shape, axis):
    _, R, C = shape
    if axis == COL:
        return R // 2, C // N_CHIPS
    return R // (2 * N_CHIPS), C


def _sum_parts(buf, grad, axis, where, *, name):
    n, L, Rp, Cp = buf.shape
    assert (Rp, Cp) == _part_geometry(grad.shape, axis)
    tr = _tile(Rp, max(2 * SUBLANES, (1 << 17) // Cp), 2 * SUBLANES)
    nr = Rp // tr

    def own_map(l, i, w):
        if axis == COL:
            return (l, w[1] * nr + i, w[0])
        return (l, (2 * w[0] + w[1]) * nr + i, 0)

    def body(w_ref, b_ref, g_ref, o_ref):
        o_ref[...] = _slot_sum(b_ref, g_ref[...].astype(F32), w_ref[2])

    return pl.pallas_call(
        body, name=name,
        grid_spec=pltpu.PrefetchScalarGridSpec(
            num_scalar_prefetch=1, grid=(L, nr),
            in_specs=[pl.BlockSpec((n, None, tr, Cp), lambda l, i, w: (0, l, i, 0)),
                      pl.BlockSpec((None, tr, Cp), own_map)],
            out_specs=pl.BlockSpec((None, None, tr, Cp), lambda l, i, w: (l, w[1], i, 0))),
        out_shape=jax.ShapeDtypeStruct((L, 2, Rp, Cp), F32),
        compiler_params=_cp(("parallel", "parallel")),
    )(where, buf, grad)


def _sum_small(buf, own, where, *, name):
    n, R, C = buf.shape
    tr = _tile(R, 512, SUBLANES)

    def body(w_ref, b_ref, g_ref, o_ref):
        o_ref[...] = _slot_sum(b_ref, g_ref[...], w_ref[2])

    return pl.pallas_call(
        body, name=name,
        grid_spec=pltpu.PrefetchScalarGridSpec(
            num_scalar_prefetch=1, grid=(R // tr,),
            in_specs=[pl.BlockSpec((n, tr, C), lambda i, w: (0, i, 0)),
                      pl.BlockSpec((tr, C), lambda i, w: (i, 0))],
            out_specs=pl.BlockSpec((tr, C), lambda i, w: (i, 0))),
        out_shape=jax.ShapeDtypeStruct((R, C), F32),
        compiler_params=_cp(("parallel",)),
    )(where, buf, own)


def _place_shard(shard, layer, axis, where, out_dtype, *, name, after=None):
    _, R, C = shard.shape
    tr = _tile(R, max(2 * SUBLANES, (1 << 18) // C), 2 * SUBLANES)
    nr = R // tr
    if axis == COL:
        full, out_map = (1, R, C * N_CHIPS), (lambda i, w: (0, i, w[0]))
    else:
        full, out_map = (1, R * N_CHIPS, C), (lambda i, w: (0, w[0] * nr + i, 0))

    def body(w_ref, x_ref, *rest):
        rest[-1][...] = x_ref[...].astype(out_dtype)

    extra = [] if after is None else [after]
    return pl.pallas_call(
        body, name=name,
        grid_spec=pltpu.PrefetchScalarGridSpec(
            num_scalar_prefetch=1, grid=(nr,),
            in_specs=[pl.BlockSpec((None, tr, C), lambda i, w: (layer, i, 0))]
            + [pl.BlockSpec(memory_space=pl.ANY)] * len(extra),
            out_specs=pl.BlockSpec((None, tr, C), out_map)),
        out_shape=jax.ShapeDtypeStruct(full, out_dtype),
        compiler_params=_cp(("parallel",)),
    )(where, shard, *extra)


def _loss_grad(y, tgt, *, name):
    T, D = y.shape
    tr = _tile(T, 512, SUBLANES)
    row = pl.BlockSpec((tr, D), lambda i: (i, 0))

    def body(y_ref, t_ref, dy_ref, l_ref, acc_ref):
        i = pl.program_id(0)

        @pl.when(i == 0)
        def _():
            acc_ref[...] = jnp.zeros_like(acc_ref)

        err = y_ref[...] - t_ref[...]
        dy_ref[...] = err * (1.0 / D)
        acc_ref[...] += jnp.sum(err * err, axis=0, keepdims=True)

        @pl.when(i == T // tr - 1)
        def _():
            tot = jnp.sum(acc_ref[...], axis=1, keepdims=True) * (0.5 / D)
            l_ref[...] = jnp.broadcast_to(tot, (1, LANES))

    return pl.pallas_call(
        body, name=name, grid=(T // tr,),
        in_specs=[row, row],
        out_specs=[row, pl.BlockSpec((1, LANES), lambda i: (0, 0))],
        out_shape=[jax.ShapeDtypeStruct((T, D), F32), jax.ShapeDtypeStruct((1, LANES), F32)],
        scratch_shapes=[pltpu.VMEM((1, D), F32)],
        compiler_params=_cp(("arbitrary",)),
    )(y, tgt)


def _adamw(w, g, m, v, *, name):
    shape = w.shape
    w2, R, C, tr = _rows_view(w)
    spec = pl.BlockSpec((tr, C), lambda i: (i, 0))

    def body(w_ref, g_ref, m_ref, v_ref, d_ref, nm_ref, nv_ref):
        gv = g_ref[...]
        nm = ADAM_B1 * m_ref[...] + (1.0 - ADAM_B1) * gv
        nv = ADAM_B2 * v_ref[...] + (1.0 - ADAM_B2) * (gv * gv)
        m_hat = nm / (1.0 - ADAM_B1 ** ADAM_STEP)
        v_hat = nv / (1.0 - ADAM_B2 ** ADAM_STEP)
        d_ref[...] = -ADAM_LR * (m_hat / (jnp.sqrt(v_hat) + ADAM_EPS) + ADAM_WD * w_ref[...])
        nm_ref[...] = nm
        nv_ref[...] = nv

    shp = jax.ShapeDtypeStruct((R, C), F32)
    outs = pl.pallas_call(
        body, name=name, grid=(R // tr,),
        in_specs=[spec] * 4, out_specs=[spec] * 3, out_shape=[shp] * 3,
        compiler_params=_cp(("parallel",)),
    )(w2, g.reshape(R, C), m.reshape(R, C), v.reshape(R, C))
    return tuple(o.reshape(shape) for o in outs)


def _pack(arrs):
    flat = jnp.concatenate([a.reshape(-1).astype(F32) for a in arrs])
    unit = SUBLANES * LANES
    pad = (-flat.shape[0]) % unit
    return jnp.pad(flat, (0, pad)).reshape(-1, LANES)


def _unpack(buf, shapes):
    flat = buf.reshape(-1)
    out, off = [], 0
    for s in shapes:
        n = int(np.prod(s))
        out.append(flat[off:off + n].reshape(s))
        off += n
    return out


def _me():
    return lax.axis_index("x"), lax.axis_index("y"), lax.axis_index("c")


def _flip(pos, k):
    x, y, c = pos
    return (1 - x if k & 4 else x, 1 - y if k & 2 else y, 1 - c if k & 1 else c)


HBM = pl.BlockSpec(memory_space=pltpu.HBM)

COL, ROW = -1, -2


def _shard_window(ref, axis, j, n):
    start = pl.multiple_of(j * n, n)
    if axis == COL:
        return ref.at[:, :, pl.ds(start, n)]
    return ref.at[:, pl.ds(start, n), :]


GATHER_PEERS = (2, 4, 6)
SEM = pl.BlockSpec(memory_space=pltpu.SEMAPHORE)
EFFECT = pltpu.SideEffectType.DATAFLOW_SIDE_EFFECTING


def _gather_copy(buf, axis, i, pi, chip, me, send_sems, recv_sems):
    win = _shard_window(buf, axis, chip, buf.shape[axis] // N_CHIPS)
    k = len(GATHER_PEERS) * i + pi
    return pltpu.make_async_remote_copy(
        src_ref=win, dst_ref=win, send_sem=send_sems.at[k], recv_sem=recv_sems.at[k],
        device_id=_flip(me, GATHER_PEERS[pi]), device_id_type=MESH)


def _gather_start(fulls, axes, *, name):
    n = len(fulls)
    n_sem = len(GATHER_PEERS) * n

    def body(*refs):
        send_sems, recv_sems = refs[n], refs[n + 1]
        bufs = refs[n + 2:2 * n + 2]
        token = refs[2 * n + 2]
        me = _me()
        for i in range(n):
            for pi in range(len(GATHER_PEERS)):
                _gather_copy(bufs[i], axes[i], i, pi, 2 * me[0] + me[1], me, send_sems, recv_sems).start()
        token[...] = jnp.zeros_like(token)

    outs = pl.pallas_call(
        body, name=name,
        in_specs=[HBM] * n, out_specs=[SEM, SEM] + [HBM] * n + [pl.BlockSpec(memory_space=pltpu.VMEM)],
        out_shape=[pltpu.SemaphoreType.DMA((n_sem,)), pltpu.SemaphoreType.DMA((n_sem,))]
        + [pltpu.HBM(f.shape, f.dtype) for f in fulls] + [jax.ShapeDtypeStruct((SUBLANES, LANES), F32)],
        input_output_aliases={i: 2 + i for i in range(n)},
        compiler_params=pltpu.CompilerParams(has_side_effects=EFFECT),
    )(*[pltpu.with_memory_space_constraint(f, pltpu.HBM) for f in fulls])
    return outs[0], outs[1], list(outs[2:2 + n]), outs[-1]


def _gather_wait(send_sems, recv_sems, bufs, axes, idxs, after, *, name):
    m = len(bufs)

    def body(*refs):
        ss, rs = refs[m], refs[m + 1]
        outs = refs[m + 3:]
        me = _me()
        for t, i in enumerate(idxs):
            for pi, k in enumerate(GATHER_PEERS):
                px, py, _ = _flip(me, k)
                _gather_copy(outs[t], axes[t], i, pi, 2 * me[0] + me[1], me, ss, rs).wait_send()
                _gather_copy(outs[t], axes[t], i, pi, 2 * px + py, me, ss, rs).wait_recv()

    return pl.pallas_call(
        body, name=name,
        in_specs=[HBM] * m + [SEM, SEM, pl.BlockSpec(memory_space=pl.ANY)], out_specs=[HBM] * m,
        out_shape=[pltpu.HBM(b.shape, b.dtype) for b in bufs],
        input_output_aliases={t: t for t in range(m)},
        compiler_params=pltpu.CompilerParams(has_side_effects=EFFECT),
    )(*bufs, send_sems, recv_sems, after)


def _grad_part(ref, layer, axis, pos):
    px, py, pc = pos
    Rp, Cp = _part_geometry(ref.shape, axis)
    chip = 2 * px + py
    if axis == COL:
        return ref.at[layer, pl.ds(pl.multiple_of(pc * Rp, Rp), Rp), pl.ds(pl.multiple_of(chip * Cp, Cp), Cp)]
    return ref.at[layer, pl.ds(pl.multiple_of((2 * chip + pc) * Rp, Rp), Rp), :]


def _scatter_copy(grad, slots, layer, axis, i, k, src_pos, dst_slot, me, send_sems, recv_sems):
    sem = (N_DEV - 1) * i + k - 1
    return pltpu.make_async_remote_copy(
        src_ref=_grad_part(grad, layer, axis, src_pos), dst_ref=slots.at[dst_slot, layer],
        send_sem=send_sems.at[sem], recv_sem=recv_sems.at[sem],
        device_id=_flip(me, k), device_id_type=MESH)


def _scatter_start(grads, slots, layers, axes, *, name):
    n = len(grads)
    n_sem = (N_DEV - 1) * n

    def body(*refs):
        send_sems, recv_sems = refs[2 * n], refs[2 * n + 1]
        g_refs, s_refs = refs[2 * n + 2:3 * n + 2], refs[3 * n + 2:4 * n + 2]
        token = refs[4 * n + 2]
        me = _me()
        my_id = 4 * me[0] + 2 * me[1] + me[2]
        for i in range(n):
            for k in range(1, N_DEV):
                _scatter_copy(g_refs[i], s_refs[i], layers[i], axes[i], i, k, _flip(me, k), my_id, me,
                              send_sems, recv_sems).start()
        token[...] = jnp.zeros_like(token)

    arrays = list(grads) + list(slots)
    outs = pl.pallas_call(
        body, name=name,
        in_specs=[HBM] * (2 * n),
        out_specs=[SEM, SEM] + [HBM] * (2 * n) + [pl.BlockSpec(memory_space=pltpu.VMEM)],
        out_shape=[pltpu.SemaphoreType.DMA((n_sem,)), pltpu.SemaphoreType.DMA((n_sem,))]
        + [pltpu.HBM(a.shape, a.dtype) for a in arrays] + [jax.ShapeDtypeStruct((SUBLANES, LANES), F32)],
        input_output_aliases={i: 2 + i for i in range(2 * n)},
        compiler_params=pltpu.CompilerParams(has_side_effects=EFFECT),
    )(*[pltpu.with_memory_space_constraint(a, pltpu.HBM) for a in arrays])
    return outs[0], outs[1], list(outs[2:2 + n]), list(outs[2 + n:2 + 2 * n]), outs[-1]


def _scatter_wait(sems, grads, slots, axes, plan, after, *, name):
    n = len(grads)
    flat_sems = [s for pair in sems for s in pair]

    def body(*refs):
        sem_refs = refs[2 * n:2 * n + len(flat_sems)]
        first_out = 2 * n + len(flat_sems) + 1
        g_refs, s_refs = refs[first_out:first_out + n], refs[first_out + n:]
        me = _me()
        for j, items in enumerate(plan):
            ss, rs = sem_refs[2 * j], sem_refs[2 * j + 1]
            for i, (a, layer) in enumerate(items):
                for k in range(1, N_DEV):
                    px, py, pc = _flip(me, k)
                    cp = _scatter_copy(g_refs[a], s_refs[a], layer, axes[a], i, k, me, 4 * px + 2 * py + pc, me, ss, rs)
                    cp.wait_send()
                    cp.wait_recv()

    arrays = list(grads) + list(slots)
    outs = pl.pallas_call(
        body, name=name,
        in_specs=[HBM] * (2 * n) + [SEM] * len(flat_sems) + [pl.BlockSpec(memory_space=pl.ANY)],
        out_specs=[HBM] * (2 * n),
        out_shape=[pltpu.HBM(a.shape, a.dtype) for a in arrays],
        input_output_aliases={i: i for i in range(2 * n)},
        compiler_params=pltpu.CompilerParams(has_side_effects=EFFECT),
    )(*arrays, *flat_sems, after)
    return list(outs[:n]), list(outs[n:])


def _small_copy(src, dst, k, slot, me, send_sems, recv_sems):
    return pltpu.make_async_remote_copy(
        src_ref=src, dst_ref=dst.at[slot], send_sem=send_sems.at[k - 1], recv_sem=recv_sems.at[k - 1],
        device_id=_flip(me, k), device_id_type=MESH)


def _small_start(small, slots, after, *, name):
    def body(src_in, dst_in, after_ref, send_sems, recv_sems, src, dst):
        me = _me()
        for k in range(1, N_DEV):
            _small_copy(src, dst, k, 4 * me[0] + 2 * me[1] + me[2], me, send_sems, recv_sems).start()

    arrays = [small, slots]
    return pl.pallas_call(
        body, name=name, in_specs=[HBM, HBM, pl.BlockSpec(memory_space=pl.ANY)], out_specs=[SEM, SEM, HBM, HBM],
        out_shape=[pltpu.SemaphoreType.DMA((N_DEV - 1,)), pltpu.SemaphoreType.DMA((N_DEV - 1,))]
        + [pltpu.HBM(a.shape, a.dtype) for a in arrays],
        input_output_aliases={0: 2, 1: 3},
        compiler_params=pltpu.CompilerParams(has_side_effects=EFFECT),
    )(*[pltpu.with_memory_space_constraint(a, pltpu.HBM) for a in arrays], after)


def _small_wait(send_sems, recv_sems, small, slots, after, *, name):
    def body(src_in, dst_in, ss, rs, after_ref, src, dst):
        me = _me()
        for k in range(1, N_DEV):
            px, py, pc = _flip(me, k)
            cp = _small_copy(src, dst, k, 4 * px + 2 * py + pc, me, ss, rs)
            cp.wait_send()
            cp.wait_recv()

    return pl.pallas_call(
        body, name=name, in_specs=[HBM, HBM, SEM, SEM, pl.BlockSpec(memory_space=pl.ANY)], out_specs=[HBM, HBM],
        out_shape=[pltpu.HBM(small.shape, small.dtype), pltpu.HBM(slots.shape, slots.dtype)],
        input_output_aliases={0: 0, 1: 1},
        compiler_params=pltpu.CompilerParams(has_side_effects=EFFECT),
    )(small, slots, send_sems, recv_sems, after)


def _pair_gather(halves, *, name):
    n = len(halves)

    def body(*refs):
        bufs = refs[n:2 * n]
        send_sems, recv_sems = refs[2 * n:]
        me = _me()
        c = me[2]
        sib = _flip(me, 1)

        def copy(i, half):
            return pltpu.make_async_remote_copy(
                src_ref=bufs[i].at[:, half], dst_ref=bufs[i].at[:, half], send_sem=send_sems.at[i],
                recv_sem=recv_sems.at[i], device_id=sib, device_id_type=MESH)

        sends = [copy(i, c) for i in range(n)]
        for cp in sends:
            cp.start()
        for i in range(n):
            copy(i, 1 - c).wait_recv()
        for cp in sends:
            cp.wait_send()

    return pl.pallas_call(
        body, name=name,
        in_specs=[HBM] * n, out_specs=[HBM] * n,
        out_shape=[jax.ShapeDtypeStruct(h.shape, h.dtype) for h in halves],
        input_output_aliases={i: i for i in range(n)},
        scratch_shapes=[pltpu.SemaphoreType.DMA((n,)), pltpu.SemaphoreType.DMA((n,))],
    )(*halves)


def _local_step(x, tgt, P, fetch, ship, n_layers):
    T, D = x.shape
    P = dict(P)
    W = {}
    depth = P["ffn_conv_b"].shape[0]
    n_a = P["a_w_s"].shape[0]
    alpha = (2 * depth) ** 0.25
    H = P["rel_table"].shape[1] // N_GROUPS
    HD = H * HEAD_DIM
    dils = [d for _, d in DILATED_GROUPS]
    assert T % (dils[-1] * BLK) == 0

    def arrive(stage, after):
        w_new, p_new = fetch(stage, after)
        W.update(w_new)
        P.update(p_new)

    bias = _band_bias(P["rel_table"], H, name="band_bias")
    saved = []
    xf, xb = x, x.astype(BF16)
    kg = vg = None
    for i in range(depth):
        s = {"x": xf, "xb": xb}
        arrive(f"{i}a", xf)
        if i < n_a:
            s["zp"] = _mm(xb, W["a_w_in", i], out_dtype=BF16, name=f"a{i}_in")
            arrive(f"{i}b", s["zp"])
            s["y"] = _sgu_fwd(s["zp"], P["a_w_s"][i], P["a_b_s"][i], P["a_ln_g"][i], P["a_ln_b"][i], name=f"a{i}_sgu")
            s["h"] = _mm(s["y"], W["a_w_out", i], name=f"a{i}_out")
        else:
            j = i - n_a
            if j == 0:
                kv = _mm(xb, W["kv_w", 0], out_dtype=BF16, name="kv_proj")
                kg = [(kv, 0) if d == 1 else (_to_group(kv[:, :HD], d), 0) for d in dils]
                vg = [(kv, 1) if d == 1 else (_to_group(kv[:, HD:], d), 0) for d in dils]
            q = _mm(xb, W["b_w_q", j], out_dtype=BF16, name=f"b{j}_q")
            s["qg"] = [(q, g) if d == 1 else (_to_group(q[:, g * HD:(g + 1) * HD], d), 0) for g, d in enumerate(dils)]
            os, ls = [], []
            for g, d in enumerate(dils):
                o_g, l_g = _attn_fwd(s["qg"][g], kg[g], vg[g], bias, g, d, name=f"b{j}_attn{g}")
                os.append(o_g)
                ls.append(l_g)
            s["oc"], s["ob"], s["lse"] = _attn_combine(os, ls, dils, name=f"b{j}_comb")
            arrive(f"{i}b", s["ob"])
            s["h"] = _mm(s["ob"], W["b_w_o", j], name=f"b{j}_o")
        s["x1"], s["x1b"] = _add_ln(xf, s["h"], P["ln_g"][i, 0], P["ln_b"][i, 0], alpha, name=f"l{i}_ln1")
        arrive(f"{i}c", s["x1b"])
        s["hup"] = _mm(s["x1b"], W["ffn_w_up", i], out_dtype=BF16, name=f"l{i}_up")
        arrive(f"{i}d", s["hup"])
        s["act"], s["ca"], s["cg"] = _convglu_fwd(s["hup"], P["ffn_conv_w"][i], P["ffn_conv_b"][i], name=f"l{i}_glu")
        s["f"] = _mm(s["act"], W["ffn_w_down", i], name=f"l{i}_down")
        xf, xb = _add_ln(s["x1"], s["f"], P["ln_g"][i, 1], P["ln_b"][i, 1], alpha, name=f"l{i}_ln2")
        saved.append(s)

    G, loss_row = _loss_grad(xf, tgt, name="loss")

    gw = {}

    def dw(key, layer, a, b, name, **kw):
        gw[key] = _mm(a, b, ta=True, out_dtype=BF16, out_into=(gw.get(key), n_layers[key], layer), name=name, **kw)

    gp = {k: [None] * n_a for k in ("a_ln_g", "a_ln_b", "a_w_s", "a_b_s")}
    gp.update({k: [None] * depth for k in ("ffn_conv_w", "ffn_conv_b", "ln_g", "ln_b")})
    dk_parts, dv_parts, dbias_parts = [], [], []
    for i in reversed(range(depth)):
        s = saved[i]
        dr2, dr2b, dg2, db2 = _ln_bwd(G, s["x1"], s["f"], P["ln_g"][i, 1], alpha, name=f"l{i}_ln2_bwd")
        dw("ffn_w_down", i, s["act"], dr2b, f"l{i}_down_dw")
        dact = _mm(dr2b, W["ffn_w_down", i], tb=True, out_dtype=BF16, name=f"l{i}_down_dx")
        dhup, dcw, dcb = _convglu_bwd(s["hup"], s["ca"], s["cg"], dact, P["ffn_conv_w"][i], name=f"l{i}_glu_bwd")
        gp["ffn_conv_w"][i] = dcw.transpose(1, 0, 2).reshape(dcw.shape[1], -1)
        gp["ffn_conv_b"][i] = dcb.reshape(-1)
        dw("ffn_w_up", i, s["x1b"], dhup, f"l{i}_up_dw", b_halves=True)
        token = ship(gw, [("ffn_w_down", i), ("ffn_w_up", i)])
        G1 = _mm(dhup, W["ffn_w_up", i], tb=True, a_halves=True, add=dr2, add_scale=alpha, after=token,
                 name=f"l{i}_up_dx")
        dr1, dr1b, dg1, db1 = _ln_bwd(G1, s["x"], s["h"], P["ln_g"][i, 0], alpha, name=f"l{i}_ln1_bwd")
        gp["ln_g"][i] = jnp.concatenate([dg1, dg2], axis=0)
        gp["ln_b"][i] = jnp.concatenate([db1, db2], axis=0)
        if i < n_a:
            dw("a_w_out", i, s["y"], dr1b, f"a{i}_out_dw")
            token = ship(gw, [("a_w_out", i)])
            dy = _mm(dr1b, W["a_w_out", i], tb=True, out_dtype=BF16, after=token, name=f"a{i}_out_dx")
            dzp, dws, dbs, dlg, dlb = _sgu_bwd(s["zp"], dy, P["a_w_s"][i], P["a_b_s"][i], P["a_ln_g"][i],
                                               P["a_ln_b"][i], name=f"a{i}_sgu_bwd")
            gp["a_w_s"][i], gp["a_b_s"][i] = dws, dbs[:, :dws.shape[0]].T
            gp["a_ln_g"][i], gp["a_ln_b"][i] = dlg[0], dlb[0]
            dw("a_w_in", i, s["xb"], dzp, f"a{i}_in_dw")
            token = ship(gw, [("a_w_in", i)])
            G = _mm(dzp, W["a_w_in", i], tb=True, add=dr1, add_scale=alpha, after=token, name=f"a{i}_in_dx")
        else:
            j = i - n_a
            dw("b_w_o", j, s["ob"], dr1b, f"b{j}_o_dw")
            do_tok = _mm(dr1b, W["b_w_o", j], tb=True, name=f"b{j}_o_dx")
            deltas, dobs = _attn_bwd_prep(do_tok, s["oc"], dils, name=f"b{j}_prep")
            dqs, dbs = [], []
            for g, d in enumerate(dils):
                dq, dk, dv, db = _attn_bwd(s["qg"][g], kg[g], vg[g], bias, g, d, dobs[g], s["lse"][g], deltas[g],
                                           name=f"b{j}_attn{g}_bwd")
                dqs.append(_from_group(dq, d))
                dk_parts.append((dk, d))
                dv_parts.append((dv, d))
                dbs.append(db)
            dbias_parts.append(jnp.stack(dbs))
            dq_tok = jnp.concatenate(dqs, axis=1)
            dw("b_w_q", j, s["xb"], dq_tok, f"b{j}_q_dw")
            token = ship(gw, [("b_w_o", j), ("b_w_q", j)])
            G = _mm(dq_tok, W["b_w_q", j], tb=True, add=dr1, add_scale=alpha, after=token, name=f"b{j}_q_dx")
            if j == 0:
                dkv = jnp.concatenate([_addn_grouped(dk_parts, BF16, name="dk_sum"),
                                       _addn_grouped(dv_parts, BF16, name="dv_sum")], axis=1)
                dw("kv_w", 0, s["xb"], dkv, "kv_dw")
                token = ship(gw, [("kv_w", 0)])
                G = _mm(dkv, W["kv_w", 0], tb=True, add=G, add_scale=1.0, after=token, name="kv_dx")
    rel = _rel_grad(dbias_parts, name="rel_grad")
    grel = rel[:, :, :H].transpose(1, 0, 2).reshape(REL_BUCKETS, N_GROUPS * H)
    gsmall = {k: jnp.stack(v) for k, v in gp.items()}
    gsmall["rel_table"] = grel
    return loss_row, G, gw, gsmall


BIG = (("a_w_in", COL), ("a_w_out", ROW), ("kv_w", ROW), ("b_w_q", COL), ("b_w_o", COL),
       ("ffn_w_up", COL), ("ffn_w_down", ROW))
SMALL_SHARDED = ("a_ln_g", "a_ln_b", "ffn_conv_w", "ln_g", "ln_b")
SMALL_REPLICATED = ("a_w_s", "a_b_s", "rel_table", "ffn_conv_b")
WEIGHTS = ("a_w_in", "a_ln_g", "a_ln_b", "a_w_s", "a_b_s", "a_w_out", "kv_w", "b_w_q", "b_w_o", "rel_table",
           "ffn_w_up", "ffn_conv_w", "ffn_conv_b", "ffn_w_down", "ln_g", "ln_b")


def kernel(x, a_w_in, a_ln_g, a_ln_b, a_w_s, a_b_s, a_w_out, kv_w, b_w_q, b_w_o, rel_table, ffn_w_up, ffn_conv_w, ffn_conv_b, ffn_w_down, ln_g, ln_b, loss_target, m_a_w_in, m_a_ln_g, m_a_ln_b, m_a_w_s, m_a_b_s, m_a_w_out, m_kv_w, m_b_w_q, m_b_w_o, m_rel_table, m_ffn_w_up, m_ffn_conv_w, m_ffn_conv_b, m_ffn_w_down, m_ln_g, m_ln_b, v_a_w_in, v_a_ln_g, v_a_ln_b, v_a_w_s, v_a_b_s, v_a_w_out, v_kv_w, v_b_w_q, v_b_w_o, v_rel_table, v_ffn_w_up, v_ffn_conv_w, v_ffn_conv_b, v_ffn_w_down, v_ln_g, v_ln_b):
    w = dict(a_w_in=a_w_in, a_ln_g=a_ln_g, a_ln_b=a_ln_b, a_w_s=a_w_s, a_b_s=a_b_s, a_w_out=a_w_out, kv_w=kv_w,
             b_w_q=b_w_q, b_w_o=b_w_o, rel_table=rel_table, ffn_w_up=ffn_w_up, ffn_conv_w=ffn_conv_w,
             ffn_conv_b=ffn_conv_b, ffn_w_down=ffn_w_down, ln_g=ln_g, ln_b=ln_b)
    m = dict(a_w_in=m_a_w_in, a_ln_g=m_a_ln_g, a_ln_b=m_a_ln_b, a_w_s=m_a_w_s, a_b_s=m_a_b_s, a_w_out=m_a_w_out,
             kv_w=m_kv_w, b_w_q=m_b_w_q, b_w_o=m_b_w_o, rel_table=m_rel_table, ffn_w_up=m_ffn_w_up,
             ffn_conv_w=m_ffn_conv_w, ffn_conv_b=m_ffn_conv_b, ffn_w_down=m_ffn_w_down, ln_g=m_ln_g, ln_b=m_ln_b)
    v = dict(a_w_in=v_a_w_in, a_ln_g=v_a_ln_g, a_ln_b=v_a_ln_b, a_w_s=v_a_w_s, a_b_s=v_a_b_s, a_w_out=v_a_w_out,
             kv_w=v_kv_w, b_w_q=v_b_w_q, b_w_o=v_b_w_o, rel_table=v_rel_table, ffn_w_up=v_ffn_w_up,
             ffn_conv_w=v_ffn_conv_w, ffn_conv_b=v_ffn_conv_b, ffn_w_down=v_ffn_w_down, ln_g=v_ln_g, ln_b=v_ln_b)
    chip = 2 * lax.axis_index("x") + lax.axis_index("y")

    big_names = [n for n, _ in BIG]
    big_axes = [a for _, a in BIG]
    where = jnp.stack([chip, lax.axis_index("c"), 2 * chip + lax.axis_index("c")]).astype(jnp.int32)
    shards = [w[n].reshape((1,) + w[n].shape) if w[n].ndim == 2 else w[n] for n in big_names]
    small_shapes = [w[n].shape for n in SMALL_SHARDED]
    small = _pack([w[n] for n in SMALL_SHARDED])
    shard_of = dict(zip(big_names, shards))
    axis_of = dict(BIG)
    n_layers = {n: s.shape[0] for n, s in shard_of.items()}
    n_a = w["a_w_s"].shape[0]
    stages = {}
    for i in range(w["ffn_w_up"].shape[0]):
        if i < n_a:
            stages[f"{i}a"], stages[f"{i}b"] = [("a_w_in", i)], [("a_w_out", i)]
        else:
            stages[f"{i}a"] = ([("kv_w", 0)] if i == n_a else []) + [("b_w_q", i - n_a)]
            stages[f"{i}b"] = [("b_w_o", i - n_a)]
        stages[f"{i}c"], stages[f"{i}d"] = [("ffn_w_up", i)], [("ffn_w_down", i)]
    stages["0a"].append(("small", 0))
    assert sum(len(st) for st in stages.values()) == sum(n_layers.values()) + 1
    flights, full_shape, all_started = {}, {}, None
    for batch, names in enumerate(([st for st in stages if st[0] == "0"], [st for st in stages if st[0] != "0"])):
        order = [key for st in names for key in stages[st]]
        placed, axes = [], []
        for n, l in order:
            if n == "small":
                placed.append(_place_shard(small[None], 0, ROW, where, F32, name="place_small"))
                axes.append(ROW)
            else:
                placed.append(_place_shard(shard_of[n], l, axis_of[n], where, BF16, name=f"place_{n}{l}",
                                           after=all_started))
                axes.append(axis_of[n])
                full_shape[n] = (n_layers[n],) + placed[-1].shape[1:]
        send_sems, recv_sems, flying, all_started = _gather_start(placed, axes, name=f"gather_start{batch}")
        for st in names:
            flights[st] = (send_sems, recv_sems, flying, axes, order)

    def fetch(stage, after):
        if stage not in stages:
            return {}, {}
        send_sems, recv_sems, flying, axes, order = flights[stage]
        if stage == "0a":
            after = all_started
        idxs = [order.index(key) for key in stages[stage]]
        landed = _gather_wait(send_sems, recv_sems, [flying[i] for i in idxs], [axes[i] for i in idxs], idxs, after,
                              name=f"gather_wait_{stage}")
        W_new, P_new = {}, {}
        for key, arr in zip(stages[stage], landed):
            if key[0] == "small":
                small_all = arr.reshape((N_CHIPS,) + small.shape)
                per_chip = [_unpack(small_all[j], small_shapes) for j in range(N_CHIPS)]
                for i, n in enumerate(SMALL_SHARDED):
                    P_new[n] = jnp.concatenate([per_chip[j][i] for j in range(N_CHIPS)], axis=-1)
            else:
                W_new[key] = arr.reshape(arr.shape[1:])
        return W_new, P_new

    slots ={n: lax.empty((N_DEV, n_layers[n]) + _part_geometry(full_shape[n], axis_of[n]), BF16) for n in big_names}
    sems, plan, tokens = [], [], []

    def ship(gw, items):
        names = [n for n, _ in items]
        send, recv, grads, landing, token = _scatter_start(
            [gw[n] for n in names], [slots[n] for n in names], [l for _, l in items], [axis_of[n] for n in names],
            name="scatter_start_" + "_".join(f"{n}{l}" for n, l in items))
        for n, g, s in zip(names, grads, landing):
            gw[n], slots[n] = g, s
        sems.append((send, recv))
        plan.append([(big_names.index(n), l) for n, l in items])
        tokens.append(token)
        return token

    loss_row, grad_x, gw, gsmall = _local_step(x[0], loss_target[0], {n: w[n] for n in SMALL_REPLICATED}, fetch,
                                               ship, n_layers)

    small_names = list(SMALL_SHARDED) + list(SMALL_REPLICATED)
    small_pack = _pack([gsmall[n] for n in small_names] + [loss_row[0, :1]])
    small_sems = _small_start(small_pack, lax.empty((N_DEV,) + small_pack.shape, F32), tokens[-1],
                              name="small_start")
    own, landed = _scatter_wait(sems, [gw[n] for n in big_names], [slots[n] for n in big_names], big_axes, plan,
                                grad_x, name="scatter_wait")
    sums = [_sum_parts(b, g, ax, where, name=f"sum_{n}") for (n, ax), b, g in zip(BIG, landed, own)]
    pairs = _pair_gather(sums, name="pair_gather")
    grad = {n: p.reshape(w[n].shape) for n, p in zip(big_names, pairs)}

    delta, new_m, new_v = {}, {}, {}
    for n, _ in BIG:
        delta[n], new_m[n], new_v[n] = _adamw(w[n], grad[n], m[n], v[n], name=f"adamw_{n}")
    small_own, small_slots = _small_wait(*small_sems, new_v[big_names[-1]], name="small_wait")
    small_sum = _sum_small(small_slots, small_own, where, name="sum_small")
    full_small = _unpack(small_sum, [gsmall[n].shape for n in small_names] + [(1,)])
    loss = full_small[-1][0]
    for n, gfull in zip(small_names, full_small):
        if n in SMALL_SHARDED:
            width = w[n].shape[-1]
            grad[n] = lax.dynamic_slice_in_dim(gfull, chip * width, width, axis=-1)
        else:
            grad[n] = gfull
    shapes = [w[n].shape for n in small_names]
    packed = _adamw(_pack([w[n] for n in small_names]), _pack([grad[n] for n in small_names]),
                    _pack([m[n] for n in small_names]), _pack([v[n] for n in small_names]), name="adamw_small")
    for out, res in zip((delta, new_m, new_v), packed):
        for n, a in zip(small_names, _unpack(res, shapes)):
            out[n] = a

    return (loss, grad_x[None], *[grad[n] for n in WEIGHTS], *[delta[n] for n in WEIGHTS],
            *[new_m[n] for n in WEIGHTS], *[new_v[n] for n in WEIGHTS])
```

```python
import functools
import math

import numpy as np
import jax
import jax.numpy as jnp
from jax import lax
from jax.experimental import pallas as pl
from jax.experimental.pallas import tpu as pltpu

F32 = jnp.float32
BF16 = jnp.bfloat16
MESH = pl.DeviceIdType.MESH

CHUNK = 128
HEAD_DIM = 64
DILATED_GROUPS = ((128, 1), (512, 4), (2048, 16))
N_GROUPS = len(DILATED_GROUPS)
BLK = 128
REL_BUCKETS = 32
REL_MAX_DIST = 2048
LN_EPS = 1e-5
NEG = -1e30
ADAM_LR = 0.001
ADAM_B1 = 0.9
ADAM_B2 = 0.999
ADAM_EPS = 1e-08
ADAM_WD = 0.01
ADAM_STEP = 10

N_CHIPS = 4
N_DEV = 8
LANES = 128
SUBLANES = 8
VMEM_LIMIT = 48 * 1024 * 1024

_SQRT_HALF = 0.7071067811865476
_INV_SQRT_2PI = 0.3989422804014327


def _cp(sem=None, vmem=VMEM_LIMIT):
    return pltpu.CompilerParams(dimension_semantics=sem, vmem_limit_bytes=vmem)


def _tile(dim, target, align=LANES):
    if dim <= target:
        return dim
    t = (target // align) * align
    while t >= align:
        if dim % t == 0:
            return t
        t -= align
    return dim


def _gelu(x):
    return 0.5 * x * (1.0 + lax.erf(x * _SQRT_HALF))


def _gelu_grad(x):
    return 0.5 * (1.0 + lax.erf(x * _SQRT_HALF)) + x * (_INV_SQRT_2PI * jnp.exp(-0.5 * x * x))


def _dot(a, b, ca, cb):
    return lax.dot_general(a, b, (((ca,), (cb,)), ((), ())), preferred_element_type=F32)


def _mn_tile(dim):
    return max(_tile(dim, 1024), _tile(dim, 1408))


def _mm(a, b, *, name, ta=False, tb=False, out_dtype=F32, tiles=None,
        a_lead=(), b_lead=(), a_halves=False, b_halves=False, add=None, add_scale=1.0, out_into=None, after=None):
    a2, b2 = a.shape[-2:], b.shape[-2:]
    M, K = (a2[1], a2[0]) if ta else a2
    Kb, N = (b2[1], b2[0]) if tb else b2
    if a_halves:
        K = 2 * K
    if b_halves:
        N = 2 * N
    assert K == Kb, (name, a.shape, b.shape)
    if tiles is None:
        k_target = 1408 if a_halves else (K if K <= 2816 else 2048)
        tm, tn = _mn_tile(M), _mn_tile(N // 2 if b_halves else N)
        tk = _tile(K // 2 if a_halves else K, k_target)
    else:
        tm = _tile(M, tiles[0])
        tn = _tile(N // 2 if b_halves else N, tiles[1])
        tk = _tile(K // 2 if a_halves else K, tiles[2])
    nm, nn, nk = M // tm, N // tn, K // tk
    nkh, nnh = nk // 2, nn // 2

    def lead(idx, rest):
        return tuple(idx) + tuple(rest)

    sq_a = (None,) * (a.ndim - 2)
    sq_b = (None,) * (b.ndim - 2)
    if a_halves:
        assert not ta and a.ndim == 3
        a_spec = pl.BlockSpec((None, tm, tk), lambda i, j, k: (k // nkh, i, k % nkh))
    elif ta:
        a_spec = pl.BlockSpec(sq_a + (tk, tm), lambda i, j, k: lead(a_lead, (k, i)))
    else:
        a_spec = pl.BlockSpec(sq_a + (tm, tk), lambda i, j, k: lead(a_lead, (i, k)))
    if b_halves:
        assert not tb and b.ndim == 3
        b_spec = pl.BlockSpec((None, tk, tn), lambda i, j, k: (j // nnh, k, j % nnh))
    elif tb:
        b_spec = pl.BlockSpec(sq_b + (tn, tk), lambda i, j, k: lead(b_lead, (j, k)))
    else:
        b_spec = pl.BlockSpec(sq_b + (tk, tn), lambda i, j, k: lead(b_lead, (k, j)))
    mn_spec = pl.BlockSpec((tm, tn), lambda i, j, k: (i, j))
    in_specs = [a_spec, b_spec]
    args = [a, b]
    if add is not None:
        in_specs.append(mn_spec)
        args.append(add)
    aliases = {}
    if out_into is None:
        o_spec, out_shape = mn_spec, jax.ShapeDtypeStruct((M, N), out_dtype)
    else:
        buf, n_layers, layer = out_into
        o_spec = pl.BlockSpec((None, tm, tn), lambda i, j, k: (layer, i, j))
        out_shape = jax.ShapeDtypeStruct((n_layers, M, N), out_dtype)
        if buf is not None:
            aliases = {len(args): 0}
            in_specs.append(pl.BlockSpec(memory_space=pl.ANY))
            args.append(buf)
    if after is not None:
        in_specs.append(pl.BlockSpec(memory_space=pl.ANY))
        args.append(after)
    n_in = len(args)
    ca, cb = (0 if ta else 1), (1 if tb else 0)

    def body(*refs):
        a_ref, b_ref = refs[:2]
        add_ref = refs[2] if add is not None else None
        o_ref = refs[n_in]

        def finish(acc):
            if add_ref is not None:
                acc = acc + add_scale * add_ref[...]
            o_ref[...] = acc.astype(o_ref.dtype)

        if nk == 1:
            finish(_dot(a_ref[...], b_ref[...], ca, cb))
        else:
            acc_ref = refs[-1]
            k = pl.program_id(2)

            @pl.when(k == 0)
            def _():
                acc_ref[...] = jnp.zeros_like(acc_ref)

            acc_ref[...] += _dot(a_ref[...], b_ref[...], ca, cb)

            @pl.when(k == nk - 1)
            def _():
                finish(acc_ref[...])

    return pl.pallas_call(
        body, name=name, grid=(nm, nn, nk),
        in_specs=in_specs, out_specs=o_spec, out_shape=out_shape,
        input_output_aliases=aliases,
        scratch_shapes=[] if nk == 1 else [pltpu.VMEM((tm, tn), F32)],
        compiler_params=_cp(("parallel", "parallel", "arbitrary")),
    )(*args)


def _ln_stats(r):
    mu = jnp.mean(r, axis=-1, keepdims=True)
    xc = r - mu
    var = jnp.mean(xc * xc, axis=-1, keepdims=True)
    rstd = lax.rsqrt(var + LN_EPS)
    return xc * rstd, rstd


def _add_ln(x, h, g, b, alpha, *, name):
    T, D = x.shape
    tr = _tile(T, 512, SUBLANES)
    row = pl.BlockSpec((tr, D), lambda i: (i, 0))
    vec = pl.BlockSpec((1, D), lambda i: (0, 0))

    def body(x_ref, h_ref, g_ref, b_ref, o_ref, ob_ref):
        xhat, _ = _ln_stats(alpha * x_ref[...] + h_ref[...])
        y = xhat * g_ref[...] + b_ref[...]
        o_ref[...] = y
        ob_ref[...] = y.astype(BF16)

    return pl.pallas_call(
        body, name=name, grid=(T // tr,),
        in_specs=[row, row, vec, vec], out_specs=[row, row],
        out_shape=[jax.ShapeDtypeStruct((T, D), F32), jax.ShapeDtypeStruct((T, D), BF16)],
        compiler_params=_cp(("parallel",)),
    )(x, h, g.reshape(1, D), b.reshape(1, D))


def _ln_bwd(dy, x, h, g, alpha, *, name):
    T, D = x.shape
    tr = _tile(T, 512, SUBLANES)
    row = pl.BlockSpec((tr, D), lambda i: (i, 0))
    vec = pl.BlockSpec((1, D), lambda i: (0, 0))

    def body(dy_ref, x_ref, h_ref, g_ref, dr_ref, drb_ref, dg_ref, db_ref):
        @pl.when(pl.program_id(0) == 0)
        def _():
            dg_ref[...] = jnp.zeros_like(dg_ref)
            db_ref[...] = jnp.zeros_like(db_ref)

        xhat, rstd = _ln_stats(alpha * x_ref[...] + h_ref[...])
        dyv = dy_ref[...]
        dxhat = dyv * g_ref[...]
        c1 = jnp.mean(dxhat, axis=-1, keepdims=True)
        c2 = jnp.mean(dxhat * xhat, axis=-1, keepdims=True)
        dr = rstd * (dxhat - c1 - xhat * c2)
        dr_ref[...] = dr
        drb_ref[...] = dr.astype(BF16)
        dg_ref[...] += jnp.sum(dyv * xhat, axis=0, keepdims=True)
        db_ref[...] += jnp.sum(dyv, axis=0, keepdims=True)

    return pl.pallas_call(
        body, name=name, grid=(T // tr,),
        in_specs=[row, row, row, vec], out_specs=[row, row, vec, vec],
        out_shape=[jax.ShapeDtypeStruct((T, D), F32), jax.ShapeDtypeStruct((T, D), BF16),
                   jax.ShapeDtypeStruct((1, D), F32), jax.ShapeDtypeStruct((1, D), F32)],
        compiler_params=_cp(("arbitrary",)),
    )(dy, x, h, g.reshape(1, D))


def _sgu_pieces(zp, lg, lb, E):
    z = _gelu(zp)
    u, v = z[:, :E], z[:, E:]
    vhat, rstd = _ln_stats(v)
    vn = vhat * lg + lb
    return u, vhat, rstd, vn


def _tril_mask():
    t = lax.broadcasted_iota(jnp.int32, (CHUNK, CHUNK), 0)
    s = lax.broadcasted_iota(jnp.int32, (CHUNK, CHUNK), 1)
    return s <= t


def _sgu_fwd(zp, ws, bs, lg, lb, *, name):
    T, E2 = zp.shape
    E = E2 // 2
    G = ws.shape[0]
    cw = E // G

    def body(zp_ref, ws_ref, bs_ref, lg_ref, lb_ref, y_ref):
        u, _, _, vn = _sgu_pieces(zp_ref[...].astype(F32), lg_ref[...], lb_ref[...], E)
        vnb = vn.astype(BF16)
        tril = _tril_mask()
        for g in range(G):
            cols = slice(g * cw, (g + 1) * cw)
            w = jnp.where(tril, ws_ref[g], 0.0).astype(BF16)
            sv = _dot(w, vnb[:, cols], 1, 0) + bs_ref[g]
            y_ref[:, cols] = (u[:, cols] * sv).astype(BF16)

    return pl.pallas_call(
        body, name=name, grid=(T // CHUNK,),
        in_specs=[pl.BlockSpec((CHUNK, E2), lambda n: (n, 0)),
                  pl.BlockSpec((G, CHUNK, CHUNK), lambda n: (0, 0, 0)),
                  pl.BlockSpec((G, CHUNK, 1), lambda n: (0, 0, 0)),
                  pl.BlockSpec((1, E), lambda n: (0, 0)),
                  pl.BlockSpec((1, E), lambda n: (0, 0))],
        out_specs=pl.BlockSpec((CHUNK, E), lambda n: (n, 0)),
        out_shape=jax.ShapeDtypeStruct((T, E), BF16),
        compiler_params=_cp(("parallel",)),
    )(zp, ws, bs.reshape(G, CHUNK, 1), lg.reshape(1, E), lb.reshape(1, E))


def _sgu_bwd(zp, dy, ws, bs, lg, lb, *, name):
    T, E2 = zp.shape
    E = E2 // 2
    G = ws.shape[0]
    cw = E // G
    assert G <= LANES

    def body(zp_ref, dy_ref, ws_ref, bs_ref, lg_ref, lb_ref,
             dzp_ref, dws_ref, dbs_ref, dlg_ref, dlb_ref, dvn_ref):
        @pl.when(pl.program_id(0) == 0)
        def _():
            dws_ref[...] = jnp.zeros_like(dws_ref)
            dbs_ref[...] = jnp.zeros_like(dbs_ref)
            dlg_ref[...] = jnp.zeros_like(dlg_ref)
            dlb_ref[...] = jnp.zeros_like(dlb_ref)

        zpv = zp_ref[...].astype(F32)
        lgv = lg_ref[...]
        u, vhat, rstd, vn = _sgu_pieces(zpv, lgv, lb_ref[...], E)
        gp = _gelu_grad(zpv)
        vnb = vn.astype(BF16)
        tril = _tril_mask()
        lane = lax.broadcasted_iota(jnp.int32, (CHUNK, LANES), 1)
        dbs = dbs_ref[...]
        for g in range(G):
            cols = slice(g * cw, (g + 1) * cw)
            w = jnp.where(tril, ws_ref[g], 0.0).astype(BF16)
            sv = _dot(w, vnb[:, cols], 1, 0) + bs_ref[g]
            dyg = dy_ref[:, cols].astype(F32)
            dzp_ref[:, cols] = (dyg * sv * gp[:, cols]).astype(BF16)
            dsv = dyg * u[:, cols]
            dsvb = dsv.astype(BF16)
            dws_ref[g] += jnp.where(tril, _dot(dsvb, vnb[:, cols], 1, 1), 0.0)
            dvn_ref[:, cols] = _dot(w, dsvb, 0, 0)
            dbs = dbs + jnp.where(lane == g, jnp.sum(dsv, axis=1, keepdims=True), 0.0)
        dbs_ref[...] = dbs
        dvn = dvn_ref[...]
        dlg_ref[...] += jnp.sum(dvn * vhat, axis=0, keepdims=True)
        dlb_ref[...] += jnp.sum(dvn, axis=0, keepdims=True)
        dvhat = dvn * lgv
        c1 = jnp.mean(dvhat, axis=-1, keepdims=True)
        c2 = jnp.mean(dvhat * vhat, axis=-1, keepdims=True)
        dv = rstd * (dvhat - c1 - vhat * c2)
        dzp_ref[:, E:] = (dv * gp[:, E:]).astype(BF16)

    vecE = pl.BlockSpec((1, E), lambda n: (0, 0))
    return pl.pallas_call(
        body, name=name, grid=(T // CHUNK,),
        in_specs=[pl.BlockSpec((CHUNK, E2), lambda n: (n, 0)),
                  pl.BlockSpec((CHUNK, E), lambda n: (n, 0)),
                  pl.BlockSpec((G, CHUNK, CHUNK), lambda n: (0, 0, 0)),
                  pl.BlockSpec((G, CHUNK, 1), lambda n: (0, 0, 0)),
                  vecE, vecE],
        out_specs=[pl.BlockSpec((CHUNK, E2), lambda n: (n, 0)),
                   pl.BlockSpec((G, CHUNK, CHUNK), lambda n: (0, 0, 0)),
                   pl.BlockSpec((CHUNK, LANES), lambda n: (0, 0)),
                   vecE, vecE],
        out_shape=[jax.ShapeDtypeStruct((T, E2), BF16),
                   jax.ShapeDtypeStruct((G, CHUNK, CHUNK), F32),
                   jax.ShapeDtypeStruct((CHUNK, LANES), F32),
                   jax.ShapeDtypeStruct((1, E), F32), jax.ShapeDtypeStruct((1, E), F32)],
        scratch_shapes=[pltpu.VMEM((CHUNK, E), F32)],
        compiler_params=_cp(("arbitrary",)),
    )(zp, dy, ws, bs.reshape(G, CHUNK, 1), lg.reshape(1, E), lb.reshape(1, E))


def _delay(h, k):
    t = lax.broadcasted_iota(jnp.int32, h.shape, 0)
    return jnp.where(t >= k, pltpu.roll(h, k, 0), 0.0)


def _advance(d, k):
    T = d.shape[0]
    t = lax.broadcasted_iota(jnp.int32, d.shape, 0)
    return jnp.where(t < T - k, pltpu.roll(d, T - k, 0), 0.0)


GLU_COLS = 256


def _conv3(h, w_ref, b_ref):
    return w_ref[2:3, :] * h + w_ref[1:2, :] * _delay(h, 1) + w_ref[0:1, :] * _delay(h, 2) + b_ref[...]


def _convglu_fwd(hup, cw, cb, *, name):
    T, F2 = hup.shape
    F = F2 // 2
    tc = _tile(F, GLU_COLS)
    nt = F // tc

    def body(ha_ref, hg_ref, wa_ref, wg_ref, ba_ref, bg_ref, o_ref, a_ref, g_ref):
        for s in range(tc // LANES):
            c = slice(s * LANES, (s + 1) * LANES)
            a = _conv3(ha_ref[:, c].astype(F32), wa_ref.at[:, c], ba_ref.at[:, c])
            g = _conv3(hg_ref[:, c].astype(F32), wg_ref.at[:, c], bg_ref.at[:, c])
            o_ref[:, c] = (_gelu(a) * g).astype(BF16)
            a_ref[:, c] = a.astype(BF16)
            g_ref[:, c] = g.astype(BF16)

    col = lambda off: pl.BlockSpec((T, tc), lambda j: (0, j + off))
    w3 = lambda off: pl.BlockSpec((3, tc), lambda j: (0, j + off))
    b1 = lambda off: pl.BlockSpec((1, tc), lambda j: (0, j + off))
    shp = jax.ShapeDtypeStruct((T, F), BF16)
    return pl.pallas_call(
        body, name=name, grid=(nt,),
        in_specs=[col(0), col(nt), w3(0), w3(nt), b1(0), b1(nt)],
        out_specs=[col(0)] * 3, out_shape=[shp] * 3,
        compiler_params=_cp(("parallel",)),
    )(hup, hup, cw, cw, cb.reshape(1, F2), cb.reshape(1, F2))


def _convglu_bwd(hup, a, g, dact, cw, *, name):
    T, F2 = hup.shape
    F = F2 // 2
    tc = _tile(F, GLU_COLS)
    nt = F // tc

    def half(h, w_ref, d, dh_ref, dw_ref, db_ref, i, c):
        d1, d2 = _advance(d, 1), _advance(d, 2)
        dh_ref[i, :, c] = (w_ref[2:3, c] * d + w_ref[1:2, c] * d1 + w_ref[0:1, c] * d2).astype(BF16)
        prod = _dot(h, jnp.concatenate([d2, d1, d], axis=1).astype(BF16), 0, 0)
        eye = (lax.broadcasted_iota(jnp.int32, (LANES, LANES), 0)
               == lax.broadcasted_iota(jnp.int32, (LANES, LANES), 1))
        for k in range(3):
            diag = jnp.where(eye, prod[:, k * LANES:(k + 1) * LANES], 0.0)
            dw_ref[i, k:k + 1, c] = jnp.sum(diag, axis=0, keepdims=True)
        db_ref[i, :, c] = jnp.sum(d, axis=0, keepdims=True)

    def body(ha_ref, hg_ref, a_ref, g_ref, d_ref, wa_ref, wg_ref, dh_ref, dw_ref, db_ref):
        for s in range(tc // LANES):
            c = slice(s * LANES, (s + 1) * LANES)
            a = a_ref[:, c].astype(F32)
            d = d_ref[:, c].astype(F32)
            half(ha_ref[:, c].astype(BF16), wa_ref, d * g_ref[:, c].astype(F32) * _gelu_grad(a), dh_ref, dw_ref,
                 db_ref, 0, c)
            half(hg_ref[:, c].astype(BF16), wg_ref, d * _gelu(a), dh_ref, dw_ref, db_ref, 1, c)

    col = lambda off: pl.BlockSpec((T, tc), lambda j: (0, j + off))
    w3 = lambda off: pl.BlockSpec((3, tc), lambda j: (0, j + off))
    return pl.pallas_call(
        body, name=name, grid=(nt,),
        in_specs=[col(0), col(nt), col(0), col(0), col(0), w3(0), w3(nt)],
        out_specs=[pl.BlockSpec((2, T, tc), lambda j: (0, 0, j)),
                   pl.BlockSpec((2, 3, tc), lambda j: (0, 0, j)),
                   pl.BlockSpec((2, 1, tc), lambda j: (0, 0, j))],
        out_shape=[jax.ShapeDtypeStruct((2, T, F), BF16),
                   jax.ShapeDtypeStruct((2, 3, F), F32),
                   jax.ShapeDtypeStruct((2, 1, F), F32)],
        compiler_params=_cp(("parallel",)),
    )(hup, hup, a, g, dact, cw, cw)


def _bucket_table():
    iq = np.arange(BLK)[:, None]
    ik = np.arange(2 * BLK)[None, :]
    delta = iq + BLK - ik
    out = []
    for win, dil in DILATED_GROUPS:
        valid = (delta >= 0) & (delta <= win // dil)
        n = (np.clip(delta, 0, None) * dil).astype(np.int32)
        max_exact = REL_BUCKETS // 2
        nf = np.maximum(n, 1).astype(np.float32)
        large = max_exact + (np.log(nf / np.float32(max_exact)) / np.float32(math.log(REL_MAX_DIST / max_exact))
                             * np.float32(REL_BUCKETS - max_exact)).astype(np.int32)
        large = np.minimum(large, REL_BUCKETS - 1)
        out.append(np.where(valid, np.where(n < max_exact, n, large), -1))
    return np.stack(out).astype(np.int32)


def _band_bias(rel_table, H, *, name):
    bkt = jnp.asarray(_bucket_table())

    def body(tab_ref, bk_ref, o_ref):
        g = pl.program_id(0)
        bk = bk_ref[...]
        prev = lax.broadcasted_iota(jnp.int32, bk.shape, 1) < BLK
        for h in range(H):
            def bucket(c, acc):
                return jnp.where(bk == c, tab_ref[c, g * H + h], acc)

            b = lax.fori_loop(0, REL_BUCKETS, bucket, jnp.zeros(bk.shape, F32))
            b = jnp.where(bk >= 0, b, NEG)
            o_ref[h, 0] = b
            o_ref[h, 1] = jnp.where(prev, NEG, b)

    return pl.pallas_call(
        body, name=name, grid=(N_GROUPS,),
        in_specs=[pl.BlockSpec(memory_space=pltpu.SMEM),
                  pl.BlockSpec((None, BLK, 2 * BLK), lambda g: (g, 0, 0))],
        out_specs=pl.BlockSpec((None, H, 2, BLK, 2 * BLK), lambda g: (g, 0, 0, 0, 0)),
        out_shape=jax.ShapeDtypeStruct((N_GROUPS, H, 2, BLK, 2 * BLK), F32),
        compiler_params=_cp(("parallel",)),
    )(rel_table, bkt)


PAIR = 2 * HEAD_DIM


def _bdot(a, b, ca, cb):
    return lax.dot_general(a, b, (((ca,), (cb,)), ((0,), (0,))), preferred_element_type=F32)


def _blocks_per_step(nb, dil):
    return 2 if (nb // dil) % 2 == 0 else 1


def _blk(i):
    return slice(i * BLK, (i + 1) * BLK)


def _heads(ref, i, H, own_columns_only):
    low = lax.broadcasted_iota(jnp.int32, (BLK, PAIR), 1) < HEAD_DIM
    out = []
    for h in range(H):
        a = ref[_blk(i), (h // 2) * PAIR:(h // 2 + 1) * PAIR]
        if own_columns_only:
            a = jnp.where(low if h % 2 == 0 else jnp.logical_not(low), a, jnp.zeros_like(a))
        out.append(a)
    return jnp.stack(out)


def _pair_sums(o):
    return [o[2 * p] + o[2 * p + 1] for p in range(o.shape[0] // 2)]


def _store_pairs(ref, i, o, dtype, scale=None):
    for p, v in enumerate(_pair_sums(o)):
        ref[_blk(i), p * PAIR:(p + 1) * PAIR] = (v if scale is None else scale * v).astype(dtype)


def _stat_tile(s):
    lane = lax.broadcasted_iota(jnp.int32, (BLK, LANES), 1)
    t = jnp.zeros((BLK, LANES), F32)
    for h in range(s.shape[0]):
        t = jnp.where(lane == h, s[h], t)
    return t


def _stat_cols(ref, i, H):
    return jnp.stack([ref[_blk(i), h:h + 1] for h in range(H)])


def _scores(qm, kp, kc, bias):
    scale = HEAD_DIM ** -0.5
    sp = _bdot(qm, kp, 2, 2) * scale + bias[:, :, :BLK]
    sc = _bdot(qm, kc, 2, 2) * scale + bias[:, :, BLK:]
    return sp, sc


def _attn_specs(nb, dil, g, H, HD, sb):
    nbg = nb // dil
    QB = _blocks_per_step(nb, dil)
    cur = lambda col: pl.BlockSpec((QB * BLK, HD), lambda b: (sb(b), col))
    prev = lambda col: pl.BlockSpec((BLK, HD), lambda b: (jnp.maximum(QB * sb(b) - 1, 0), col))
    stat = pl.BlockSpec((QB * BLK, LANES), lambda b: (sb(b), 0))
    bias0 = pl.BlockSpec((None, H, None, BLK, 2 * BLK),
                         lambda b: (g, 0, jnp.where((QB * sb(b)) % nbg == 0, 1, 0), 0, 0))
    bias1 = pl.BlockSpec((None, H, None, BLK, 2 * BLK), lambda b: (g, 0, 0, 0, 0))
    return cur, prev, stat, bias0, bias1


def _attn_fwd(q, k, v, bias, g, dil, *, name):
    (qa, qc), (ka, kc_), (va, vc_) = q, k, v
    T = qa.shape[0]
    H = bias.shape[1]
    HD = H * HEAD_DIM
    assert H % 2 == 0 and H <= LANES and T % (dil * BLK) == 0
    nb = T // BLK
    QB = _blocks_per_step(nb, dil)
    cur, prev, stat, bias0, bias1 = _attn_specs(nb, dil, g, H, HD, lambda b: b)

    def body(q_ref, kp_ref, kc_ref, vp_ref, vc_ref, b0_ref, b1_ref, o_ref, l_ref):
        for i in range(QB):
            kpi, vpi = (kp_ref, vp_ref) if i == 0 else (kc_ref, vc_ref)
            sp, sc = _scores(_heads(q_ref, i, H, True), _heads(kpi, max(i - 1, 0), H, False),
                             _heads(kc_ref, i, H, False), (b0_ref if i == 0 else b1_ref)[...])
            m = jnp.maximum(jnp.max(sp, axis=-1, keepdims=True), jnp.max(sc, axis=-1, keepdims=True))
            pp, pc = jnp.exp(sp - m), jnp.exp(sc - m)
            den = jnp.sum(pp, axis=-1, keepdims=True) + jnp.sum(pc, axis=-1, keepdims=True)
            o = (_bdot((pp / den).astype(BF16), _heads(vpi, max(i - 1, 0), H, True), 2, 1)
                 + _bdot((pc / den).astype(BF16), _heads(vc_ref, i, H, True), 2, 1))
            _store_pairs(o_ref, i, o, F32)
            l_ref[_blk(i), :] = _stat_tile(m + jnp.log(den))

    return pl.pallas_call(
        body, name=name, grid=(nb // QB,),
        in_specs=[cur(qc), prev(kc_), cur(kc_), prev(vc_), cur(vc_), bias0, bias1],
        out_specs=[cur(0), stat],
        out_shape=[jax.ShapeDtypeStruct((T, HD), F32), jax.ShapeDtypeStruct((T, LANES), F32)],
        compiler_params=_cp(("parallel",)),
    )(qa, ka, ka, va, va, bias, bias)


def _grouped_spec(tr, d, C):
    return pl.BlockSpec((d, tr // d, C), lambda i: (0, i, 0))


def _as_grouped(a, d):
    T, C = a.shape
    return a.reshape(d, T // d, C)


def _to_token_slabs(ref, slabs, d):
    n_r = ref.shape[1]
    for j in range(ref.shape[2] // LANES):
        for r in range(d):
            slabs.at[j][pl.ds(r, n_r, stride=d), :] = ref[r, :, j * LANES:(j + 1) * LANES].astype(F32)


def _from_token_slabs(slabs, ref, d):
    n_r = ref.shape[1]
    for j in range(ref.shape[2] // LANES):
        for r in range(d):
            ref[r, :, j * LANES:(j + 1) * LANES] = slabs.at[j][pl.ds(r, n_r, stride=d), :].astype(ref.dtype)


def _attn_combine(os, ls, dils, *, name):
    n = len(os)
    T, HD = os[0].shape
    H = HD // HEAD_DIM
    tr = _tile(T, 512, SUBLANES)
    n_sl = HD // LANES
    assert PAIR == LANES

    def body(*refs):
        o_refs, l_refs = refs[:n], refs[n:2 * n]
        oc_ref, ocb_ref = refs[2 * n:2 * n + 2]
        lt_refs = refs[2 * n + 2:3 * n + 2]
        o_scr, l_scr = refs[3 * n + 2], refs[3 * n + 3]
        for g in range(n):
            _to_token_slabs(o_refs[g], o_scr.at[g], dils[g])
            _to_token_slabs(l_refs[g], l_scr.at[g], dils[g])
        ls_ = [l_scr[g, 0] for g in range(n)]
        m = ls_[0]
        for l in ls_[1:]:
            m = jnp.maximum(m, l)
        ws = [jnp.exp(l - m) for l in ls_]
        den = ws[0]
        for w in ws[1:]:
            den = den + w
        l_scr[n, 0] = m + jnp.log(den)
        for g in range(n):
            _from_token_slabs(l_scr.at[n], lt_refs[g], dils[g])
        ws = [w / den for w in ws]
        low = lax.broadcasted_iota(jnp.int32, (tr, PAIR), 1) < HEAD_DIM
        for p in range(H // 2):
            acc = jnp.zeros((tr, PAIR), F32)
            for g in range(n):
                w = jnp.where(low, ws[g][:, 2 * p:2 * p + 1], ws[g][:, 2 * p + 1:2 * p + 2])
                acc = acc + w * o_scr[g, p]
            oc_ref[:, p * PAIR:(p + 1) * PAIR] = acc
            ocb_ref[:, p * PAIR:(p + 1) * PAIR] = acc.astype(BF16)

    wide = pl.BlockSpec((tr, HD), lambda i: (i, 0))
    lts = pl.pallas_call(
        body, name=name, grid=(T // tr,),
        in_specs=[_grouped_spec(tr, d, HD) for d in dils] + [_grouped_spec(tr, d, LANES) for d in dils],
        out_specs=[wide, wide] + [_grouped_spec(tr, d, LANES) for d in dils],
        out_shape=[jax.ShapeDtypeStruct((T, HD), F32), jax.ShapeDtypeStruct((T, HD), BF16)]
        + [jax.ShapeDtypeStruct((d, T // d, LANES), F32) for d in dils],
        scratch_shapes=[pltpu.VMEM((n, n_sl, tr, LANES), F32), pltpu.VMEM((n + 1, 1, tr, LANES), F32)],
        compiler_params=_cp(("parallel",)),
    )(*[_as_grouped(o, d) for o, d in zip(os, dils)], *[_as_grouped(l, d) for l, d in zip(ls, dils)])
    return lts[0], lts[1], [l.reshape(T, LANES) for l in lts[2:]]


def _attn_bwd_prep(do, oc, dils, *, name):
    T, HD = do.shape
    H = HD // HEAD_DIM
    n = len(dils)
    tr = _tile(T, 512, SUBLANES)
    n_sl = HD // LANES
    wide = pl.BlockSpec((tr, HD), lambda i: (i, 0))

    def body(*refs):
        do_ref, oc_ref = refs[:2]
        dl_refs, dob_refs = refs[2:2 + n], refs[2 + n:2 + 2 * n]
        do_scr, dl_scr = refs[2 + 2 * n], refs[3 + 2 * n]
        low = lax.broadcasted_iota(jnp.int32, (tr, PAIR), 1) < HEAD_DIM
        lane = lax.broadcasted_iota(jnp.int32, (tr, LANES), 1)
        dl = jnp.zeros((tr, LANES), F32)
        for p in range(H // 2):
            d = do_ref[:, p * PAIR:(p + 1) * PAIR]
            do_scr[p] = d
            prod = d * oc_ref[:, p * PAIR:(p + 1) * PAIR]
            dl = jnp.where(lane == 2 * p, jnp.sum(jnp.where(low, prod, 0.0), axis=-1, keepdims=True), dl)
            dl = jnp.where(lane == 2 * p + 1, jnp.sum(jnp.where(low, 0.0, prod), axis=-1, keepdims=True), dl)
        dl_scr[0] = dl
        for g in range(n):
            _from_token_slabs(dl_scr, dl_refs[g], dils[g])
            _from_token_slabs(do_scr, dob_refs[g], dils[g])

    outs = pl.pallas_call(
        body, name=name, grid=(T // tr,),
        in_specs=[wide, wide],
        out_specs=[_grouped_spec(tr, d, LANES) for d in dils] + [_grouped_spec(tr, d, HD) for d in dils],
        out_shape=[jax.ShapeDtypeStruct((d, T // d, LANES), F32) for d in dils]
        + [jax.ShapeDtypeStruct((d, T // d, HD), BF16) for d in dils],
        scratch_shapes=[pltpu.VMEM((n_sl, tr, LANES), F32), pltpu.VMEM((1, tr, LANES), F32)],
        compiler_params=_cp(("parallel",)),
    )(do, oc)
    return [a.reshape(T, LANES) for a in outs[:n]], [a.reshape(T, HD) for a in outs[n:]]


def _attn_bwd(q, k, v, bias, g, dil, do, lse, delta, *, name):
    (qa, qc), (ka, kc_), (va, vc_) = q, k, v
    T, HD = do.shape
    H = HD // HEAD_DIM
    nb = T // BLK
    scale = HEAD_DIM ** -0.5
    QB = _blocks_per_step(nb, dil)
    ns = nb // QB
    cur, prev, stat, bias0, bias1 = _attn_specs(nb, dil, g, H, HD, lambda b: jnp.minimum(b, ns - 1))
    late = pl.BlockSpec((QB * BLK, HD), lambda b: (jnp.maximum(b - 1, 0), 0))

    def body(q_ref, kp_ref, kc_ref, vp_ref, vc_ref, b0_ref, b1_ref, do_ref, l_ref, dl_ref,
             dq_ref, dk_ref, dv_ref, db_ref, ck_ref, cv_ref):
        b = pl.program_id(0)

        @pl.when(b == 0)
        def _():
            db_ref[...] = jnp.zeros_like(db_ref)
            ck_ref[...] = jnp.zeros_like(ck_ref)
            cv_ref[...] = jnp.zeros_like(cv_ref)

        @pl.when(b < ns)
        def _():
            dk_ref[...] = ck_ref[...]
            dv_ref[...] = cv_ref[...]
            for i in range(QB):
                kpi, vpi, ip = (kp_ref, vp_ref, 0) if i == 0 else (kc_ref, vc_ref, i - 1)
                qm, dom = _heads(q_ref, i, H, True), _heads(do_ref, i, H, True)
                sp, sc = _scores(qm, _heads(kpi, ip, H, False), _heads(kc_ref, i, H, False),
                                 (b0_ref if i == 0 else b1_ref)[...])
                l, dl = _stat_cols(l_ref, i, H), _stat_cols(dl_ref, i, H)
                pp, pc = jnp.exp(sp - l), jnp.exp(sc - l)
                dsp = pp * (_bdot(dom, _heads(vpi, ip, H, False), 2, 2) - dl)
                dsc = pc * (_bdot(dom, _heads(vc_ref, i, H, False), 2, 2) - dl)
                db_ref[:, :, :BLK] += dsp
                db_ref[:, :, BLK:] += dsc
                dspb, dscb = dsp.astype(BF16), dsc.astype(BF16)
                _store_pairs(dq_ref, i, _bdot(dspb, _heads(kpi, ip, H, True), 2, 1)
                             + _bdot(dscb, _heads(kc_ref, i, H, True), 2, 1), BF16, scale)
                dkp = _pair_sums(_bdot(dspb, qm, 1, 1))
                dkc = _pair_sums(_bdot(dscb, qm, 1, 1))
                dvp = _pair_sums(_bdot(pp.astype(BF16), dom, 1, 1))
                dvc = _pair_sums(_bdot(pc.astype(BF16), dom, 1, 1))
                for p in range(H // 2):
                    cols = slice(p * PAIR, (p + 1) * PAIR)
                    if i == 0:
                        dk_ref[_blk(QB - 1), cols] += scale * dkp[p]
                        dv_ref[_blk(QB - 1), cols] += dvp[p]
                    else:
                        ck_ref[_blk(i - 1), cols] += scale * dkp[p]
                        cv_ref[_blk(i - 1), cols] += dvp[p]
                    ck_ref[_blk(i), cols] = scale * dkc[p]
                    cv_ref[_blk(i), cols] = dvc[p]

        @pl.when(b == ns)
        def _():
            dk_ref[...] = ck_ref[...]
            dv_ref[...] = cv_ref[...]

    f32 = jax.ShapeDtypeStruct((T, HD), F32)
    return pl.pallas_call(
        body, name=name, grid=(ns + 1,),
        in_specs=[cur(qc), prev(kc_), cur(kc_), prev(vc_), cur(vc_), bias0, bias1, cur(0), stat, stat],
        out_specs=[cur(0), late, late, pl.BlockSpec((H, BLK, 2 * BLK), lambda b: (0, 0, 0))],
        out_shape=[jax.ShapeDtypeStruct((T, HD), BF16), f32, f32,
                   jax.ShapeDtypeStruct((H, BLK, 2 * BLK), F32)],
        scratch_shapes=[pltpu.VMEM((QB * BLK, HD), F32), pltpu.VMEM((QB * BLK, HD), F32)],
        compiler_params=_cp(("arbitrary",)),
    )(qa, ka, ka, va, va, bias, bias, do, lse, delta)


def _rel_grad(dbs, *, name):
    H = dbs[0].shape[1]
    n = len(dbs)
    assert H <= LANES
    bkt = jnp.asarray(_bucket_table())

    def body(*refs):
        bk = refs[n][...]
        o_ref = refs[n + 1]
        db = refs[0][...]
        for r in refs[1:n]:
            db = db + r[...]
        row = lax.broadcasted_iota(jnp.int32, (REL_BUCKETS, LANES), 0)
        col = lax.broadcasted_iota(jnp.int32, (REL_BUCKETS, LANES), 1)

        def bucket(c, acc):
            sel = bk == c
            for h in range(H):
                val = jnp.sum(jnp.where(sel, db[h], 0.0))
                acc = jnp.where((row == c) & (col == h), val, acc)
            return acc

        o_ref[...] = lax.fori_loop(0, REL_BUCKETS, bucket, jnp.zeros((REL_BUCKETS, LANES), F32))

    dspec = pl.BlockSpec((None, H, BLK, 2 * BLK), lambda g: (g, 0, 0, 0))
    return pl.pallas_call(
        body, name=name, grid=(N_GROUPS,),
        in_specs=[dspec] * n + [pl.BlockSpec((None, BLK, 2 * BLK), lambda g: (g, 0, 0))],
        out_specs=pl.BlockSpec((None, REL_BUCKETS, LANES), lambda g: (g, 0, 0)),
        out_shape=jax.ShapeDtypeStruct((N_GROUPS, REL_BUCKETS, LANES), F32),
        compiler_params=_cp(("parallel",)),
    )(*dbs, bkt)


def _to_group(a, d):
    T, C = a.shape
    return a if d == 1 else a.reshape(T // d, d, C).transpose(1, 0, 2).reshape(T, C)


def _from_group(a, d):
    T, C = a.shape
    return a if d == 1 else a.reshape(d, T // d, C).transpose(1, 0, 2).reshape(T, C)


def _rows_view(a):
    a2 = a.reshape(-1, a.shape[-1])
    R, C = a2.shape
    tr = _tile(R, max(SUBLANES, (1 << 18) // C), SUBLANES)
    return a2, R, C, tr


def _addn(xs, out_dtype, *, name):
    shape = xs[0].shape
    x2s = [_rows_view(x)[0] for x in xs]
    _, R, C, tr = _rows_view(xs[0])
    spec = pl.BlockSpec((tr, C), lambda i: (i, 0))

    def body(*refs):
        acc = refs[0][...].astype(F32)
        for r in refs[1:-1]:
            acc = acc + r[...].astype(F32)
        refs[-1][...] = acc.astype(out_dtype)

    out = pl.pallas_call(
        body, name=name, grid=(R // tr,),
        in_specs=[spec] * len(xs), out_specs=spec,
        out_shape=jax.ShapeDtypeStruct((R, C), out_dtype),
        compiler_params=_cp(("parallel",)),
    )(*x2s)
    return out.reshape(shape)


def _addn_grouped(parts, out_dtype, *, name):
    T, C = parts[0][0].shape
    tr = _tile(T, 512, SUBLANES)
    assert all(tr % d == 0 for _, d in parts)

    assert C % LANES == 0

    def body(*refs):
        o_ref, acc_ref = refs[-2], refs[-1]
        acc_ref[...] = jnp.zeros_like(acc_ref)
        for ref, (_, d) in zip(refs, parts):
            for r in range(d):
                for j in range(C // LANES):
                    acc_ref.at[j][pl.ds(r, tr // d, stride=d), :] += ref[r, :, j * LANES:(j + 1) * LANES]
        for j in range(C // LANES):
            o_ref[:, j * LANES:(j + 1) * LANES] = acc_ref[j].astype(out_dtype)

    return pl.pallas_call(
        body, name=name, grid=(T // tr,),
        in_specs=[pl.BlockSpec((d, tr // d, C), lambda i: (0, i, 0)) for _, d in parts],
        out_specs=pl.BlockSpec((tr, C), lambda i: (i, 0)),
        out_shape=jax.ShapeDtypeStruct((T, C), out_dtype),
        scratch_shapes=[pltpu.VMEM((C // LANES, tr, LANES), F32)],
        compiler_params=_cp(("parallel",)),
    )(*[a.reshape(d, T // d, C) for a, d in parts])


def _slot_sum(b_ref, own, my_id):
    acc = None
    for s in range(b_ref.shape[0]):
        term = jnp.where(my_id == s, own, b_ref[s].astype(F32))
        acc = term if acc is None else acc + term
    return acc


def _part_geometry(shape, axis):
    _, R, C = shape
    if axis == COL:
        return R // 2, C // N_CHIPS
    return R // (2 * N_CHIPS), C


def _sum_parts(buf, grad, axis, where, *, name):
    n, L, Rp, Cp = buf.shape
    assert (Rp, Cp) == _part_geometry(grad.shape, axis)
    tr = _tile(Rp, max(2 * SUBLANES, (1 << 17) // Cp), 2 * SUBLANES)
    nr = Rp // tr

    def own_map(l, i, w):
        if axis == COL:
            return (l, w[1] * nr + i, w[0])
        return (l, (2 * w[0] + w[1]) * nr + i, 0)

    def body(w_ref, b_ref, g_ref, o_ref):
        o_ref[...] = _slot_sum(b_ref, g_ref[...].astype(F32), w_ref[2])

    return pl.pallas_call(
        body, name=name,
        grid_spec=pltpu.PrefetchScalarGridSpec(
            num_scalar_prefetch=1, grid=(L, nr),
            in_specs=[pl.BlockSpec((n, None, tr, Cp), lambda l, i, w: (0, l, i, 0)),
                      pl.BlockSpec((None, tr, Cp), own_map)],
            out_specs=pl.BlockSpec((None, None, tr, Cp), lambda l, i, w: (l, w[1], i, 0))),
        out_shape=jax.ShapeDtypeStruct((L, 2, Rp, Cp), F32),
        compiler_params=_cp(("parallel", "parallel")),
    )(where, buf, grad)


def _sum_small(buf, own, where, *, name):
    n, R, C = buf.shape
    tr = _tile(R, 512, SUBLANES)

    def body(w_ref, b_ref, g_ref, o_ref):
        o_ref[...] = _slot_sum(b_ref, g_ref[...], w_ref[2])

    return pl.pallas_call(
        body, name=name,
        grid_spec=pltpu.PrefetchScalarGridSpec(
            num_scalar_prefetch=1, grid=(R // tr,),
            in_specs=[pl.BlockSpec((n, tr, C), lambda i, w: (0, i, 0)),
                      pl.BlockSpec((tr, C), lambda i, w: (i, 0))],
            out_specs=pl.BlockSpec((tr, C), lambda i, w: (i, 0))),
        out_shape=jax.ShapeDtypeStruct((R, C), F32),
        compiler_params=_cp(("parallel",)),
    )(where, buf, own)


def _place_shard(shard, layer, axis, where, out_dtype, *, name, after=None):
    _, R, C = shard.shape
    tr = _tile(R, max(2 * SUBLANES, (1 << 18) // C), 2 * SUBLANES)
    nr = R // tr
    if axis == COL:
        full, out_map = (1, R, C * N_CHIPS), (lambda i, w: (0, i, w[0]))
    else:
        full, out_map = (1, R * N_CHIPS, C), (lambda i, w: (0, w[0] * nr + i, 0))

    def body(w_ref, x_ref, *rest):
        rest[-1][...] = x_ref[...].astype(out_dtype)

    extra = [] if after is None else [after]
    return pl.pallas_call(
        body, name=name,
        grid_spec=pltpu.PrefetchScalarGridSpec(
            num_scalar_prefetch=1, grid=(nr,),
            in_specs=[pl.BlockSpec((None, tr, C), lambda i, w: (layer, i, 0))]
            + [pl.BlockSpec(memory_space=pl.ANY)] * len(extra),
            out_specs=pl.BlockSpec((None, tr, C), out_map)),
        out_shape=jax.ShapeDtypeStruct(full, out_dtype),
        compiler_params=_cp(("parallel",)),
    )(where, shard, *extra)


def _loss_grad(y, tgt, *, name):
    T, D = y.shape
    tr = _tile(T, 512, SUBLANES)
    row = pl.BlockSpec((tr, D), lambda i: (i, 0))

    def body(y_ref, t_ref, dy_ref, l_ref, acc_ref):
        i = pl.program_id(0)

        @pl.when(i == 0)
        def _():
            acc_ref[...] = jnp.zeros_like(acc_ref)

        err = y_ref[...] - t_ref[...]
        dy_ref[...] = err * (1.0 / D)
        acc_ref[...] += jnp.sum(err * err, axis=0, keepdims=True)

        @pl.when(i == T // tr - 1)
        def _():
            tot = jnp.sum(acc_ref[...], axis=1, keepdims=True) * (0.5 / D)
            l_ref[...] = jnp.broadcast_to(tot, (1, LANES))

    return pl.pallas_call(
        body, name=name, grid=(T // tr,),
        in_specs=[row, row],
        out_specs=[row, pl.BlockSpec((1, LANES), lambda i: (0, 0))],
        out_shape=[jax.ShapeDtypeStruct((T, D), F32), jax.ShapeDtypeStruct((1, LANES), F32)],
        scratch_shapes=[pltpu.VMEM((1, D), F32)],
        compiler_params=_cp(("arbitrary",)),
    )(y, tgt)


def _adamw(w, g, m, v, *, name):
    shape = w.shape
    w2, R, C, tr = _rows_view(w)
    spec = pl.BlockSpec((tr, C), lambda i: (i, 0))

    def body(w_ref, g_ref, m_ref, v_ref, d_ref, nm_ref, nv_ref):
        gv = g_ref[...]
        nm = ADAM_B1 * m_ref[...] + (1.0 - ADAM_B1) * gv
        nv = ADAM_B2 * v_ref[...] + (1.0 - ADAM_B2) * (gv * gv)
        m_hat = nm / (1.0 - ADAM_B1 ** ADAM_STEP)
        v_hat = nv / (1.0 - ADAM_B2 ** ADAM_STEP)
        d_ref[...] = -ADAM_LR * (m_hat / (jnp.sqrt(v_hat) + ADAM_EPS) + ADAM_WD * w_ref[...])
        nm_ref[...] = nm
        nv_ref[...] = nv

    shp = jax.ShapeDtypeStruct((R, C), F32)
    outs = pl.pallas_call(
        body, name=name, grid=(R // tr,),
        in_specs=[spec] * 4, out_specs=[spec] * 3, out_shape=[shp] * 3,
        compiler_params=_cp(("parallel",)),
    )(w2, g.reshape(R, C), m.reshape(R, C), v.reshape(R, C))
    return tuple(o.reshape(shape) for o in outs)


def _pack(arrs):
    flat = jnp.concatenate([a.reshape(-1).astype(F32) for a in arrs])
    unit = SUBLANES * LANES
    pad = (-flat.shape[0]) % unit
    return jnp.pad(flat, (0, pad)).reshape(-1, LANES)


def _unpack(buf, shapes):
    flat = buf.reshape(-1)
    out, off = [], 0
    for s in shapes:
        n = int(np.prod(s))
        out.append(flat[off:off + n].reshape(s))
        off += n
    return out


def _me():
    return lax.axis_index("x"), lax.axis_index("y"), lax.axis_index("c")


def _flip(pos, k):
    x, y, c = pos
    return (1 - x if k & 4 else x, 1 - y if k & 2 else y, 1 - c if k & 1 else c)


HBM = pl.BlockSpec(memory_space=pltpu.HBM)

COL, ROW = -1, -2


def _shard_window(ref, axis, j, n):
    start = pl.multiple_of(j * n, n)
    if axis == COL:
        return ref.at[:, :, pl.ds(start, n)]
    return ref.at[:, pl.ds(start, n), :]


GATHER_PEERS = (2, 4, 6)
SEM = pl.BlockSpec(memory_space=pltpu.SEMAPHORE)
EFFECT = pltpu.SideEffectType.DATAFLOW_SIDE_EFFECTING


def _gather_copy(buf, axis, i, pi, chip, me, send_sems, recv_sems):
    win = _shard_window(buf, axis, chip, buf.shape[axis] // N_CHIPS)
    k = len(GATHER_PEERS) * i + pi
    return pltpu.make_async_remote_copy(
        src_ref=win, dst_ref=win, send_sem=send_sems.at[k], recv_sem=recv_sems.at[k],
        device_id=_flip(me, GATHER_PEERS[pi]), device_id_type=MESH)


def _gather_start(fulls, axes, *, name):
    n = len(fulls)
    n_sem = len(GATHER_PEERS) * n

    def body(*refs):
        send_sems, recv_sems = refs[n], refs[n + 1]
        bufs = refs[n + 2:2 * n + 2]
        token = refs[2 * n + 2]
        me = _me()
        for i in range(n):
            for pi in range(len(GATHER_PEERS)):
                _gather_copy(bufs[i], axes[i], i, pi, 2 * me[0] + me[1], me, send_sems, recv_sems).start()
        token[...] = jnp.zeros_like(token)

    outs = pl.pallas_call(
        body, name=name,
        in_specs=[HBM] * n, out_specs=[SEM, SEM] + [HBM] * n + [pl.BlockSpec(memory_space=pltpu.VMEM)],
        out_shape=[pltpu.SemaphoreType.DMA((n_sem,)), pltpu.SemaphoreType.DMA((n_sem,))]
        + [pltpu.HBM(f.shape, f.dtype) for f in fulls] + [jax.ShapeDtypeStruct((SUBLANES, LANES), F32)],
        input_output_aliases={i: 2 + i for i in range(n)},
        compiler_params=pltpu.CompilerParams(has_side_effects=EFFECT),
    )(*[pltpu.with_memory_space_constraint(f, pltpu.HBM) for f in fulls])
    return outs[0], outs[1], list(outs[2:2 + n]), outs[-1]


def _gather_wait(send_sems, recv_sems, bufs, axes, idxs, after, *, name):
    m = len(bufs)

    def body(*refs):
        ss, rs = refs[m], refs[m + 1]
        outs = refs[m + 3:]
        me = _me()
        for t, i in enumerate(idxs):
            for pi, k in enumerate(GATHER_PEERS):
                px, py, _ = _flip(me, k)
                _gather_copy(outs[t], axes[t], i, pi, 2 * me[0] + me[1], me, ss, rs).wait_send()
                _gather_copy(outs[t], axes[t], i, pi, 2 * px + py, me, ss, rs).wait_recv()

    return pl.pallas_call(
        body, name=name,
        in_specs=[HBM] * m + [SEM, SEM, pl.BlockSpec(memory_space=pl.ANY)], out_specs=[HBM] * m,
        out_shape=[pltpu.HBM(b.shape, b.dtype) for b in bufs],
        input_output_aliases={t: t for t in range(m)},
        compiler_params=pltpu.CompilerParams(has_side_effects=EFFECT),
    )(*bufs, send_sems, recv_sems, after)


def _grad_part(ref, layer, axis, pos):
    px, py, pc = pos
    Rp, Cp = _part_geometry(ref.shape, axis)
    chip = 2 * px + py
    if axis == COL:
        return ref.at[layer, pl.ds(pl.multiple_of(pc * Rp, Rp), Rp), pl.ds(pl.multiple_of(chip * Cp, Cp), Cp)]
    return ref.at[layer, pl.ds(pl.multiple_of((2 * chip + pc) * Rp, Rp), Rp), :]


def _scatter_copy(grad, slots, layer, axis, i, k, src_pos, dst_slot, me, send_sems, recv_sems):
    sem = (N_DEV - 1) * i + k - 1
    return pltpu.make_async_remote_copy(
        src_ref=_grad_part(grad, layer, axis, src_pos), dst_ref=slots.at[dst_slot, layer],
        send_sem=send_sems.at[sem], recv_sem=recv_sems.at[sem],
        device_id=_flip(me, k), device_id_type=MESH)


def _scatter_start(grads, slots, layers, axes, *, name):
    n = len(grads)
    n_sem = (N_DEV - 1) * n

    def body(*refs):
        send_sems, recv_sems = refs[2 * n], refs[2 * n + 1]
        g_refs, s_refs = refs[2 * n + 2:3 * n + 2], refs[3 * n + 2:4 * n + 2]
        token = refs[4 * n + 2]
        me = _me()
        my_id = 4 * me[0] + 2 * me[1] + me[2]
        for i in range(n):
            for k in range(1, N_DEV):
                _scatter_copy(g_refs[i], s_refs[i], layers[i], axes[i], i, k, _flip(me, k), my_id, me,
                              send_sems, recv_sems).start()
        token[...] = jnp.zeros_like(token)

    arrays = list(grads) + list(slots)
    outs = pl.pallas_call(
        body, name=name,
        in_specs=[HBM] * (2 * n),
        out_specs=[SEM, SEM] + [HBM] * (2 * n) + [pl.BlockSpec(memory_space=pltpu.VMEM)],
        out_shape=[pltpu.SemaphoreType.DMA((n_sem,)), pltpu.SemaphoreType.DMA((n_sem,))]
        + [pltpu.HBM(a.shape, a.dtype) for a in arrays] + [jax.ShapeDtypeStruct((SUBLANES, LANES), F32)],
        input_output_aliases={i: 2 + i for i in range(2 * n)},
        compiler_params=pltpu.CompilerParams(has_side_effects=EFFECT),
    )(*[pltpu.with_memory_space_constraint(a, pltpu.HBM) for a in arrays])
    return outs[0], outs[1], list(outs[2:2 + n]), list(outs[2 + n:2 + 2 * n]), outs[-1]


def _scatter_wait(sems, grads, slots, axes, plan, after, *, name):
    n = len(grads)
    flat_sems = [s for pair in sems for s in pair]

    def body(*refs):
        sem_refs = refs[2 * n:2 * n + len(flat_sems)]
        first_out = 2 * n + len(flat_sems) + 1
        g_refs, s_refs = refs[first_out:first_out + n], refs[first_out + n:]
        me = _me()
        for j, items in enumerate(plan):
            ss, rs = sem_refs[2 * j], sem_refs[2 * j + 1]
            for i, (a, layer) in enumerate(items):
                for k in range(1, N_DEV):
                    px, py, pc = _flip(me, k)
                    cp = _scatter_copy(g_refs[a], s_refs[a], layer, axes[a], i, k, me, 4 * px + 2 * py + pc, me, ss, rs)
                    cp.wait_send()
                    cp.wait_recv()

    arrays = list(grads) + list(slots)
    outs = pl.pallas_call(
        body, name=name,
        in_specs=[HBM] * (2 * n) + [SEM] * len(flat_sems) + [pl.BlockSpec(memory_space=pl.ANY)],
        out_specs=[HBM] * (2 * n),
        out_shape=[pltpu.HBM(a.shape, a.dtype) for a in arrays],
        input_output_aliases={i: i for i in range(2 * n)},
        compiler_params=pltpu.CompilerParams(has_side_effects=EFFECT),
    )(*arrays, *flat_sems, after)
    return list(outs[:n]), list(outs[n:])


def _small_copy(src, dst, k, slot, me, send_sems, recv_sems):
    return pltpu.make_async_remote_copy(
        src_ref=src, dst_ref=dst.at[slot], send_sem=send_sems.at[k - 1], recv_sem=recv_sems.at[k - 1],
        device_id=_flip(me, k), device_id_type=MESH)


def _small_start(small, slots, after, *, name):
    def body(src_in, dst_in, after_ref, send_sems, recv_sems, src, dst):
        me = _me()
        for k in range(1, N_DEV):
            _small_copy(src, dst, k, 4 * me[0] + 2 * me[1] + me[2], me, send_sems, recv_sems).start()

    arrays = [small, slots]
    return pl.pallas_call(
        body, name=name, in_specs=[HBM, HBM, pl.BlockSpec(memory_space=pl.ANY)], out_specs=[SEM, SEM, HBM, HBM],
        out_shape=[pltpu.SemaphoreType.DMA((N_DEV - 1,)), pltpu.SemaphoreType.DMA((N_DEV - 1,))]
        + [pltpu.HBM(a.shape, a.dtype) for a in arrays],
        input_output_aliases={0: 2, 1: 3},
        compiler_params=pltpu.CompilerParams(has_side_effects=EFFECT),
    )(*[pltpu.with_memory_space_constraint(a, pltpu.HBM) for a in arrays], after)


def _small_wait(send_sems, recv_sems, small, slots, after, *, name):
    def body(src_in, dst_in, ss, rs, after_ref, src, dst):
        me = _me()
        for k in range(1, N_DEV):
            px, py, pc = _flip(me, k)
            cp = _small_copy(src, dst, k, 4 * px + 2 * py + pc, me, ss, rs)
            cp.wait_send()
            cp.wait_recv()

    return pl.pallas_call(
        body, name=name, in_specs=[HBM, HBM, SEM, SEM, pl.BlockSpec(memory_space=pl.ANY)], out_specs=[HBM, HBM],
        out_shape=[pltpu.HBM(small.shape, small.dtype), pltpu.HBM(slots.shape, slots.dtype)],
        input_output_aliases={0: 0, 1: 1},
        compiler_params=pltpu.CompilerParams(has_side_effects=EFFECT),
    )(small, slots, send_sems, recv_sems, after)


def _pair_gather(halves, *, name):
    n = len(halves)

    def body(*refs):
        bufs = refs[n:2 * n]
        send_sems, recv_sems = refs[2 * n:]
        me = _me()
        c = me[2]
        sib = _flip(me, 1)

        def copy(i, half):
            return pltpu.make_async_remote_copy(
                src_ref=bufs[i].at[:, half], dst_ref=bufs[i].at[:, half], send_sem=send_sems.at[i],
                recv_sem=recv_sems.at[i], device_id=sib, device_id_type=MESH)

        sends = [copy(i, c) for i in range(n)]
        for cp in sends:
            cp.start()
        for i in range(n):
            copy(i, 1 - c).wait_recv()
        for cp in sends:
            cp.wait_send()

    return pl.pallas_call(
        body, name=name,
        in_specs=[HBM] * n, out_specs=[HBM] * n,
        out_shape=[jax.ShapeDtypeStruct(h.shape, h.dtype) for h in halves],
        input_output_aliases={i: i for i in range(n)},
        scratch_shapes=[pltpu.SemaphoreType.DMA((n,)), pltpu.SemaphoreType.DMA((n,))],
    )(*halves)


def _local_step(x, tgt, P, fetch, ship, n_layers):
    T, D = x.shape
    P = dict(P)
    W = {}
    depth = P["ffn_conv_b"].shape[0]
    n_a = P["a_w_s"].shape[0]
    alpha = (2 * depth) ** 0.25
    H = P["rel_table"].shape[1] // N_GROUPS
    HD = H * HEAD_DIM
    dils = [d for _, d in DILATED_GROUPS]
    assert T % (dils[-1] * BLK) == 0

    def arrive(stage, after):
        w_new, p_new = fetch(stage, after)
        W.update(w_new)
        P.update(p_new)

    bias = _band_bias(P["rel_table"], H, name="band_bias")
    saved = []
    xf, xb = x, x.astype(BF16)
    kg = vg = None
    for i in range(depth):
        s = {"x": xf, "xb": xb}
        arrive(f"{i}a", xf)
        if i < n_a:
            s["zp"] = _mm(xb, W["a_w_in", i], out_dtype=BF16, name=f"a{i}_in")
            arrive(f"{i}b", s["zp"])
            s["y"] = _sgu_fwd(s["zp"], P["a_w_s"][i], P["a_b_s"][i], P["a_ln_g"][i], P["a_ln_b"][i], name=f"a{i}_sgu")
            s["h"] = _mm(s["y"], W["a_w_out", i], name=f"a{i}_out")
        else:
            j = i - n_a
            if j == 0:
                kv = _mm(xb, W["kv_w", 0], out_dtype=BF16, name="kv_proj")
                kg = [(kv, 0) if d == 1 else (_to_group(kv[:, :HD], d), 0) for d in dils]
                vg = [(kv, 1) if d == 1 else (_to_group(kv[:, HD:], d), 0) for d in dils]
            q = _mm(xb, W["b_w_q", j], out_dtype=BF16, name=f"b{j}_q")
            s["qg"] = [(q, g) if d == 1 else (_to_group(q[:, g * HD:(g + 1) * HD], d), 0) for g, d in enumerate(dils)]
            os, ls = [], []
            for g, d in enumerate(dils):
                o_g, l_g = _attn_fwd(s["qg"][g], kg[g], vg[g], bias, g, d, name=f"b{j}_attn{g}")
                os.append(o_g)
                ls.append(l_g)
            s["oc"], s["ob"], s["lse"] = _attn_combine(os, ls, dils, name=f"b{j}_comb")
            arrive(f"{i}b", s["ob"])
            s["h"] = _mm(s["ob"], W["b_w_o", j], name=f"b{j}_o")
        s["x1"], s["x1b"] = _add_ln(xf, s["h"], P["ln_g"][i, 0], P["ln_b"][i, 0], alpha, name=f"l{i}_ln1")
        arrive(f"{i}c", s["x1b"])
        s["hup"] = _mm(s["x1b"], W["ffn_w_up", i], out_dtype=BF16, name=f"l{i}_up")
        arrive(f"{i}d", s["hup"])
        s["act"], s["ca"], s["cg"] = _convglu_fwd(s["hup"], P["ffn_conv_w"][i], P["ffn_conv_b"][i], name=f"l{i}_glu")
        s["f"] = _mm(s["act"], W["ffn_w_down", i], name=f"l{i}_down")
        xf, xb = _add_ln(s["x1"], s["f"], P["ln_g"][i, 1], P["ln_b"][i, 1], alpha, name=f"l{i}_ln2")
        saved.append(s)

    G, loss_row = _loss_grad(xf, tgt, name="loss")

    gw = {}

    def dw(key, layer, a, b, name, **kw):
        gw[key] = _mm(a, b, ta=True, out_dtype=BF16, out_into=(gw.get(key), n_layers[key], layer), name=name, **kw)

    gp = {k: [None] * n_a for k in ("a_ln_g", "a_ln_b", "a_w_s", "a_b_s")}
    gp.update({k: [None] * depth for k in ("ffn_conv_w", "ffn_conv_b", "ln_g", "ln_b")})
    dk_parts, dv_parts, dbias_parts = [], [], []
    for i in reversed(range(depth)):
        s = saved[i]
        dr2, dr2b, dg2, db2 = _ln_bwd(G, s["x1"], s["f"], P["ln_g"][i, 1], alpha, name=f"l{i}_ln2_bwd")
        dw("ffn_w_down", i, s["act"], dr2b, f"l{i}_down_dw")
        dact = _mm(dr2b, W["ffn_w_down", i], tb=True, out_dtype=BF16, name=f"l{i}_down_dx")
        dhup, dcw, dcb = _convglu_bwd(s["hup"], s["ca"], s["cg"], dact, P["ffn_conv_w"][i], name=f"l{i}_glu_bwd")
        gp["ffn_conv_w"][i] = dcw.transpose(1, 0, 2).reshape(dcw.shape[1], -1)
        gp["ffn_conv_b"][i] = dcb.reshape(-1)
        dw("ffn_w_up", i, s["x1b"], dhup, f"l{i}_up_dw", b_halves=True)
        token = ship(gw, [("ffn_w_down", i), ("ffn_w_up", i)])
        G1 = _mm(dhup, W["ffn_w_up", i], tb=True, a_halves=True, add=dr2, add_scale=alpha, after=token,
                 name=f"l{i}_up_dx")
        dr1, dr1b, dg1, db1 = _ln_bwd(G1, s["x"], s["h"], P["ln_g"][i, 0], alpha, name=f"l{i}_ln1_bwd")
        gp["ln_g"][i] = jnp.concatenate([dg1, dg2], axis=0)
        gp["ln_b"][i] = jnp.concatenate([db1, db2], axis=0)
        if i < n_a:
            dw("a_w_out", i, s["y"], dr1b, f"a{i}_out_dw")
            token = ship(gw, [("a_w_out", i)])
            dy = _mm(dr1b, W["a_w_out", i], tb=True, out_dtype=BF16, after=token, name=f"a{i}_out_dx")
            dzp, dws, dbs, dlg, dlb = _sgu_bwd(s["zp"], dy, P["a_w_s"][i], P["a_b_s"][i], P["a_ln_g"][i],
                                               P["a_ln_b"][i], name=f"a{i}_sgu_bwd")
            gp["a_w_s"][i], gp["a_b_s"][i] = dws, dbs[:, :dws.shape[0]].T
            gp["a_ln_g"][i], gp["a_ln_b"][i] = dlg[0], dlb[0]
            dw("a_w_in", i, s["xb"], dzp, f"a{i}_in_dw")
            token = ship(gw, [("a_w_in", i)])
            G = _mm(dzp, W["a_w_in", i], tb=True, add=dr1, add_scale=alpha, after=token, name=f"a{i}_in_dx")
        else:
            j = i - n_a
            dw("b_w_o", j, s["ob"], dr1b, f"b{j}_o_dw")
            do_tok = _mm(dr1b, W["b_w_o", j], tb=True, name=f"b{j}_o_dx")
            deltas, dobs = _attn_bwd_prep(do_tok, s["oc"], dils, name=f"b{j}_prep")
            dqs, dbs = [], []
            for g, d in enumerate(dils):
                dq, dk, dv, db = _attn_bwd(s["qg"][g], kg[g], vg[g], bias, g, d, dobs[g], s["lse"][g], deltas[g],
                                           name=f"b{j}_attn{g}_bwd")
                dqs.append(_from_group(dq, d))
                dk_parts.append((dk, d))
                dv_parts.append((dv, d))
                dbs.append(db)
            dbias_parts.append(jnp.stack(dbs))
            dq_tok = jnp.concatenate(dqs, axis=1)
            dw("b_w_q", j, s["xb"], dq_tok, f"b{j}_q_dw")
            token = ship(gw, [("b_w_o", j), ("b_w_q", j)])
            G = _mm(dq_tok, W["b_w_q", j], tb=True, add=dr1, add_scale=alpha, after=token, name=f"b{j}_q_dx")
            if j == 0:
                dkv = jnp.concatenate([_addn_grouped(dk_parts, BF16, name="dk_sum"),
                                       _addn_grouped(dv_parts, BF16, name="dv_sum")], axis=1)
                dw("kv_w", 0, s["xb"], dkv, "kv_dw")
                token = ship(gw, [("kv_w", 0)])
                G = _mm(dkv, W["kv_w", 0], tb=True, add=G, add_scale=1.0, after=token, name="kv_dx")
    rel = _rel_grad(dbias_parts, name="rel_grad")
    grel = rel[:, :, :H].transpose(1, 0, 2).reshape(REL_BUCKETS, N_GROUPS * H)
    gsmall = {k: jnp.stack(v) for k, v in gp.items()}
    gsmall["rel_table"] = grel
    return loss_row, G, gw, gsmall


BIG = (("a_w_in", COL), ("a_w_out", ROW), ("kv_w", ROW), ("b_w_q", COL), ("b_w_o", COL),
       ("ffn_w_up", COL), ("ffn_w_down", ROW))
SMALL_SHARDED = ("a_ln_g", "a_ln_b", "ffn_conv_w", "ln_g", "ln_b")
SMALL_REPLICATED = ("a_w_s", "a_b_s", "rel_table", "ffn_conv_b")
WEIGHTS = ("a_w_in", "a_ln_g", "a_ln_b", "a_w_s", "a_b_s", "a_w_out", "kv_w", "b_w_q", "b_w_o", "rel_table",
           "ffn_w_up", "ffn_conv_w", "ffn_conv_b", "ffn_w_down", "ln_g", "ln_b")


def kernel(x, a_w_in, a_ln_g, a_ln_b, a_w_s, a_b_s, a_w_out, kv_w, b_w_q, b_w_o, rel_table, ffn_w_up, ffn_conv_w, ffn_conv_b, ffn_w_down, ln_g, ln_b, loss_target, m_a_w_in, m_a_ln_g, m_a_ln_b, m_a_w_s, m_a_b_s, m_a_w_out, m_kv_w, m_b_w_q, m_b_w_o, m_rel_table, m_ffn_w_up, m_ffn_conv_w, m_ffn_conv_b, m_ffn_w_down, m_ln_g, m_ln_b, v_a_w_in, v_a_ln_g, v_a_ln_b, v_a_w_s, v_a_b_s, v_a_w_out, v_kv_w, v_b_w_q, v_b_w_o, v_rel_table, v_ffn_w_up, v_ffn_conv_w, v_ffn_conv_b, v_ffn_w_down, v_ln_g, v_ln_b):
    w = dict(a_w_in=a_w_in, a_ln_g=a_ln_g, a_ln_b=a_ln_b, a_w_s=a_w_s, a_b_s=a_b_s, a_w_out=a_w_out, kv_w=kv_w,
             b_w_q=b_w_q, b_w_o=b_w_o, rel_table=rel_table, ffn_w_up=ffn_w_up, ffn_conv_w=ffn_conv_w,
             ffn_conv_b=ffn_conv_b, ffn_w_down=ffn_w_down, ln_g=ln_g, ln_b=ln_b)
    m = dict(a_w_in=m_a_w_in, a_ln_g=m_a_ln_g, a_ln_b=m_a_ln_b, a_w_s=m_a_w_s, a_b_s=m_a_b_s, a_w_out=m_a_w_out,
             kv_w=m_kv_w, b_w_q=m_b_w_q, b_w_o=m_b_w_o, rel_table=m_rel_table, ffn_w_up=m_ffn_w_up,
             ffn_conv_w=m_ffn_conv_w, ffn_conv_b=m_ffn_conv_b, ffn_w_down=m_ffn_w_down, ln_g=m_ln_g, ln_b=m_ln_b)
    v = dict(a_w_in=v_a_w_in, a_ln_g=v_a_ln_g, a_ln_b=v_a_ln_b, a_w_s=v_a_w_s, a_b_s=v_a_b_s, a_w_out=v_a_w_out,
             kv_w=v_kv_w, b_w_q=v_b_w_q, b_w_o=v_b_w_o, rel_table=v_rel_table, ffn_w_up=v_ffn_w_up,
             ffn_conv_w=v_ffn_conv_w, ffn_conv_b=v_ffn_conv_b, ffn_w_down=v_ffn_w_down, ln_g=v_ln_g, ln_b=v_ln_b)
    chip = 2 * lax.axis_index("x") + lax.axis_index("y")

    big_names = [n for n, _ in BIG]
    big_axes = [a for _, a in BIG]
    where = jnp.stack([chip, lax.axis_index("c"), 2 * chip + lax.axis_index("c")]).astype(jnp.int32)
    shards = [w[n].reshape((1,) + w[n].shape) if w[n].ndim == 2 else w[n] for n in big_names]
    small_shapes = [w[n].shape for n in SMALL_SHARDED]
    small = _pack([w[n] for n in SMALL_SHARDED])
    shard_of = dict(zip(big_names, shards))
    axis_of = dict(BIG)
    n_layers = {n: s.shape[0] for n, s in shard_of.items()}
    n_a = w["a_w_s"].shape[0]
    stages = {}
    for i in range(w["ffn_w_up"].shape[0]):
        if i < n_a:
            parts = [[("a_w_in", i)], [("a_w_out", i)], [("ffn_w_up", i)], [("ffn_w_down", i)]]
        else:
            parts = [([("kv_w", 0)] if i == n_a else []) + [("b_w_q", i - n_a)], [("b_w_o", i - n_a)],
                     [("ffn_w_up", i)], [("ffn_w_down", i)]]
        if i == 1:
            parts = [parts[0], parts[1] + parts[2] + parts[3]]
        elif i > 1:
            parts = [parts[0] + parts[1] + parts[2] + parts[3]]
        for tag, part in zip("abcd", parts):
            stages[f"{i}{tag}"] = part
    stages["0a"].append(("small", 0))
    assert sum(len(st) for st in stages.values()) == sum(n_layers.values()) + 1
    flights, full_shape, all_started = {}, {}, None
    for batch, names in enumerate(([st for st in stages if st[0] == "0"], [st for st in stages if st[0] != "0"])):
        order = [key for st in names for key in stages[st]]
        placed, axes = [], []
        for n, l in order:
            if n == "small":
                placed.append(_place_shard(small[None], 0, ROW, where, F32, name="place_small"))
                axes.append(ROW)
            else:
                placed.append(_place_shard(shard_of[n], l, axis_of[n], where, BF16, name=f"place_{n}{l}",
                                           after=all_started))
                axes.append(axis_of[n])
                full_shape[n] = (n_layers[n],) + placed[-1].shape[1:]
        send_sems, recv_sems, flying, all_started = _gather_start(placed, axes, name=f"gather_start{batch}")
        for st in names:
            flights[st] = (send_sems, recv_sems, flying, axes, order)

    def fetch(stage, after):
        if stage not in stages:
            return {}, {}
        send_sems, recv_sems, flying, axes, order = flights[stage]
        if stage == "0a":
            after = all_started
        idxs = [order.index(key) for key in stages[stage]]
        landed = _gather_wait(send_sems, recv_sems, [flying[i] for i in idxs], [axes[i] for i in idxs], idxs, after,
                              name=f"gather_wait_{stage}")
        W_new, P_new = {}, {}
        for key, arr in zip(stages[stage], landed):
            if key[0] == "small":
                small_all = arr.reshape((N_CHIPS,) + small.shape)
                per_chip = [_unpack(small_all[j], small_shapes) for j in range(N_CHIPS)]
                for i, n in enumerate(SMALL_SHARDED):
                    P_new[n] = jnp.concatenate([per_chip[j][i] for j in range(N_CHIPS)], axis=-1)
            else:
                W_new[key] = arr.reshape(arr.shape[1:])
        return W_new, P_new

    slots ={n: lax.empty((N_DEV, n_layers[n]) + _part_geometry(full_shape[n], axis_of[n]), BF16) for n in big_names}
    sems, plan, tokens = [], [], []

    def ship(gw, items):
        names = [n for n, _ in items]
        send, recv, grads, landing, token = _scatter_start(
            [gw[n] for n in names], [slots[n] for n in names], [l for _, l in items], [axis_of[n] for n in names],
            name="scatter_start_" + "_".join(f"{n}{l}" for n, l in items))
        for n, g, s in zip(names, grads, landing):
            gw[n], slots[n] = g, s
        sems.append((send, recv))
        plan.append([(big_names.index(n), l) for n, l in items])
        tokens.append(token)
        return token

    loss_row, grad_x, gw, gsmall = _local_step(x[0], loss_target[0], {n: w[n] for n in SMALL_REPLICATED}, fetch,
                                               ship, n_layers)

    small_names = list(SMALL_SHARDED) + list(SMALL_REPLICATED)
    small_pack = _pack([gsmall[n] for n in small_names] + [loss_row[0, :1]])
    small_sems = _small_start(small_pack, lax.empty((N_DEV,) + small_pack.shape, F32), tokens[-1],
                              name="small_start")
    own, landed = _scatter_wait(sems, [gw[n] for n in big_names], [slots[n] for n in big_names], big_axes, plan,
                                grad_x, name="scatter_wait")
    sums = [_sum_parts(b, g, ax, where, name=f"sum_{n}") for (n, ax), b, g in zip(BIG, landed, own)]
    pairs = _pair_gather(sums, name="pair_gather")
    grad = {n: p.reshape(w[n].shape) for n, p in zip(big_names, pairs)}

    delta, new_m, new_v = {}, {}, {}
    for n, _ in BIG:
        delta[n], new_m[n], new_v[n] = _adamw(w[n], grad[n], m[n], v[n], name=f"adamw_{n}")
    small_own, small_slots = _small_wait(*small_sems, new_v[big_names[-1]], name="small_wait")
    small_sum = _sum_small(small_slots, small_own, where, name="sum_small")
    full_small = _unpack(small_sum, [gsmall[n].shape for n in small_names] + [(1,)])
    loss = full_small[-1][0]
    for n, gfull in zip(small_names, full_small):
        if n in SMALL_SHARDED:
            width = w[n].shape[-1]
            grad[n] = lax.dynamic_slice_in_dim(gfull, chip * width, width, axis=-1)
        else:
            grad[n] = gfull
    shapes = [w[n].shape for n in small_names]
    packed = _adamw(_pack([w[n] for n in small_names]), _pack([grad[n] for n in small_names]),
                    _pack([m[n] for n in small_names]), _pack([v[n] for n in small_names]), name="adamw_small")
    for out, res in zip((delta, new_m, new_v), packed):
        for n, a in zip(small_names, _unpack(res, shapes)):
            out[n] = a

    return (loss, grad_x[None], *[grad[n] for n in WEIGHTS], *[delta[n] for n in WEIGHTS],
            *[new_m[n] for n in WEIGHTS], *[new_v[n] for n in WEIGHTS])
```

```python
import functools
import math

import numpy as np
import jax
import jax.numpy as jnp
from jax import lax
from jax.experimental import pallas as pl
from jax.experimental.pallas import tpu as pltpu

F32 = jnp.float32
BF16 = jnp.bfloat16
MESH = pl.DeviceIdType.MESH

CHUNK = 128
HEAD_DIM = 64
DILATED_GROUPS = ((128, 1), (512, 4), (2048, 16))
N_GROUPS = len(DILATED_GROUPS)
BLK = 128
REL_BUCKETS = 32
REL_MAX_DIST = 2048
LN_EPS = 1e-5
NEG = -1e30
ADAM_LR = 0.001
ADAM_B1 = 0.9
ADAM_B2 = 0.999
ADAM_EPS = 1e-08
ADAM_WD = 0.01
ADAM_STEP = 10

N_CHIPS = 4
N_DEV = 8
LANES = 128
SUBLANES = 8
VMEM_LIMIT = 48 * 1024 * 1024

_SQRT_HALF = 0.7071067811865476
_INV_SQRT_2PI = 0.3989422804014327


def _cp(sem=None, vmem=VMEM_LIMIT):
    return pltpu.CompilerParams(dimension_semantics=sem, vmem_limit_bytes=vmem)


def _tile(dim, target, align=LANES):
    if dim <= target:
        return dim
    t = (target // align) * align
    while t >= align:
        if dim % t == 0:
            return t
        t -= align
    return dim


def _gelu(x):
    return 0.5 * x * (1.0 + lax.erf(x * _SQRT_HALF))


def _gelu_grad(x):
    return 0.5 * (1.0 + lax.erf(x * _SQRT_HALF)) + x * (_INV_SQRT_2PI * jnp.exp(-0.5 * x * x))


def _dot(a, b, ca, cb):
    return lax.dot_general(a, b, (((ca,), (cb,)), ((), ())), preferred_element_type=F32)


def _mn_tile(dim):
    return max(_tile(dim, 1024), _tile(dim, 1408))


def _mm(a, b, *, name, ta=False, tb=False, out_dtype=F32, tiles=None,
        a_lead=(), b_lead=(), a_halves=False, b_halves=False, add=None, add_scale=1.0, out_into=None, after=None):
    a2, b2 = a.shape[-2:], b.shape[-2:]
    M, K = (a2[1], a2[0]) if ta else a2
    Kb, N = (b2[1], b2[0]) if tb else b2
    if a_halves:
        K = 2 * K
    if b_halves:
        N = 2 * N
    assert K == Kb, (name, a.shape, b.shape)
    if tiles is None:
        k_target = 1408 if a_halves else (K if K <= 2816 else 2048)
        tm, tn = _mn_tile(M), _mn_tile(N // 2 if b_halves else N)
        tk = _tile(K // 2 if a_halves else K, k_target)
    else:
        tm = _tile(M, tiles[0])
        tn = _tile(N // 2 if b_halves else N, tiles[1])
        tk = _tile(K // 2 if a_halves else K, tiles[2])
    nm, nn, nk = M // tm, N // tn, K // tk
    nkh, nnh = nk // 2, nn // 2

    def lead(idx, rest):
        return tuple(idx) + tuple(rest)

    sq_a = (None,) * (a.ndim - 2)
    sq_b = (None,) * (b.ndim - 2)
    if a_halves:
        assert not ta and a.ndim == 3
        a_spec = pl.BlockSpec((None, tm, tk), lambda i, j, k: (k // nkh, i, k % nkh))
    elif ta:
        a_spec = pl.BlockSpec(sq_a + (tk, tm), lambda i, j, k: lead(a_lead, (k, i)))
    else:
        a_spec = pl.BlockSpec(sq_a + (tm, tk), lambda i, j, k: lead(a_lead, (i, k)))
    if b_halves:
        assert not tb and b.ndim == 3
        b_spec = pl.BlockSpec((None, tk, tn), lambda i, j, k: (j // nnh, k, j % nnh))
    elif tb:
        b_spec = pl.BlockSpec(sq_b + (tn, tk), lambda i, j, k: lead(b_lead, (j, k)))
    else:
        b_spec = pl.BlockSpec(sq_b + (tk, tn), lambda i, j, k: lead(b_lead, (k, j)))
    mn_spec = pl.BlockSpec((tm, tn), lambda i, j, k: (i, j))
    in_specs = [a_spec, b_spec]
    args = [a, b]
    if add is not None:
        in_specs.append(mn_spec)
        args.append(add)
    aliases = {}
    if out_into is None:
        o_spec, out_shape = mn_spec, jax.ShapeDtypeStruct((M, N), out_dtype)
    else:
        buf, n_layers, layer = out_into
        o_spec = pl.BlockSpec((None, tm, tn), lambda i, j, k: (layer, i, j))
        out_shape = jax.ShapeDtypeStruct((n_layers, M, N), out_dtype)
        if buf is not None:
            aliases = {len(args): 0}
            in_specs.append(pl.BlockSpec(memory_space=pl.ANY))
            args.append(buf)
    if after is not None:
        in_specs.append(pl.BlockSpec(memory_space=pl.ANY))
        args.append(after)
    n_in = len(args)
    ca, cb = (0 if ta else 1), (1 if tb else 0)

    def body(*refs):
        a_ref, b_ref = refs[:2]
        add_ref = refs[2] if add is not None else None
        o_ref = refs[n_in]

        def finish(acc):
            if add_ref is not None:
                acc = acc + add_scale * add_ref[...]
            o_ref[...] = acc.astype(o_ref.dtype)

        if nk == 1:
            finish(_dot(a_ref[...], b_ref[...], ca, cb))
        else:
            acc_ref = refs[-1]
            k = pl.program_id(2)

            @pl.when(k == 0)
            def _():
                acc_ref[...] = jnp.zeros_like(acc_ref)

            acc_ref[...] += _dot(a_ref[...], b_ref[...], ca, cb)

            @pl.when(k == nk - 1)
            def _():
                finish(acc_ref[...])

    return pl.pallas_call(
        body, name=name, grid=(nm, nn, nk),
        in_specs=in_specs, out_specs=o_spec, out_shape=out_shape,
        input_output_aliases=aliases,
        scratch_shapes=[] if nk == 1 else [pltpu.VMEM((tm, tn), F32)],
        compiler_params=_cp(("parallel", "parallel", "arbitrary")),
    )(*args)


def _ln_stats(r):
    mu = jnp.mean(r, axis=-1, keepdims=True)
    xc = r - mu
    var = jnp.mean(xc * xc, axis=-1, keepdims=True)
    rstd = lax.rsqrt(var + LN_EPS)
    return xc * rstd, rstd


def _add_ln(x, h, g, b, alpha, *, name):
    T, D = x.shape
    tr = _tile(T, 512, SUBLANES)
    row = pl.BlockSpec((tr, D), lambda i: (i, 0))
    vec = pl.BlockSpec((1, D), lambda i: (0, 0))

    def body(x_ref, h_ref, g_ref, b_ref, o_ref, ob_ref):
        xhat, _ = _ln_stats(alpha * x_ref[...] + h_ref[...])
        y = xhat * g_ref[...] + b_ref[...]
        o_ref[...] = y
        ob_ref[...] = y.astype(BF16)

    return pl.pallas_call(
        body, name=name, grid=(T // tr,),
        in_specs=[row, row, vec, vec], out_specs=[row, row],
        out_shape=[jax.ShapeDtypeStruct((T, D), F32), jax.ShapeDtypeStruct((T, D), BF16)],
        compiler_params=_cp(("parallel",)),
    )(x, h, g.reshape(1, D), b.reshape(1, D))


def _ln_bwd(dy, x, h, g, alpha, *, name):
    T, D = x.shape
    tr = _tile(T, 512, SUBLANES)
    row = pl.BlockSpec((tr, D), lambda i: (i, 0))
    vec = pl.BlockSpec((1, D), lambda i: (0, 0))

    def body(dy_ref, x_ref, h_ref, g_ref, dr_ref, drb_ref, dg_ref, db_ref):
        @pl.when(pl.program_id(0) == 0)
        def _():
            dg_ref[...] = jnp.zeros_like(dg_ref)
            db_ref[...] = jnp.zeros_like(db_ref)

        xhat, rstd = _ln_stats(alpha * x_ref[...] + h_ref[...])
        dyv = dy_ref[...]
        dxhat = dyv * g_ref[...]
        c1 = jnp.mean(dxhat, axis=-1, keepdims=True)
        c2 = jnp.mean(dxhat * xhat, axis=-1, keepdims=True)
        dr = rstd * (dxhat - c1 - xhat * c2)
        dr_ref[...] = dr
        drb_ref[...] = dr.astype(BF16)
        dg_ref[...] += jnp.sum(dyv * xhat, axis=0, keepdims=True)
        db_ref[...] += jnp.sum(dyv, axis=0, keepdims=True)

    return pl.pallas_call(
        body, name=name, grid=(T // tr,),
        in_specs=[row, row, row, vec], out_specs=[row, row, vec, vec],
        out_shape=[jax.ShapeDtypeStruct((T, D), F32), jax.ShapeDtypeStruct((T, D), BF16),
                   jax.ShapeDtypeStruct((1, D), F32), jax.ShapeDtypeStruct((1, D), F32)],
        compiler_params=_cp(("arbitrary",)),
    )(dy, x, h, g.reshape(1, D))


def _sgu_pieces(zp, lg, lb, E):
    z = _gelu(zp)
    u, v = z[:, :E], z[:, E:]
    vhat, rstd = _ln_stats(v)
    vn = vhat * lg + lb
    return u, vhat, rstd, vn


def _tril_mask():
    t = lax.broadcasted_iota(jnp.int32, (CHUNK, CHUNK), 0)
    s = lax.broadcasted_iota(jnp.int32, (CHUNK, CHUNK), 1)
    return s <= t


def _sgu_fwd(zp, ws, bs, lg, lb, *, name):
    T, E2 = zp.shape
    E = E2 // 2
    G = ws.shape[0]
    cw = E // G

    def body(zp_ref, ws_ref, bs_ref, lg_ref, lb_ref, y_ref):
        u, _, _, vn = _sgu_pieces(zp_ref[...].astype(F32), lg_ref[...], lb_ref[...], E)
        vnb = vn.astype(BF16)
        tril = _tril_mask()
        for g in range(G):
            cols = slice(g * cw, (g + 1) * cw)
            w = jnp.where(tril, ws_ref[g], 0.0).astype(BF16)
            sv = _dot(w, vnb[:, cols], 1, 0) + bs_ref[g]
            y_ref[:, cols] = (u[:, cols] * sv).astype(BF16)

    return pl.pallas_call(
        body, name=name, grid=(T // CHUNK,),
        in_specs=[pl.BlockSpec((CHUNK, E2), lambda n: (n, 0)),
                  pl.BlockSpec((G, CHUNK, CHUNK), lambda n: (0, 0, 0)),
                  pl.BlockSpec((G, CHUNK, 1), lambda n: (0, 0, 0)),
                  pl.BlockSpec((1, E), lambda n: (0, 0)),
                  pl.BlockSpec((1, E), lambda n: (0, 0))],
        out_specs=pl.BlockSpec((CHUNK, E), lambda n: (n, 0)),
        out_shape=jax.ShapeDtypeStruct((T, E), BF16),
        compiler_params=_cp(("parallel",)),
    )(zp, ws, bs.reshape(G, CHUNK, 1), lg.reshape(1, E), lb.reshape(1, E))


def _sgu_bwd(zp, dy, ws, bs, lg, lb, *, name):
    T, E2 = zp.shape
    E = E2 // 2
    G = ws.shape[0]
    cw = E // G
    assert G <= LANES

    def body(zp_ref, dy_ref, ws_ref, bs_ref, lg_ref, lb_ref,
             dzp_ref, dws_ref, dbs_ref, dlg_ref, dlb_ref, dvn_ref):
        @pl.when(pl.program_id(0) == 0)
        def _():
            dws_ref[...] = jnp.zeros_like(dws_ref)
            dbs_ref[...] = jnp.zeros_like(dbs_ref)
            dlg_ref[...] = jnp.zeros_like(dlg_ref)
            dlb_ref[...] = jnp.zeros_like(dlb_ref)

        zpv = zp_ref[...].astype(F32)
        lgv = lg_ref[...]
        u, vhat, rstd, vn = _sgu_pieces(zpv, lgv, lb_ref[...], E)
        gp = _gelu_grad(zpv)
        vnb = vn.astype(BF16)
        tril = _tril_mask()
        lane = lax.broadcasted_iota(jnp.int32, (CHUNK, LANES), 1)
        dbs = dbs_ref[...]
        for g in range(G):
            cols = slice(g * cw, (g + 1) * cw)
            w = jnp.where(tril, ws_ref[g], 0.0).astype(BF16)
            sv = _dot(w, vnb[:, cols], 1, 0) + bs_ref[g]
            dyg = dy_ref[:, cols].astype(F32)
            dzp_ref[:, cols] = (dyg * sv * gp[:, cols]).astype(BF16)
            dsv = dyg * u[:, cols]
            dsvb = dsv.astype(BF16)
            dws_ref[g] += jnp.where(tril, _dot(dsvb, vnb[:, cols], 1, 1), 0.0)
            dvn_ref[:, cols] = _dot(w, dsvb, 0, 0)
            dbs = dbs + jnp.where(lane == g, jnp.sum(dsv, axis=1, keepdims=True), 0.0)
        dbs_ref[...] = dbs
        dvn = dvn_ref[...]
        dlg_ref[...] += jnp.sum(dvn * vhat, axis=0, keepdims=True)
        dlb_ref[...] += jnp.sum(dvn, axis=0, keepdims=True)
        dvhat = dvn * lgv
        c1 = jnp.mean(dvhat, axis=-1, keepdims=True)
        c2 = jnp.mean(dvhat * vhat, axis=-1, keepdims=True)
        dv = rstd * (dvhat - c1 - vhat * c2)
        dzp_ref[:, E:] = (dv * gp[:, E:]).astype(BF16)

    vecE = pl.BlockSpec((1, E), lambda n: (0, 0))
    return pl.pallas_call(
        body, name=name, grid=(T // CHUNK,),
        in_specs=[pl.BlockSpec((CHUNK, E2), lambda n: (n, 0)),
                  pl.BlockSpec((CHUNK, E), lambda n: (n, 0)),
                  pl.BlockSpec((G, CHUNK, CHUNK), lambda n: (0, 0, 0)),
                  pl.BlockSpec((G, CHUNK, 1), lambda n: (0, 0, 0)),
                  vecE, vecE],
        out_specs=[pl.BlockSpec((CHUNK, E2), lambda n: (n, 0)),
                   pl.BlockSpec((G, CHUNK, CHUNK), lambda n: (0, 0, 0)),
                   pl.BlockSpec((CHUNK, LANES), lambda n: (0, 0)),
                   vecE, vecE],
        out_shape=[jax.ShapeDtypeStruct((T, E2), BF16),
                   jax.ShapeDtypeStruct((G, CHUNK, CHUNK), F32),
                   jax.ShapeDtypeStruct((CHUNK, LANES), F32),
                   jax.ShapeDtypeStruct((1, E), F32), jax.ShapeDtypeStruct((1, E), F32)],
        scratch_shapes=[pltpu.VMEM((CHUNK, E), F32)],
        compiler_params=_cp(("arbitrary",)),
    )(zp, dy, ws, bs.reshape(G, CHUNK, 1), lg.reshape(1, E), lb.reshape(1, E))


def _delay(h, k):
    t = lax.broadcasted_iota(jnp.int32, h.shape, 0)
    return jnp.where(t >= k, pltpu.roll(h, k, 0), 0.0)


def _advance(d, k):
    T = d.shape[0]
    t = lax.broadcasted_iota(jnp.int32, d.shape, 0)
    return jnp.where(t < T - k, pltpu.roll(d, T - k, 0), 0.0)


def _conv3(h, w_ref, b_ref):
    return w_ref[2:3, :] * h + w_ref[1:2, :] * _delay(h, 1) + w_ref[0:1, :] * _delay(h, 2) + b_ref[...]


def _convglu_fwd(hup, cw, cb, *, name):
    T, F2 = hup.shape
    F = F2 // 2
    tc = LANES
    nt = F // tc

    def body(ha_ref, hg_ref, wa_ref, wg_ref, ba_ref, bg_ref, o_ref, a_ref, g_ref):
        a = _conv3(ha_ref[...].astype(F32), wa_ref, ba_ref)
        g = _conv3(hg_ref[...].astype(F32), wg_ref, bg_ref)
        o_ref[...] = (_gelu(a) * g).astype(BF16)
        a_ref[...] = a.astype(BF16)
        g_ref[...] = g.astype(BF16)

    col = lambda off: pl.BlockSpec((T, tc), lambda j: (0, j + off))
    w3 = lambda off: pl.BlockSpec((3, tc), lambda j: (0, j + off))
    b1 = lambda off: pl.BlockSpec((1, tc), lambda j: (0, j + off))
    shp = jax.ShapeDtypeStruct((T, F), BF16)
    return pl.pallas_call(
        body, name=name, grid=(nt,),
        in_specs=[col(0), col(nt), w3(0), w3(nt), b1(0), b1(nt)],
        out_specs=[col(0)] * 3, out_shape=[shp] * 3,
        compiler_params=_cp(("parallel",)),
    )(hup, hup, cw, cw, cb.reshape(1, F2), cb.reshape(1, F2))


def _convglu_bwd(hup, a, g, dact, cw, *, name):
    T, F2 = hup.shape
    F = F2 // 2
    tc = LANES
    nt = F // tc

    def half(h, w_ref, d, dh_ref, dw_ref, db_ref, i):
        d1, d2 = _advance(d, 1), _advance(d, 2)
        dh_ref[i] = (w_ref[2:3, :] * d + w_ref[1:2, :] * d1 + w_ref[0:1, :] * d2).astype(BF16)
        c = _dot(h, jnp.concatenate([d2, d1, d], axis=1).astype(BF16), 0, 0)
        eye = lax.broadcasted_iota(jnp.int32, (tc, tc), 0) == lax.broadcasted_iota(jnp.int32, (tc, tc), 1)
        for k in range(3):
            dw_ref[i, k:k + 1, :] = jnp.sum(jnp.where(eye, c[:, k * tc:(k + 1) * tc], 0.0), axis=0, keepdims=True)
        db_ref[i] = jnp.sum(d, axis=0, keepdims=True)

    def body(ha_ref, hg_ref, a_ref, g_ref, d_ref, wa_ref, wg_ref, dh_ref, dw_ref, db_ref):
        a = a_ref[...].astype(F32)
        d = d_ref[...].astype(F32)
        half(ha_ref[...].astype(BF16), wa_ref, d * g_ref[...].astype(F32) * _gelu_grad(a), dh_ref, dw_ref, db_ref, 0)
        half(hg_ref[...].astype(BF16), wg_ref, d * _gelu(a), dh_ref, dw_ref, db_ref, 1)

    col = lambda off: pl.BlockSpec((T, tc), lambda j: (0, j + off))
    w3 = lambda off: pl.BlockSpec((3, tc), lambda j: (0, j + off))
    return pl.pallas_call(
        body, name=name, grid=(nt,),
        in_specs=[col(0), col(nt), col(0), col(0), col(0), w3(0), w3(nt)],
        out_specs=[pl.BlockSpec((2, T, tc), lambda j: (0, 0, j)),
                   pl.BlockSpec((2, 3, tc), lambda j: (0, 0, j)),
                   pl.BlockSpec((2, 1, tc), lambda j: (0, 0, j))],
        out_shape=[jax.ShapeDtypeStruct((2, T, F), BF16),
                   jax.ShapeDtypeStruct((2, 3, F), F32),
                   jax.ShapeDtypeStruct((2, 1, F), F32)],
        compiler_params=_cp(("parallel",)),
    )(hup, hup, a, g, dact, cw, cw)


def _bucket_table():
    iq = np.arange(BLK)[:, None]
    ik = np.arange(2 * BLK)[None, :]
    delta = iq + BLK - ik
    out = []
    for win, dil in DILATED_GROUPS:
        valid = (delta >= 0) & (delta <= win // dil)
        n = (np.clip(delta, 0, None) * dil).astype(np.int32)
        max_exact = REL_BUCKETS // 2
        nf = np.maximum(n, 1).astype(np.float32)
        large = max_exact + (np.log(nf / np.float32(max_exact)) / np.float32(math.log(REL_MAX_DIST / max_exact))
                             * np.float32(REL_BUCKETS - max_exact)).astype(np.int32)
        large = np.minimum(large, REL_BUCKETS - 1)
        out.append(np.where(valid, np.where(n < max_exact, n, large), -1))
    return np.stack(out).astype(np.int32)


def _band_bias(rel_table, H, *, name):
    bkt = jnp.asarray(_bucket_table())

    def body(tab_ref, bk_ref, o_ref):
        g = pl.program_id(0)
        bk = bk_ref[...]
        prev = lax.broadcasted_iota(jnp.int32, bk.shape, 1) < BLK
        for h in range(H):
            def bucket(c, acc):
                return jnp.where(bk == c, tab_ref[c, g * H + h], acc)

            b = lax.fori_loop(0, REL_BUCKETS, bucket, jnp.zeros(bk.shape, F32))
            b = jnp.where(bk >= 0, b, NEG)
            o_ref[h, 0] = b
            o_ref[h, 1] = jnp.where(prev, NEG, b)

    return pl.pallas_call(
        body, name=name, grid=(N_GROUPS,),
        in_specs=[pl.BlockSpec(memory_space=pltpu.SMEM),
                  pl.BlockSpec((None, BLK, 2 * BLK), lambda g: (g, 0, 0))],
        out_specs=pl.BlockSpec((None, H, 2, BLK, 2 * BLK), lambda g: (g, 0, 0, 0, 0)),
        out_shape=jax.ShapeDtypeStruct((N_GROUPS, H, 2, BLK, 2 * BLK), F32),
        compiler_params=_cp(("parallel",)),
    )(rel_table, bkt)


PAIR = 2 * HEAD_DIM


def _bdot(a, b, ca, cb):
    return lax.dot_general(a, b, (((ca,), (cb,)), ((0,), (0,))), preferred_element_type=F32)


def _blocks_per_step(nb, dil):
    return 2 if (nb // dil) % 2 == 0 else 1


def _blk(i):
    return slice(i * BLK, (i + 1) * BLK)


def _heads(ref, i, H, own_columns_only):
    low = lax.broadcasted_iota(jnp.int32, (BLK, PAIR), 1) < HEAD_DIM
    out = []
    for h in range(H):
        a = ref[_blk(i), (h // 2) * PAIR:(h // 2 + 1) * PAIR]
        if own_columns_only:
            a = jnp.where(low if h % 2 == 0 else jnp.logical_not(low), a, jnp.zeros_like(a))
        out.append(a)
    return jnp.stack(out)


def _pair_sums(o):
    return [o[2 * p] + o[2 * p + 1] for p in range(o.shape[0] // 2)]


def _store_pairs(ref, i, o, dtype, scale=None):
    for p, v in enumerate(_pair_sums(o)):
        ref[_blk(i), p * PAIR:(p + 1) * PAIR] = (v if scale is None else scale * v).astype(dtype)


def _stat_tile(s):
    lane = lax.broadcasted_iota(jnp.int32, (BLK, LANES), 1)
    t = jnp.zeros((BLK, LANES), F32)
    for h in range(s.shape[0]):
        t = jnp.where(lane == h, s[h], t)
    return t


def _stat_cols(ref, i, H):
    return jnp.stack([ref[_blk(i), h:h + 1] for h in range(H)])


def _scores(qm, kp, kc, bias):
    scale = HEAD_DIM ** -0.5
    sp = _bdot(qm, kp, 2, 2) * scale + bias[:, :, :BLK]
    sc = _bdot(qm, kc, 2, 2) * scale + bias[:, :, BLK:]
    return sp, sc


def _attn_specs(nb, dil, g, H, HD, sb):
    nbg = nb // dil
    QB = _blocks_per_step(nb, dil)
    cur = lambda col: pl.BlockSpec((QB * BLK, HD), lambda b: (sb(b), col))
    prev = lambda col: pl.BlockSpec((BLK, HD), lambda b: (jnp.maximum(QB * sb(b) - 1, 0), col))
    stat = pl.BlockSpec((QB * BLK, LANES), lambda b: (sb(b), 0))
    bias0 = pl.BlockSpec((None, H, None, BLK, 2 * BLK),
                         lambda b: (g, 0, jnp.where((QB * sb(b)) % nbg == 0, 1, 0), 0, 0))
    bias1 = pl.BlockSpec((None, H, None, BLK, 2 * BLK), lambda b: (g, 0, 0, 0, 0))
    return cur, prev, stat, bias0, bias1


def _attn_fwd(q, k, v, bias, g, dil, *, name):
    (qa, qc), (ka, kc_), (va, vc_) = q, k, v
    T = qa.shape[0]
    H = bias.shape[1]
    HD = H * HEAD_DIM
    assert H % 2 == 0 and H <= LANES and T % (dil * BLK) == 0
    nb = T // BLK
    QB = _blocks_per_step(nb, dil)
    cur, prev, stat, bias0, bias1 = _attn_specs(nb, dil, g, H, HD, lambda b: b)

    def body(q_ref, kp_ref, kc_ref, vp_ref, vc_ref, b0_ref, b1_ref, o_ref, l_ref):
        for i in range(QB):
            kpi, vpi = (kp_ref, vp_ref) if i == 0 else (kc_ref, vc_ref)
            sp, sc = _scores(_heads(q_ref, i, H, True), _heads(kpi, max(i - 1, 0), H, False),
                             _heads(kc_ref, i, H, False), (b0_ref if i == 0 else b1_ref)[...])
            m = jnp.maximum(jnp.max(sp, axis=-1, keepdims=True), jnp.max(sc, axis=-1, keepdims=True))
            pp, pc = jnp.exp(sp - m), jnp.exp(sc - m)
            den = jnp.sum(pp, axis=-1, keepdims=True) + jnp.sum(pc, axis=-1, keepdims=True)
            o = (_bdot((pp / den).astype(BF16), _heads(vpi, max(i - 1, 0), H, True), 2, 1)
                 + _bdot((pc / den).astype(BF16), _heads(vc_ref, i, H, True), 2, 1))
            _store_pairs(o_ref, i, o, F32)
            l_ref[_blk(i), :] = _stat_tile(m + jnp.log(den))

    return pl.pallas_call(
        body, name=name, grid=(nb // QB,),
        in_specs=[cur(qc), prev(kc_), cur(kc_), prev(vc_), cur(vc_), bias0, bias1],
        out_specs=[cur(0), stat],
        out_shape=[jax.ShapeDtypeStruct((T, HD), F32), jax.ShapeDtypeStruct((T, LANES), F32)],
        compiler_params=_cp(("parallel",)),
    )(qa, ka, ka, va, va, bias, bias)


def _grouped_spec(tr, d, C):
    return pl.BlockSpec((d, tr // d, C), lambda i: (0, i, 0))


def _as_grouped(a, d):
    T, C = a.shape
    return a.reshape(d, T // d, C)


def _to_token_slabs(ref, slabs, d):
    n_r = ref.shape[1]
    for j in range(ref.shape[2] // LANES):
        for r in range(d):
            slabs.at[j][pl.ds(r, n_r, stride=d), :] = ref[r, :, j * LANES:(j + 1) * LANES].astype(F32)


def _from_token_slabs(slabs, ref, d):
    n_r = ref.shape[1]
    for j in range(ref.shape[2] // LANES):
        for r in range(d):
            ref[r, :, j * LANES:(j + 1) * LANES] = slabs.at[j][pl.ds(r, n_r, stride=d), :].astype(ref.dtype)


def _attn_combine(os, ls, dils, *, name):
    n = len(os)
    T, HD = os[0].shape
    H = HD // HEAD_DIM
    tr = _tile(T, 512, SUBLANES)
    n_sl = HD // LANES
    assert PAIR == LANES

    def body(*refs):
        o_refs, l_refs = refs[:n], refs[n:2 * n]
        oc_ref, ocb_ref = refs[2 * n:2 * n + 2]
        lt_refs = refs[2 * n + 2:3 * n + 2]
        o_scr, l_scr = refs[3 * n + 2], refs[3 * n + 3]
        for g in range(n):
            _to_token_slabs(o_refs[g], o_scr.at[g], dils[g])
            _to_token_slabs(l_refs[g], l_scr.at[g], dils[g])
        ls_ = [l_scr[g, 0] for g in range(n)]
        m = ls_[0]
        for l in ls_[1:]:
            m = jnp.maximum(m, l)
        ws = [jnp.exp(l - m) for l in ls_]
        den = ws[0]
        for w in ws[1:]:
            den = den + w
        l_scr[n, 0] = m + jnp.log(den)
        for g in range(n):
            _from_token_slabs(l_scr.at[n], lt_refs[g], dils[g])
        ws = [w / den for w in ws]
        low = lax.broadcasted_iota(jnp.int32, (tr, PAIR), 1) < HEAD_DIM
        for p in range(H // 2):
            acc = jnp.zeros((tr, PAIR), F32)
            for g in range(n):
                w = jnp.where(low, ws[g][:, 2 * p:2 * p + 1], ws[g][:, 2 * p + 1:2 * p + 2])
                acc = acc + w * o_scr[g, p]
            oc_ref[:, p * PAIR:(p + 1) * PAIR] = acc
            ocb_ref[:, p * PAIR:(p + 1) * PAIR] = acc.astype(BF16)

    wide = pl.BlockSpec((tr, HD), lambda i: (i, 0))
    lts = pl.pallas_call(
        body, name=name, grid=(T // tr,),
        in_specs=[_grouped_spec(tr, d, HD) for d in dils] + [_grouped_spec(tr, d, LANES) for d in dils],
        out_specs=[wide, wide] + [_grouped_spec(tr, d, LANES) for d in dils],
        out_shape=[jax.ShapeDtypeStruct((T, HD), F32), jax.ShapeDtypeStruct((T, HD), BF16)]
        + [jax.ShapeDtypeStruct((d, T // d, LANES), F32) for d in dils],
        scratch_shapes=[pltpu.VMEM((n, n_sl, tr, LANES), F32), pltpu.VMEM((n + 1, 1, tr, LANES), F32)],
        compiler_params=_cp(("parallel",)),
    )(*[_as_grouped(o, d) for o, d in zip(os, dils)], *[_as_grouped(l, d) for l, d in zip(ls, dils)])
    return lts[0], lts[1], [l.reshape(T, LANES) for l in lts[2:]]


def _attn_bwd_prep(do, oc, dils, *, name):
    T, HD = do.shape
    H = HD // HEAD_DIM
    n = len(dils)
    tr = _tile(T, 512, SUBLANES)
    n_sl = HD // LANES
    wide = pl.BlockSpec((tr, HD), lambda i: (i, 0))

    def body(*refs):
        do_ref, oc_ref = refs[:2]
        dl_refs, dob_refs = refs[2:2 + n], refs[2 + n:2 + 2 * n]
        do_scr, dl_scr = refs[2 + 2 * n], refs[3 + 2 * n]
        low = lax.broadcasted_iota(jnp.int32, (tr, PAIR), 1) < HEAD_DIM
        lane = lax.broadcasted_iota(jnp.int32, (tr, LANES), 1)
        dl = jnp.zeros((tr, LANES), F32)
        for p in range(H // 2):
            d = do_ref[:, p * PAIR:(p + 1) * PAIR]
            do_scr[p] = d
            prod = d * oc_ref[:, p * PAIR:(p + 1) * PAIR]
            dl = jnp.where(lane == 2 * p, jnp.sum(jnp.where(low, prod, 0.0), axis=-1, keepdims=True), dl)
            dl = jnp.where(lane == 2 * p + 1, jnp.sum(jnp.where(low, 0.0, prod), axis=-1, keepdims=True), dl)
        dl_scr[0] = dl
        for g in range(n):
            _from_token_slabs(dl_scr, dl_refs[g], dils[g])
            _from_token_slabs(do_scr, dob_refs[g], dils[g])

    outs = pl.pallas_call(
        body, name=name, grid=(T // tr,),
        in_specs=[wide, wide],
        out_specs=[_grouped_spec(tr, d, LANES) for d in dils] + [_grouped_spec(tr, d, HD) for d in dils],
        out_shape=[jax.ShapeDtypeStruct((d, T // d, LANES), F32) for d in dils]
        + [jax.ShapeDtypeStruct((d, T // d, HD), BF16) for d in dils],
        scratch_shapes=[pltpu.VMEM((n_sl, tr, LANES), F32), pltpu.VMEM((1, tr, LANES), F32)],
        compiler_params=_cp(("parallel",)),
    )(do, oc)
    return [a.reshape(T, LANES) for a in outs[:n]], [a.reshape(T, HD) for a in outs[n:]]


def _attn_bwd(q, k, v, bias, g, dil, do, lse, delta, *, name):
    (qa, qc), (ka, kc_), (va, vc_) = q, k, v
    T, HD = do.shape
    H = HD // HEAD_DIM
    nb = T // BLK
    scale = HEAD_DIM ** -0.5
    QB = _blocks_per_step(nb, dil)
    ns = nb // QB
    cur, prev, stat, bias0, bias1 = _attn_specs(nb, dil, g, H, HD, lambda b: jnp.minimum(b, ns - 1))
    late = pl.BlockSpec((QB * BLK, HD), lambda b: (jnp.maximum(b - 1, 0), 0))

    def body(q_ref, kp_ref, kc_ref, vp_ref, vc_ref, b0_ref, b1_ref, do_ref, l_ref, dl_ref,
             dq_ref, dk_ref, dv_ref, db_ref, ck_ref, cv_ref):
        b = pl.program_id(0)

        @pl.when(b == 0)
        def _():
            db_ref[...] = jnp.zeros_like(db_ref)
            ck_ref[...] = jnp.zeros_like(ck_ref)
            cv_ref[...] = jnp.zeros_like(cv_ref)

        @pl.when(b < ns)
        def _():
            dk_ref[...] = ck_ref[...]
            dv_ref[...] = cv_ref[...]
            for i in range(QB):
                kpi, vpi, ip = (kp_ref, vp_ref, 0) if i == 0 else (kc_ref, vc_ref, i - 1)
                qm, dom = _heads(q_ref, i, H, True), _heads(do_ref, i, H, True)
                sp, sc = _scores(qm, _heads(kpi, ip, H, False), _heads(kc_ref, i, H, False),
                                 (b0_ref if i == 0 else b1_ref)[...])
                l, dl = _stat_cols(l_ref, i, H), _stat_cols(dl_ref, i, H)
                pp, pc = jnp.exp(sp - l), jnp.exp(sc - l)
                dsp = pp * (_bdot(dom, _heads(vpi, ip, H, False), 2, 2) - dl)
                dsc = pc * (_bdot(dom, _heads(vc_ref, i, H, False), 2, 2) - dl)
                db_ref[:, :, :BLK] += dsp
                db_ref[:, :, BLK:] += dsc
                dspb, dscb = dsp.astype(BF16), dsc.astype(BF16)
                _store_pairs(dq_ref, i, _bdot(dspb, _heads(kpi, ip, H, True), 2, 1)
                             + _bdot(dscb, _heads(kc_ref, i, H, True), 2, 1), BF16, scale)
                dkp = _pair_sums(_bdot(dspb, qm, 1, 1))
                dkc = _pair_sums(_bdot(dscb, qm, 1, 1))
                dvp = _pair_sums(_bdot(pp.astype(BF16), dom, 1, 1))
                dvc = _pair_sums(_bdot(pc.astype(BF16), dom, 1, 1))
                for p in range(H // 2):
                    cols = slice(p * PAIR, (p + 1) * PAIR)
                    if i == 0:
                        dk_ref[_blk(QB - 1), cols] += scale * dkp[p]
                        dv_ref[_blk(QB - 1), cols] += dvp[p]
                    else:
                        ck_ref[_blk(i - 1), cols] += scale * dkp[p]
                        cv_ref[_blk(i - 1), cols] += dvp[p]
                    ck_ref[_blk(i), cols] = scale * dkc[p]
                    cv_ref[_blk(i), cols] = dvc[p]

        @pl.when(b == ns)
        def _():
            dk_ref[...] = ck_ref[...]
            dv_ref[...] = cv_ref[...]

    f32 = jax.ShapeDtypeStruct((T, HD), F32)
    return pl.pallas_call(
        body, name=name, grid=(ns + 1,),
        in_specs=[cur(qc), prev(kc_), cur(kc_), prev(vc_), cur(vc_), bias0, bias1, cur(0), stat, stat],
        out_specs=[cur(0), late, late, pl.BlockSpec((H, BLK, 2 * BLK), lambda b: (0, 0, 0))],
        out_shape=[jax.ShapeDtypeStruct((T, HD), BF16), f32, f32,
                   jax.ShapeDtypeStruct((H, BLK, 2 * BLK), F32)],
        scratch_shapes=[pltpu.VMEM((QB * BLK, HD), F32), pltpu.VMEM((QB * BLK, HD), F32)],
        compiler_params=_cp(("arbitrary",)),
    )(qa, ka, ka, va, va, bias, bias, do, lse, delta)


def _rel_grad(dbs, *, name):
    H = dbs[0].shape[1]
    n = len(dbs)
    assert H <= LANES
    bkt = jnp.asarray(_bucket_table())

    def body(*refs):
        bk = refs[n][...]
        o_ref = refs[n + 1]
        db = refs[0][...]
        for r in refs[1:n]:
            db = db + r[...]
        row = lax.broadcasted_iota(jnp.int32, (REL_BUCKETS, LANES), 0)
        col = lax.broadcasted_iota(jnp.int32, (REL_BUCKETS, LANES), 1)

        def bucket(c, acc):
            sel = bk == c
            for h in range(H):
                val = jnp.sum(jnp.where(sel, db[h], 0.0))
                acc = jnp.where((row == c) & (col == h), val, acc)
            return acc

        o_ref[...] = lax.fori_loop(0, REL_BUCKETS, bucket, jnp.zeros((REL_BUCKETS, LANES), F32))

    dspec = pl.BlockSpec((None, H, BLK, 2 * BLK), lambda g: (g, 0, 0, 0))
    return pl.pallas_call(
        body, name=name, grid=(N_GROUPS,),
        in_specs=[dspec] * n + [pl.BlockSpec((None, BLK, 2 * BLK), lambda g: (g, 0, 0))],
        out_specs=pl.BlockSpec((None, REL_BUCKETS, LANES), lambda g: (g, 0, 0)),
        out_shape=jax.ShapeDtypeStruct((N_GROUPS, REL_BUCKETS, LANES), F32),
        compiler_params=_cp(("parallel",)),
    )(*dbs, bkt)


def _to_group(a, d):
    T, C = a.shape
    return a if d == 1 else a.reshape(T // d, d, C).transpose(1, 0, 2).reshape(T, C)


def _from_group(a, d):
    T, C = a.shape
    return a if d == 1 else a.reshape(d, T // d, C).transpose(1, 0, 2).reshape(T, C)


def _rows_view(a):
    a2 = a.reshape(-1, a.shape[-1])
    R, C = a2.shape
    tr = _tile(R, max(SUBLANES, (1 << 18) // C), SUBLANES)
    return a2, R, C, tr


def _addn(xs, out_dtype, *, name):
    shape = xs[0].shape
    x2s = [_rows_view(x)[0] for x in xs]
    _, R, C, tr = _rows_view(xs[0])
    spec = pl.BlockSpec((tr, C), lambda i: (i, 0))

    def body(*refs):
        acc = refs[0][...].astype(F32)
        for r in refs[1:-1]:
            acc = acc + r[...].astype(F32)
        refs[-1][...] = acc.astype(out_dtype)

    out = pl.pallas_call(
        body, name=name, grid=(R // tr,),
        in_specs=[spec] * len(xs), out_specs=spec,
        out_shape=jax.ShapeDtypeStruct((R, C), out_dtype),
        compiler_params=_cp(("parallel",)),
    )(*x2s)
    return out.reshape(shape)


def _addn_grouped(parts, out_dtype, *, name):
    T, C = parts[0][0].shape
    tr = _tile(T, 512, SUBLANES)
    assert all(tr % d == 0 for _, d in parts)

    assert C % LANES == 0

    def body(*refs):
        o_ref, acc_ref = refs[-2], refs[-1]
        acc_ref[...] = jnp.zeros_like(acc_ref)
        for ref, (_, d) in zip(refs, parts):
            for r in range(d):
                for j in range(C // LANES):
                    acc_ref.at[j][pl.ds(r, tr // d, stride=d), :] += ref[r, :, j * LANES:(j + 1) * LANES]
        for j in range(C // LANES):
            o_ref[:, j * LANES:(j + 1) * LANES] = acc_ref[j].astype(out_dtype)

    return pl.pallas_call(
        body, name=name, grid=(T // tr,),
        in_specs=[pl.BlockSpec((d, tr // d, C), lambda i: (0, i, 0)) for _, d in parts],
        out_specs=pl.BlockSpec((tr, C), lambda i: (i, 0)),
        out_shape=jax.ShapeDtypeStruct((T, C), out_dtype),
        scratch_shapes=[pltpu.VMEM((C // LANES, tr, LANES), F32)],
        compiler_params=_cp(("parallel",)),
    )(*[a.reshape(d, T // d, C) for a, d in parts])


def _slot_sum(b_ref, own, my_id):
    acc = None
    for s in range(b_ref.shape[0]):
        term = jnp.where(my_id == s, own, b_ref[s].astype(F32))
        acc = term if acc is None else acc + term
    return acc


def _part_geometry(shape, axis):
    _, R, C = shape
    if axis == COL:
        return R // 2, C // N_CHIPS
    return R // (2 * N_CHIPS), C


def _sum_parts(buf, grad, axis, where, *, name):
    n, L, Rp, Cp = buf.shape
    assert (Rp, Cp) == _part_geometry(grad.shape, axis)
    tr = _tile(Rp, max(2 * SUBLANES, (1 << 17) // Cp), 2 * SUBLANES)
    nr = Rp // tr

    def own_map(l, i, w):
        if axis == COL:
            return (l, w[1] * nr + i, w[0])
        return (l, (2 * w[0] + w[1]) * nr + i, 0)

    def body(w_ref, b_ref, g_ref, o_ref):
        o_ref[...] = _slot_sum(b_ref, g_ref[...].astype(F32), w_ref[2])

    return pl.pallas_call(
        body, name=name,
        grid_spec=pltpu.PrefetchScalarGridSpec(
            num_scalar_prefetch=1, grid=(L, nr),
            in_specs=[pl.BlockSpec((n, None, tr, Cp), lambda l, i, w: (0, l, i, 0)),
                      pl.BlockSpec((None, tr, Cp), own_map)],
            out_specs=pl.BlockSpec((None, None, tr, Cp), lambda l, i, w: (l, w[1], i, 0))),
        out_shape=jax.ShapeDtypeStruct((L, 2, Rp, Cp), F32),
        compiler_params=_cp(("parallel", "parallel")),
    )(where, buf, grad)


def _sum_small(buf, own, where, *, name):
    n, R, C = buf.shape
    tr = _tile(R, 512, SUBLANES)

    def body(w_ref, b_ref, g_ref, o_ref):
        o_ref[...] = _slot_sum(b_ref, g_ref[...], w_ref[2])

    return pl.pallas_call(
        body, name=name,
        grid_spec=pltpu.PrefetchScalarGridSpec(
            num_scalar_prefetch=1, grid=(R // tr,),
            in_specs=[pl.BlockSpec((n, tr, C), lambda i, w: (0, i, 0)),
                      pl.BlockSpec((tr, C), lambda i, w: (i, 0))],
            out_specs=pl.BlockSpec((tr, C), lambda i, w: (i, 0))),
        out_shape=jax.ShapeDtypeStruct((R, C), F32),
        compiler_params=_cp(("parallel",)),
    )(where, buf, own)


def _place_shard(shard, layer, axis, where, out_dtype, *, name, after=None):
    _, R, C = shard.shape
    tr = _tile(R, max(2 * SUBLANES, (1 << 18) // C), 2 * SUBLANES)
    nr = R // tr
    if axis == COL:
        full, out_map = (1, R, C * N_CHIPS), (lambda i, w: (0, i, w[0]))
    else:
        full, out_map = (1, R * N_CHIPS, C), (lambda i, w: (0, w[0] * nr + i, 0))

    def body(w_ref, x_ref, *rest):
        rest[-1][...] = x_ref[...].astype(out_dtype)

    extra = [] if after is None else [after]
    return pl.pallas_call(
        body, name=name,
        grid_spec=pltpu.PrefetchScalarGridSpec(
            num_scalar_prefetch=1, grid=(nr,),
            in_specs=[pl.BlockSpec((None, tr, C), lambda i, w: (layer, i, 0))]
            + [pl.BlockSpec(memory_space=pl.ANY)] * len(extra),
            out_specs=pl.BlockSpec((None, tr, C), out_map)),
        out_shape=jax.ShapeDtypeStruct(full, out_dtype),
        compiler_params=_cp(("parallel",)),
    )(where, shard, *extra)


def _loss_grad(y, tgt, *, name):
    T, D = y.shape
    tr = _tile(T, 512, SUBLANES)
    row = pl.BlockSpec((tr, D), lambda i: (i, 0))

    def body(y_ref, t_ref, dy_ref, l_ref, acc_ref):
        i = pl.program_id(0)

        @pl.when(i == 0)
        def _():
            acc_ref[...] = jnp.zeros_like(acc_ref)

        err = y_ref[...] - t_ref[...]
        dy_ref[...] = err * (1.0 / D)
        acc_ref[...] += jnp.sum(err * err, axis=0, keepdims=True)

        @pl.when(i == T // tr - 1)
        def _():
            tot = jnp.sum(acc_ref[...], axis=1, keepdims=True) * (0.5 / D)
            l_ref[...] = jnp.broadcast_to(tot, (1, LANES))

    return pl.pallas_call(
        body, name=name, grid=(T // tr,),
        in_specs=[row, row],
        out_specs=[row, pl.BlockSpec((1, LANES), lambda i: (0, 0))],
        out_shape=[jax.ShapeDtypeStruct((T, D), F32), jax.ShapeDtypeStruct((1, LANES), F32)],
        scratch_shapes=[pltpu.VMEM((1, D), F32)],
        compiler_params=_cp(("arbitrary",)),
    )(y, tgt)


def _adamw(w, g, m, v, *, name):
    shape = w.shape
    w2, R, C, tr = _rows_view(w)
    spec = pl.BlockSpec((tr, C), lambda i: (i, 0))

    def body(w_ref, g_ref, m_ref, v_ref, d_ref, nm_ref, nv_ref):
        gv = g_ref[...]
        nm = ADAM_B1 * m_ref[...] + (1.0 - ADAM_B1) * gv
        nv = ADAM_B2 * v_ref[...] + (1.0 - ADAM_B2) * (gv * gv)
        m_hat = nm / (1.0 - ADAM_B1 ** ADAM_STEP)
        v_hat = nv / (1.0 - ADAM_B2 ** ADAM_STEP)
        d_ref[...] = -ADAM_LR * (m_hat / (jnp.sqrt(v_hat) + ADAM_EPS) + ADAM_WD * w_ref[...])
        nm_ref[...] = nm
        nv_ref[...] = nv

    shp = jax.ShapeDtypeStruct((R, C), F32)
    outs = pl.pallas_call(
        body, name=name, grid=(R // tr,),
        in_specs=[spec] * 4, out_specs=[spec] * 3, out_shape=[shp] * 3,
        compiler_params=_cp(("parallel",)),
    )(w2, g.reshape(R, C), m.reshape(R, C), v.reshape(R, C))
    return tuple(o.reshape(shape) for o in outs)


def _pack(arrs):
    flat = jnp.concatenate([a.reshape(-1).astype(F32) for a in arrs])
    unit = SUBLANES * LANES
    pad = (-flat.shape[0]) % unit
    return jnp.pad(flat, (0, pad)).reshape(-1, LANES)


def _unpack(buf, shapes):
    flat = buf.reshape(-1)
    out, off = [], 0
    for s in shapes:
        n = int(np.prod(s))
        out.append(flat[off:off + n].reshape(s))
        off += n
    return out


def _me():
    return lax.axis_index("x"), lax.axis_index("y"), lax.axis_index("c")


def _flip(pos, k):
    x, y, c = pos
    return (1 - x if k & 4 else x, 1 - y if k & 2 else y, 1 - c if k & 1 else c)


HBM = pl.BlockSpec(memory_space=pltpu.HBM)

COL, ROW = -1, -2


def _shard_window(ref, axis, j, n):
    start = pl.multiple_of(j * n, n)
    if axis == COL:
        return ref.at[:, :, pl.ds(start, n)]
    return ref.at[:, pl.ds(start, n), :]


GATHER_PEERS = (2, 4, 6)
SEM = pl.BlockSpec(memory_space=pltpu.SEMAPHORE)
EFFECT = pltpu.SideEffectType.DATAFLOW_SIDE_EFFECTING


def _gather_copy(buf, axis, i, pi, chip, me, send_sems, recv_sems):
    win = _shard_window(buf, axis, chip, buf.shape[axis] // N_CHIPS)
    k = len(GATHER_PEERS) * i + pi
    return pltpu.make_async_remote_copy(
        src_ref=win, dst_ref=win, send_sem=send_sems.at[k], recv_sem=recv_sems.at[k],
        device_id=_flip(me, GATHER_PEERS[pi]), device_id_type=MESH)


def _gather_start(fulls, axes, *, name):
    n = len(fulls)
    n_sem = len(GATHER_PEERS) * n

    def body(*refs):
        send_sems, recv_sems = refs[n], refs[n + 1]
        bufs = refs[n + 2:2 * n + 2]
        token = refs[2 * n + 2]
        me = _me()
        for i in range(n):
            for pi in range(len(GATHER_PEERS)):
                _gather_copy(bufs[i], axes[i], i, pi, 2 * me[0] + me[1], me, send_sems, recv_sems).start()
        token[...] = jnp.zeros_like(token)

    outs = pl.pallas_call(
        body, name=name,
        in_specs=[HBM] * n, out_specs=[SEM, SEM] + [HBM] * n + [pl.BlockSpec(memory_space=pltpu.VMEM)],
        out_shape=[pltpu.SemaphoreType.DMA((n_sem,)), pltpu.SemaphoreType.DMA((n_sem,))]
        + [pltpu.HBM(f.shape, f.dtype) for f in fulls] + [jax.ShapeDtypeStruct((SUBLANES, LANES), F32)],
        input_output_aliases={i: 2 + i for i in range(n)},
        compiler_params=pltpu.CompilerParams(has_side_effects=EFFECT),
    )(*[pltpu.with_memory_space_constraint(f, pltpu.HBM) for f in fulls])
    return outs[0], outs[1], list(outs[2:2 + n]), outs[-1]


def _gather_wait(send_sems, recv_sems, bufs, axes, idxs, after, *, name):
    m = len(bufs)

    def body(*refs):
        ss, rs = refs[m], refs[m + 1]
        outs = refs[m + 3:]
        me = _me()
        for t, i in enumerate(idxs):
            for pi, k in enumerate(GATHER_PEERS):
                px, py, _ = _flip(me, k)
                _gather_copy(outs[t], axes[t], i, pi, 2 * me[0] + me[1], me, ss, rs).wait_send()
                _gather_copy(outs[t], axes[t], i, pi, 2 * px + py, me, ss, rs).wait_recv()

    return pl.pallas_call(
        body, name=name,
        in_specs=[HBM] * m + [SEM, SEM, pl.BlockSpec(memory_space=pl.ANY)], out_specs=[HBM] * m,
        out_shape=[pltpu.HBM(b.shape, b.dtype) for b in bufs],
        input_output_aliases={t: t for t in range(m)},
        compiler_params=pltpu.CompilerParams(has_side_effects=EFFECT),
    )(*bufs, send_sems, recv_sems, after)


def _grad_part(ref, layer, axis, pos):
    px, py, pc = pos
    Rp, Cp = _part_geometry(ref.shape, axis)
    chip = 2 * px + py
    if axis == COL:
        return ref.at[layer, pl.ds(pl.multiple_of(pc * Rp, Rp), Rp), pl.ds(pl.multiple_of(chip * Cp, Cp), Cp)]
    return ref.at[layer, pl.ds(pl.multiple_of((2 * chip + pc) * Rp, Rp), Rp), :]


def _scatter_copy(grad, slots, layer, axis, i, k, src_pos, dst_slot, me, send_sems, recv_sems):
    sem = (N_DEV - 1) * i + k - 1
    return pltpu.make_async_remote_copy(
        src_ref=_grad_part(grad, layer, axis, src_pos), dst_ref=slots.at[dst_slot, layer],
        send_sem=send_sems.at[sem], recv_sem=recv_sems.at[sem],
        device_id=_flip(me, k), device_id_type=MESH)


def _scatter_start(grads, slots, layers, axes, *, name):
    n = len(grads)
    n_sem = (N_DEV - 1) * n

    def body(*refs):
        send_sems, recv_sems = refs[2 * n], refs[2 * n + 1]
        g_refs, s_refs = refs[2 * n + 2:3 * n + 2], refs[3 * n + 2:4 * n + 2]
        token = refs[4 * n + 2]
        me = _me()
        my_id = 4 * me[0] + 2 * me[1] + me[2]
        for i in range(n):
            for k in range(1, N_DEV):
                _scatter_copy(g_refs[i], s_refs[i], layers[i], axes[i], i, k, _flip(me, k), my_id, me,
                              send_sems, recv_sems).start()
        token[...] = jnp.zeros_like(token)

    arrays = list(grads) + list(slots)
    outs = pl.pallas_call(
        body, name=name,
        in_specs=[HBM] * (2 * n),
        out_specs=[SEM, SEM] + [HBM] * (2 * n) + [pl.BlockSpec(memory_space=pltpu.VMEM)],
        out_shape=[pltpu.SemaphoreType.DMA((n_sem,)), pltpu.SemaphoreType.DMA((n_sem,))]
        + [pltpu.HBM(a.shape, a.dtype) for a in arrays] + [jax.ShapeDtypeStruct((SUBLANES, LANES), F32)],
        input_output_aliases={i: 2 + i for i in range(2 * n)},
        compiler_params=pltpu.CompilerParams(has_side_effects=EFFECT),
    )(*[pltpu.with_memory_space_constraint(a, pltpu.HBM) for a in arrays])
    return outs[0], outs[1], list(outs[2:2 + n]), list(outs[2 + n:2 + 2 * n]), outs[-1]


def _scatter_wait(sems, grads, slots, axes, plan, after, *, name):
    n = len(grads)
    flat_sems = [s for pair in sems for s in pair]

    def body(*refs):
        sem_refs = refs[2 * n:2 * n + len(flat_sems)]
        first_out = 2 * n + len(flat_sems) + 1
        g_refs, s_refs = refs[first_out:first_out + n], refs[first_out + n:]
        me = _me()
        for j, items in enumerate(plan):
            ss, rs = sem_refs[2 * j], sem_refs[2 * j + 1]
            for i, (a, layer) in enumerate(items):
                for k in range(1, N_DEV):
                    px, py, pc = _flip(me, k)
                    cp = _scatter_copy(g_refs[a], s_refs[a], layer, axes[a], i, k, me, 4 * px + 2 * py + pc, me, ss, rs)
                    cp.wait_send()
                    cp.wait_recv()

    arrays = list(grads) + list(slots)
    outs = pl.pallas_call(
        body, name=name,
        in_specs=[HBM] * (2 * n) + [SEM] * len(flat_sems) + [pl.BlockSpec(memory_space=pl.ANY)],
        out_specs=[HBM] * (2 * n),
        out_shape=[pltpu.HBM(a.shape, a.dtype) for a in arrays],
        input_output_aliases={i: i for i in range(2 * n)},
        compiler_params=pltpu.CompilerParams(has_side_effects=EFFECT),
    )(*arrays, *flat_sems, after)
    return list(outs[:n]), list(outs[n:])


def _small_copy(src, dst, k, slot, me, send_sems, recv_sems):
    return pltpu.make_async_remote_copy(
        src_ref=src, dst_ref=dst.at[slot], send_sem=send_sems.at[k - 1], recv_sem=recv_sems.at[k - 1],
        device_id=_flip(me, k), device_id_type=MESH)


def _small_start(small, slots, after, *, name):
    def body(src_in, dst_in, after_ref, send_sems, recv_sems, src, dst):
        me = _me()
        for k in range(1, N_DEV):
            _small_copy(src, dst, k, 4 * me[0] + 2 * me[1] + me[2], me, send_sems, recv_sems).start()

    arrays = [small, slots]
    return pl.pallas_call(
        body, name=name, in_specs=[HBM, HBM, pl.BlockSpec(memory_space=pl.ANY)], out_specs=[SEM, SEM, HBM, HBM],
        out_shape=[pltpu.SemaphoreType.DMA((N_DEV - 1,)), pltpu.SemaphoreType.DMA((N_DEV - 1,))]
        + [pltpu.HBM(a.shape, a.dtype) for a in arrays],
        input_output_aliases={0: 2, 1: 3},
        compiler_params=pltpu.CompilerParams(has_side_effects=EFFECT),
    )(*[pltpu.with_memory_space_constraint(a, pltpu.HBM) for a in arrays], after)


def _small_wait(send_sems, recv_sems, small, slots, after, *, name):
    def body(src_in, dst_in, ss, rs, after_ref, src, dst):
        me = _me()
        for k in range(1, N_DEV):
            px, py, pc = _flip(me, k)
            cp = _small_copy(src, dst, k, 4 * px + 2 * py + pc, me, ss, rs)
            cp.wait_send()
            cp.wait_recv()

    return pl.pallas_call(
        body, name=name, in_specs=[HBM, HBM, SEM, SEM, pl.BlockSpec(memory_space=pl.ANY)], out_specs=[HBM, HBM],
        out_shape=[pltpu.HBM(small.shape, small.dtype), pltpu.HBM(slots.shape, slots.dtype)],
        input_output_aliases={0: 0, 1: 1},
        compiler_params=pltpu.CompilerParams(has_side_effects=EFFECT),
    )(small, slots, send_sems, recv_sems, after)


def _pair_gather(halves, *, name):
    n = len(halves)

    def body(*refs):
        bufs = refs[n:2 * n]
        send_sems, recv_sems = refs[2 * n:]
        me = _me()
        c = me[2]
        sib = _flip(me, 1)

        def copy(i, half):
            return pltpu.make_async_remote_copy(
                src_ref=bufs[i].at[:, half], dst_ref=bufs[i].at[:, half], send_sem=send_sems.at[i],
                recv_sem=recv_sems.at[i], device_id=sib, device_id_type=MESH)

        sends = [copy(i, c) for i in range(n)]
        for cp in sends:
            cp.start()
        for i in range(n):
            copy(i, 1 - c).wait_recv()
        for cp in sends:
            cp.wait_send()

    return pl.pallas_call(
        body, name=name,
        in_specs=[HBM] * n, out_specs=[HBM] * n,
        out_shape=[jax.ShapeDtypeStruct(h.shape, h.dtype) for h in halves],
        input_output_aliases={i: i for i in range(n)},
        scratch_shapes=[pltpu.SemaphoreType.DMA((n,)), pltpu.SemaphoreType.DMA((n,))],
    )(*halves)


def _local_step(x, tgt, P, fetch, ship, n_layers):
    T, D = x.shape
    P = dict(P)
    W = {}
    depth = P["ffn_conv_b"].shape[0]
    n_a = P["a_w_s"].shape[0]
    alpha = (2 * depth) ** 0.25
    H = P["rel_table"].shape[1] // N_GROUPS
    HD = H * HEAD_DIM
    dils = [d for _, d in DILATED_GROUPS]
    assert T % (dils[-1] * BLK) == 0

    def arrive(stage, after):
        w_new, p_new = fetch(stage, after)
        W.update(w_new)
        P.update(p_new)

    bias = _band_bias(P["rel_table"], H, name="band_bias")
    saved = []
    xf, xb = x, x.astype(BF16)
    kg = vg = None
    for i in range(depth):
        s = {"x": xf, "xb": xb}
        arrive(f"{i}a", xf)
        if i < n_a:
            s["zp"] = _mm(xb, W["a_w_in", i], out_dtype=BF16, name=f"a{i}_in")
            arrive(f"{i}b", s["zp"])
            s["y"] = _sgu_fwd(s["zp"], P["a_w_s"][i], P["a_b_s"][i], P["a_ln_g"][i], P["a_ln_b"][i], name=f"a{i}_sgu")
            s["h"] = _mm(s["y"], W["a_w_out", i], name=f"a{i}_out")
        else:
            j = i - n_a
            if j == 0:
                kv = _mm(xb, W["kv_w", 0], out_dtype=BF16, name="kv_proj")
                kg = [(kv, 0) if d == 1 else (_to_group(kv[:, :HD], d), 0) for d in dils]
                vg = [(kv, 1) if d == 1 else (_to_group(kv[:, HD:], d), 0) for d in dils]
            q = _mm(xb, W["b_w_q", j], out_dtype=BF16, name=f"b{j}_q")
            s["qg"] = [(q, g) if d == 1 else (_to_group(q[:, g * HD:(g + 1) * HD], d), 0) for g, d in enumerate(dils)]
            os, ls = [], []
            for g, d in enumerate(dils):
                o_g, l_g = _attn_fwd(s["qg"][g], kg[g], vg[g], bias, g, d, name=f"b{j}_attn{g}")
                os.append(o_g)
                ls.append(l_g)
            s["oc"], s["ob"], s["lse"] = _attn_combine(os, ls, dils, name=f"b{j}_comb")
            arrive(f"{i}b", s["ob"])
            s["h"] = _mm(s["ob"], W["b_w_o", j], name=f"b{j}_o")
        s["x1"], s["x1b"] = _add_ln(xf, s["h"], P["ln_g"][i, 0], P["ln_b"][i, 0], alpha, name=f"l{i}_ln1")
        arrive(f"{i}c", s["x1b"])
        s["hup"] = _mm(s["x1b"], W["ffn_w_up", i], out_dtype=BF16, name=f"l{i}_up")
        arrive(f"{i}d", s["hup"])
        s["act"], s["ca"], s["cg"] = _convglu_fwd(s["hup"], P["ffn_conv_w"][i], P["ffn_conv_b"][i], name=f"l{i}_glu")
        s["f"] = _mm(s["act"], W["ffn_w_down", i], name=f"l{i}_down")
        xf, xb = _add_ln(s["x1"], s["f"], P["ln_g"][i, 1], P["ln_b"][i, 1], alpha, name=f"l{i}_ln2")
        saved.append(s)

    G, loss_row = _loss_grad(xf, tgt, name="loss")

    gw = {}

    def dw(key, layer, a, b, name, **kw):
        gw[key] = _mm(a, b, ta=True, out_dtype=BF16, out_into=(gw.get(key), n_layers[key], layer), name=name, **kw)

    gp = {k: [None] * n_a for k in ("a_ln_g", "a_ln_b", "a_w_s", "a_b_s")}
    gp.update({k: [None] * depth for k in ("ffn_conv_w", "ffn_conv_b", "ln_g", "ln_b")})
    dk_parts, dv_parts, dbias_parts = [], [], []
    for i in reversed(range(depth)):
        s = saved[i]
        dr2, dr2b, dg2, db2 = _ln_bwd(G, s["x1"], s["f"], P["ln_g"][i, 1], alpha, name=f"l{i}_ln2_bwd")
        dw("ffn_w_down", i, s["act"], dr2b, f"l{i}_down_dw")
        dact = _mm(dr2b, W["ffn_w_down", i], tb=True, out_dtype=BF16, name=f"l{i}_down_dx")
        dhup, dcw, dcb = _convglu_bwd(s["hup"], s["ca"], s["cg"], dact, P["ffn_conv_w"][i], name=f"l{i}_glu_bwd")
        gp["ffn_conv_w"][i] = dcw.transpose(1, 0, 2).reshape(dcw.shape[1], -1)
        gp["ffn_conv_b"][i] = dcb.reshape(-1)
        dw("ffn_w_up", i, s["x1b"], dhup, f"l{i}_up_dw", b_halves=True)
        token = ship(gw, [("ffn_w_down", i), ("ffn_w_up", i)])
        G1 = _mm(dhup, W["ffn_w_up", i], tb=True, a_halves=True, add=dr2, add_scale=alpha, after=token,
                 tiles=(512, 1024, 2816), name=f"l{i}_up_dx")
        dr1, dr1b, dg1, db1 = _ln_bwd(G1, s["x"], s["h"], P["ln_g"][i, 0], alpha, name=f"l{i}_ln1_bwd")
        gp["ln_g"][i] = jnp.concatenate([dg1, dg2], axis=0)
        gp["ln_b"][i] = jnp.concatenate([db1, db2], axis=0)
        if i < n_a:
            dw("a_w_out", i, s["y"], dr1b, f"a{i}_out_dw")
            token = ship(gw, [("a_w_out", i)])
            dy = _mm(dr1b, W["a_w_out", i], tb=True, out_dtype=BF16, after=token, name=f"a{i}_out_dx")
            dzp, dws, dbs, dlg, dlb = _sgu_bwd(s["zp"], dy, P["a_w_s"][i], P["a_b_s"][i], P["a_ln_g"][i],
                                               P["a_ln_b"][i], name=f"a{i}_sgu_bwd")
            gp["a_w_s"][i], gp["a_b_s"][i] = dws, dbs[:, :dws.shape[0]].T
            gp["a_ln_g"][i], gp["a_ln_b"][i] = dlg[0], dlb[0]
            dw("a_w_in", i, s["xb"], dzp, f"a{i}_in_dw")
            token = ship(gw, [("a_w_in", i)])
            G = _mm(dzp, W["a_w_in", i], tb=True, add=dr1, add_scale=alpha, after=token, tiles=(512, 1024, 4096),
                    name=f"a{i}_in_dx")
        else:
            j = i - n_a
            dw("b_w_o", j, s["ob"], dr1b, f"b{j}_o_dw")
            do_tok = _mm(dr1b, W["b_w_o", j], tb=True, name=f"b{j}_o_dx")
            deltas, dobs = _attn_bwd_prep(do_tok, s["oc"], dils, name=f"b{j}_prep")
            dqs, dbs = [], []
            for g, d in enumerate(dils):
                dq, dk, dv, db = _attn_bwd(s["qg"][g], kg[g], vg[g], bias, g, d, dobs[g], s["lse"][g], deltas[g],
                                           name=f"b{j}_attn{g}_bwd")
                dqs.append(_from_group(dq, d))
                dk_parts.append((dk, d))
                dv_parts.append((dv, d))
                dbs.append(db)
            dbias_parts.append(jnp.stack(dbs))
            dq_tok = jnp.concatenate(dqs, axis=1)
            dw("b_w_q", j, s["xb"], dq_tok, f"b{j}_q_dw")
            token = ship(gw, [("b_w_o", j), ("b_w_q", j)])
            G = _mm(dq_tok, W["b_w_q", j], tb=True, add=dr1, add_scale=alpha, after=token, name=f"b{j}_q_dx")
            if j == 0:
                dkv = jnp.concatenate([_addn_grouped(dk_parts, BF16, name="dk_sum"),
                                       _addn_grouped(dv_parts, BF16, name="dv_sum")], axis=1)
                dw("kv_w", 0, s["xb"], dkv, "kv_dw")
                token = ship(gw, [("kv_w", 0)])
                G = _mm(dkv, W["kv_w", 0], tb=True, add=G, add_scale=1.0, after=token, name="kv_dx")
    rel = _rel_grad(dbias_parts, name="rel_grad")
    grel = rel[:, :, :H].transpose(1, 0, 2).reshape(REL_BUCKETS, N_GROUPS * H)
    gsmall = {k: jnp.stack(v) for k, v in gp.items()}
    gsmall["rel_table"] = grel
    return loss_row, G, gw, gsmall


BIG = (("a_w_in", COL), ("a_w_out", ROW), ("kv_w", ROW), ("b_w_q", COL), ("b_w_o", COL),
       ("ffn_w_up", COL), ("ffn_w_down", ROW))
SMALL_SHARDED = ("a_ln_g", "a_ln_b", "ffn_conv_w", "ln_g", "ln_b")
SMALL_REPLICATED = ("a_w_s", "a_b_s", "rel_table", "ffn_conv_b")
WEIGHTS = ("a_w_in", "a_ln_g", "a_ln_b", "a_w_s", "a_b_s", "a_w_out", "kv_w", "b_w_q", "b_w_o", "rel_table",
           "ffn_w_up", "ffn_conv_w", "ffn_conv_b", "ffn_w_down", "ln_g", "ln_b")


def kernel(x, a_w_in, a_ln_g, a_ln_b, a_w_s, a_b_s, a_w_out, kv_w, b_w_q, b_w_o, rel_table, ffn_w_up, ffn_conv_w, ffn_conv_b, ffn_w_down, ln_g, ln_b, loss_target, m_a_w_in, m_a_ln_g, m_a_ln_b, m_a_w_s, m_a_b_s, m_a_w_out, m_kv_w, m_b_w_q, m_b_w_o, m_rel_table, m_ffn_w_up, m_ffn_conv_w, m_ffn_conv_b, m_ffn_w_down, m_ln_g, m_ln_b, v_a_w_in, v_a_ln_g, v_a_ln_b, v_a_w_s, v_a_b_s, v_a_w_out, v_kv_w, v_b_w_q, v_b_w_o, v_rel_table, v_ffn_w_up, v_ffn_conv_w, v_ffn_conv_b, v_ffn_w_down, v_ln_g, v_ln_b):
    w = dict(a_w_in=a_w_in, a_ln_g=a_ln_g, a_ln_b=a_ln_b, a_w_s=a_w_s, a_b_s=a_b_s, a_w_out=a_w_out, kv_w=kv_w,
             b_w_q=b_w_q, b_w_o=b_w_o, rel_table=rel_table, ffn_w_up=ffn_w_up, ffn_conv_w=ffn_conv_w,
             ffn_conv_b=ffn_conv_b, ffn_w_down=ffn_w_down, ln_g=ln_g, ln_b=ln_b)
    m = dict(a_w_in=m_a_w_in, a_ln_g=m_a_ln_g, a_ln_b=m_a_ln_b, a_w_s=m_a_w_s, a_b_s=m_a_b_s, a_w_out=m_a_w_out,
             kv_w=m_kv_w, b_w_q=m_b_w_q, b_w_o=m_b_w_o, rel_table=m_rel_table, ffn_w_up=m_ffn_w_up,
             ffn_conv_w=m_ffn_conv_w, ffn_conv_b=m_ffn_conv_b, ffn_w_down=m_ffn_w_down, ln_g=m_ln_g, ln_b=m_ln_b)
    v = dict(a_w_in=v_a_w_in, a_ln_g=v_a_ln_g, a_ln_b=v_a_ln_b, a_w_s=v_a_w_s, a_b_s=v_a_b_s, a_w_out=v_a_w_out,
             kv_w=v_kv_w, b_w_q=v_b_w_q, b_w_o=v_b_w_o, rel_table=v_rel_table, ffn_w_up=v_ffn_w_up,
             ffn_conv_w=v_ffn_conv_w, ffn_conv_b=v_ffn_conv_b, ffn_w_down=v_ffn_w_down, ln_g=v_ln_g, ln_b=v_ln_b)
    chip = 2 * lax.axis_index("x") + lax.axis_index("y")

    big_names = [n for n, _ in BIG]
    big_axes = [a for _, a in BIG]
    where = jnp.stack([chip, lax.axis_index("c"), 2 * chip + lax.axis_index("c")]).astype(jnp.int32)
    shards = [w[n].reshape((1,) + w[n].shape) if w[n].ndim == 2 else w[n] for n in big_names]
    small_shapes = [w[n].shape for n in SMALL_SHARDED]
    small = _pack([w[n] for n in SMALL_SHARDED])
    shard_of = dict(zip(big_names, shards))
    axis_of = dict(BIG)
    n_layers = {n: s.shape[0] for n, s in shard_of.items()}
    n_a = w["a_w_s"].shape[0]
    stages = {}
    for i in range(w["ffn_w_up"].shape[0]):
        if i < n_a:
            stages[f"{i}a"], stages[f"{i}b"] = [("a_w_in", i)], [("a_w_out", i)]
        else:
            stages[f"{i}a"] = ([("kv_w", 0)] if i == n_a else []) + [("b_w_q", i - n_a)]
            stages[f"{i}b"] = [("b_w_o", i - n_a)]
        stages[f"{i}c"], stages[f"{i}d"] = [("ffn_w_up", i)], [("ffn_w_down", i)]
    stages["0a"].append(("small", 0))
    assert sum(len(st) for st in stages.values()) == sum(n_layers.values()) + 1
    flights, full_shape, all_started = {}, {}, None
    for batch, names in enumerate(([st for st in stages if st[0] == "0"], [st for st in stages if st[0] != "0"])):
        order = [key for st in names for key in stages[st]]
        placed, axes = [], []
        for n, l in order:
            if n == "small":
                placed.append(_place_shard(small[None], 0, ROW, where, F32, name="place_small"))
                axes.append(ROW)
            else:
                placed.append(_place_shard(shard_of[n], l, axis_of[n], where, BF16, name=f"place_{n}{l}",
                                           after=all_started))
                axes.append(axis_of[n])
                full_shape[n] = (n_layers[n],) + placed[-1].shape[1:]
        send_sems, recv_sems, flying, all_started = _gather_start(placed, axes, name=f"gather_start{batch}")
        for st in names:
            flights[st] = (send_sems, recv_sems, flying, axes, order)

    def fetch(stage, after):
        if stage not in stages:
            return {}, {}
        send_sems, recv_sems, flying, axes, order = flights[stage]
        if stage == "0a":
            after = all_started
        idxs = [order.index(key) for key in stages[stage]]
        landed = _gather_wait(send_sems, recv_sems, [flying[i] for i in idxs], [axes[i] for i in idxs], idxs, after,
                              name=f"gather_wait_{stage}")
        W_new, P_new = {}, {}
        for key, arr in zip(stages[stage], landed):
            if key[0] == "small":
                small_all = arr.reshape((N_CHIPS,) + small.shape)
                per_chip = [_unpack(small_all[j], small_shapes) for j in range(N_CHIPS)]
                for i, n in enumerate(SMALL_SHARDED):
                    P_new[n] = jnp.concatenate([per_chip[j][i] for j in range(N_CHIPS)], axis=-1)
            else:
                W_new[key] = arr.reshape(arr.shape[1:])
        return W_new, P_new

    slots ={n: lax.empty((N_DEV, n_layers[n]) + _part_geometry(full_shape[n], axis_of[n]), BF16) for n in big_names}
    sems, plan, tokens = [], [], []

    def ship(gw, items):
        names = [n for n, _ in items]
        send, recv, grads, landing, token = _scatter_start(
            [gw[n] for n in names], [slots[n] for n in names], [l for _, l in items], [axis_of[n] for n in names],
            name="scatter_start_" + "_".join(f"{n}{l}" for n, l in items))
        for n, g, s in zip(names, grads, landing):
            gw[n], slots[n] = g, s
        sems.append((send, recv))
        plan.append([(big_names.index(n), l) for n, l in items])
        tokens.append(token)
        return token

    loss_row, grad_x, gw, gsmall = _local_step(x[0], loss_target[0], {n: w[n] for n in SMALL_REPLICATED}, fetch,
                                               ship, n_layers)

    small_names = list(SMALL_SHARDED) + list(SMALL_REPLICATED)
    small_pack = _pack([gsmall[n] for n in small_names] + [loss_row[0, :1]])
    small_sems = _small_start(small_pack, lax.empty((N_DEV,) + small_pack.shape, F32), tokens[-1],
                              name="small_start")
    own, landed = _scatter_wait(sems, [gw[n] for n in big_names], [slots[n] for n in big_names], big_axes, plan,
                                grad_x, name="scatter_wait")
    sums = [_sum_parts(b, g, ax, where, name=f"sum_{n}") for (n, ax), b, g in zip(BIG, landed, own)]
    pairs = _pair_gather(sums, name="pair_gather")
    grad = {n: p.reshape(w[n].shape) for n, p in zip(big_names, pairs)}

    delta, new_m, new_v = {}, {}, {}
    for n, _ in BIG:
        delta[n], new_m[n], new_v[n] = _adamw(w[n], grad[n], m[n], v[n], name=f"adamw_{n}")
    small_own, small_slots = _small_wait(*small_sems, new_v[big_names[-1]], name="small_wait")
    small_sum = _sum_small(small_slots, small_own, where, name="sum_small")
    full_small = _unpack(small_sum, [gsmall[n].shape for n in small_names] + [(1,)])
    loss = full_small[-1][0]
    for n, gfull in zip(small_names, full_small):
        if n in SMALL_SHARDED:
            width = w[n].shape[-1]
            grad[n] = lax.dynamic_slice_in_dim(gfull, chip * width, width, axis=-1)
        else:
            grad[n] = gfull
    shapes = [w[n].shape for n in small_names]
    packed = _adamw(_pack([w[n] for n in small_names]), _pack([grad[n] for n in small_names]),
                    _pack([m[n] for n in small_names]), _pack([v[n] for n in small_names]), name="adamw_small")
    for out, res in zip((delta, new_m, new_v), packed):
        for n, a in zip(small_names, _unpack(res, shapes)):
            out[n] = a

    return (loss, grad_x[None], *[grad[n] for n in WEIGHTS], *[delta[n] for n in WEIGHTS],
            *[new_m[n] for n in WEIGHTS], *[new_v[n] for n in WEIGHTS])
```

```python
import functools
import math

import numpy as np
import jax
import jax.numpy as jnp
from jax import lax
from jax.experimental import pallas as pl
from jax.experimental.pallas import tpu as pltpu

F32 = jnp.float32
BF16 = jnp.bfloat16
MESH = pl.DeviceIdType.MESH

CHUNK = 128
HEAD_DIM = 64
DILATED_GROUPS = ((128, 1), (512, 4), (2048, 16))
N_GROUPS = len(DILATED_GROUPS)
BLK = 128
REL_BUCKETS = 32
REL_MAX_DIST = 2048
LN_EPS = 1e-5
NEG = -1e30
ADAM_LR = 0.001
ADAM_B1 = 0.9
ADAM_B2 = 0.999
ADAM_EPS = 1e-08
ADAM_WD = 0.01
ADAM_STEP = 10

N_CHIPS = 4
N_DEV = 8
LANES = 128
SUBLANES = 8
VMEM_LIMIT = 48 * 1024 * 1024

_SQRT_HALF = 0.7071067811865476
_INV_SQRT_2PI = 0.3989422804014327


def _cp(sem=None, vmem=VMEM_LIMIT):
    return pltpu.CompilerParams(dimension_semantics=sem, vmem_limit_bytes=vmem)


def _tile(dim, target, align=LANES):
    if dim <= target:
        return dim
    t = (target // align) * align
    while t >= align:
        if dim % t == 0:
            return t
        t -= align
    return dim


def _gelu(x):
    return 0.5 * x * (1.0 + lax.erf(x * _SQRT_HALF))


def _gelu_grad(x):
    return 0.5 * (1.0 + lax.erf(x * _SQRT_HALF)) + x * (_INV_SQRT_2PI * jnp.exp(-0.5 * x * x))


def _dot(a, b, ca, cb):
    return lax.dot_general(a, b, (((ca,), (cb,)), ((), ())), preferred_element_type=F32)


def _mn_tile(dim):
    return max(_tile(dim, 1024), _tile(dim, 1408))


def _mm(a, b, *, name, ta=False, tb=False, out_dtype=F32, tiles=None,
        a_lead=(), b_lead=(), a_halves=False, b_halves=False, add=None, add_scale=1.0, out_into=None, after=None):
    a2, b2 = a.shape[-2:], b.shape[-2:]
    M, K = (a2[1], a2[0]) if ta else a2
    Kb, N = (b2[1], b2[0]) if tb else b2
    if a_halves:
        K = 2 * K
    if b_halves:
        N = 2 * N
    assert K == Kb, (name, a.shape, b.shape)
    if tiles is None:
        k_target = 1408 if a_halves else (K if K <= 2816 else 2048)
        tm, tn = _mn_tile(M), _mn_tile(N // 2 if b_halves else N)
        tk = _tile(K // 2 if a_halves else K, k_target)
    else:
        tm = _tile(M, tiles[0])
        tn = _tile(N // 2 if b_halves else N, tiles[1])
        tk = _tile(K // 2 if a_halves else K, tiles[2])
    nm, nn, nk = M // tm, N // tn, K // tk
    nkh, nnh = nk // 2, nn // 2

    def lead(idx, rest):
        return tuple(idx) + tuple(rest)

    sq_a = (None,) * (a.ndim - 2)
    sq_b = (None,) * (b.ndim - 2)
    if a_halves:
        assert not ta and a.ndim == 3
        a_spec = pl.BlockSpec((None, tm, tk), lambda i, j, k: (k // nkh, i, k % nkh))
    elif ta:
        a_spec = pl.BlockSpec(sq_a + (tk, tm), lambda i, j, k: lead(a_lead, (k, i)))
    else:
        a_spec = pl.BlockSpec(sq_a + (tm, tk), lambda i, j, k: lead(a_lead, (i, k)))
    if b_halves:
        assert not tb and b.ndim == 3
        b_spec = pl.BlockSpec((None, tk, tn), lambda i, j, k: (j // nnh, k, j % nnh))
    elif tb:
        b_spec = pl.BlockSpec(sq_b + (tn, tk), lambda i, j, k: lead(b_lead, (j, k)))
    else:
        b_spec = pl.BlockSpec(sq_b + (tk, tn), lambda i, j, k: lead(b_lead, (k, j)))
    mn_spec = pl.BlockSpec((tm, tn), lambda i, j, k: (i, j))
    in_specs = [a_spec, b_spec]
    args = [a, b]
    if add is not None:
        in_specs.append(mn_spec)
        args.append(add)
    aliases = {}
    if out_into is None:
        o_spec, out_shape = mn_spec, jax.ShapeDtypeStruct((M, N), out_dtype)
    else:
        buf, n_layers, layer = out_into
        o_spec = pl.BlockSpec((None, tm, tn), lambda i, j, k: (layer, i, j))
        out_shape = jax.ShapeDtypeStruct((n_layers, M, N), out_dtype)
        if buf is not None:
            aliases = {len(args): 0}
            in_specs.append(pl.BlockSpec(memory_space=pl.ANY))
            args.append(buf)
    if after is not None:
        in_specs.append(pl.BlockSpec(memory_space=pl.ANY))
        args.append(after)
    n_in = len(args)
    ca, cb = (0 if ta else 1), (1 if tb else 0)

    def body(*refs):
        a_ref, b_ref = refs[:2]
        add_ref = refs[2] if add is not None else None
        o_ref = refs[n_in]

        def finish(acc):
            if add_ref is not None:
                acc = acc + add_scale * add_ref[...]
            o_ref[...] = acc.astype(o_ref.dtype)

        if nk == 1:
            finish(_dot(a_ref[...], b_ref[...], ca, cb))
        else:
            acc_ref = refs[-1]
            k = pl.program_id(2)

            @pl.when(k == 0)
            def _():
                acc_ref[...] = jnp.zeros_like(acc_ref)

            acc_ref[...] += _dot(a_ref[...], b_ref[...], ca, cb)

            @pl.when(k == nk - 1)
            def _():
                finish(acc_ref[...])

    return pl.pallas_call(
        body, name=name, grid=(nm, nn, nk),
        in_specs=in_specs, out_specs=o_spec, out_shape=out_shape,
        input_output_aliases=aliases,
        scratch_shapes=[] if nk == 1 else [pltpu.VMEM((tm, tn), F32)],
        compiler_params=_cp(("parallel", "parallel", "arbitrary")),
    )(*args)


def _ln_stats(r):
    mu = jnp.mean(r, axis=-1, keepdims=True)
    xc = r - mu
    var = jnp.mean(xc * xc, axis=-1, keepdims=True)
    rstd = lax.rsqrt(var + LN_EPS)
    return xc * rstd, rstd


def _add_ln(x, h, g, b, alpha, *, name):
    T, D = x.shape
    tr = _tile(T, 512, SUBLANES)
    row = pl.BlockSpec((tr, D), lambda i: (i, 0))
    vec = pl.BlockSpec((1, D), lambda i: (0, 0))

    def body(x_ref, h_ref, g_ref, b_ref, o_ref, ob_ref):
        xhat, _ = _ln_stats(alpha * x_ref[...] + h_ref[...])
        y = xhat * g_ref[...] + b_ref[...]
        o_ref[...] = y
        ob_ref[...] = y.astype(BF16)

    return pl.pallas_call(
        body, name=name, grid=(T // tr,),
        in_specs=[row, row, vec, vec], out_specs=[row, row],
        out_shape=[jax.ShapeDtypeStruct((T, D), F32), jax.ShapeDtypeStruct((T, D), BF16)],
        compiler_params=_cp(("parallel",)),
    )(x, h, g.reshape(1, D), b.reshape(1, D))


def _ln_bwd(dy, x, h, g, alpha, *, name):
    T, D = x.shape
    tr = _tile(T, 512, SUBLANES)
    row = pl.BlockSpec((tr, D), lambda i: (i, 0))
    vec = pl.BlockSpec((1, D), lambda i: (0, 0))

    def body(dy_ref, x_ref, h_ref, g_ref, dr_ref, drb_ref, dg_ref, db_ref):
        @pl.when(pl.program_id(0) == 0)
        def _():
            dg_ref[...] = jnp.zeros_like(dg_ref)
            db_ref[...] = jnp.zeros_like(db_ref)

        xhat, rstd = _ln_stats(alpha * x_ref[...] + h_ref[...])
        dyv = dy_ref[...]
        dxhat = dyv * g_ref[...]
        c1 = jnp.mean(dxhat, axis=-1, keepdims=True)
        c2 = jnp.mean(dxhat * xhat, axis=-1, keepdims=True)
        dr = rstd * (dxhat - c1 - xhat * c2)
        dr_ref[...] = dr
        drb_ref[...] = dr.astype(BF16)
        dg_ref[...] += jnp.sum(dyv * xhat, axis=0, keepdims=True)
        db_ref[...] += jnp.sum(dyv, axis=0, keepdims=True)

    return pl.pallas_call(
        body, name=name, grid=(T // tr,),
        in_specs=[row, row, row, vec], out_specs=[row, row, vec, vec],
        out_shape=[jax.ShapeDtypeStruct((T, D), F32), jax.ShapeDtypeStruct((T, D), BF16),
                   jax.ShapeDtypeStruct((1, D), F32), jax.ShapeDtypeStruct((1, D), F32)],
        compiler_params=_cp(("arbitrary",)),
    )(dy, x, h, g.reshape(1, D))


def _sgu_pieces(zp, lg, lb, E):
    z = _gelu(zp)
    u, v = z[:, :E], z[:, E:]
    vhat, rstd = _ln_stats(v)
    vn = vhat * lg + lb
    return u, vhat, rstd, vn


def _tril_mask():
    t = lax.broadcasted_iota(jnp.int32, (CHUNK, CHUNK), 0)
    s = lax.broadcasted_iota(jnp.int32, (CHUNK, CHUNK), 1)
    return s <= t


def _sgu_fwd(zp, ws, bs, lg, lb, *, name):
    T, E2 = zp.shape
    E = E2 // 2
    G = ws.shape[0]
    cw = E // G

    def body(zp_ref, ws_ref, bs_ref, lg_ref, lb_ref, y_ref):
        u, _, _, vn = _sgu_pieces(zp_ref[...].astype(F32), lg_ref[...], lb_ref[...], E)
        vnb = vn.astype(BF16)
        tril = _tril_mask()
        for g in range(G):
            cols = slice(g * cw, (g + 1) * cw)
            w = jnp.where(tril, ws_ref[g], 0.0).astype(BF16)
            sv = _dot(w, vnb[:, cols], 1, 0) + bs_ref[g]
            y_ref[:, cols] = (u[:, cols] * sv).astype(BF16)

    return pl.pallas_call(
        body, name=name, grid=(T // CHUNK,),
        in_specs=[pl.BlockSpec((CHUNK, E2), lambda n: (n, 0)),
                  pl.BlockSpec((G, CHUNK, CHUNK), lambda n: (0, 0, 0)),
                  pl.BlockSpec((G, CHUNK, 1), lambda n: (0, 0, 0)),
                  pl.BlockSpec((1, E), lambda n: (0, 0)),
                  pl.BlockSpec((1, E), lambda n: (0, 0))],
        out_specs=pl.BlockSpec((CHUNK, E), lambda n: (n, 0)),
        out_shape=jax.ShapeDtypeStruct((T, E), BF16),
        compiler_params=_cp(("parallel",)),
    )(zp, ws, bs.reshape(G, CHUNK, 1), lg.reshape(1, E), lb.reshape(1, E))


def _sgu_bwd(zp, dy, ws, bs, lg, lb, *, name):
    T, E2 = zp.shape
    E = E2 // 2
    G = ws.shape[0]
    cw = E // G
    assert G <= LANES

    def body(zp_ref, dy_ref, ws_ref, bs_ref, lg_ref, lb_ref,
             dzp_ref, dws_ref, dbs_ref, dlg_ref, dlb_ref, dvn_ref):
        @pl.when(pl.program_id(0) == 0)
        def _():
            dws_ref[...] = jnp.zeros_like(dws_ref)
            dbs_ref[...] = jnp.zeros_like(dbs_ref)
            dlg_ref[...] = jnp.zeros_like(dlg_ref)
            dlb_ref[...] = jnp.zeros_like(dlb_ref)

        zpv = zp_ref[...].astype(F32)
        lgv = lg_ref[...]
        u, vhat, rstd, vn = _sgu_pieces(zpv, lgv, lb_ref[...], E)
        gp = _gelu_grad(zpv)
        vnb = vn.astype(BF16)
        tril = _tril_mask()
        lane = lax.broadcasted_iota(jnp.int32, (CHUNK, LANES), 1)
        dbs = dbs_ref[...]
        for g in range(G):
            cols = slice(g * cw, (g + 1) * cw)
            w = jnp.where(tril, ws_ref[g], 0.0).astype(BF16)
            sv = _dot(w, vnb[:, cols], 1, 0) + bs_ref[g]
            dyg = dy_ref[:, cols].astype(F32)
            dzp_ref[:, cols] = (dyg * sv * gp[:, cols]).astype(BF16)
            dsv = dyg * u[:, cols]
            dsvb = dsv.astype(BF16)
            dws_ref[g] += jnp.where(tril, _dot(dsvb, vnb[:, cols], 1, 1), 0.0)
            dvn_ref[:, cols] = _dot(w, dsvb, 0, 0)
            dbs = dbs + jnp.where(lane == g, jnp.sum(dsv, axis=1, keepdims=True), 0.0)
        dbs_ref[...] = dbs
        dvn = dvn_ref[...]
        dlg_ref[...] += jnp.sum(dvn * vhat, axis=0, keepdims=True)
        dlb_ref[...] += jnp.sum(dvn, axis=0, keepdims=True)
        dvhat = dvn * lgv
        c1 = jnp.mean(dvhat, axis=-1, keepdims=True)
        c2 = jnp.mean(dvhat * vhat, axis=-1, keepdims=True)
        dv = rstd * (dvhat - c1 - vhat * c2)
        dzp_ref[:, E:] = (dv * gp[:, E:]).astype(BF16)

    vecE = pl.BlockSpec((1, E), lambda n: (0, 0))
    return pl.pallas_call(
        body, name=name, grid=(T // CHUNK,),
        in_specs=[pl.BlockSpec((CHUNK, E2), lambda n: (n, 0)),
                  pl.BlockSpec((CHUNK, E), lambda n: (n, 0)),
                  pl.BlockSpec((G, CHUNK, CHUNK), lambda n: (0, 0, 0)),
                  pl.BlockSpec((G, CHUNK, 1), lambda n: (0, 0, 0)),
                  vecE, vecE],
        out_specs=[pl.BlockSpec((CHUNK, E2), lambda n: (n, 0)),
                   pl.BlockSpec((G, CHUNK, CHUNK), lambda n: (0, 0, 0)),
                   pl.BlockSpec((CHUNK, LANES), lambda n: (0, 0)),
                   vecE, vecE],
        out_shape=[jax.ShapeDtypeStruct((T, E2), BF16),
                   jax.ShapeDtypeStruct((G, CHUNK, CHUNK), F32),
                   jax.ShapeDtypeStruct((CHUNK, LANES), F32),
                   jax.ShapeDtypeStruct((1, E), F32), jax.ShapeDtypeStruct((1, E), F32)],
        scratch_shapes=[pltpu.VMEM((CHUNK, E), F32)],
        compiler_params=_cp(("arbitrary",)),
    )(zp, dy, ws, bs.reshape(G, CHUNK, 1), lg.reshape(1, E), lb.reshape(1, E))


def _delay(h, k):
    t = lax.broadcasted_iota(jnp.int32, h.shape, 0)
    return jnp.where(t >= k, pltpu.roll(h, k, 0), 0.0)


def _advance(d, k):
    T = d.shape[0]
    t = lax.broadcasted_iota(jnp.int32, d.shape, 0)
    return jnp.where(t < T - k, pltpu.roll(d, T - k, 0), 0.0)


def _conv3(h, w_ref, b_ref):
    return w_ref[2:3, :] * h + w_ref[1:2, :] * _delay(h, 1) + w_ref[0:1, :] * _delay(h, 2) + b_ref[...]


def _convglu_fwd(hup, cw, cb, *, name):
    T, F2 = hup.shape
    F = F2 // 2
    tc = LANES
    nt = F // tc

    def body(ha_ref, hg_ref, wa_ref, wg_ref, ba_ref, bg_ref, o_ref, a_ref, g_ref):
        a = _conv3(ha_ref[...].astype(F32), wa_ref, ba_ref)
        g = _conv3(hg_ref[...].astype(F32), wg_ref, bg_ref)
        o_ref[...] = (_gelu(a) * g).astype(BF16)
        a_ref[...] = a.astype(BF16)
        g_ref[...] = g.astype(BF16)

    col = lambda off: pl.BlockSpec((T, tc), lambda j: (0, j + off))
    w3 = lambda off: pl.BlockSpec((3, tc), lambda j: (0, j + off))
    b1 = lambda off: pl.BlockSpec((1, tc), lambda j: (0, j + off))
    shp = jax.ShapeDtypeStruct((T, F), BF16)
    return pl.pallas_call(
        body, name=name, grid=(nt,),
        in_specs=[col(0), col(nt), w3(0), w3(nt), b1(0), b1(nt)],
        out_specs=[col(0)] * 3, out_shape=[shp] * 3,
        compiler_params=_cp(("parallel",)),
    )(hup, hup, cw, cw, cb.reshape(1, F2), cb.reshape(1, F2))


def _convglu_bwd(hup, a, g, dact, cw, *, name):
    T, F2 = hup.shape
    F = F2 // 2
    tc = LANES
    nt = F // tc

    def half(h, w_ref, d, dh_ref, dw_ref, db_ref, i):
        d1, d2 = _advance(d, 1), _advance(d, 2)
        dh_ref[i] = (w_ref[2:3, :] * d + w_ref[1:2, :] * d1 + w_ref[0:1, :] * d2).astype(BF16)
        c = _dot(h, jnp.concatenate([d2, d1, d], axis=1).astype(BF16), 0, 0)
        eye = lax.broadcasted_iota(jnp.int32, (tc, tc), 0) == lax.broadcasted_iota(jnp.int32, (tc, tc), 1)
        for k in range(3):
            dw_ref[i, k:k + 1, :] = jnp.sum(jnp.where(eye, c[:, k * tc:(k + 1) * tc], 0.0), axis=0, keepdims=True)
        db_ref[i] = jnp.sum(d, axis=0, keepdims=True)

    def body(ha_ref, hg_ref, a_ref, g_ref, d_ref, wa_ref, wg_ref, dh_ref, dw_ref, db_ref):
        a = a_ref[...].astype(F32)
        d = d_ref[...].astype(F32)
        half(ha_ref[...].astype(BF16), wa_ref, d * g_ref[...].astype(F32) * _gelu_grad(a), dh_ref, dw_ref, db_ref, 0)
        half(hg_ref[...].astype(BF16), wg_ref, d * _gelu(a), dh_ref, dw_ref, db_ref, 1)

    col = lambda off: pl.BlockSpec((T, tc), lambda j: (0, j + off))
    w3 = lambda off: pl.BlockSpec((3, tc), lambda j: (0, j + off))
    return pl.pallas_call(
        body, name=name, grid=(nt,),
        in_specs=[col(0), col(nt), col(0), col(0), col(0), w3(0), w3(nt)],
        out_specs=[pl.BlockSpec((2, T, tc), lambda j: (0, 0, j)),
                   pl.BlockSpec((2, 3, tc), lambda j: (0, 0, j)),
                   pl.BlockSpec((2, 1, tc), lambda j: (0, 0, j))],
        out_shape=[jax.ShapeDtypeStruct((2, T, F), BF16),
                   jax.ShapeDtypeStruct((2, 3, F), F32),
                   jax.ShapeDtypeStruct((2, 1, F), F32)],
        compiler_params=_cp(("parallel",)),
    )(hup, hup, a, g, dact, cw, cw)


def _bucket_table():
    iq = np.arange(BLK)[:, None]
    ik = np.arange(2 * BLK)[None, :]
    delta = iq + BLK - ik
    out = []
    for win, dil in DILATED_GROUPS:
        valid = (delta >= 0) & (delta <= win // dil)
        n = (np.clip(delta, 0, None) * dil).astype(np.int32)
        max_exact = REL_BUCKETS // 2
        nf = np.maximum(n, 1).astype(np.float32)
        large = max_exact + (np.log(nf / np.float32(max_exact)) / np.float32(math.log(REL_MAX_DIST / max_exact))
                             * np.float32(REL_BUCKETS - max_exact)).astype(np.int32)
        large = np.minimum(large, REL_BUCKETS - 1)
        out.append(np.where(valid, np.where(n < max_exact, n, large), -1))
    return np.stack(out).astype(np.int32)


def _band_bias(rel_table, H, *, name):
    bkt = jnp.asarray(_bucket_table())

    def body(tab_ref, bk_ref, o_ref):
        g = pl.program_id(0)
        bk = bk_ref[...]
        prev = lax.broadcasted_iota(jnp.int32, bk.shape, 1) < BLK
        for h in range(H):
            def bucket(c, acc):
                return jnp.where(bk == c, tab_ref[c, g * H + h], acc)

            b = lax.fori_loop(0, REL_BUCKETS, bucket, jnp.zeros(bk.shape, F32))
            b = jnp.where(bk >= 0, b, NEG)
            o_ref[h, 0] = b
            o_ref[h, 1] = jnp.where(prev, NEG, b)

    return pl.pallas_call(
        body, name=name, grid=(N_GROUPS,),
        in_specs=[pl.BlockSpec(memory_space=pltpu.SMEM),
                  pl.BlockSpec((None, BLK, 2 * BLK), lambda g: (g, 0, 0))],
        out_specs=pl.BlockSpec((None, H, 2, BLK, 2 * BLK), lambda g: (g, 0, 0, 0, 0)),
        out_shape=jax.ShapeDtypeStruct((N_GROUPS, H, 2, BLK, 2 * BLK), F32),
        compiler_params=_cp(("parallel",)),
    )(rel_table, bkt)


PAIR = 2 * HEAD_DIM


def _bdot(a, b, ca, cb):
    return lax.dot_general(a, b, (((ca,), (cb,)), ((0,), (0,))), preferred_element_type=F32)


def _blocks_per_step(nb, dil):
    return 2 if (nb // dil) % 2 == 0 else 1


def _blk(i):
    return slice(i * BLK, (i + 1) * BLK)


def _heads(ref, i, H, own_columns_only):
    low = lax.broadcasted_iota(jnp.int32, (BLK, PAIR), 1) < HEAD_DIM
    out = []
    for h in range(H):
        a = ref[_blk(i), (h // 2) * PAIR:(h // 2 + 1) * PAIR]
        if own_columns_only:
            a = jnp.where(low if h % 2 == 0 else jnp.logical_not(low), a, jnp.zeros_like(a))
        out.append(a)
    return jnp.stack(out)


def _pair_sums(o):
    return [o[2 * p] + o[2 * p + 1] for p in range(o.shape[0] // 2)]


def _store_pairs(ref, i, o, dtype, scale=None):
    for p, v in enumerate(_pair_sums(o)):
        ref[_blk(i), p * PAIR:(p + 1) * PAIR] = (v if scale is None else scale * v).astype(dtype)


def _stat_tile(s):
    lane = lax.broadcasted_iota(jnp.int32, (BLK, LANES), 1)
    t = jnp.zeros((BLK, LANES), F32)
    for h in range(s.shape[0]):
        t = jnp.where(lane == h, s[h], t)
    return t


def _stat_cols(ref, i, H):
    return jnp.stack([ref[_blk(i), h:h + 1] for h in range(H)])


def _scores(qm, kp, kc, bias):
    scale = HEAD_DIM ** -0.5
    sp = _bdot(qm, kp, 2, 2) * scale + bias[:, :, :BLK]
    sc = _bdot(qm, kc, 2, 2) * scale + bias[:, :, BLK:]
    return sp, sc


def _attn_specs(nb, dil, g, H, HD, sb):
    nbg = nb // dil
    QB = _blocks_per_step(nb, dil)
    cur = lambda col: pl.BlockSpec((QB * BLK, HD), lambda b: (sb(b), col))
    prev = lambda col: pl.BlockSpec((BLK, HD), lambda b: (jnp.maximum(QB * sb(b) - 1, 0), col))
    stat = pl.BlockSpec((QB * BLK, LANES), lambda b: (sb(b), 0))
    bias0 = pl.BlockSpec((None, H, None, BLK, 2 * BLK),
                         lambda b: (g, 0, jnp.where((QB * sb(b)) % nbg == 0, 1, 0), 0, 0))
    bias1 = pl.BlockSpec((None, H, None, BLK, 2 * BLK), lambda b: (g, 0, 0, 0, 0))
    return cur, prev, stat, bias0, bias1


def _attn_fwd(q, k, v, bias, g, dil, *, name):
    (qa, qc), (ka, kc_), (va, vc_) = q, k, v
    T = qa.shape[0]
    H = bias.shape[1]
    HD = H * HEAD_DIM
    assert H % 2 == 0 and H <= LANES and T % (dil * BLK) == 0
    nb = T // BLK
    QB = _blocks_per_step(nb, dil)
    cur, prev, stat, bias0, bias1 = _attn_specs(nb, dil, g, H, HD, lambda b: b)

    def body(q_ref, kp_ref, kc_ref, vp_ref, vc_ref, b0_ref, b1_ref, o_ref, l_ref):
        for i in range(QB):
            kpi, vpi = (kp_ref, vp_ref) if i == 0 else (kc_ref, vc_ref)
            sp, sc = _scores(_heads(q_ref, i, H, True), _heads(kpi, max(i - 1, 0), H, False),
                             _heads(kc_ref, i, H, False), (b0_ref if i == 0 else b1_ref)[...])
            m = jnp.maximum(jnp.max(sp, axis=-1, keepdims=True), jnp.max(sc, axis=-1, keepdims=True))
            pp, pc = jnp.exp(sp - m), jnp.exp(sc - m)
            den = jnp.sum(pp, axis=-1, keepdims=True) + jnp.sum(pc, axis=-1, keepdims=True)
            o = (_bdot((pp / den).astype(BF16), _heads(vpi, max(i - 1, 0), H, True), 2, 1)
                 + _bdot((pc / den).astype(BF16), _heads(vc_ref, i, H, True), 2, 1))
            _store_pairs(o_ref, i, o, F32)
            l_ref[_blk(i), :] = _stat_tile(m + jnp.log(den))

    return pl.pallas_call(
        body, name=name, grid=(nb // QB,),
        in_specs=[cur(qc), prev(kc_), cur(kc_), prev(vc_), cur(vc_), bias0, bias1],
        out_specs=[cur(0), stat],
        out_shape=[jax.ShapeDtypeStruct((T, HD), F32), jax.ShapeDtypeStruct((T, LANES), F32)],
        compiler_params=_cp(("parallel",)),
    )(qa, ka, ka, va, va, bias, bias)


def _grouped_spec(tr, d, C):
    return pl.BlockSpec((d, tr // d, C), lambda i: (0, i, 0))


def _as_grouped(a, d):
    T, C = a.shape
    return a.reshape(d, T // d, C)


def _to_token_slabs(ref, slabs, d):
    n_r = ref.shape[1]
    for j in range(ref.shape[2] // LANES):
        for r in range(d):
            slabs.at[j][pl.ds(r, n_r, stride=d), :] = ref[r, :, j * LANES:(j + 1) * LANES].astype(F32)


def _from_token_slabs(slabs, ref, d):
    n_r = ref.shape[1]
    for j in range(ref.shape[2] // LANES):
        for r in range(d):
            ref[r, :, j * LANES:(j + 1) * LANES] = slabs.at[j][pl.ds(r, n_r, stride=d), :].astype(ref.dtype)


def _regroup_to_tokens(parts, out_dtype, *, name):
    T, C = parts[0][0].shape
    n = len(parts)
    tr = _tile(T, 512, SUBLANES)
    n_sl = C // LANES

    def body(*refs):
        o_ref, scr = refs[n], refs[n + 1]
        for g, (_, d) in enumerate(parts):
            _to_token_slabs(refs[g], scr, d)
            for j in range(n_sl):
                o_ref[:, g * C + j * LANES:g * C + (j + 1) * LANES] = scr[j].astype(out_dtype)

    return pl.pallas_call(
        body, name=name, grid=(T // tr,),
        in_specs=[_grouped_spec(tr, d, C) for _, d in parts],
        out_specs=pl.BlockSpec((tr, n * C), lambda i: (i, 0)),
        out_shape=jax.ShapeDtypeStruct((T, n * C), out_dtype),
        scratch_shapes=[pltpu.VMEM((n_sl, tr, LANES), F32)],
        compiler_params=_cp(("parallel",)),
    )(*[_as_grouped(a, d) for a, d in parts])


def _attn_combine(os, ls, dils, *, name):
    n = len(os)
    T, HD = os[0].shape
    H = HD // HEAD_DIM
    tr = _tile(T, 512, SUBLANES)
    n_sl = HD // LANES
    assert PAIR == LANES

    def body(*refs):
        o_refs, l_refs = refs[:n], refs[n:2 * n]
        oc_ref, ocb_ref = refs[2 * n:2 * n + 2]
        lt_refs = refs[2 * n + 2:3 * n + 2]
        o_scr, l_scr = refs[3 * n + 2], refs[3 * n + 3]
        for g in range(n):
            _to_token_slabs(o_refs[g], o_scr.at[g], dils[g])
            _to_token_slabs(l_refs[g], l_scr.at[g], dils[g])
        ls_ = [l_scr[g, 0] for g in range(n)]
        m = ls_[0]
        for l in ls_[1:]:
            m = jnp.maximum(m, l)
        ws = [jnp.exp(l - m) for l in ls_]
        den = ws[0]
        for w in ws[1:]:
            den = den + w
        l_scr[n, 0] = m + jnp.log(den)
        for g in range(n):
            _from_token_slabs(l_scr.at[n], lt_refs[g], dils[g])
        ws = [w / den for w in ws]
        low = lax.broadcasted_iota(jnp.int32, (tr, PAIR), 1) < HEAD_DIM
        for p in range(H // 2):
            acc = jnp.zeros((tr, PAIR), F32)
            for g in range(n):
                w = jnp.where(low, ws[g][:, 2 * p:2 * p + 1], ws[g][:, 2 * p + 1:2 * p + 2])
                acc = acc + w * o_scr[g, p]
            oc_ref[:, p * PAIR:(p + 1) * PAIR] = acc
            ocb_ref[:, p * PAIR:(p + 1) * PAIR] = acc.astype(BF16)

    wide = pl.BlockSpec((tr, HD), lambda i: (i, 0))
    lts = pl.pallas_call(
        body, name=name, grid=(T // tr,),
        in_specs=[_grouped_spec(tr, d, HD) for d in dils] + [_grouped_spec(tr, d, LANES) for d in dils],
        out_specs=[wide, wide] + [_grouped_spec(tr, d, LANES) for d in dils],
        out_shape=[jax.ShapeDtypeStruct((T, HD), F32), jax.ShapeDtypeStruct((T, HD), BF16)]
        + [jax.ShapeDtypeStruct((d, T // d, LANES), F32) for d in dils],
        scratch_shapes=[pltpu.VMEM((n, n_sl, tr, LANES), F32), pltpu.VMEM((n + 1, 1, tr, LANES), F32)],
        compiler_params=_cp(("parallel",)),
    )(*[_as_grouped(o, d) for o, d in zip(os, dils)], *[_as_grouped(l, d) for l, d in zip(ls, dils)])
    return lts[0], lts[1], [l.reshape(T, LANES) for l in lts[2:]]


def _attn_bwd_prep(do, oc, dils, *, name):
    T, HD = do.shape
    H = HD // HEAD_DIM
    n = len(dils)
    tr = _tile(T, 512, SUBLANES)
    n_sl = HD // LANES
    wide = pl.BlockSpec((tr, HD), lambda i: (i, 0))

    def body(*refs):
        do_ref, oc_ref = refs[:2]
        dl_refs, dob_refs = refs[2:2 + n], refs[2 + n:2 + 2 * n]
        do_scr, dl_scr = refs[2 + 2 * n], refs[3 + 2 * n]
        low = lax.broadcasted_iota(jnp.int32, (tr, PAIR), 1) < HEAD_DIM
        lane = lax.broadcasted_iota(jnp.int32, (tr, LANES), 1)
        dl = jnp.zeros((tr, LANES), F32)
        for p in range(H // 2):
            d = do_ref[:, p * PAIR:(p + 1) * PAIR]
            do_scr[p] = d
            prod = d * oc_ref[:, p * PAIR:(p + 1) * PAIR]
            dl = jnp.where(lane == 2 * p, jnp.sum(jnp.where(low, prod, 0.0), axis=-1, keepdims=True), dl)
            dl = jnp.where(lane == 2 * p + 1, jnp.sum(jnp.where(low, 0.0, prod), axis=-1, keepdims=True), dl)
        dl_scr[0] = dl
        for g in range(n):
            _from_token_slabs(dl_scr, dl_refs[g], dils[g])
            _from_token_slabs(do_scr, dob_refs[g], dils[g])

    outs = pl.pallas_call(
        body, name=name, grid=(T // tr,),
        in_specs=[wide, wide],
        out_specs=[_grouped_spec(tr, d, LANES) for d in dils] + [_grouped_spec(tr, d, HD) for d in dils],
        out_shape=[jax.ShapeDtypeStruct((d, T // d, LANES), F32) for d in dils]
        + [jax.ShapeDtypeStruct((d, T // d, HD), BF16) for d in dils],
        scratch_shapes=[pltpu.VMEM((n_sl, tr, LANES), F32), pltpu.VMEM((1, tr, LANES), F32)],
        compiler_params=_cp(("parallel",)),
    )(do, oc)
    return [a.reshape(T, LANES) for a in outs[:n]], [a.reshape(T, HD) for a in outs[n:]]


def _attn_bwd(q, k, v, bias, g, dil, do, lse, delta, *, name):
    (qa, qc), (ka, kc_), (va, vc_) = q, k, v
    T, HD = do.shape
    H = HD // HEAD_DIM
    nb = T // BLK
    scale = HEAD_DIM ** -0.5
    QB = _blocks_per_step(nb, dil)
    ns = nb // QB
    cur, prev, stat, bias0, bias1 = _attn_specs(nb, dil, g, H, HD, lambda b: jnp.minimum(b, ns - 1))
    late = pl.BlockSpec((QB * BLK, HD), lambda b: (jnp.maximum(b - 1, 0), 0))

    def body(q_ref, kp_ref, kc_ref, vp_ref, vc_ref, b0_ref, b1_ref, do_ref, l_ref, dl_ref,
             dq_ref, dk_ref, dv_ref, db_ref, ck_ref, cv_ref):
        b = pl.program_id(0)

        @pl.when(b == 0)
        def _():
            db_ref[...] = jnp.zeros_like(db_ref)
            ck_ref[...] = jnp.zeros_like(ck_ref)
            cv_ref[...] = jnp.zeros_like(cv_ref)

        @pl.when(b < ns)
        def _():
            dk_ref[...] = ck_ref[...]
            dv_ref[...] = cv_ref[...]
            for i in range(QB):
                kpi, vpi, ip = (kp_ref, vp_ref, 0) if i == 0 else (kc_ref, vc_ref, i - 1)
                qm, dom = _heads(q_ref, i, H, True), _heads(do_ref, i, H, True)
                sp, sc = _scores(qm, _heads(kpi, ip, H, False), _heads(kc_ref, i, H, False),
                                 (b0_ref if i == 0 else b1_ref)[...])
                l, dl = _stat_cols(l_ref, i, H), _stat_cols(dl_ref, i, H)
                pp, pc = jnp.exp(sp - l), jnp.exp(sc - l)
                dsp = pp * (_bdot(dom, _heads(vpi, ip, H, False), 2, 2) - dl)
                dsc = pc * (_bdot(dom, _heads(vc_ref, i, H, False), 2, 2) - dl)
                db_ref[:, :, :BLK] += dsp
                db_ref[:, :, BLK:] += dsc
                dspb, dscb = dsp.astype(BF16), dsc.astype(BF16)
                _store_pairs(dq_ref, i, _bdot(dspb, _heads(kpi, ip, H, True), 2, 1)
                             + _bdot(dscb, _heads(kc_ref, i, H, True), 2, 1), F32, scale)
                dkp = _pair_sums(_bdot(dspb, qm, 1, 1))
                dkc = _pair_sums(_bdot(dscb, qm, 1, 1))
                dvp = _pair_sums(_bdot(pp.astype(BF16), dom, 1, 1))
                dvc = _pair_sums(_bdot(pc.astype(BF16), dom, 1, 1))
                for p in range(H // 2):
                    cols = slice(p * PAIR, (p + 1) * PAIR)
                    if i == 0:
                        dk_ref[_blk(QB - 1), cols] += scale * dkp[p]
                        dv_ref[_blk(QB - 1), cols] += dvp[p]
                    else:
                        ck_ref[_blk(i - 1), cols] += scale * dkp[p]
                        cv_ref[_blk(i - 1), cols] += dvp[p]
                    ck_ref[_blk(i), cols] = scale * dkc[p]
                    cv_ref[_blk(i), cols] = dvc[p]

        @pl.when(b == ns)
        def _():
            dk_ref[...] = ck_ref[...]
            dv_ref[...] = cv_ref[...]

    f32 = jax.ShapeDtypeStruct((T, HD), F32)
    return pl.pallas_call(
        body, name=name, grid=(ns + 1,),
        in_specs=[cur(qc), prev(kc_), cur(kc_), prev(vc_), cur(vc_), bias0, bias1, cur(0), stat, stat],
        out_specs=[cur(0), late, late, pl.BlockSpec((H, BLK, 2 * BLK), lambda b: (0, 0, 0))],
        out_shape=[f32, f32, f32,
                   jax.ShapeDtypeStruct((H, BLK, 2 * BLK), F32)],
        scratch_shapes=[pltpu.VMEM((QB * BLK, HD), F32), pltpu.VMEM((QB * BLK, HD), F32)],
        compiler_params=_cp(("arbitrary",)),
    )(qa, ka, ka, va, va, bias, bias, do, lse, delta)


def _rel_grad(dbs, *, name):
    H = dbs[0].shape[1]
    n = len(dbs)
    assert H <= LANES
    bkt = jnp.asarray(_bucket_table())

    def body(*refs):
        bk = refs[n][...]
        o_ref = refs[n + 1]
        db = refs[0][...]
        for r in refs[1:n]:
            db = db + r[...]
        row = lax.broadcasted_iota(jnp.int32, (REL_BUCKETS, LANES), 0)
        col = lax.broadcasted_iota(jnp.int32, (REL_BUCKETS, LANES), 1)

        def bucket(c, acc):
            sel = bk == c
            for h in range(H):
                val = jnp.sum(jnp.where(sel, db[h], 0.0))
                acc = jnp.where((row == c) & (col == h), val, acc)
            return acc

        o_ref[...] = lax.fori_loop(0, REL_BUCKETS, bucket, jnp.zeros((REL_BUCKETS, LANES), F32))

    dspec = pl.BlockSpec((None, H, BLK, 2 * BLK), lambda g: (g, 0, 0, 0))
    return pl.pallas_call(
        body, name=name, grid=(N_GROUPS,),
        in_specs=[dspec] * n + [pl.BlockSpec((None, BLK, 2 * BLK), lambda g: (g, 0, 0))],
        out_specs=pl.BlockSpec((None, REL_BUCKETS, LANES), lambda g: (g, 0, 0)),
        out_shape=jax.ShapeDtypeStruct((N_GROUPS, REL_BUCKETS, LANES), F32),
        compiler_params=_cp(("parallel",)),
    )(*dbs, bkt)


def _to_group(a, d):
    T, C = a.shape
    return a if d == 1 else a.reshape(T // d, d, C).transpose(1, 0, 2).reshape(T, C)


def _from_group(a, d):
    T, C = a.shape
    return a if d == 1 else a.reshape(d, T // d, C).transpose(1, 0, 2).reshape(T, C)


def _rows_view(a):
    a2 = a.reshape(-1, a.shape[-1])
    R, C = a2.shape
    tr = _tile(R, max(SUBLANES, (1 << 18) // C), SUBLANES)
    return a2, R, C, tr


def _addn(xs, out_dtype, *, name):
    shape = xs[0].shape
    x2s = [_rows_view(x)[0] for x in xs]
    _, R, C, tr = _rows_view(xs[0])
    spec = pl.BlockSpec((tr, C), lambda i: (i, 0))

    def body(*refs):
        acc = refs[0][...].astype(F32)
        for r in refs[1:-1]:
            acc = acc + r[...].astype(F32)
        refs[-1][...] = acc.astype(out_dtype)

    out = pl.pallas_call(
        body, name=name, grid=(R // tr,),
        in_specs=[spec] * len(xs), out_specs=spec,
        out_shape=jax.ShapeDtypeStruct((R, C), out_dtype),
        compiler_params=_cp(("parallel",)),
    )(*x2s)
    return out.reshape(shape)


def _addn_grouped(parts, out_dtype, *, name):
    T, C = parts[0][0].shape
    tr = _tile(T, 512, SUBLANES)
    assert all(tr % d == 0 for _, d in parts)

    assert C % LANES == 0

    def body(*refs):
        o_ref, acc_ref = refs[-2], refs[-1]
        acc_ref[...] = jnp.zeros_like(acc_ref)
        for ref, (_, d) in zip(refs, parts):
            for r in range(d):
                for j in range(C // LANES):
                    acc_ref.at[j][pl.ds(r, tr // d, stride=d), :] += ref[r, :, j * LANES:(j + 1) * LANES]
        for j in range(C // LANES):
            o_ref[:, j * LANES:(j + 1) * LANES] = acc_ref[j].astype(out_dtype)

    return pl.pallas_call(
        body, name=name, grid=(T // tr,),
        in_specs=[pl.BlockSpec((d, tr // d, C), lambda i: (0, i, 0)) for _, d in parts],
        out_specs=pl.BlockSpec((tr, C), lambda i: (i, 0)),
        out_shape=jax.ShapeDtypeStruct((T, C), out_dtype),
        scratch_shapes=[pltpu.VMEM((C // LANES, tr, LANES), F32)],
        compiler_params=_cp(("parallel",)),
    )(*[a.reshape(d, T // d, C) for a, d in parts])


def _slot_sum(b_ref, own, my_id):
    acc = None
    for s in range(b_ref.shape[0]):
        term = jnp.where(my_id == s, own, b_ref[s].astype(F32))
        acc = term if acc is None else acc + term
    return acc


def _part_geometry(shape, axis):
    _, R, C = shape
    if axis == COL:
        return R // 2, C // N_CHIPS
    return R // (2 * N_CHIPS), C


def _sum_parts(buf, grad, axis, where, *, name):
    n, L, Rp, Cp = buf.shape
    assert (Rp, Cp) == _part_geometry(grad.shape, axis)
    tr = _tile(Rp, max(2 * SUBLANES, (1 << 18) // Cp), 2 * SUBLANES)
    nr = Rp // tr

    def own_map(l, i, w):
        if axis == COL:
            return (l, w[1] * nr + i, w[0])
        return (l, (2 * w[0] + w[1]) * nr + i, 0)

    def body(w_ref, b_ref, g_ref, o_ref):
        o_ref[...] = _slot_sum(b_ref, g_ref[...].astype(F32), w_ref[2])

    return pl.pallas_call(
        body, name=name,
        grid_spec=pltpu.PrefetchScalarGridSpec(
            num_scalar_prefetch=1, grid=(L, nr),
            in_specs=[pl.BlockSpec((n, None, tr, Cp), lambda l, i, w: (0, l, i, 0)),
                      pl.BlockSpec((None, tr, Cp), own_map)],
            out_specs=pl.BlockSpec((None, None, tr, Cp), lambda l, i, w: (l, w[1], i, 0))),
        out_shape=jax.ShapeDtypeStruct((L, 2, Rp, Cp), F32),
        compiler_params=_cp(("parallel", "parallel")),
    )(where, buf, grad)


def _sum_small(buf, own, where, *, name):
    n, R, C = buf.shape
    tr = _tile(R, 512, SUBLANES)

    def body(w_ref, b_ref, g_ref, o_ref):
        o_ref[...] = _slot_sum(b_ref, g_ref[...], w_ref[2])

    return pl.pallas_call(
        body, name=name,
        grid_spec=pltpu.PrefetchScalarGridSpec(
            num_scalar_prefetch=1, grid=(R // tr,),
            in_specs=[pl.BlockSpec((n, tr, C), lambda i, w: (0, i, 0)),
                      pl.BlockSpec((tr, C), lambda i, w: (i, 0))],
            out_specs=pl.BlockSpec((tr, C), lambda i, w: (i, 0))),
        out_shape=jax.ShapeDtypeStruct((R, C), F32),
        compiler_params=_cp(("parallel",)),
    )(where, buf, own)


def _place_shard(shard, layer, axis, where, out_dtype, *, name, after=None):
    _, R, C = shard.shape
    tr = _tile(R, max(2 * SUBLANES, (1 << 19) // C), 2 * SUBLANES)
    nr = R // tr
    if axis == COL:
        full, out_map = (1, R, C * N_CHIPS), (lambda i, w: (0, i, w[0]))
    else:
        full, out_map = (1, R * N_CHIPS, C), (lambda i, w: (0, w[0] * nr + i, 0))

    def body(w_ref, x_ref, *rest):
        rest[-1][...] = x_ref[...].astype(out_dtype)

    extra = [] if after is None else [after]
    return pl.pallas_call(
        body, name=name,
        grid_spec=pltpu.PrefetchScalarGridSpec(
            num_scalar_prefetch=1, grid=(nr,),
            in_specs=[pl.BlockSpec((None, tr, C), lambda i, w: (layer, i, 0))]
            + [pl.BlockSpec(memory_space=pl.ANY)] * len(extra),
            out_specs=pl.BlockSpec((None, tr, C), out_map)),
        out_shape=jax.ShapeDtypeStruct(full, out_dtype),
        compiler_params=_cp(("parallel",)),
    )(where, shard, *extra)


def _loss_grad(y, tgt, *, name):
    T, D = y.shape
    tr = _tile(T, 512, SUBLANES)
    row = pl.BlockSpec((tr, D), lambda i: (i, 0))

    def body(y_ref, t_ref, dy_ref, l_ref, acc_ref):
        i = pl.program_id(0)

        @pl.when(i == 0)
        def _():
            acc_ref[...] = jnp.zeros_like(acc_ref)

        err = y_ref[...] - t_ref[...]
        dy_ref[...] = err * (1.0 / D)
        acc_ref[...] += jnp.sum(err * err, axis=0, keepdims=True)

        @pl.when(i == T // tr - 1)
        def _():
            tot = jnp.sum(acc_ref[...], axis=1, keepdims=True) * (0.5 / D)
            l_ref[...] = jnp.broadcast_to(tot, (1, LANES))

    return pl.pallas_call(
        body, name=name, grid=(T // tr,),
        in_specs=[row, row],
        out_specs=[row, pl.BlockSpec((1, LANES), lambda i: (0, 0))],
        out_shape=[jax.ShapeDtypeStruct((T, D), F32), jax.ShapeDtypeStruct((1, LANES), F32)],
        scratch_shapes=[pltpu.VMEM((1, D), F32)],
        compiler_params=_cp(("arbitrary",)),
    )(y, tgt)


def _adamw(w, g, m, v, *, name):
    shape = w.shape
    w2, R, C, tr = _rows_view(w)
    spec = pl.BlockSpec((tr, C), lambda i: (i, 0))

    def body(w_ref, g_ref, m_ref, v_ref, d_ref, nm_ref, nv_ref):
        gv = g_ref[...]
        nm = ADAM_B1 * m_ref[...] + (1.0 - ADAM_B1) * gv
        nv = ADAM_B2 * v_ref[...] + (1.0 - ADAM_B2) * (gv * gv)
        m_hat = nm / (1.0 - ADAM_B1 ** ADAM_STEP)
        v_hat = nv / (1.0 - ADAM_B2 ** ADAM_STEP)
        d_ref[...] = -ADAM_LR * (m_hat / (jnp.sqrt(v_hat) + ADAM_EPS) + ADAM_WD * w_ref[...])
        nm_ref[...] = nm
        nv_ref[...] = nv

    shp = jax.ShapeDtypeStruct((R, C), F32)
    outs = pl.pallas_call(
        body, name=name, grid=(R // tr,),
        in_specs=[spec] * 4, out_specs=[spec] * 3, out_shape=[shp] * 3,
        compiler_params=_cp(("parallel",)),
    )(w2, g.reshape(R, C), m.reshape(R, C), v.reshape(R, C))
    return tuple(o.reshape(shape) for o in outs)


def _pack(arrs):
    flat = jnp.concatenate([a.reshape(-1).astype(F32) for a in arrs])
    unit = SUBLANES * LANES
    pad = (-flat.shape[0]) % unit
    return jnp.pad(flat, (0, pad)).reshape(-1, LANES)


def _unpack(buf, shapes):
    flat = buf.reshape(-1)
    out, off = [], 0
    for s in shapes:
        n = int(np.prod(s))
        out.append(flat[off:off + n].reshape(s))
        off += n
    return out


def _me():
    return lax.axis_index("x"), lax.axis_index("y"), lax.axis_index("c")


def _flip(pos, k):
    x, y, c = pos
    return (1 - x if k & 4 else x, 1 - y if k & 2 else y, 1 - c if k & 1 else c)


HBM = pl.BlockSpec(memory_space=pltpu.HBM)

COL, ROW = -1, -2


def _shard_window(ref, axis, j, n):
    start = pl.multiple_of(j * n, n)
    if axis == COL:
        return ref.at[:, :, pl.ds(start, n)]
    return ref.at[:, pl.ds(start, n), :]


GATHER_PEERS = (2, 4, 6)
SEM = pl.BlockSpec(memory_space=pltpu.SEMAPHORE)
EFFECT = pltpu.SideEffectType.DATAFLOW_SIDE_EFFECTING


def _gather_copy(buf, axis, i, pi, chip, me, send_sems, recv_sems):
    win = _shard_window(buf, axis, chip, buf.shape[axis] // N_CHIPS)
    k = len(GATHER_PEERS) * i + pi
    return pltpu.make_async_remote_copy(
        src_ref=win, dst_ref=win, send_sem=send_sems.at[k], recv_sem=recv_sems.at[k],
        device_id=_flip(me, GATHER_PEERS[pi]), device_id_type=MESH)


def _gather_start(fulls, axes, *, name):
    n = len(fulls)
    n_sem = len(GATHER_PEERS) * n

    def body(*refs):
        send_sems, recv_sems = refs[n], refs[n + 1]
        bufs = refs[n + 2:2 * n + 2]
        token = refs[2 * n + 2]
        me = _me()
        for i in range(n):
            for pi in range(len(GATHER_PEERS)):
                _gather_copy(bufs[i], axes[i], i, pi, 2 * me[0] + me[1], me, send_sems, recv_sems).start()
        token[...] = jnp.zeros_like(token)

    outs = pl.pallas_call(
        body, name=name,
        in_specs=[HBM] * n, out_specs=[SEM, SEM] + [HBM] * n + [pl.BlockSpec(memory_space=pltpu.VMEM)],
        out_shape=[pltpu.SemaphoreType.DMA((n_sem,)), pltpu.SemaphoreType.DMA((n_sem,))]
        + [pltpu.HBM(f.shape, f.dtype) for f in fulls] + [jax.ShapeDtypeStruct((SUBLANES, LANES), F32)],
        input_output_aliases={i: 2 + i for i in range(n)},
        compiler_params=pltpu.CompilerParams(has_side_effects=EFFECT),
    )(*[pltpu.with_memory_space_constraint(f, pltpu.HBM) for f in fulls])
    return outs[0], outs[1], list(outs[2:2 + n]), outs[-1]


def _gather_wait(send_sems, recv_sems, bufs, axes, idxs, after, *, name):
    m = len(bufs)

    def body(*refs):
        ss, rs = refs[m], refs[m + 1]
        outs = refs[m + 3:]
        me = _me()
        for t, i in enumerate(idxs):
            for pi, k in enumerate(GATHER_PEERS):
                px, py, _ = _flip(me, k)
                _gather_copy(outs[t], axes[t], i, pi, 2 * me[0] + me[1], me, ss, rs).wait_send()
                _gather_copy(outs[t], axes[t], i, pi, 2 * px + py, me, ss, rs).wait_recv()

    return pl.pallas_call(
        body, name=name,
        in_specs=[HBM] * m + [SEM, SEM, pl.BlockSpec(memory_space=pl.ANY)], out_specs=[HBM] * m,
        out_shape=[pltpu.HBM(b.shape, b.dtype) for b in bufs],
        input_output_aliases={t: t for t in range(m)},
        compiler_params=pltpu.CompilerParams(has_side_effects=EFFECT),
    )(*bufs, send_sems, recv_sems, after)


def _grad_part(ref, layer, axis, pos):
    px, py, pc = pos
    Rp, Cp = _part_geometry(ref.shape, axis)
    chip = 2 * px + py
    if axis == COL:
        return ref.at[layer, pl.ds(pl.multiple_of(pc * Rp, Rp), Rp), pl.ds(pl.multiple_of(chip * Cp, Cp), Cp)]
    return ref.at[layer, pl.ds(pl.multiple_of((2 * chip + pc) * Rp, Rp), Rp), :]


def _scatter_copy(grad, slots, layer, axis, i, k, src_pos, dst_slot, me, send_sems, recv_sems):
    sem = (N_DEV - 1) * i + k - 1
    return pltpu.make_async_remote_copy(
        src_ref=_grad_part(grad, layer, axis, src_pos), dst_ref=slots.at[dst_slot, layer],
        send_sem=send_sems.at[sem], recv_sem=recv_sems.at[sem],
        device_id=_flip(me, k), device_id_type=MESH)


def _scatter_start(grads, slots, layers, axes, *, name):
    n = len(grads)
    n_sem = (N_DEV - 1) * n

    def body(*refs):
        send_sems, recv_sems = refs[2 * n], refs[2 * n + 1]
        g_refs, s_refs = refs[2 * n + 2:3 * n + 2], refs[3 * n + 2:4 * n + 2]
        token = refs[4 * n + 2]
        me = _me()
        my_id = 4 * me[0] + 2 * me[1] + me[2]
        for i in range(n):
            for k in range(1, N_DEV):
                _scatter_copy(g_refs[i], s_refs[i], layers[i], axes[i], i, k, _flip(me, k), my_id, me,
                              send_sems, recv_sems).start()
        token[...] = jnp.zeros_like(token)

    arrays = list(grads) + list(slots)
    outs = pl.pallas_call(
        body, name=name,
        in_specs=[HBM] * (2 * n),
        out_specs=[SEM, SEM] + [HBM] * (2 * n) + [pl.BlockSpec(memory_space=pltpu.VMEM)],
        out_shape=[pltpu.SemaphoreType.DMA((n_sem,)), pltpu.SemaphoreType.DMA((n_sem,))]
        + [pltpu.HBM(a.shape, a.dtype) for a in arrays] + [jax.ShapeDtypeStruct((SUBLANES, LANES), F32)],
        input_output_aliases={i: 2 + i for i in range(2 * n)},
        compiler_params=pltpu.CompilerParams(has_side_effects=EFFECT),
    )(*[pltpu.with_memory_space_constraint(a, pltpu.HBM) for a in arrays])
    return outs[0], outs[1], list(outs[2:2 + n]), list(outs[2 + n:2 + 2 * n]), outs[-1]


def _scatter_wait(sems, grads, slots, axes, plan, after, *, name):
    n = len(grads)
    flat_sems = [s for pair in sems for s in pair]

    def body(*refs):
        sem_refs = refs[2 * n:2 * n + len(flat_sems)]
        first_out = 2 * n + len(flat_sems) + 1
        g_refs, s_refs = refs[first_out:first_out + n], refs[first_out + n:]
        me = _me()
        for j, items in enumerate(plan):
            ss, rs = sem_refs[2 * j], sem_refs[2 * j + 1]
            for i, (a, layer) in enumerate(items):
                for k in range(1, N_DEV):
                    px, py, pc = _flip(me, k)
                    cp = _scatter_copy(g_refs[a], s_refs[a], layer, axes[a], i, k, me, 4 * px + 2 * py + pc, me, ss, rs)
                    cp.wait_send()
                    cp.wait_recv()

    arrays = list(grads) + list(slots)
    outs = pl.pallas_call(
        body, name=name,
        in_specs=[HBM] * (2 * n) + [SEM] * len(flat_sems) + [pl.BlockSpec(memory_space=pl.ANY)],
        out_specs=[HBM] * (2 * n),
        out_shape=[pltpu.HBM(a.shape, a.dtype) for a in arrays],
        input_output_aliases={i: i for i in range(2 * n)},
        compiler_params=pltpu.CompilerParams(has_side_effects=EFFECT),
    )(*arrays, *flat_sems, after)
    return list(outs[:n]), list(outs[n:])


def _small_copy(src, dst, k, slot, me, send_sems, recv_sems):
    return pltpu.make_async_remote_copy(
        src_ref=src, dst_ref=dst.at[slot], send_sem=send_sems.at[k - 1], recv_sem=recv_sems.at[k - 1],
        device_id=_flip(me, k), device_id_type=MESH)


def _small_start(small, slots, after, *, name):
    def body(src_in, dst_in, after_ref, send_sems, recv_sems, src, dst):
        me = _me()
        for k in range(1, N_DEV):
            _small_copy(src, dst, k, 4 * me[0] + 2 * me[1] + me[2], me, send_sems, recv_sems).start()

    arrays = [small, slots]
    return pl.pallas_call(
        body, name=name, in_specs=[HBM, HBM, pl.BlockSpec(memory_space=pl.ANY)], out_specs=[SEM, SEM, HBM, HBM],
        out_shape=[pltpu.SemaphoreType.DMA((N_DEV - 1,)), pltpu.SemaphoreType.DMA((N_DEV - 1,))]
        + [pltpu.HBM(a.shape, a.dtype) for a in arrays],
        input_output_aliases={0: 2, 1: 3},
        compiler_params=pltpu.CompilerParams(has_side_effects=EFFECT),
    )(*[pltpu.with_memory_space_constraint(a, pltpu.HBM) for a in arrays], after)


def _small_wait(send_sems, recv_sems, small, slots, after, *, name):
    def body(src_in, dst_in, ss, rs, after_ref, src, dst):
        me = _me()
        for k in range(1, N_DEV):
            px, py, pc = _flip(me, k)
            cp = _small_copy(src, dst, k, 4 * px + 2 * py + pc, me, ss, rs)
            cp.wait_send()
            cp.wait_recv()

    return pl.pallas_call(
        body, name=name, in_specs=[HBM, HBM, SEM, SEM, pl.BlockSpec(memory_space=pl.ANY)], out_specs=[HBM, HBM],
        out_shape=[pltpu.HBM(small.shape, small.dtype), pltpu.HBM(slots.shape, slots.dtype)],
        input_output_aliases={0: 0, 1: 1},
        compiler_params=pltpu.CompilerParams(has_side_effects=EFFECT),
    )(small, slots, send_sems, recv_sems, after)


def _pair_gather(halves, *, name):
    n = len(halves)

    def body(*refs):
        bufs = refs[n:2 * n]
        send_sems, recv_sems = refs[2 * n:]
        me = _me()
        c = me[2]
        sib = _flip(me, 1)

        def copy(i, half):
            return pltpu.make_async_remote_copy(
                src_ref=bufs[i].at[:, half], dst_ref=bufs[i].at[:, half], send_sem=send_sems.at[i],
                recv_sem=recv_sems.at[i], device_id=sib, device_id_type=MESH)

        sends = [copy(i, c) for i in range(n)]
        for cp in sends:
            cp.start()
        for i in range(n):
            copy(i, 1 - c).wait_recv()
        for cp in sends:
            cp.wait_send()

    return pl.pallas_call(
        body, name=name,
        in_specs=[HBM] * n, out_specs=[HBM] * n,
        out_shape=[jax.ShapeDtypeStruct(h.shape, h.dtype) for h in halves],
        input_output_aliases={i: i for i in range(n)},
        scratch_shapes=[pltpu.SemaphoreType.DMA((n,)), pltpu.SemaphoreType.DMA((n,))],
    )(*halves)


def _local_step(x, tgt, P, fetch, ship, n_layers):
    T, D = x.shape
    P = dict(P)
    W = {}
    depth = P["ffn_conv_b"].shape[0]
    n_a = P["a_w_s"].shape[0]
    alpha = (2 * depth) ** 0.25
    H = P["rel_table"].shape[1] // N_GROUPS
    HD = H * HEAD_DIM
    dils = [d for _, d in DILATED_GROUPS]
    assert T % (dils[-1] * BLK) == 0

    def arrive(stage, after):
        w_new, p_new = fetch(stage, after)
        W.update(w_new)
        P.update(p_new)

    bias = _band_bias(P["rel_table"], H, name="band_bias")
    saved = []
    xf, xb = x, x.astype(BF16)
    kg = vg = None
    for i in range(depth):
        s = {"x": xf, "xb": xb}
        arrive(f"{i}a", xf)
        if i < n_a:
            s["zp"] = _mm(xb, W["a_w_in", i], out_dtype=BF16, name=f"a{i}_in")
            arrive(f"{i}b", s["zp"])
            s["y"] = _sgu_fwd(s["zp"], P["a_w_s"][i], P["a_b_s"][i], P["a_ln_g"][i], P["a_ln_b"][i], name=f"a{i}_sgu")
            s["h"] = _mm(s["y"], W["a_w_out", i], name=f"a{i}_out")
        else:
            j = i - n_a
            if j == 0:
                kv = _mm(xb, W["kv_w", 0], out_dtype=BF16, name="kv_proj")
                kg = [(kv, 0) if d == 1 else (_to_group(kv[:, :HD], d), 0) for d in dils]
                vg = [(kv, 1) if d == 1 else (_to_group(kv[:, HD:], d), 0) for d in dils]
            q = _mm(xb, W["b_w_q", j], out_dtype=BF16, name=f"b{j}_q")
            s["qg"] = [(q, g) if d == 1 else (_to_group(q[:, g * HD:(g + 1) * HD], d), 0) for g, d in enumerate(dils)]
            os, ls = [], []
            for g, d in enumerate(dils):
                o_g, l_g = _attn_fwd(s["qg"][g], kg[g], vg[g], bias, g, d, name=f"b{j}_attn{g}")
                os.append(o_g)
                ls.append(l_g)
            s["oc"], s["ob"], s["lse"] = _attn_combine(os, ls, dils, name=f"b{j}_comb")
            arrive(f"{i}b", s["ob"])
            s["h"] = _mm(s["ob"], W["b_w_o", j], name=f"b{j}_o")
        s["x1"], s["x1b"] = _add_ln(xf, s["h"], P["ln_g"][i, 0], P["ln_b"][i, 0], alpha, name=f"l{i}_ln1")
        arrive(f"{i}c", s["x1b"])
        s["hup"] = _mm(s["x1b"], W["ffn_w_up", i], out_dtype=BF16, name=f"l{i}_up")
        arrive(f"{i}d", s["hup"])
        s["act"], s["ca"], s["cg"] = _convglu_fwd(s["hup"], P["ffn_conv_w"][i], P["ffn_conv_b"][i], name=f"l{i}_glu")
        s["f"] = _mm(s["act"], W["ffn_w_down", i], name=f"l{i}_down")
        xf, xb = _add_ln(s["x1"], s["f"], P["ln_g"][i, 1], P["ln_b"][i, 1], alpha, name=f"l{i}_ln2")
        saved.append(s)

    G, loss_row = _loss_grad(xf, tgt, name="loss")

    gw = {}

    def dw(key, layer, a, b, name, **kw):
        gw[key] = _mm(a, b, ta=True, out_dtype=BF16, out_into=(gw.get(key), n_layers[key], layer), name=name, **kw)

    gp = {k: [None] * n_a for k in ("a_ln_g", "a_ln_b", "a_w_s", "a_b_s")}
    gp.update({k: [None] * depth for k in ("ffn_conv_w", "ffn_conv_b", "ln_g", "ln_b")})
    dk_parts, dv_parts, dbias_parts = [], [], []
    for i in reversed(range(depth)):
        s = saved[i]
        dr2, dr2b, dg2, db2 = _ln_bwd(G, s["x1"], s["f"], P["ln_g"][i, 1], alpha, name=f"l{i}_ln2_bwd")
        dw("ffn_w_down", i, s["act"], dr2b, f"l{i}_down_dw")
        dact = _mm(dr2b, W["ffn_w_down", i], tb=True, out_dtype=BF16, name=f"l{i}_down_dx")
        dhup, dcw, dcb = _convglu_bwd(s["hup"], s["ca"], s["cg"], dact, P["ffn_conv_w"][i], name=f"l{i}_glu_bwd")
        gp["ffn_conv_w"][i] = dcw.transpose(1, 0, 2).reshape(dcw.shape[1], -1)
        gp["ffn_conv_b"][i] = dcb.reshape(-1)
        dw("ffn_w_up", i, s["x1b"], dhup, f"l{i}_up_dw", b_halves=True)
        token = ship(gw, [("ffn_w_down", i), ("ffn_w_up", i)])
        G1 = _mm(dhup, W["ffn_w_up", i], tb=True, a_halves=True, add=dr2, add_scale=alpha, after=token,
                 name=f"l{i}_up_dx")
        dr1, dr1b, dg1, db1 = _ln_bwd(G1, s["x"], s["h"], P["ln_g"][i, 0], alpha, name=f"l{i}_ln1_bwd")
        gp["ln_g"][i] = jnp.concatenate([dg1, dg2], axis=0)
        gp["ln_b"][i] = jnp.concatenate([db1, db2], axis=0)
        if i < n_a:
            dw("a_w_out", i, s["y"], dr1b, f"a{i}_out_dw")
            token = ship(gw, [("a_w_out", i)])
            dy = _mm(dr1b, W["a_w_out", i], tb=True, out_dtype=BF16, after=token, name=f"a{i}_out_dx")
            dzp, dws, dbs, dlg, dlb = _sgu_bwd(s["zp"], dy, P["a_w_s"][i], P["a_b_s"][i], P["a_ln_g"][i],
                                               P["a_ln_b"][i], name=f"a{i}_sgu_bwd")
            gp["a_w_s"][i], gp["a_b_s"][i] = dws, dbs[:, :dws.shape[0]].T
            gp["a_ln_g"][i], gp["a_ln_b"][i] = dlg[0], dlb[0]
            dw("a_w_in", i, s["xb"], dzp, f"a{i}_in_dw")
            token = ship(gw, [("a_w_in", i)])
            G = _mm(dzp, W["a_w_in", i], tb=True, add=dr1, add_scale=alpha, after=token, name=f"a{i}_in_dx")
        else:
            j = i - n_a
            dw("b_w_o", j, s["ob"], dr1b, f"b{j}_o_dw")
            do_tok = _mm(dr1b, W["b_w_o", j], tb=True, name=f"b{j}_o_dx")
            deltas, dobs = _attn_bwd_prep(do_tok, s["oc"], dils, name=f"b{j}_prep")
            dqs, dbs = [], []
            for g, d in enumerate(dils):
                dq, dk, dv, db = _attn_bwd(s["qg"][g], kg[g], vg[g], bias, g, d, dobs[g], s["lse"][g], deltas[g],
                                           name=f"b{j}_attn{g}_bwd")
                dqs.append((dq, d))
                dk_parts.append((dk, d))
                dv_parts.append((dv, d))
                dbs.append(db)
            dbias_parts.append(jnp.stack(dbs))
            dq_tok = _regroup_to_tokens(dqs, BF16, name=f"b{j}_dq_tokens")
            dw("b_w_q", j, s["xb"], dq_tok, f"b{j}_q_dw")
            token = ship(gw, [("b_w_o", j), ("b_w_q", j)])
            G = _mm(dq_tok, W["b_w_q", j], tb=True, add=dr1, add_scale=alpha, after=token, name=f"b{j}_q_dx")
            if j == 0:
                dkv = jnp.concatenate([_addn_grouped(dk_parts, BF16, name="dk_sum"),
                                       _addn_grouped(dv_parts, BF16, name="dv_sum")], axis=1)
                dw("kv_w", 0, s["xb"], dkv, "kv_dw")
                token = ship(gw, [("kv_w", 0)])
                G = _mm(dkv, W["kv_w", 0], tb=True, add=G, add_scale=1.0, after=token, name="kv_dx")
    rel = _rel_grad(dbias_parts, name="rel_grad")
    grel = rel[:, :, :H].transpose(1, 0, 2).reshape(REL_BUCKETS, N_GROUPS * H)
    gsmall = {k: jnp.stack(v) for k, v in gp.items()}
    gsmall["rel_table"] = grel
    return loss_row, G, gw, gsmall


BIG = (("a_w_in", COL), ("a_w_out", ROW), ("kv_w", ROW), ("b_w_q", COL), ("b_w_o", COL),
       ("ffn_w_up", COL), ("ffn_w_down", ROW))
SMALL_SHARDED = ("a_ln_g", "a_ln_b", "ffn_conv_w", "ln_g", "ln_b")
SMALL_REPLICATED = ("a_w_s", "a_b_s", "rel_table", "ffn_conv_b")
WEIGHTS = ("a_w_in", "a_ln_g", "a_ln_b", "a_w_s", "a_b_s", "a_w_out", "kv_w", "b_w_q", "b_w_o", "rel_table",
           "ffn_w_up", "ffn_conv_w", "ffn_conv_b", "ffn_w_down", "ln_g", "ln_b")


def kernel(x, a_w_in, a_ln_g, a_ln_b, a_w_s, a_b_s, a_w_out, kv_w, b_w_q, b_w_o, rel_table, ffn_w_up, ffn_conv_w, ffn_conv_b, ffn_w_down, ln_g, ln_b, loss_target, m_a_w_in, m_a_ln_g, m_a_ln_b, m_a_w_s, m_a_b_s, m_a_w_out, m_kv_w, m_b_w_q, m_b_w_o, m_rel_table, m_ffn_w_up, m_ffn_conv_w, m_ffn_conv_b, m_ffn_w_down, m_ln_g, m_ln_b, v_a_w_in, v_a_ln_g, v_a_ln_b, v_a_w_s, v_a_b_s, v_a_w_out, v_kv_w, v_b_w_q, v_b_w_o, v_rel_table, v_ffn_w_up, v_ffn_conv_w, v_ffn_conv_b, v_ffn_w_down, v_ln_g, v_ln_b):
    w = dict(a_w_in=a_w_in, a_ln_g=a_ln_g, a_ln_b=a_ln_b, a_w_s=a_w_s, a_b_s=a_b_s, a_w_out=a_w_out, kv_w=kv_w,
             b_w_q=b_w_q, b_w_o=b_w_o, rel_table=rel_table, ffn_w_up=ffn_w_up, ffn_conv_w=ffn_conv_w,
             ffn_conv_b=ffn_conv_b, ffn_w_down=ffn_w_down, ln_g=ln_g, ln_b=ln_b)
    m = dict(a_w_in=m_a_w_in, a_ln_g=m_a_ln_g, a_ln_b=m_a_ln_b, a_w_s=m_a_w_s, a_b_s=m_a_b_s, a_w_out=m_a_w_out,
             kv_w=m_kv_w, b_w_q=m_b_w_q, b_w_o=m_b_w_o, rel_table=m_rel_table, ffn_w_up=m_ffn_w_up,
             ffn_conv_w=m_ffn_conv_w, ffn_conv_b=m_ffn_conv_b, ffn_w_down=m_ffn_w_down, ln_g=m_ln_g, ln_b=m_ln_b)
    v = dict(a_w_in=v_a_w_in, a_ln_g=v_a_ln_g, a_ln_b=v_a_ln_b, a_w_s=v_a_w_s, a_b_s=v_a_b_s, a_w_out=v_a_w_out,
             kv_w=v_kv_w, b_w_q=v_b_w_q, b_w_o=v_b_w_o, rel_table=v_rel_table, ffn_w_up=v_ffn_w_up,
             ffn_conv_w=v_ffn_conv_w, ffn_conv_b=v_ffn_conv_b, ffn_w_down=v_ffn_w_down, ln_g=v_ln_g, ln_b=v_ln_b)
    chip = 2 * lax.axis_index("x") + lax.axis_index("y")

    big_names = [n for n, _ in BIG]
    big_axes = [a for _, a in BIG]
    where = jnp.stack([chip, lax.axis_index("c"), 2 * chip + lax.axis_index("c")]).astype(jnp.int32)
    shards = [w[n].reshape((1,) + w[n].shape) if w[n].ndim == 2 else w[n] for n in big_names]
    small_shapes = [w[n].shape for n in SMALL_SHARDED]
    small = _pack([w[n] for n in SMALL_SHARDED])
    shard_of = dict(zip(big_names, shards))
    axis_of = dict(BIG)
    n_layers = {n: s.shape[0] for n, s in shard_of.items()}
    n_a = w["a_w_s"].shape[0]
    stages = {}
    for i in range(w["ffn_w_up"].shape[0]):
        if i < n_a:
            stages[f"{i}a"], stages[f"{i}b"] = [("a_w_in", i)], [("a_w_out", i)]
        else:
            stages[f"{i}a"] = ([("kv_w", 0)] if i == n_a else []) + [("b_w_q", i - n_a)]
            stages[f"{i}b"] = [("b_w_o", i - n_a)]
        stages[f"{i}c"], stages[f"{i}d"] = [("ffn_w_up", i)], [("ffn_w_down", i)]
    stages["0a"].append(("small", 0))
    assert sum(len(st) for st in stages.values()) == sum(n_layers.values()) + 1
    flights, full_shape, all_started = {}, {}, None
    for batch, names in enumerate(([st for st in stages if st[0] == "0"], [st for st in stages if st[0] != "0"])):
        order = [key for st in names for key in stages[st]]
        placed, axes = [], []
        for n, l in order:
            if n == "small":
                placed.append(_place_shard(small[None], 0, ROW, where, F32, name="place_small"))
                axes.append(ROW)
            else:
                placed.append(_place_shard(shard_of[n], l, axis_of[n], where, BF16, name=f"place_{n}{l}",
                                           after=all_started))
                axes.append(axis_of[n])
                full_shape[n] = (n_layers[n],) + placed[-1].shape[1:]
        send_sems, recv_sems, flying, all_started = _gather_start(placed, axes, name=f"gather_start{batch}")
        for st in names:
            flights[st] = (send_sems, recv_sems, flying, axes, order)

    def fetch(stage, after):
        if stage not in stages:
            return {}, {}
        send_sems, recv_sems, flying, axes, order = flights[stage]
        if stage == "0a":
            after = all_started
        idxs = [order.index(key) for key in stages[stage]]
        landed = _gather_wait(send_sems, recv_sems, [flying[i] for i in idxs], [axes[i] for i in idxs], idxs, after,
                              name=f"gather_wait_{stage}")
        W_new, P_new = {}, {}
        for key, arr in zip(stages[stage], landed):
            if key[0] == "small":
                small_all = arr.reshape((N_CHIPS,) + small.shape)
                per_chip = [_unpack(small_all[j], small_shapes) for j in range(N_CHIPS)]
                for i, n in enumerate(SMALL_SHARDED):
                    P_new[n] = jnp.concatenate([per_chip[j][i] for j in range(N_CHIPS)], axis=-1)
            else:
                W_new[key] = arr.reshape(arr.shape[1:])
        return W_new, P_new

    slots ={n: lax.empty((N_DEV, n_layers[n]) + _part_geometry(full_shape[n], axis_of[n]), BF16) for n in big_names}
    sems, plan, tokens = [], [], []

    def ship(gw, items):
        names = [n for n, _ in items]
        send, recv, grads, landing, token = _scatter_start(
            [gw[n] for n in names], [slots[n] for n in names], [l for _, l in items], [axis_of[n] for n in names],
            name="scatter_start_" + "_".join(f"{n}{l}" for n, l in items))
        for n, g, s in zip(names, grads, landing):
            gw[n], slots[n] = g, s
        sems.append((send, recv))
        plan.append([(big_names.index(n), l) for n, l in items])
        tokens.append(token)
        return token

    loss_row, grad_x, gw, gsmall = _local_step(x[0], loss_target[0], {n: w[n] for n in SMALL_REPLICATED}, fetch,
                                               ship, n_layers)

    small_names = list(SMALL_SHARDED) + list(SMALL_REPLICATED)
    small_pack = _pack([gsmall[n] for n in small_names] + [loss_row[0, :1]])
    small_sems = _small_start(small_pack, lax.empty((N_DEV,) + small_pack.shape, F32), tokens[-1],
                              name="small_start")
    own, landed = _scatter_wait(sems, [gw[n] for n in big_names], [slots[n] for n in big_names], big_axes, plan,
                                grad_x, name="scatter_wait")
    sums = [_sum_parts(b, g, ax, where, name=f"sum_{n}") for (n, ax), b, g in zip(BIG, landed, own)]
    pairs = _pair_gather(sums, name="pair_gather")
    grad = {n: p.reshape(w[n].shape) for n, p in zip(big_names, pairs)}

    delta, new_m, new_v = {}, {}, {}
    for n, _ in BIG:
        delta[n], new_m[n], new_v[n] = _adamw(w[n], grad[n], m[n], v[n], name=f"adamw_{n}")
    small_own, small_slots = _small_wait(*small_sems, new_v[big_names[-1]], name="small_wait")
    small_sum = _sum_small(small_slots, small_own, where, name="sum_small")
    full_small = _unpack(small_sum, [gsmall[n].shape for n in small_names] + [(1,)])
    loss = full_small[-1][0]
    for n, gfull in zip(small_names, full_small):
        if n in SMALL_SHARDED:
            width = w[n].shape[-1]
            grad[n] = lax.dynamic_slice_in_dim(gfull, chip * width, width, axis=-1)
        else:
            grad[n] = gfull
    shapes = [w[n].shape for n in small_names]
    packed = _adamw(_pack([w[n] for n in small_names]), _pack([grad[n] for n in small_names]),
                    _pack([m[n] for n in small_names]), _pack([v[n] for n in small_names]), name="adamw_small")
    for out, res in zip((delta, new_m, new_v), packed):
        for n, a in zip(small_names, _unpack(res, shapes)):
            out[n] = a

    return (loss, grad_x[None], *[grad[n] for n in WEIGHTS], *[delta[n] for n in WEIGHTS],
            *[new_m[n] for n in WEIGHTS], *[new_v[n] for n in WEIGHTS])
```

```python
import functools
import math

import numpy as np
import jax
import jax.numpy as jnp
from jax import lax
from jax.experimental import pallas as pl
from jax.experimental.pallas import tpu as pltpu

F32 = jnp.float32
BF16 = jnp.bfloat16
MESH = pl.DeviceIdType.MESH

CHUNK = 128
HEAD_DIM = 64
DILATED_GROUPS = ((128, 1), (512, 4), (2048, 16))
N_GROUPS = len(DILATED_GROUPS)
BLK = 128
REL_BUCKETS = 32
REL_MAX_DIST = 2048
LN_EPS = 1e-5
NEG = -1e30
ADAM_LR = 0.001
ADAM_B1 = 0.9
ADAM_B2 = 0.999
ADAM_EPS = 1e-08
ADAM_WD = 0.01
ADAM_STEP = 10

N_CHIPS = 4
N_DEV = 8
LANES = 128
SUBLANES = 8
VMEM_LIMIT = 48 * 1024 * 1024

_SQRT_HALF = 0.7071067811865476
_INV_SQRT_2PI = 0.3989422804014327


def _cp(sem=None, vmem=VMEM_LIMIT):
    return pltpu.CompilerParams(dimension_semantics=sem, vmem_limit_bytes=vmem)


def _tile(dim, target, align=LANES):
    if dim <= target:
        return dim
    t = (target // align) * align
    while t >= align:
        if dim % t == 0:
            return t
        t -= align
    return dim


def _gelu(x):
    return 0.5 * x * (1.0 + lax.erf(x * _SQRT_HALF))


def _gelu_grad(x):
    return 0.5 * (1.0 + lax.erf(x * _SQRT_HALF)) + x * (_INV_SQRT_2PI * jnp.exp(-0.5 * x * x))


def _dot(a, b, ca, cb):
    return lax.dot_general(a, b, (((ca,), (cb,)), ((), ())), preferred_element_type=F32)


def _mn_tile(dim):
    return max(_tile(dim, 1024), _tile(dim, 1408))


def _mm(a, b, *, name, ta=False, tb=False, out_dtype=F32, tiles=None,
        a_lead=(), b_lead=(), a_halves=False, b_halves=False, add=None, add_scale=1.0, out_into=None, after=None):
    a2, b2 = a.shape[-2:], b.shape[-2:]
    M, K = (a2[1], a2[0]) if ta else a2
    Kb, N = (b2[1], b2[0]) if tb else b2
    if a_halves:
        K = 2 * K
    if b_halves:
        N = 2 * N
    assert K == Kb, (name, a.shape, b.shape)
    if tiles is None:
        k_target = 1408 if a_halves else (K if K <= 2816 else 2048)
        tm, tn = _mn_tile(M), _mn_tile(N // 2 if b_halves else N)
        tk = _tile(K // 2 if a_halves else K, k_target)
    else:
        tm = _tile(M, tiles[0])
        tn = _tile(N // 2 if b_halves else N, tiles[1])
        tk = _tile(K // 2 if a_halves else K, tiles[2])
    nm, nn, nk = M // tm, N // tn, K // tk
    nkh, nnh = nk // 2, nn // 2

    def lead(idx, rest):
        return tuple(idx) + tuple(rest)

    sq_a = (None,) * (a.ndim - 2)
    sq_b = (None,) * (b.ndim - 2)
    if a_halves:
        assert not ta and a.ndim == 3
        a_spec = pl.BlockSpec((None, tm, tk), lambda i, j, k: (k // nkh, i, k % nkh))
    elif ta:
        a_spec = pl.BlockSpec(sq_a + (tk, tm), lambda i, j, k: lead(a_lead, (k, i)))
    else:
        a_spec = pl.BlockSpec(sq_a + (tm, tk), lambda i, j, k: lead(a_lead, (i, k)))
    if b_halves:
        assert not tb and b.ndim == 3
        b_spec = pl.BlockSpec((None, tk, tn), lambda i, j, k: (j // nnh, k, j % nnh))
    elif tb:
        b_spec = pl.BlockSpec(sq_b + (tn, tk), lambda i, j, k: lead(b_lead, (j, k)))
    else:
        b_spec = pl.BlockSpec(sq_b + (tk, tn), lambda i, j, k: lead(b_lead, (k, j)))
    mn_spec = pl.BlockSpec((tm, tn), lambda i, j, k: (i, j))
    in_specs = [a_spec, b_spec]
    args = [a, b]
    if add is not None:
        in_specs.append(mn_spec)
        args.append(add)
    aliases = {}
    if out_into is None:
        o_spec, out_shape = mn_spec, jax.ShapeDtypeStruct((M, N), out_dtype)
    else:
        buf, n_layers, layer = out_into
        o_spec = pl.BlockSpec((None, tm, tn), lambda i, j, k: (layer, i, j))
        out_shape = jax.ShapeDtypeStruct((n_layers, M, N), out_dtype)
        if buf is not None:
            aliases = {len(args): 0}
            in_specs.append(pl.BlockSpec(memory_space=pl.ANY))
            args.append(buf)
    if after is not None:
        in_specs.append(pl.BlockSpec(memory_space=pl.ANY))
        args.append(after)
    n_in = len(args)
    ca, cb = (0 if ta else 1), (1 if tb else 0)

    def body(*refs):
        a_ref, b_ref = refs[:2]
        add_ref = refs[2] if add is not None else None
        o_ref = refs[n_in]

        def finish(acc):
            if add_ref is not None:
                acc = acc + add_scale * add_ref[...]
            o_ref[...] = acc.astype(o_ref.dtype)

        if nk == 1:
            finish(_dot(a_ref[...], b_ref[...], ca, cb))
        else:
            acc_ref = refs[-1]
            k = pl.program_id(2)

            @pl.when(k == 0)
            def _():
                acc_ref[...] = jnp.zeros_like(acc_ref)

            acc_ref[...] += _dot(a_ref[...], b_ref[...], ca, cb)

            @pl.when(k == nk - 1)
            def _():
                finish(acc_ref[...])

    return pl.pallas_call(
        body, name=name, grid=(nm, nn, nk),
        in_specs=in_specs, out_specs=o_spec, out_shape=out_shape,
        input_output_aliases=aliases,
        scratch_shapes=[] if nk == 1 else [pltpu.VMEM((tm, tn), F32)],
        compiler_params=_cp(("parallel", "parallel", "arbitrary")),
    )(*args)


def _ln_stats(r):
    mu = jnp.mean(r, axis=-1, keepdims=True)
    xc = r - mu
    var = jnp.mean(xc * xc, axis=-1, keepdims=True)
    rstd = lax.rsqrt(var + LN_EPS)
    return xc * rstd, rstd


def _add_ln(x, h, g, b, alpha, *, name):
    T, D = x.shape
    tr = _tile(T, 1024, SUBLANES)
    row = pl.BlockSpec((tr, D), lambda i: (i, 0))
    vec = pl.BlockSpec((1, D), lambda i: (0, 0))

    def body(x_ref, h_ref, g_ref, b_ref, o_ref, ob_ref):
        xhat, _ = _ln_stats(alpha * x_ref[...] + h_ref[...])
        y = xhat * g_ref[...] + b_ref[...]
        o_ref[...] = y
        ob_ref[...] = y.astype(BF16)

    return pl.pallas_call(
        body, name=name, grid=(T // tr,),
        in_specs=[row, row, vec, vec], out_specs=[row, row],
        out_shape=[jax.ShapeDtypeStruct((T, D), F32), jax.ShapeDtypeStruct((T, D), BF16)],
        compiler_params=_cp(("parallel",)),
    )(x, h, g.reshape(1, D), b.reshape(1, D))


def _ln_bwd(dy, x, h, g, alpha, *, name):
    T, D = x.shape
    tr = _tile(T, 512, SUBLANES)
    row = pl.BlockSpec((tr, D), lambda i: (i, 0))
    vec = pl.BlockSpec((1, D), lambda i: (0, 0))

    def body(dy_ref, x_ref, h_ref, g_ref, dr_ref, drb_ref, dg_ref, db_ref):
        @pl.when(pl.program_id(0) == 0)
        def _():
            dg_ref[...] = jnp.zeros_like(dg_ref)
            db_ref[...] = jnp.zeros_like(db_ref)

        xhat, rstd = _ln_stats(alpha * x_ref[...] + h_ref[...])
        dyv = dy_ref[...]
        dxhat = dyv * g_ref[...]
        c1 = jnp.mean(dxhat, axis=-1, keepdims=True)
        c2 = jnp.mean(dxhat * xhat, axis=-1, keepdims=True)
        dr = rstd * (dxhat - c1 - xhat * c2)
        dr_ref[...] = dr
        drb_ref[...] = dr.astype(BF16)
        dg_ref[...] += jnp.sum(dyv * xhat, axis=0, keepdims=True)
        db_ref[...] += jnp.sum(dyv, axis=0, keepdims=True)

    return pl.pallas_call(
        body, name=name, grid=(T // tr,),
        in_specs=[row, row, row, vec], out_specs=[row, row, vec, vec],
        out_shape=[jax.ShapeDtypeStruct((T, D), F32), jax.ShapeDtypeStruct((T, D), BF16),
                   jax.ShapeDtypeStruct((1, D), F32), jax.ShapeDtypeStruct((1, D), F32)],
        compiler_params=_cp(("arbitrary",)),
    )(dy, x, h, g.reshape(1, D))


def _sgu_pieces(zp, lg, lb, E):
    z = _gelu(zp)
    u, v = z[:, :E], z[:, E:]
    vhat, rstd = _ln_stats(v)
    vn = vhat * lg + lb
    return u, vhat, rstd, vn


def _tril_mask():
    t = lax.broadcasted_iota(jnp.int32, (CHUNK, CHUNK), 0)
    s = lax.broadcasted_iota(jnp.int32, (CHUNK, CHUNK), 1)
    return s <= t


def _sgu_fwd(zp, ws, bs, lg, lb, *, name):
    T, E2 = zp.shape
    E = E2 // 2
    G = ws.shape[0]
    cw = E // G

    def body(zp_ref, ws_ref, bs_ref, lg_ref, lb_ref, y_ref):
        u, _, _, vn = _sgu_pieces(zp_ref[...].astype(F32), lg_ref[...], lb_ref[...], E)
        vnb = vn.astype(BF16)
        tril = _tril_mask()
        for g in range(G):
            cols = slice(g * cw, (g + 1) * cw)
            w = jnp.where(tril, ws_ref[g], 0.0).astype(BF16)
            sv = _dot(w, vnb[:, cols], 1, 0) + bs_ref[g]
            y_ref[:, cols] = (u[:, cols] * sv).astype(BF16)

    return pl.pallas_call(
        body, name=name, grid=(T // CHUNK,),
        in_specs=[pl.BlockSpec((CHUNK, E2), lambda n: (n, 0)),
                  pl.BlockSpec((G, CHUNK, CHUNK), lambda n: (0, 0, 0)),
                  pl.BlockSpec((G, CHUNK, 1), lambda n: (0, 0, 0)),
                  pl.BlockSpec((1, E), lambda n: (0, 0)),
                  pl.BlockSpec((1, E), lambda n: (0, 0))],
        out_specs=pl.BlockSpec((CHUNK, E), lambda n: (n, 0)),
        out_shape=jax.ShapeDtypeStruct((T, E), BF16),
        compiler_params=_cp(("parallel",)),
    )(zp, ws, bs.reshape(G, CHUNK, 1), lg.reshape(1, E), lb.reshape(1, E))


def _sgu_bwd(zp, dy, ws, bs, lg, lb, *, name):
    T, E2 = zp.shape
    E = E2 // 2
    G = ws.shape[0]
    cw = E // G
    assert G <= LANES

    def body(zp_ref, dy_ref, ws_ref, bs_ref, lg_ref, lb_ref,
             dzp_ref, dws_ref, dbs_ref, dlg_ref, dlb_ref, dvn_ref):
        @pl.when(pl.program_id(0) == 0)
        def _():
            dws_ref[...] = jnp.zeros_like(dws_ref)
            dbs_ref[...] = jnp.zeros_like(dbs_ref)
            dlg_ref[...] = jnp.zeros_like(dlg_ref)
            dlb_ref[...] = jnp.zeros_like(dlb_ref)

        zpv = zp_ref[...].astype(F32)
        lgv = lg_ref[...]
        u, vhat, rstd, vn = _sgu_pieces(zpv, lgv, lb_ref[...], E)
        gp = _gelu_grad(zpv)
        vnb = vn.astype(BF16)
        tril = _tril_mask()
        lane = lax.broadcasted_iota(jnp.int32, (CHUNK, LANES), 1)
        dbs = dbs_ref[...]
        for g in range(G):
            cols = slice(g * cw, (g + 1) * cw)
            w = jnp.where(tril, ws_ref[g], 0.0).astype(BF16)
            sv = _dot(w, vnb[:, cols], 1, 0) + bs_ref[g]
            dyg = dy_ref[:, cols].astype(F32)
            dzp_ref[:, cols] = (dyg * sv * gp[:, cols]).astype(BF16)
            dsv = dyg * u[:, cols]
            dsvb = dsv.astype(BF16)
            dws_ref[g] += jnp.where(tril, _dot(dsvb, vnb[:, cols], 1, 1), 0.0)
            dvn_ref[:, cols] = _dot(w, dsvb, 0, 0)
            dbs = dbs + jnp.where(lane == g, jnp.sum(dsv, axis=1, keepdims=True), 0.0)
        dbs_ref[...] = dbs
        dvn = dvn_ref[...]
        dlg_ref[...] += jnp.sum(dvn * vhat, axis=0, keepdims=True)
        dlb_ref[...] += jnp.sum(dvn, axis=0, keepdims=True)
        dvhat = dvn * lgv
        c1 = jnp.mean(dvhat, axis=-1, keepdims=True)
        c2 = jnp.mean(dvhat * vhat, axis=-1, keepdims=True)
        dv = rstd * (dvhat - c1 - vhat * c2)
        dzp_ref[:, E:] = (dv * gp[:, E:]).astype(BF16)

    vecE = pl.BlockSpec((1, E), lambda n: (0, 0))
    return pl.pallas_call(
        body, name=name, grid=(T // CHUNK,),
        in_specs=[pl.BlockSpec((CHUNK, E2), lambda n: (n, 0)),
                  pl.BlockSpec((CHUNK, E), lambda n: (n, 0)),
                  pl.BlockSpec((G, CHUNK, CHUNK), lambda n: (0, 0, 0)),
                  pl.BlockSpec((G, CHUNK, 1), lambda n: (0, 0, 0)),
                  vecE, vecE],
        out_specs=[pl.BlockSpec((CHUNK, E2), lambda n: (n, 0)),
                   pl.BlockSpec((G, CHUNK, CHUNK), lambda n: (0, 0, 0)),
                   pl.BlockSpec((CHUNK, LANES), lambda n: (0, 0)),
                   vecE, vecE],
        out_shape=[jax.ShapeDtypeStruct((T, E2), BF16),
                   jax.ShapeDtypeStruct((G, CHUNK, CHUNK), F32),
                   jax.ShapeDtypeStruct((CHUNK, LANES), F32),
                   jax.ShapeDtypeStruct((1, E), F32), jax.ShapeDtypeStruct((1, E), F32)],
        scratch_shapes=[pltpu.VMEM((CHUNK, E), F32)],
        compiler_params=_cp(("arbitrary",)),
    )(zp, dy, ws, bs.reshape(G, CHUNK, 1), lg.reshape(1, E), lb.reshape(1, E))


def _delay(h, k):
    t = lax.broadcasted_iota(jnp.int32, h.shape, 0)
    return jnp.where(t >= k, pltpu.roll(h, k, 0), 0.0)


def _advance(d, k):
    T = d.shape[0]
    t = lax.broadcasted_iota(jnp.int32, d.shape, 0)
    return jnp.where(t < T - k, pltpu.roll(d, T - k, 0), 0.0)


def _conv3(h, w_ref, b_ref):
    return w_ref[2:3, :] * h + w_ref[1:2, :] * _delay(h, 1) + w_ref[0:1, :] * _delay(h, 2) + b_ref[...]


def _convglu_fwd(hup, cw, cb, *, name):
    T, F2 = hup.shape
    F = F2 // 2
    tc = LANES
    nt = F // tc

    def body(ha_ref, hg_ref, wa_ref, wg_ref, ba_ref, bg_ref, o_ref, a_ref, g_ref):
        a = _conv3(ha_ref[...].astype(F32), wa_ref, ba_ref)
        g = _conv3(hg_ref[...].astype(F32), wg_ref, bg_ref)
        o_ref[...] = (_gelu(a) * g).astype(BF16)
        a_ref[...] = a.astype(BF16)
        g_ref[...] = g.astype(BF16)

    col = lambda off: pl.BlockSpec((T, tc), lambda j: (0, j + off))
    w3 = lambda off: pl.BlockSpec((3, tc), lambda j: (0, j + off))
    b1 = lambda off: pl.BlockSpec((1, tc), lambda j: (0, j + off))
    shp = jax.ShapeDtypeStruct((T, F), BF16)
    return pl.pallas_call(
        body, name=name, grid=(nt,),
        in_specs=[col(0), col(nt), w3(0), w3(nt), b1(0), b1(nt)],
        out_specs=[col(0)] * 3, out_shape=[shp] * 3,
        compiler_params=_cp(("parallel",)),
    )(hup, hup, cw, cw, cb.reshape(1, F2), cb.reshape(1, F2))


def _convglu_bwd(hup, a, g, dact, cw, *, name):
    T, F2 = hup.shape
    F = F2 // 2
    tc = LANES
    nt = F // tc

    def half(h, w_ref, d, dh_ref, dw_ref, db_ref, i):
        d1, d2 = _advance(d, 1), _advance(d, 2)
        dh_ref[i] = (w_ref[2:3, :] * d + w_ref[1:2, :] * d1 + w_ref[0:1, :] * d2).astype(BF16)
        c = _dot(h, jnp.concatenate([d2, d1, d], axis=1).astype(BF16), 0, 0)
        eye = lax.broadcasted_iota(jnp.int32, (tc, tc), 0) == lax.broadcasted_iota(jnp.int32, (tc, tc), 1)
        for k in range(3):
            dw_ref[i, k:k + 1, :] = jnp.sum(jnp.where(eye, c[:, k * tc:(k + 1) * tc], 0.0), axis=0, keepdims=True)
        db_ref[i] = jnp.sum(d, axis=0, keepdims=True)

    def body(ha_ref, hg_ref, a_ref, g_ref, d_ref, wa_ref, wg_ref, dh_ref, dw_ref, db_ref):
        a = a_ref[...].astype(F32)
        d = d_ref[...].astype(F32)
        half(ha_ref[...].astype(BF16), wa_ref, d * g_ref[...].astype(F32) * _gelu_grad(a), dh_ref, dw_ref, db_ref, 0)
        half(hg_ref[...].astype(BF16), wg_ref, d * _gelu(a), dh_ref, dw_ref, db_ref, 1)

    col = lambda off: pl.BlockSpec((T, tc), lambda j: (0, j + off))
    w3 = lambda off: pl.BlockSpec((3, tc), lambda j: (0, j + off))
    return pl.pallas_call(
        body, name=name, grid=(nt,),
        in_specs=[col(0), col(nt), col(0), col(0), col(0), w3(0), w3(nt)],
        out_specs=[pl.BlockSpec((2, T, tc), lambda j: (0, 0, j)),
                   pl.BlockSpec((2, 3, tc), lambda j: (0, 0, j)),
                   pl.BlockSpec((2, 1, tc), lambda j: (0, 0, j))],
        out_shape=[jax.ShapeDtypeStruct((2, T, F), BF16),
                   jax.ShapeDtypeStruct((2, 3, F), F32),
                   jax.ShapeDtypeStruct((2, 1, F), F32)],
        compiler_params=_cp(("parallel",)),
    )(hup, hup, a, g, dact, cw, cw)


def _bucket_table():
    iq = np.arange(BLK)[:, None]
    ik = np.arange(2 * BLK)[None, :]
    delta = iq + BLK - ik
    out = []
    for win, dil in DILATED_GROUPS:
        valid = (delta >= 0) & (delta <= win // dil)
        n = (np.clip(delta, 0, None) * dil).astype(np.int32)
        max_exact = REL_BUCKETS // 2
        nf = np.maximum(n, 1).astype(np.float32)
        large = max_exact + (np.log(nf / np.float32(max_exact)) / np.float32(math.log(REL_MAX_DIST / max_exact))
                             * np.float32(REL_BUCKETS - max_exact)).astype(np.int32)
        large = np.minimum(large, REL_BUCKETS - 1)
        out.append(np.where(valid, np.where(n < max_exact, n, large), -1))
    return np.stack(out).astype(np.int32)


def _band_bias(rel_table, H, *, name):
    bkt = jnp.asarray(_bucket_table())

    def body(tab_ref, bk_ref, o_ref):
        g = pl.program_id(0)
        bk = bk_ref[...]
        prev = lax.broadcasted_iota(jnp.int32, bk.shape, 1) < BLK
        for h in range(H):
            def bucket(c, acc):
                return jnp.where(bk == c, tab_ref[c, g * H + h], acc)

            b = lax.fori_loop(0, REL_BUCKETS, bucket, jnp.zeros(bk.shape, F32))
            b = jnp.where(bk >= 0, b, NEG)
            o_ref[h, 0] = b
            o_ref[h, 1] = jnp.where(prev, NEG, b)

    return pl.pallas_call(
        body, name=name, grid=(N_GROUPS,),
        in_specs=[pl.BlockSpec(memory_space=pltpu.SMEM),
                  pl.BlockSpec((None, BLK, 2 * BLK), lambda g: (g, 0, 0))],
        out_specs=pl.BlockSpec((None, H, 2, BLK, 2 * BLK), lambda g: (g, 0, 0, 0, 0)),
        out_shape=jax.ShapeDtypeStruct((N_GROUPS, H, 2, BLK, 2 * BLK), F32),
        compiler_params=_cp(("parallel",)),
    )(rel_table, bkt)


PAIR = 2 * HEAD_DIM


def _bdot(a, b, ca, cb):
    return lax.dot_general(a, b, (((ca,), (cb,)), ((0,), (0,))), preferred_element_type=F32)


def _blocks_per_step(nb, dil):
    return 2 if (nb // dil) % 2 == 0 else 1


def _blk(i):
    return slice(i * BLK, (i + 1) * BLK)


def _heads(ref, i, H, own_columns_only):
    low = lax.broadcasted_iota(jnp.int32, (BLK, PAIR), 1) < HEAD_DIM
    out = []
    for h in range(H):
        a = ref[_blk(i), (h // 2) * PAIR:(h // 2 + 1) * PAIR]
        if own_columns_only:
            a = jnp.where(low if h % 2 == 0 else jnp.logical_not(low), a, jnp.zeros_like(a))
        out.append(a)
    return jnp.stack(out)


def _pair_sums(o):
    return [o[2 * p] + o[2 * p + 1] for p in range(o.shape[0] // 2)]


def _store_pairs(ref, i, o, dtype, scale=None):
    for p, v in enumerate(_pair_sums(o)):
        ref[_blk(i), p * PAIR:(p + 1) * PAIR] = (v if scale is None else scale * v).astype(dtype)


def _stat_tile(s):
    lane = lax.broadcasted_iota(jnp.int32, (BLK, LANES), 1)
    t = jnp.zeros((BLK, LANES), F32)
    for h in range(s.shape[0]):
        t = jnp.where(lane == h, s[h], t)
    return t


def _stat_cols(ref, i, H):
    return jnp.stack([ref[_blk(i), h:h + 1] for h in range(H)])


def _scores(qm, kp, kc, bias):
    scale = HEAD_DIM ** -0.5
    sp = _bdot(qm, kp, 2, 2) * scale + bias[:, :, :BLK]
    sc = _bdot(qm, kc, 2, 2) * scale + bias[:, :, BLK:]
    return sp, sc


def _attn_specs(nb, dil, g, H, HD, sb):
    nbg = nb // dil
    QB = _blocks_per_step(nb, dil)
    cur = lambda col: pl.BlockSpec((QB * BLK, HD), lambda b: (sb(b), col))
    prev = lambda col: pl.BlockSpec((BLK, HD), lambda b: (jnp.maximum(QB * sb(b) - 1, 0), col))
    stat = pl.BlockSpec((QB * BLK, LANES), lambda b: (sb(b), 0))
    bias0 = pl.BlockSpec((None, H, None, BLK, 2 * BLK),
                         lambda b: (g, 0, jnp.where((QB * sb(b)) % nbg == 0, 1, 0), 0, 0))
    bias1 = pl.BlockSpec((None, H, None, BLK, 2 * BLK), lambda b: (g, 0, 0, 0, 0))
    return cur, prev, stat, bias0, bias1


def _attn_fwd(q, k, v, bias, g, dil, *, name):
    (qa, qc), (ka, kc_), (va, vc_) = q, k, v
    T = qa.shape[0]
    H = bias.shape[1]
    HD = H * HEAD_DIM
    assert H % 2 == 0 and H <= LANES and T % (dil * BLK) == 0
    nb = T // BLK
    QB = _blocks_per_step(nb, dil)
    cur, prev, stat, bias0, bias1 = _attn_specs(nb, dil, g, H, HD, lambda b: b)

    def body(q_ref, kp_ref, kc_ref, vp_ref, vc_ref, b0_ref, b1_ref, o_ref, l_ref):
        for i in range(QB):
            kpi, vpi = (kp_ref, vp_ref) if i == 0 else (kc_ref, vc_ref)
            sp, sc = _scores(_heads(q_ref, i, H, True), _heads(kpi, max(i - 1, 0), H, False),
                             _heads(kc_ref, i, H, False), (b0_ref if i == 0 else b1_ref)[...])
            m = jnp.maximum(jnp.max(sp, axis=-1, keepdims=True), jnp.max(sc, axis=-1, keepdims=True))
            pp, pc = jnp.exp(sp - m), jnp.exp(sc - m)
            den = jnp.sum(pp, axis=-1, keepdims=True) + jnp.sum(pc, axis=-1, keepdims=True)
            o = (_bdot((pp / den).astype(BF16), _heads(vpi, max(i - 1, 0), H, True), 2, 1)
                 + _bdot((pc / den).astype(BF16), _heads(vc_ref, i, H, True), 2, 1))
            _store_pairs(o_ref, i, o, F32)
            l_ref[_blk(i), :] = _stat_tile(m + jnp.log(den))

    return pl.pallas_call(
        body, name=name, grid=(nb // QB,),
        in_specs=[cur(qc), prev(kc_), cur(kc_), prev(vc_), cur(vc_), bias0, bias1],
        out_specs=[cur(0), stat],
        out_shape=[jax.ShapeDtypeStruct((T, HD), F32), jax.ShapeDtypeStruct((T, LANES), F32)],
        compiler_params=_cp(("parallel",)),
    )(qa, ka, ka, va, va, bias, bias)


def _grouped_spec(tr, d, C):
    return pl.BlockSpec((d, tr // d, C), lambda i: (0, i, 0))


def _as_grouped(a, d):
    T, C = a.shape
    return a.reshape(d, T // d, C)


def _to_token_slabs(ref, slabs, d):
    n_r = ref.shape[1]
    for j in range(ref.shape[2] // LANES):
        for r in range(d):
            slabs.at[j][pl.ds(r, n_r, stride=d), :] = ref[r, :, j * LANES:(j + 1) * LANES].astype(F32)


def _from_token_slabs(slabs, ref, d):
    n_r = ref.shape[1]
    for j in range(ref.shape[2] // LANES):
        for r in range(d):
            ref[r, :, j * LANES:(j + 1) * LANES] = slabs.at[j][pl.ds(r, n_r, stride=d), :].astype(ref.dtype)


def _regroup_to_tokens(parts, out_dtype, *, name):
    T, C = parts[0][0].shape
    n = len(parts)
    tr = _tile(T, 512, SUBLANES)
    n_sl = C // LANES

    def body(*refs):
        o_ref, scr = refs[n], refs[n + 1]
        for g, (_, d) in enumerate(parts):
            _to_token_slabs(refs[g], scr, d)
            for j in range(n_sl):
                o_ref[:, g * C + j * LANES:g * C + (j + 1) * LANES] = scr[j].astype(out_dtype)

    return pl.pallas_call(
        body, name=name, grid=(T // tr,),
        in_specs=[_grouped_spec(tr, d, C) for _, d in parts],
        out_specs=pl.BlockSpec((tr, n * C), lambda i: (i, 0)),
        out_shape=jax.ShapeDtypeStruct((T, n * C), out_dtype),
        scratch_shapes=[pltpu.VMEM((n_sl, tr, LANES), F32)],
        compiler_params=_cp(("parallel",)),
    )(*[_as_grouped(a, d) for a, d in parts])


def _attn_combine(os, ls, dils, *, name):
    n = len(os)
    T, HD = os[0].shape
    H = HD // HEAD_DIM
    tr = _tile(T, 512, SUBLANES)
    n_sl = HD // LANES
    assert PAIR == LANES

    def body(*refs):
        o_refs, l_refs = refs[:n], refs[n:2 * n]
        oc_ref, ocb_ref = refs[2 * n:2 * n + 2]
        lt_refs = refs[2 * n + 2:3 * n + 2]
        o_scr, l_scr = refs[3 * n + 2], refs[3 * n + 3]
        for g in range(n):
            _to_token_slabs(o_refs[g], o_scr.at[g], dils[g])
            _to_token_slabs(l_refs[g], l_scr.at[g], dils[g])
        ls_ = [l_scr[g, 0] for g in range(n)]
        m = ls_[0]
        for l in ls_[1:]:
            m = jnp.maximum(m, l)
        ws = [jnp.exp(l - m) for l in ls_]
        den = ws[0]
        for w in ws[1:]:
            den = den + w
        l_scr[n, 0] = m + jnp.log(den)
        for g in range(n):
            _from_token_slabs(l_scr.at[n], lt_refs[g], dils[g])
        ws = [w / den for w in ws]
        low = lax.broadcasted_iota(jnp.int32, (tr, PAIR), 1) < HEAD_DIM
        for p in range(H // 2):
            acc = jnp.zeros((tr, PAIR), F32)
            for g in range(n):
                w = jnp.where(low, ws[g][:, 2 * p:2 * p + 1], ws[g][:, 2 * p + 1:2 * p + 2])
                acc = acc + w * o_scr[g, p]
            oc_ref[:, p * PAIR:(p + 1) * PAIR] = acc
            ocb_ref[:, p * PAIR:(p + 1) * PAIR] = acc.astype(BF16)

    wide = pl.BlockSpec((tr, HD), lambda i: (i, 0))
    lts = pl.pallas_call(
        body, name=name, grid=(T // tr,),
        in_specs=[_grouped_spec(tr, d, HD) for d in dils] + [_grouped_spec(tr, d, LANES) for d in dils],
        out_specs=[wide, wide] + [_grouped_spec(tr, d, LANES) for d in dils],
        out_shape=[jax.ShapeDtypeStruct((T, HD), F32), jax.ShapeDtypeStruct((T, HD), BF16)]
        + [jax.ShapeDtypeStruct((d, T // d, LANES), F32) for d in dils],
        scratch_shapes=[pltpu.VMEM((n, n_sl, tr, LANES), F32), pltpu.VMEM((n + 1, 1, tr, LANES), F32)],
        compiler_params=_cp(("parallel",)),
    )(*[_as_grouped(o, d) for o, d in zip(os, dils)], *[_as_grouped(l, d) for l, d in zip(ls, dils)])
    return lts[0], lts[1], [l.reshape(T, LANES) for l in lts[2:]]


def _attn_bwd_prep(do, oc, dils, *, name):
    T, HD = do.shape
    H = HD // HEAD_DIM
    n = len(dils)
    tr = _tile(T, 512, SUBLANES)
    n_sl = HD // LANES
    wide = pl.BlockSpec((tr, HD), lambda i: (i, 0))

    def body(*refs):
        do_ref, oc_ref = refs[:2]
        dl_refs, dob_refs = refs[2:2 + n], refs[2 + n:2 + 2 * n]
        do_scr, dl_scr = refs[2 + 2 * n], refs[3 + 2 * n]
        low = lax.broadcasted_iota(jnp.int32, (tr, PAIR), 1) < HEAD_DIM
        lane = lax.broadcasted_iota(jnp.int32, (tr, LANES), 1)
        dl = jnp.zeros((tr, LANES), F32)
        for p in range(H // 2):
            d = do_ref[:, p * PAIR:(p + 1) * PAIR]
            do_scr[p] = d
            prod = d * oc_ref[:, p * PAIR:(p + 1) * PAIR]
            dl = jnp.where(lane == 2 * p, jnp.sum(jnp.where(low, prod, 0.0), axis=-1, keepdims=True), dl)
            dl = jnp.where(lane == 2 * p + 1, jnp.sum(jnp.where(low, 0.0, prod), axis=-1, keepdims=True), dl)
        dl_scr[0] = dl
        for g in range(n):
            _from_token_slabs(dl_scr, dl_refs[g], dils[g])
            _from_token_slabs(do_scr, dob_refs[g], dils[g])

    outs = pl.pallas_call(
        body, name=name, grid=(T // tr,),
        in_specs=[wide, wide],
        out_specs=[_grouped_spec(tr, d, LANES) for d in dils] + [_grouped_spec(tr, d, HD) for d in dils],
        out_shape=[jax.ShapeDtypeStruct((d, T // d, LANES), F32) for d in dils]
        + [jax.ShapeDtypeStruct((d, T // d, HD), BF16) for d in dils],
        scratch_shapes=[pltpu.VMEM((n_sl, tr, LANES), F32), pltpu.VMEM((1, tr, LANES), F32)],
        compiler_params=_cp(("parallel",)),
    )(do, oc)
    return [a.reshape(T, LANES) for a in outs[:n]], [a.reshape(T, HD) for a in outs[n:]]


def _attn_bwd(q, k, v, bias, g, dil, do, lse, delta, *, name):
    (qa, qc), (ka, kc_), (va, vc_) = q, k, v
    T, HD = do.shape
    H = HD // HEAD_DIM
    nb = T // BLK
    scale = HEAD_DIM ** -0.5
    QB = _blocks_per_step(nb, dil)
    ns = nb // QB
    cur, prev, stat, bias0, bias1 = _attn_specs(nb, dil, g, H, HD, lambda b: jnp.minimum(b, ns - 1))
    late = pl.BlockSpec((QB * BLK, HD), lambda b: (jnp.maximum(b - 1, 0), 0))

    def body(q_ref, kp_ref, kc_ref, vp_ref, vc_ref, b0_ref, b1_ref, do_ref, l_ref, dl_ref,
             dq_ref, dk_ref, dv_ref, db_ref, ck_ref, cv_ref):
        b = pl.program_id(0)

        @pl.when(b == 0)
        def _():
            db_ref[...] = jnp.zeros_like(db_ref)
            ck_ref[...] = jnp.zeros_like(ck_ref)
            cv_ref[...] = jnp.zeros_like(cv_ref)

        @pl.when(b < ns)
        def _():
            dk_ref[...] = ck_ref[...]
            dv_ref[...] = cv_ref[...]
            for i in range(QB):
                kpi, vpi, ip = (kp_ref, vp_ref, 0) if i == 0 else (kc_ref, vc_ref, i - 1)
                qm, dom = _heads(q_ref, i, H, True), _heads(do_ref, i, H, True)
                sp, sc = _scores(qm, _heads(kpi, ip, H, False), _heads(kc_ref, i, H, False),
                                 (b0_ref if i == 0 else b1_ref)[...])
                l, dl = _stat_cols(l_ref, i, H), _stat_cols(dl_ref, i, H)
                pp, pc = jnp.exp(sp - l), jnp.exp(sc - l)
                dsp = pp * (_bdot(dom, _heads(vpi, ip, H, False), 2, 2) - dl)
                dsc = pc * (_bdot(dom, _heads(vc_ref, i, H, False), 2, 2) - dl)
                db_ref[:, :, :BLK] += dsp
                db_ref[:, :, BLK:] += dsc
                dspb, dscb = dsp.astype(BF16), dsc.astype(BF16)
                _store_pairs(dq_ref, i, _bdot(dspb, _heads(kpi, ip, H, True), 2, 1)
                             + _bdot(dscb, _heads(kc_ref, i, H, True), 2, 1), F32, scale)
                dkp = _pair_sums(_bdot(dspb, qm, 1, 1))
                dkc = _pair_sums(_bdot(dscb, qm, 1, 1))
                dvp = _pair_sums(_bdot(pp.astype(BF16), dom, 1, 1))
                dvc = _pair_sums(_bdot(pc.astype(BF16), dom, 1, 1))
                for p in range(H // 2):
                    cols = slice(p * PAIR, (p + 1) * PAIR)
                    if i == 0:
                        dk_ref[_blk(QB - 1), cols] += scale * dkp[p]
                        dv_ref[_blk(QB - 1), cols] += dvp[p]
                    else:
                        ck_ref[_blk(i - 1), cols] += scale * dkp[p]
                        cv_ref[_blk(i - 1), cols] += dvp[p]
                    ck_ref[_blk(i), cols] = scale * dkc[p]
                    cv_ref[_blk(i), cols] = dvc[p]

        @pl.when(b == ns)
        def _():
            dk_ref[...] = ck_ref[...]
            dv_ref[...] = cv_ref[...]

    f32 = jax.ShapeDtypeStruct((T, HD), F32)
    return pl.pallas_call(
        body, name=name, grid=(ns + 1,),
        in_specs=[cur(qc), prev(kc_), cur(kc_), prev(vc_), cur(vc_), bias0, bias1, cur(0), stat, stat],
        out_specs=[cur(0), late, late, pl.BlockSpec((H, BLK, 2 * BLK), lambda b: (0, 0, 0))],
        out_shape=[f32, f32, f32,
                   jax.ShapeDtypeStruct((H, BLK, 2 * BLK), F32)],
        scratch_shapes=[pltpu.VMEM((QB * BLK, HD), F32), pltpu.VMEM((QB * BLK, HD), F32)],
        compiler_params=_cp(("arbitrary",)),
    )(qa, ka, ka, va, va, bias, bias, do, lse, delta)


def _rel_grad(dbs, *, name):
    H = dbs[0].shape[1]
    n = len(dbs)
    assert H <= LANES
    bkt = jnp.asarray(_bucket_table())

    def body(*refs):
        bk = refs[n][...]
        o_ref = refs[n + 1]
        db = refs[0][...]
        for r in refs[1:n]:
            db = db + r[...]
        row = lax.broadcasted_iota(jnp.int32, (REL_BUCKETS, LANES), 0)
        col = lax.broadcasted_iota(jnp.int32, (REL_BUCKETS, LANES), 1)

        def bucket(c, acc):
            sel = bk == c
            for h in range(H):
                val = jnp.sum(jnp.where(sel, db[h], 0.0))
                acc = jnp.where((row == c) & (col == h), val, acc)
            return acc

        o_ref[...] = lax.fori_loop(0, REL_BUCKETS, bucket, jnp.zeros((REL_BUCKETS, LANES), F32))

    dspec = pl.BlockSpec((None, H, BLK, 2 * BLK), lambda g: (g, 0, 0, 0))
    return pl.pallas_call(
        body, name=name, grid=(N_GROUPS,),
        in_specs=[dspec] * n + [pl.BlockSpec((None, BLK, 2 * BLK), lambda g: (g, 0, 0))],
        out_specs=pl.BlockSpec((None, REL_BUCKETS, LANES), lambda g: (g, 0, 0)),
        out_shape=jax.ShapeDtypeStruct((N_GROUPS, REL_BUCKETS, LANES), F32),
        compiler_params=_cp(("parallel",)),
    )(*dbs, bkt)


def _to_group(a, d):
    T, C = a.shape
    return a if d == 1 else a.reshape(T // d, d, C).transpose(1, 0, 2).reshape(T, C)


def _from_group(a, d):
    T, C = a.shape
    return a if d == 1 else a.reshape(d, T // d, C).transpose(1, 0, 2).reshape(T, C)


def _rows_view(a):
    a2 = a.reshape(-1, a.shape[-1])
    R, C = a2.shape
    tr = _tile(R, max(SUBLANES, (1 << 19) // C), SUBLANES)
    return a2, R, C, tr


def _addn(xs, out_dtype, *, name):
    shape = xs[0].shape
    x2s = [_rows_view(x)[0] for x in xs]
    _, R, C, tr = _rows_view(xs[0])
    spec = pl.BlockSpec((tr, C), lambda i: (i, 0))

    def body(*refs):
        acc = refs[0][...].astype(F32)
        for r in refs[1:-1]:
            acc = acc + r[...].astype(F32)
        refs[-1][...] = acc.astype(out_dtype)

    out = pl.pallas_call(
        body, name=name, grid=(R // tr,),
        in_specs=[spec] * len(xs), out_specs=spec,
        out_shape=jax.ShapeDtypeStruct((R, C), out_dtype),
        compiler_params=_cp(("parallel",)),
    )(*x2s)
    return out.reshape(shape)


def _addn_grouped(parts, out_dtype, *, name):
    T, C = parts[0][0].shape
    tr = _tile(T, 512, SUBLANES)
    assert all(tr % d == 0 for _, d in parts)

    assert C % LANES == 0

    def body(*refs):
        o_ref, acc_ref = refs[-2], refs[-1]
        acc_ref[...] = jnp.zeros_like(acc_ref)
        for ref, (_, d) in zip(refs, parts):
            for r in range(d):
                for j in range(C // LANES):
                    acc_ref.at[j][pl.ds(r, tr // d, stride=d), :] += ref[r, :, j * LANES:(j + 1) * LANES]
        for j in range(C // LANES):
            o_ref[:, j * LANES:(j + 1) * LANES] = acc_ref[j].astype(out_dtype)

    return pl.pallas_call(
        body, name=name, grid=(T // tr,),
        in_specs=[pl.BlockSpec((d, tr // d, C), lambda i: (0, i, 0)) for _, d in parts],
        out_specs=pl.BlockSpec((tr, C), lambda i: (i, 0)),
        out_shape=jax.ShapeDtypeStruct((T, C), out_dtype),
        scratch_shapes=[pltpu.VMEM((C // LANES, tr, LANES), F32)],
        compiler_params=_cp(("parallel",)),
    )(*[a.reshape(d, T // d, C) for a, d in parts])


def _slot_sum(b_ref, own, my_id):
    acc = None
    for s in range(b_ref.shape[0]):
        term = jnp.where(my_id == s, own, b_ref[s].astype(F32))
        acc = term if acc is None else acc + term
    return acc


def _part_geometry(shape, axis):
    _, R, C = shape
    if axis == COL:
        return R // 2, C // N_CHIPS
    return R // (2 * N_CHIPS), C


def _sum_parts(buf, grad, axis, where, *, name):
    n, L, Rp, Cp = buf.shape
    assert (Rp, Cp) == _part_geometry(grad.shape, axis)
    tr = _tile(Rp, max(2 * SUBLANES, (1 << 18) // Cp), 2 * SUBLANES)
    nr = Rp // tr

    def own_map(l, i, w):
        if axis == COL:
            return (l, w[1] * nr + i, w[0])
        return (l, (2 * w[0] + w[1]) * nr + i, 0)

    def body(w_ref, b_ref, g_ref, o_ref):
        o_ref[...] = _slot_sum(b_ref, g_ref[...].astype(F32), w_ref[2])

    return pl.pallas_call(
        body, name=name,
        grid_spec=pltpu.PrefetchScalarGridSpec(
            num_scalar_prefetch=1, grid=(L, nr),
            in_specs=[pl.BlockSpec((n, None, tr, Cp), lambda l, i, w: (0, l, i, 0)),
                      pl.BlockSpec((None, tr, Cp), own_map)],
            out_specs=pl.BlockSpec((None, None, tr, Cp), lambda l, i, w: (l, w[1], i, 0))),
        out_shape=jax.ShapeDtypeStruct((L, 2, Rp, Cp), F32),
        compiler_params=_cp(("parallel", "parallel")),
    )(where, buf, grad)


def _sum_small(buf, own, where, *, name):
    n, R, C = buf.shape
    tr = _tile(R, 512, SUBLANES)

    def body(w_ref, b_ref, g_ref, o_ref):
        o_ref[...] = _slot_sum(b_ref, g_ref[...], w_ref[2])

    return pl.pallas_call(
        body, name=name,
        grid_spec=pltpu.PrefetchScalarGridSpec(
            num_scalar_prefetch=1, grid=(R // tr,),
            in_specs=[pl.BlockSpec((n, tr, C), lambda i, w: (0, i, 0)),
                      pl.BlockSpec((tr, C), lambda i, w: (i, 0))],
            out_specs=pl.BlockSpec((tr, C), lambda i, w: (i, 0))),
        out_shape=jax.ShapeDtypeStruct((R, C), F32),
        compiler_params=_cp(("parallel",)),
    )(where, buf, own)


def _place_shard(shard, layer, axis, where, out_dtype, *, name, after=None):
    _, R, C = shard.shape
    tr = _tile(R, max(2 * SUBLANES, (1 << 19) // C), 2 * SUBLANES)
    nr = R // tr
    if axis == COL:
        full, out_map = (1, R, C * N_CHIPS), (lambda i, w: (0, i, w[0]))
    else:
        full, out_map = (1, R * N_CHIPS, C), (lambda i, w: (0, w[0] * nr + i, 0))

    def body(w_ref, x_ref, *rest):
        rest[-1][...] = x_ref[...].astype(out_dtype)

    extra = [] if after is None else [after]
    return pl.pallas_call(
        body, name=name,
        grid_spec=pltpu.PrefetchScalarGridSpec(
            num_scalar_prefetch=1, grid=(nr,),
            in_specs=[pl.BlockSpec((None, tr, C), lambda i, w: (layer, i, 0))]
            + [pl.BlockSpec(memory_space=pl.ANY)] * len(extra),
            out_specs=pl.BlockSpec((None, tr, C), out_map)),
        out_shape=jax.ShapeDtypeStruct(full, out_dtype),
        compiler_params=_cp(("parallel",)),
    )(where, shard, *extra)


def _loss_grad(y, tgt, *, name):
    T, D = y.shape
    tr = _tile(T, 512, SUBLANES)
    row = pl.BlockSpec((tr, D), lambda i: (i, 0))

    def body(y_ref, t_ref, dy_ref, l_ref, acc_ref):
        i = pl.program_id(0)

        @pl.when(i == 0)
        def _():
            acc_ref[...] = jnp.zeros_like(acc_ref)

        err = y_ref[...] - t_ref[...]
        dy_ref[...] = err * (1.0 / D)
        acc_ref[...] += jnp.sum(err * err, axis=0, keepdims=True)

        @pl.when(i == T // tr - 1)
        def _():
            tot = jnp.sum(acc_ref[...], axis=1, keepdims=True) * (0.5 / D)
            l_ref[...] = jnp.broadcast_to(tot, (1, LANES))

    return pl.pallas_call(
        body, name=name, grid=(T // tr,),
        in_specs=[row, row],
        out_specs=[row, pl.BlockSpec((1, LANES), lambda i: (0, 0))],
        out_shape=[jax.ShapeDtypeStruct((T, D), F32), jax.ShapeDtypeStruct((1, LANES), F32)],
        scratch_shapes=[pltpu.VMEM((1, D), F32)],
        compiler_params=_cp(("arbitrary",)),
    )(y, tgt)


def _adamw(w, g, m, v, *, name):
    shape = w.shape
    w2, R, C, tr = _rows_view(w)
    spec = pl.BlockSpec((tr, C), lambda i: (i, 0))

    def body(w_ref, g_ref, m_ref, v_ref, d_ref, nm_ref, nv_ref):
        gv = g_ref[...]
        nm = ADAM_B1 * m_ref[...] + (1.0 - ADAM_B1) * gv
        nv = ADAM_B2 * v_ref[...] + (1.0 - ADAM_B2) * (gv * gv)
        m_hat = nm / (1.0 - ADAM_B1 ** ADAM_STEP)
        v_hat = nv / (1.0 - ADAM_B2 ** ADAM_STEP)
        d_ref[...] = -ADAM_LR * (m_hat / (jnp.sqrt(v_hat) + ADAM_EPS) + ADAM_WD * w_ref[...])
        nm_ref[...] = nm
        nv_ref[...] = nv

    shp = jax.ShapeDtypeStruct((R, C), F32)
    outs = pl.pallas_call(
        body, name=name, grid=(R // tr,),
        in_specs=[spec] * 4, out_specs=[spec] * 3, out_shape=[shp] * 3,
        compiler_params=_cp(("parallel",)),
    )(w2, g.reshape(R, C), m.reshape(R, C), v.reshape(R, C))
    return tuple(o.reshape(shape) for o in outs)


def _pack(arrs):
    flat = jnp.concatenate([a.reshape(-1).astype(F32) for a in arrs])
    unit = SUBLANES * LANES
    pad = (-flat.shape[0]) % unit
    return jnp.pad(flat, (0, pad)).reshape(-1, LANES)


def _unpack(buf, shapes):
    flat = buf.reshape(-1)
    out, off = [], 0
    for s in shapes:
        n = int(np.prod(s))
        out.append(flat[off:off + n].reshape(s))
        off += n
    return out


def _me():
    return lax.axis_index("x"), lax.axis_index("y"), lax.axis_index("c")


def _flip(pos, k):
    x, y, c = pos
    return (1 - x if k & 4 else x, 1 - y if k & 2 else y, 1 - c if k & 1 else c)


HBM = pl.BlockSpec(memory_space=pltpu.HBM)

COL, ROW = -1, -2


def _shard_window(ref, axis, j, n):
    start = pl.multiple_of(j * n, n)
    if axis == COL:
        return ref.at[:, :, pl.ds(start, n)]
    return ref.at[:, pl.ds(start, n), :]


GATHER_PEERS = (2, 4, 6)
SEM = pl.BlockSpec(memory_space=pltpu.SEMAPHORE)
EFFECT = pltpu.SideEffectType.DATAFLOW_SIDE_EFFECTING


def _gather_copy(buf, axis, i, pi, chip, me, send_sems, recv_sems):
    win = _shard_window(buf, axis, chip, buf.shape[axis] // N_CHIPS)
    k = len(GATHER_PEERS) * i + pi
    return pltpu.make_async_remote_copy(
        src_ref=win, dst_ref=win, send_sem=send_sems.at[k], recv_sem=recv_sems.at[k],
        device_id=_flip(me, GATHER_PEERS[pi]), device_id_type=MESH)


def _gather_start(fulls, axes, *, name):
    n = len(fulls)
    n_sem = len(GATHER_PEERS) * n

    def body(*refs):
        send_sems, recv_sems = refs[n], refs[n + 1]
        bufs = refs[n + 2:2 * n + 2]
        token = refs[2 * n + 2]
        me = _me()
        for i in range(n):
            for pi in range(len(GATHER_PEERS)):
                _gather_copy(bufs[i], axes[i], i, pi, 2 * me[0] + me[1], me, send_sems, recv_sems).start()
        token[...] = jnp.zeros_like(token)

    outs = pl.pallas_call(
        body, name=name,
        in_specs=[HBM] * n, out_specs=[SEM, SEM] + [HBM] * n + [pl.BlockSpec(memory_space=pltpu.VMEM)],
        out_shape=[pltpu.SemaphoreType.DMA((n_sem,)), pltpu.SemaphoreType.DMA((n_sem,))]
        + [pltpu.HBM(f.shape, f.dtype) for f in fulls] + [jax.ShapeDtypeStruct((SUBLANES, LANES), F32)],
        input_output_aliases={i: 2 + i for i in range(n)},
        compiler_params=pltpu.CompilerParams(has_side_effects=EFFECT),
    )(*[pltpu.with_memory_space_constraint(f, pltpu.HBM) for f in fulls])
    return outs[0], outs[1], list(outs[2:2 + n]), outs[-1]


def _gather_wait(send_sems, recv_sems, bufs, axes, idxs, after, *, name):
    m = len(bufs)

    def body(*refs):
        ss, rs = refs[m], refs[m + 1]
        outs = refs[m + 3:]
        me = _me()
        for t, i in enumerate(idxs):
            for pi, k in enumerate(GATHER_PEERS):
                px, py, _ = _flip(me, k)
                _gather_copy(outs[t], axes[t], i, pi, 2 * me[0] + me[1], me, ss, rs).wait_send()
                _gather_copy(outs[t], axes[t], i, pi, 2 * px + py, me, ss, rs).wait_recv()

    return pl.pallas_call(
        body, name=name,
        in_specs=[HBM] * m + [SEM, SEM, pl.BlockSpec(memory_space=pl.ANY)], out_specs=[HBM] * m,
        out_shape=[pltpu.HBM(b.shape, b.dtype) for b in bufs],
        input_output_aliases={t: t for t in range(m)},
        compiler_params=pltpu.CompilerParams(has_side_effects=EFFECT),
    )(*bufs, send_sems, recv_sems, after)


def _grad_part(ref, layer, axis, pos):
    px, py, pc = pos
    Rp, Cp = _part_geometry(ref.shape, axis)
    chip = 2 * px + py
    if axis == COL:
        return ref.at[layer, pl.ds(pl.multiple_of(pc * Rp, Rp), Rp), pl.ds(pl.multiple_of(chip * Cp, Cp), Cp)]
    return ref.at[layer, pl.ds(pl.multiple_of((2 * chip + pc) * Rp, Rp), Rp), :]


def _scatter_copy(grad, slots, layer, axis, i, k, src_pos, dst_slot, me, send_sems, recv_sems):
    sem = (N_DEV - 1) * i + k - 1
    return pltpu.make_async_remote_copy(
        src_ref=_grad_part(grad, layer, axis, src_pos), dst_ref=slots.at[dst_slot, layer],
        send_sem=send_sems.at[sem], recv_sem=recv_sems.at[sem],
        device_id=_flip(me, k), device_id_type=MESH)


def _scatter_start(grads, slots, layers, axes, *, name):
    n = len(grads)
    n_sem = (N_DEV - 1) * n

    def body(*refs):
        send_sems, recv_sems = refs[2 * n], refs[2 * n + 1]
        g_refs, s_refs = refs[2 * n + 2:3 * n + 2], refs[3 * n + 2:4 * n + 2]
        token = refs[4 * n + 2]
        me = _me()
        my_id = 4 * me[0] + 2 * me[1] + me[2]
        for i in range(n):
            for k in range(1, N_DEV):
                _scatter_copy(g_refs[i], s_refs[i], layers[i], axes[i], i, k, _flip(me, k), my_id, me,
                              send_sems, recv_sems).start()
        token[...] = jnp.zeros_like(token)

    arrays = list(grads) + list(slots)
    outs = pl.pallas_call(
        body, name=name,
        in_specs=[HBM] * (2 * n),
        out_specs=[SEM, SEM] + [HBM] * (2 * n) + [pl.BlockSpec(memory_space=pltpu.VMEM)],
        out_shape=[pltpu.SemaphoreType.DMA((n_sem,)), pltpu.SemaphoreType.DMA((n_sem,))]
        + [pltpu.HBM(a.shape, a.dtype) for a in arrays] + [jax.ShapeDtypeStruct((SUBLANES, LANES), F32)],
        input_output_aliases={i: 2 + i for i in range(2 * n)},
        compiler_params=pltpu.CompilerParams(has_side_effects=EFFECT),
    )(*[pltpu.with_memory_space_constraint(a, pltpu.HBM) for a in arrays])
    return outs[0], outs[1], list(outs[2:2 + n]), list(outs[2 + n:2 + 2 * n]), outs[-1]


def _scatter_wait(sems, grads, slots, axes, plan, after, *, name):
    n = len(grads)
    flat_sems = [s for pair in sems for s in pair]

    def body(*refs):
        sem_refs = refs[2 * n:2 * n + len(flat_sems)]
        first_out = 2 * n + len(flat_sems) + 1
        g_refs, s_refs = refs[first_out:first_out + n], refs[first_out + n:]
        me = _me()
        for j, items in enumerate(plan):
            ss, rs = sem_refs[2 * j], sem_refs[2 * j + 1]
            for i, (a, layer) in enumerate(items):
                for k in range(1, N_DEV):
                    px, py, pc = _flip(me, k)
                    cp = _scatter_copy(g_refs[a], s_refs[a], layer, axes[a], i, k, me, 4 * px + 2 * py + pc, me, ss, rs)
                    cp.wait_send()
                    cp.wait_recv()

    arrays = list(grads) + list(slots)
    outs = pl.pallas_call(
        body, name=name,
        in_specs=[HBM] * (2 * n) + [SEM] * len(flat_sems) + [pl.BlockSpec(memory_space=pl.ANY)],
        out_specs=[HBM] * (2 * n),
        out_shape=[pltpu.HBM(a.shape, a.dtype) for a in arrays],
        input_output_aliases={i: i for i in range(2 * n)},
        compiler_params=pltpu.CompilerParams(has_side_effects=EFFECT),
    )(*arrays, *flat_sems, after)
    return list(outs[:n]), list(outs[n:])


def _small_copy(src, dst, k, slot, me, send_sems, recv_sems):
    return pltpu.make_async_remote_copy(
        src_ref=src, dst_ref=dst.at[slot], send_sem=send_sems.at[k - 1], recv_sem=recv_sems.at[k - 1],
        device_id=_flip(me, k), device_id_type=MESH)


def _small_start(small, slots, after, *, name):
    def body(src_in, dst_in, after_ref, send_sems, recv_sems, src, dst):
        me = _me()
        for k in range(1, N_DEV):
            _small_copy(src, dst, k, 4 * me[0] + 2 * me[1] + me[2], me, send_sems, recv_sems).start()

    arrays = [small, slots]
    return pl.pallas_call(
        body, name=name, in_specs=[HBM, HBM, pl.BlockSpec(memory_space=pl.ANY)], out_specs=[SEM, SEM, HBM, HBM],
        out_shape=[pltpu.SemaphoreType.DMA((N_DEV - 1,)), pltpu.SemaphoreType.DMA((N_DEV - 1,))]
        + [pltpu.HBM(a.shape, a.dtype) for a in arrays],
        input_output_aliases={0: 2, 1: 3},
        compiler_params=pltpu.CompilerParams(has_side_effects=EFFECT),
    )(*[pltpu.with_memory_space_constraint(a, pltpu.HBM) for a in arrays], after)


def _small_wait(send_sems, recv_sems, small, slots, after, *, name):
    def body(src_in, dst_in, ss, rs, after_ref, src, dst):
        me = _me()
        for k in range(1, N_DEV):
            px, py, pc = _flip(me, k)
            cp = _small_copy(src, dst, k, 4 * px + 2 * py + pc, me, ss, rs)
            cp.wait_send()
            cp.wait_recv()

    return pl.pallas_call(
        body, name=name, in_specs=[HBM, HBM, SEM, SEM, pl.BlockSpec(memory_space=pl.ANY)], out_specs=[HBM, HBM],
        out_shape=[pltpu.HBM(small.shape, small.dtype), pltpu.HBM(slots.shape, slots.dtype)],
        input_output_aliases={0: 0, 1: 1},
        compiler_params=pltpu.CompilerParams(has_side_effects=EFFECT),
    )(small, slots, send_sems, recv_sems, after)


def _pair_gather(halves, *, name):
    n = len(halves)

    def body(*refs):
        bufs = refs[n:2 * n]
        send_sems, recv_sems = refs[2 * n:]
        me = _me()
        c = me[2]
        sib = _flip(me, 1)

        def copy(i, half):
            return pltpu.make_async_remote_copy(
                src_ref=bufs[i].at[:, half], dst_ref=bufs[i].at[:, half], send_sem=send_sems.at[i],
                recv_sem=recv_sems.at[i], device_id=sib, device_id_type=MESH)

        sends = [copy(i, c) for i in range(n)]
        for cp in sends:
            cp.start()
        for i in range(n):
            copy(i, 1 - c).wait_recv()
        for cp in sends:
            cp.wait_send()

    return pl.pallas_call(
        body, name=name,
        in_specs=[HBM] * n, out_specs=[HBM] * n,
        out_shape=[jax.ShapeDtypeStruct(h.shape, h.dtype) for h in halves],
        input_output_aliases={i: i for i in range(n)},
        scratch_shapes=[pltpu.SemaphoreType.DMA((n,)), pltpu.SemaphoreType.DMA((n,))],
    )(*halves)


def _local_step(x, tgt, P, fetch, ship, n_layers):
    T, D = x.shape
    P = dict(P)
    W = {}
    depth = P["ffn_conv_b"].shape[0]
    n_a = P["a_w_s"].shape[0]
    alpha = (2 * depth) ** 0.25
    H = P["rel_table"].shape[1] // N_GROUPS
    HD = H * HEAD_DIM
    dils = [d for _, d in DILATED_GROUPS]
    assert T % (dils[-1] * BLK) == 0

    def arrive(stage, after):
        w_new, p_new = fetch(stage, after)
        W.update(w_new)
        P.update(p_new)

    bias = _band_bias(P["rel_table"], H, name="band_bias")
    saved = []
    xf, xb = x, x.astype(BF16)
    kg = vg = None
    for i in range(depth):
        s = {"x": xf, "xb": xb}
        arrive(f"{i}a", xf)
        if i < n_a:
            s["zp"] = _mm(xb, W["a_w_in", i], out_dtype=BF16, name=f"a{i}_in")
            arrive(f"{i}b", s["zp"])
            s["y"] = _sgu_fwd(s["zp"], P["a_w_s"][i], P["a_b_s"][i], P["a_ln_g"][i], P["a_ln_b"][i], name=f"a{i}_sgu")
            s["h"] = _mm(s["y"], W["a_w_out", i], name=f"a{i}_out")
        else:
            j = i - n_a
            if j == 0:
                kv = _mm(xb, W["kv_w", 0], out_dtype=BF16, name="kv_proj")
                kg = [(kv, 0) if d == 1 else (_to_group(kv[:, :HD], d), 0) for d in dils]
                vg = [(kv, 1) if d == 1 else (_to_group(kv[:, HD:], d), 0) for d in dils]
            q = _mm(xb, W["b_w_q", j], out_dtype=BF16, name=f"b{j}_q")
            s["qg"] = [(q, g) if d == 1 else (_to_group(q[:, g * HD:(g + 1) * HD], d), 0) for g, d in enumerate(dils)]
            os, ls = [], []
            for g, d in enumerate(dils):
                o_g, l_g = _attn_fwd(s["qg"][g], kg[g], vg[g], bias, g, d, name=f"b{j}_attn{g}")
                os.append(o_g)
                ls.append(l_g)
            s["oc"], s["ob"], s["lse"] = _attn_combine(os, ls, dils, name=f"b{j}_comb")
            arrive(f"{i}b", s["ob"])
            s["h"] = _mm(s["ob"], W["b_w_o", j], name=f"b{j}_o")
        s["x1"], s["x1b"] = _add_ln(xf, s["h"], P["ln_g"][i, 0], P["ln_b"][i, 0], alpha, name=f"l{i}_ln1")
        arrive(f"{i}c", s["x1b"])
        s["hup"] = _mm(s["x1b"], W["ffn_w_up", i], out_dtype=BF16, name=f"l{i}_up")
        arrive(f"{i}d", s["hup"])
        s["act"], s["ca"], s["cg"] = _convglu_fwd(s["hup"], P["ffn_conv_w"][i], P["ffn_conv_b"][i], name=f"l{i}_glu")
        s["f"] = _mm(s["act"], W["ffn_w_down", i], name=f"l{i}_down")
        xf, xb = _add_ln(s["x1"], s["f"], P["ln_g"][i, 1], P["ln_b"][i, 1], alpha, name=f"l{i}_ln2")
        saved.append(s)

    G, loss_row = _loss_grad(xf, tgt, name="loss")

    gw = {}

    def dw(key, layer, a, b, name, **kw):
        gw[key] = _mm(a, b, ta=True, out_dtype=BF16, out_into=(gw.get(key), n_layers[key], layer), name=name, **kw)

    gp = {k: [None] * n_a for k in ("a_ln_g", "a_ln_b", "a_w_s", "a_b_s")}
    gp.update({k: [None] * depth for k in ("ffn_conv_w", "ffn_conv_b", "ln_g", "ln_b")})
    dk_parts, dv_parts, dbias_parts = [], [], []
    for i in reversed(range(depth)):
        s = saved[i]
        dr2, dr2b, dg2, db2 = _ln_bwd(G, s["x1"], s["f"], P["ln_g"][i, 1], alpha, name=f"l{i}_ln2_bwd")
        dw("ffn_w_down", i, s["act"], dr2b, f"l{i}_down_dw")
        dact = _mm(dr2b, W["ffn_w_down", i], tb=True, out_dtype=BF16, name=f"l{i}_down_dx")
        dhup, dcw, dcb = _convglu_bwd(s["hup"], s["ca"], s["cg"], dact, P["ffn_conv_w"][i], name=f"l{i}_glu_bwd")
        gp["ffn_conv_w"][i] = dcw.transpose(1, 0, 2).reshape(dcw.shape[1], -1)
        gp["ffn_conv_b"][i] = dcb.reshape(-1)
        dw("ffn_w_up", i, s["x1b"], dhup, f"l{i}_up_dw", b_halves=True)
        token = ship(gw, [("ffn_w_down", i), ("ffn_w_up", i)])
        G1 = _mm(dhup, W["ffn_w_up", i], tb=True, a_halves=True, add=dr2, add_scale=alpha, after=token,
                 name=f"l{i}_up_dx")
        dr1, dr1b, dg1, db1 = _ln_bwd(G1, s["x"], s["h"], P["ln_g"][i, 0], alpha, name=f"l{i}_ln1_bwd")
        gp["ln_g"][i] = jnp.concatenate([dg1, dg2], axis=0)
        gp["ln_b"][i] = jnp.concatenate([db1, db2], axis=0)
        if i < n_a:
            dw("a_w_out", i, s["y"], dr1b, f"a{i}_out_dw")
            token = ship(gw, [("a_w_out", i)])
            dy = _mm(dr1b, W["a_w_out", i], tb=True, out_dtype=BF16, after=token, name=f"a{i}_out_dx")
            dzp, dws, dbs, dlg, dlb = _sgu_bwd(s["zp"], dy, P["a_w_s"][i], P["a_b_s"][i], P["a_ln_g"][i],
                                               P["a_ln_b"][i], name=f"a{i}_sgu_bwd")
            gp["a_w_s"][i], gp["a_b_s"][i] = dws, dbs[:, :dws.shape[0]].T
            gp["a_ln_g"][i], gp["a_ln_b"][i] = dlg[0], dlb[0]
            dw("a_w_in", i, s["xb"], dzp, f"a{i}_in_dw")
            token = ship(gw, [("a_w_in", i)])
            G = _mm(dzp, W["a_w_in", i], tb=True, add=dr1, add_scale=alpha, after=token, name=f"a{i}_in_dx")
        else:
            j = i - n_a
            dw("b_w_o", j, s["ob"], dr1b, f"b{j}_o_dw")
            do_tok = _mm(dr1b, W["b_w_o", j], tb=True, name=f"b{j}_o_dx")
            deltas, dobs = _attn_bwd_prep(do_tok, s["oc"], dils, name=f"b{j}_prep")
            dqs, dbs = [], []
            for g, d in enumerate(dils):
                dq, dk, dv, db = _attn_bwd(s["qg"][g], kg[g], vg[g], bias, g, d, dobs[g], s["lse"][g], deltas[g],
                                           name=f"b{j}_attn{g}_bwd")
                dqs.append((dq, d))
                dk_parts.append((dk, d))
                dv_parts.append((dv, d))
                dbs.append(db)
            dbias_parts.append(jnp.stack(dbs))
            dq_tok = _regroup_to_tokens(dqs, BF16, name=f"b{j}_dq_tokens")
            dw("b_w_q", j, s["xb"], dq_tok, f"b{j}_q_dw")
            token = ship(gw, [("b_w_o", j), ("b_w_q", j)])
            G = _mm(dq_tok, W["b_w_q", j], tb=True, add=dr1, add_scale=alpha, after=token, name=f"b{j}_q_dx")
            if j == 0:
                dkv = jnp.concatenate([_addn_grouped(dk_parts, BF16, name="dk_sum"),
                                       _addn_grouped(dv_parts, BF16, name="dv_sum")], axis=1)
                dw("kv_w", 0, s["xb"], dkv, "kv_dw")
                token = ship(gw, [("kv_w", 0)])
                G = _mm(dkv, W["kv_w", 0], tb=True, add=G, add_scale=1.0, after=token, name="kv_dx")
    rel = _rel_grad(dbias_parts, name="rel_grad")
    grel = rel[:, :, :H].transpose(1, 0, 2).reshape(REL_BUCKETS, N_GROUPS * H)
    gsmall = {k: jnp.stack(v) for k, v in gp.items()}
    gsmall["rel_table"] = grel
    return loss_row, G, gw, gsmall


BIG = (("a_w_in", COL), ("a_w_out", ROW), ("kv_w", ROW), ("b_w_q", COL), ("b_w_o", COL),
       ("ffn_w_up", COL), ("ffn_w_down", ROW))
SMALL_SHARDED = ("a_ln_g", "a_ln_b", "ffn_conv_w", "ln_g", "ln_b")
SMALL_REPLICATED = ("a_w_s", "a_b_s", "rel_table", "ffn_conv_b")
WEIGHTS = ("a_w_in", "a_ln_g", "a_ln_b", "a_w_s", "a_b_s", "a_w_out", "kv_w", "b_w_q", "b_w_o", "rel_table",
           "ffn_w_up", "ffn_conv_w", "ffn_conv_b", "ffn_w_down", "ln_g", "ln_b")


def kernel(x, a_w_in, a_ln_g, a_ln_b, a_w_s, a_b_s, a_w_out, kv_w, b_w_q, b_w_o, rel_table, ffn_w_up, ffn_conv_w, ffn_conv_b, ffn_w_down, ln_g, ln_b, loss_target, m_a_w_in, m_a_ln_g, m_a_ln_b, m_a_w_s, m_a_b_s, m_a_w_out, m_kv_w, m_b_w_q, m_b_w_o, m_rel_table, m_ffn_w_up, m_ffn_conv_w, m_ffn_conv_b, m_ffn_w_down, m_ln_g, m_ln_b, v_a_w_in, v_a_ln_g, v_a_ln_b, v_a_w_s, v_a_b_s, v_a_w_out, v_kv_w, v_b_w_q, v_b_w_o, v_rel_table, v_ffn_w_up, v_ffn_conv_w, v_ffn_conv_b, v_ffn_w_down, v_ln_g, v_ln_b):
    w = dict(a_w_in=a_w_in, a_ln_g=a_ln_g, a_ln_b=a_ln_b, a_w_s=a_w_s, a_b_s=a_b_s, a_w_out=a_w_out, kv_w=kv_w,
             b_w_q=b_w_q, b_w_o=b_w_o, rel_table=rel_table, ffn_w_up=ffn_w_up, ffn_conv_w=ffn_conv_w,
             ffn_conv_b=ffn_conv_b, ffn_w_down=ffn_w_down, ln_g=ln_g, ln_b=ln_b)
    m = dict(a_w_in=m_a_w_in, a_ln_g=m_a_ln_g, a_ln_b=m_a_ln_b, a_w_s=m_a_w_s, a_b_s=m_a_b_s, a_w_out=m_a_w_out,
             kv_w=m_kv_w, b_w_q=m_b_w_q, b_w_o=m_b_w_o, rel_table=m_rel_table, ffn_w_up=m_ffn_w_up,
             ffn_conv_w=m_ffn_conv_w, ffn_conv_b=m_ffn_conv_b, ffn_w_down=m_ffn_w_down, ln_g=m_ln_g, ln_b=m_ln_b)
    v = dict(a_w_in=v_a_w_in, a_ln_g=v_a_ln_g, a_ln_b=v_a_ln_b, a_w_s=v_a_w_s, a_b_s=v_a_b_s, a_w_out=v_a_w_out,
             kv_w=v_kv_w, b_w_q=v_b_w_q, b_w_o=v_b_w_o, rel_table=v_rel_table, ffn_w_up=v_ffn_w_up,
             ffn_conv_w=v_ffn_conv_w, ffn_conv_b=v_ffn_conv_b, ffn_w_down=v_ffn_w_down, ln_g=v_ln_g, ln_b=v_ln_b)
    chip = 2 * lax.axis_index("x") + lax.axis_index("y")

    big_names = [n for n, _ in BIG]
    big_axes = [a for _, a in BIG]
    where = jnp.stack([chip, lax.axis_index("c"), 2 * chip + lax.axis_index("c")]).astype(jnp.int32)
    shards = [w[n].reshape((1,) + w[n].shape) if w[n].ndim == 2 else w[n] for n in big_names]
    small_shapes = [w[n].shape for n in SMALL_SHARDED]
    small = _pack([w[n] for n in SMALL_SHARDED])
    shard_of = dict(zip(big_names, shards))
    axis_of = dict(BIG)
    n_layers = {n: s.shape[0] for n, s in shard_of.items()}
    n_a = w["a_w_s"].shape[0]
    stages = {}
    for i in range(w["ffn_w_up"].shape[0]):
        if i < n_a:
            stages[f"{i}a"], stages[f"{i}b"] = [("a_w_in", i)], [("a_w_out", i)]
        else:
            stages[f"{i}a"] = ([("kv_w", 0)] if i == n_a else []) + [("b_w_q", i - n_a)]
            stages[f"{i}b"] = [("b_w_o", i - n_a)]
        stages[f"{i}c"], stages[f"{i}d"] = [("ffn_w_up", i)], [("ffn_w_down", i)]
    stages["0a"].append(("small", 0))
    assert sum(len(st) for st in stages.values()) == sum(n_layers.values()) + 1
    flights, full_shape, all_started = {}, {}, None
    for batch, names in enumerate(([st for st in stages if st[0] == "0"], [st for st in stages if st[0] != "0"])):
        order = [key for st in names for key in stages[st]]
        placed, axes = [], []
        for n, l in order:
            if n == "small":
                placed.append(_place_shard(small[None], 0, ROW, where, F32, name="place_small"))
                axes.append(ROW)
            else:
                placed.append(_place_shard(shard_of[n], l, axis_of[n], where, BF16, name=f"place_{n}{l}",
                                           after=all_started))
                axes.append(axis_of[n])
                full_shape[n] = (n_layers[n],) + placed[-1].shape[1:]
        send_sems, recv_sems, flying, all_started = _gather_start(placed, axes, name=f"gather_start{batch}")
        for st in names:
            flights[st] = (send_sems, recv_sems, flying, axes, order)

    def fetch(stage, after):
        if stage not in stages:
            return {}, {}
        send_sems, recv_sems, flying, axes, order = flights[stage]
        if stage == "0a":
            after = all_started
        idxs = [order.index(key) for key in stages[stage]]
        landed = _gather_wait(send_sems, recv_sems, [flying[i] for i in idxs], [axes[i] for i in idxs], idxs, after,
                              name=f"gather_wait_{stage}")
        W_new, P_new = {}, {}
        for key, arr in zip(stages[stage], landed):
            if key[0] == "small":
                small_all = arr.reshape((N_CHIPS,) + small.shape)
                per_chip = [_unpack(small_all[j], small_shapes) for j in range(N_CHIPS)]
                for i, n in enumerate(SMALL_SHARDED):
                    P_new[n] = jnp.concatenate([per_chip[j][i] for j in range(N_CHIPS)], axis=-1)
            else:
                W_new[key] = arr.reshape(arr.shape[1:])
        return W_new, P_new

    slots ={n: lax.empty((N_DEV, n_layers[n]) + _part_geometry(full_shape[n], axis_of[n]), BF16) for n in big_names}
    sems, plan, tokens = [], [], []

    def ship(gw, items):
        names = [n for n, _ in items]
        send, recv, grads, landing, token = _scatter_start(
            [gw[n] for n in names], [slots[n] for n in names], [l for _, l in items], [axis_of[n] for n in names],
            name="scatter_start_" + "_".join(f"{n}{l}" for n, l in items))
        for n, g, s in zip(names, grads, landing):
            gw[n], slots[n] = g, s
        sems.append((send, recv))
        plan.append([(big_names.index(n), l) for n, l in items])
        tokens.append(token)
        return token

    loss_row, grad_x, gw, gsmall = _local_step(x[0], loss_target[0], {n: w[n] for n in SMALL_REPLICATED}, fetch,
                                               ship, n_layers)

    small_names = list(SMALL_SHARDED) + list(SMALL_REPLICATED)
    small_pack = _pack([gsmall[n] for n in small_names] + [loss_row[0, :1]])
    small_sems = _small_start(small_pack, lax.empty((N_DEV,) + small_pack.shape, F32), tokens[-1],
                              name="small_start")
    own, landed = _scatter_wait(sems, [gw[n] for n in big_names], [slots[n] for n in big_names], big_axes, plan,
                                grad_x, name="scatter_wait")
    sums = [_sum_parts(b, g, ax, where, name=f"sum_{n}") for (n, ax), b, g in zip(BIG, landed, own)]
    pairs = _pair_gather(sums, name="pair_gather")
    grad = {n: p.reshape(w[n].shape) for n, p in zip(big_names, pairs)}

    delta, new_m, new_v = {}, {}, {}
    for n, _ in BIG:
        delta[n], new_m[n], new_v[n] = _adamw(w[n], grad[n], m[n], v[n], name=f"adamw_{n}")
    small_own, small_slots = _small_wait(*small_sems, new_v[big_names[-1]], name="small_wait")
    small_sum = _sum_small(small_slots, small_own, where, name="sum_small")
    full_small = _unpack(small_sum, [gsmall[n].shape for n in small_names] + [(1,)])
    loss = full_small[-1][0]
    for n, gfull in zip(small_names, full_small):
        if n in SMALL_SHARDED:
            width = w[n].shape[-1]
            grad[n] = lax.dynamic_slice_in_dim(gfull, chip * width, width, axis=-1)
        else:
            grad[n] = gfull
    shapes = [w[n].shape for n in small_names]
    packed = _adamw(_pack([w[n] for n in small_names]), _pack([grad[n] for n in small_names]),
                    _pack([m[n] for n in small_names]), _pack([v[n] for n in small_names]), name="adamw_small")
    for out, res in zip((delta, new_m, new_v), packed):
        for n, a in zip(small_names, _unpack(res, shapes)):
            out[n] = a

    return (loss, grad_x[None], *[grad[n] for n in WEIGHTS], *[delta[n] for n in WEIGHTS],
            *[new_m[n] for n in WEIGHTS], *[new_v[n] for n in WEIGHTS])
```

```python
import functools
import math

import numpy as np
import jax
import jax.numpy as jnp
from jax import lax
from jax.experimental import pallas as pl
from jax.experimental.pallas import tpu as pltpu

F32 = jnp.float32
BF16 = jnp.bfloat16
MESH = pl.DeviceIdType.MESH

CHUNK = 128
HEAD_DIM = 64
DILATED_GROUPS = ((128, 1), (512, 4), (2048, 16))
N_GROUPS = len(DILATED_GROUPS)
BLK = 128
REL_BUCKETS = 32
REL_MAX_DIST = 2048
LN_EPS = 1e-5
NEG = -1e30
ADAM_LR = 0.001
ADAM_B1 = 0.9
ADAM_B2 = 0.999
ADAM_EPS = 1e-08
ADAM_WD = 0.01
ADAM_STEP = 10

N_CHIPS = 4
N_DEV = 8
LANES = 128
SUBLANES = 8
VMEM_LIMIT = 48 * 1024 * 1024

_SQRT_HALF = 0.7071067811865476
_INV_SQRT_2PI = 0.3989422804014327


def _cp(sem=None, vmem=VMEM_LIMIT):
    return pltpu.CompilerParams(dimension_semantics=sem, vmem_limit_bytes=vmem)


def _tile(dim, target, align=LANES):
    if dim <= target:
        return dim
    t = (target // align) * align
    while t >= align:
        if dim % t == 0:
            return t
        t -= align
    return dim


def _gelu(x):
    return 0.5 * x * (1.0 + lax.erf(x * _SQRT_HALF))


def _gelu_grad(x):
    return 0.5 * (1.0 + lax.erf(x * _SQRT_HALF)) + x * (_INV_SQRT_2PI * jnp.exp(-0.5 * x * x))


def _dot(a, b, ca, cb):
    return lax.dot_general(a, b, (((ca,), (cb,)), ((), ())), preferred_element_type=F32)


def _mn_tile(dim):
    return max(_tile(dim, 1024), _tile(dim, 1408))


def _mm(a, b, *, name, ta=False, tb=False, out_dtype=F32, tiles=None,
        a_lead=(), b_lead=(), a_halves=False, b_halves=False, add=None, add_scale=1.0, out_into=None, after=None):
    a2, b2 = a.shape[-2:], b.shape[-2:]
    M, K = (a2[1], a2[0]) if ta else a2
    Kb, N = (b2[1], b2[0]) if tb else b2
    if a_halves:
        K = 2 * K
    if b_halves:
        N = 2 * N
    assert K == Kb, (name, a.shape, b.shape)
    if tiles is None:
        k_target = 1408 if a_halves else (K if K <= 2816 else 2048)
        tm, tn = _mn_tile(M), _mn_tile(N // 2 if b_halves else N)
        tk = _tile(K // 2 if a_halves else K, k_target)
    else:
        tm = _tile(M, tiles[0])
        tn = _tile(N // 2 if b_halves else N, tiles[1])
        tk = _tile(K // 2 if a_halves else K, tiles[2])
    nm, nn, nk = M // tm, N // tn, K // tk
    nkh, nnh = nk // 2, nn // 2

    def lead(idx, rest):
        return tuple(idx) + tuple(rest)

    sq_a = (None,) * (a.ndim - 2)
    sq_b = (None,) * (b.ndim - 2)
    if a_halves:
        assert not ta and a.ndim == 3
        a_spec = pl.BlockSpec((None, tm, tk), lambda i, j, k: (k // nkh, i, k % nkh))
    elif ta:
        a_spec = pl.BlockSpec(sq_a + (tk, tm), lambda i, j, k: lead(a_lead, (k, i)))
    else:
        a_spec = pl.BlockSpec(sq_a + (tm, tk), lambda i, j, k: lead(a_lead, (i, k)))
    if b_halves:
        assert not tb and b.ndim == 3
        b_spec = pl.BlockSpec((None, tk, tn), lambda i, j, k: (j // nnh, k, j % nnh))
    elif tb:
        b_spec = pl.BlockSpec(sq_b + (tn, tk), lambda i, j, k: lead(b_lead, (j, k)))
    else:
        b_spec = pl.BlockSpec(sq_b + (tk, tn), lambda i, j, k: lead(b_lead, (k, j)))
    mn_spec = pl.BlockSpec((tm, tn), lambda i, j, k: (i, j))
    in_specs = [a_spec, b_spec]
    args = [a, b]
    if add is not None:
        in_specs.append(mn_spec)
        args.append(add)
    aliases = {}
    if out_into is None:
        o_spec, out_shape = mn_spec, jax.ShapeDtypeStruct((M, N), out_dtype)
    else:
        buf, n_layers, layer = out_into
        o_spec = pl.BlockSpec((None, tm, tn), lambda i, j, k: (layer, i, j))
        out_shape = jax.ShapeDtypeStruct((n_layers, M, N), out_dtype)
        if buf is not None:
            aliases = {len(args): 0}
            in_specs.append(pl.BlockSpec(memory_space=pl.ANY))
            args.append(buf)
    if after is not None:
        in_specs.append(pl.BlockSpec(memory_space=pl.ANY))
        args.append(after)
    n_in = len(args)
    ca, cb = (0 if ta else 1), (1 if tb else 0)

    def body(*refs):
        a_ref, b_ref = refs[:2]
        add_ref = refs[2] if add is not None else None
        o_ref = refs[n_in]

        def finish(acc):
            if add_ref is not None:
                acc = acc + add_scale * add_ref[...]
            o_ref[...] = acc.astype(o_ref.dtype)

        if nk == 1:
            finish(_dot(a_ref[...], b_ref[...], ca, cb))
        else:
            acc_ref = refs[-1]
            k = pl.program_id(2)

            @pl.when(k == 0)
            def _():
                acc_ref[...] = jnp.zeros_like(acc_ref)

            acc_ref[...] += _dot(a_ref[...], b_ref[...], ca, cb)

            @pl.when(k == nk - 1)
            def _():
                finish(acc_ref[...])

    return pl.pallas_call(
        body, name=name, grid=(nm, nn, nk),
        in_specs=in_specs, out_specs=o_spec, out_shape=out_shape,
        input_output_aliases=aliases,
        scratch_shapes=[] if nk == 1 else [pltpu.VMEM((tm, tn), F32)],
        compiler_params=_cp(("parallel", "parallel", "arbitrary")),
    )(*args)


def _ln_stats(r):
    mu = jnp.mean(r, axis=-1, keepdims=True)
    xc = r - mu
    var = jnp.mean(xc * xc, axis=-1, keepdims=True)
    rstd = lax.rsqrt(var + LN_EPS)
    return xc * rstd, rstd


def _add_ln(x, h, g, b, alpha, *, name):
    T, D = x.shape
    tr = _tile(T, 1024, SUBLANES)
    row = pl.BlockSpec((tr, D), lambda i: (i, 0))
    vec = pl.BlockSpec((1, D), lambda i: (0, 0))

    def body(x_ref, h_ref, g_ref, b_ref, o_ref, ob_ref):
        xhat, _ = _ln_stats(alpha * x_ref[...] + h_ref[...])
        y = xhat * g_ref[...] + b_ref[...]
        o_ref[...] = y
        ob_ref[...] = y.astype(BF16)

    return pl.pallas_call(
        body, name=name, grid=(T // tr,),
        in_specs=[row, row, vec, vec], out_specs=[row, row],
        out_shape=[jax.ShapeDtypeStruct((T, D), F32), jax.ShapeDtypeStruct((T, D), BF16)],
        compiler_params=_cp(("parallel",)),
    )(x, h, g.reshape(1, D), b.reshape(1, D))


def _ln_bwd(dy, x, h, g, alpha, *, name):
    T, D = x.shape
    tr = _tile(T, 512, SUBLANES)
    row = pl.BlockSpec((tr, D), lambda i: (i, 0))
    vec = pl.BlockSpec((1, D), lambda i: (0, 0))

    def body(dy_ref, x_ref, h_ref, g_ref, dr_ref, drb_ref, dg_ref, db_ref):
        @pl.when(pl.program_id(0) == 0)
        def _():
            dg_ref[...] = jnp.zeros_like(dg_ref)
            db_ref[...] = jnp.zeros_like(db_ref)

        xhat, rstd = _ln_stats(alpha * x_ref[...] + h_ref[...])
        dyv = dy_ref[...]
        dxhat = dyv * g_ref[...]
        c1 = jnp.mean(dxhat, axis=-1, keepdims=True)
        c2 = jnp.mean(dxhat * xhat, axis=-1, keepdims=True)
        dr = rstd * (dxhat - c1 - xhat * c2)
        dr_ref[...] = dr
        drb_ref[...] = dr.astype(BF16)
        dg_ref[...] += jnp.sum(dyv * xhat, axis=0, keepdims=True)
        db_ref[...] += jnp.sum(dyv, axis=0, keepdims=True)

    return pl.pallas_call(
        body, name=name, grid=(T // tr,),
        in_specs=[row, row, row, vec], out_specs=[row, row, vec, vec],
        out_shape=[jax.ShapeDtypeStruct((T, D), F32), jax.ShapeDtypeStruct((T, D), BF16),
                   jax.ShapeDtypeStruct((1, D), F32), jax.ShapeDtypeStruct((1, D), F32)],
        compiler_params=_cp(("arbitrary",)),
    )(dy, x, h, g.reshape(1, D))


def _sgu_pieces(zp, lg, lb, E):
    z = _gelu(zp)
    u, v = z[:, :E], z[:, E:]
    vhat, rstd = _ln_stats(v)
    vn = vhat * lg + lb
    return u, vhat, rstd, vn


def _tril_mask():
    t = lax.broadcasted_iota(jnp.int32, (CHUNK, CHUNK), 0)
    s = lax.broadcasted_iota(jnp.int32, (CHUNK, CHUNK), 1)
    return s <= t


def _sgu_fwd(zp, ws, bs, lg, lb, *, name):
    T, E2 = zp.shape
    E = E2 // 2
    G = ws.shape[0]
    cw = E // G

    def body(zp_ref, ws_ref, bs_ref, lg_ref, lb_ref, y_ref):
        u, _, _, vn = _sgu_pieces(zp_ref[...].astype(F32), lg_ref[...], lb_ref[...], E)
        vnb = vn.astype(BF16)
        tril = _tril_mask()
        for g in range(G):
            cols = slice(g * cw, (g + 1) * cw)
            w = jnp.where(tril, ws_ref[g], 0.0).astype(BF16)
            sv = _dot(w, vnb[:, cols], 1, 0) + bs_ref[g]
            y_ref[:, cols] = (u[:, cols] * sv).astype(BF16)

    return pl.pallas_call(
        body, name=name, grid=(T // CHUNK,),
        in_specs=[pl.BlockSpec((CHUNK, E2), lambda n: (n, 0)),
                  pl.BlockSpec((G, CHUNK, CHUNK), lambda n: (0, 0, 0)),
                  pl.BlockSpec((G, CHUNK, 1), lambda n: (0, 0, 0)),
                  pl.BlockSpec((1, E), lambda n: (0, 0)),
                  pl.BlockSpec((1, E), lambda n: (0, 0))],
        out_specs=pl.BlockSpec((CHUNK, E), lambda n: (n, 0)),
        out_shape=jax.ShapeDtypeStruct((T, E), BF16),
        compiler_params=_cp(("parallel",)),
    )(zp, ws, bs.reshape(G, CHUNK, 1), lg.reshape(1, E), lb.reshape(1, E))


def _sgu_bwd(zp, dy, ws, bs, lg, lb, *, name):
    T, E2 = zp.shape
    E = E2 // 2
    G = ws.shape[0]
    cw = E // G
    assert G <= LANES

    def body(zp_ref, dy_ref, ws_ref, bs_ref, lg_ref, lb_ref,
             dzp_ref, dws_ref, dbs_ref, dlg_ref, dlb_ref, dvn_ref):
        @pl.when(pl.program_id(0) == 0)
        def _():
            dws_ref[...] = jnp.zeros_like(dws_ref)
            dbs_ref[...] = jnp.zeros_like(dbs_ref)
            dlg_ref[...] = jnp.zeros_like(dlg_ref)
            dlb_ref[...] = jnp.zeros_like(dlb_ref)

        zpv = zp_ref[...].astype(F32)
        lgv = lg_ref[...]
        u, vhat, rstd, vn = _sgu_pieces(zpv, lgv, lb_ref[...], E)
        gp = _gelu_grad(zpv)
        vnb = vn.astype(BF16)
        tril = _tril_mask()
        lane = lax.broadcasted_iota(jnp.int32, (CHUNK, LANES), 1)
        dbs = dbs_ref[...]
        for g in range(G):
            cols = slice(g * cw, (g + 1) * cw)
            w = jnp.where(tril, ws_ref[g], 0.0).astype(BF16)
            sv = _dot(w, vnb[:, cols], 1, 0) + bs_ref[g]
            dyg = dy_ref[:, cols].astype(F32)
            dzp_ref[:, cols] = (dyg * sv * gp[:, cols]).astype(BF16)
            dsv = dyg * u[:, cols]
            dsvb = dsv.astype(BF16)
            dws_ref[g] += jnp.where(tril, _dot(dsvb, vnb[:, cols], 1, 1), 0.0)
            dvn_ref[:, cols] = _dot(w, dsvb, 0, 0)
            dbs = dbs + jnp.where(lane == g, jnp.sum(dsv, axis=1, keepdims=True), 0.0)
        dbs_ref[...] = dbs
        dvn = dvn_ref[...]
        dlg_ref[...] += jnp.sum(dvn * vhat, axis=0, keepdims=True)
        dlb_ref[...] += jnp.sum(dvn, axis=0, keepdims=True)
        dvhat = dvn * lgv
        c1 = jnp.mean(dvhat, axis=-1, keepdims=True)
        c2 = jnp.mean(dvhat * vhat, axis=-1, keepdims=True)
        dv = rstd * (dvhat - c1 - vhat * c2)
        dzp_ref[:, E:] = (dv * gp[:, E:]).astype(BF16)

    vecE = pl.BlockSpec((1, E), lambda n: (0, 0))
    return pl.pallas_call(
        body, name=name, grid=(T // CHUNK,),
        in_specs=[pl.BlockSpec((CHUNK, E2), lambda n: (n, 0)),
                  pl.BlockSpec((CHUNK, E), lambda n: (n, 0)),
                  pl.BlockSpec((G, CHUNK, CHUNK), lambda n: (0, 0, 0)),
                  pl.BlockSpec((G, CHUNK, 1), lambda n: (0, 0, 0)),
                  vecE, vecE],
        out_specs=[pl.BlockSpec((CHUNK, E2), lambda n: (n, 0)),
                   pl.BlockSpec((G, CHUNK, CHUNK), lambda n: (0, 0, 0)),
                   pl.BlockSpec((CHUNK, LANES), lambda n: (0, 0)),
                   vecE, vecE],
        out_shape=[jax.ShapeDtypeStruct((T, E2), BF16),
                   jax.ShapeDtypeStruct((G, CHUNK, CHUNK), F32),
                   jax.ShapeDtypeStruct((CHUNK, LANES), F32),
                   jax.ShapeDtypeStruct((1, E), F32), jax.ShapeDtypeStruct((1, E), F32)],
        scratch_shapes=[pltpu.VMEM((CHUNK, E), F32)],
        compiler_params=_cp(("arbitrary",)),
    )(zp, dy, ws, bs.reshape(G, CHUNK, 1), lg.reshape(1, E), lb.reshape(1, E))


def _delay(h, k):
    t = lax.broadcasted_iota(jnp.int32, h.shape, 0)
    return jnp.where(t >= k, pltpu.roll(h, k, 0), 0.0)


def _advance(d, k):
    T = d.shape[0]
    t = lax.broadcasted_iota(jnp.int32, d.shape, 0)
    return jnp.where(t < T - k, pltpu.roll(d, T - k, 0), 0.0)


def _conv3(h, w_ref, b_ref):
    return w_ref[2:3, :] * h + w_ref[1:2, :] * _delay(h, 1) + w_ref[0:1, :] * _delay(h, 2) + b_ref[...]


def _convglu_fwd(hup, cw, cb, *, name):
    T, F2 = hup.shape
    F = F2 // 2
    tc = LANES
    nt = F // tc

    def body(ha_ref, hg_ref, wa_ref, wg_ref, ba_ref, bg_ref, o_ref, a_ref, g_ref):
        a = _conv3(ha_ref[...].astype(F32), wa_ref, ba_ref)
        g = _conv3(hg_ref[...].astype(F32), wg_ref, bg_ref)
        o_ref[...] = (_gelu(a) * g).astype(BF16)
        a_ref[...] = a.astype(BF16)
        g_ref[...] = g.astype(BF16)

    col = lambda off: pl.BlockSpec((T, tc), lambda j: (0, j + off))
    w3 = lambda off: pl.BlockSpec((3, tc), lambda j: (0, j + off))
    b1 = lambda off: pl.BlockSpec((1, tc), lambda j: (0, j + off))
    shp = jax.ShapeDtypeStruct((T, F), BF16)
    return pl.pallas_call(
        body, name=name, grid=(nt,),
        in_specs=[col(0), col(nt), w3(0), w3(nt), b1(0), b1(nt)],
        out_specs=[col(0)] * 3, out_shape=[shp] * 3,
        compiler_params=_cp(("parallel",)),
    )(hup, hup, cw, cw, cb.reshape(1, F2), cb.reshape(1, F2))


def _convglu_bwd(hup, a, g, dact, cw, *, name):
    T, F2 = hup.shape
    F = F2 // 2
    tc = LANES
    nt = F // tc

    def half(h, w_ref, d, dh_ref, dw_ref, db_ref, i):
        d1, d2 = _advance(d, 1), _advance(d, 2)
        dh_ref[i] = (w_ref[2:3, :] * d + w_ref[1:2, :] * d1 + w_ref[0:1, :] * d2).astype(BF16)
        c = _dot(h, jnp.concatenate([d2, d1, d], axis=1).astype(BF16), 0, 0)
        eye = lax.broadcasted_iota(jnp.int32, (tc, tc), 0) == lax.broadcasted_iota(jnp.int32, (tc, tc), 1)
        for k in range(3):
            dw_ref[i, k:k + 1, :] = jnp.sum(jnp.where(eye, c[:, k * tc:(k + 1) * tc], 0.0), axis=0, keepdims=True)
        db_ref[i] = jnp.sum(d, axis=0, keepdims=True)

    def body(ha_ref, hg_ref, a_ref, g_ref, d_ref, wa_ref, wg_ref, dh_ref, dw_ref, db_ref):
        a = a_ref[...].astype(F32)
        d = d_ref[...].astype(F32)
        half(ha_ref[...].astype(BF16), wa_ref, d * g_ref[...].astype(F32) * _gelu_grad(a), dh_ref, dw_ref, db_ref, 0)
        half(hg_ref[...].astype(BF16), wg_ref, d * _gelu(a), dh_ref, dw_ref, db_ref, 1)

    col = lambda off: pl.BlockSpec((T, tc), lambda j: (0, j + off))
    w3 = lambda off: pl.BlockSpec((3, tc), lambda j: (0, j + off))
    return pl.pallas_call(
        body, name=name, grid=(nt,),
        in_specs=[col(0), col(nt), col(0), col(0), col(0), w3(0), w3(nt)],
        out_specs=[pl.BlockSpec((2, T, tc), lambda j: (0, 0, j)),
                   pl.BlockSpec((2, 3, tc), lambda j: (0, 0, j)),
                   pl.BlockSpec((2, 1, tc), lambda j: (0, 0, j))],
        out_shape=[jax.ShapeDtypeStruct((2, T, F), BF16),
                   jax.ShapeDtypeStruct((2, 3, F), F32),
                   jax.ShapeDtypeStruct((2, 1, F), F32)],
        compiler_params=_cp(("parallel",)),
    )(hup, hup, a, g, dact, cw, cw)


def _bucket_table():
    iq = np.arange(BLK)[:, None]
    ik = np.arange(2 * BLK)[None, :]
    delta = iq + BLK - ik
    out = []
    for win, dil in DILATED_GROUPS:
        valid = (delta >= 0) & (delta <= win // dil)
        n = (np.clip(delta, 0, None) * dil).astype(np.int32)
        max_exact = REL_BUCKETS // 2
        nf = np.maximum(n, 1).astype(np.float32)
        large = max_exact + (np.log(nf / np.float32(max_exact)) / np.float32(math.log(REL_MAX_DIST / max_exact))
                             * np.float32(REL_BUCKETS - max_exact)).astype(np.int32)
        large = np.minimum(large, REL_BUCKETS - 1)
        out.append(np.where(valid, np.where(n < max_exact, n, large), -1))
    return np.stack(out).astype(np.int32)


def _band_bias(rel_table, H, *, name):
    bkt = jnp.asarray(_bucket_table())

    def body(tab_ref, bk_ref, o_ref):
        g = pl.program_id(0)
        bk = bk_ref[...]
        prev = lax.broadcasted_iota(jnp.int32, bk.shape, 1) < BLK
        for h in range(H):
            def bucket(c, acc):
                return jnp.where(bk == c, tab_ref[c, g * H + h], acc)

            b = lax.fori_loop(0, REL_BUCKETS, bucket, jnp.zeros(bk.shape, F32))
            b = jnp.where(bk >= 0, b, NEG)
            o_ref[h, 0] = b
            o_ref[h, 1] = jnp.where(prev, NEG, b)

    return pl.pallas_call(
        body, name=name, grid=(N_GROUPS,),
        in_specs=[pl.BlockSpec(memory_space=pltpu.SMEM),
                  pl.BlockSpec((None, BLK, 2 * BLK), lambda g: (g, 0, 0))],
        out_specs=pl.BlockSpec((None, H, 2, BLK, 2 * BLK), lambda g: (g, 0, 0, 0, 0)),
        out_shape=jax.ShapeDtypeStruct((N_GROUPS, H, 2, BLK, 2 * BLK), F32),
        compiler_params=_cp(("parallel",)),
    )(rel_table, bkt)


PAIR = 2 * HEAD_DIM


def _bdot(a, b, ca, cb):
    return lax.dot_general(a, b, (((ca,), (cb,)), ((0,), (0,))), preferred_element_type=F32)


def _blocks_per_step(nb, dil):
    return 2 if (nb // dil) % 2 == 0 else 1


def _blk(i):
    return slice(i * BLK, (i + 1) * BLK)


def _heads(ref, i, H, own_columns_only):
    low = lax.broadcasted_iota(jnp.int32, (BLK, PAIR), 1) < HEAD_DIM
    out = []
    for h in range(H):
        a = ref[_blk(i), (h // 2) * PAIR:(h // 2 + 1) * PAIR]
        if own_columns_only:
            a = jnp.where(low if h % 2 == 0 else jnp.logical_not(low), a, jnp.zeros_like(a))
        out.append(a)
    return jnp.stack(out)


def _pair_sums(o):
    return [o[2 * p] + o[2 * p + 1] for p in range(o.shape[0] // 2)]


def _store_pairs(ref, i, o, dtype, scale=None):
    for p, v in enumerate(_pair_sums(o)):
        ref[_blk(i), p * PAIR:(p + 1) * PAIR] = (v if scale is None else scale * v).astype(dtype)


def _stat_tile(s):
    lane = lax.broadcasted_iota(jnp.int32, (BLK, LANES), 1)
    t = jnp.zeros((BLK, LANES), F32)
    for h in range(s.shape[0]):
        t = jnp.where(lane == h, s[h], t)
    return t


def _stat_cols(ref, i, H):
    return jnp.stack([ref[_blk(i), h:h + 1] for h in range(H)])


def _scores(qm, kp, kc, bias):
    scale = HEAD_DIM ** -0.5
    sp = _bdot(qm, kp, 2, 2) * scale + bias[:, :, :BLK]
    sc = _bdot(qm, kc, 2, 2) * scale + bias[:, :, BLK:]
    return sp, sc


def _attn_specs(nb, dil, g, H, HD, sb):
    nbg = nb // dil
    QB = _blocks_per_step(nb, dil)
    cur = lambda col: pl.BlockSpec((QB * BLK, HD), lambda b: (sb(b), col))
    prev = lambda col: pl.BlockSpec((BLK, HD), lambda b: (jnp.maximum(QB * sb(b) - 1, 0), col))
    stat = pl.BlockSpec((QB * BLK, LANES), lambda b: (sb(b), 0))
    bias0 = pl.BlockSpec((None, H, None, BLK, 2 * BLK),
                         lambda b: (g, 0, jnp.where((QB * sb(b)) % nbg == 0, 1, 0), 0, 0))
    bias1 = pl.BlockSpec((None, H, None, BLK, 2 * BLK), lambda b: (g, 0, 0, 0, 0))
    return cur, prev, stat, bias0, bias1


def _attn_fwd(q, k, v, bias, g, dil, *, name):
    (qa, qc), (ka, kc_), (va, vc_) = q, k, v
    T = qa.shape[0]
    H = bias.shape[1]
    HD = H * HEAD_DIM
    assert H % 2 == 0 and H <= LANES and T % (dil * BLK) == 0
    nb = T // BLK
    QB = _blocks_per_step(nb, dil)
    cur, prev, stat, bias0, bias1 = _attn_specs(nb, dil, g, H, HD, lambda b: b)

    def body(q_ref, kp_ref, kc_ref, vp_ref, vc_ref, b0_ref, b1_ref, o_ref, l_ref):
        for i in range(QB):
            kpi, vpi = (kp_ref, vp_ref) if i == 0 else (kc_ref, vc_ref)
            sp, sc = _scores(_heads(q_ref, i, H, True), _heads(kpi, max(i - 1, 0), H, False),
                             _heads(kc_ref, i, H, False), (b0_ref if i == 0 else b1_ref)[...])
            m = jnp.maximum(jnp.max(sp, axis=-1, keepdims=True), jnp.max(sc, axis=-1, keepdims=True))
            pp, pc = jnp.exp(sp - m), jnp.exp(sc - m)
            den = jnp.sum(pp, axis=-1, keepdims=True) + jnp.sum(pc, axis=-1, keepdims=True)
            o = (_bdot((pp / den).astype(BF16), _heads(vpi, max(i - 1, 0), H, True), 2, 1)
                 + _bdot((pc / den).astype(BF16), _heads(vc_ref, i, H, True), 2, 1))
            _store_pairs(o_ref, i, o, F32)
            l_ref[_blk(i), :] = _stat_tile(m + jnp.log(den))

    return pl.pallas_call(
        body, name=name, grid=(nb // QB,),
        in_specs=[cur(qc), prev(kc_), cur(kc_), prev(vc_), cur(vc_), bias0, bias1],
        out_specs=[cur(0), stat],
        out_shape=[jax.ShapeDtypeStruct((T, HD), F32), jax.ShapeDtypeStruct((T, LANES), F32)],
        compiler_params=_cp(("parallel",)),
    )(qa, ka, ka, va, va, bias, bias)


def _grouped_spec(tr, d, C):
    return pl.BlockSpec((d, tr // d, C), lambda i: (0, i, 0))


def _as_grouped(a, d):
    T, C = a.shape
    return a.reshape(d, T // d, C)


def _to_token_slabs(ref, slabs, d):
    n_r = ref.shape[1]
    for j in range(ref.shape[2] // LANES):
        for r in range(d):
            slabs.at[j][pl.ds(r, n_r, stride=d), :] = ref[r, :, j * LANES:(j + 1) * LANES].astype(F32)


def _from_token_slabs(slabs, ref, d):
    n_r = ref.shape[1]
    for j in range(ref.shape[2] // LANES):
        for r in range(d):
            ref[r, :, j * LANES:(j + 1) * LANES] = slabs.at[j][pl.ds(r, n_r, stride=d), :].astype(ref.dtype)


def _regroup_to_tokens(parts, out_dtype, *, name):
    T, C = parts[0][0].shape
    n = len(parts)
    tr = _tile(T, 1024, SUBLANES)
    n_sl = C // LANES

    def body(*refs):
        o_ref, scr = refs[n], refs[n + 1]
        for g, (_, d) in enumerate(parts):
            _to_token_slabs(refs[g], scr, d)
            for j in range(n_sl):
                o_ref[:, g * C + j * LANES:g * C + (j + 1) * LANES] = scr[j].astype(out_dtype)

    return pl.pallas_call(
        body, name=name, grid=(T // tr,),
        in_specs=[_grouped_spec(tr, d, C) for _, d in parts],
        out_specs=pl.BlockSpec((tr, n * C), lambda i: (i, 0)),
        out_shape=jax.ShapeDtypeStruct((T, n * C), out_dtype),
        scratch_shapes=[pltpu.VMEM((n_sl, tr, LANES), F32)],
        compiler_params=_cp(("parallel",)),
    )(*[_as_grouped(a, d) for a, d in parts])


def _attn_combine(os, ls, dils, *, name):
    n = len(os)
    T, HD = os[0].shape
    H = HD // HEAD_DIM
    tr = _tile(T, 1024, SUBLANES)
    n_sl = HD // LANES
    assert PAIR == LANES

    def body(*refs):
        o_refs, l_refs = refs[:n], refs[n:2 * n]
        oc_ref, ocb_ref = refs[2 * n:2 * n + 2]
        lt_refs = refs[2 * n + 2:3 * n + 2]
        o_scr, l_scr = refs[3 * n + 2], refs[3 * n + 3]
        for g in range(n):
            _to_token_slabs(o_refs[g], o_scr.at[g], dils[g])
            _to_token_slabs(l_refs[g], l_scr.at[g], dils[g])
        ls_ = [l_scr[g, 0] for g in range(n)]
        m = ls_[0]
        for l in ls_[1:]:
            m = jnp.maximum(m, l)
        ws = [jnp.exp(l - m) for l in ls_]
        den = ws[0]
        for w in ws[1:]:
            den = den + w
        l_scr[n, 0] = m + jnp.log(den)
        for g in range(n):
            _from_token_slabs(l_scr.at[n], lt_refs[g], dils[g])
        ws = [w / den for w in ws]
        low = lax.broadcasted_iota(jnp.int32, (tr, PAIR), 1) < HEAD_DIM
        for p in range(H // 2):
            acc = jnp.zeros((tr, PAIR), F32)
            for g in range(n):
                w = jnp.where(low, ws[g][:, 2 * p:2 * p + 1], ws[g][:, 2 * p + 1:2 * p + 2])
                acc = acc + w * o_scr[g, p]
            oc_ref[:, p * PAIR:(p + 1) * PAIR] = acc
            ocb_ref[:, p * PAIR:(p + 1) * PAIR] = acc.astype(BF16)

    wide = pl.BlockSpec((tr, HD), lambda i: (i, 0))
    lts = pl.pallas_call(
        body, name=name, grid=(T // tr,),
        in_specs=[_grouped_spec(tr, d, HD) for d in dils] + [_grouped_spec(tr, d, LANES) for d in dils],
        out_specs=[wide, wide] + [_grouped_spec(tr, d, LANES) for d in dils],
        out_shape=[jax.ShapeDtypeStruct((T, HD), F32), jax.ShapeDtypeStruct((T, HD), BF16)]
        + [jax.ShapeDtypeStruct((d, T // d, LANES), F32) for d in dils],
        scratch_shapes=[pltpu.VMEM((n, n_sl, tr, LANES), F32), pltpu.VMEM((n + 1, 1, tr, LANES), F32)],
        compiler_params=_cp(("parallel",)),
    )(*[_as_grouped(o, d) for o, d in zip(os, dils)], *[_as_grouped(l, d) for l, d in zip(ls, dils)])
    return lts[0], lts[1], [l.reshape(T, LANES) for l in lts[2:]]


def _attn_bwd_prep(do, oc, dils, *, name):
    T, HD = do.shape
    H = HD // HEAD_DIM
    n = len(dils)
    tr = _tile(T, 1024, SUBLANES)
    n_sl = HD // LANES
    wide = pl.BlockSpec((tr, HD), lambda i: (i, 0))

    def body(*refs):
        do_ref, oc_ref = refs[:2]
        dl_refs, dob_refs = refs[2:2 + n], refs[2 + n:2 + 2 * n]
        do_scr, dl_scr = refs[2 + 2 * n], refs[3 + 2 * n]
        low = lax.broadcasted_iota(jnp.int32, (tr, PAIR), 1) < HEAD_DIM
        lane = lax.broadcasted_iota(jnp.int32, (tr, LANES), 1)
        dl = jnp.zeros((tr, LANES), F32)
        for p in range(H // 2):
            d = do_ref[:, p * PAIR:(p + 1) * PAIR]
            do_scr[p] = d
            prod = d * oc_ref[:, p * PAIR:(p + 1) * PAIR]
            dl = jnp.where(lane == 2 * p, jnp.sum(jnp.where(low, prod, 0.0), axis=-1, keepdims=True), dl)
            dl = jnp.where(lane == 2 * p + 1, jnp.sum(jnp.where(low, 0.0, prod), axis=-1, keepdims=True), dl)
        dl_scr[0] = dl
        for g in range(n):
            _from_token_slabs(dl_scr, dl_refs[g], dils[g])
            _from_token_slabs(do_scr, dob_refs[g], dils[g])

    outs = pl.pallas_call(
        body, name=name, grid=(T // tr,),
        in_specs=[wide, wide],
        out_specs=[_grouped_spec(tr, d, LANES) for d in dils] + [_grouped_spec(tr, d, HD) for d in dils],
        out_shape=[jax.ShapeDtypeStruct((d, T // d, LANES), F32) for d in dils]
        + [jax.ShapeDtypeStruct((d, T // d, HD), BF16) for d in dils],
        scratch_shapes=[pltpu.VMEM((n_sl, tr, LANES), F32), pltpu.VMEM((1, tr, LANES), F32)],
        compiler_params=_cp(("parallel",)),
    )(do, oc)
    return [a.reshape(T, LANES) for a in outs[:n]], [a.reshape(T, HD) for a in outs[n:]]


def _attn_bwd(q, k, v, bias, g, dil, do, lse, delta, *, name):
    (qa, qc), (ka, kc_), (va, vc_) = q, k, v
    T, HD = do.shape
    H = HD // HEAD_DIM
    nb = T // BLK
    scale = HEAD_DIM ** -0.5
    QB = _blocks_per_step(nb, dil)
    ns = nb // QB
    cur, prev, stat, bias0, bias1 = _attn_specs(nb, dil, g, H, HD, lambda b: jnp.minimum(b, ns - 1))
    late = pl.BlockSpec((QB * BLK, HD), lambda b: (jnp.maximum(b - 1, 0), 0))

    def body(q_ref, kp_ref, kc_ref, vp_ref, vc_ref, b0_ref, b1_ref, do_ref, l_ref, dl_ref,
             dq_ref, dk_ref, dv_ref, db_ref, ck_ref, cv_ref):
        b = pl.program_id(0)

        @pl.when(b == 0)
        def _():
            db_ref[...] = jnp.zeros_like(db_ref)
            ck_ref[...] = jnp.zeros_like(ck_ref)
            cv_ref[...] = jnp.zeros_like(cv_ref)

        @pl.when(b < ns)
        def _():
            dk_ref[...] = ck_ref[...]
            dv_ref[...] = cv_ref[...]
            for i in range(QB):
                kpi, vpi, ip = (kp_ref, vp_ref, 0) if i == 0 else (kc_ref, vc_ref, i - 1)
                qm, dom = _heads(q_ref, i, H, True), _heads(do_ref, i, H, True)
                sp, sc = _scores(qm, _heads(kpi, ip, H, False), _heads(kc_ref, i, H, False),
                                 (b0_ref if i == 0 else b1_ref)[...])
                l, dl = _stat_cols(l_ref, i, H), _stat_cols(dl_ref, i, H)
                pp, pc = jnp.exp(sp - l), jnp.exp(sc - l)
                dsp = pp * (_bdot(dom, _heads(vpi, ip, H, False), 2, 2) - dl)
                dsc = pc * (_bdot(dom, _heads(vc_ref, i, H, False), 2, 2) - dl)
                db_ref[:, :, :BLK] += dsp
                db_ref[:, :, BLK:] += dsc
                dspb, dscb = dsp.astype(BF16), dsc.astype(BF16)
                _store_pairs(dq_ref, i, _bdot(dspb, _heads(kpi, ip, H, True), 2, 1)
                             + _bdot(dscb, _heads(kc_ref, i, H, True), 2, 1), F32, scale)
                dkp = _pair_sums(_bdot(dspb, qm, 1, 1))
                dkc = _pair_sums(_bdot(dscb, qm, 1, 1))
                dvp = _pair_sums(_bdot(pp.astype(BF16), dom, 1, 1))
                dvc = _pair_sums(_bdot(pc.astype(BF16), dom, 1, 1))
                for p in range(H // 2):
                    cols = slice(p * PAIR, (p + 1) * PAIR)
                    if i == 0:
                        dk_ref[_blk(QB - 1), cols] += scale * dkp[p]
                        dv_ref[_blk(QB - 1), cols] += dvp[p]
                    else:
                        ck_ref[_blk(i - 1), cols] += scale * dkp[p]
                        cv_ref[_blk(i - 1), cols] += dvp[p]
                    ck_ref[_blk(i), cols] = scale * dkc[p]
                    cv_ref[_blk(i), cols] = dvc[p]

        @pl.when(b == ns)
        def _():
            dk_ref[...] = ck_ref[...]
            dv_ref[...] = cv_ref[...]

    f32 = jax.ShapeDtypeStruct((T, HD), F32)
    return pl.pallas_call(
        body, name=name, grid=(ns + 1,),
        in_specs=[cur(qc), prev(kc_), cur(kc_), prev(vc_), cur(vc_), bias0, bias1, cur(0), stat, stat],
        out_specs=[cur(0), late, late, pl.BlockSpec((H, BLK, 2 * BLK), lambda b: (0, 0, 0))],
        out_shape=[f32, f32, f32,
                   jax.ShapeDtypeStruct((H, BLK, 2 * BLK), F32)],
        scratch_shapes=[pltpu.VMEM((QB * BLK, HD), F32), pltpu.VMEM((QB * BLK, HD), F32)],
        compiler_params=_cp(("arbitrary",)),
    )(qa, ka, ka, va, va, bias, bias, do, lse, delta)


def _rel_grad(dbs, *, name):
    H = dbs[0].shape[1]
    n = len(dbs)
    assert H <= LANES
    bkt = jnp.asarray(_bucket_table())

    def body(*refs):
        bk = refs[n][...]
        o_ref = refs[n + 1]
        db = refs[0][...]
        for r in refs[1:n]:
            db = db + r[...]
        row = lax.broadcasted_iota(jnp.int32, (REL_BUCKETS, LANES), 0)
        col = lax.broadcasted_iota(jnp.int32, (REL_BUCKETS, LANES), 1)

        def bucket(c, acc):
            sel = bk == c
            for h in range(H):
                val = jnp.sum(jnp.where(sel, db[h], 0.0))
                acc = jnp.where((row == c) & (col == h), val, acc)
            return acc

        o_ref[...] = lax.fori_loop(0, REL_BUCKETS, bucket, jnp.zeros((REL_BUCKETS, LANES), F32))

    dspec = pl.BlockSpec((None, H, BLK, 2 * BLK), lambda g: (g, 0, 0, 0))
    return pl.pallas_call(
        body, name=name, grid=(N_GROUPS,),
        in_specs=[dspec] * n + [pl.BlockSpec((None, BLK, 2 * BLK), lambda g: (g, 0, 0))],
        out_specs=pl.BlockSpec((None, REL_BUCKETS, LANES), lambda g: (g, 0, 0)),
        out_shape=jax.ShapeDtypeStruct((N_GROUPS, REL_BUCKETS, LANES), F32),
        compiler_params=_cp(("parallel",)),
    )(*dbs, bkt)


def _to_group(a, d):
    T, C = a.shape
    return a if d == 1 else a.reshape(T // d, d, C).transpose(1, 0, 2).reshape(T, C)


def _from_group(a, d):
    T, C = a.shape
    return a if d == 1 else a.reshape(d, T // d, C).transpose(1, 0, 2).reshape(T, C)


def _rows_view(a):
    a2 = a.reshape(-1, a.shape[-1])
    R, C = a2.shape
    tr = _tile(R, max(SUBLANES, (1 << 19) // C), SUBLANES)
    return a2, R, C, tr


def _addn(xs, out_dtype, *, name):
    shape = xs[0].shape
    x2s = [_rows_view(x)[0] for x in xs]
    _, R, C, tr = _rows_view(xs[0])
    spec = pl.BlockSpec((tr, C), lambda i: (i, 0))

    def body(*refs):
        acc = refs[0][...].astype(F32)
        for r in refs[1:-1]:
            acc = acc + r[...].astype(F32)
        refs[-1][...] = acc.astype(out_dtype)

    out = pl.pallas_call(
        body, name=name, grid=(R // tr,),
        in_specs=[spec] * len(xs), out_specs=spec,
        out_shape=jax.ShapeDtypeStruct((R, C), out_dtype),
        compiler_params=_cp(("parallel",)),
    )(*x2s)
    return out.reshape(shape)


def _addn_grouped(parts, out_dtype, *, name):
    T, C = parts[0][0].shape
    tr = _tile(T, 1024, SUBLANES)
    assert all(tr % d == 0 for _, d in parts)

    assert C % LANES == 0

    def body(*refs):
        o_ref, acc_ref = refs[-2], refs[-1]
        acc_ref[...] = jnp.zeros_like(acc_ref)
        for ref, (_, d) in zip(refs, parts):
            for r in range(d):
                for j in range(C // LANES):
                    acc_ref.at[j][pl.ds(r, tr // d, stride=d), :] += ref[r, :, j * LANES:(j + 1) * LANES]
        for j in range(C // LANES):
            o_ref[:, j * LANES:(j + 1) * LANES] = acc_ref[j].astype(out_dtype)

    return pl.pallas_call(
        body, name=name, grid=(T // tr,),
        in_specs=[pl.BlockSpec((d, tr // d, C), lambda i: (0, i, 0)) for _, d in parts],
        out_specs=pl.BlockSpec((tr, C), lambda i: (i, 0)),
        out_shape=jax.ShapeDtypeStruct((T, C), out_dtype),
        scratch_shapes=[pltpu.VMEM((C // LANES, tr, LANES), F32)],
        compiler_params=_cp(("parallel",)),
    )(*[a.reshape(d, T // d, C) for a, d in parts])


def _slot_sum(b_ref, own, my_id):
    acc = None
    for s in range(b_ref.shape[0]):
        term = jnp.where(my_id == s, own, b_ref[s].astype(F32))
        acc = term if acc is None else acc + term
    return acc


def _part_geometry(shape, axis):
    _, R, C = shape
    if axis == COL:
        return R // 2, C // N_CHIPS
    return R // (2 * N_CHIPS), C


def _sum_parts(buf, grad, axis, where, *, name):
    n, L, Rp, Cp = buf.shape
    assert (Rp, Cp) == _part_geometry(grad.shape, axis)
    tr = _tile(Rp, max(2 * SUBLANES, (1 << 18) // Cp), 2 * SUBLANES)
    nr = Rp // tr

    def own_map(l, i, w):
        if axis == COL:
            return (l, w[1] * nr + i, w[0])
        return (l, (2 * w[0] + w[1]) * nr + i, 0)

    def body(w_ref, b_ref, g_ref, o_ref):
        o_ref[...] = _slot_sum(b_ref, g_ref[...].astype(F32), w_ref[2])

    return pl.pallas_call(
        body, name=name,
        grid_spec=pltpu.PrefetchScalarGridSpec(
            num_scalar_prefetch=1, grid=(L, nr),
            in_specs=[pl.BlockSpec((n, None, tr, Cp), lambda l, i, w: (0, l, i, 0)),
                      pl.BlockSpec((None, tr, Cp), own_map)],
            out_specs=pl.BlockSpec((None, None, tr, Cp), lambda l, i, w: (l, w[1], i, 0))),
        out_shape=jax.ShapeDtypeStruct((L, 2, Rp, Cp), F32),
        compiler_params=_cp(("parallel", "parallel")),
    )(where, buf, grad)


def _sum_small(buf, own, where, *, name):
    n, R, C = buf.shape
    tr = _tile(R, 512, SUBLANES)

    def body(w_ref, b_ref, g_ref, o_ref):
        o_ref[...] = _slot_sum(b_ref, g_ref[...], w_ref[2])

    return pl.pallas_call(
        body, name=name,
        grid_spec=pltpu.PrefetchScalarGridSpec(
            num_scalar_prefetch=1, grid=(R // tr,),
            in_specs=[pl.BlockSpec((n, tr, C), lambda i, w: (0, i, 0)),
                      pl.BlockSpec((tr, C), lambda i, w: (i, 0))],
            out_specs=pl.BlockSpec((tr, C), lambda i, w: (i, 0))),
        out_shape=jax.ShapeDtypeStruct((R, C), F32),
        compiler_params=_cp(("parallel",)),
    )(where, buf, own)


def _place_shard(shard, layer, axis, where, out_dtype, *, name, after=None):
    _, R, C = shard.shape
    tr = _tile(R, max(2 * SUBLANES, (1 << 19) // C), 2 * SUBLANES)
    nr = R // tr
    if axis == COL:
        full, out_map = (1, R, C * N_CHIPS), (lambda i, w: (0, i, w[0]))
    else:
        full, out_map = (1, R * N_CHIPS, C), (lambda i, w: (0, w[0] * nr + i, 0))

    def body(w_ref, x_ref, *rest):
        rest[-1][...] = x_ref[...].astype(out_dtype)

    extra = [] if after is None else [after]
    return pl.pallas_call(
        body, name=name,
        grid_spec=pltpu.PrefetchScalarGridSpec(
            num_scalar_prefetch=1, grid=(nr,),
            in_specs=[pl.BlockSpec((None, tr, C), lambda i, w: (layer, i, 0))]
            + [pl.BlockSpec(memory_space=pl.ANY)] * len(extra),
            out_specs=pl.BlockSpec((None, tr, C), out_map)),
        out_shape=jax.ShapeDtypeStruct(full, out_dtype),
        compiler_params=_cp(("parallel",)),
    )(where, shard, *extra)


def _loss_grad(y, tgt, *, name):
    T, D = y.shape
    tr = _tile(T, 512, SUBLANES)
    row = pl.BlockSpec((tr, D), lambda i: (i, 0))

    def body(y_ref, t_ref, dy_ref, l_ref, acc_ref):
        i = pl.program_id(0)

        @pl.when(i == 0)
        def _():
            acc_ref[...] = jnp.zeros_like(acc_ref)

        err = y_ref[...] - t_ref[...]
        dy_ref[...] = err * (1.0 / D)
        acc_ref[...] += jnp.sum(err * err, axis=0, keepdims=True)

        @pl.when(i == T // tr - 1)
        def _():
            tot = jnp.sum(acc_ref[...], axis=1, keepdims=True) * (0.5 / D)
            l_ref[...] = jnp.broadcast_to(tot, (1, LANES))

    return pl.pallas_call(
        body, name=name, grid=(T // tr,),
        in_specs=[row, row],
        out_specs=[row, pl.BlockSpec((1, LANES), lambda i: (0, 0))],
        out_shape=[jax.ShapeDtypeStruct((T, D), F32), jax.ShapeDtypeStruct((1, LANES), F32)],
        scratch_shapes=[pltpu.VMEM((1, D), F32)],
        compiler_params=_cp(("arbitrary",)),
    )(y, tgt)


def _adamw(w, g, m, v, *, name):
    shape = w.shape
    w2, R, C, tr = _rows_view(w)
    spec = pl.BlockSpec((tr, C), lambda i: (i, 0))

    def body(w_ref, g_ref, m_ref, v_ref, d_ref, nm_ref, nv_ref):
        gv = g_ref[...]
        nm = ADAM_B1 * m_ref[...] + (1.0 - ADAM_B1) * gv
        nv = ADAM_B2 * v_ref[...] + (1.0 - ADAM_B2) * (gv * gv)
        m_hat = nm / (1.0 - ADAM_B1 ** ADAM_STEP)
        v_hat = nv / (1.0 - ADAM_B2 ** ADAM_STEP)
        d_ref[...] = -ADAM_LR * (m_hat / (jnp.sqrt(v_hat) + ADAM_EPS) + ADAM_WD * w_ref[...])
        nm_ref[...] = nm
        nv_ref[...] = nv

    shp = jax.ShapeDtypeStruct((R, C), F32)
    outs = pl.pallas_call(
        body, name=name, grid=(R // tr,),
        in_specs=[spec] * 4, out_specs=[spec] * 3, out_shape=[shp] * 3,
        compiler_params=_cp(("parallel",)),
    )(w2, g.reshape(R, C), m.reshape(R, C), v.reshape(R, C))
    return tuple(o.reshape(shape) for o in outs)


def _pack(arrs):
    flat = jnp.concatenate([a.reshape(-1).astype(F32) for a in arrs])
    unit = SUBLANES * LANES
    pad = (-flat.shape[0]) % unit
    return jnp.pad(flat, (0, pad)).reshape(-1, LANES)


def _unpack(buf, shapes):
    flat = buf.reshape(-1)
    out, off = [], 0
    for s in shapes:
        n = int(np.prod(s))
        out.append(flat[off:off + n].reshape(s))
        off += n
    return out


def _me():
    return lax.axis_index("x"), lax.axis_index("y"), lax.axis_index("c")


def _flip(pos, k):
    x, y, c = pos
    return (1 - x if k & 4 else x, 1 - y if k & 2 else y, 1 - c if k & 1 else c)


HBM = pl.BlockSpec(memory_space=pltpu.HBM)

COL, ROW = -1, -2


def _shard_window(ref, axis, j, n):
    start = pl.multiple_of(j * n, n)
    if axis == COL:
        return ref.at[:, :, pl.ds(start, n)]
    return ref.at[:, pl.ds(start, n), :]


GATHER_PEERS = (2, 4, 6)
SEM = pl.BlockSpec(memory_space=pltpu.SEMAPHORE)
EFFECT = pltpu.SideEffectType.DATAFLOW_SIDE_EFFECTING


def _gather_copy(buf, axis, i, pi, chip, me, send_sems, recv_sems):
    win = _shard_window(buf, axis, chip, buf.shape[axis] // N_CHIPS)
    k = len(GATHER_PEERS) * i + pi
    return pltpu.make_async_remote_copy(
        src_ref=win, dst_ref=win, send_sem=send_sems.at[k], recv_sem=recv_sems.at[k],
        device_id=_flip(me, GATHER_PEERS[pi]), device_id_type=MESH)


def _gather_start(fulls, axes, *, name):
    n = len(fulls)
    n_sem = len(GATHER_PEERS) * n

    def body(*refs):
        send_sems, recv_sems = refs[n], refs[n + 1]
        bufs = refs[n + 2:2 * n + 2]
        token = refs[2 * n + 2]
        me = _me()
        for i in range(n):
            for pi in range(len(GATHER_PEERS)):
                _gather_copy(bufs[i], axes[i], i, pi, 2 * me[0] + me[1], me, send_sems, recv_sems).start()
        token[...] = jnp.zeros_like(token)

    outs = pl.pallas_call(
        body, name=name,
        in_specs=[HBM] * n, out_specs=[SEM, SEM] + [HBM] * n + [pl.BlockSpec(memory_space=pltpu.VMEM)],
        out_shape=[pltpu.SemaphoreType.DMA((n_sem,)), pltpu.SemaphoreType.DMA((n_sem,))]
        + [pltpu.HBM(f.shape, f.dtype) for f in fulls] + [jax.ShapeDtypeStruct((SUBLANES, LANES), F32)],
        input_output_aliases={i: 2 + i for i in range(n)},
        compiler_params=pltpu.CompilerParams(has_side_effects=EFFECT),
    )(*[pltpu.with_memory_space_constraint(f, pltpu.HBM) for f in fulls])
    return outs[0], outs[1], list(outs[2:2 + n]), outs[-1]


def _gather_wait(send_sems, recv_sems, bufs, axes, idxs, after, *, name):
    m = len(bufs)

    def body(*refs):
        ss, rs = refs[m], refs[m + 1]
        outs = refs[m + 3:]
        me = _me()
        for t, i in enumerate(idxs):
            for pi, k in enumerate(GATHER_PEERS):
                px, py, _ = _flip(me, k)
                _gather_copy(outs[t], axes[t], i, pi, 2 * me[0] + me[1], me, ss, rs).wait_send()
                _gather_copy(outs[t], axes[t], i, pi, 2 * px + py, me, ss, rs).wait_recv()

    return pl.pallas_call(
        body, name=name,
        in_specs=[HBM] * m + [SEM, SEM, pl.BlockSpec(memory_space=pl.ANY)], out_specs=[HBM] * m,
        out_shape=[pltpu.HBM(b.shape, b.dtype) for b in bufs],
        input_output_aliases={t: t for t in range(m)},
        compiler_params=pltpu.CompilerParams(has_side_effects=EFFECT),
    )(*bufs, send_sems, recv_sems, after)


def _grad_part(ref, layer, axis, pos):
    px, py, pc = pos
    Rp, Cp = _part_geometry(ref.shape, axis)
    chip = 2 * px + py
    if axis == COL:
        return ref.at[layer, pl.ds(pl.multiple_of(pc * Rp, Rp), Rp), pl.ds(pl.multiple_of(chip * Cp, Cp), Cp)]
    return ref.at[layer, pl.ds(pl.multiple_of((2 * chip + pc) * Rp, Rp), Rp), :]


def _scatter_copy(grad, slots, layer, axis, i, k, src_pos, dst_slot, me, send_sems, recv_sems):
    sem = (N_DEV - 1) * i + k - 1
    return pltpu.make_async_remote_copy(
        src_ref=_grad_part(grad, layer, axis, src_pos), dst_ref=slots.at[dst_slot, layer],
        send_sem=send_sems.at[sem], recv_sem=recv_sems.at[sem],
        device_id=_flip(me, k), device_id_type=MESH)


def _scatter_start(grads, slots, layers, axes, *, name):
    n = len(grads)
    n_sem = (N_DEV - 1) * n

    def body(*refs):
        send_sems, recv_sems = refs[2 * n], refs[2 * n + 1]
        g_refs, s_refs = refs[2 * n + 2:3 * n + 2], refs[3 * n + 2:4 * n + 2]
        token = refs[4 * n + 2]
        me = _me()
        my_id = 4 * me[0] + 2 * me[1] + me[2]
        for i in range(n):
            for k in range(1, N_DEV):
                _scatter_copy(g_refs[i], s_refs[i], layers[i], axes[i], i, k, _flip(me, k), my_id, me,
                              send_sems, recv_sems).start()
        token[...] = jnp.zeros_like(token)

    arrays = list(grads) + list(slots)
    outs = pl.pallas_call(
        body, name=name,
        in_specs=[HBM] * (2 * n),
        out_specs=[SEM, SEM] + [HBM] * (2 * n) + [pl.BlockSpec(memory_space=pltpu.VMEM)],
        out_shape=[pltpu.SemaphoreType.DMA((n_sem,)), pltpu.SemaphoreType.DMA((n_sem,))]
        + [pltpu.HBM(a.shape, a.dtype) for a in arrays] + [jax.ShapeDtypeStruct((SUBLANES, LANES), F32)],
        input_output_aliases={i: 2 + i for i in range(2 * n)},
        compiler_params=pltpu.CompilerParams(has_side_effects=EFFECT),
    )(*[pltpu.with_memory_space_constraint(a, pltpu.HBM) for a in arrays])
    return outs[0], outs[1], list(outs[2:2 + n]), list(outs[2 + n:2 + 2 * n]), outs[-1]


def _scatter_wait(sems, grads, slots, axes, plan, after, *, name):
    n = len(grads)
    flat_sems = [s for pair in sems for s in pair]

    def body(*refs):
        sem_refs = refs[2 * n:2 * n + len(flat_sems)]
        first_out = 2 * n + len(flat_sems) + 1
        g_refs, s_refs = refs[first_out:first_out + n], refs[first_out + n:]
        me = _me()
        for j, items in enumerate(plan):
            ss, rs = sem_refs[2 * j], sem_refs[2 * j + 1]
            for i, (a, layer) in enumerate(items):
                for k in range(1, N_DEV):
                    px, py, pc = _flip(me, k)
                    cp = _scatter_copy(g_refs[a], s_refs[a], layer, axes[a], i, k, me, 4 * px + 2 * py + pc, me, ss, rs)
                    cp.wait_send()
                    cp.wait_recv()

    arrays = list(grads) + list(slots)
    outs = pl.pallas_call(
        body, name=name,
        in_specs=[HBM] * (2 * n) + [SEM] * len(flat_sems) + [pl.BlockSpec(memory_space=pl.ANY)],
        out_specs=[HBM] * (2 * n),
        out_shape=[pltpu.HBM(a.shape, a.dtype) for a in arrays],
        input_output_aliases={i: i for i in range(2 * n)},
        compiler_params=pltpu.CompilerParams(has_side_effects=EFFECT),
    )(*arrays, *flat_sems, after)
    return list(outs[:n]), list(outs[n:])


def _small_copy(src, dst, k, slot, me, send_sems, recv_sems):
    return pltpu.make_async_remote_copy(
        src_ref=src, dst_ref=dst.at[slot], send_sem=send_sems.at[k - 1], recv_sem=recv_sems.at[k - 1],
        device_id=_flip(me, k), device_id_type=MESH)


def _small_start(small, slots, after, *, name):
    def body(src_in, dst_in, after_ref, send_sems, recv_sems, src, dst):
        me = _me()
        for k in range(1, N_DEV):
            _small_copy(src, dst, k, 4 * me[0] + 2 * me[1] + me[2], me, send_sems, recv_sems).start()

    arrays = [small, slots]
    return pl.pallas_call(
        body, name=name, in_specs=[HBM, HBM, pl.BlockSpec(memory_space=pl.ANY)], out_specs=[SEM, SEM, HBM, HBM],
        out_shape=[pltpu.SemaphoreType.DMA((N_DEV - 1,)), pltpu.SemaphoreType.DMA((N_DEV - 1,))]
        + [pltpu.HBM(a.shape, a.dtype) for a in arrays],
        input_output_aliases={0: 2, 1: 3},
        compiler_params=pltpu.CompilerParams(has_side_effects=EFFECT),
    )(*[pltpu.with_memory_space_constraint(a, pltpu.HBM) for a in arrays], after)


def _small_wait(send_sems, recv_sems, small, slots, after, *, name):
    def body(src_in, dst_in, ss, rs, after_ref, src, dst):
        me = _me()
        for k in range(1, N_DEV):
            px, py, pc = _flip(me, k)
            cp = _small_copy(src, dst, k, 4 * px + 2 * py + pc, me, ss, rs)
            cp.wait_send()
            cp.wait_recv()

    return pl.pallas_call(
        body, name=name, in_specs=[HBM, HBM, SEM, SEM, pl.BlockSpec(memory_space=pl.ANY)], out_specs=[HBM, HBM],
        out_shape=[pltpu.HBM(small.shape, small.dtype), pltpu.HBM(slots.shape, slots.dtype)],
        input_output_aliases={0: 0, 1: 1},
        compiler_params=pltpu.CompilerParams(has_side_effects=EFFECT),
    )(small, slots, send_sems, recv_sems, after)


def _pair_gather(halves, *, name):
    n = len(halves)

    def body(*refs):
        bufs = refs[n:2 * n]
        send_sems, recv_sems = refs[2 * n:]
        me = _me()
        c = me[2]
        sib = _flip(me, 1)

        def copy(i, half):
            return pltpu.make_async_remote_copy(
                src_ref=bufs[i].at[:, half], dst_ref=bufs[i].at[:, half], send_sem=send_sems.at[i],
                recv_sem=recv_sems.at[i], device_id=sib, device_id_type=MESH)

        sends = [copy(i, c) for i in range(n)]
        for cp in sends:
            cp.start()
        for i in range(n):
            copy(i, 1 - c).wait_recv()
        for cp in sends:
            cp.wait_send()

    return pl.pallas_call(
        body, name=name,
        in_specs=[HBM] * n, out_specs=[HBM] * n,
        out_shape=[jax.ShapeDtypeStruct(h.shape, h.dtype) for h in halves],
        input_output_aliases={i: i for i in range(n)},
        scratch_shapes=[pltpu.SemaphoreType.DMA((n,)), pltpu.SemaphoreType.DMA((n,))],
    )(*halves)


def _local_step(x, tgt, P, fetch, ship, n_layers):
    T, D = x.shape
    P = dict(P)
    W = {}
    depth = P["ffn_conv_b"].shape[0]
    n_a = P["a_w_s"].shape[0]
    alpha = (2 * depth) ** 0.25
    H = P["rel_table"].shape[1] // N_GROUPS
    HD = H * HEAD_DIM
    dils = [d for _, d in DILATED_GROUPS]
    assert T % (dils[-1] * BLK) == 0

    def arrive(stage, after):
        w_new, p_new = fetch(stage, after)
        W.update(w_new)
        P.update(p_new)

    bias = _band_bias(P["rel_table"], H, name="band_bias")
    saved = []
    xf, xb = x, x.astype(BF16)
    kg = vg = None
    for i in range(depth):
        s = {"x": xf, "xb": xb}
        arrive(f"{i}a", xf)
        if i < n_a:
            s["zp"] = _mm(xb, W["a_w_in", i], out_dtype=BF16, name=f"a{i}_in")
            arrive(f"{i}b", s["zp"])
            s["y"] = _sgu_fwd(s["zp"], P["a_w_s"][i], P["a_b_s"][i], P["a_ln_g"][i], P["a_ln_b"][i], name=f"a{i}_sgu")
            s["h"] = _mm(s["y"], W["a_w_out", i], name=f"a{i}_out")
        else:
            j = i - n_a
            if j == 0:
                kv = _mm(xb, W["kv_w", 0], out_dtype=BF16, name="kv_proj")
                kg = [(kv, 0) if d == 1 else (_to_group(kv[:, :HD], d), 0) for d in dils]
                vg = [(kv, 1) if d == 1 else (_to_group(kv[:, HD:], d), 0) for d in dils]
            q = _mm(xb, W["b_w_q", j], out_dtype=BF16, name=f"b{j}_q")
            s["qg"] = [(q, g) if d == 1 else (_to_group(q[:, g * HD:(g + 1) * HD], d), 0) for g, d in enumerate(dils)]
            os, ls = [], []
            for g, d in enumerate(dils):
                o_g, l_g = _attn_fwd(s["qg"][g], kg[g], vg[g], bias, g, d, name=f"b{j}_attn{g}")
                os.append(o_g)
                ls.append(l_g)
            s["oc"], s["ob"], s["lse"] = _attn_combine(os, ls, dils, name=f"b{j}_comb")
            arrive(f"{i}b", s["ob"])
            s["h"] = _mm(s["ob"], W["b_w_o", j], name=f"b{j}_o")
        s["x1"], s["x1b"] = _add_ln(xf, s["h"], P["ln_g"][i, 0], P["ln_b"][i, 0], alpha, name=f"l{i}_ln1")
        arrive(f"{i}c", s["x1b"])
        s["hup"] = _mm(s["x1b"], W["ffn_w_up", i], out_dtype=BF16, name=f"l{i}_up")
        arrive(f"{i}d", s["hup"])
        s["act"], s["ca"], s["cg"] = _convglu_fwd(s["hup"], P["ffn_conv_w"][i], P["ffn_conv_b"][i], name=f"l{i}_glu")
        s["f"] = _mm(s["act"], W["ffn_w_down", i], name=f"l{i}_down")
        xf, xb = _add_ln(s["x1"], s["f"], P["ln_g"][i, 1], P["ln_b"][i, 1], alpha, name=f"l{i}_ln2")
        saved.append(s)

    G, loss_row = _loss_grad(xf, tgt, name="loss")

    gw = {}

    def dw(key, layer, a, b, name, **kw):
        gw[key] = _mm(a, b, ta=True, out_dtype=BF16, out_into=(gw.get(key), n_layers[key], layer), name=name, **kw)

    gp = {k: [None] * n_a for k in ("a_ln_g", "a_ln_b", "a_w_s", "a_b_s")}
    gp.update({k: [None] * depth for k in ("ffn_conv_w", "ffn_conv_b", "ln_g", "ln_b")})
    dk_parts, dv_parts, dbias_parts = [], [], []
    for i in reversed(range(depth)):
        s = saved[i]
        dr2, dr2b, dg2, db2 = _ln_bwd(G, s["x1"], s["f"], P["ln_g"][i, 1], alpha, name=f"l{i}_ln2_bwd")
        dw("ffn_w_down", i, s["act"], dr2b, f"l{i}_down_dw")
        dact = _mm(dr2b, W["ffn_w_down", i], tb=True, out_dtype=BF16, name=f"l{i}_down_dx")
        dhup, dcw, dcb = _convglu_bwd(s["hup"], s["ca"], s["cg"], dact, P["ffn_conv_w"][i], name=f"l{i}_glu_bwd")
        gp["ffn_conv_w"][i] = dcw.transpose(1, 0, 2).reshape(dcw.shape[1], -1)
        gp["ffn_conv_b"][i] = dcb.reshape(-1)
        dw("ffn_w_up", i, s["x1b"], dhup, f"l{i}_up_dw", b_halves=True)
        token = ship(gw, [("ffn_w_down", i), ("ffn_w_up", i)])
        G1 = _mm(dhup, W["ffn_w_up", i], tb=True, a_halves=True, add=dr2, add_scale=alpha, after=token,
                 name=f"l{i}_up_dx")
        dr1, dr1b, dg1, db1 = _ln_bwd(G1, s["x"], s["h"], P["ln_g"][i, 0], alpha, name=f"l{i}_ln1_bwd")
        gp["ln_g"][i] = jnp.concatenate([dg1, dg2], axis=0)
        gp["ln_b"][i] = jnp.concatenate([db1, db2], axis=0)
        if i < n_a:
            dw("a_w_out", i, s["y"], dr1b, f"a{i}_out_dw")
            token = ship(gw, [("a_w_out", i)])
            dy = _mm(dr1b, W["a_w_out", i], tb=True, out_dtype=BF16, after=token, name=f"a{i}_out_dx")
            dzp, dws, dbs, dlg, dlb = _sgu_bwd(s["zp"], dy, P["a_w_s"][i], P["a_b_s"][i], P["a_ln_g"][i],
                                               P["a_ln_b"][i], name=f"a{i}_sgu_bwd")
            gp["a_w_s"][i], gp["a_b_s"][i] = dws, dbs[:, :dws.shape[0]].T
            gp["a_ln_g"][i], gp["a_ln_b"][i] = dlg[0], dlb[0]
            dw("a_w_in", i, s["xb"], dzp, f"a{i}_in_dw")
            token = ship(gw, [("a_w_in", i)])
            G = _mm(dzp, W["a_w_in", i], tb=True, add=dr1, add_scale=alpha, after=token, name=f"a{i}_in_dx")
        else:
            j = i - n_a
            dw("b_w_o", j, s["ob"], dr1b, f"b{j}_o_dw")
            do_tok = _mm(dr1b, W["b_w_o", j], tb=True, name=f"b{j}_o_dx")
            deltas, dobs = _attn_bwd_prep(do_tok, s["oc"], dils, name=f"b{j}_prep")
            dqs, dbs = [], []
            for g, d in enumerate(dils):
                dq, dk, dv, db = _attn_bwd(s["qg"][g], kg[g], vg[g], bias, g, d, dobs[g], s["lse"][g], deltas[g],
                                           name=f"b{j}_attn{g}_bwd")
                dqs.append((dq, d))
                dk_parts.append((dk, d))
                dv_parts.append((dv, d))
                dbs.append(db)
            dbias_parts.append(jnp.stack(dbs))
            dq_tok = _regroup_to_tokens(dqs, BF16, name=f"b{j}_dq_tokens")
            dw("b_w_q", j, s["xb"], dq_tok, f"b{j}_q_dw")
            token = ship(gw, [("b_w_o", j), ("b_w_q", j)])
            G = _mm(dq_tok, W["b_w_q", j], tb=True, add=dr1, add_scale=alpha, after=token, name=f"b{j}_q_dx")
            if j == 0:
                dkv = jnp.concatenate([_addn_grouped(dk_parts, BF16, name="dk_sum"),
                                       _addn_grouped(dv_parts, BF16, name="dv_sum")], axis=1)
                dw("kv_w", 0, s["xb"], dkv, "kv_dw")
                token = ship(gw, [("kv_w", 0)])
                G = _mm(dkv, W["kv_w", 0], tb=True, add=G, add_scale=1.0, after=token, name="kv_dx")
    rel = _rel_grad(dbias_parts, name="rel_grad")
    grel = rel[:, :, :H].transpose(1, 0, 2).reshape(REL_BUCKETS, N_GROUPS * H)
    gsmall = {k: jnp.stack(v) for k, v in gp.items()}
    gsmall["rel_table"] = grel
    return loss_row, G, gw, gsmall


BIG = (("a_w_in", COL), ("a_w_out", ROW), ("kv_w", ROW), ("b_w_q", COL), ("b_w_o", COL),
       ("ffn_w_up", COL), ("ffn_w_down", ROW))
SMALL_SHARDED = ("a_ln_g", "a_ln_b", "ffn_conv_w", "ln_g", "ln_b")
SMALL_REPLICATED = ("a_w_s", "a_b_s", "rel_table", "ffn_conv_b")
WEIGHTS = ("a_w_in", "a_ln_g", "a_ln_b", "a_w_s", "a_b_s", "a_w_out", "kv_w", "b_w_q", "b_w_o", "rel_table",
           "ffn_w_up", "ffn_conv_w", "ffn_conv_b", "ffn_w_down", "ln_g", "ln_b")


def kernel(x, a_w_in, a_ln_g, a_ln_b, a_w_s, a_b_s, a_w_out, kv_w, b_w_q, b_w_o, rel_table, ffn_w_up, ffn_conv_w, ffn_conv_b, ffn_w_down, ln_g, ln_b, loss_target, m_a_w_in, m_a_ln_g, m_a_ln_b, m_a_w_s, m_a_b_s, m_a_w_out, m_kv_w, m_b_w_q, m_b_w_o, m_rel_table, m_ffn_w_up, m_ffn_conv_w, m_ffn_conv_b, m_ffn_w_down, m_ln_g, m_ln_b, v_a_w_in, v_a_ln_g, v_a_ln_b, v_a_w_s, v_a_b_s, v_a_w_out, v_kv_w, v_b_w_q, v_b_w_o, v_rel_table, v_ffn_w_up, v_ffn_conv_w, v_ffn_conv_b, v_ffn_w_down, v_ln_g, v_ln_b):
    w = dict(a_w_in=a_w_in, a_ln_g=a_ln_g, a_ln_b=a_ln_b, a_w_s=a_w_s, a_b_s=a_b_s, a_w_out=a_w_out, kv_w=kv_w,
             b_w_q=b_w_q, b_w_o=b_w_o, rel_table=rel_table, ffn_w_up=ffn_w_up, ffn_conv_w=ffn_conv_w,
             ffn_conv_b=ffn_conv_b, ffn_w_down=ffn_w_down, ln_g=ln_g, ln_b=ln_b)
    m = dict(a_w_in=m_a_w_in, a_ln_g=m_a_ln_g, a_ln_b=m_a_ln_b, a_w_s=m_a_w_s, a_b_s=m_a_b_s, a_w_out=m_a_w_out,
             kv_w=m_kv_w, b_w_q=m_b_w_q, b_w_o=m_b_w_o, rel_table=m_rel_table, ffn_w_up=m_ffn_w_up,
             ffn_conv_w=m_ffn_conv_w, ffn_conv_b=m_ffn_conv_b, ffn_w_down=m_ffn_w_down, ln_g=m_ln_g, ln_b=m_ln_b)
    v = dict(a_w_in=v_a_w_in, a_ln_g=v_a_ln_g, a_ln_b=v_a_ln_b, a_w_s=v_a_w_s, a_b_s=v_a_b_s, a_w_out=v_a_w_out,
             kv_w=v_kv_w, b_w_q=v_b_w_q, b_w_o=v_b_w_o, rel_table=v_rel_table, ffn_w_up=v_ffn_w_up,
             ffn_conv_w=v_ffn_conv_w, ffn_conv_b=v_ffn_conv_b, ffn_w_down=v_ffn_w_down, ln_g=v_ln_g, ln_b=v_ln_b)
    chip = 2 * lax.axis_index("x") + lax.axis_index("y")

    big_names = [n for n, _ in BIG]
    big_axes = [a for _, a in BIG]
    where = jnp.stack([chip, lax.axis_index("c"), 2 * chip + lax.axis_index("c")]).astype(jnp.int32)
    shards = [w[n].reshape((1,) + w[n].shape) if w[n].ndim == 2 else w[n] for n in big_names]
    small_shapes = [w[n].shape for n in SMALL_SHARDED]
    small = _pack([w[n] for n in SMALL_SHARDED])
    shard_of = dict(zip(big_names, shards))
    axis_of = dict(BIG)
    n_layers = {n: s.shape[0] for n, s in shard_of.items()}
    n_a = w["a_w_s"].shape[0]
    stages = {}
    for i in range(w["ffn_w_up"].shape[0]):
        if i < n_a:
            stages[f"{i}a"], stages[f"{i}b"] = [("a_w_in", i)], [("a_w_out", i)]
        else:
            stages[f"{i}a"] = ([("kv_w", 0)] if i == n_a else []) + [("b_w_q", i - n_a)]
            stages[f"{i}b"] = [("b_w_o", i - n_a)]
        stages[f"{i}c"], stages[f"{i}d"] = [("ffn_w_up", i)], [("ffn_w_down", i)]
    stages["0a"].append(("small", 0))
    assert sum(len(st) for st in stages.values()) == sum(n_layers.values()) + 1
    flights, full_shape, all_started = {}, {}, None
    for batch, names in enumerate(([st for st in stages if st[0] == "0"], [st for st in stages if st[0] != "0"])):
        order = [key for st in names for key in stages[st]]
        placed, axes = [], []
        for n, l in order:
            if n == "small":
                placed.append(_place_shard(small[None], 0, ROW, where, F32, name="place_small"))
                axes.append(ROW)
            else:
                placed.append(_place_shard(shard_of[n], l, axis_of[n], where, BF16, name=f"place_{n}{l}",
                                           after=all_started))
                axes.append(axis_of[n])
                full_shape[n] = (n_layers[n],) + placed[-1].shape[1:]
        send_sems, recv_sems, flying, all_started = _gather_start(placed, axes, name=f"gather_start{batch}")
        for st in names:
            flights[st] = (send_sems, recv_sems, flying, axes, order)

    def fetch(stage, after):
        if stage not in stages:
            return {}, {}
        send_sems, recv_sems, flying, axes, order = flights[stage]
        if stage == "0a":
            after = all_started
        idxs = [order.index(key) for key in stages[stage]]
        landed = _gather_wait(send_sems, recv_sems, [flying[i] for i in idxs], [axes[i] for i in idxs], idxs, after,
                              name=f"gather_wait_{stage}")
        W_new, P_new = {}, {}
        for key, arr in zip(stages[stage], landed):
            if key[0] == "small":
                small_all = arr.reshape((N_CHIPS,) + small.shape)
                per_chip = [_unpack(small_all[j], small_shapes) for j in range(N_CHIPS)]
                for i, n in enumerate(SMALL_SHARDED):
                    P_new[n] = jnp.concatenate([per_chip[j][i] for j in range(N_CHIPS)], axis=-1)
            else:
                W_new[key] = arr.reshape(arr.shape[1:])
        return W_new, P_new

    slots ={n: lax.empty((N_DEV, n_layers[n]) + _part_geometry(full_shape[n], axis_of[n]), BF16) for n in big_names}
    sems, plan, tokens = [], [], []

    def ship(gw, items):
        names = [n for n, _ in items]
        send, recv, grads, landing, token = _scatter_start(
            [gw[n] for n in names], [slots[n] for n in names], [l for _, l in items], [axis_of[n] for n in names],
            name="scatter_start_" + "_".join(f"{n}{l}" for n, l in items))
        for n, g, s in zip(names, grads, landing):
            gw[n], slots[n] = g, s
        sems.append((send, recv))
        plan.append([(big_names.index(n), l) for n, l in items])
        tokens.append(token)
        return token

    loss_row, grad_x, gw, gsmall = _local_step(x[0], loss_target[0], {n: w[n] for n in SMALL_REPLICATED}, fetch,
                                               ship, n_layers)

    small_names = list(SMALL_SHARDED) + list(SMALL_REPLICATED)
    small_pack = _pack([gsmall[n] for n in small_names] + [loss_row[0, :1]])
    small_sems = _small_start(small_pack, lax.empty((N_DEV,) + small_pack.shape, F32), tokens[-1],
                              name="small_start")
    own, landed = _scatter_wait(sems, [gw[n] for n in big_names], [slots[n] for n in big_names], big_axes, plan,
                                grad_x, name="scatter_wait")
    sums = [_sum_parts(b, g, ax, where, name=f"sum_{n}") for (n, ax), b, g in zip(BIG, landed, own)]
    pairs = _pair_gather(sums, name="pair_gather")
    grad = {n: p.reshape(w[n].shape) for n, p in zip(big_names, pairs)}

    delta, new_m, new_v = {}, {}, {}
    for n, _ in BIG:
        delta[n], new_m[n], new_v[n] = _adamw(w[n], grad[n], m[n], v[n], name=f"adamw_{n}")
    small_own, small_slots = _small_wait(*small_sems, new_v[big_names[-1]], name="small_wait")
    small_sum = _sum_small(small_slots, small_own, where, name="sum_small")
    full_small = _unpack(small_sum, [gsmall[n].shape for n in small_names] + [(1,)])
    loss = full_small[-1][0]
    for n, gfull in zip(small_names, full_small):
        if n in SMALL_SHARDED:
            width = w[n].shape[-1]
            grad[n] = lax.dynamic_slice_in_dim(gfull, chip * width, width, axis=-1)
        else:
            grad[n] = gfull
    shapes = [w[n].shape for n in small_names]
    packed = _adamw(_pack([w[n] for n in small_names]), _pack([grad[n] for n in small_names]),
                    _pack([m[n] for n in small_names]), _pack([v[n] for n in small_names]), name="adamw_small")
    for out, res in zip((delta, new_m, new_v), packed):
        for n, a in zip(small_names, _unpack(res, shapes)):
            out[n] = a

    return (loss, grad_x[None], *[grad[n] for n in WEIGHTS], *[delta[n] for n in WEIGHTS],
            *[new_m[n] for n in WEIGHTS], *[new_v[n] for n in WEIGHTS])
```

```python
import functools
import math

import numpy as np
import jax
import jax.numpy as jnp
from jax import lax
from jax.experimental import pallas as pl
from jax.experimental.pallas import tpu as pltpu

F32 = jnp.float32
BF16 = jnp.bfloat16
MESH = pl.DeviceIdType.MESH

CHUNK = 128
HEAD_DIM = 64
DILATED_GROUPS = ((128, 1), (512, 4), (2048, 16))
N_GROUPS = len(DILATED_GROUPS)
BLK = 128
REL_BUCKETS = 32
REL_MAX_DIST = 2048
LN_EPS = 1e-5
NEG = -1e30
ADAM_LR = 0.001
ADAM_B1 = 0.9
ADAM_B2 = 0.999
ADAM_EPS = 1e-08
ADAM_WD = 0.01
ADAM_STEP = 10

N_CHIPS = 4
N_DEV = 8
LANES = 128
SUBLANES = 8
VMEM_LIMIT = 48 * 1024 * 1024

_SQRT_HALF = 0.7071067811865476
_INV_SQRT_2PI = 0.3989422804014327


def _cp(sem=None, vmem=VMEM_LIMIT):
    return pltpu.CompilerParams(dimension_semantics=sem, vmem_limit_bytes=vmem)


def _tile(dim, target, align=LANES):
    if dim <= target:
        return dim
    t = (target // align) * align
    while t >= align:
        if dim % t == 0:
            return t
        t -= align
    return dim


def _gelu(x):
    return 0.5 * x * (1.0 + lax.erf(x * _SQRT_HALF))


def _gelu_grad(x):
    return 0.5 * (1.0 + lax.erf(x * _SQRT_HALF)) + x * (_INV_SQRT_2PI * jnp.exp(-0.5 * x * x))


def _dot(a, b, ca, cb):
    return lax.dot_general(a, b, (((ca,), (cb,)), ((), ())), preferred_element_type=F32)


def _mn_tile(dim):
    return max(_tile(dim, 1024), _tile(dim, 1408))


def _mm(a, b, *, name, ta=False, tb=False, out_dtype=F32, tiles=None,
        a_lead=(), b_lead=(), a_halves=False, b_halves=False, add=None, add_scale=1.0, out_into=None, after=None):
    a2, b2 = a.shape[-2:], b.shape[-2:]
    M, K = (a2[1], a2[0]) if ta else a2
    Kb, N = (b2[1], b2[0]) if tb else b2
    if a_halves:
        K = 2 * K
    if b_halves:
        N = 2 * N
    assert K == Kb, (name, a.shape, b.shape)
    if tiles is None:
        k_target = 1408 if a_halves else (K if K <= 2816 else 2048)
        tm, tn = _mn_tile(M), _mn_tile(N // 2 if b_halves else N)
        tk = _tile(K // 2 if a_halves else K, k_target)
    else:
        tm = _tile(M, tiles[0])
        tn = _tile(N // 2 if b_halves else N, tiles[1])
        tk = _tile(K // 2 if a_halves else K, tiles[2])
    nm, nn, nk = M // tm, N // tn, K // tk
    nkh, nnh = nk // 2, nn // 2

    def lead(idx, rest):
        return tuple(idx) + tuple(rest)

    sq_a = (None,) * (a.ndim - 2)
    sq_b = (None,) * (b.ndim - 2)
    if a_halves:
        assert not ta and a.ndim == 3
        a_spec = pl.BlockSpec((None, tm, tk), lambda i, j, k: (k // nkh, i, k % nkh))
    elif ta:
        a_spec = pl.BlockSpec(sq_a + (tk, tm), lambda i, j, k: lead(a_lead, (k, i)))
    else:
        a_spec = pl.BlockSpec(sq_a + (tm, tk), lambda i, j, k: lead(a_lead, (i, k)))
    if b_halves:
        assert not tb and b.ndim == 3
        b_spec = pl.BlockSpec((None, tk, tn), lambda i, j, k: (j // nnh, k, j % nnh))
    elif tb:
        b_spec = pl.BlockSpec(sq_b + (tn, tk), lambda i, j, k: lead(b_lead, (j, k)))
    else:
        b_spec = pl.BlockSpec(sq_b + (tk, tn), lambda i, j, k: lead(b_lead, (k, j)))
    mn_spec = pl.BlockSpec((tm, tn), lambda i, j, k: (i, j))
    in_specs = [a_spec, b_spec]
    args = [a, b]
    if add is not None:
        in_specs.append(mn_spec)
        args.append(add)
    aliases = {}
    if out_into is None:
        o_spec, out_shape = mn_spec, jax.ShapeDtypeStruct((M, N), out_dtype)
    else:
        buf, n_layers, layer = out_into
        o_spec = pl.BlockSpec((None, tm, tn), lambda i, j, k: (layer, i, j))
        out_shape = jax.ShapeDtypeStruct((n_layers, M, N), out_dtype)
        if buf is not None:
            aliases = {len(args): 0}
            in_specs.append(pl.BlockSpec(memory_space=pl.ANY))
            args.append(buf)
    if after is not None:
        in_specs.append(pl.BlockSpec(memory_space=pl.ANY))
        args.append(after)
    n_in = len(args)
    ca, cb = (0 if ta else 1), (1 if tb else 0)

    def body(*refs):
        a_ref, b_ref = refs[:2]
        add_ref = refs[2] if add is not None else None
        o_ref = refs[n_in]

        def finish(acc):
            if add_ref is not None:
                acc = acc + add_scale * add_ref[...]
            o_ref[...] = acc.astype(o_ref.dtype)

        if nk == 1:
            finish(_dot(a_ref[...], b_ref[...], ca, cb))
        else:
            acc_ref = refs[-1]
            k = pl.program_id(2)

            @pl.when(k == 0)
            def _():
                acc_ref[...] = jnp.zeros_like(acc_ref)

            acc_ref[...] += _dot(a_ref[...], b_ref[...], ca, cb)

            @pl.when(k == nk - 1)
            def _():
                finish(acc_ref[...])

    return pl.pallas_call(
        body, name=name, grid=(nm, nn, nk),
        in_specs=in_specs, out_specs=o_spec, out_shape=out_shape,
        input_output_aliases=aliases,
        scratch_shapes=[] if nk == 1 else [pltpu.VMEM((tm, tn), F32)],
        compiler_params=_cp(("parallel", "parallel", "arbitrary")),
    )(*args)


def _ln_stats(r):
    mu = jnp.mean(r, axis=-1, keepdims=True)
    xc = r - mu
    var = jnp.mean(xc * xc, axis=-1, keepdims=True)
    rstd = lax.rsqrt(var + LN_EPS)
    return xc * rstd, rstd


def _add_ln(x, h, g, b, alpha, *, name):
    T, D = x.shape
    tr = _tile(T, 1024, SUBLANES)
    row = pl.BlockSpec((tr, D), lambda i: (i, 0))
    vec = pl.BlockSpec((1, D), lambda i: (0, 0))

    def body(x_ref, h_ref, g_ref, b_ref, o_ref, ob_ref):
        xhat, _ = _ln_stats(alpha * x_ref[...] + h_ref[...])
        y = xhat * g_ref[...] + b_ref[...]
        o_ref[...] = y
        ob_ref[...] = y.astype(BF16)

    return pl.pallas_call(
        body, name=name, grid=(T // tr,),
        in_specs=[row, row, vec, vec], out_specs=[row, row],
        out_shape=[jax.ShapeDtypeStruct((T, D), F32), jax.ShapeDtypeStruct((T, D), BF16)],
        compiler_params=_cp(("parallel",)),
    )(x, h, g.reshape(1, D), b.reshape(1, D))


def _ln_bwd(dy, x, h, g, alpha, *, name):
    T, D = x.shape
    tr = _tile(T, 512, SUBLANES)
    row = pl.BlockSpec((tr, D), lambda i: (i, 0))
    vec = pl.BlockSpec((1, D), lambda i: (0, 0))

    def body(dy_ref, x_ref, h_ref, g_ref, dr_ref, drb_ref, dg_ref, db_ref):
        @pl.when(pl.program_id(0) == 0)
        def _():
            dg_ref[...] = jnp.zeros_like(dg_ref)
            db_ref[...] = jnp.zeros_like(db_ref)

        xhat, rstd = _ln_stats(alpha * x_ref[...] + h_ref[...])
        dyv = dy_ref[...]
        dxhat = dyv * g_ref[...]
        c1 = jnp.mean(dxhat, axis=-1, keepdims=True)
        c2 = jnp.mean(dxhat * xhat, axis=-1, keepdims=True)
        dr = rstd * (dxhat - c1 - xhat * c2)
        dr_ref[...] = dr
        drb_ref[...] = dr.astype(BF16)
        dg_ref[...] += jnp.sum(dyv * xhat, axis=0, keepdims=True)
        db_ref[...] += jnp.sum(dyv, axis=0, keepdims=True)

    return pl.pallas_call(
        body, name=name, grid=(T // tr,),
        in_specs=[row, row, row, vec], out_specs=[row, row, vec, vec],
        out_shape=[jax.ShapeDtypeStruct((T, D), F32), jax.ShapeDtypeStruct((T, D), BF16),
                   jax.ShapeDtypeStruct((1, D), F32), jax.ShapeDtypeStruct((1, D), F32)],
        compiler_params=_cp(("arbitrary",)),
    )(dy, x, h, g.reshape(1, D))


def _sgu_pieces(zp, lg, lb, E):
    z = _gelu(zp)
    u, v = z[:, :E], z[:, E:]
    vhat, rstd = _ln_stats(v)
    vn = vhat * lg + lb
    return u, vhat, rstd, vn


def _tril_mask():
    t = lax.broadcasted_iota(jnp.int32, (CHUNK, CHUNK), 0)
    s = lax.broadcasted_iota(jnp.int32, (CHUNK, CHUNK), 1)
    return s <= t


def _sgu_fwd(zp, ws, bs, lg, lb, *, name):
    T, E2 = zp.shape
    E = E2 // 2
    G = ws.shape[0]
    cw = E // G

    def body(zp_ref, ws_ref, bs_ref, lg_ref, lb_ref, y_ref):
        u, _, _, vn = _sgu_pieces(zp_ref[...].astype(F32), lg_ref[...], lb_ref[...], E)
        vnb = vn.astype(BF16)
        tril = _tril_mask()
        for g in range(G):
            cols = slice(g * cw, (g + 1) * cw)
            w = jnp.where(tril, ws_ref[g], 0.0).astype(BF16)
            sv = _dot(w, vnb[:, cols], 1, 0) + bs_ref[g]
            y_ref[:, cols] = (u[:, cols] * sv).astype(BF16)

    return pl.pallas_call(
        body, name=name, grid=(T // CHUNK,),
        in_specs=[pl.BlockSpec((CHUNK, E2), lambda n: (n, 0)),
                  pl.BlockSpec((G, CHUNK, CHUNK), lambda n: (0, 0, 0)),
                  pl.BlockSpec((G, CHUNK, 1), lambda n: (0, 0, 0)),
                  pl.BlockSpec((1, E), lambda n: (0, 0)),
                  pl.BlockSpec((1, E), lambda n: (0, 0))],
        out_specs=pl.BlockSpec((CHUNK, E), lambda n: (n, 0)),
        out_shape=jax.ShapeDtypeStruct((T, E), BF16),
        compiler_params=_cp(("parallel",)),
    )(zp, ws, bs.reshape(G, CHUNK, 1), lg.reshape(1, E), lb.reshape(1, E))


def _sgu_bwd(zp, dy, ws, bs, lg, lb, *, name):
    T, E2 = zp.shape
    E = E2 // 2
    G = ws.shape[0]
    cw = E // G
    assert G <= LANES

    def body(zp_ref, dy_ref, ws_ref, bs_ref, lg_ref, lb_ref,
             dzp_ref, dws_ref, dbs_ref, dlg_ref, dlb_ref, dvn_ref):
        @pl.when(pl.program_id(0) == 0)
        def _():
            dws_ref[...] = jnp.zeros_like(dws_ref)
            dbs_ref[...] = jnp.zeros_like(dbs_ref)
            dlg_ref[...] = jnp.zeros_like(dlg_ref)
            dlb_ref[...] = jnp.zeros_like(dlb_ref)

        zpv = zp_ref[...].astype(F32)
        lgv = lg_ref[...]
        u, vhat, rstd, vn = _sgu_pieces(zpv, lgv, lb_ref[...], E)
        gp = _gelu_grad(zpv)
        vnb = vn.astype(BF16)
        tril = _tril_mask()
        lane = lax.broadcasted_iota(jnp.int32, (CHUNK, LANES), 1)
        dbs = dbs_ref[...]
        for g in range(G):
            cols = slice(g * cw, (g + 1) * cw)
            w = jnp.where(tril, ws_ref[g], 0.0).astype(BF16)
            sv = _dot(w, vnb[:, cols], 1, 0) + bs_ref[g]
            dyg = dy_ref[:, cols].astype(F32)
            dzp_ref[:, cols] = (dyg * sv * gp[:, cols]).astype(BF16)
            dsv = dyg * u[:, cols]
            dsvb = dsv.astype(BF16)
            dws_ref[g] += jnp.where(tril, _dot(dsvb, vnb[:, cols], 1, 1), 0.0)
            dvn_ref[:, cols] = _dot(w, dsvb, 0, 0)
            dbs = dbs + jnp.where(lane == g, jnp.sum(dsv, axis=1, keepdims=True), 0.0)
        dbs_ref[...] = dbs
        dvn = dvn_ref[...]
        dlg_ref[...] += jnp.sum(dvn * vhat, axis=0, keepdims=True)
        dlb_ref[...] += jnp.sum(dvn, axis=0, keepdims=True)
        dvhat = dvn * lgv
        c1 = jnp.mean(dvhat, axis=-1, keepdims=True)
        c2 = jnp.mean(dvhat * vhat, axis=-1, keepdims=True)
        dv = rstd * (dvhat - c1 - vhat * c2)
        dzp_ref[:, E:] = (dv * gp[:, E:]).astype(BF16)

    vecE = pl.BlockSpec((1, E), lambda n: (0, 0))
    return pl.pallas_call(
        body, name=name, grid=(T // CHUNK,),
        in_specs=[pl.BlockSpec((CHUNK, E2), lambda n: (n, 0)),
                  pl.BlockSpec((CHUNK, E), lambda n: (n, 0)),
                  pl.BlockSpec((G, CHUNK, CHUNK), lambda n: (0, 0, 0)),
                  pl.BlockSpec((G, CHUNK, 1), lambda n: (0, 0, 0)),
                  vecE, vecE],
        out_specs=[pl.BlockSpec((CHUNK, E2), lambda n: (n, 0)),
                   pl.BlockSpec((G, CHUNK, CHUNK), lambda n: (0, 0, 0)),
                   pl.BlockSpec((CHUNK, LANES), lambda n: (0, 0)),
                   vecE, vecE],
        out_shape=[jax.ShapeDtypeStruct((T, E2), BF16),
                   jax.ShapeDtypeStruct((G, CHUNK, CHUNK), F32),
                   jax.ShapeDtypeStruct((CHUNK, LANES), F32),
                   jax.ShapeDtypeStruct((1, E), F32), jax.ShapeDtypeStruct((1, E), F32)],
        scratch_shapes=[pltpu.VMEM((CHUNK, E), F32)],
        compiler_params=_cp(("arbitrary",)),
    )(zp, dy, ws, bs.reshape(G, CHUNK, 1), lg.reshape(1, E), lb.reshape(1, E))


def _delay(h, k):
    t = lax.broadcasted_iota(jnp.int32, h.shape, 0)
    return jnp.where(t >= k, pltpu.roll(h, k, 0), 0.0)


def _advance(d, k):
    T = d.shape[0]
    t = lax.broadcasted_iota(jnp.int32, d.shape, 0)
    return jnp.where(t < T - k, pltpu.roll(d, T - k, 0), 0.0)


def _conv3(h, w_ref, b_ref):
    return w_ref[2:3, :] * h + w_ref[1:2, :] * _delay(h, 1) + w_ref[0:1, :] * _delay(h, 2) + b_ref[...]


def _convglu_fwd(hup, cw, cb, *, name):
    T, F2 = hup.shape
    F = F2 // 2
    tc = LANES
    nt = F // tc

    def body(ha_ref, hg_ref, wa_ref, wg_ref, ba_ref, bg_ref, o_ref, a_ref, g_ref):
        a = _conv3(ha_ref[...].astype(F32), wa_ref, ba_ref)
        g = _conv3(hg_ref[...].astype(F32), wg_ref, bg_ref)
        o_ref[...] = (_gelu(a) * g).astype(BF16)
        a_ref[...] = a.astype(BF16)
        g_ref[...] = g.astype(BF16)

    col = lambda off: pl.BlockSpec((T, tc), lambda j: (0, j + off))
    w3 = lambda off: pl.BlockSpec((3, tc), lambda j: (0, j + off))
    b1 = lambda off: pl.BlockSpec((1, tc), lambda j: (0, j + off))
    shp = jax.ShapeDtypeStruct((T, F), BF16)
    return pl.pallas_call(
        body, name=name, grid=(nt,),
        in_specs=[col(0), col(nt), w3(0), w3(nt), b1(0), b1(nt)],
        out_specs=[pl.BlockSpec((T, tc), lambda j: (0, j))] * 3, out_shape=[shp] * 3,
        compiler_params=_cp(("parallel",)),
    )(hup, hup, cw, cw, cb.reshape(1, F2), cb.reshape(1, F2))


def _convglu_bwd(hup, a, g, dact, cw, *, name):
    T, F2 = hup.shape
    F = F2 // 2
    tc = LANES
    nt = F // tc

    def half(h, w_ref, d, dh_ref, dw_ref, db_ref, i):
        d1, d2 = _advance(d, 1), _advance(d, 2)
        dh_ref[i] = (w_ref[2:3, :] * d + w_ref[1:2, :] * d1 + w_ref[0:1, :] * d2).astype(BF16)
        c = _dot(h, jnp.concatenate([d2, d1, d], axis=1).astype(BF16), 0, 0)
        eye = lax.broadcasted_iota(jnp.int32, (tc, tc), 0) == lax.broadcasted_iota(jnp.int32, (tc, tc), 1)
        for k in range(3):
            dw_ref[i, k:k + 1, :] = jnp.sum(jnp.where(eye, c[:, k * tc:(k + 1) * tc], 0.0), axis=0, keepdims=True)
        db_ref[i] = jnp.sum(d, axis=0, keepdims=True)

    n_buf = 3
    assert hup.dtype == a.dtype == g.dtype == dact.dtype == BF16

    def body(ha_hbm, hg_hbm, a_hbm, g_hbm, d_hbm, wa_ref, wg_ref, dh_ref, dw_ref, db_ref, bufs, sems):
        j = pl.program_id(0)
        srcs = ((ha_hbm, 0), (hg_hbm, nt), (a_hbm, 0), (g_hbm, 0), (d_hbm, 0))

        def copy(i, step):
            ref, off = srcs[i]
            cols = pl.ds(pl.multiple_of((step + off) * tc, tc), tc)
            return pltpu.make_async_copy(ref.at[:, cols], bufs.at[i, step % n_buf], sems.at[i, step % n_buf])

        @pl.when(j == 0)
        def _():
            for s in range(min(n_buf - 1, nt)):
                for i in range(len(srcs)):
                    copy(i, s).start()

        @pl.when(j + n_buf - 1 < nt)
        def _():
            for i in range(len(srcs)):
                copy(i, j + n_buf - 1).start()

        for i in range(len(srcs)):
            copy(i, j).wait()
        slot = j % n_buf
        a = bufs[2, slot].astype(F32)
        d = bufs[4, slot].astype(F32)
        half(bufs[0, slot], wa_ref, d * bufs[3, slot].astype(F32) * _gelu_grad(a), dh_ref, dw_ref, db_ref, 0)
        half(bufs[1, slot], wg_ref, d * _gelu(a), dh_ref, dw_ref, db_ref, 1)

    hbm = pl.BlockSpec(memory_space=pl.ANY)
    w3 = lambda off: pl.BlockSpec((3, tc), lambda j: (0, j + off))
    return pl.pallas_call(
        body, name=name, grid=(nt,),
        in_specs=[hbm, hbm, hbm, hbm, hbm, w3(0), w3(nt)],
        scratch_shapes=[pltpu.VMEM((5, n_buf, T, tc), BF16), pltpu.SemaphoreType.DMA((5, n_buf))],
        out_specs=[pl.BlockSpec((2, T, tc), lambda j: (0, 0, j)),
                   pl.BlockSpec((2, 3, tc), lambda j: (0, 0, j)),
                   pl.BlockSpec((2, 1, tc), lambda j: (0, 0, j))],
        out_shape=[jax.ShapeDtypeStruct((2, T, F), BF16),
                   jax.ShapeDtypeStruct((2, 3, F), F32),
                   jax.ShapeDtypeStruct((2, 1, F), F32)],
        compiler_params=_cp(("arbitrary",)),
    )(hup, hup, a, g, dact, cw, cw)


def _bucket_table():
    iq = np.arange(BLK)[:, None]
    ik = np.arange(2 * BLK)[None, :]
    delta = iq + BLK - ik
    out = []
    for win, dil in DILATED_GROUPS:
        valid = (delta >= 0) & (delta <= win // dil)
        n = (np.clip(delta, 0, None) * dil).astype(np.int32)
        max_exact = REL_BUCKETS // 2
        nf = np.maximum(n, 1).astype(np.float32)
        large = max_exact + (np.log(nf / np.float32(max_exact)) / np.float32(math.log(REL_MAX_DIST / max_exact))
                             * np.float32(REL_BUCKETS - max_exact)).astype(np.int32)
        large = np.minimum(large, REL_BUCKETS - 1)
        out.append(np.where(valid, np.where(n < max_exact, n, large), -1))
    return np.stack(out).astype(np.int32)


def _band_bias(rel_table, H, *, name):
    bkt = jnp.asarray(_bucket_table())

    def body(tab_ref, bk_ref, o_ref):
        g = pl.program_id(0)
        bk = bk_ref[...]
        prev = lax.broadcasted_iota(jnp.int32, bk.shape, 1) < BLK
        for h in range(H):
            def bucket(c, acc):
                return jnp.where(bk == c, tab_ref[c, g * H + h], acc)

            b = lax.fori_loop(0, REL_BUCKETS, bucket, jnp.zeros(bk.shape, F32))
            b = jnp.where(bk >= 0, b, NEG)
            o_ref[h, 0] = b
            o_ref[h, 1] = jnp.where(prev, NEG, b)

    return pl.pallas_call(
        body, name=name, grid=(N_GROUPS,),
        in_specs=[pl.BlockSpec(memory_space=pltpu.SMEM),
                  pl.BlockSpec((None, BLK, 2 * BLK), lambda g: (g, 0, 0))],
        out_specs=pl.BlockSpec((None, H, 2, BLK, 2 * BLK), lambda g: (g, 0, 0, 0, 0)),
        out_shape=jax.ShapeDtypeStruct((N_GROUPS, H, 2, BLK, 2 * BLK), F32),
        compiler_params=_cp(("parallel",)),
    )(rel_table, bkt)


PAIR = 2 * HEAD_DIM


def _bdot(a, b, ca, cb):
    return lax.dot_general(a, b, (((ca,), (cb,)), ((0,), (0,))), preferred_element_type=F32)


def _blocks_per_step(nb, dil):
    return 2 if (nb // dil) % 2 == 0 else 1


def _blk(i):
    return slice(i * BLK, (i + 1) * BLK)


def _heads(ref, i, H, own_columns_only):
    low = lax.broadcasted_iota(jnp.int32, (BLK, PAIR), 1) < HEAD_DIM
    out = []
    for h in range(H):
        a = ref[_blk(i), (h // 2) * PAIR:(h // 2 + 1) * PAIR]
        if own_columns_only:
            a = jnp.where(low if h % 2 == 0 else jnp.logical_not(low), a, jnp.zeros_like(a))
        out.append(a)
    return jnp.stack(out)


def _pair_sums(o):
    return [o[2 * p] + o[2 * p + 1] for p in range(o.shape[0] // 2)]


def _store_pairs(ref, i, o, dtype, scale=None):
    for p, v in enumerate(_pair_sums(o)):
        ref[_blk(i), p * PAIR:(p + 1) * PAIR] = (v if scale is None else scale * v).astype(dtype)


def _stat_tile(s):
    lane = lax.broadcasted_iota(jnp.int32, (BLK, LANES), 1)
    t = jnp.zeros((BLK, LANES), F32)
    for h in range(s.shape[0]):
        t = jnp.where(lane == h, s[h], t)
    return t


def _stat_cols(ref, i, H):
    return jnp.stack([ref[_blk(i), h:h + 1] for h in range(H)])


def _scores(qm, kp, kc, bias):
    scale = HEAD_DIM ** -0.5
    sp = _bdot(qm, kp, 2, 2) * scale + bias[:, :, :BLK]
    sc = _bdot(qm, kc, 2, 2) * scale + bias[:, :, BLK:]
    return sp, sc


def _attn_specs(nb, dil, g, H, HD, sb):
    nbg = nb // dil
    QB = _blocks_per_step(nb, dil)
    cur = lambda col: pl.BlockSpec((QB * BLK, HD), lambda b: (sb(b), col))
    prev = lambda col: pl.BlockSpec((BLK, HD), lambda b: (jnp.maximum(QB * sb(b) - 1, 0), col))
    stat = pl.BlockSpec((QB * BLK, LANES), lambda b: (sb(b), 0))
    bias0 = pl.BlockSpec((None, H, None, BLK, 2 * BLK),
                         lambda b: (g, 0, jnp.where((QB * sb(b)) % nbg == 0, 1, 0), 0, 0))
    bias1 = pl.BlockSpec((None, H, None, BLK, 2 * BLK), lambda b: (g, 0, 0, 0, 0))
    return cur, prev, stat, bias0, bias1


def _attn_fwd(q, k, v, bias, g, dil, *, name):
    (qa, qc), (ka, kc_), (va, vc_) = q, k, v
    T = qa.shape[0]
    H = bias.shape[1]
    HD = H * HEAD_DIM
    assert H % 2 == 0 and H <= LANES and T % (dil * BLK) == 0
    nb = T // BLK
    QB = _blocks_per_step(nb, dil)
    cur, prev, stat, bias0, bias1 = _attn_specs(nb, dil, g, H, HD, lambda b: b)

    def body(q_ref, kp_ref, kc_ref, vp_ref, vc_ref, b0_ref, b1_ref, o_ref, l_ref):
        for i in range(QB):
            kpi, vpi = (kp_ref, vp_ref) if i == 0 else (kc_ref, vc_ref)
            sp, sc = _scores(_heads(q_ref, i, H, True), _heads(kpi, max(i - 1, 0), H, False),
                             _heads(kc_ref, i, H, False), (b0_ref if i == 0 else b1_ref)[...])
            m = jnp.maximum(jnp.max(sp, axis=-1, keepdims=True), jnp.max(sc, axis=-1, keepdims=True))
            pp, pc = jnp.exp(sp - m), jnp.exp(sc - m)
            den = jnp.sum(pp, axis=-1, keepdims=True) + jnp.sum(pc, axis=-1, keepdims=True)
            o = (_bdot((pp / den).astype(BF16), _heads(vpi, max(i - 1, 0), H, True), 2, 1)
                 + _bdot((pc / den).astype(BF16), _heads(vc_ref, i, H, True), 2, 1))
            _store_pairs(o_ref, i, o, F32)
            l_ref[_blk(i), :] = _stat_tile(m + jnp.log(den))

    return pl.pallas_call(
        body, name=name, grid=(nb // QB,),
        in_specs=[cur(qc), prev(kc_), cur(kc_), prev(vc_), cur(vc_), bias0, bias1],
        out_specs=[cur(0), stat],
        out_shape=[jax.ShapeDtypeStruct((T, HD), F32), jax.ShapeDtypeStruct((T, LANES), F32)],
        compiler_params=_cp(("parallel",)),
    )(qa, ka, ka, va, va, bias, bias)


def _grouped_spec(tr, d, C):
    return pl.BlockSpec((d, tr // d, C), lambda i: (0, i, 0))


def _as_grouped(a, d):
    T, C = a.shape
    return a.reshape(d, T // d, C)


def _to_token_slabs(ref, slabs, d):
    n_r = ref.shape[1]
    for j in range(ref.shape[2] // LANES):
        for r in range(d):
            slabs.at[j][pl.ds(r, n_r, stride=d), :] = ref[r, :, j * LANES:(j + 1) * LANES].astype(F32)


def _from_token_slabs(slabs, ref, d):
    n_r = ref.shape[1]
    for j in range(ref.shape[2] // LANES):
        for r in range(d):
            ref[r, :, j * LANES:(j + 1) * LANES] = slabs.at[j][pl.ds(r, n_r, stride=d), :].astype(ref.dtype)


def _regroup_to_tokens(parts, out_dtype, *, name):
    T, C = parts[0][0].shape
    n = len(parts)
    tr = _tile(T, 512, SUBLANES)
    n_sl = C // LANES

    def body(*refs):
        o_ref, scr = refs[n], refs[n + 1]
        for g, (_, d) in enumerate(parts):
            _to_token_slabs(refs[g], scr, d)
            for j in range(n_sl):
                o_ref[:, g * C + j * LANES:g * C + (j + 1) * LANES] = scr[j].astype(out_dtype)

    return pl.pallas_call(
        body, name=name, grid=(T // tr,),
        in_specs=[_grouped_spec(tr, d, C) for _, d in parts],
        out_specs=pl.BlockSpec((tr, n * C), lambda i: (i, 0)),
        out_shape=jax.ShapeDtypeStruct((T, n * C), out_dtype),
        scratch_shapes=[pltpu.VMEM((n_sl, tr, LANES), F32)],
        compiler_params=_cp(("parallel",)),
    )(*[_as_grouped(a, d) for a, d in parts])


def _attn_combine(os, ls, dils, *, name):
    n = len(os)
    T, HD = os[0].shape
    H = HD // HEAD_DIM
    tr = _tile(T, 512, SUBLANES)
    n_sl = HD // LANES
    assert PAIR == LANES

    def body(*refs):
        o_refs, l_refs = refs[:n], refs[n:2 * n]
        oc_ref, ocb_ref = refs[2 * n:2 * n + 2]
        lt_refs = refs[2 * n + 2:3 * n + 2]
        o_scr, l_scr = refs[3 * n + 2], refs[3 * n + 3]
        for g in range(n):
            _to_token_slabs(o_refs[g], o_scr.at[g], dils[g])
            _to_token_slabs(l_refs[g], l_scr.at[g], dils[g])
        ls_ = [l_scr[g, 0] for g in range(n)]
        m = ls_[0]
        for l in ls_[1:]:
            m = jnp.maximum(m, l)
        ws = [jnp.exp(l - m) for l in ls_]
        den = ws[0]
        for w in ws[1:]:
            den = den + w
        l_scr[n, 0] = m + jnp.log(den)
        for g in range(n):
            _from_token_slabs(l_scr.at[n], lt_refs[g], dils[g])
        ws = [w / den for w in ws]
        low = lax.broadcasted_iota(jnp.int32, (tr, PAIR), 1) < HEAD_DIM
        for p in range(H // 2):
            acc = jnp.zeros((tr, PAIR), F32)
            for g in range(n):
                w = jnp.where(low, ws[g][:, 2 * p:2 * p + 1], ws[g][:, 2 * p + 1:2 * p + 2])
                acc = acc + w * o_scr[g, p]
            oc_ref[:, p * PAIR:(p + 1) * PAIR] = acc
            ocb_ref[:, p * PAIR:(p + 1) * PAIR] = acc.astype(BF16)

    wide = pl.BlockSpec((tr, HD), lambda i: (i, 0))
    lts = pl.pallas_call(
        body, name=name, grid=(T // tr,),
        in_specs=[_grouped_spec(tr, d, HD) for d in dils] + [_grouped_spec(tr, d, LANES) for d in dils],
        out_specs=[wide, wide] + [_grouped_spec(tr, d, LANES) for d in dils],
        out_shape=[jax.ShapeDtypeStruct((T, HD), F32), jax.ShapeDtypeStruct((T, HD), BF16)]
        + [jax.ShapeDtypeStruct((d, T // d, LANES), F32) for d in dils],
        scratch_shapes=[pltpu.VMEM((n, n_sl, tr, LANES), F32), pltpu.VMEM((n + 1, 1, tr, LANES), F32)],
        compiler_params=_cp(("parallel",)),
    )(*[_as_grouped(o, d) for o, d in zip(os, dils)], *[_as_grouped(l, d) for l, d in zip(ls, dils)])
    return lts[0], lts[1], [l.reshape(T, LANES) for l in lts[2:]]


def _attn_bwd_prep(do, oc, dils, *, name):
    T, HD = do.shape
    H = HD // HEAD_DIM
    n = len(dils)
    tr = _tile(T, 512, SUBLANES)
    n_sl = HD // LANES
    wide = pl.BlockSpec((tr, HD), lambda i: (i, 0))

    def body(*refs):
        do_ref, oc_ref = refs[:2]
        dl_refs, dob_refs = refs[2:2 + n], refs[2 + n:2 + 2 * n]
        do_scr, dl_scr = refs[2 + 2 * n], refs[3 + 2 * n]
        low = lax.broadcasted_iota(jnp.int32, (tr, PAIR), 1) < HEAD_DIM
        lane = lax.broadcasted_iota(jnp.int32, (tr, LANES), 1)
        dl = jnp.zeros((tr, LANES), F32)
        for p in range(H // 2):
            d = do_ref[:, p * PAIR:(p + 1) * PAIR]
            do_scr[p] = d
            prod = d * oc_ref[:, p * PAIR:(p + 1) * PAIR]
            dl = jnp.where(lane == 2 * p, jnp.sum(jnp.where(low, prod, 0.0), axis=-1, keepdims=True), dl)
            dl = jnp.where(lane == 2 * p + 1, jnp.sum(jnp.where(low, 0.0, prod), axis=-1, keepdims=True), dl)
        dl_scr[0] = dl
        for g in range(n):
            _from_token_slabs(dl_scr, dl_refs[g], dils[g])
            _from_token_slabs(do_scr, dob_refs[g], dils[g])

    outs = pl.pallas_call(
        body, name=name, grid=(T // tr,),
        in_specs=[wide, wide],
        out_specs=[_grouped_spec(tr, d, LANES) for d in dils] + [_grouped_spec(tr, d, HD) for d in dils],
        out_shape=[jax.ShapeDtypeStruct((d, T // d, LANES), F32) for d in dils]
        + [jax.ShapeDtypeStruct((d, T // d, HD), BF16) for d in dils],
        scratch_shapes=[pltpu.VMEM((n_sl, tr, LANES), F32), pltpu.VMEM((1, tr, LANES), F32)],
        compiler_params=_cp(("parallel",)),
    )(do, oc)
    return [a.reshape(T, LANES) for a in outs[:n]], [a.reshape(T, HD) for a in outs[n:]]


def _attn_bwd(q, k, v, bias, g, dil, do, lse, delta, *, name):
    (qa, qc), (ka, kc_), (va, vc_) = q, k, v
    T, HD = do.shape
    H = HD // HEAD_DIM
    nb = T // BLK
    scale = HEAD_DIM ** -0.5
    QB = _blocks_per_step(nb, dil)
    ns = nb // QB
    cur, prev, stat, bias0, bias1 = _attn_specs(nb, dil, g, H, HD, lambda b: jnp.minimum(b, ns - 1))
    late = pl.BlockSpec((QB * BLK, HD), lambda b: (jnp.maximum(b - 1, 0), 0))

    def body(q_ref, kp_ref, kc_ref, vp_ref, vc_ref, b0_ref, b1_ref, do_ref, l_ref, dl_ref,
             dq_ref, dk_ref, dv_ref, db_ref, ck_ref, cv_ref):
        b = pl.program_id(0)

        @pl.when(b == 0)
        def _():
            db_ref[...] = jnp.zeros_like(db_ref)
            ck_ref[...] = jnp.zeros_like(ck_ref)
            cv_ref[...] = jnp.zeros_like(cv_ref)

        @pl.when(b < ns)
        def _():
            dk_ref[...] = ck_ref[...]
            dv_ref[...] = cv_ref[...]
            for i in range(QB):
                kpi, vpi, ip = (kp_ref, vp_ref, 0) if i == 0 else (kc_ref, vc_ref, i - 1)
                qm, dom = _heads(q_ref, i, H, True), _heads(do_ref, i, H, True)
                sp, sc = _scores(qm, _heads(kpi, ip, H, False), _heads(kc_ref, i, H, False),
                                 (b0_ref if i == 0 else b1_ref)[...])
                l, dl = _stat_cols(l_ref, i, H), _stat_cols(dl_ref, i, H)
                pp, pc = jnp.exp(sp - l), jnp.exp(sc - l)
                dsp = pp * (_bdot(dom, _heads(vpi, ip, H, False), 2, 2) - dl)
                dsc = pc * (_bdot(dom, _heads(vc_ref, i, H, False), 2, 2) - dl)
                db_ref[:, :, :BLK] += dsp
                db_ref[:, :, BLK:] += dsc
                dspb, dscb = dsp.astype(BF16), dsc.astype(BF16)
                _store_pairs(dq_ref, i, _bdot(dspb, _heads(kpi, ip, H, True), 2, 1)
                             + _bdot(dscb, _heads(kc_ref, i, H, True), 2, 1), F32, scale)
                dkp = _pair_sums(_bdot(dspb, qm, 1, 1))
                dkc = _pair_sums(_bdot(dscb, qm, 1, 1))
                dvp = _pair_sums(_bdot(pp.astype(BF16), dom, 1, 1))
                dvc = _pair_sums(_bdot(pc.astype(BF16), dom, 1, 1))
                for p in range(H // 2):
                    cols = slice(p * PAIR, (p + 1) * PAIR)
                    if i == 0:
                        dk_ref[_blk(QB - 1), cols] += scale * dkp[p]
                        dv_ref[_blk(QB - 1), cols] += dvp[p]
                    else:
                        ck_ref[_blk(i - 1), cols] += scale * dkp[p]
                        cv_ref[_blk(i - 1), cols] += dvp[p]
                    ck_ref[_blk(i), cols] = scale * dkc[p]
                    cv_ref[_blk(i), cols] = dvc[p]

        @pl.when(b == ns)
        def _():
            dk_ref[...] = ck_ref[...]
            dv_ref[...] = cv_ref[...]

    f32 = jax.ShapeDtypeStruct((T, HD), F32)
    return pl.pallas_call(
        body, name=name, grid=(ns + 1,),
        in_specs=[cur(qc), prev(kc_), cur(kc_), prev(vc_), cur(vc_), bias0, bias1, cur(0), stat, stat],
        out_specs=[cur(0), late, late, pl.BlockSpec((H, BLK, 2 * BLK), lambda b: (0, 0, 0))],
        out_shape=[f32, f32, f32,
                   jax.ShapeDtypeStruct((H, BLK, 2 * BLK), F32)],
        scratch_shapes=[pltpu.VMEM((QB * BLK, HD), F32), pltpu.VMEM((QB * BLK, HD), F32)],
        compiler_params=_cp(("arbitrary",)),
    )(qa, ka, ka, va, va, bias, bias, do, lse, delta)


def _rel_grad(dbs, *, name):
    H = dbs[0].shape[1]
    n = len(dbs)
    assert H <= LANES
    bkt = jnp.asarray(_bucket_table())

    def body(*refs):
        bk = refs[n][...]
        o_ref = refs[n + 1]
        db = refs[0][...]
        for r in refs[1:n]:
            db = db + r[...]
        row = lax.broadcasted_iota(jnp.int32, (REL_BUCKETS, LANES), 0)
        col = lax.broadcasted_iota(jnp.int32, (REL_BUCKETS, LANES), 1)

        def bucket(c, acc):
            sel = bk == c
            for h in range(H):
                val = jnp.sum(jnp.where(sel, db[h], 0.0))
                acc = jnp.where((row == c) & (col == h), val, acc)
            return acc

        o_ref[...] = lax.fori_loop(0, REL_BUCKETS, bucket, jnp.zeros((REL_BUCKETS, LANES), F32))

    dspec = pl.BlockSpec((None, H, BLK, 2 * BLK), lambda g: (g, 0, 0, 0))
    return pl.pallas_call(
        body, name=name, grid=(N_GROUPS,),
        in_specs=[dspec] * n + [pl.BlockSpec((None, BLK, 2 * BLK), lambda g: (g, 0, 0))],
        out_specs=pl.BlockSpec((None, REL_BUCKETS, LANES), lambda g: (g, 0, 0)),
        out_shape=jax.ShapeDtypeStruct((N_GROUPS, REL_BUCKETS, LANES), F32),
        compiler_params=_cp(("parallel",)),
    )(*dbs, bkt)


def _to_group(a, d):
    T, C = a.shape
    return a if d == 1 else a.reshape(T // d, d, C).transpose(1, 0, 2).reshape(T, C)


def _from_group(a, d):
    T, C = a.shape
    return a if d == 1 else a.reshape(d, T // d, C).transpose(1, 0, 2).reshape(T, C)


def _rows_view(a):
    a2 = a.reshape(-1, a.shape[-1])
    R, C = a2.shape
    tr = _tile(R, max(SUBLANES, (1 << 19) // C), SUBLANES)
    return a2, R, C, tr


def _addn(xs, out_dtype, *, name):
    shape = xs[0].shape
    x2s = [_rows_view(x)[0] for x in xs]
    _, R, C, tr = _rows_view(xs[0])
    spec = pl.BlockSpec((tr, C), lambda i: (i, 0))

    def body(*refs):
        acc = refs[0][...].astype(F32)
        for r in refs[1:-1]:
            acc = acc + r[...].astype(F32)
        refs[-1][...] = acc.astype(out_dtype)

    out = pl.pallas_call(
        body, name=name, grid=(R // tr,),
        in_specs=[spec] * len(xs), out_specs=spec,
        out_shape=jax.ShapeDtypeStruct((R, C), out_dtype),
        compiler_params=_cp(("parallel",)),
    )(*x2s)
    return out.reshape(shape)


def _addn_grouped(parts, out_dtype, *, name):
    T, C = parts[0][0].shape
    tr = _tile(T, 512, SUBLANES)
    assert all(tr % d == 0 for _, d in parts)

    assert C % LANES == 0

    def body(*refs):
        o_ref, acc_ref = refs[-2], refs[-1]
        acc_ref[...] = jnp.zeros_like(acc_ref)
        for ref, (_, d) in zip(refs, parts):
            for r in range(d):
                for j in range(C // LANES):
                    acc_ref.at[j][pl.ds(r, tr // d, stride=d), :] += ref[r, :, j * LANES:(j + 1) * LANES]
        for j in range(C // LANES):
            o_ref[:, j * LANES:(j + 1) * LANES] = acc_ref[j].astype(out_dtype)

    return pl.pallas_call(
        body, name=name, grid=(T // tr,),
        in_specs=[pl.BlockSpec((d, tr // d, C), lambda i: (0, i, 0)) for _, d in parts],
        out_specs=pl.BlockSpec((tr, C), lambda i: (i, 0)),
        out_shape=jax.ShapeDtypeStruct((T, C), out_dtype),
        scratch_shapes=[pltpu.VMEM((C // LANES, tr, LANES), F32)],
        compiler_params=_cp(("parallel",)),
    )(*[a.reshape(d, T // d, C) for a, d in parts])


def _slot_sum(b_ref, own, my_id):
    acc = None
    for s in range(b_ref.shape[0]):
        term = jnp.where(my_id == s, own, b_ref[s].astype(F32))
        acc = term if acc is None else acc + term
    return acc


def _part_geometry(shape, axis):
    _, R, C = shape
    if axis == COL:
        return R // 2, C // N_CHIPS
    return R // (2 * N_CHIPS), C


def _sum_parts(buf, grad, axis, where, *, name):
    n, L, Rp, Cp = buf.shape
    assert (Rp, Cp) == _part_geometry(grad.shape, axis)
    tr = _tile(Rp, max(2 * SUBLANES, (1 << 18) // Cp), 2 * SUBLANES)
    nr = Rp // tr

    def own_map(l, i, w):
        if axis == COL:
            return (l, w[1] * nr + i, w[0])
        return (l, (2 * w[0] + w[1]) * nr + i, 0)

    def body(w_ref, b_ref, g_ref, o_ref):
        o_ref[...] = _slot_sum(b_ref, g_ref[...].astype(F32), w_ref[2])

    return pl.pallas_call(
        body, name=name,
        grid_spec=pltpu.PrefetchScalarGridSpec(
            num_scalar_prefetch=1, grid=(L, nr),
            in_specs=[pl.BlockSpec((n, None, tr, Cp), lambda l, i, w: (0, l, i, 0)),
                      pl.BlockSpec((None, tr, Cp), own_map)],
            out_specs=pl.BlockSpec((None, None, tr, Cp), lambda l, i, w: (l, w[1], i, 0))),
        out_shape=jax.ShapeDtypeStruct((L, 2, Rp, Cp), F32),
        compiler_params=_cp(("parallel", "parallel")),
    )(where, buf, grad)


def _sum_small(buf, own, where, *, name):
    n, R, C = buf.shape
    tr = _tile(R, 512, SUBLANES)

    def body(w_ref, b_ref, g_ref, o_ref):
        o_ref[...] = _slot_sum(b_ref, g_ref[...], w_ref[2])

    return pl.pallas_call(
        body, name=name,
        grid_spec=pltpu.PrefetchScalarGridSpec(
            num_scalar_prefetch=1, grid=(R // tr,),
            in_specs=[pl.BlockSpec((n, tr, C), lambda i, w: (0, i, 0)),
                      pl.BlockSpec((tr, C), lambda i, w: (i, 0))],
            out_specs=pl.BlockSpec((tr, C), lambda i, w: (i, 0))),
        out_shape=jax.ShapeDtypeStruct((R, C), F32),
        compiler_params=_cp(("parallel",)),
    )(where, buf, own)


def _place_shard(shard, layer, axis, where, out_dtype, *, name, after=None):
    _, R, C = shard.shape
    tr = _tile(R, max(2 * SUBLANES, (1 << 19) // C), 2 * SUBLANES)
    nr = R // tr
    if axis == COL:
        full, out_map = (1, R, C * N_CHIPS), (lambda i, w: (0, i, w[0]))
    else:
        full, out_map = (1, R * N_CHIPS, C), (lambda i, w: (0, w[0] * nr + i, 0))

    def body(w_ref, x_ref, *rest):
        rest[-1][...] = x_ref[...].astype(out_dtype)

    extra = [] if after is None else [after]
    return pl.pallas_call(
        body, name=name,
        grid_spec=pltpu.PrefetchScalarGridSpec(
            num_scalar_prefetch=1, grid=(nr,),
            in_specs=[pl.BlockSpec((None, tr, C), lambda i, w: (layer, i, 0))]
            + [pl.BlockSpec(memory_space=pl.ANY)] * len(extra),
            out_specs=pl.BlockSpec((None, tr, C), out_map)),
        out_shape=jax.ShapeDtypeStruct(full, out_dtype),
        compiler_params=_cp(("parallel",)),
    )(where, shard, *extra)


def _loss_grad(y, tgt, *, name):
    T, D = y.shape
    tr = _tile(T, 512, SUBLANES)
    row = pl.BlockSpec((tr, D), lambda i: (i, 0))

    def body(y_ref, t_ref, dy_ref, l_ref, acc_ref):
        i = pl.program_id(0)

        @pl.when(i == 0)
        def _():
            acc_ref[...] = jnp.zeros_like(acc_ref)

        err = y_ref[...] - t_ref[...]
        dy_ref[...] = err * (1.0 / D)
        acc_ref[...] += jnp.sum(err * err, axis=0, keepdims=True)

        @pl.when(i == T // tr - 1)
        def _():
            tot = jnp.sum(acc_ref[...], axis=1, keepdims=True) * (0.5 / D)
            l_ref[...] = jnp.broadcast_to(tot, (1, LANES))

    return pl.pallas_call(
        body, name=name, grid=(T // tr,),
        in_specs=[row, row],
        out_specs=[row, pl.BlockSpec((1, LANES), lambda i: (0, 0))],
        out_shape=[jax.ShapeDtypeStruct((T, D), F32), jax.ShapeDtypeStruct((1, LANES), F32)],
        scratch_shapes=[pltpu.VMEM((1, D), F32)],
        compiler_params=_cp(("arbitrary",)),
    )(y, tgt)


def _adamw(w, g, m, v, *, name):
    shape = w.shape
    w2, R, C, tr = _rows_view(w)
    spec = pl.BlockSpec((tr, C), lambda i: (i, 0))

    def body(w_ref, g_ref, m_ref, v_ref, d_ref, nm_ref, nv_ref):
        gv = g_ref[...]
        nm = ADAM_B1 * m_ref[...] + (1.0 - ADAM_B1) * gv
        nv = ADAM_B2 * v_ref[...] + (1.0 - ADAM_B2) * (gv * gv)
        m_hat = nm / (1.0 - ADAM_B1 ** ADAM_STEP)
        v_hat = nv / (1.0 - ADAM_B2 ** ADAM_STEP)
        d_ref[...] = -ADAM_LR * (m_hat / (jnp.sqrt(v_hat) + ADAM_EPS) + ADAM_WD * w_ref[...])
        nm_ref[...] = nm
        nv_ref[...] = nv

    shp = jax.ShapeDtypeStruct((R, C), F32)
    outs = pl.pallas_call(
        body, name=name, grid=(R // tr,),
        in_specs=[spec] * 4, out_specs=[spec] * 3, out_shape=[shp] * 3,
        compiler_params=_cp(("parallel",)),
    )(w2, g.reshape(R, C), m.reshape(R, C), v.reshape(R, C))
    return tuple(o.reshape(shape) for o in outs)


def _pack(arrs):
    flat = jnp.concatenate([a.reshape(-1).astype(F32) for a in arrs])
    unit = SUBLANES * LANES
    pad = (-flat.shape[0]) % unit
    return jnp.pad(flat, (0, pad)).reshape(-1, LANES)


def _unpack(buf, shapes):
    flat = buf.reshape(-1)
    out, off = [], 0
    for s in shapes:
        n = int(np.prod(s))
        out.append(flat[off:off + n].reshape(s))
        off += n
    return out


def _me():
    return lax.axis_index("x"), lax.axis_index("y"), lax.axis_index("c")


def _flip(pos, k):
    x, y, c = pos
    return (1 - x if k & 4 else x, 1 - y if k & 2 else y, 1 - c if k & 1 else c)


HBM = pl.BlockSpec(memory_space=pltpu.HBM)

COL, ROW = -1, -2


def _shard_window(ref, axis, j, n):
    start = pl.multiple_of(j * n, n)
    if axis == COL:
        return ref.at[:, :, pl.ds(start, n)]
    return ref.at[:, pl.ds(start, n), :]


GATHER_PEERS = (2, 4, 6)
SEM = pl.BlockSpec(memory_space=pltpu.SEMAPHORE)
EFFECT = pltpu.SideEffectType.DATAFLOW_SIDE_EFFECTING


def _gather_copy(buf, axis, i, pi, chip, me, send_sems, recv_sems):
    win = _shard_window(buf, axis, chip, buf.shape[axis] // N_CHIPS)
    k = len(GATHER_PEERS) * i + pi
    return pltpu.make_async_remote_copy(
        src_ref=win, dst_ref=win, send_sem=send_sems.at[k], recv_sem=recv_sems.at[k],
        device_id=_flip(me, GATHER_PEERS[pi]), device_id_type=MESH)


def _gather_start(fulls, axes, *, name):
    n = len(fulls)
    n_sem = len(GATHER_PEERS) * n

    def body(*refs):
        send_sems, recv_sems = refs[n], refs[n + 1]
        bufs = refs[n + 2:2 * n + 2]
        token = refs[2 * n + 2]
        me = _me()
        for i in range(n):
            for pi in range(len(GATHER_PEERS)):
                _gather_copy(bufs[i], axes[i], i, pi, 2 * me[0] + me[1], me, send_sems, recv_sems).start()
        token[...] = jnp.zeros_like(token)

    outs = pl.pallas_call(
        body, name=name,
        in_specs=[HBM] * n, out_specs=[SEM, SEM] + [HBM] * n + [pl.BlockSpec(memory_space=pltpu.VMEM)],
        out_shape=[pltpu.SemaphoreType.DMA((n_sem,)), pltpu.SemaphoreType.DMA((n_sem,))]
        + [pltpu.HBM(f.shape, f.dtype) for f in fulls] + [jax.ShapeDtypeStruct((SUBLANES, LANES), F32)],
        input_output_aliases={i: 2 + i for i in range(n)},
        compiler_params=pltpu.CompilerParams(has_side_effects=EFFECT),
    )(*[pltpu.with_memory_space_constraint(f, pltpu.HBM) for f in fulls])
    return outs[0], outs[1], list(outs[2:2 + n]), outs[-1]


def _gather_wait(send_sems, recv_sems, bufs, axes, idxs, after, *, name):
    m = len(bufs)

    def body(*refs):
        ss, rs = refs[m], refs[m + 1]
        outs = refs[m + 3:]
        me = _me()
        for t, i in enumerate(idxs):
            for pi, k in enumerate(GATHER_PEERS):
                px, py, _ = _flip(me, k)
                _gather_copy(outs[t], axes[t], i, pi, 2 * me[0] + me[1], me, ss, rs).wait_send()
                _gather_copy(outs[t], axes[t], i, pi, 2 * px + py, me, ss, rs).wait_recv()

    return pl.pallas_call(
        body, name=name,
        in_specs=[HBM] * m + [SEM, SEM, pl.BlockSpec(memory_space=pl.ANY)], out_specs=[HBM] * m,
        out_shape=[pltpu.HBM(b.shape, b.dtype) for b in bufs],
        input_output_aliases={t: t for t in range(m)},
        compiler_params=pltpu.CompilerParams(has_side_effects=EFFECT),
    )(*bufs, send_sems, recv_sems, after)


def _grad_part(ref, layer, axis, pos):
    px, py, pc = pos
    Rp, Cp = _part_geometry(ref.shape, axis)
    chip = 2 * px + py
    if axis == COL:
        return ref.at[layer, pl.ds(pl.multiple_of(pc * Rp, Rp), Rp), pl.ds(pl.multiple_of(chip * Cp, Cp), Cp)]
    return ref.at[layer, pl.ds(pl.multiple_of((2 * chip + pc) * Rp, Rp), Rp), :]


def _scatter_copy(grad, slots, layer, axis, i, k, src_pos, dst_slot, me, send_sems, recv_sems):
    sem = (N_DEV - 1) * i + k - 1
    return pltpu.make_async_remote_copy(
        src_ref=_grad_part(grad, layer, axis, src_pos), dst_ref=slots.at[dst_slot, layer],
        send_sem=send_sems.at[sem], recv_sem=recv_sems.at[sem],
        device_id=_flip(me, k), device_id_type=MESH)


def _scatter_start(grads, slots, layers, axes, *, name):
    n = len(grads)
    n_sem = (N_DEV - 1) * n

    def body(*refs):
        send_sems, recv_sems = refs[2 * n], refs[2 * n + 1]
        g_refs, s_refs = refs[2 * n + 2:3 * n + 2], refs[3 * n + 2:4 * n + 2]
        token = refs[4 * n + 2]
        me = _me()
        my_id = 4 * me[0] + 2 * me[1] + me[2]
        for i in range(n):
            for k in range(1, N_DEV):
                _scatter_copy(g_refs[i], s_refs[i], layers[i], axes[i], i, k, _flip(me, k), my_id, me,
                              send_sems, recv_sems).start()
        token[...] = jnp.zeros_like(token)

    arrays = list(grads) + list(slots)
    outs = pl.pallas_call(
        body, name=name,
        in_specs=[HBM] * (2 * n),
        out_specs=[SEM, SEM] + [HBM] * (2 * n) + [pl.BlockSpec(memory_space=pltpu.VMEM)],
        out_shape=[pltpu.SemaphoreType.DMA((n_sem,)), pltpu.SemaphoreType.DMA((n_sem,))]
        + [pltpu.HBM(a.shape, a.dtype) for a in arrays] + [jax.ShapeDtypeStruct((SUBLANES, LANES), F32)],
        input_output_aliases={i: 2 + i for i in range(2 * n)},
        compiler_params=pltpu.CompilerParams(has_side_effects=EFFECT),
    )(*[pltpu.with_memory_space_constraint(a, pltpu.HBM) for a in arrays])
    return outs[0], outs[1], list(outs[2:2 + n]), list(outs[2 + n:2 + 2 * n]), outs[-1]


def _scatter_wait(sems, grads, slots, axes, plan, after, *, name):
    n = len(grads)
    flat_sems = [s for pair in sems for s in pair]

    def body(*refs):
        sem_refs = refs[2 * n:2 * n + len(flat_sems)]
        first_out = 2 * n + len(flat_sems) + 1
        g_refs, s_refs = refs[first_out:first_out + n], refs[first_out + n:]
        me = _me()
        for j, items in enumerate(plan):
            ss, rs = sem_refs[2 * j], sem_refs[2 * j + 1]
            for i, (a, layer) in enumerate(items):
                for k in range(1, N_DEV):
                    px, py, pc = _flip(me, k)
                    cp = _scatter_copy(g_refs[a], s_refs[a], layer, axes[a], i, k, me, 4 * px + 2 * py + pc, me, ss, rs)
                    cp.wait_send()
                    cp.wait_recv()

    arrays = list(grads) + list(slots)
    outs = pl.pallas_call(
        body, name=name,
        in_specs=[HBM] * (2 * n) + [SEM] * len(flat_sems) + [pl.BlockSpec(memory_space=pl.ANY)],
        out_specs=[HBM] * (2 * n),
        out_shape=[pltpu.HBM(a.shape, a.dtype) for a in arrays],
        input_output_aliases={i: i for i in range(2 * n)},
        compiler_params=pltpu.CompilerParams(has_side_effects=EFFECT),
    )(*arrays, *flat_sems, after)
    return list(outs[:n]), list(outs[n:])


def _small_copy(src, dst, k, slot, me, send_sems, recv_sems):
    return pltpu.make_async_remote_copy(
        src_ref=src, dst_ref=dst.at[slot], send_sem=send_sems.at[k - 1], recv_sem=recv_sems.at[k - 1],
        device_id=_flip(me, k), device_id_type=MESH)


def _small_start(small, slots, after, *, name):
    def body(src_in, dst_in, after_ref, send_sems, recv_sems, src, dst):
        me = _me()
        for k in range(1, N_DEV):
            _small_copy(src, dst, k, 4 * me[0] + 2 * me[1] + me[2], me, send_sems, recv_sems).start()

    arrays = [small, slots]
    return pl.pallas_call(
        body, name=name, in_specs=[HBM, HBM, pl.BlockSpec(memory_space=pl.ANY)], out_specs=[SEM, SEM, HBM, HBM],
        out_shape=[pltpu.SemaphoreType.DMA((N_DEV - 1,)), pltpu.SemaphoreType.DMA((N_DEV - 1,))]
        + [pltpu.HBM(a.shape, a.dtype) for a in arrays],
        input_output_aliases={0: 2, 1: 3},
        compiler_params=pltpu.CompilerParams(has_side_effects=EFFECT),
    )(*[pltpu.with_memory_space_constraint(a, pltpu.HBM) for a in arrays], after)


def _small_wait(send_sems, recv_sems, small, slots, after, *, name):
    def body(src_in, dst_in, ss, rs, after_ref, src, dst):
        me = _me()
        for k in range(1, N_DEV):
            px, py, pc = _flip(me, k)
            cp = _small_copy(src, dst, k, 4 * px + 2 * py + pc, me, ss, rs)
            cp.wait_send()
            cp.wait_recv()

    return pl.pallas_call(
        body, name=name, in_specs=[HBM, HBM, SEM, SEM, pl.BlockSpec(memory_space=pl.ANY)], out_specs=[HBM, HBM],
        out_shape=[pltpu.HBM(small.shape, small.dtype), pltpu.HBM(slots.shape, slots.dtype)],
        input_output_aliases={0: 0, 1: 1},
        compiler_params=pltpu.CompilerParams(has_side_effects=EFFECT),
    )(small, slots, send_sems, recv_sems, after)


def _pair_gather(halves, *, name):
    n = len(halves)

    def body(*refs):
        bufs = refs[n:2 * n]
        send_sems, recv_sems = refs[2 * n:]
        me = _me()
        c = me[2]
        sib = _flip(me, 1)

        def copy(i, half):
            return pltpu.make_async_remote_copy(
                src_ref=bufs[i].at[:, half], dst_ref=bufs[i].at[:, half], send_sem=send_sems.at[i],
                recv_sem=recv_sems.at[i], device_id=sib, device_id_type=MESH)

        sends = [copy(i, c) for i in range(n)]
        for cp in sends:
            cp.start()
        for i in range(n):
            copy(i, 1 - c).wait_recv()
        for cp in sends:
            cp.wait_send()

    return pl.pallas_call(
        body, name=name,
        in_specs=[HBM] * n, out_specs=[HBM] * n,
        out_shape=[jax.ShapeDtypeStruct(h.shape, h.dtype) for h in halves],
        input_output_aliases={i: i for i in range(n)},
        scratch_shapes=[pltpu.SemaphoreType.DMA((n,)), pltpu.SemaphoreType.DMA((n,))],
    )(*halves)


def _local_step(x, tgt, P, fetch, ship, n_layers):
    T, D = x.shape
    P = dict(P)
    W = {}
    depth = P["ffn_conv_b"].shape[0]
    n_a = P["a_w_s"].shape[0]
    alpha = (2 * depth) ** 0.25
    H = P["rel_table"].shape[1] // N_GROUPS
    HD = H * HEAD_DIM
    dils = [d for _, d in DILATED_GROUPS]
    assert T % (dils[-1] * BLK) == 0

    def arrive(stage, after):
        w_new, p_new = fetch(stage, after)
        W.update(w_new)
        P.update(p_new)

    bias = _band_bias(P["rel_table"], H, name="band_bias")
    saved = []
    xf, xb = x, x.astype(BF16)
    kg = vg = None
    for i in range(depth):
        s = {"x": xf, "xb": xb}
        arrive(f"{i}a", xf)
        if i < n_a:
            s["zp"] = _mm(xb, W["a_w_in", i], out_dtype=BF16, name=f"a{i}_in")
            arrive(f"{i}b", s["zp"])
            s["y"] = _sgu_fwd(s["zp"], P["a_w_s"][i], P["a_b_s"][i], P["a_ln_g"][i], P["a_ln_b"][i], name=f"a{i}_sgu")
            s["h"] = _mm(s["y"], W["a_w_out", i], name=f"a{i}_out")
        else:
            j = i - n_a
            if j == 0:
                kv = _mm(xb, W["kv_w", 0], out_dtype=BF16, name="kv_proj")
                kg = [(kv, 0) if d == 1 else (_to_group(kv[:, :HD], d), 0) for d in dils]
                vg = [(kv, 1) if d == 1 else (_to_group(kv[:, HD:], d), 0) for d in dils]
            q = _mm(xb, W["b_w_q", j], out_dtype=BF16, name=f"b{j}_q")
            s["qg"] = [(q, g) if d == 1 else (_to_group(q[:, g * HD:(g + 1) * HD], d), 0) for g, d in enumerate(dils)]
            os, ls = [], []
            for g, d in enumerate(dils):
                o_g, l_g = _attn_fwd(s["qg"][g], kg[g], vg[g], bias, g, d, name=f"b{j}_attn{g}")
                os.append(o_g)
                ls.append(l_g)
            s["oc"], s["ob"], s["lse"] = _attn_combine(os, ls, dils, name=f"b{j}_comb")
            arrive(f"{i}b", s["ob"])
            s["h"] = _mm(s["ob"], W["b_w_o", j], name=f"b{j}_o")
        s["x1"], s["x1b"] = _add_ln(xf, s["h"], P["ln_g"][i, 0], P["ln_b"][i, 0], alpha, name=f"l{i}_ln1")
        arrive(f"{i}c", s["x1b"])
        s["hup"] = _mm(s["x1b"], W["ffn_w_up", i], out_dtype=BF16, name=f"l{i}_up")
        arrive(f"{i}d", s["hup"])
        s["act"], s["ca"], s["cg"] = _convglu_fwd(s["hup"], P["ffn_conv_w"][i], P["ffn_conv_b"][i], name=f"l{i}_glu")
        s["f"] = _mm(s["act"], W["ffn_w_down", i], name=f"l{i}_down")
        xf, xb = _add_ln(s["x1"], s["f"], P["ln_g"][i, 1], P["ln_b"][i, 1], alpha, name=f"l{i}_ln2")
        saved.append(s)

    G, loss_row = _loss_grad(xf, tgt, name="loss")

    gw = {}

    def dw(key, layer, a, b, name, **kw):
        gw[key] = _mm(a, b, ta=True, out_dtype=BF16, out_into=(gw.get(key), n_layers[key], layer), name=name, **kw)

    gp = {k: [None] * n_a for k in ("a_ln_g", "a_ln_b", "a_w_s", "a_b_s")}
    gp.update({k: [None] * depth for k in ("ffn_conv_w", "ffn_conv_b", "ln_g", "ln_b")})
    dk_parts, dv_parts, dbias_parts = [], [], []
    for i in reversed(range(depth)):
        s = saved[i]
        dr2, dr2b, dg2, db2 = _ln_bwd(G, s["x1"], s["f"], P["ln_g"][i, 1], alpha, name=f"l{i}_ln2_bwd")
        dw("ffn_w_down", i, s["act"], dr2b, f"l{i}_down_dw")
        dact = _mm(dr2b, W["ffn_w_down", i], tb=True, out_dtype=BF16, name=f"l{i}_down_dx")
        dhup, dcw, dcb = _convglu_bwd(s["hup"], s["ca"], s["cg"], dact, P["ffn_conv_w"][i], name=f"l{i}_glu_bwd")
        gp["ffn_conv_w"][i] = dcw.transpose(1, 0, 2).reshape(dcw.shape[1], -1)
        gp["ffn_conv_b"][i] = dcb.reshape(-1)
        dw("ffn_w_up", i, s["x1b"], dhup, f"l{i}_up_dw", b_halves=True)
        token = ship(gw, [("ffn_w_down", i), ("ffn_w_up", i)])
        G1 = _mm(dhup, W["ffn_w_up", i], tb=True, a_halves=True, add=dr2, add_scale=alpha, after=token,
                 name=f"l{i}_up_dx")
        dr1, dr1b, dg1, db1 = _ln_bwd(G1, s["x"], s["h"], P["ln_g"][i, 0], alpha, name=f"l{i}_ln1_bwd")
        gp["ln_g"][i] = jnp.concatenate([dg1, dg2], axis=0)
        gp["ln_b"][i] = jnp.concatenate([db1, db2], axis=0)
        if i < n_a:
            dw("a_w_out", i, s["y"], dr1b, f"a{i}_out_dw")
            token = ship(gw, [("a_w_out", i)])
            dy = _mm(dr1b, W["a_w_out", i], tb=True, out_dtype=BF16, after=token, name=f"a{i}_out_dx")
            dzp, dws, dbs, dlg, dlb = _sgu_bwd(s["zp"], dy, P["a_w_s"][i], P["a_b_s"][i], P["a_ln_g"][i],
                                               P["a_ln_b"][i], name=f"a{i}_sgu_bwd")
            gp["a_w_s"][i], gp["a_b_s"][i] = dws, dbs[:, :dws.shape[0]].T
            gp["a_ln_g"][i], gp["a_ln_b"][i] = dlg[0], dlb[0]
            dw("a_w_in", i, s["xb"], dzp, f"a{i}_in_dw")
            token = ship(gw, [("a_w_in", i)])
            G = _mm(dzp, W["a_w_in", i], tb=True, add=dr1, add_scale=alpha, after=token, name=f"a{i}_in_dx")
        else:
            j = i - n_a
            dw("b_w_o", j, s["ob"], dr1b, f"b{j}_o_dw")
            do_tok = _mm(dr1b, W["b_w_o", j], tb=True, name=f"b{j}_o_dx")
            deltas, dobs = _attn_bwd_prep(do_tok, s["oc"], dils, name=f"b{j}_prep")
            dqs, dbs = [], []
            for g, d in enumerate(dils):
                dq, dk, dv, db = _attn_bwd(s["qg"][g], kg[g], vg[g], bias, g, d, dobs[g], s["lse"][g], deltas[g],
                                           name=f"b{j}_attn{g}_bwd")
                dqs.append((dq, d))
                dk_parts.append((dk, d))
                dv_parts.append((dv, d))
                dbs.append(db)
            dbias_parts.append(jnp.stack(dbs))
            dq_tok = _regroup_to_tokens(dqs, BF16, name=f"b{j}_dq_tokens")
            dw("b_w_q", j, s["xb"], dq_tok, f"b{j}_q_dw")
            token = ship(gw, [("b_w_o", j), ("b_w_q", j)])
            G = _mm(dq_tok, W["b_w_q", j], tb=True, add=dr1, add_scale=alpha, after=token, name=f"b{j}_q_dx")
            if j == 0:
                dkv = jnp.concatenate([_addn_grouped(dk_parts, BF16, name="dk_sum"),
                                       _addn_grouped(dv_parts, BF16, name="dv_sum")], axis=1)
                dw("kv_w", 0, s["xb"], dkv, "kv_dw")
                token = ship(gw, [("kv_w", 0)])
                G = _mm(dkv, W["kv_w", 0], tb=True, add=G, add_scale=1.0, after=token, name="kv_dx")
    rel = _rel_grad(dbias_parts, name="rel_grad")
    grel = rel[:, :, :H].transpose(1, 0, 2).reshape(REL_BUCKETS, N_GROUPS * H)
    gsmall = {k: jnp.stack(v) for k, v in gp.items()}
    gsmall["rel_table"] = grel
    return loss_row, G, gw, gsmall


BIG = (("a_w_in", COL), ("a_w_out", ROW), ("kv_w", ROW), ("b_w_q", COL), ("b_w_o", COL),
       ("ffn_w_up", COL), ("ffn_w_down", ROW))
SMALL_SHARDED = ("a_ln_g", "a_ln_b", "ffn_conv_w", "ln_g", "ln_b")
SMALL_REPLICATED = ("a_w_s", "a_b_s", "rel_table", "ffn_conv_b")
WEIGHTS = ("a_w_in", "a_ln_g", "a_ln_b", "a_w_s", "a_b_s", "a_w_out", "kv_w", "b_w_q", "b_w_o", "rel_table",
           "ffn_w_up", "ffn_conv_w", "ffn_conv_b", "ffn_w_down", "ln_g", "ln_b")


def kernel(x, a_w_in, a_ln_g, a_ln_b, a_w_s, a_b_s, a_w_out, kv_w, b_w_q, b_w_o, rel_table, ffn_w_up, ffn_conv_w, ffn_conv_b, ffn_w_down, ln_g, ln_b, loss_target, m_a_w_in, m_a_ln_g, m_a_ln_b, m_a_w_s, m_a_b_s, m_a_w_out, m_kv_w, m_b_w_q, m_b_w_o, m_rel_table, m_ffn_w_up, m_ffn_conv_w, m_ffn_conv_b, m_ffn_w_down, m_ln_g, m_ln_b, v_a_w_in, v_a_ln_g, v_a_ln_b, v_a_w_s, v_a_b_s, v_a_w_out, v_kv_w, v_b_w_q, v_b_w_o, v_rel_table, v_ffn_w_up, v_ffn_conv_w, v_ffn_conv_b, v_ffn_w_down, v_ln_g, v_ln_b):
    w = dict(a_w_in=a_w_in, a_ln_g=a_ln_g, a_ln_b=a_ln_b, a_w_s=a_w_s, a_b_s=a_b_s, a_w_out=a_w_out, kv_w=kv_w,
             b_w_q=b_w_q, b_w_o=b_w_o, rel_table=rel_table, ffn_w_up=ffn_w_up, ffn_conv_w=ffn_conv_w,
             ffn_conv_b=ffn_conv_b, ffn_w_down=ffn_w_down, ln_g=ln_g, ln_b=ln_b)
    m = dict(a_w_in=m_a_w_in, a_ln_g=m_a_ln_g, a_ln_b=m_a_ln_b, a_w_s=m_a_w_s, a_b_s=m_a_b_s, a_w_out=m_a_w_out,
             kv_w=m_kv_w, b_w_q=m_b_w_q, b_w_o=m_b_w_o, rel_table=m_rel_table, ffn_w_up=m_ffn_w_up,
             ffn_conv_w=m_ffn_conv_w, ffn_conv_b=m_ffn_conv_b, ffn_w_down=m_ffn_w_down, ln_g=m_ln_g, ln_b=m_ln_b)
    v = dict(a_w_in=v_a_w_in, a_ln_g=v_a_ln_g, a_ln_b=v_a_ln_b, a_w_s=v_a_w_s, a_b_s=v_a_b_s, a_w_out=v_a_w_out,
             kv_w=v_kv_w, b_w_q=v_b_w_q, b_w_o=v_b_w_o, rel_table=v_rel_table, ffn_w_up=v_ffn_w_up,
             ffn_conv_w=v_ffn_conv_w, ffn_conv_b=v_ffn_conv_b, ffn_w_down=v_ffn_w_down, ln_g=v_ln_g, ln_b=v_ln_b)
    chip = 2 * lax.axis_index("x") + lax.axis_index("y")

    big_names = [n for n, _ in BIG]
    big_axes = [a for _, a in BIG]
    where = jnp.stack([chip, lax.axis_index("c"), 2 * chip + lax.axis_index("c")]).astype(jnp.int32)
    shards = [w[n].reshape((1,) + w[n].shape) if w[n].ndim == 2 else w[n] for n in big_names]
    small_shapes = [w[n].shape for n in SMALL_SHARDED]
    small = _pack([w[n] for n in SMALL_SHARDED])
    shard_of = dict(zip(big_names, shards))
    axis_of = dict(BIG)
    n_layers = {n: s.shape[0] for n, s in shard_of.items()}
    n_a = w["a_w_s"].shape[0]
    stages = {}
    for i in range(w["ffn_w_up"].shape[0]):
        if i < n_a:
            stages[f"{i}a"], stages[f"{i}b"] = [("a_w_in", i)], [("a_w_out", i)]
        else:
            stages[f"{i}a"] = ([("kv_w", 0)] if i == n_a else []) + [("b_w_q", i - n_a)]
            stages[f"{i}b"] = [("b_w_o", i - n_a)]
        stages[f"{i}c"], stages[f"{i}d"] = [("ffn_w_up", i)], [("ffn_w_down", i)]
    stages["0a"].append(("small", 0))
    assert sum(len(st) for st in stages.values()) == sum(n_layers.values()) + 1
    flights, full_shape, all_started = {}, {}, None
    for batch, names in enumerate(([st for st in stages if st[0] == "0"], [st for st in stages if st[0] != "0"])):
        order = [key for st in names for key in stages[st]]
        placed, axes = [], []
        for n, l in order:
            if n == "small":
                placed.append(_place_shard(small[None], 0, ROW, where, F32, name="place_small"))
                axes.append(ROW)
            else:
                placed.append(_place_shard(shard_of[n], l, axis_of[n], where, BF16, name=f"place_{n}{l}",
                                           after=all_started))
                axes.append(axis_of[n])
                full_shape[n] = (n_layers[n],) + placed[-1].shape[1:]
        send_sems, recv_sems, flying, all_started = _gather_start(placed, axes, name=f"gather_start{batch}")
        for st in names:
            flights[st] = (send_sems, recv_sems, flying, axes, order)

    def fetch(stage, after):
        if stage not in stages:
            return {}, {}
        send_sems, recv_sems, flying, axes, order = flights[stage]
        if stage == "0a":
            after = all_started
        idxs = [order.index(key) for key in stages[stage]]
        landed = _gather_wait(send_sems, recv_sems, [flying[i] for i in idxs], [axes[i] for i in idxs], idxs, after,
                              name=f"gather_wait_{stage}")
        W_new, P_new = {}, {}
        for key, arr in zip(stages[stage], landed):
            if key[0] == "small":
                small_all = arr.reshape((N_CHIPS,) + small.shape)
                per_chip = [_unpack(small_all[j], small_shapes) for j in range(N_CHIPS)]
                for i, n in enumerate(SMALL_SHARDED):
                    P_new[n] = jnp.concatenate([per_chip[j][i] for j in range(N_CHIPS)], axis=-1)
            else:
                W_new[key] = arr.reshape(arr.shape[1:])
        return W_new, P_new

    slots ={n: lax.empty((N_DEV, n_layers[n]) + _part_geometry(full_shape[n], axis_of[n]), BF16) for n in big_names}
    sems, plan, tokens = [], [], []

    def ship(gw, items):
        names = [n for n, _ in items]
        send, recv, grads, landing, token = _scatter_start(
            [gw[n] for n in names], [slots[n] for n in names], [l for _, l in items], [axis_of[n] for n in names],
            name="scatter_start_" + "_".join(f"{n}{l}" for n, l in items))
        for n, g, s in zip(names, grads, landing):
            gw[n], slots[n] = g, s
        sems.append((send, recv))
        plan.append([(big_names.index(n), l) for n, l in items])
        tokens.append(token)
        return token

    loss_row, grad_x, gw, gsmall = _local_step(x[0], loss_target[0], {n: w[n] for n in SMALL_REPLICATED}, fetch,
                                               ship, n_layers)

    small_names = list(SMALL_SHARDED) + list(SMALL_REPLICATED)
    small_pack = _pack([gsmall[n] for n in small_names] + [loss_row[0, :1]])
    small_sems = _small_start(small_pack, lax.empty((N_DEV,) + small_pack.shape, F32), tokens[-1],
                              name="small_start")
    own, landed = _scatter_wait(sems, [gw[n] for n in big_names], [slots[n] for n in big_names], big_axes, plan,
                                grad_x, name="scatter_wait")
    sums = [_sum_parts(b, g, ax, where, name=f"sum_{n}") for (n, ax), b, g in zip(BIG, landed, own)]
    pairs = _pair_gather(sums, name="pair_gather")
    grad = {n: p.reshape(w[n].shape) for n, p in zip(big_names, pairs)}

    delta, new_m, new_v = {}, {}, {}
    for n, _ in BIG:
        delta[n], new_m[n], new_v[n] = _adamw(w[n], grad[n], m[n], v[n], name=f"adamw_{n}")
    small_own, small_slots = _small_wait(*small_sems, new_v[big_names[-1]], name="small_wait")
    small_sum = _sum_small(small_slots, small_own, where, name="sum_small")
    full_small = _unpack(small_sum, [gsmall[n].shape for n in small_names] + [(1,)])
    loss = full_small[-1][0]
    for n, gfull in zip(small_names, full_small):
        if n in SMALL_SHARDED:
            width = w[n].shape[-1]
            grad[n] = lax.dynamic_slice_in_dim(gfull, chip * width, width, axis=-1)
        else:
            grad[n] = gfull
    shapes = [w[n].shape for n in small_names]
    packed = _adamw(_pack([w[n] for n in small_names]), _pack([grad[n] for n in small_names]),
                    _pack([m[n] for n in small_names]), _pack([v[n] for n in small_names]), name="adamw_small")
    for out, res in zip((delta, new_m, new_v), packed):
        for n, a in zip(small_names, _unpack(res, shapes)):
            out[n] = a

    return (loss, grad_x[None], *[grad[n] for n in WEIGHTS], *[delta[n] for n in WEIGHTS],
            *[new_m[n] for n in WEIGHTS], *[new_v[n] for n in WEIGHTS])
```
